```python
import math
import jax
import jax.numpy as jnp
from jax import lax
import numpy as np

D_MODEL = 1024
BATCH = 1
SEQ = 16384
DEPTH = 1
DEC_BATCH = 32
DEC_SEQ = 4
PAST_LEN = 16384
PAGE_SIZE = 128

H_NSA = 8
N_KV = 2
G_NSA = H_NSA // N_KV
HD = 64
CMP_STRIDE = 16
CMP_BLOCK = 2 * CMP_STRIDE
CMP_HIDDEN = 256
SEL_BLOCK = 64
N_SEL = 16
WINDOW = 512
Q_BLOCK = 128
N_BUCKETS = 32
MAX_DISTANCE = 128
H_RWKV = 8
HD_RWKV = 64
D_RWKV = H_RWKV * HD_RWKV
DECAY_LORA = 32
AAA_LORA = 32
GATE_LORA = 96
GN_EPS = 64e-5
D_FF = 2816
LN_EPS = 1e-5
ALPHA = (2 * DEPTH) ** 0.25
BETA = (8 * DEPTH) ** -0.25

NSA_SIZES = (H_NSA * HD,) + (N_KV * HD,) * 6 + (H_NSA * 3,)
RWKV_SIZES = (D_RWKV, D_RWKV, D_RWKV, DECAY_LORA, AAA_LORA, GATE_LORA)
NSA_COLS = sum(NSA_SIZES)
RWKV_COLS = sum(RWKV_SIZES)
IN_COLS = NSA_COLS + RWKV_COLS
D_MIX = H_NSA * HD + D_RWKV

kernel_name = 'nsa_rwkv7_macaron_deepnorm_step'


def split_cols(p, sizes):
    return jnp.split(p, [int(s) for s in np.cumsum(sizes)[:-1]], axis=-1)


def layer_norm(x, g, b):
    xf = x.astype(jnp.float32)
    mu = xf.mean(-1, keepdims=True)
    var = jnp.square(xf - mu).mean(-1, keepdims=True)
    return ((xf - mu) * lax.rsqrt(var + LN_EPS) * g + b).astype(x.dtype)


def swiglu(x, w_gate, w_up, w_down):
    return (jax.nn.silu(x @ w_gate) * (x @ w_up)) @ w_down


def rel_bucket(dist):
    d = jnp.maximum(dist, 0)
    max_exact = N_BUCKETS // 2
    df = jnp.maximum(d, 1).astype(jnp.float32)
    large = max_exact + (jnp.log(df / max_exact) / math.log(MAX_DISTANCE / max_exact)
                         * (N_BUCKETS - max_exact)).astype(jnp.int32)
    return jnp.where(d < max_exact, d, jnp.minimum(large, N_BUCKETS - 1))


def bias_shared(rel_table, dist):
    b = rel_table[rel_bucket(dist)]
    return jnp.moveaxis(b, -1, 0).reshape(N_KV, G_NSA, *dist.shape)


def bias_grouped(rel_table, dist):
    tbl = rel_table.reshape(N_BUCKETS, N_KV, G_NSA)
    b = jax.vmap(lambda dk, tk: tk[rel_bucket(dk)], in_axes=(1, 1), out_axes=1)(dist, tbl)
    return jnp.moveaxis(b, -1, 2)


def masked_softmax(s, mask):
    p = jax.nn.softmax(jnp.where(mask, s.astype(jnp.float32), -1e30), axis=-1)
    return p * mask


def compress(kv, pe, w1, b1, w2):
    B, L = kv.shape[:2]
    chunks = kv.reshape(B, L // CMP_STRIDE, CMP_STRIDE, N_KV, HD)
    h_first = jnp.einsum('bcskd,sdh->bckh', chunks + pe[:CMP_STRIDE, None, :], w1[:CMP_STRIDE])
    h_second = jnp.einsum('bcskd,sdh->bckh', chunks + pe[CMP_STRIDE:, None, :], w1[CMP_STRIDE:])
    hidden = jax.nn.gelu(h_first[:, :-1] + h_second[:, 1:] + b1)
    return hidden @ w2


def to_sel_blocks(t):
    B, L = t.shape[:2]
    return t.reshape(B, L // SEL_BLOCK, SEL_BLOCK, N_KV, HD).transpose(0, 3, 1, 2, 4)


def nsa_attend(q, gates, q_pos, kc, vc, ks_blk, vs_blk, k_band, v_band, band_pos, rel_table):
    B, Tq = q.shape[:2]
    qg = q.reshape(B, Tq, N_KV, G_NSA, HD) * (HD ** -0.5)
    nc = kc.shape[1]
    c_end = jnp.arange(nc, dtype=jnp.int32) * CMP_STRIDE + (CMP_BLOCK - 1)
    dist_c = q_pos[:, None] - c_end[None, :]
    s_c = jnp.einsum('btkgd,bnkd->bkgtn', qg, kc) + bias_shared(rel_table, dist_c)
    p_c = masked_softmax(s_c, dist_c >= 0)
    o_c = jnp.einsum('bkgtn,bnkd->btkgd', p_c.astype(vc.dtype), vc)
    ns = ks_blk.shape[2]
    ratio = SEL_BLOCK // CMP_STRIDE
    imp = jnp.pad(p_c.sum(axis=2), ((0, 0), (0, 0), (0, 0), (1, 1)))
    imp_sel = imp[..., :ratio * ns].reshape(B, N_KV, Tq, ns, ratio).sum(-1) + imp[..., ratio::ratio]
    blk = jnp.arange(ns, dtype=jnp.int32)[None, :]
    cur = (q_pos // SEL_BLOCK)[:, None]
    future = blk * SEL_BLOCK > q_pos[:, None]
    forced = (blk == 0) | (blk == cur) | (blk == cur - 1)
    score = jnp.where(future, -jnp.inf, jnp.where(forced, jnp.inf, imp_sel))
    n_pick = min(N_SEL, ns)
    top_val, top_idx = lax.top_k(score, n_pick)
    gather = jax.vmap(jax.vmap(lambda blocks, ix: blocks[ix]))
    ks = gather(ks_blk, top_idx).reshape(B, N_KV, Tq, n_pick * SEL_BLOCK, HD)
    vs = gather(vs_blk, top_idx).reshape(B, N_KV, Tq, n_pick * SEL_BLOCK, HD)
    kpos = (top_idx[..., None] * SEL_BLOCK + jnp.arange(SEL_BLOCK, dtype=jnp.int32)).reshape(B, N_KV, Tq, -1)
    valid = jnp.broadcast_to((top_val > -jnp.inf)[..., None], top_idx.shape + (SEL_BLOCK,)).reshape(kpos.shape)
    dist_s = q_pos[:, None] - kpos
    s_s = jnp.einsum('btkgd,bktnd->bkgtn', qg, ks) + bias_grouped(rel_table, dist_s)
    p_s = masked_softmax(s_s, (valid & (dist_s >= 0))[:, :, None])
    o_s = jnp.einsum('bkgtn,bktnd->btkgd', p_s.astype(vs.dtype), vs)
    dist_w = q_pos[:, None] - band_pos[None, :]
    mask_w = (dist_w >= 0) & (dist_w < WINDOW) & (band_pos[None, :] >= 0)
    s_w = jnp.einsum('btkgd,blkd->bkgtl', qg, k_band) + bias_shared(rel_table, dist_w)
    p_w = masked_softmax(s_w, mask_w)
    o_w = jnp.einsum('bkgtl,blkd->btkgd', p_w.astype(v_band.dtype), v_band)
    o = gates[..., 0:1] * o_c + gates[..., 1:2] * o_s + gates[..., 2:3] * o_w
    return o.reshape(B, Tq, H_NSA * HD)


def rwkv_mix(p, shift_state, wkv_state, lw):
    B, T = p.shape[:2]
    p_prev = jnp.concatenate([shift_state[:, None].astype(p.dtype), p[:, :-1]], axis=1)
    xs = p + (p_prev - p) * lw['rwkv_mu']
    r, k, v, wl, al, gl = split_cols(xs, RWKV_SIZES)
    w = -jax.nn.softplus(-(lw['rwkv_w0'] + jnp.tanh(wl) @ lw['rwkv_w2'])) - 0.5
    a = jax.nn.sigmoid(lw['rwkv_a0'] + al @ lw['rwkv_a2'])
    g = jax.nn.sigmoid(gl) @ lw['rwkv_g2']
    heads = lambda t: t.reshape(B, T, H_RWKV, HD_RWKV).astype(jnp.float32)
    kk = heads(k * lw['rwkv_k_k'])
    kk = kk / jnp.maximum(jnp.sqrt(jnp.sum(kk * kk, axis=-1, keepdims=True)), 1e-12)
    k = k * (1 + (a - 1) * lw['rwkv_k_a'])
    rh, kh, vh, ah = heads(r), heads(k), heads(v), heads(a)
    decay = jnp.exp(-jnp.exp(heads(w)))

    def step(S, inp):
        r_t, w_t, k_t, v_t, kk_t, a_t = inp
        sa = jnp.einsum('bhij,bhj->bhi', S, -kk_t)
        S = S * w_t[:, :, None, :] + sa[..., None] * (kk_t * a_t)[:, :, None, :] + v_t[..., None] * k_t[:, :, None, :]
        return S, jnp.einsum('bhij,bhj->bhi', S, r_t)

    seq_in = tuple(jnp.moveaxis(t, 1, 0) for t in (rh, decay, kh, vh, kk, ah))
    S_final, y = lax.scan(step, wkv_state.astype(jnp.float32), seq_in)
    y = jnp.moveaxis(y, 0, 1)
    mu = y.mean(-1, keepdims=True)
    var = jnp.square(y - mu).mean(-1, keepdims=True)
    yn = ((y - mu) * lax.rsqrt(var + GN_EPS)).reshape(B, T, D_RWKV) * lw['rwkv_gn_w'] + lw['rwkv_gn_b']
    bonus = (jnp.sum(rh * kh * lw['rwkv_r_k'], axis=-1, keepdims=True) * vh).reshape(B, T, D_RWKV)
    out = (yn + bonus).astype(p.dtype) * g
    return out, p[:, -1], S_final


def split_mixer_cols(p):
    B, T = p.shape[:2]
    q, kc, vc, ks, vs, kw, vw, gl = split_cols(p[..., :NSA_COLS], NSA_SIZES)
    kv = lambda t: t.reshape(B, T, N_KV, HD)
    gates = jax.nn.sigmoid(gl.astype(jnp.float32)).reshape(B, T, N_KV, G_NSA, 3).astype(p.dtype)
    return q.reshape(B, T, H_NSA, HD), gates, kv(kc), kv(vc), kv(ks), kv(vs), kv(kw), kv(vw), p[..., NSA_COLS:]


def compress_kv(kc_raw, vc_raw, lw):
    kc = compress(kc_raw, lw['cmp_pe_k'], lw['cmp_w1_k'], lw['cmp_b1_k'], lw['cmp_w2_k'])
    vc = compress(vc_raw, lw['cmp_pe_v'], lw['cmp_w1_v'], lw['cmp_b1_v'], lw['cmp_w2_v'])
    return kc, vc


def prompt_mixer(p, lw, rel_table):
    B, T = p.shape[:2]
    q, gates, kc_raw, vc_raw, ks, vs, kw, vw, pr = split_mixer_cols(p)
    kc, vc = compress_kv(kc_raw, vc_raw, lw)
    ks_blk, vs_blk = to_sel_blocks(ks), to_sel_blocks(vs)
    pad = ((0, 0), (WINDOW, 0), (0, 0), (0, 0))
    kw_pad, vw_pad = jnp.pad(kw, pad), jnp.pad(vw, pad)
    band = WINDOW + Q_BLOCK

    def q_block(i):
        s0 = i * Q_BLOCK
        q_pos = s0 + jnp.arange(Q_BLOCK, dtype=jnp.int32)
        band_pos = s0 - WINDOW + jnp.arange(band, dtype=jnp.int32)
        sl = lambda t, n: lax.dynamic_slice_in_dim(t, s0, n, axis=1)
        return nsa_attend(sl(q, Q_BLOCK), sl(gates, Q_BLOCK), q_pos, kc, vc, ks_blk, vs_blk,
                          sl(kw_pad, band), sl(vw_pad, band), band_pos, rel_table)

    o_nsa = lax.map(q_block, jnp.arange(T // Q_BLOCK, dtype=jnp.int32))
    o_nsa = jnp.moveaxis(o_nsa, 0, 1).reshape(B, T, H_NSA * HD)
    o_rwkv, shift, wkv = rwkv_mix(pr, jnp.zeros((B, RWKV_COLS), p.dtype),
                                  jnp.zeros((B, H_RWKV, HD_RWKV, HD_RWKV), jnp.float32), lw)
    kv_rows = jnp.stack([kc_raw, vc_raw, ks, vs], axis=2)
    win_rows = jnp.stack([kw, vw], axis=2)[:, T - min(WINDOW, T):]
    return jnp.concatenate([o_nsa, o_rwkv], axis=-1), (kv_rows, win_rows, shift, wkv)


def sample_mixer(p, lw, rel_table, kv_pool, page_table, win_buf, shift_state, wkv_state):
    B, T = p.shape[:2]
    q, gates, kc_raw, vc_raw, ks, vs, kw, vw, pr = split_mixer_cols(p)
    kv_rows = jnp.stack([kc_raw, vc_raw, ks, vs], axis=2)
    past = kv_pool[page_table].reshape(B, PAST_LEN, 4, N_KV, HD)
    full = jnp.concatenate([past, kv_rows.astype(past.dtype)], axis=1)
    L = PAST_LEN + T
    Lp = -(-L // SEL_BLOCK) * SEL_BLOCK
    full = jnp.pad(full, ((0, 0), (0, Lp - L), (0, 0), (0, 0), (0, 0)))
    kc, vc = compress_kv(full[:, :, 0], full[:, :, 1], lw)
    ks_blk, vs_blk = to_sel_blocks(full[:, :, 2]), to_sel_blocks(full[:, :, 3])
    band = jnp.concatenate([win_buf, jnp.stack([kw, vw], axis=2).astype(win_buf.dtype)], axis=1)
    w_buf = win_buf.shape[1]
    band_pos = PAST_LEN - w_buf + jnp.arange(w_buf + T, dtype=jnp.int32)
    q_pos = PAST_LEN + jnp.arange(T, dtype=jnp.int32)
    o_nsa = nsa_attend(q, gates, q_pos, kc, vc, ks_blk, vs_blk, band[:, :, 0], band[:, :, 1], band_pos, rel_table)
    o_rwkv, shift, wkv = rwkv_mix(pr, shift_state, wkv_state, lw)
    return jnp.concatenate([o_nsa, o_rwkv], axis=-1), (kv_rows, band[:, T:], shift, wkv)


def layer_forward(x, c, lw, mixer):
    mod = (jax.nn.silu(c) @ lw['w_ada'] + lw['b_ada']).reshape(c.shape[0], 9, 1, D_MODEL)

    def modulate(x, i):
        return x * (1 + mod[:, 3 * i + 1]) + mod[:, 3 * i]

    def post_norm(x, out, i):
        return layer_norm(ALPHA * x + (1 + mod[:, 3 * i + 2]) * out, lw['ln_g'][i], lw['ln_b'][i])

    x = post_norm(x, 0.5 * swiglu(modulate(x, 0), lw['ffn1_gate'], lw['ffn1_up'], lw['ffn1_down']), 0)
    mix, state = mixer(modulate(x, 1) @ lw['w_in'])
    x = post_norm(x, mix @ lw['w_out'], 1)
    x = post_norm(x, 0.5 * swiglu(modulate(x, 2), lw['ffn2_gate'], lw['ffn2_up'], lw['ffn2_down']), 2)
    return x, state


def setup_inputs(seed: int = 0) -> dict:
    key = jax.random.key(seed)
    k_pages, k_mu, key = jax.random.split(key, 3)
    keys = iter(jax.random.split(key, 64))
    nrm = lambda shape, s=1.0: s * jax.random.normal(next(keys), shape, jnp.float32)
    n_pages = PAST_LEN // PAGE_SIZE
    n_used = DEC_BATCH * n_pages
    n_pool = (5 * n_used + 3) // 4
    page_table = jax.random.permutation(k_pages, n_pool)[:n_used].reshape(DEC_BATCH, n_pages).astype(jnp.int32)
    w_buf = min(WINDOW, PAST_LEN)
    return {
        'x_prompt': nrm((BATCH, SEQ, D_MODEL)),
        'x_sample': nrm((DEC_BATCH, DEC_SEQ, D_MODEL)),
        'cache_nsa_kv': nrm((DEPTH, n_pool, PAGE_SIZE, 4, N_KV, HD)),
        'cache_nsa_win': nrm((DEPTH, DEC_BATCH, w_buf, 2, N_KV, HD)),
        'state_rwkv_shift': nrm((DEPTH, DEC_BATCH, RWKV_COLS)),
        'state_rwkv_wkv': nrm((DEPTH, DEC_BATCH, H_RWKV, HD_RWKV, HD_RWKV), 0.3),
        'page_table': page_table,
        'c_prompt': nrm((BATCH, D_MODEL)),
        'c_sample': nrm((DEC_BATCH, D_MODEL)),
        'rel_table': nrm((N_BUCKETS, H_NSA), 0.5),
        'w_ada': nrm((DEPTH, D_MODEL, 9 * D_MODEL), 0.2 * D_MODEL ** -0.5),
        'b_ada': nrm((DEPTH, 9 * D_MODEL), 0.02),
        'ln_g': 1.0 + nrm((DEPTH, 3, D_MODEL), 0.02),
        'ln_b': nrm((DEPTH, 3, D_MODEL), 0.02),
        'ffn1_gate': nrm((DEPTH, D_MODEL, D_FF), D_MODEL ** -0.5),
        'ffn1_up': nrm((DEPTH, D_MODEL, D_FF), D_MODEL ** -0.5),
        'ffn1_down': nrm((DEPTH, D_FF, D_MODEL), BETA * D_FF ** -0.5),
        'ffn2_gate': nrm((DEPTH, D_MODEL, D_FF), D_MODEL ** -0.5),
        'ffn2_up': nrm((DEPTH, D_MODEL, D_FF), D_MODEL ** -0.5),
        'ffn2_down': nrm((DEPTH, D_FF, D_MODEL), BETA * D_FF ** -0.5),
        'w_in': nrm((DEPTH, D_MODEL, IN_COLS), D_MODEL ** -0.5),
        'w_out': nrm((DEPTH, D_MIX, D_MODEL), BETA * D_MIX ** -0.5),
        'cmp_pe_k': nrm((DEPTH, CMP_BLOCK, HD), 0.1),
        'cmp_w1_k': nrm((DEPTH, CMP_BLOCK, HD, CMP_HIDDEN), (CMP_BLOCK * HD) ** -0.5),
        'cmp_b1_k': nrm((DEPTH, CMP_HIDDEN), 0.02),
        'cmp_w2_k': nrm((DEPTH, CMP_HIDDEN, HD), CMP_HIDDEN ** -0.5),
        'cmp_pe_v': nrm((DEPTH, CMP_BLOCK, HD), 0.1),
        'cmp_w1_v': nrm((DEPTH, CMP_BLOCK, HD, CMP_HIDDEN), (CMP_BLOCK * HD) ** -0.5),
        'cmp_b1_v': nrm((DEPTH, CMP_HIDDEN), 0.02),
        'cmp_w2_v': nrm((DEPTH, CMP_HIDDEN, HD), CMP_HIDDEN ** -0.5),
        'rwkv_mu': jax.random.uniform(k_mu, (DEPTH, RWKV_COLS), jnp.float32),
        'rwkv_w0': jnp.linspace(-6.0, 1.0, D_RWKV, dtype=jnp.float32)[None] + nrm((DEPTH, D_RWKV), 0.1),
        'rwkv_w2': nrm((DEPTH, DECAY_LORA, D_RWKV), 0.1 * DECAY_LORA ** -0.5),
        'rwkv_a0': nrm((DEPTH, D_RWKV), 0.1),
        'rwkv_a2': nrm((DEPTH, AAA_LORA, D_RWKV), 0.1 * AAA_LORA ** -0.5),
        'rwkv_g2': nrm((DEPTH, GATE_LORA, D_RWKV), GATE_LORA ** -0.5),
        'rwkv_k_k': 0.85 + nrm((DEPTH, D_RWKV), 0.02),
        'rwkv_k_a': 1.0 + nrm((DEPTH, D_RWKV), 0.02),
        'rwkv_r_k': nrm((DEPTH, H_RWKV, HD_RWKV), 0.1),
        'rwkv_gn_w': 1.0 + nrm((DEPTH, D_RWKV), 0.02),
        'rwkv_gn_b': nrm((DEPTH, D_RWKV), 0.02),
    }


def reference(x_prompt, x_sample, cache_nsa_kv, cache_nsa_win, state_rwkv_shift, state_rwkv_wkv,
              page_table, c_prompt, c_sample, rel_table, w_ada, b_ada, ln_g, ln_b,
              ffn1_gate, ffn1_up, ffn1_down, ffn2_gate, ffn2_up, ffn2_down, w_in, w_out,
              cmp_pe_k, cmp_w1_k, cmp_b1_k, cmp_w2_k, cmp_pe_v, cmp_w1_v, cmp_b1_v, cmp_w2_v,
              rwkv_mu, rwkv_w0, rwkv_w2, rwkv_a0, rwkv_a2, rwkv_g2, rwkv_k_k, rwkv_k_a, rwkv_r_k,
              rwkv_gn_w, rwkv_gn_b):
    xp, xs = x_prompt, x_sample
    st_prompt, st_sample = [], []
    for l in range(DEPTH):
        lw = {
            'w_ada': w_ada[l], 'b_ada': b_ada[l], 'ln_g': ln_g[l], 'ln_b': ln_b[l],
            'ffn1_gate': ffn1_gate[l], 'ffn1_up': ffn1_up[l], 'ffn1_down': ffn1_down[l],
            'ffn2_gate': ffn2_gate[l], 'ffn2_up': ffn2_up[l], 'ffn2_down': ffn2_down[l],
            'w_in': w_in[l], 'w_out': w_out[l],
            'cmp_pe_k': cmp_pe_k[l], 'cmp_w1_k': cmp_w1_k[l], 'cmp_b1_k': cmp_b1_k[l], 'cmp_w2_k': cmp_w2_k[l],
            'cmp_pe_v': cmp_pe_v[l], 'cmp_w1_v': cmp_w1_v[l], 'cmp_b1_v': cmp_b1_v[l], 'cmp_w2_v': cmp_w2_v[l],
            'rwkv_mu': rwkv_mu[l], 'rwkv_w0': rwkv_w0[l], 'rwkv_w2': rwkv_w2[l], 'rwkv_a0': rwkv_a0[l],
            'rwkv_a2': rwkv_a2[l], 'rwkv_g2': rwkv_g2[l], 'rwkv_k_k': rwkv_k_k[l], 'rwkv_k_a': rwkv_k_a[l],
            'rwkv_r_k': rwkv_r_k[l], 'rwkv_gn_w': rwkv_gn_w[l], 'rwkv_gn_b': rwkv_gn_b[l],
        }
        xp, sp = layer_forward(xp, c_prompt, lw, lambda p: prompt_mixer(p, lw, rel_table))
        xs, ss = layer_forward(xs, c_sample, lw, lambda p: sample_mixer(
            p, lw, rel_table, cache_nsa_kv[l], page_table, cache_nsa_win[l], state_rwkv_shift[l], state_rwkv_wkv[l]))
        st_prompt.append(sp)
        st_sample.append(ss)
    stack = lambda states, i: jnp.stack([s[i] for s in states])
    return (xp, xs,
            stack(st_prompt, 0), stack(st_prompt, 1), stack(st_prompt, 2), stack(st_prompt, 3),
            stack(st_sample, 0), stack(st_sample, 1), stack(st_sample, 2), stack(st_sample, 3))
```

```python
import functools
import math

import numpy as np
import jax
import jax.numpy as jnp
from jax import lax
from jax.experimental import pallas as pl
from jax.experimental.pallas import tpu as pltpu

D_MODEL = 1024
PAGE_SIZE = 128
H_NSA = 8
N_KV = 2
G_NSA = H_NSA // N_KV
HD = 64
CMP_STRIDE = 16
CMP_BLOCK = 2 * CMP_STRIDE
CMP_HIDDEN = 256
SEL_BLOCK = 64
N_SEL = 16
WINDOW = 512
Q_BLOCK = 128
N_BUCKETS = 32
MAX_DISTANCE = 128
H_RWKV = 8
HD_RWKV = 64
D_RWKV = H_RWKV * HD_RWKV
DECAY_LORA = 32
AAA_LORA = 32
GATE_LORA = 96
GN_EPS = 64e-5
D_FF = 2816
LN_EPS = 1e-5
DEPTH = 1
ALPHA = (2 * DEPTH) ** 0.25

NSA_SIZES = (H_NSA * HD,) + (N_KV * HD,) * 6 + (H_NSA * 3,)
RWKV_SIZES = (D_RWKV, D_RWKV, D_RWKV, DECAY_LORA, AAA_LORA, GATE_LORA)
NSA_COLS = sum(NSA_SIZES)
RWKV_COLS = sum(RWKV_SIZES)

F32 = jnp.float32
BF16 = jnp.bfloat16
LANE = 128
NEG = -(2.0 ** 100)
M_INIT = -(2.0 ** 103)
VMEM_LIMIT = 56 * 1024 * 1024

RW_PAD = 3 * D_RWKV + 3 * LANE
P_Q, P_KV, P_WIN, P_GATE, P_RW = 0, 512, 1024, 1280, 1408
P_COLS = P_RW + RW_PAD
KEY_TILE = 512
MASK_ROWS = 16
V_ROWS = 144


def _bucket_lows():
    d = np.arange(0, 4 * MAX_DISTANCE, dtype=np.int64)
    max_exact = N_BUCKETS // 2
    df = np.maximum(d, 1).astype(np.float32)
    large = max_exact + (np.log(df / np.float32(max_exact)) / np.float32(math.log(MAX_DISTANCE / max_exact))
                         * np.float32(N_BUCKETS - max_exact)).astype(np.int32)
    b = np.where(d < max_exact, d, np.minimum(large, N_BUCKETS - 1))
    lows = [int(np.argmax(b >= k)) for k in range(N_BUCKETS)]
    return b, lows


_BUCKET_OF, _BUCKET_LOW = _bucket_lows()
FAR_DIST = _BUCKET_LOW[N_BUCKETS - 1]


def _resident(shape):
    nd = len(shape)
    return pl.BlockSpec(shape, lambda *_: (0,) * nd, pipeline_mode=pl.Buffered(1))


def _params(sem):
    return pltpu.CompilerParams(dimension_semantics=sem, vmem_limit_bytes=VMEM_LIMIT)


def _split2(x):
    hi = x.astype(BF16)
    lo = (x - hi.astype(F32)).astype(BF16)
    return hi, lo


def _dot_exact_rhs(x, rhs_bf16, terms=2):
    acc = None
    rem = x
    for _ in range(terms):
        part = rem.astype(BF16)
        d = jnp.dot(part, rhs_bf16, preferred_element_type=F32)
        acc = d if acc is None else acc + d
        rem = rem - part.astype(F32)
    return acc


def _dot_exact_lhs(lhs_bf16, x, terms=3):
    acc = None
    rem = x
    for _ in range(terms):
        part = rem.astype(BF16)
        d = jnp.dot(lhs_bf16, part, preferred_element_type=F32)
        acc = d if acc is None else acc + d
        rem = rem - part.astype(F32)
    return acc


def _layer_norm(y, g, b):
    mu = jnp.mean(y, axis=-1, keepdims=True)
    yc = y - mu
    var = jnp.mean(yc * yc, axis=-1, keepdims=True)
    return yc * lax.rsqrt(var + LN_EPS) * g + b


def _bias_chain(d, tab_rows):
    out = tab_rows[0] + jnp.zeros(d.shape, F32)
    for b in range(1, N_BUCKETS):
        out = jnp.where(d >= _BUCKET_LOW[b], tab_rows[b], out)
    return out


def _ada_kernel(c_ref, w_ref, b_ref, o_ref):
    c = c_ref[...]
    h = (c * jax.nn.sigmoid(c)).astype(BF16)
    o_ref[...] = jnp.dot(h, w_ref[...].astype(BF16), preferred_element_type=F32) + b_ref[...]


def _ada(c_all, w_ada, b_ada):
    rows, n = c_all.shape[0], w_ada.shape[1]
    tn = 1152
    return pl.pallas_call(
        _ada_kernel,
        grid=(n // tn,),
        in_specs=[pl.BlockSpec((rows, D_MODEL), lambda j: (0, 0)),
                  pl.BlockSpec((D_MODEL, tn), lambda j: (0, j)),
                  pl.BlockSpec((1, tn), lambda j: (0, j))],
        out_specs=pl.BlockSpec((rows, tn), lambda j: (0, j)),
        out_shape=jax.ShapeDtypeStruct((rows, n), F32),
        compiler_params=_params(("arbitrary",)),
        name="ada",
    )(c_all, w_ada, b_ada.reshape(1, n))


FF_CHUNKS = 2


def _ffn_kernel(x_ref, sh_ref, sc_ref, gt_ref, lng_ref, lnb_ref, wg_ref, wu_ref, wd_ref, o_ref):
    x = x_ref[...]
    h = (x * (1.0 + sc_ref[...]) + sh_ref[...]).astype(BF16)
    ck = D_FF // FF_CHUNKS
    acc = jnp.zeros(x.shape, F32)
    for c in range(FF_CHUNKS):
        a = jnp.dot(h, wg_ref[:, c * ck:(c + 1) * ck], preferred_element_type=F32)
        b = jnp.dot(h, wu_ref[:, c * ck:(c + 1) * ck], preferred_element_type=F32)
        t = (a * jax.nn.sigmoid(a) * b).astype(BF16)
        acc = acc + jnp.dot(t, wd_ref[c * ck:(c + 1) * ck, :], preferred_element_type=F32)
    y = ALPHA * x + (1.0 + gt_ref[...]) * (0.5 * acc)
    o_ref[...] = _layer_norm(y, lng_ref[...], lnb_ref[...])


def _mod_spec(mod, tm):
    if mod.shape[0] == 1:
        return pl.BlockSpec((1, D_MODEL), lambda i: (0, 0))
    return pl.BlockSpec((tm, D_MODEL), lambda i: (i, 0))


def _ffn(x, shift, scale, gate, ln_g, ln_b, wg, wu, wd):
    rows = x.shape[0]
    tm = min(512, rows)
    row = lambda i: (i, 0)
    return pl.pallas_call(
        _ffn_kernel,
        grid=(rows // tm,),
        in_specs=[pl.BlockSpec((tm, D_MODEL), row), _mod_spec(shift, tm), _mod_spec(scale, tm), _mod_spec(gate, tm),
                  _resident((1, D_MODEL)), _resident((1, D_MODEL)),
                  _resident((D_MODEL, D_FF)), _resident((D_MODEL, D_FF)), _resident((D_FF, D_MODEL))],
        out_specs=pl.BlockSpec((tm, D_MODEL), row),
        out_shape=jax.ShapeDtypeStruct((rows, D_MODEL), F32),
        compiler_params=_params(("arbitrary",)),
        name="ffn",
    )(x, shift, scale, gate, ln_g.reshape(1, -1), ln_b.reshape(1, -1), wg, wu, wd)


def _proj_kernel(x_ref, sh_ref, sc_ref, w_ref, q_ref, kv_ref, win_ref, g_ref, pr_ref):
    h = (x_ref[...] * (1.0 + sc_ref[...]) + sh_ref[...]).astype(BF16)
    p = jnp.dot(h, w_ref[...], preferred_element_type=F32)
    q_ref[...] = p[:, P_Q:P_KV]
    kv_ref[...] = p[:, P_KV:P_WIN]
    win_ref[...] = p[:, P_WIN:P_GATE]
    g_ref[...] = jax.nn.sigmoid(p[:, P_GATE:P_RW])
    pr_ref[...] = p[:, P_RW:P_COLS]


def _proj(x, shift, scale, w_in_p):
    rows = x.shape[0]
    tm = min(512, rows)
    row = lambda i: (i, 0)
    widths = (512, 512, 256, LANE, RW_PAD)
    return pl.pallas_call(
        _proj_kernel,
        grid=(rows // tm,),
        in_specs=[pl.BlockSpec((tm, D_MODEL), row), _mod_spec(shift, tm), _mod_spec(scale, tm),
                  _resident((D_MODEL, P_COLS))],
        out_specs=[pl.BlockSpec((tm, w), row) for w in widths],
        out_shape=[jax.ShapeDtypeStruct((rows, w), F32) for w in widths],
        compiler_params=_params(("arbitrary",)),
        name="proj",
    )(x, shift, scale, w_in_p)


def _prep_w_in(w_in):
    pad = lambda a, n: jnp.pad(a, ((0, 0), (0, n - a.shape[1])))
    nsa, rw = w_in[:, :NSA_COLS], w_in[:, NSA_COLS:]
    gl = nsa[:, 1280:1304].reshape(D_MODEL, H_NSA, 3).transpose(0, 2, 1).reshape(D_MODEL, 3 * H_NSA)
    cols = [nsa[:, :1280], pad(gl, LANE), _rwkv_cols(rw)]
    return jnp.concatenate(cols, axis=1).astype(BF16)


def _rwkv_cols(a):
    pad = lambda t: jnp.pad(t, [(0, 0)] * (t.ndim - 1) + [(0, LANE - t.shape[-1])])
    n = 3 * D_RWKV
    return jnp.concatenate([a[..., :n], pad(a[..., n:n + 32]), pad(a[..., n + 32:n + 64]), pad(a[..., n + 64:n + 160])], axis=-1)


def _rwkv_uncols(a):
    n = 3 * D_RWKV
    return jnp.concatenate([a[..., :n], a[..., n:n + 32], a[..., n + LANE:n + LANE + 32], a[..., n + 2 * LANE:n + 2 * LANE + 96]], axis=-1)


RW_SUB = 8
RW_PAIRS = H_RWKV // 2
RW_BLOCK = 64


def _lora(x, w_ref):
    w = w_ref[...]
    w_hi = w.astype(BF16)
    w_lo = (w - w_hi.astype(F32)).astype(BF16)
    return _dot_exact_rhs(x, w_hi) + jnp.dot(x.astype(BF16), w_lo, preferred_element_type=F32)


def _rwkv_kernel(pr_ref, sh0_ref, s0_ref, mu_ref, w0_ref, a0_ref, kk_ref, ka_ref, rk_ref, gw_ref, gb_ref,
                 w2_ref, a2_ref, g2_ref, bo_ref, bo2_ref, o_ref, sout_ref,
                 prev_scr, s_scr, nkk_scr, dec_scr, bet_scr, k2_scr, r_scr, v_scr, yacc_scr, *, n_valid):
    tb = pr_ref.shape[0]
    step = pl.program_id(1)

    @pl.when(step == 0)
    def _():
        prev_scr[...] = sh0_ref[...]
        s_scr[...] = s0_ref[...]

    yacc_scr[...] = jnp.zeros(yacc_scr.shape, F32)
    p = pr_ref[...]
    rows = lax.broadcasted_iota(jnp.int32, (tb, 1), 0)
    prev = jnp.where(rows == 0, prev_scr[...], pltpu.roll(p, 1, axis=0))
    prev_scr[...] = p[tb - 1:tb, :]
    xs = p + (prev - p) * mu_ref[...]
    n = D_RWKV
    r, k, v = xs[:, :n], xs[:, n:2 * n], xs[:, 2 * n:3 * n]
    wl, al, gl = xs[:, 3 * n:3 * n + LANE], xs[:, 3 * n + LANE:3 * n + 2 * LANE], xs[:, 3 * n + 2 * LANE:]
    z = -(w0_ref[...] + _lora(jnp.tanh(wl), w2_ref))
    w = -(jnp.maximum(z, 0.0) + jnp.log(1.0 + jnp.exp(-jnp.abs(z)))) - 0.5
    a = jax.nn.sigmoid(a0_ref[...] + _lora(al, a2_ref))
    g = _lora(jax.nn.sigmoid(gl), g2_ref)
    kk = k * kk_ref[...]
    ss = _dot_exact_rhs(kk * kk, bo_ref[...])
    kk = kk / jnp.maximum(jnp.sqrt(ss), 1e-12)
    k2 = k * (1.0 + (a - 1.0) * ka_ref[...])
    nkk_scr[...] = -kk
    dec_scr[...] = jnp.exp(-jnp.exp(w))
    bet_scr[...] = kk * a
    k2_scr[...] = k2
    r_scr[...] = r
    v_scr[...] = v

    bo2 = bo2_ref[...]
    lane = lax.broadcasted_iota(jnp.int32, (HD_RWKV, LANE), 1)
    sub = lax.broadcasted_iota(jnp.int32, (HD_RWKV, LANE), 0)
    lane_in_head = lane - jnp.where(lane >= HD_RWKV, HD_RWKV, 0)
    diag = lane_in_head == sub

    low_half = lane < HD_RWKV

    def head_sums(x):
        first = jnp.sum(jnp.where(low_half, x, 0.0), axis=1, keepdims=True)
        both = jnp.sum(x, axis=1, keepdims=True)
        return jnp.where(low_half, first, both - first)

    def head_sums_mxu(x):
        hi = x.astype(BF16)
        lo = (x - hi.astype(F32)).astype(BF16)
        return jnp.dot(jnp.concatenate([hi, lo], axis=1), bo2, preferred_element_type=F32)

    def group(sb, n_tok):
        base = pl.multiple_of(sb * RW_SUB, RW_SUB)
        state = [s_scr[pp] for pp in range(RW_PAIRS)]
        nkk8, dec8, bet8, k28, r8, v8 = (ref[pl.ds(base, RW_SUB), :] for ref in (nkk_scr, dec_scr, bet_scr, k2_scr, r_scr, v_scr))
        sel = jnp.concatenate([jnp.where(diag, v8[tt:tt + 1, pp * LANE:(pp + 1) * LANE], 0.0)
                               for tt in range(n_tok) for pp in range(RW_PAIRS)], axis=0)
        vcols = head_sums_mxu(sel)
        readout = []
        for tt in range(n_tok):
            for pp in range(RW_PAIRS):
                cols = slice(pp * LANE, (pp + 1) * LANE)
                row = lambda a: a[tt:tt + 1, cols]
                at = (tt * RW_PAIRS + pp) * HD_RWKV
                st = state[pp]
                sa = head_sums(st * row(nkk8))
                st = st * row(dec8) + sa * row(bet8) + vcols[at:at + HD_RWKV] * row(k28)
                state[pp] = st
                readout.append(st * row(r8))
        for pp in range(RW_PAIRS):
            s_scr[pp] = state[pp]
        ycols = head_sums_mxu(jnp.concatenate(readout, axis=0))
        for tt in range(n_tok):
            collect = lane_in_head == base + tt
            for pp in range(RW_PAIRS):
                at = (tt * RW_PAIRS + pp) * HD_RWKV
                yacc_scr[pp, :HD_RWKV, :] = jnp.where(collect, ycols[at:at + HD_RWKV], yacc_scr[pp, :HD_RWKV, :])

    if n_valid == tb:
        def body(sb, carry):
            group(sb, RW_SUB)
            return carry
        lax.fori_loop(0, tb // RW_SUB, body, 0)
    else:
        group(0, n_valid)

    pieces = []
    lane_t = lax.broadcasted_iota(jnp.int32, (tb, LANE), 1)
    for pp in range(RW_PAIRS):
        yt = yacc_scr[pp].T
        pieces.append(jnp.where(lane_t < HD_RWKV, yt[:tb], pltpu.roll(yt[RW_BLOCK:RW_BLOCK + tb], HD_RWKV, axis=1)))
    y = jnp.concatenate(pieces, axis=1)
    mean = _dot_exact_rhs(y, bo_ref[...]) * (1.0 / HD_RWKV)
    yc = y - mean
    var = _dot_exact_rhs(yc * yc, bo_ref[...]) * (1.0 / HD_RWKV)
    yn = yc * lax.rsqrt(var + GN_EPS) * gw_ref[...] + gb_ref[...]
    bonus = _dot_exact_rhs(r * k2 * rk_ref[...], bo_ref[...]) * v
    o_ref[...] = (yn + bonus) * g
    sout_ref[...] = s_scr[...]


def _pair_state(s):
    B = s.shape[0]
    return s.reshape(B, RW_PAIRS, 2, HD_RWKV, HD_RWKV).transpose(0, 1, 3, 2, 4).reshape(B, RW_PAIRS, HD_RWKV, LANE)


def _unpair_state(s):
    B = s.shape[0]
    return s.reshape(B, RW_PAIRS, HD_RWKV, 2, HD_RWKV).transpose(0, 1, 3, 2, 4).reshape(B, H_RWKV, HD_RWKV, HD_RWKV)


def _rwkv(pr, shift0, s0, lw, n_valid):
    B, T, _ = pr.shape
    tb = min(RW_BLOCK, T)
    n = D_RWKV
    vec = lambda a: a.reshape(1, n)
    padrow = lambda a: jnp.pad(a, ((0, LANE - a.shape[0]), (0, 0)))
    blk = np.arange(n) // HD_RWKV
    block_ones = jnp.asarray(blk[:, None] == blk[None, :], BF16)
    consts = [_rwkv_cols(lw['rwkv_mu']).reshape(1, RW_PAD), vec(lw['rwkv_w0']), vec(lw['rwkv_a0']), vec(lw['rwkv_k_k']),
              vec(lw['rwkv_k_a']), vec(lw['rwkv_r_k']), vec(lw['rwkv_gn_w']), vec(lw['rwkv_gn_b']),
              padrow(lw['rwkv_w2']), padrow(lw['rwkv_a2']), padrow(lw['rwkv_g2']), block_ones, jnp.concatenate([block_ones[:LANE, :LANE]] * 2, axis=0)]
    kern = functools.partial(_rwkv_kernel, n_valid=n_valid)
    state_spec = pl.BlockSpec((None, RW_PAIRS, HD_RWKV, LANE), lambda b, j: (b, 0, 0, 0))
    o, s = pl.pallas_call(
        kern,
        grid=(B, T // tb),
        in_specs=[pl.BlockSpec((None, tb, RW_PAD), lambda b, j: (b, j, 0)),
                  pl.BlockSpec((None, 1, RW_PAD), lambda b, j: (b, 0, 0)), state_spec]
                 + [_resident(c.shape) for c in consts],
        out_specs=[pl.BlockSpec((None, tb, n), lambda b, j: (b, j, 0)), state_spec],
        out_shape=[jax.ShapeDtypeStruct((B, T, n), F32), jax.ShapeDtypeStruct((B, RW_PAIRS, HD_RWKV, LANE), F32)],
        scratch_shapes=[pltpu.VMEM((1, RW_PAD), F32), pltpu.VMEM((RW_PAIRS, HD_RWKV, LANE), F32)]
                       + [pltpu.VMEM((tb, n), F32)] * 6 + [pltpu.VMEM((RW_PAIRS, LANE, LANE), F32)],
        compiler_params=_params(("arbitrary", "arbitrary")),
        name="rwkv",
    )(pr, shift0, _pair_state(s0), *consts)
    return o, _unpair_state(s)


CMP_PAGES = 32
CHUNKS_PER_PAGE = PAGE_SIZE // CMP_STRIDE
CMP_K = CMP_STRIDE * N_KV * HD


def _compress_kernel(pt_ref, *refs, n_pages):
    weights = refs[2 * n_pages + 2:2 * n_pages + 10]
    outs = refs[2 * n_pages + 10:2 * n_pages + 12]
    rows = CHUNKS_PER_PAGE * n_pages
    for kind in range(2):
        pages, nxt = refs[kind * n_pages:(kind + 1) * n_pages], refs[2 * n_pages + kind]
        pe_ref, w_ref, b_ref, w2_ref = weights[4 * kind:4 * kind + 4]
        x = jnp.concatenate(
            [jnp.concatenate([pg[pl.ds(s, CHUNKS_PER_PAGE, stride=CMP_STRIDE), :] for pg in pages], axis=0)
             for s in range(CMP_STRIDE)], axis=1)
        xn = jnp.concatenate([nxt[s:s + 1, :] for s in range(CMP_STRIDE)], axis=1)
        x_ext = jnp.concatenate([x, jnp.broadcast_to(xn, (8, CMP_K))], axis=0)
        h_first = jnp.dot((x + pe_ref[0]).astype(BF16), w_ref[0], preferred_element_type=F32)
        h_second = jnp.dot((x_ext + pe_ref[1]).astype(BF16), w_ref[1], preferred_element_type=F32)
        h_next = pltpu.roll(h_second, rows + 8 - 1, axis=0)[:rows]
        hidden = jax.nn.gelu(h_first + h_next + b_ref[...])
        outs[kind][...] = jnp.dot(hidden.astype(BF16), w2_ref[...], preferred_element_type=F32)


def _compress_weights(pe, w1, b1, w2):
    eye = jnp.eye(N_KV, dtype=F32)
    halves = []
    for half in range(2):
        w = w1[half * CMP_STRIDE:(half + 1) * CMP_STRIDE]
        halves.append(jnp.einsum('sdn,hg->shdgn', w, eye).reshape(CMP_K, N_KV * CMP_HIDDEN))
    pe2 = jnp.stack([jnp.broadcast_to(pe[half * CMP_STRIDE:(half + 1) * CMP_STRIDE, None, :], (CMP_STRIDE, N_KV, HD)).reshape(1, CMP_K)
                     for half in range(2)])
    w2bd = jnp.einsum('nd,hg->hngd', w2, eye).reshape(N_KV * CMP_HIDDEN, N_KV * HD)
    return [pe2, jnp.stack(halves).astype(BF16), jnp.tile(b1, N_KV).reshape(1, -1), w2bd.astype(BF16)]


def _compress(pool, page_table, lw):
    B, n_pages_total = page_table.shape
    n_pages = min(CMP_PAGES, n_pages_total)
    rows = CHUNKS_PER_PAGE * n_pages
    weights = (_compress_weights(lw['cmp_pe_k'], lw['cmp_w1_k'], lw['cmp_b1_k'], lw['cmp_w2_k'])
               + _compress_weights(lw['cmp_pe_v'], lw['cmp_w1_v'], lw['cmp_b1_v'], lw['cmp_w2_v']))
    width = N_KV * HD

    def page_map(p, kind):
        return lambda b, j, pt: (pt[b, j * n_pages + p], 0, kind)

    def next_map(kind):
        return lambda b, j, pt: (pt[b, jnp.minimum((j + 1) * n_pages, n_pages_total - 1)], 0, kind)

    const = lambda a: pl.BlockSpec(a.shape, lambda b, j, pt: (0,) * a.ndim)
    out_spec = pl.BlockSpec((None, rows, N_KV * HD), lambda b, j, pt: (b, j, 0))
    out_shape = jax.ShapeDtypeStruct((B, n_pages_total * CHUNKS_PER_PAGE, N_KV * HD), F32)
    grid_spec = pltpu.PrefetchScalarGridSpec(
        num_scalar_prefetch=1,
        grid=(B, n_pages_total // n_pages),
        in_specs=[pl.BlockSpec((None, PAGE_SIZE, width), page_map(p, kind)) for kind in range(2) for p in range(n_pages)]
                 + [pl.BlockSpec((None, CMP_STRIDE, width), next_map(kind)) for kind in range(2)] + [const(a) for a in weights],
        out_specs=[out_spec, out_spec],
    )
    return pl.pallas_call(
        functools.partial(_compress_kernel, n_pages=n_pages),
        grid_spec=grid_spec,
        out_shape=[out_shape, out_shape],
        compiler_params=_params(("arbitrary", "arbitrary")),
        name="compress",
    )(page_table, *([pool] * (2 * n_pages + 2)), *weights)


BAND_ROWS = 1152


def _band_kernel(tab_ref, bkt_ref, o_ref):
    h = pl.program_id(0)
    bkt = bkt_ref[...]
    out = jnp.full(bkt.shape, NEG, F32)
    for b in range(N_BUCKETS):
        out = jnp.where(bkt == b, tab_ref[b, h], out)
    o_ref[...] = out


def _band(rel_table):
    u = np.arange(BAND_ROWS)[:, None]
    qi = np.arange(Q_BLOCK)[None, :]
    d = qi + WINDOW - u
    bkt = np.where(d >= 0, _BUCKET_OF[np.clip(d, 0, len(_BUCKET_OF) - 1)], -1).astype(np.int32)
    return pl.pallas_call(
        _band_kernel,
        grid=(H_NSA,),
        in_specs=[pl.BlockSpec(memory_space=pltpu.SMEM), pl.BlockSpec((BAND_ROWS, Q_BLOCK), lambda h: (0, 0))],
        out_specs=pl.BlockSpec((None, BAND_ROWS, Q_BLOCK), lambda h: (h, 0, 0)),
        out_shape=jax.ShapeDtypeStruct((H_NSA, BAND_ROWS, Q_BLOCK), F32),
        compiler_params=_params(("arbitrary",)),
        name="band",
    )(rel_table, jnp.asarray(bkt))


def _softmax_cols(s):
    m = jnp.max(s, axis=0, keepdims=True)
    e = jnp.exp(s - m)
    l = jnp.sum(e, axis=0, keepdims=True)
    return e * jnp.where(m > 0.5 * NEG, 1.0 / l, 0.0)


def _select_blocks(impsel, qpos, n_pick):
    ns = impsel.shape[0]
    blk = lax.broadcasted_iota(jnp.int32, impsel.shape, 0)
    cur = jnp.right_shift(qpos, 6)
    future = blk * SEL_BLOCK > qpos
    forced = (blk == 0) | (blk == cur) | (blk == cur - 1)
    score = jnp.where(future, -jnp.inf, jnp.where(forced, jnp.inf, impsel))
    chosen = jnp.zeros(impsel.shape, F32)
    for _ in range(n_pick):
        best = jnp.max(score, axis=0, keepdims=True)
        first = jnp.min(jnp.where(score == best, blk, ns), axis=0, keepdims=True)
        hit = (blk == first) & (best > -jnp.inf)
        chosen = jnp.where(hit, 1.0, chosen)
        score = jnp.where(hit, -jnp.inf, score)
    return jnp.where(chosen > 0.0, 0.0, NEG)


def _pool_matrix(ns, nc):
    j = np.arange(ns)[:, None]
    n = np.arange(nc)[None, :]
    ratio = SEL_BLOCK // CMP_STRIDE
    return jnp.asarray((n >= ratio * j - 1) & (n <= ratio * j + ratio - 1), BF16)


def _nsa_prompt_kernel(tab_ref, q_ref, g_ref, kc_ref, vct_ref, ks_ref, vst_ref, kw_ref, vwt_ref, band_ref, pool_ref, o_ref,
                       rhs_scr, mask_scr, acc_scr, m_scr, sc_scr):
    i = pl.program_id(0)
    ncp = kc_ref.shape[0]
    ns = pool_ref.shape[0]
    s0 = i * Q_BLOCK
    q_t = (q_ref[...] * HD ** -0.5).T
    g_t = g_ref[...].T
    lane_q = lax.broadcasted_iota(jnp.int32, (1, G_NSA * Q_BLOCK), 1) & (Q_BLOCK - 1)
    qpos = s0 + lax.broadcasted_iota(jnp.int32, (1, Q_BLOCK), 1)
    rhs_scr[...] = jnp.zeros(rhs_scr.shape, BF16)
    heads_out = []
    for k in range(N_KV):
        heads = [G_NSA * k + g for g in range(G_NSA)]
        lanes4 = lambda f: jnp.concatenate([f(h) for h in heads], axis=1)
        qcols = lanes4(lambda h: q_t[h * HD:(h + 1) * HD, :])
        zero = jnp.zeros_like(qcols)
        top = jnp.concatenate([qcols, zero] if k == 0 else [zero, qcols], axis=0).astype(BF16)
        rhs_scr[0:N_KV * HD, :] = top
        far_row = lanes4(lambda h: band_ref[h, 0:1, :])

        n0 = pl.multiple_of(jnp.clip(8 * i - 16, 0, ncp - 32), 8)
        nrow = lax.broadcasted_iota(jnp.int32, (ncp, 1), 0)
        sc_scr[...] = jnp.dot(kc_ref[...], top, preferred_element_type=F32) + jnp.where(nrow < n0, far_row, NEG)
        d_edge = qpos - (CMP_STRIDE * (n0 + lax.broadcasted_iota(jnp.int32, (32, 1), 0)) + CMP_BLOCK - 1)
        edge_bias = lanes4(lambda h: jnp.where(d_edge >= 0, _bias_chain(d_edge, [tab_ref[b, h] for b in range(N_BUCKETS)]), NEG))
        sc_scr[pl.ds(n0, 32), :] = jnp.dot(kc_ref[pl.ds(n0, 32), :], top, preferred_element_type=F32) + edge_bias
        p_c = _softmax_cols(sc_scr[...])
        o_c = jnp.dot(vct_ref[...], p_c.astype(BF16), preferred_element_type=F32)[k * HD:(k + 1) * HD]
        imp = p_c[:, 0:Q_BLOCK]
        for g in range(1, G_NSA):
            imp = imp + p_c[:, g * Q_BLOCK:(g + 1) * Q_BLOCK]
        impsel = _dot_exact_lhs(pool_ref[...], imp)
        mask_t = _select_blocks(impsel, qpos, min(N_SEL, ns)).astype(BF16)
        mask_scr[...] = jnp.concatenate([mask_t] * G_NSA, axis=1)

        m_scr[...] = jnp.full(m_scr.shape, M_INIT, F32)
        acc_scr[...] = jnp.zeros(acc_scr.shape, F32)
        kd = i // 4
        r = i % 4

        def tile(kt, bias):
            rhs_scr[N_KV * HD:N_KV * HD + MASK_ROWS, :] = mask_scr[pl.ds(pl.multiple_of((kt // 2) * MASK_ROWS, MASK_ROWS), MASK_ROWS), :]
            keys = ks_ref[pl.ds(pl.multiple_of(kt * KEY_TILE, KEY_TILE), KEY_TILE), :]
            s = jnp.dot(keys, rhs_scr[...], preferred_element_type=F32) + bias
            m_old = m_scr[...]
            m_new = jnp.maximum(m_old, jnp.max(s, axis=0, keepdims=True))
            p = jnp.exp(s - m_new).astype(BF16)
            acc_scr[...] = jnp.exp(m_old - m_new) * acc_scr[...] + jnp.dot(vst_ref[kt], p, preferred_element_type=F32)
            m_scr[...] = m_new

        band_at = lambda start: lanes4(lambda h: band_ref[h, pl.ds(pl.multiple_of(start, Q_BLOCK), KEY_TILE), :])
        prev_near = (r == 0) & (kd >= 1)

        def far_body(kt, carry):
            tile(kt, far_row)
            return carry

        lax.fori_loop(0, kd - prev_near.astype(jnp.int32), far_body, 0)

        @pl.when(prev_near)
        def _():
            tile(kd - 1, band_at(0))

        tile(kd, band_at(KEY_TILE - Q_BLOCK * r))
        acc = acc_scr[...]
        o_s = acc[k * HD:(k + 1) * HD] / acc[N_KV * HD:N_KV * HD + 1]

        ws = pl.multiple_of(jnp.maximum(s0 - WINDOW, 0), Q_BLOCK)
        u0 = pl.multiple_of(WINDOW - (s0 - ws), Q_BLOCK)
        n_win = WINDOW + Q_BLOCK
        u = u0 + lax.broadcasted_iota(jnp.int32, (n_win, 1), 0)
        s_w = (jnp.dot(kw_ref[pl.ds(ws, n_win), :], top, preferred_element_type=F32)
               + lanes4(lambda h: band_ref[h, pl.ds(u0, n_win), :]) + jnp.where(u > lane_q, 0.0, NEG))
        m_w = jnp.max(s_w, axis=0, keepdims=True)
        p_w = jnp.exp(s_w - m_w).astype(BF16)
        acc_w = jnp.zeros((V_ROWS, G_NSA * Q_BLOCK), F32)
        for j in range(n_win // Q_BLOCK):
            acc_w = acc_w + jnp.dot(vwt_ref[ws // Q_BLOCK + j], p_w[j * Q_BLOCK:(j + 1) * Q_BLOCK], preferred_element_type=F32)
        o_w = acc_w[k * HD:(k + 1) * HD] / acc_w[N_KV * HD:N_KV * HD + 1]

        for g, h in enumerate(heads):
            cols = slice(g * Q_BLOCK, (g + 1) * Q_BLOCK)
            heads_out.append(o_c[:, cols] * g_t[h:h + 1] + o_s[:, cols] * g_t[H_NSA + h:H_NSA + h + 1]
                             + o_w[:, cols] * g_t[2 * H_NSA + h:2 * H_NSA + h + 1])
    o_ref[...] = jnp.concatenate(heads_out, axis=0).T


def _sel_pattern(rows):
    key = np.arange(rows)[:, None]
    b = np.arange(LANE)[None, :]
    return jnp.asarray((key // SEL_BLOCK) % MASK_ROWS == b, BF16)


def _values_t(v, tile):
    T = v.shape[0]
    vt = jnp.concatenate([v.T, jnp.ones((1, T), F32), jnp.zeros((V_ROWS - N_KV * HD - 1, T), F32)], axis=0)
    return vt.reshape(V_ROWS, T // tile, tile).transpose(1, 0, 2).astype(BF16)


def _nsa_prompt(q, gates, kv, win, kc, vc, band, rel_table):
    T = q.shape[0]
    ncp, ns = kc.shape[0], T // SEL_BLOCK
    ks_aug = jnp.concatenate([kv[:, 256:384].astype(BF16), _sel_pattern(T)], axis=1)
    operands = [q, gates, kc.astype(BF16), vc.T.astype(BF16), ks_aug, _values_t(kv[:, 384:512], KEY_TILE),
                win[:, 0:128].astype(BF16), _values_t(win[:, 128:256], Q_BLOCK), band, _pool_matrix(ns, ncp)]
    blk = lambda w: pl.BlockSpec((Q_BLOCK, w), lambda i: (i, 0))
    return pl.pallas_call(
        _nsa_prompt_kernel,
        grid=(T // Q_BLOCK,),
        in_specs=[pl.BlockSpec(memory_space=pltpu.SMEM), blk(H_NSA * HD), blk(LANE)] + [_resident(a.shape) for a in operands[2:]],
        out_specs=blk(H_NSA * HD),
        out_shape=jax.ShapeDtypeStruct((T, H_NSA * HD), F32),
        scratch_shapes=[pltpu.VMEM((2 * LANE, G_NSA * Q_BLOCK), BF16), pltpu.VMEM((ns, G_NSA * Q_BLOCK), BF16),
                        pltpu.VMEM((V_ROWS, G_NSA * Q_BLOCK), F32), pltpu.VMEM((1, G_NSA * Q_BLOCK), F32),
                        pltpu.VMEM((ncp, G_NSA * Q_BLOCK), F32)],
        compiler_params=_params(("arbitrary",)),
        name="nsa_prompt",
    )(rel_table, *operands)


SMP_PAGES = 8
TOK_PAD = 8
SMP_COLS = H_NSA * TOK_PAD


def _nsa_sample_kernel(pt_ref, *refs, n_pages, n_valid, past):
    pages = refs[:n_pages]
    (q_ref, g_ref, kc_ref, vc_ref, kvn_ref, win_ref, winn_ref, tab_ref, pool_ref, gsum_ref, epat_ref, o_ref,
     top_scr, mask_scr, acc_scr, m_scr, l_scr, oc_scr, ow_scr) = refs[n_pages:]
    j = pl.program_id(1)
    ncp, wbuf = kc_ref.shape[0], win_ref.shape[0]
    tile_keys = n_pages * PAGE_SIZE
    lane = lax.broadcasted_iota(jnp.int32, (1, LANE), 1)
    tok = lane & (TOK_PAD - 1)
    second_kv = lane >= G_NSA * TOK_PAD
    tab = [tab_ref[b:b + 1, :] for b in range(N_BUCKETS)]
    far_row = tab[N_BUCKETS - 1]
    own_rows = lambda x: jnp.where(second_kv, x[HD:2 * HD], x[0:HD])
    pad_rows = lambda x: jnp.concatenate([x, jnp.zeros((LANE - x.shape[0], x.shape[1]), x.dtype)], axis=0)
    trow = lax.broadcasted_iota(jnp.int32, (LANE, 1), 0)
    d_new = tok - trow
    new_bias = jnp.where((d_new >= 0) & (trow < n_valid), _bias_chain(jnp.maximum(d_new, 0), tab), NEG)

    def attend_update(s, values):
        m_old = m_scr[...]
        m_new = jnp.maximum(m_old, jnp.max(s, axis=0, keepdims=True))
        alpha = jnp.exp(m_old - m_new)
        p = jnp.exp(s - m_new)
        l_scr[...] = alpha * l_scr[...] + jnp.sum(p, axis=0, keepdims=True)
        acc_scr[...] = alpha * acc_scr[...] + jnp.dot(values.T.astype(BF16), p.astype(BF16), preferred_element_type=F32)
        m_scr[...] = m_new

    def reset():
        m_scr[...] = jnp.full(m_scr.shape, M_INIT, F32)
        l_scr[...] = jnp.zeros(l_scr.shape, F32)
        acc_scr[...] = jnp.zeros(acc_scr.shape, F32)

    @pl.when(j == 0)
    def _():
        q_t = pad_rows(q_ref[...] * HD ** -0.5).T
        halves = []
        for k in range(N_KV):
            part = jnp.zeros((HD, LANE), F32)
            for g in range(G_NSA):
                h = G_NSA * k + g
                piece = q_t[h * HD:(h + 1) * HD, :]
                part = part + (pltpu.roll(piece, TOK_PAD * h, axis=1) if h else piece)
            halves.append(part)
        top = jnp.concatenate(halves, axis=0).astype(BF16)
        top_scr[...] = top
        qpos = past + tok

        n0 = ncp - 32
        kcb = kc_ref[...].astype(BF16)
        d_edge = qpos - (CMP_STRIDE * (n0 + lax.broadcasted_iota(jnp.int32, (32, 1), 0)) + CMP_BLOCK - 1)
        s_c = jnp.concatenate([
            jnp.dot(kcb[:n0], top, preferred_element_type=F32) + far_row,
            jnp.dot(kcb[n0:], top, preferred_element_type=F32) + jnp.where(d_edge >= 0, _bias_chain(jnp.maximum(d_edge, 0), tab), NEG)], axis=0)
        p_c = _softmax_cols(s_c)
        oc_scr[...] = own_rows(jnp.dot(vc_ref[...].T.astype(BF16), p_c.astype(BF16), preferred_element_type=F32))
        imp = _dot_exact_rhs(p_c, gsum_ref[...], terms=3)
        mask_scr[...] = _select_blocks(_dot_exact_lhs(pool_ref[...], imp), qpos, N_SEL)

        wk = win_ref[...]
        d_w = wbuf + tok - lax.broadcasted_iota(jnp.int32, (wbuf, 1), 0)
        near = wbuf - LANE
        s_w = jnp.dot(wk[:, :LANE].astype(BF16), top, preferred_element_type=F32)
        s_w = (jnp.concatenate([s_w[:near] + far_row, s_w[near:] + _bias_chain(d_w[near:], tab)], axis=0)
               + jnp.where(d_w < WINDOW, 0.0, NEG))
        reset()
        attend_update(s_w, wk[:, LANE:])
        wn = pad_rows(winn_ref[...])
        attend_update(jnp.dot(wn[:, :LANE].astype(BF16), top, preferred_element_type=F32) + new_bias, wn[:, LANE:])
        ow_scr[...] = own_rows(acc_scr[...]) / l_scr[...]
        reset()

    kv_tile = jnp.concatenate([pg[...] for pg in pages], axis=0)
    slab = mask_scr[pl.ds(pl.multiple_of(j * MASK_ROWS, MASK_ROWS), MASK_ROWS), :].astype(BF16)
    rhs = jnp.concatenate([top_scr[...], slab, jnp.zeros((LANE - MASK_ROWS, LANE), BF16)], axis=0)
    s = jnp.dot(jnp.concatenate([kv_tile[:, :LANE].astype(BF16), epat_ref[...]], axis=1), rhs, preferred_element_type=F32)
    near = tile_keys - LANE
    d_near = past + tok - (j * tile_keys + near + lax.broadcasted_iota(jnp.int32, (LANE, 1), 0))
    s = jnp.concatenate([s[:near] + far_row, s[near:] + _bias_chain(d_near, tab)], axis=0)
    attend_update(s, kv_tile[:, LANE:])

    @pl.when(j == pl.num_programs(1) - 1)
    def _():
        kn = pad_rows(kvn_ref[...])
        last_blk = past // SEL_BLOCK
        s_n = (jnp.dot(kn[:, 2 * LANE:3 * LANE].astype(BF16), top_scr[...], preferred_element_type=F32)
               + new_bias + mask_scr[last_blk:last_blk + 1, :])
        attend_update(s_n, kn[:, 3 * LANE:])
        o_s = own_rows(acc_scr[...]) / l_scr[...]
        g_t = pad_rows(g_ref[...]).T
        gate_rows = []
        for b in range(3):
            row = g_t[b * H_NSA:b * H_NSA + 1]
            for h in range(1, H_NSA):
                row = row + pltpu.roll(g_t[b * H_NSA + h:b * H_NSA + h + 1], TOK_PAD * h, axis=1)
            gate_rows.append(row)
        o_col = oc_scr[...] * gate_rows[0] + o_s * gate_rows[1] + ow_scr[...] * gate_rows[2]
        per_head = [o_col if h == 0 else pltpu.roll(o_col, LANE - TOK_PAD * h, axis=1) for h in range(H_NSA)]
        o_ref[...] = jnp.concatenate(per_head, axis=0).T[:TOK_PAD]


def _nsa_sample(pool, page_table, q, gates, kc, vc, kv_new, win_buf, win_new, rel_table, n_valid):
    B, n_pages_total = page_table.shape
    past = n_pages_total * PAGE_SIZE
    ncp = kc.shape[1]
    ns = past // SEL_BLOCK + 1
    nsp = -(-ns // MASK_ROWS) * MASK_ROWS
    col = np.arange(LANE)
    used = col < SMP_COLS
    gsum = jnp.asarray(((col[:, None] // (G_NSA * TOK_PAD) == col[None, :] // (G_NSA * TOK_PAD))
                        & (col[:, None] % TOK_PAD == col[None, :] % TOK_PAD) & used[:, None] & used[None, :]), BF16)
    tab_cols = jnp.pad(jnp.repeat(rel_table, TOK_PAD, axis=1), ((0, 0), (0, LANE - SMP_COLS)))
    consts = [tab_cols, _pool_matrix(nsp, ncp), gsum, _sel_pattern(SMP_PAGES * PAGE_SIZE)]
    per_seq = [q, gates, kc, vc, kv_new, win_buf, win_new]
    seq_spec = lambda a: pl.BlockSpec((None,) + a.shape[1:], lambda b, j, pt: (b,) + (0,) * (a.ndim - 1))
    const = lambda a: pl.BlockSpec(a.shape, lambda b, j, pt: (0,) * a.ndim)

    def page_map(p):
        return lambda b, j, pt: (pt[b, j * SMP_PAGES + p], 0, 1)

    grid_spec = pltpu.PrefetchScalarGridSpec(
        num_scalar_prefetch=1,
        grid=(B, n_pages_total // SMP_PAGES),
        in_specs=[pl.BlockSpec((None, PAGE_SIZE, 2 * LANE), page_map(p)) for p in range(SMP_PAGES)]
                 + [seq_spec(a) for a in per_seq] + [const(a) for a in consts],
        out_specs=pl.BlockSpec((None, TOK_PAD, H_NSA * HD), lambda b, j, pt: (b, 0, 0)),
        scratch_shapes=[pltpu.VMEM((LANE, LANE), BF16), pltpu.VMEM((nsp, LANE), F32), pltpu.VMEM((LANE, LANE), F32),
                        pltpu.VMEM((1, LANE), F32), pltpu.VMEM((1, LANE), F32), pltpu.VMEM((HD, LANE), F32), pltpu.VMEM((HD, LANE), F32)],
    )
    return pl.pallas_call(
        functools.partial(_nsa_sample_kernel, n_pages=SMP_PAGES, n_valid=n_valid, past=past),
        grid_spec=grid_spec,
        out_shape=jax.ShapeDtypeStruct((B, TOK_PAD, H_NSA * HD), F32),
        compiler_params=_params(("arbitrary", "arbitrary")),
        name="nsa_sample",
    )(page_table, *([pool] * SMP_PAGES), *per_seq, *consts)


def _outproj_kernel(x_ref, nsa_ref, rw_ref, gt_ref, lng_ref, lnb_ref, w_ref, o_ref):
    half = H_NSA * HD
    out = (jnp.dot(nsa_ref[...].astype(BF16), w_ref[0:half, :], preferred_element_type=F32)
           + jnp.dot(rw_ref[...].astype(BF16), w_ref[half:, :], preferred_element_type=F32))
    y = ALPHA * x_ref[...] + (1.0 + gt_ref[...]) * out
    o_ref[...] = _layer_norm(y, lng_ref[...], lnb_ref[...])


def _outproj(x, o_nsa, o_rwkv, gate, ln_g, ln_b, w_out):
    rows = x.shape[0]
    tm = min(512, rows)
    row = lambda i: (i, 0)
    return pl.pallas_call(
        _outproj_kernel,
        grid=(rows // tm,),
        in_specs=[pl.BlockSpec((tm, D_MODEL), row), pl.BlockSpec((tm, H_NSA * HD), row), pl.BlockSpec((tm, D_RWKV), row),
                  _mod_spec(gate, tm), _resident((1, D_MODEL)), _resident((1, D_MODEL)), _resident(w_out.shape)],
        out_specs=pl.BlockSpec((tm, D_MODEL), row),
        out_shape=jax.ShapeDtypeStruct((rows, D_MODEL), F32),
        compiler_params=_params(("arbitrary",)),
        name="outproj",
    )(x, o_nsa, o_rwkv, gate, ln_g.reshape(1, -1), ln_b.reshape(1, -1), w_out)


def kernel(x_prompt, x_sample, cache_nsa_kv, cache_nsa_win, state_rwkv_shift, state_rwkv_wkv, page_table, c_prompt, c_sample, rel_table, w_ada, b_ada, ln_g, ln_b, ffn1_gate, ffn1_up, ffn1_down, ffn2_gate, ffn2_up, ffn2_down, w_in, w_out, cmp_pe_k, cmp_w1_k, cmp_b1_k, cmp_w2_k, cmp_pe_v, cmp_w1_v, cmp_b1_v, cmp_w2_v, rwkv_mu, rwkv_w0, rwkv_w2, rwkv_a0, rwkv_a2, rwkv_g2, rwkv_k_k, rwkv_k_a, rwkv_r_k, rwkv_gn_w, rwkv_gn_b):
    assert w_ada.shape[0] == DEPTH == 1 and x_prompt.shape[0] == 1
    l = 0
    lw = dict(cmp_pe_k=cmp_pe_k[l], cmp_w1_k=cmp_w1_k[l], cmp_b1_k=cmp_b1_k[l], cmp_w2_k=cmp_w2_k[l],
              cmp_pe_v=cmp_pe_v[l], cmp_w1_v=cmp_w1_v[l], cmp_b1_v=cmp_b1_v[l], cmp_w2_v=cmp_w2_v[l],
              rwkv_mu=rwkv_mu[l], rwkv_w0=rwkv_w0[l], rwkv_w2=rwkv_w2[l], rwkv_a0=rwkv_a0[l], rwkv_a2=rwkv_a2[l], rwkv_g2=rwkv_g2[l],
              rwkv_k_k=rwkv_k_k[l], rwkv_k_a=rwkv_k_a[l], rwkv_r_k=rwkv_r_k[l], rwkv_gn_w=rwkv_gn_w[l], rwkv_gn_b=rwkv_gn_b[l])
    T = x_prompt.shape[1]
    nb, nt = x_sample.shape[0], x_sample.shape[1]
    assert nt <= TOK_PAD
    n_seq = 1 + nb
    c_all = jnp.concatenate([c_prompt, c_sample, jnp.zeros((-n_seq % 8, D_MODEL), F32)], axis=0)
    mod = _ada(c_all, w_ada[l], b_ada[l])
    mod_p = mod[0:1].reshape(9, 1, D_MODEL)
    mod_s = jnp.repeat(mod[1:n_seq].reshape(nb, 9, D_MODEL), nt, axis=0).transpose(1, 0, 2)
    ffn1 = [w[l].astype(BF16) for w in (ffn1_gate, ffn1_up, ffn1_down)]
    ffn2 = [w[l].astype(BF16) for w in (ffn2_gate, ffn2_up, ffn2_down)]
    w_in_p = _prep_w_in(w_in[l])
    w_out_b = w_out[l].astype(BF16)

    def trunk_in(x, m):
        x1 = _ffn(x, m[0], m[1], m[2], ln_g[l, 0], ln_b[l, 0], *ffn1)
        return x1, _proj(x1, m[3], m[4], w_in_p)

    def trunk_out(x1, o_nsa, o_rwkv, m):
        x2 = _outproj(x1, o_nsa, o_rwkv, m[5], ln_g[l, 1], ln_b[l, 1], w_out_b)
        return _ffn(x2, m[6], m[7], m[8], ln_g[l, 2], ln_b[l, 2], *ffn2)

    xp1, (q, kv, win, gates, pr) = trunk_in(x_prompt[0], mod_p)
    o_rw, wkv_p = _rwkv(pr[None], jnp.zeros((1, 1, RW_PAD), F32), jnp.zeros((1, H_RWKV, HD_RWKV, HD_RWKV), F32), lw, min(RW_BLOCK, T))
    n_rows = T // PAGE_SIZE
    kc, vc = _compress(kv.reshape(n_rows, PAGE_SIZE, 4 * LANE), jnp.arange(n_rows, dtype=jnp.int32)[None], lw)
    o_nsa = _nsa_prompt(q, gates, kv, win, kc[0], vc[0], _band(rel_table), rel_table)
    y_prompt = trunk_out(xp1, o_nsa, o_rw[0], mod_p)
    kv_prompt = kv.reshape(1, 1, T, 4, N_KV, HD)
    win_prompt = win[T - min(WINDOW, T):].reshape(1, 1, -1, 2, N_KV, HD)
    shift_prompt = _rwkv_uncols(pr[T - 1]).reshape(1, 1, RWKV_COLS)

    xs1, (q_s, kv_s, win_s, gates_s, pr_s) = trunk_in(x_sample.reshape(nb * nt, D_MODEL), mod_s)
    tokens = lambda a: jnp.pad(a.reshape(nb, nt, -1), ((0, 0), (0, TOK_PAD - nt), (0, 0)))
    o_rw_s, wkv_s = _rwkv(tokens(pr_s), _rwkv_cols(state_rwkv_shift[l])[:, None], state_rwkv_wkv[l], lw, nt)
    pool = cache_nsa_kv[l].reshape(-1, PAGE_SIZE, 4 * LANE)
    kc_s, vc_s = _compress(pool, page_table, lw)
    win_buf = cache_nsa_win[l]
    o_nsa_s = _nsa_sample(pool, page_table, tokens(q_s), tokens(gates_s), kc_s, vc_s, tokens(kv_s),
                          win_buf.reshape(nb, -1, 2 * LANE), tokens(win_s), rel_table, nt)
    y_sample = trunk_out(xs1, o_nsa_s[:, :nt].reshape(nb * nt, -1), o_rw_s[:, :nt].reshape(nb * nt, -1), mod_s)
    kv_sample = kv_s.reshape(1, nb, nt, 4, N_KV, HD)
    win_sample = jnp.concatenate([win_buf, win_s.reshape(nb, nt, 2, N_KV, HD)], axis=1)[None, :, nt:]
    shift_sample = _rwkv_uncols(pr_s.reshape(nb, nt, -1)[:, -1])[None]
    return (y_prompt[None], y_sample.reshape(nb, nt, D_MODEL), kv_prompt, win_prompt, shift_prompt, wkv_p[None],
            kv_sample, win_sample, shift_sample, wkv_s[None])
```

```python
import functools
import math

import numpy as np
import jax
import jax.numpy as jnp
from jax import lax
from jax.experimental import pallas as pl
from jax.experimental.pallas import tpu as pltpu

D_MODEL = 1024
PAGE_SIZE = 128
H_NSA = 8
N_KV = 2
G_NSA = H_NSA // N_KV
HD = 64
CMP_STRIDE = 16
CMP_BLOCK = 2 * CMP_STRIDE
CMP_HIDDEN = 256
SEL_BLOCK = 64
N_SEL = 16
WINDOW = 512
Q_BLOCK = 128
N_BUCKETS = 32
MAX_DISTANCE = 128
H_RWKV = 8
HD_RWKV = 64
D_RWKV = H_RWKV * HD_RWKV
DECAY_LORA = 32
AAA_LORA = 32
GATE_LORA = 96
GN_EPS = 64e-5
D_FF = 2816
LN_EPS = 1e-5
DEPTH = 1
ALPHA = (2 * DEPTH) ** 0.25

NSA_SIZES = (H_NSA * HD,) + (N_KV * HD,) * 6 + (H_NSA * 3,)
RWKV_SIZES = (D_RWKV, D_RWKV, D_RWKV, DECAY_LORA, AAA_LORA, GATE_LORA)
NSA_COLS = sum(NSA_SIZES)
RWKV_COLS = sum(RWKV_SIZES)

F32 = jnp.float32
BF16 = jnp.bfloat16
LANE = 128
NEG = -(2.0 ** 100)
M_INIT = -(2.0 ** 103)
VMEM_LIMIT = 56 * 1024 * 1024

RW_PAD = 3 * D_RWKV + 3 * LANE
P_Q, P_KV, P_WIN, P_GATE, P_RW = 0, 512, 1024, 1280, 1408
P_COLS = P_RW + RW_PAD
KEY_TILE = 512
MASK_ROWS = 16
V_ROWS = 144
MASK_ROW0 = N_KV * HD
FAR_ROW0 = MASK_ROW0 + MASK_ROWS


def _bucket_lows():
    d = np.arange(0, 4 * MAX_DISTANCE, dtype=np.int64)
    max_exact = N_BUCKETS // 2
    df = np.maximum(d, 1).astype(np.float32)
    large = max_exact + (np.log(df / np.float32(max_exact)) / np.float32(math.log(MAX_DISTANCE / max_exact))
                         * np.float32(N_BUCKETS - max_exact)).astype(np.int32)
    b = np.where(d < max_exact, d, np.minimum(large, N_BUCKETS - 1))
    lows = [int(np.argmax(b >= k)) for k in range(N_BUCKETS)]
    return b, lows


_BUCKET_OF, _BUCKET_LOW = _bucket_lows()
FAR_DIST = _BUCKET_LOW[N_BUCKETS - 1]


def _resident(shape):
    nd = len(shape)
    return pl.BlockSpec(shape, lambda *_: (0,) * nd, pipeline_mode=pl.Buffered(1))


def _params(sem):
    return pltpu.CompilerParams(dimension_semantics=sem, vmem_limit_bytes=VMEM_LIMIT)


def _split2(x):
    hi = x.astype(BF16)
    lo = (x - hi.astype(F32)).astype(BF16)
    return hi, lo


def _dot_exact_rhs(x, rhs_bf16, terms=2):
    acc = None
    rem = x
    for _ in range(terms):
        part = rem.astype(BF16)
        d = jnp.dot(part, rhs_bf16, preferred_element_type=F32)
        acc = d if acc is None else acc + d
        rem = rem - part.astype(F32)
    return acc


def _dot_exact_lhs(lhs_bf16, x, terms=3):
    acc = None
    rem = x
    for _ in range(terms):
        part = rem.astype(BF16)
        d = jnp.dot(lhs_bf16, part, preferred_element_type=F32)
        acc = d if acc is None else acc + d
        rem = rem - part.astype(F32)
    return acc


def _layer_norm(y, g, b):
    mu = jnp.mean(y, axis=-1, keepdims=True)
    yc = y - mu
    var = jnp.mean(yc * yc, axis=-1, keepdims=True)
    return yc * lax.rsqrt(var + LN_EPS) * g + b


def _bias_chain(d, tab_rows):
    out = tab_rows[0] + jnp.zeros(d.shape, F32)
    for b in range(1, N_BUCKETS):
        out = jnp.where(d >= _BUCKET_LOW[b], tab_rows[b], out)
    return out


def _ada_kernel(c_ref, w_ref, b_ref, o_ref):
    c = c_ref[...]
    h = (c * jax.nn.sigmoid(c)).astype(BF16)
    o_ref[...] = jnp.dot(h, w_ref[...].astype(BF16), preferred_element_type=F32) + b_ref[...]


def _ada(c_all, w_ada, b_ada):
    rows, n = c_all.shape[0], w_ada.shape[1]
    tn = 1152
    return pl.pallas_call(
        _ada_kernel,
        grid=(n // tn,),
        in_specs=[pl.BlockSpec((rows, D_MODEL), lambda j: (0, 0)),
                  pl.BlockSpec((D_MODEL, tn), lambda j: (0, j)),
                  pl.BlockSpec((1, tn), lambda j: (0, j))],
        out_specs=pl.BlockSpec((rows, tn), lambda j: (0, j)),
        out_shape=jax.ShapeDtypeStruct((rows, n), F32),
        compiler_params=_params(("arbitrary",)),
        name="ada",
    )(c_all, w_ada, b_ada.reshape(1, n))


FF_CHUNKS = 2


def _ffn_kernel(x_ref, sh_ref, sc_ref, gt_ref, lng_ref, lnb_ref, wg_ref, wu_ref, wd_ref, o_ref):
    x = x_ref[...]
    h = (x * (1.0 + sc_ref[...]) + sh_ref[...]).astype(BF16)
    ck = D_FF // FF_CHUNKS
    acc = jnp.zeros(x.shape, F32)
    for c in range(FF_CHUNKS):
        a = jnp.dot(h, wg_ref[:, c * ck:(c + 1) * ck], preferred_element_type=F32)
        b = jnp.dot(h, wu_ref[:, c * ck:(c + 1) * ck], preferred_element_type=F32)
        t = (a * jax.nn.sigmoid(a) * b).astype(BF16)
        acc = acc + jnp.dot(t, wd_ref[c * ck:(c + 1) * ck, :], preferred_element_type=F32)
    y = ALPHA * x + (1.0 + gt_ref[...]) * (0.5 * acc)
    o_ref[...] = _layer_norm(y, lng_ref[...], lnb_ref[...])


def _mod_spec(mod, tm):
    if mod.shape[0] == 1:
        return pl.BlockSpec((1, D_MODEL), lambda i: (0, 0))
    return pl.BlockSpec((tm, D_MODEL), lambda i: (i, 0))


def _ffn(x, shift, scale, gate, ln_g, ln_b, wg, wu, wd):
    rows = x.shape[0]
    tm = min(512, rows)
    row = lambda i: (i, 0)
    return pl.pallas_call(
        _ffn_kernel,
        grid=(rows // tm,),
        in_specs=[pl.BlockSpec((tm, D_MODEL), row), _mod_spec(shift, tm), _mod_spec(scale, tm), _mod_spec(gate, tm),
                  _resident((1, D_MODEL)), _resident((1, D_MODEL)),
                  _resident((D_MODEL, D_FF)), _resident((D_MODEL, D_FF)), _resident((D_FF, D_MODEL))],
        out_specs=pl.BlockSpec((tm, D_MODEL), row),
        out_shape=jax.ShapeDtypeStruct((rows, D_MODEL), F32),
        compiler_params=_params(("arbitrary",)),
        name="ffn",
    )(x, shift, scale, gate, ln_g.reshape(1, -1), ln_b.reshape(1, -1), wg, wu, wd)


def _proj_kernel(x_ref, sh_ref, sc_ref, w_ref, q_ref, kv_ref, win_ref, g_ref, pr_ref):
    h = (x_ref[...] * (1.0 + sc_ref[...]) + sh_ref[...]).astype(BF16)
    p = jnp.dot(h, w_ref[...], preferred_element_type=F32)
    q_ref[...] = p[:, P_Q:P_KV]
    kv_ref[...] = p[:, P_KV:P_WIN]
    win_ref[...] = p[:, P_WIN:P_GATE]
    g_ref[...] = jax.nn.sigmoid(p[:, P_GATE:P_RW])
    pr_ref[...] = p[:, P_RW:P_COLS]


def _proj(x, shift, scale, w_in_p):
    rows = x.shape[0]
    tm = min(512, rows)
    row = lambda i: (i, 0)
    widths = (512, 512, 256, LANE, RW_PAD)
    return pl.pallas_call(
        _proj_kernel,
        grid=(rows // tm,),
        in_specs=[pl.BlockSpec((tm, D_MODEL), row), _mod_spec(shift, tm), _mod_spec(scale, tm),
                  _resident((D_MODEL, P_COLS))],
        out_specs=[pl.BlockSpec((tm, w), row) for w in widths],
        out_shape=[jax.ShapeDtypeStruct((rows, w), F32) for w in widths],
        compiler_params=_params(("arbitrary",)),
        name="proj",
    )(x, shift, scale, w_in_p)


def _prep_w_in(w_in):
    pad = lambda a, n: jnp.pad(a, ((0, 0), (0, n - a.shape[1])))
    nsa, rw = w_in[:, :NSA_COLS], w_in[:, NSA_COLS:]
    gl = nsa[:, 1280:1304].reshape(D_MODEL, H_NSA, 3).transpose(0, 2, 1).reshape(D_MODEL, 3 * H_NSA)
    cols = [nsa[:, :1280], pad(gl, LANE), _rwkv_cols(rw)]
    return jnp.concatenate(cols, axis=1).astype(BF16)


def _rwkv_cols(a):
    pad = lambda t: jnp.pad(t, [(0, 0)] * (t.ndim - 1) + [(0, LANE - t.shape[-1])])
    n = 3 * D_RWKV
    return jnp.concatenate([a[..., :n], pad(a[..., n:n + 32]), pad(a[..., n + 32:n + 64]), pad(a[..., n + 64:n + 160])], axis=-1)


def _rwkv_uncols(a):
    n = 3 * D_RWKV
    return jnp.concatenate([a[..., :n], a[..., n:n + 32], a[..., n + LANE:n + LANE + 32], a[..., n + 2 * LANE:n + 2 * LANE + 96]], axis=-1)


RW_SUB = 8
RW_PAIRS = H_RWKV // 2
RW_BLOCK = 64


def _lora(x, w_ref):
    w = w_ref[...]
    w_hi = w.astype(BF16)
    w_lo = (w - w_hi.astype(F32)).astype(BF16)
    return _dot_exact_rhs(x, w_hi) + jnp.dot(x.astype(BF16), w_lo, preferred_element_type=F32)


def _rwkv_kernel(pr_ref, sh0_ref, s0_ref, mu_ref, w0_ref, a0_ref, kk_ref, ka_ref, rk_ref, gw_ref, gb_ref,
                 w2_ref, a2_ref, g2_ref, bo_ref, bo2_ref, o_ref, sout_ref,
                 prev_scr, s_scr, nkk_scr, dec_scr, bet_scr, k2_scr, r_scr, v_scr, yacc_scr, *, n_valid):
    tb = pr_ref.shape[0]
    step = pl.program_id(1)

    @pl.when(step == 0)
    def _():
        prev_scr[...] = sh0_ref[...]
        s_scr[...] = s0_ref[...]

    yacc_scr[...] = jnp.zeros(yacc_scr.shape, F32)
    p = pr_ref[...]
    rows = lax.broadcasted_iota(jnp.int32, (tb, 1), 0)
    prev = jnp.where(rows == 0, prev_scr[...], pltpu.roll(p, 1, axis=0))
    prev_scr[...] = p[tb - 1:tb, :]
    xs = p + (prev - p) * mu_ref[...]
    n = D_RWKV
    r, k, v = xs[:, :n], xs[:, n:2 * n], xs[:, 2 * n:3 * n]
    wl, al, gl = xs[:, 3 * n:3 * n + LANE], xs[:, 3 * n + LANE:3 * n + 2 * LANE], xs[:, 3 * n + 2 * LANE:]
    z = -(w0_ref[...] + _lora(jnp.tanh(wl), w2_ref))
    w = -(jnp.maximum(z, 0.0) + jnp.log(1.0 + jnp.exp(-jnp.abs(z)))) - 0.5
    a = jax.nn.sigmoid(a0_ref[...] + _lora(al, a2_ref))
    g = _lora(jax.nn.sigmoid(gl), g2_ref)
    kk = k * kk_ref[...]
    ss = _dot_exact_rhs(kk * kk, bo_ref[...])
    kk = kk / jnp.maximum(jnp.sqrt(ss), 1e-12)
    k2 = k * (1.0 + (a - 1.0) * ka_ref[...])
    nkk_scr[...] = -kk
    dec_scr[...] = jnp.exp(-jnp.exp(w))
    bet_scr[...] = kk * a
    k2_scr[...] = k2
    r_scr[...] = r
    v_scr[...] = v

    bo2 = bo2_ref[...]
    lane = lax.broadcasted_iota(jnp.int32, (HD_RWKV, LANE), 1)
    sub = lax.broadcasted_iota(jnp.int32, (HD_RWKV, LANE), 0)
    lane_in_head = lane - jnp.where(lane >= HD_RWKV, HD_RWKV, 0)
    diag = lane_in_head == sub

    low_half = lane < HD_RWKV

    def head_sums(x):
        first = jnp.sum(jnp.where(low_half, x, 0.0), axis=1, keepdims=True)
        both = jnp.sum(x, axis=1, keepdims=True)
        return jnp.where(low_half, first, both - first)

    def head_sums_mxu(x, split):
        hi = x.astype(BF16)
        if not split:
            return jnp.dot(hi, bo2[:LANE], preferred_element_type=F32)
        lo = (x - hi.astype(F32)).astype(BF16)
        return jnp.dot(jnp.concatenate([hi, lo], axis=1), bo2, preferred_element_type=F32)

    def group(sb, n_tok):
        base = pl.multiple_of(sb * RW_SUB, RW_SUB)
        state = [s_scr[pp] for pp in range(RW_PAIRS)]
        nkk8, dec8, bet8, k28, r8, v8 = (ref[pl.ds(base, RW_SUB), :] for ref in (nkk_scr, dec_scr, bet_scr, k2_scr, r_scr, v_scr))
        sel = jnp.concatenate([jnp.where(diag, v8[tt:tt + 1, pp * LANE:(pp + 1) * LANE], 0.0)
                               for tt in range(n_tok) for pp in range(RW_PAIRS)], axis=0)
        vcols = head_sums_mxu(sel, split=True)
        readout = []
        for tt in range(n_tok):
            for pp in range(RW_PAIRS):
                cols = slice(pp * LANE, (pp + 1) * LANE)
                row = lambda a: a[tt:tt + 1, cols]
                at = (tt * RW_PAIRS + pp) * HD_RWKV
                st = state[pp]
                sa = head_sums(st * row(nkk8))
                st = st * row(dec8) + sa * row(bet8) + vcols[at:at + HD_RWKV] * row(k28)
                state[pp] = st
                readout.append(st * row(r8))
        for pp in range(RW_PAIRS):
            s_scr[pp] = state[pp]
        ycols = head_sums_mxu(jnp.concatenate(readout, axis=0), split=False)
        for tt in range(n_tok):
            collect = lane_in_head == base + tt
            for pp in range(RW_PAIRS):
                at = (tt * RW_PAIRS + pp) * HD_RWKV
                yacc_scr[pp, :HD_RWKV, :] = jnp.where(collect, ycols[at:at + HD_RWKV], yacc_scr[pp, :HD_RWKV, :])

    if n_valid == tb:
        def body(sb, carry):
            group(sb, RW_SUB)
            return carry
        lax.fori_loop(0, tb // RW_SUB, body, 0)
    else:
        group(0, n_valid)

    pieces = []
    lane_t = lax.broadcasted_iota(jnp.int32, (tb, LANE), 1)
    for pp in range(RW_PAIRS):
        yt = yacc_scr[pp].T
        pieces.append(jnp.where(lane_t < HD_RWKV, yt[:tb], pltpu.roll(yt[RW_BLOCK:RW_BLOCK + tb], HD_RWKV, axis=1)))
    y = jnp.concatenate(pieces, axis=1)
    mean = _dot_exact_rhs(y, bo_ref[...]) * (1.0 / HD_RWKV)
    yc = y - mean
    var = _dot_exact_rhs(yc * yc, bo_ref[...]) * (1.0 / HD_RWKV)
    yn = yc * lax.rsqrt(var + GN_EPS) * gw_ref[...] + gb_ref[...]
    bonus = _dot_exact_rhs(r * k2 * rk_ref[...], bo_ref[...]) * v
    o_ref[...] = (yn + bonus) * g
    sout_ref[...] = s_scr[...]


def _pair_state(s):
    B = s.shape[0]
    return s.reshape(B, RW_PAIRS, 2, HD_RWKV, HD_RWKV).transpose(0, 1, 3, 2, 4).reshape(B, RW_PAIRS, HD_RWKV, LANE)


def _unpair_state(s):
    B = s.shape[0]
    return s.reshape(B, RW_PAIRS, HD_RWKV, 2, HD_RWKV).transpose(0, 1, 3, 2, 4).reshape(B, H_RWKV, HD_RWKV, HD_RWKV)


def _rwkv(pr, shift0, s0, lw, n_valid):
    B, T, _ = pr.shape
    tb = min(RW_BLOCK, T)
    n = D_RWKV
    vec = lambda a: a.reshape(1, n)
    padrow = lambda a: jnp.pad(a, ((0, LANE - a.shape[0]), (0, 0)))
    blk = np.arange(n) // HD_RWKV
    block_ones = jnp.asarray(blk[:, None] == blk[None, :], BF16)
    consts = [_rwkv_cols(lw['rwkv_mu']).reshape(1, RW_PAD), vec(lw['rwkv_w0']), vec(lw['rwkv_a0']), vec(lw['rwkv_k_k']),
              vec(lw['rwkv_k_a']), vec(lw['rwkv_r_k']), vec(lw['rwkv_gn_w']), vec(lw['rwkv_gn_b']),
              padrow(lw['rwkv_w2']), padrow(lw['rwkv_a2']), padrow(lw['rwkv_g2']), block_ones, jnp.concatenate([block_ones[:LANE, :LANE]] * 2, axis=0)]
    kern = functools.partial(_rwkv_kernel, n_valid=n_valid)
    state_spec = pl.BlockSpec((None, RW_PAIRS, HD_RWKV, LANE), lambda b, j: (b, 0, 0, 0))
    o, s = pl.pallas_call(
        kern,
        grid=(B, T // tb),
        in_specs=[pl.BlockSpec((None, tb, RW_PAD), lambda b, j: (b, j, 0)),
                  pl.BlockSpec((None, 1, RW_PAD), lambda b, j: (b, 0, 0)), state_spec]
                 + [_resident(c.shape) for c in consts],
        out_specs=[pl.BlockSpec((None, tb, n), lambda b, j: (b, j, 0)), state_spec],
        out_shape=[jax.ShapeDtypeStruct((B, T, n), F32), jax.ShapeDtypeStruct((B, RW_PAIRS, HD_RWKV, LANE), F32)],
        scratch_shapes=[pltpu.VMEM((1, RW_PAD), F32), pltpu.VMEM((RW_PAIRS, HD_RWKV, LANE), F32)]
                       + [pltpu.VMEM((tb, n), F32)] * 6 + [pltpu.VMEM((RW_PAIRS, LANE, LANE), F32)],
        compiler_params=_params(("arbitrary", "arbitrary")),
        name="rwkv",
    )(pr, shift0, _pair_state(s0), *consts)
    return o, _unpair_state(s)


CMP_PAGES = 32
CHUNKS_PER_PAGE = PAGE_SIZE // CMP_STRIDE
CMP_K = CMP_STRIDE * N_KV * HD


def _compress_kernel(pt_ref, *refs, n_pages, transposed):
    weights = refs[2 * n_pages + 2:2 * n_pages + 10]
    outs = refs[2 * n_pages + 10:2 * n_pages + 12]
    rows_scr = refs[2 * n_pages + 12]
    rows = CHUNKS_PER_PAGE * n_pages
    for kind in range(2):
        pages, nxt = refs[kind * n_pages:(kind + 1) * n_pages], refs[2 * n_pages + kind]
        pe_ref, w_ref, b_ref, w2_ref = weights[4 * kind:4 * kind + 4]
        for p, pg in enumerate(pages):
            rows_scr[p * PAGE_SIZE:(p + 1) * PAGE_SIZE, :] = pg[...].T if transposed else pg[...]
        nxt_rows = nxt[...].T[:CMP_STRIDE] if transposed else nxt[...]
        x = jnp.concatenate([rows_scr[pl.ds(s, rows, stride=CMP_STRIDE), :] for s in range(CMP_STRIDE)], axis=1)
        xn = jnp.concatenate([nxt_rows[s:s + 1, :] for s in range(CMP_STRIDE)], axis=1)
        x_ext = jnp.concatenate([x, jnp.broadcast_to(xn, (8, CMP_K))], axis=0)
        h_first = jnp.dot((x + pe_ref[0]).astype(BF16), w_ref[0], preferred_element_type=F32)
        h_second = jnp.dot((x_ext + pe_ref[1]).astype(BF16), w_ref[1], preferred_element_type=F32)
        h_next = pltpu.roll(h_second, rows + 8 - 1, axis=0)[:rows]
        hidden = jax.nn.gelu(h_first + h_next + b_ref[...])
        outs[kind][...] = jnp.dot(hidden.astype(BF16), w2_ref[...], preferred_element_type=F32)


def _compress_weights(pe, w1, b1, w2):
    eye = jnp.eye(N_KV, dtype=F32)
    halves = []
    for half in range(2):
        w = w1[half * CMP_STRIDE:(half + 1) * CMP_STRIDE]
        halves.append(jnp.einsum('sdn,hg->shdgn', w, eye).reshape(CMP_K, N_KV * CMP_HIDDEN))
    pe2 = jnp.stack([jnp.broadcast_to(pe[half * CMP_STRIDE:(half + 1) * CMP_STRIDE, None, :], (CMP_STRIDE, N_KV, HD)).reshape(1, CMP_K)
                     for half in range(2)])
    w2bd = jnp.einsum('nd,hg->hngd', w2, eye).reshape(N_KV * CMP_HIDDEN, N_KV * HD)
    return [pe2, jnp.stack(halves).astype(BF16), jnp.tile(b1, N_KV).reshape(1, -1), w2bd.astype(BF16)]


def _compress(pool, page_table, lw, transposed):
    B, n_pages_total = page_table.shape
    n_pages = min(CMP_PAGES, n_pages_total)
    rows = CHUNKS_PER_PAGE * n_pages
    weights = (_compress_weights(lw['cmp_pe_k'], lw['cmp_w1_k'], lw['cmp_b1_k'], lw['cmp_w2_k'])
               + _compress_weights(lw['cmp_pe_v'], lw['cmp_w1_v'], lw['cmp_b1_v'], lw['cmp_w2_v']))
    width = N_KV * HD
    at = (lambda page, kind: (page, kind, 0)) if transposed else (lambda page, kind: (page, 0, kind))

    def page_map(p, kind):
        return lambda b, j, pt: at(pt[b, j * n_pages + p], kind)

    def next_map(kind):
        return lambda b, j, pt: at(pt[b, jnp.minimum((j + 1) * n_pages, n_pages_total - 1)], kind)

    const = lambda a: pl.BlockSpec(a.shape, lambda b, j, pt: (0,) * a.ndim)
    out_spec = pl.BlockSpec((None, rows, N_KV * HD), lambda b, j, pt: (b, j, 0))
    out_shape = jax.ShapeDtypeStruct((B, n_pages_total * CHUNKS_PER_PAGE, N_KV * HD), F32)
    next_rows = PAGE_SIZE if transposed else CMP_STRIDE
    grid_spec = pltpu.PrefetchScalarGridSpec(
        num_scalar_prefetch=1,
        grid=(B, n_pages_total // n_pages),
        in_specs=[pl.BlockSpec((None, PAGE_SIZE, width), page_map(p, kind)) for kind in range(2) for p in range(n_pages)]
                 + [pl.BlockSpec((None, next_rows, width), next_map(kind)) for kind in range(2)] + [const(a) for a in weights],
        out_specs=[out_spec, out_spec],
        scratch_shapes=[pltpu.VMEM((n_pages * PAGE_SIZE, width), F32)],
    )
    return pl.pallas_call(
        functools.partial(_compress_kernel, n_pages=n_pages, transposed=transposed),
        grid_spec=grid_spec,
        out_shape=[out_shape, out_shape],
        compiler_params=_params(("arbitrary", "arbitrary")),
        name="compress",
    )(page_table, *([pool] * (2 * n_pages + 2)), *weights)


BAND_ROWS = 1152


def _band_kernel(tab_ref, bkt_ref, o_ref):
    h = pl.program_id(0)
    bkt = bkt_ref[...]
    out = jnp.full(bkt.shape, NEG, F32)
    for b in range(N_BUCKETS):
        out = jnp.where(bkt == b, tab_ref[b, h], out)
    o_ref[...] = out


def _band(rel_table):
    u = np.arange(BAND_ROWS)[:, None]
    qi = np.arange(Q_BLOCK)[None, :]
    d = qi + WINDOW - u
    bkt = np.where(d >= 0, _BUCKET_OF[np.clip(d, 0, len(_BUCKET_OF) - 1)], -1).astype(np.int32)
    return pl.pallas_call(
        _band_kernel,
        grid=(H_NSA,),
        in_specs=[pl.BlockSpec(memory_space=pltpu.SMEM), pl.BlockSpec((BAND_ROWS, Q_BLOCK), lambda h: (0, 0))],
        out_specs=pl.BlockSpec((None, BAND_ROWS, Q_BLOCK), lambda h: (h, 0, 0)),
        out_shape=jax.ShapeDtypeStruct((H_NSA, BAND_ROWS, Q_BLOCK), F32),
        compiler_params=_params(("arbitrary",)),
        name="band",
    )(rel_table, jnp.asarray(bkt))


def _softmax_cols(s):
    m = jnp.max(s, axis=0, keepdims=True)
    e = jnp.exp(s - m)
    l = jnp.sum(e, axis=0, keepdims=True)
    return e * jnp.where(m > 0.5 * NEG, 1.0 / l, 0.0)


def _select_blocks(impsel, qpos, n_pick):
    ns = impsel.shape[0]
    blk = lax.broadcasted_iota(jnp.int32, impsel.shape, 0)
    cur = jnp.right_shift(qpos, 6)
    future = blk * SEL_BLOCK > qpos
    forced = (blk == 0) | (blk == cur) | (blk == cur - 1)
    score = jnp.where(future, -jnp.inf, jnp.where(forced, jnp.inf, impsel))
    chosen = jnp.zeros(impsel.shape, F32)
    for _ in range(n_pick):
        best = jnp.max(score, axis=0, keepdims=True)
        first = jnp.min(jnp.where(score == best, blk, ns), axis=0, keepdims=True)
        hit = (blk == first) & (best > -jnp.inf)
        chosen = jnp.where(hit, 1.0, chosen)
        score = jnp.where(hit, -jnp.inf, score)
    return jnp.where(chosen > 0.0, 0.0, NEG)


def _pool_matrix(ns, nc):
    j = np.arange(ns)[:, None]
    n = np.arange(nc)[None, :]
    ratio = SEL_BLOCK // CMP_STRIDE
    return jnp.asarray((n >= ratio * j - 1) & (n <= ratio * j + ratio - 1), BF16)


def _nsa_prompt_kernel(tab_ref, q_ref, g_ref, kc_ref, vct_ref, ks_ref, vst_ref, kw_ref, vwt_ref, band_ref, pool_ref, o_ref,
                       rhs_scr, mask_scr, acc_scr, m_scr, sc_scr):
    i = pl.program_id(0)
    ncp = kc_ref.shape[0]
    ns = pool_ref.shape[0]
    s0 = i * Q_BLOCK
    q_t = (q_ref[...] * HD ** -0.5).T
    g_t = g_ref[...].T
    lane_q = lax.broadcasted_iota(jnp.int32, (1, G_NSA * Q_BLOCK), 1) & (Q_BLOCK - 1)
    qpos = s0 + lax.broadcasted_iota(jnp.int32, (1, Q_BLOCK), 1)
    rhs_scr[...] = jnp.zeros(rhs_scr.shape, BF16)
    heads_out = []
    for k in range(N_KV):
        heads = [G_NSA * k + g for g in range(G_NSA)]
        lanes4 = lambda f: jnp.concatenate([f(h) for h in heads], axis=1)
        qcols = lanes4(lambda h: q_t[h * HD:(h + 1) * HD, :])
        zero = jnp.zeros_like(qcols)
        top = jnp.concatenate([qcols, zero] if k == 0 else [zero, qcols], axis=0).astype(BF16)
        rhs_scr[0:N_KV * HD, :] = top
        far_row = lanes4(lambda h: band_ref[h, 0:1, :])

        n0 = pl.multiple_of(jnp.clip(8 * i - 16, 0, ncp - 32), 8)
        nrow = lax.broadcasted_iota(jnp.int32, (ncp, 1), 0)
        sc_scr[...] = jnp.dot(kc_ref[...], top, preferred_element_type=F32) + jnp.where(nrow < n0, far_row, NEG)
        d_edge = qpos - (CMP_STRIDE * (n0 + lax.broadcasted_iota(jnp.int32, (32, 1), 0)) + CMP_BLOCK - 1)
        edge_bias = lanes4(lambda h: jnp.where(d_edge >= 0, _bias_chain(d_edge, [tab_ref[b, h] for b in range(N_BUCKETS)]), NEG))
        sc_scr[pl.ds(n0, 32), :] = jnp.dot(kc_ref[pl.ds(n0, 32), :], top, preferred_element_type=F32) + edge_bias
        p_c = _softmax_cols(sc_scr[...])
        o_c = jnp.dot(vct_ref[...], p_c.astype(BF16), preferred_element_type=F32)[k * HD:(k + 1) * HD]
        imp = p_c[:, 0:Q_BLOCK]
        for g in range(1, G_NSA):
            imp = imp + p_c[:, g * Q_BLOCK:(g + 1) * Q_BLOCK]
        impsel = _dot_exact_lhs(pool_ref[...], imp)
        mask_t = _select_blocks(impsel, qpos, min(N_SEL, ns)).astype(BF16)
        mask_scr[...] = jnp.concatenate([mask_t] * G_NSA, axis=1)

        m_scr[...] = jnp.full(m_scr.shape, M_INIT, F32)
        acc_scr[...] = jnp.zeros(acc_scr.shape, F32)
        kd = i // 4
        r = i % 4

        far_hi = far_row.astype(BF16).astype(F32)
        rhs_scr[FAR_ROW0:FAR_ROW0 + MASK_ROWS, :] = jnp.concatenate(
            [far_hi, far_row - far_hi, jnp.zeros((MASK_ROWS - 2, G_NSA * Q_BLOCK), F32)], axis=0).astype(BF16)

        def attend(slab, kts, extra):
            rhs_scr[MASK_ROW0:MASK_ROW0 + MASK_ROWS, :] = mask_scr[pl.ds(pl.multiple_of(slab * MASK_ROWS, MASK_ROWS), MASK_ROWS), :]
            scores = []
            for kt, add in zip(kts, extra):
                keys = ks_ref[pl.ds(pl.multiple_of(kt * KEY_TILE, KEY_TILE), KEY_TILE), :]
                s = jnp.dot(keys, rhs_scr[...], preferred_element_type=F32)
                scores.append(s if add is None else s + add)
            m_old = m_scr[...]
            m_new = m_old
            for s in scores:
                m_new = jnp.maximum(m_new, jnp.max(s, axis=0, keepdims=True))
            acc = jnp.exp(m_old - m_new) * acc_scr[...]
            for kt, s in zip(kts, scores):
                acc = acc + jnp.dot(vst_ref[kt], jnp.exp(s - m_new).astype(BF16), preferred_element_type=F32)
            acc_scr[...] = acc
            m_scr[...] = m_new

        near_at = lambda start: lanes4(lambda h: band_ref[h, pl.ds(pl.multiple_of(start, Q_BLOCK), KEY_TILE), :]) - far_row
        prev_near = (r == 0) & (kd >= 1)
        kd_odd = (kd & 1) == 1
        even_prev = jnp.logical_not(kd_odd) & prev_near

        def far_body(pair, carry):
            attend(pair, [2 * pair, 2 * pair + 1], [None, None])
            return carry

        lax.fori_loop(0, kd // 2 - even_prev.astype(jnp.int32), far_body, 0)

        @pl.when(kd_odd)
        def _():
            attend(kd // 2, [kd - 1, kd], [jnp.where(prev_near, near_at(0), 0.0), near_at(KEY_TILE - Q_BLOCK * r)])

        @pl.when(even_prev)
        def _():
            attend(kd // 2 - 1, [kd - 2, kd - 1], [None, near_at(0)])

        @pl.when(jnp.logical_not(kd_odd))
        def _():
            attend(kd // 2, [kd], [near_at(KEY_TILE - Q_BLOCK * r)])

        acc = acc_scr[...]
        o_s = acc[k * HD:(k + 1) * HD] / acc[N_KV * HD:N_KV * HD + 1]

        ws = pl.multiple_of(jnp.maximum(s0 - WINDOW, 0), Q_BLOCK)
        u0 = pl.multiple_of(WINDOW - (s0 - ws), Q_BLOCK)
        n_win = WINDOW + Q_BLOCK
        u = u0 + lax.broadcasted_iota(jnp.int32, (n_win, 1), 0)
        s_w = (jnp.dot(kw_ref[pl.ds(ws, n_win), :], top, preferred_element_type=F32)
               + lanes4(lambda h: band_ref[h, pl.ds(u0, n_win), :]) + jnp.where(u > lane_q, 0.0, NEG))
        m_w = jnp.max(s_w, axis=0, keepdims=True)
        p_w = jnp.exp(s_w - m_w).astype(BF16)
        acc_w = jnp.zeros((V_ROWS, G_NSA * Q_BLOCK), F32)
        for j in range(n_win // Q_BLOCK):
            acc_w = acc_w + jnp.dot(vwt_ref[ws // Q_BLOCK + j], p_w[j * Q_BLOCK:(j + 1) * Q_BLOCK], preferred_element_type=F32)
        o_w = acc_w[k * HD:(k + 1) * HD] / acc_w[N_KV * HD:N_KV * HD + 1]

        for g, h in enumerate(heads):
            cols = slice(g * Q_BLOCK, (g + 1) * Q_BLOCK)
            heads_out.append(o_c[:, cols] * g_t[h:h + 1] + o_s[:, cols] * g_t[H_NSA + h:H_NSA + h + 1]
                             + o_w[:, cols] * g_t[2 * H_NSA + h:2 * H_NSA + h + 1])
    o_ref[...] = jnp.concatenate(heads_out, axis=0).T


def _sel_pattern(rows):
    key = np.arange(rows)[:, None]
    b = np.arange(LANE)[None, :]
    ones = (b >= MASK_ROWS) & (b < MASK_ROWS + 2)
    return jnp.asarray(((key // SEL_BLOCK) % MASK_ROWS == b) | ones, BF16)


def _values_t(v, tile):
    T = v.shape[0]
    vt = jnp.concatenate([v.T, jnp.ones((1, T), F32), jnp.zeros((V_ROWS - N_KV * HD - 1, T), F32)], axis=0)
    return vt.reshape(V_ROWS, T // tile, tile).transpose(1, 0, 2).astype(BF16)


def _nsa_prompt(q, gates, kv, win, kc, vc, band, rel_table):
    T = q.shape[0]
    ncp, ns = kc.shape[0], T // SEL_BLOCK
    ks_aug = jnp.concatenate([kv[:, 256:384].astype(BF16), _sel_pattern(T)], axis=1)
    operands = [q, gates, kc.astype(BF16), vc.T.astype(BF16), ks_aug, _values_t(kv[:, 384:512], KEY_TILE),
                win[:, 0:128].astype(BF16), _values_t(win[:, 128:256], Q_BLOCK), band, _pool_matrix(ns, ncp)]
    blk = lambda w: pl.BlockSpec((Q_BLOCK, w), lambda i: (i, 0))
    return pl.pallas_call(
        _nsa_prompt_kernel,
        grid=(T // Q_BLOCK,),
        in_specs=[pl.BlockSpec(memory_space=pltpu.SMEM), blk(H_NSA * HD), blk(LANE)] + [_resident(a.shape) for a in operands[2:]],
        out_specs=blk(H_NSA * HD),
        out_shape=jax.ShapeDtypeStruct((T, H_NSA * HD), F32),
        scratch_shapes=[pltpu.VMEM((2 * LANE, G_NSA * Q_BLOCK), BF16), pltpu.VMEM((ns, G_NSA * Q_BLOCK), BF16),
                        pltpu.VMEM((V_ROWS, G_NSA * Q_BLOCK), F32), pltpu.VMEM((1, G_NSA * Q_BLOCK), F32),
                        pltpu.VMEM((ncp, G_NSA * Q_BLOCK), F32)],
        compiler_params=_params(("arbitrary",)),
        name="nsa_prompt",
    )(rel_table, *operands)


SMP_PAGES = 8
TOK_PAD = 8
SMP_COLS = H_NSA * TOK_PAD


def _nsa_sample_kernel(pt_ref, *refs, n_pages, n_valid, past):
    pages = refs[:n_pages]
    (q_ref, g_ref, kc_ref, vc_ref, kvn_ref, win_ref, winn_ref, tab_ref, pool_ref, gsum_ref, epat_ref, o_ref,
     top_scr, mask_scr, acc_scr, m_scr, l_scr, oc_scr, ow_scr) = refs[n_pages:]
    j = pl.program_id(1)
    ncp, wbuf = kc_ref.shape[0], win_ref.shape[1]
    tile_keys = n_pages * PAGE_SIZE
    lane = lax.broadcasted_iota(jnp.int32, (1, LANE), 1)
    tok = lane & (TOK_PAD - 1)
    second_kv = lane >= G_NSA * TOK_PAD
    tab = [tab_ref[b:b + 1, :] for b in range(N_BUCKETS)]
    far_row = tab[N_BUCKETS - 1]
    own_rows = lambda x: jnp.where(second_kv, x[HD:2 * HD], x[0:HD])
    pad_rows = lambda x: jnp.concatenate([x, jnp.zeros((LANE - x.shape[0], x.shape[1]), x.dtype)], axis=0)
    trow = lax.broadcasted_iota(jnp.int32, (LANE, 1), 0)
    d_new = tok - trow
    new_bias = jnp.where((d_new >= 0) & (trow < n_valid), _bias_chain(jnp.maximum(d_new, 0), tab), NEG)

    def attend_update(s, values_t):
        m_old = m_scr[...]
        m_new = jnp.maximum(m_old, jnp.max(s, axis=0, keepdims=True))
        alpha = jnp.exp(m_old - m_new)
        p = jnp.exp(s - m_new)
        l_scr[...] = alpha * l_scr[...] + jnp.sum(p, axis=0, keepdims=True)
        acc_scr[...] = alpha * acc_scr[...] + jnp.dot(values_t.astype(BF16), p.astype(BF16), preferred_element_type=F32)
        m_scr[...] = m_new

    def reset():
        m_scr[...] = jnp.full(m_scr.shape, M_INIT, F32)
        l_scr[...] = jnp.zeros(l_scr.shape, F32)
        acc_scr[...] = jnp.zeros(acc_scr.shape, F32)

    @pl.when(j == 0)
    def _():
        q_t = pad_rows(q_ref[...] * HD ** -0.5).T
        halves = []
        for k in range(N_KV):
            part = jnp.zeros((HD, LANE), F32)
            for g in range(G_NSA):
                h = G_NSA * k + g
                piece = q_t[h * HD:(h + 1) * HD, :]
                part = part + (pltpu.roll(piece, TOK_PAD * h, axis=1) if h else piece)
            halves.append(part)
        top = jnp.concatenate(halves, axis=0).astype(BF16)
        top_scr[...] = top
        qpos = past + tok

        n0 = ncp - 32
        kcb = kc_ref[...].astype(BF16)
        d_edge = qpos - (CMP_STRIDE * (n0 + lax.broadcasted_iota(jnp.int32, (32, 1), 0)) + CMP_BLOCK - 1)
        s_c = jnp.concatenate([
            jnp.dot(kcb[:n0], top, preferred_element_type=F32) + far_row,
            jnp.dot(kcb[n0:], top, preferred_element_type=F32) + jnp.where(d_edge >= 0, _bias_chain(jnp.maximum(d_edge, 0), tab), NEG)], axis=0)
        p_c = _softmax_cols(s_c)
        oc_scr[...] = own_rows(jnp.dot(vc_ref[...].T.astype(BF16), p_c.astype(BF16), preferred_element_type=F32))
        imp = _dot_exact_rhs(p_c, gsum_ref[...], terms=3)
        mask_scr[...] = _select_blocks(_dot_exact_lhs(pool_ref[...], imp), qpos, N_SEL)

        wk = win_ref[0:LANE, :].T
        d_w = wbuf + tok - lax.broadcasted_iota(jnp.int32, (wbuf, 1), 0)
        near = wbuf - LANE
        s_w = jnp.dot(wk.astype(BF16), top, preferred_element_type=F32)
        s_w = (jnp.concatenate([s_w[:near] + far_row, s_w[near:] + _bias_chain(d_w[near:], tab)], axis=0)
               + jnp.where(d_w < WINDOW, 0.0, NEG))
        reset()
        attend_update(s_w, win_ref[LANE:, :])
        wn = pad_rows(winn_ref[...])
        attend_update(jnp.dot(wn[:, :LANE].astype(BF16), top, preferred_element_type=F32) + new_bias, wn[:, LANE:].T)
        ow_scr[...] = own_rows(acc_scr[...]) / l_scr[...]
        reset()

    k_tile = jnp.concatenate([pg[0:LANE, :].T for pg in pages], axis=0)
    vt_tile = jnp.concatenate([pg[LANE:, :] for pg in pages], axis=1)
    slab = mask_scr[pl.ds(pl.multiple_of(j * MASK_ROWS, MASK_ROWS), MASK_ROWS), :].astype(BF16)
    rhs = jnp.concatenate([top_scr[...], slab, jnp.zeros((LANE - MASK_ROWS, LANE), BF16)], axis=0)
    s = jnp.dot(jnp.concatenate([k_tile.astype(BF16), epat_ref[...]], axis=1), rhs, preferred_element_type=F32)
    near = tile_keys - LANE
    d_near = past + tok - (j * tile_keys + near + lax.broadcasted_iota(jnp.int32, (LANE, 1), 0))
    s = jnp.concatenate([s[:near] + far_row, s[near:] + _bias_chain(d_near, tab)], axis=0)
    attend_update(s, vt_tile)

    @pl.when(j == pl.num_programs(1) - 1)
    def _():
        kn = pad_rows(kvn_ref[...])
        last_blk = past // SEL_BLOCK
        s_n = (jnp.dot(kn[:, 2 * LANE:3 * LANE].astype(BF16), top_scr[...], preferred_element_type=F32)
               + new_bias + mask_scr[last_blk:last_blk + 1, :])
        attend_update(s_n, kn[:, 3 * LANE:].T)
        o_s = own_rows(acc_scr[...]) / l_scr[...]
        g_t = pad_rows(g_ref[...]).T
        gate_rows = []
        for b in range(3):
            row = g_t[b * H_NSA:b * H_NSA + 1]
            for h in range(1, H_NSA):
                row = row + pltpu.roll(g_t[b * H_NSA + h:b * H_NSA + h + 1], TOK_PAD * h, axis=1)
            gate_rows.append(row)
        o_col = oc_scr[...] * gate_rows[0] + o_s * gate_rows[1] + ow_scr[...] * gate_rows[2]
        per_head = [o_col if h == 0 else pltpu.roll(o_col, LANE - TOK_PAD * h, axis=1) for h in range(H_NSA)]
        o_ref[...] = jnp.concatenate(per_head, axis=0).T[:TOK_PAD]


def _nsa_sample(pool, page_table, q, gates, kc, vc, kv_new, win_buf, win_new, rel_table, n_valid):
    B, n_pages_total = page_table.shape
    past = n_pages_total * PAGE_SIZE
    ncp = kc.shape[1]
    ns = past // SEL_BLOCK + 1
    nsp = -(-ns // MASK_ROWS) * MASK_ROWS
    col = np.arange(LANE)
    used = col < SMP_COLS
    gsum = jnp.asarray(((col[:, None] // (G_NSA * TOK_PAD) == col[None, :] // (G_NSA * TOK_PAD))
                        & (col[:, None] % TOK_PAD == col[None, :] % TOK_PAD) & used[:, None] & used[None, :]), BF16)
    tab_cols = jnp.pad(jnp.repeat(rel_table, TOK_PAD, axis=1), ((0, 0), (0, LANE - SMP_COLS)))
    consts = [tab_cols, _pool_matrix(nsp, ncp), gsum, _sel_pattern(SMP_PAGES * PAGE_SIZE)]
    per_seq = [q, gates, kc, vc, kv_new, win_buf, win_new]
    seq_spec = lambda a: pl.BlockSpec((None,) + a.shape[1:], lambda b, j, pt: (b,) + (0,) * (a.ndim - 1))
    const = lambda a: pl.BlockSpec(a.shape, lambda b, j, pt: (0,) * a.ndim)

    def page_map(p):
        return lambda b, j, pt: (pt[b, j * SMP_PAGES + p], 1, 0)

    grid_spec = pltpu.PrefetchScalarGridSpec(
        num_scalar_prefetch=1,
        grid=(B, n_pages_total // SMP_PAGES),
        in_specs=[pl.BlockSpec((None, 2 * LANE, PAGE_SIZE), page_map(p)) for p in range(SMP_PAGES)]
                 + [seq_spec(a) for a in per_seq] + [const(a) for a in consts],
        out_specs=pl.BlockSpec((None, TOK_PAD, H_NSA * HD), lambda b, j, pt: (b, 0, 0)),
        scratch_shapes=[pltpu.VMEM((LANE, LANE), BF16), pltpu.VMEM((nsp, LANE), F32), pltpu.VMEM((LANE, LANE), F32),
                        pltpu.VMEM((1, LANE), F32), pltpu.VMEM((1, LANE), F32), pltpu.VMEM((HD, LANE), F32), pltpu.VMEM((HD, LANE), F32)],
    )
    return pl.pallas_call(
        functools.partial(_nsa_sample_kernel, n_pages=SMP_PAGES, n_valid=n_valid, past=past),
        grid_spec=grid_spec,
        out_shape=jax.ShapeDtypeStruct((B, TOK_PAD, H_NSA * HD), F32),
        compiler_params=_params(("arbitrary", "arbitrary")),
        name="nsa_sample",
    )(page_table, *([pool] * SMP_PAGES), *per_seq, *consts)


def _outproj_kernel(x_ref, nsa_ref, rw_ref, gt_ref, lng_ref, lnb_ref, w_ref, o_ref):
    half = H_NSA * HD
    out = (jnp.dot(nsa_ref[...].astype(BF16), w_ref[0:half, :], preferred_element_type=F32)
           + jnp.dot(rw_ref[...].astype(BF16), w_ref[half:, :], preferred_element_type=F32))
    y = ALPHA * x_ref[...] + (1.0 + gt_ref[...]) * out
    o_ref[...] = _layer_norm(y, lng_ref[...], lnb_ref[...])


def _outproj(x, o_nsa, o_rwkv, gate, ln_g, ln_b, w_out):
    rows = x.shape[0]
    tm = min(512, rows)
    row = lambda i: (i, 0)
    return pl.pallas_call(
        _outproj_kernel,
        grid=(rows // tm,),
        in_specs=[pl.BlockSpec((tm, D_MODEL), row), pl.BlockSpec((tm, H_NSA * HD), row), pl.BlockSpec((tm, D_RWKV), row),
                  _mod_spec(gate, tm), _resident((1, D_MODEL)), _resident((1, D_MODEL)), _resident(w_out.shape)],
        out_specs=pl.BlockSpec((tm, D_MODEL), row),
        out_shape=jax.ShapeDtypeStruct((rows, D_MODEL), F32),
        compiler_params=_params(("arbitrary",)),
        name="outproj",
    )(x, o_nsa, o_rwkv, gate, ln_g.reshape(1, -1), ln_b.reshape(1, -1), w_out)


def kernel(x_prompt, x_sample, cache_nsa_kv, cache_nsa_win, state_rwkv_shift, state_rwkv_wkv, page_table, c_prompt, c_sample, rel_table, w_ada, b_ada, ln_g, ln_b, ffn1_gate, ffn1_up, ffn1_down, ffn2_gate, ffn2_up, ffn2_down, w_in, w_out, cmp_pe_k, cmp_w1_k, cmp_b1_k, cmp_w2_k, cmp_pe_v, cmp_w1_v, cmp_b1_v, cmp_w2_v, rwkv_mu, rwkv_w0, rwkv_w2, rwkv_a0, rwkv_a2, rwkv_g2, rwkv_k_k, rwkv_k_a, rwkv_r_k, rwkv_gn_w, rwkv_gn_b):
    assert w_ada.shape[0] == DEPTH == 1 and x_prompt.shape[0] == 1
    l = 0
    lw = dict(cmp_pe_k=cmp_pe_k[l], cmp_w1_k=cmp_w1_k[l], cmp_b1_k=cmp_b1_k[l], cmp_w2_k=cmp_w2_k[l],
              cmp_pe_v=cmp_pe_v[l], cmp_w1_v=cmp_w1_v[l], cmp_b1_v=cmp_b1_v[l], cmp_w2_v=cmp_w2_v[l],
              rwkv_mu=rwkv_mu[l], rwkv_w0=rwkv_w0[l], rwkv_w2=rwkv_w2[l], rwkv_a0=rwkv_a0[l], rwkv_a2=rwkv_a2[l], rwkv_g2=rwkv_g2[l],
              rwkv_k_k=rwkv_k_k[l], rwkv_k_a=rwkv_k_a[l], rwkv_r_k=rwkv_r_k[l], rwkv_gn_w=rwkv_gn_w[l], rwkv_gn_b=rwkv_gn_b[l])
    T = x_prompt.shape[1]
    nb, nt = x_sample.shape[0], x_sample.shape[1]
    assert nt <= TOK_PAD
    n_seq = 1 + nb
    c_all = jnp.concatenate([c_prompt, c_sample, jnp.zeros((-n_seq % 8, D_MODEL), F32)], axis=0)
    mod = _ada(c_all, w_ada[l], b_ada[l])
    mod_p = mod[0:1].reshape(9, 1, D_MODEL)
    mod_s = jnp.repeat(mod[1:n_seq].reshape(nb, 9, D_MODEL), nt, axis=0).transpose(1, 0, 2)
    ffn1 = [w[l].astype(BF16) for w in (ffn1_gate, ffn1_up, ffn1_down)]
    ffn2 = [w[l].astype(BF16) for w in (ffn2_gate, ffn2_up, ffn2_down)]
    w_in_p = _prep_w_in(w_in[l])
    w_out_b = w_out[l].astype(BF16)

    def trunk_in(x, m):
        x1 = _ffn(x, m[0], m[1], m[2], ln_g[l, 0], ln_b[l, 0], *ffn1)
        return x1, _proj(x1, m[3], m[4], w_in_p)

    def trunk_out(x1, o_nsa, o_rwkv, m):
        x2 = _outproj(x1, o_nsa, o_rwkv, m[5], ln_g[l, 1], ln_b[l, 1], w_out_b)
        return _ffn(x2, m[6], m[7], m[8], ln_g[l, 2], ln_b[l, 2], *ffn2)

    xp1, (q, kv, win, gates, pr) = trunk_in(x_prompt[0], mod_p)
    o_rw, wkv_p = _rwkv(pr[None], jnp.zeros((1, 1, RW_PAD), F32), jnp.zeros((1, H_RWKV, HD_RWKV, HD_RWKV), F32), lw, min(RW_BLOCK, T))
    n_rows = T // PAGE_SIZE
    kc, vc = _compress(kv.reshape(n_rows, PAGE_SIZE, 4 * LANE), jnp.arange(n_rows, dtype=jnp.int32)[None], lw, transposed=False)
    o_nsa = _nsa_prompt(q, gates, kv, win, kc[0], vc[0], _band(rel_table), rel_table)
    y_prompt = trunk_out(xp1, o_nsa, o_rw[0], mod_p)
    kv_prompt = kv.reshape(1, 1, T, 4, N_KV, HD)
    win_prompt = win[T - min(WINDOW, T):].reshape(1, 1, -1, 2, N_KV, HD)
    shift_prompt = _rwkv_uncols(pr[T - 1]).reshape(1, 1, RWKV_COLS)

    xs1, (q_s, kv_s, win_s, gates_s, pr_s) = trunk_in(x_sample.reshape(nb * nt, D_MODEL), mod_s)
    tokens = lambda a: jnp.pad(a.reshape(nb, nt, -1), ((0, 0), (0, TOK_PAD - nt), (0, 0)))
    o_rw_s, wkv_s = _rwkv(tokens(pr_s), _rwkv_cols(state_rwkv_shift[l])[:, None], state_rwkv_wkv[l], lw, nt)
    pool_t = jnp.transpose(cache_nsa_kv[l], (0, 2, 3, 4, 1)).reshape(-1, 4 * LANE, PAGE_SIZE)
    kc_s, vc_s = _compress(pool_t, page_table, lw, transposed=True)
    win_buf = cache_nsa_win[l]
    win_t = jnp.transpose(win_buf, (0, 2, 3, 4, 1)).reshape(nb, 2 * LANE, -1)
    o_nsa_s = _nsa_sample(pool_t, page_table, tokens(q_s), tokens(gates_s), kc_s, vc_s, tokens(kv_s),
                          win_t, tokens(win_s), rel_table, nt)
    y_sample = trunk_out(xs1, o_nsa_s[:, :nt].reshape(nb * nt, -1), o_rw_s[:, :nt].reshape(nb * nt, -1), mod_s)
    kv_sample = kv_s.reshape(1, nb, nt, 4, N_KV, HD)
    win_sample = jnp.concatenate([win_buf, win_s.reshape(nb, nt, 2, N_KV, HD)], axis=1)[None, :, nt:]
    shift_sample = _rwkv_uncols(pr_s.reshape(nb, nt, -1)[:, -1])[None]
    return (y_prompt[None], y_sample.reshape(nb, nt, D_MODEL), kv_prompt, win_prompt, shift_prompt, wkv_p[None],
            kv_sample, win_sample, shift_sample, wkv_s[None])
```

```python
import functools
import math

import numpy as np
import jax
import jax.numpy as jnp
from jax import lax
from jax.experimental import pallas as pl
from jax.experimental.pallas import tpu as pltpu

D_MODEL = 1024
PAGE_SIZE = 128
H_NSA = 8
N_KV = 2
G_NSA = H_NSA // N_KV
HD = 64
CMP_STRIDE = 16
CMP_BLOCK = 2 * CMP_STRIDE
CMP_HIDDEN = 256
SEL_BLOCK = 64
N_SEL = 16
WINDOW = 512
Q_BLOCK = 128
N_BUCKETS = 32
MAX_DISTANCE = 128
H_RWKV = 8
HD_RWKV = 64
D_RWKV = H_RWKV * HD_RWKV
DECAY_LORA = 32
AAA_LORA = 32
GATE_LORA = 96
GN_EPS = 64e-5
D_FF = 2816
LN_EPS = 1e-5
DEPTH = 1
ALPHA = (2 * DEPTH) ** 0.25

NSA_SIZES = (H_NSA * HD,) + (N_KV * HD,) * 6 + (H_NSA * 3,)
RWKV_SIZES = (D_RWKV, D_RWKV, D_RWKV, DECAY_LORA, AAA_LORA, GATE_LORA)
NSA_COLS = sum(NSA_SIZES)
RWKV_COLS = sum(RWKV_SIZES)

F32 = jnp.float32
BF16 = jnp.bfloat16
LANE = 128
NEG = -(2.0 ** 100)
M_INIT = -(2.0 ** 103)
VMEM_LIMIT = 56 * 1024 * 1024

RW_PAD = 3 * D_RWKV + 3 * LANE
P_Q, P_KV, P_WIN, P_GATE, P_RW = 0, 512, 1024, 1280, 1408
P_COLS = P_RW + RW_PAD
KEY_TILE = 512
MASK_ROWS = 16
V_ROWS = 144
MASK_ROW0 = N_KV * HD
SEL_MASK0 = HD
SEL_FAR0 = SEL_MASK0 + MASK_ROWS
V_ROWS_KV = HD + MASK_ROWS


def _bucket_lows():
    d = np.arange(0, 4 * MAX_DISTANCE, dtype=np.int64)
    max_exact = N_BUCKETS // 2
    df = np.maximum(d, 1).astype(np.float32)
    large = max_exact + (np.log(df / np.float32(max_exact)) / np.float32(math.log(MAX_DISTANCE / max_exact))
                         * np.float32(N_BUCKETS - max_exact)).astype(np.int32)
    b = np.where(d < max_exact, d, np.minimum(large, N_BUCKETS - 1))
    lows = [int(np.argmax(b >= k)) for k in range(N_BUCKETS)]
    return b, lows


_BUCKET_OF, _BUCKET_LOW = _bucket_lows()
FAR_DIST = _BUCKET_LOW[N_BUCKETS - 1]


def _resident(shape):
    nd = len(shape)
    return pl.BlockSpec(shape, lambda *_: (0,) * nd, pipeline_mode=pl.Buffered(1))


def _params(sem):
    return pltpu.CompilerParams(dimension_semantics=sem, vmem_limit_bytes=VMEM_LIMIT)


def _split2(x):
    hi = x.astype(BF16)
    lo = (x - hi.astype(F32)).astype(BF16)
    return hi, lo


def _dot_exact_rhs(x, rhs_bf16, terms=2):
    acc = None
    rem = x
    for _ in range(terms):
        part = rem.astype(BF16)
        d = jnp.dot(part, rhs_bf16, preferred_element_type=F32)
        acc = d if acc is None else acc + d
        rem = rem - part.astype(F32)
    return acc


def _dot_exact_lhs(lhs_bf16, x, terms=3):
    acc = None
    rem = x
    for _ in range(terms):
        part = rem.astype(BF16)
        d = jnp.dot(lhs_bf16, part, preferred_element_type=F32)
        acc = d if acc is None else acc + d
        rem = rem - part.astype(F32)
    return acc


def _layer_norm(y, g, b):
    mu = jnp.mean(y, axis=-1, keepdims=True)
    yc = y - mu
    var = jnp.mean(yc * yc, axis=-1, keepdims=True)
    return yc * lax.rsqrt(var + LN_EPS) * g + b


def _bias_chain(d, tab_rows):
    out = tab_rows[0] + jnp.zeros(d.shape, F32)
    for b in range(1, N_BUCKETS):
        out = jnp.where(d >= _BUCKET_LOW[b], tab_rows[b], out)
    return out


def _ada_kernel(c_ref, w_ref, b_ref, o_ref):
    c = c_ref[...]
    h = (c * jax.nn.sigmoid(c)).astype(BF16)
    o_ref[...] = jnp.dot(h, w_ref[...].astype(BF16), preferred_element_type=F32) + b_ref[...]


def _ada(c_all, w_ada, b_ada):
    rows, n = c_all.shape[0], w_ada.shape[1]
    tn = 1152
    return pl.pallas_call(
        _ada_kernel,
        grid=(n // tn,),
        in_specs=[pl.BlockSpec((rows, D_MODEL), lambda j: (0, 0)),
                  pl.BlockSpec((D_MODEL, tn), lambda j: (0, j)),
                  pl.BlockSpec((1, tn), lambda j: (0, j))],
        out_specs=pl.BlockSpec((rows, tn), lambda j: (0, j)),
        out_shape=jax.ShapeDtypeStruct((rows, n), F32),
        compiler_params=_params(("arbitrary",)),
        name="ada",
    )(c_all, w_ada, b_ada.reshape(1, n))


FF_CHUNKS = 2


def _ffn_kernel(x_ref, sh_ref, sc_ref, gt_ref, lng_ref, lnb_ref, wg_ref, wu_ref, wd_ref, o_ref):
    x = x_ref[...]
    h = (x * (1.0 + sc_ref[...]) + sh_ref[...]).astype(BF16)
    ck = D_FF // FF_CHUNKS
    acc = jnp.zeros(x.shape, F32)
    for c in range(FF_CHUNKS):
        a = jnp.dot(h, wg_ref[:, c * ck:(c + 1) * ck], preferred_element_type=F32)
        b = jnp.dot(h, wu_ref[:, c * ck:(c + 1) * ck], preferred_element_type=F32)
        t = (a * jax.nn.sigmoid(a) * b).astype(BF16)
        acc = acc + jnp.dot(t, wd_ref[c * ck:(c + 1) * ck, :], preferred_element_type=F32)
    y = ALPHA * x + (1.0 + gt_ref[...]) * (0.5 * acc)
    o_ref[...] = _layer_norm(y, lng_ref[...], lnb_ref[...])


def _mod_spec(mod, tm):
    if mod.shape[0] == 1:
        return pl.BlockSpec((1, D_MODEL), lambda i: (0, 0))
    return pl.BlockSpec((tm, D_MODEL), lambda i: (i, 0))


def _ffn(x, shift, scale, gate, ln_g, ln_b, wg, wu, wd):
    rows = x.shape[0]
    tm = min(512, rows)
    row = lambda i: (i, 0)
    return pl.pallas_call(
        _ffn_kernel,
        grid=(rows // tm,),
        in_specs=[pl.BlockSpec((tm, D_MODEL), row), _mod_spec(shift, tm), _mod_spec(scale, tm), _mod_spec(gate, tm),
                  _resident((1, D_MODEL)), _resident((1, D_MODEL)),
                  _resident((D_MODEL, D_FF)), _resident((D_MODEL, D_FF)), _resident((D_FF, D_MODEL))],
        out_specs=pl.BlockSpec((tm, D_MODEL), row),
        out_shape=jax.ShapeDtypeStruct((rows, D_MODEL), F32),
        compiler_params=_params(("arbitrary",)),
        name="ffn",
    )(x, shift, scale, gate, ln_g.reshape(1, -1), ln_b.reshape(1, -1), wg, wu, wd)


def _proj_kernel(x_ref, sh_ref, sc_ref, w_ref, q_ref, kv_ref, win_ref, g_ref, pr_ref):
    h = (x_ref[...] * (1.0 + sc_ref[...]) + sh_ref[...]).astype(BF16)
    p = jnp.dot(h, w_ref[...], preferred_element_type=F32)
    q_ref[...] = p[:, P_Q:P_KV]
    kv_ref[...] = p[:, P_KV:P_WIN]
    win_ref[...] = p[:, P_WIN:P_GATE]
    g_ref[...] = jax.nn.sigmoid(p[:, P_GATE:P_RW])
    pr_ref[...] = p[:, P_RW:P_COLS]


def _proj(x, shift, scale, w_in_p):
    rows = x.shape[0]
    tm = min(512, rows)
    row = lambda i: (i, 0)
    widths = (512, 512, 256, LANE, RW_PAD)
    return pl.pallas_call(
        _proj_kernel,
        grid=(rows // tm,),
        in_specs=[pl.BlockSpec((tm, D_MODEL), row), _mod_spec(shift, tm), _mod_spec(scale, tm),
                  _resident((D_MODEL, P_COLS))],
        out_specs=[pl.BlockSpec((tm, w), row) for w in widths],
        out_shape=[jax.ShapeDtypeStruct((rows, w), F32) for w in widths],
        compiler_params=_params(("arbitrary",)),
        name="proj",
    )(x, shift, scale, w_in_p)


def _prep_w_in(w_in):
    pad = lambda a, n: jnp.pad(a, ((0, 0), (0, n - a.shape[1])))
    nsa, rw = w_in[:, :NSA_COLS], w_in[:, NSA_COLS:]
    gl = nsa[:, 1280:1304].reshape(D_MODEL, H_NSA, 3).transpose(0, 2, 1).reshape(D_MODEL, 3 * H_NSA)
    cols = [nsa[:, :1280], pad(gl, LANE), _rwkv_cols(rw)]
    return jnp.concatenate(cols, axis=1).astype(BF16)


def _rwkv_cols(a):
    pad = lambda t: jnp.pad(t, [(0, 0)] * (t.ndim - 1) + [(0, LANE - t.shape[-1])])
    n = 3 * D_RWKV
    return jnp.concatenate([a[..., :n], pad(a[..., n:n + 32]), pad(a[..., n + 32:n + 64]), pad(a[..., n + 64:n + 160])], axis=-1)


def _rwkv_uncols(a):
    n = 3 * D_RWKV
    return jnp.concatenate([a[..., :n], a[..., n:n + 32], a[..., n + LANE:n + LANE + 32], a[..., n + 2 * LANE:n + 2 * LANE + 96]], axis=-1)


RW_SUB = 8
RW_PAIRS = H_RWKV // 2
RW_BLOCK = 64


def _lora(x, w_ref):
    w = w_ref[...]
    w_hi = w.astype(BF16)
    w_lo = (w - w_hi.astype(F32)).astype(BF16)
    return _dot_exact_rhs(x, w_hi) + jnp.dot(x.astype(BF16), w_lo, preferred_element_type=F32)


def _rwkv_kernel(pr_ref, sh0_ref, s0_ref, mu_ref, w0_ref, a0_ref, kk_ref, ka_ref, rk_ref, gw_ref, gb_ref,
                 w2_ref, a2_ref, g2_ref, bo_ref, bo2_ref, o_ref, sout_ref,
                 prev_scr, s_scr, nkk_scr, dec_scr, bet_scr, k2_scr, r_scr, v_scr, yacc_scr, *, n_valid):
    tb = pr_ref.shape[0]
    step = pl.program_id(1)

    @pl.when(step == 0)
    def _():
        prev_scr[...] = sh0_ref[...]
        s_scr[...] = s0_ref[...]

    yacc_scr[...] = jnp.zeros(yacc_scr.shape, F32)
    p = pr_ref[...]
    rows = lax.broadcasted_iota(jnp.int32, (tb, 1), 0)
    prev = jnp.where(rows == 0, prev_scr[...], pltpu.roll(p, 1, axis=0))
    prev_scr[...] = p[tb - 1:tb, :]
    xs = p + (prev - p) * mu_ref[...]
    n = D_RWKV
    r, k, v = xs[:, :n], xs[:, n:2 * n], xs[:, 2 * n:3 * n]
    wl, al, gl = xs[:, 3 * n:3 * n + LANE], xs[:, 3 * n + LANE:3 * n + 2 * LANE], xs[:, 3 * n + 2 * LANE:]
    z = -(w0_ref[...] + _lora(jnp.tanh(wl), w2_ref))
    w = -(jnp.maximum(z, 0.0) + jnp.log(1.0 + jnp.exp(-jnp.abs(z)))) - 0.5
    a = jax.nn.sigmoid(a0_ref[...] + _lora(al, a2_ref))
    g = _lora(jax.nn.sigmoid(gl), g2_ref)
    kk = k * kk_ref[...]
    ss = _dot_exact_rhs(kk * kk, bo_ref[...])
    kk = kk / jnp.maximum(jnp.sqrt(ss), 1e-12)
    k2 = k * (1.0 + (a - 1.0) * ka_ref[...])
    nkk_scr[...] = -kk
    dec_scr[...] = jnp.exp(-jnp.exp(w))
    bet_scr[...] = kk * a
    k2_scr[...] = k2
    r_scr[...] = r
    v_scr[...] = v

    bo2 = bo2_ref[...]
    lane = lax.broadcasted_iota(jnp.int32, (HD_RWKV, LANE), 1)
    sub = lax.broadcasted_iota(jnp.int32, (HD_RWKV, LANE), 0)
    lane_in_head = lane - jnp.where(lane >= HD_RWKV, HD_RWKV, 0)
    diag = lane_in_head == sub

    low_half = lane < HD_RWKV

    def head_sums(x):
        first = jnp.sum(jnp.where(low_half, x, 0.0), axis=1, keepdims=True)
        both = jnp.sum(x, axis=1, keepdims=True)
        return jnp.where(low_half, first, both - first)

    def head_sums_mxu(x, split):
        hi = x.astype(BF16)
        if not split:
            return jnp.dot(hi, bo2[:LANE], preferred_element_type=F32)
        lo = (x - hi.astype(F32)).astype(BF16)
        return jnp.dot(jnp.concatenate([hi, lo], axis=1), bo2, preferred_element_type=F32)

    def group(sb, n_tok):
        base = pl.multiple_of(sb * RW_SUB, RW_SUB)
        state = [s_scr[pp] for pp in range(RW_PAIRS)]
        nkk8, dec8, bet8, k28, r8, v8 = (ref[pl.ds(base, RW_SUB), :] for ref in (nkk_scr, dec_scr, bet_scr, k2_scr, r_scr, v_scr))
        sel = jnp.concatenate([jnp.where(diag, v8[tt:tt + 1, pp * LANE:(pp + 1) * LANE], 0.0)
                               for tt in range(n_tok) for pp in range(RW_PAIRS)], axis=0)
        vcols = head_sums_mxu(sel, split=True)
        readout = []
        for tt in range(n_tok):
            for pp in range(RW_PAIRS):
                cols = slice(pp * LANE, (pp + 1) * LANE)
                row = lambda a: a[tt:tt + 1, cols]
                at = (tt * RW_PAIRS + pp) * HD_RWKV
                st = state[pp]
                sa = head_sums(st * row(nkk8))
                st = st * row(dec8) + sa * row(bet8) + vcols[at:at + HD_RWKV] * row(k28)
                state[pp] = st
                readout.append(st * row(r8))
        for pp in range(RW_PAIRS):
            s_scr[pp] = state[pp]
        ycols = head_sums_mxu(jnp.concatenate(readout, axis=0), split=False)
        for tt in range(n_tok):
            collect = lane_in_head == base + tt
            for pp in range(RW_PAIRS):
                at = (tt * RW_PAIRS + pp) * HD_RWKV
                yacc_scr[pp, :HD_RWKV, :] = jnp.where(collect, ycols[at:at + HD_RWKV], yacc_scr[pp, :HD_RWKV, :])

    if n_valid == tb:
        def body(sb, carry):
            group(sb, RW_SUB)
            return carry
        lax.fori_loop(0, tb // RW_SUB, body, 0)
    else:
        group(0, n_valid)

    pieces = []
    lane_t = lax.broadcasted_iota(jnp.int32, (tb, LANE), 1)
    for pp in range(RW_PAIRS):
        yt = yacc_scr[pp].T
        pieces.append(jnp.where(lane_t < HD_RWKV, yt[:tb], pltpu.roll(yt[RW_BLOCK:RW_BLOCK + tb], HD_RWKV, axis=1)))
    y = jnp.concatenate(pieces, axis=1)
    mean = _dot_exact_rhs(y, bo_ref[...]) * (1.0 / HD_RWKV)
    yc = y - mean
    var = _dot_exact_rhs(yc * yc, bo_ref[...]) * (1.0 / HD_RWKV)
    yn = yc * lax.rsqrt(var + GN_EPS) * gw_ref[...] + gb_ref[...]
    bonus = _dot_exact_rhs(r * k2 * rk_ref[...], bo_ref[...]) * v
    o_ref[...] = (yn + bonus) * g
    sout_ref[...] = s_scr[...]


def _pair_state(s):
    B = s.shape[0]
    return s.reshape(B, RW_PAIRS, 2, HD_RWKV, HD_RWKV).transpose(0, 1, 3, 2, 4).reshape(B, RW_PAIRS, HD_RWKV, LANE)


def _unpair_state(s):
    B = s.shape[0]
    return s.reshape(B, RW_PAIRS, HD_RWKV, 2, HD_RWKV).transpose(0, 1, 3, 2, 4).reshape(B, H_RWKV, HD_RWKV, HD_RWKV)


def _rwkv(pr, shift0, s0, lw, n_valid):
    B, T, _ = pr.shape
    tb = min(RW_BLOCK, T)
    n = D_RWKV
    vec = lambda a: a.reshape(1, n)
    padrow = lambda a: jnp.pad(a, ((0, LANE - a.shape[0]), (0, 0)))
    blk = np.arange(n) // HD_RWKV
    block_ones = jnp.asarray(blk[:, None] == blk[None, :], BF16)
    consts = [_rwkv_cols(lw['rwkv_mu']).reshape(1, RW_PAD), vec(lw['rwkv_w0']), vec(lw['rwkv_a0']), vec(lw['rwkv_k_k']),
              vec(lw['rwkv_k_a']), vec(lw['rwkv_r_k']), vec(lw['rwkv_gn_w']), vec(lw['rwkv_gn_b']),
              padrow(lw['rwkv_w2']), padrow(lw['rwkv_a2']), padrow(lw['rwkv_g2']), block_ones, jnp.concatenate([block_ones[:LANE, :LANE]] * 2, axis=0)]
    kern = functools.partial(_rwkv_kernel, n_valid=n_valid)
    state_spec = pl.BlockSpec((None, RW_PAIRS, HD_RWKV, LANE), lambda b, j: (b, 0, 0, 0))
    o, s = pl.pallas_call(
        kern,
        grid=(B, T // tb),
        in_specs=[pl.BlockSpec((None, tb, RW_PAD), lambda b, j: (b, j, 0)),
                  pl.BlockSpec((None, 1, RW_PAD), lambda b, j: (b, 0, 0)), state_spec]
                 + [_resident(c.shape) for c in consts],
        out_specs=[pl.BlockSpec((None, tb, n), lambda b, j: (b, j, 0)), state_spec],
        out_shape=[jax.ShapeDtypeStruct((B, T, n), F32), jax.ShapeDtypeStruct((B, RW_PAIRS, HD_RWKV, LANE), F32)],
        scratch_shapes=[pltpu.VMEM((1, RW_PAD), F32), pltpu.VMEM((RW_PAIRS, HD_RWKV, LANE), F32)]
                       + [pltpu.VMEM((tb, n), F32)] * 6 + [pltpu.VMEM((RW_PAIRS, LANE, LANE), F32)],
        compiler_params=_params(("arbitrary", "arbitrary")),
        name="rwkv",
    )(pr, shift0, _pair_state(s0), *consts)
    return o, _unpair_state(s)


CMP_PAGES = 32
CHUNKS_PER_PAGE = PAGE_SIZE // CMP_STRIDE
CMP_K = CMP_STRIDE * N_KV * HD


def _compress_kernel(pt_ref, *refs, n_pages, transposed):
    weights = refs[2 * n_pages + 2:2 * n_pages + 10]
    outs = refs[2 * n_pages + 10:2 * n_pages + 12]
    rows_scr = refs[2 * n_pages + 12]
    rows = CHUNKS_PER_PAGE * n_pages
    for kind in range(2):
        pages, nxt = refs[kind * n_pages:(kind + 1) * n_pages], refs[2 * n_pages + kind]
        pe_ref, w_ref, b_ref, w2_ref = weights[4 * kind:4 * kind + 4]
        for p, pg in enumerate(pages):
            rows_scr[p * PAGE_SIZE:(p + 1) * PAGE_SIZE, :] = pg[...].T if transposed else pg[...]
        nxt_rows = nxt[...].T[:CMP_STRIDE] if transposed else nxt[...]
        x = jnp.concatenate([rows_scr[pl.ds(s, rows, stride=CMP_STRIDE), :] for s in range(CMP_STRIDE)], axis=1)
        xn = jnp.concatenate([nxt_rows[s:s + 1, :] for s in range(CMP_STRIDE)], axis=1)
        x_ext = jnp.concatenate([x, jnp.broadcast_to(xn, (8, CMP_K))], axis=0)
        h_first = jnp.dot((x + pe_ref[0]).astype(BF16), w_ref[0], preferred_element_type=F32)
        h_second = jnp.dot((x_ext + pe_ref[1]).astype(BF16), w_ref[1], preferred_element_type=F32)
        h_next = pltpu.roll(h_second, rows + 8 - 1, axis=0)[:rows]
        hidden = jax.nn.gelu(h_first + h_next + b_ref[...])
        outs[kind][...] = jnp.dot(hidden.astype(BF16), w2_ref[...], preferred_element_type=F32)


def _compress_weights(pe, w1, b1, w2):
    eye = jnp.eye(N_KV, dtype=F32)
    halves = []
    for half in range(2):
        w = w1[half * CMP_STRIDE:(half + 1) * CMP_STRIDE]
        halves.append(jnp.einsum('sdn,hg->shdgn', w, eye).reshape(CMP_K, N_KV * CMP_HIDDEN))
    pe2 = jnp.stack([jnp.broadcast_to(pe[half * CMP_STRIDE:(half + 1) * CMP_STRIDE, None, :], (CMP_STRIDE, N_KV, HD)).reshape(1, CMP_K)
                     for half in range(2)])
    w2bd = jnp.einsum('nd,hg->hngd', w2, eye).reshape(N_KV * CMP_HIDDEN, N_KV * HD)
    return [pe2, jnp.stack(halves).astype(BF16), jnp.tile(b1, N_KV).reshape(1, -1), w2bd.astype(BF16)]


def _compress(pool, page_table, lw, transposed):
    B, n_pages_total = page_table.shape
    n_pages = min(CMP_PAGES, n_pages_total)
    rows = CHUNKS_PER_PAGE * n_pages
    weights = (_compress_weights(lw['cmp_pe_k'], lw['cmp_w1_k'], lw['cmp_b1_k'], lw['cmp_w2_k'])
               + _compress_weights(lw['cmp_pe_v'], lw['cmp_w1_v'], lw['cmp_b1_v'], lw['cmp_w2_v']))
    width = N_KV * HD
    at = (lambda page, kind: (page, kind, 0)) if transposed else (lambda page, kind: (page, 0, kind))

    def page_map(p, kind):
        return lambda b, j, pt: at(pt[b, j * n_pages + p], kind)

    def next_map(kind):
        return lambda b, j, pt: at(pt[b, jnp.minimum((j + 1) * n_pages, n_pages_total - 1)], kind)

    const = lambda a: pl.BlockSpec(a.shape, lambda b, j, pt: (0,) * a.ndim)
    out_spec = pl.BlockSpec((None, rows, N_KV * HD), lambda b, j, pt: (b, j, 0))
    out_shape = jax.ShapeDtypeStruct((B, n_pages_total * CHUNKS_PER_PAGE, N_KV * HD), F32)
    next_rows = PAGE_SIZE if transposed else CMP_STRIDE
    grid_spec = pltpu.PrefetchScalarGridSpec(
        num_scalar_prefetch=1,
        grid=(B, n_pages_total // n_pages),
        in_specs=[pl.BlockSpec((None, PAGE_SIZE, width), page_map(p, kind)) for kind in range(2) for p in range(n_pages)]
                 + [pl.BlockSpec((None, next_rows, width), next_map(kind)) for kind in range(2)] + [const(a) for a in weights],
        out_specs=[out_spec, out_spec],
        scratch_shapes=[pltpu.VMEM((n_pages * PAGE_SIZE, width), F32)],
    )
    return pl.pallas_call(
        functools.partial(_compress_kernel, n_pages=n_pages, transposed=transposed),
        grid_spec=grid_spec,
        out_shape=[out_shape, out_shape],
        compiler_params=_params(("arbitrary", "arbitrary")),
        name="compress",
    )(page_table, *([pool] * (2 * n_pages + 2)), *weights)


BAND_ROWS = 1152


def _band_kernel(tab_ref, bkt_ref, o_ref):
    h = pl.program_id(0)
    bkt = bkt_ref[...]
    out = jnp.full(bkt.shape, NEG, F32)
    for b in range(N_BUCKETS):
        out = jnp.where(bkt == b, tab_ref[b, h], out)
    o_ref[...] = out


def _band(rel_table):
    u = np.arange(BAND_ROWS)[:, None]
    qi = np.arange(Q_BLOCK)[None, :]
    d = qi + WINDOW - u
    bkt = np.where(d >= 0, _BUCKET_OF[np.clip(d, 0, len(_BUCKET_OF) - 1)], -1).astype(np.int32)
    return pl.pallas_call(
        _band_kernel,
        grid=(H_NSA,),
        in_specs=[pl.BlockSpec(memory_space=pltpu.SMEM), pl.BlockSpec((BAND_ROWS, Q_BLOCK), lambda h: (0, 0))],
        out_specs=pl.BlockSpec((None, BAND_ROWS, Q_BLOCK), lambda h: (h, 0, 0)),
        out_shape=jax.ShapeDtypeStruct((H_NSA, BAND_ROWS, Q_BLOCK), F32),
        compiler_params=_params(("arbitrary",)),
        name="band",
    )(rel_table, jnp.asarray(bkt))


def _softmax_cols(s):
    m = jnp.max(s, axis=0, keepdims=True)
    e = jnp.exp(s - m)
    l = jnp.sum(e, axis=0, keepdims=True)
    return e * jnp.where(m > 0.5 * NEG, 1.0 / l, 0.0)


def _select_blocks(impsel, qpos, n_pick):
    ns = impsel.shape[0]
    blk = lax.broadcasted_iota(jnp.int32, impsel.shape, 0)
    cur = jnp.right_shift(qpos, 6)
    future = blk * SEL_BLOCK > qpos
    forced = (blk == 0) | (blk == cur) | (blk == cur - 1)
    score = jnp.where(future, -jnp.inf, jnp.where(forced, jnp.inf, impsel))
    chosen = jnp.zeros(impsel.shape, F32)
    for _ in range(n_pick):
        best = jnp.max(score, axis=0, keepdims=True)
        first = jnp.min(jnp.where(score == best, blk, ns), axis=0, keepdims=True)
        hit = (blk == first) & (best > -jnp.inf)
        chosen = jnp.where(hit, 1.0, chosen)
        score = jnp.where(hit, -jnp.inf, score)
    return jnp.where(chosen > 0.0, 0.0, NEG)


def _pool_matrix(ns, nc):
    j = np.arange(ns)[:, None]
    n = np.arange(nc)[None, :]
    ratio = SEL_BLOCK // CMP_STRIDE
    return jnp.asarray((n >= ratio * j - 1) & (n <= ratio * j + ratio - 1), BF16)


def _nsa_prompt_kernel(tab_ref, q_ref, g_ref, kc_ref, vct_ref, ks_ref, vst_ref, kw_ref, vwt_ref, band_ref, pool_ref, o_ref,
                       rhs_scr, mask_scr, acc_scr, m_scr, sc_scr, sa_scr, sb_scr):
    i = pl.program_id(0)
    ncp = kc_ref.shape[0]
    ns = pool_ref.shape[0]
    s0 = i * Q_BLOCK
    q_t = (q_ref[...] * HD ** -0.5).T
    g_t = g_ref[...].T
    lane_q = lax.broadcasted_iota(jnp.int32, (1, G_NSA * Q_BLOCK), 1) & (Q_BLOCK - 1)
    qpos = s0 + lax.broadcasted_iota(jnp.int32, (1, Q_BLOCK), 1)
    rhs_scr[...] = jnp.zeros(rhs_scr.shape, BF16)
    heads_out = []

    def compressed(k):
        lanes4 = lambda f: jnp.concatenate([f(G_NSA * k + g) for g in range(G_NSA)], axis=1)
        qcols = lanes4(lambda h: q_t[h * HD:(h + 1) * HD, :])
        zero = jnp.zeros_like(qcols)
        top = jnp.concatenate([qcols, zero] if k == 0 else [zero, qcols], axis=0).astype(BF16)
        far_row = lanes4(lambda h: band_ref[h, 0:1, :])
        n0 = pl.multiple_of(jnp.clip(8 * i - 16, 0, ncp - 32), 8)
        nrow = lax.broadcasted_iota(jnp.int32, (ncp, 1), 0)
        sc_scr[...] = jnp.dot(kc_ref[...], top, preferred_element_type=F32) + jnp.where(nrow < n0, far_row, NEG)
        d_edge = qpos - (CMP_STRIDE * (n0 + lax.broadcasted_iota(jnp.int32, (32, 1), 0)) + CMP_BLOCK - 1)
        edge_bias = lanes4(lambda h: jnp.where(d_edge >= 0, _bias_chain(d_edge, [tab_ref[b, h] for b in range(N_BUCKETS)]), NEG))
        sc_scr[pl.ds(n0, 32), :] = jnp.dot(kc_ref[pl.ds(n0, 32), :], top, preferred_element_type=F32) + edge_bias
        p_c = _softmax_cols(sc_scr[...])
        o_c = jnp.dot(vct_ref[...], p_c.astype(BF16), preferred_element_type=F32)[k * HD:(k + 1) * HD]
        imp = p_c[:, 0:Q_BLOCK]
        for g in range(1, G_NSA):
            imp = imp + p_c[:, g * Q_BLOCK:(g + 1) * Q_BLOCK]
        return qcols, top, far_row, o_c, _dot_exact_lhs(pool_ref[...], imp, terms=2)

    branches = [compressed(k) for k in range(N_KV)]
    masks = [_select_blocks(b[4], qpos, min(N_SEL, ns)).astype(BF16) for b in branches]
    for k in range(N_KV):
        heads = [G_NSA * k + g for g in range(G_NSA)]
        lanes4 = lambda f: jnp.concatenate([f(h) for h in heads], axis=1)
        qcols, top, far_row, o_c, _ = branches[k]
        mask_scr[...] = jnp.concatenate([masks[k]] * G_NSA, axis=1)

        m_scr[...] = jnp.full(m_scr.shape, M_INIT, F32)
        acc_scr[...] = jnp.zeros(acc_scr.shape, F32)
        kd = i // 4
        r = i % 4

        far_hi = far_row.astype(BF16).astype(F32)
        rhs_scr[0:HD, :] = qcols.astype(BF16)
        rhs_scr[SEL_FAR0:SEL_FAR0 + MASK_ROWS, :] = jnp.concatenate(
            [far_hi, far_row - far_hi, jnp.zeros((MASK_ROWS - 2, G_NSA * Q_BLOCK), F32)], axis=0).astype(BF16)

        def scores(slab, kts, extra):
            rhs_scr[SEL_MASK0:SEL_MASK0 + MASK_ROWS, :] = mask_scr[pl.ds(pl.multiple_of(slab * MASK_ROWS, MASK_ROWS), MASK_ROWS), :]
            rhs = rhs_scr[...]
            out = []
            for kt, add in zip(kts, extra):
                s = jnp.dot(ks_ref[k, pl.ds(pl.multiple_of(kt * KEY_TILE, KEY_TILE), KEY_TILE), :], rhs, preferred_element_type=F32)
                out.append(s if add is None else s + add)
            return out

        def update(kts, tiles):
            m_old = m_scr[...]
            m_new = m_old
            for s in tiles:
                m_new = jnp.maximum(m_new, jnp.max(s, axis=0, keepdims=True))
            acc = jnp.exp(m_old - m_new) * acc_scr[...]
            for kt, s in zip(kts, tiles):
                acc = acc + jnp.dot(vst_ref[k, kt], jnp.exp(s - m_new).astype(BF16), preferred_element_type=F32)
            acc_scr[...] = acc
            m_scr[...] = m_new

        def attend(slab, kts, extra):
            update(kts, scores(slab, kts, extra))

        near_at = lambda start: lanes4(lambda h: band_ref[h, pl.ds(pl.multiple_of(start, Q_BLOCK), KEY_TILE), :]) - far_row
        prev_near = (r == 0) & (kd >= 1)
        kd_odd = (kd & 1) == 1
        even_prev = jnp.logical_not(kd_odd) & prev_near
        n_pairs = kd // 2 - even_prev.astype(jnp.int32)
        n_quads = n_pairs // 2

        def pair_scores(dst, pair):
            lo, hi = scores(pair, [2 * pair, 2 * pair + 1], [None, None])
            dst[0:KEY_TILE, :] = lo
            dst[KEY_TILE:, :] = hi

        def pair_update(src, pair):
            update([2 * pair, 2 * pair + 1], [src[0:KEY_TILE, :], src[KEY_TILE:, :]])

        @pl.when(n_quads > 0)
        def _():
            pair_scores(sa_scr, 0)

        def quad_body(qd, carry):
            first = 2 * qd
            pair_scores(sb_scr, first + 1)
            pair_update(sa_scr, first)
            pair_scores(sa_scr, jnp.minimum(first + 2, 2 * n_quads - 2))
            pair_update(sb_scr, first + 1)
            return carry

        lax.fori_loop(0, n_quads, quad_body, 0)

        @pl.when((n_pairs & 1) == 1)
        def _():
            attend(n_pairs - 1, [2 * n_pairs - 2, 2 * n_pairs - 1], [None, None])

        @pl.when(kd_odd)
        def _():
            attend(kd // 2, [kd - 1, kd], [jnp.where(prev_near, near_at(0), 0.0), near_at(KEY_TILE - Q_BLOCK * r)])

        @pl.when(even_prev)
        def _():
            attend(kd // 2 - 1, [kd - 2, kd - 1], [None, near_at(0)])

        @pl.when(jnp.logical_not(kd_odd))
        def _():
            attend(kd // 2, [kd], [near_at(KEY_TILE - Q_BLOCK * r)])

        acc = acc_scr[...]
        o_s = acc[0:HD] / acc[HD:HD + 1]

        ws = pl.multiple_of(jnp.maximum(s0 - WINDOW, 0), Q_BLOCK)
        u0 = pl.multiple_of(WINDOW - (s0 - ws), Q_BLOCK)
        n_win = WINDOW + Q_BLOCK
        u = u0 + lax.broadcasted_iota(jnp.int32, (n_win, 1), 0)
        s_w = (jnp.dot(kw_ref[pl.ds(ws, n_win), :], top, preferred_element_type=F32)
               + lanes4(lambda h: band_ref[h, pl.ds(u0, n_win), :]) + jnp.where(u > lane_q, 0.0, NEG))
        m_w = jnp.max(s_w, axis=0, keepdims=True)
        p_w = jnp.exp(s_w - m_w).astype(BF16)
        acc_w = jnp.zeros((V_ROWS, G_NSA * Q_BLOCK), F32)
        for j in range(n_win // Q_BLOCK):
            acc_w = acc_w + jnp.dot(vwt_ref[ws // Q_BLOCK + j], p_w[j * Q_BLOCK:(j + 1) * Q_BLOCK], preferred_element_type=F32)
        o_w = acc_w[k * HD:(k + 1) * HD] / acc_w[N_KV * HD:N_KV * HD + 1]

        for g, h in enumerate(heads):
            cols = slice(g * Q_BLOCK, (g + 1) * Q_BLOCK)
            heads_out.append(o_c[:, cols] * g_t[h:h + 1] + o_s[:, cols] * g_t[H_NSA + h:H_NSA + h + 1]
                             + o_w[:, cols] * g_t[2 * H_NSA + h:2 * H_NSA + h + 1])
    o_ref[...] = jnp.concatenate(heads_out, axis=0).T


def _sel_pattern(rows, width):
    key = np.arange(rows)[:, None]
    b = np.arange(width)[None, :]
    ones = (b >= MASK_ROWS) & (b < MASK_ROWS + 2)
    return jnp.asarray(((key // SEL_BLOCK) % MASK_ROWS == b) | ones, BF16)


def _values_t(v, tile):
    T, n = v.shape
    rows = n + MASK_ROWS
    vt = jnp.concatenate([v.T, jnp.ones((1, T), F32), jnp.zeros((rows - n - 1, T), F32)], axis=0)
    return vt.reshape(rows, T // tile, tile).transpose(1, 0, 2).astype(BF16)


def _nsa_prompt(q, gates, kv, win, kc, vc, band, rel_table):
    T = q.shape[0]
    ncp, ns = kc.shape[0], T // SEL_BLOCK
    k_sel = lambda h: kv[:, 256 + h * HD:256 + (h + 1) * HD]
    v_sel = lambda h: kv[:, 384 + h * HD:384 + (h + 1) * HD]
    pattern = _sel_pattern(T, LANE - HD)
    ks_aug = jnp.stack([jnp.concatenate([k_sel(h).astype(BF16), pattern], axis=1) for h in range(N_KV)])
    operands = [q, gates, kc.astype(BF16), vc.T.astype(BF16), ks_aug, jnp.stack([_values_t(v_sel(h), KEY_TILE) for h in range(N_KV)]),
                win[:, 0:128].astype(BF16), _values_t(win[:, 128:256], Q_BLOCK), band, _pool_matrix(ns, ncp)]
    blk = lambda w: pl.BlockSpec((Q_BLOCK, w), lambda i: (i, 0))
    return pl.pallas_call(
        _nsa_prompt_kernel,
        grid=(T // Q_BLOCK,),
        in_specs=[pl.BlockSpec(memory_space=pltpu.SMEM), blk(H_NSA * HD), blk(LANE)] + [_resident(a.shape) for a in operands[2:]],
        out_specs=blk(H_NSA * HD),
        out_shape=jax.ShapeDtypeStruct((T, H_NSA * HD), F32),
        scratch_shapes=[pltpu.VMEM((LANE, G_NSA * Q_BLOCK), BF16), pltpu.VMEM((ns, G_NSA * Q_BLOCK), BF16),
                        pltpu.VMEM((V_ROWS_KV, G_NSA * Q_BLOCK), F32), pltpu.VMEM((1, G_NSA * Q_BLOCK), F32),
                        pltpu.VMEM((ncp, G_NSA * Q_BLOCK), F32)] + [pltpu.VMEM((2 * KEY_TILE, G_NSA * Q_BLOCK), F32)] * 2,
        compiler_params=_params(("arbitrary",)),
        name="nsa_prompt",
    )(rel_table, *operands)


SMP_PAGES = 8
TOK_PAD = 8
SMP_COLS = H_NSA * TOK_PAD


def _nsa_sample_kernel(pt_ref, *refs, n_pages, n_valid, past):
    pages = refs[:n_pages]
    (q_ref, g_ref, kc_ref, vc_ref, kvn_ref, win_ref, winn_ref, tab_ref, pool_ref, gsum_ref, epat_ref, o_ref,
     top_scr, mask_scr, acc_scr, m_scr, l_scr, oc_scr, ow_scr) = refs[n_pages:]
    j = pl.program_id(1)
    ncp, wbuf = kc_ref.shape[0], win_ref.shape[1]
    tile_keys = n_pages * PAGE_SIZE
    lane = lax.broadcasted_iota(jnp.int32, (1, LANE), 1)
    tok = lane & (TOK_PAD - 1)
    second_kv = lane >= G_NSA * TOK_PAD
    tab = [tab_ref[b:b + 1, :] for b in range(N_BUCKETS)]
    far_row = tab[N_BUCKETS - 1]
    own_rows = lambda x: jnp.where(second_kv, x[HD:2 * HD], x[0:HD])
    pad_rows = lambda x: jnp.concatenate([x, jnp.zeros((LANE - x.shape[0], x.shape[1]), x.dtype)], axis=0)
    trow = lax.broadcasted_iota(jnp.int32, (LANE, 1), 0)
    d_new = tok - trow
    new_bias = jnp.where((d_new >= 0) & (trow < n_valid), _bias_chain(jnp.maximum(d_new, 0), tab), NEG)

    def attend_update(s, values_t):
        m_old = m_scr[...]
        m_new = jnp.maximum(m_old, jnp.max(s, axis=0, keepdims=True))
        alpha = jnp.exp(m_old - m_new)
        p = jnp.exp(s - m_new)
        l_scr[...] = alpha * l_scr[...] + jnp.sum(p, axis=0, keepdims=True)
        acc_scr[...] = alpha * acc_scr[...] + jnp.dot(values_t.astype(BF16), p.astype(BF16), preferred_element_type=F32)
        m_scr[...] = m_new

    def reset():
        m_scr[...] = jnp.full(m_scr.shape, M_INIT, F32)
        l_scr[...] = jnp.zeros(l_scr.shape, F32)
        acc_scr[...] = jnp.zeros(acc_scr.shape, F32)

    @pl.when(j == 0)
    def _():
        q_t = pad_rows(q_ref[...] * HD ** -0.5).T
        halves = []
        for k in range(N_KV):
            part = jnp.zeros((HD, LANE), F32)
            for g in range(G_NSA):
                h = G_NSA * k + g
                piece = q_t[h * HD:(h + 1) * HD, :]
                part = part + (pltpu.roll(piece, TOK_PAD * h, axis=1) if h else piece)
            halves.append(part)
        top = jnp.concatenate(halves, axis=0).astype(BF16)
        top_scr[...] = top
        qpos = past + tok

        n0 = ncp - 32
        kcb = kc_ref[...].astype(BF16)
        d_edge = qpos - (CMP_STRIDE * (n0 + lax.broadcasted_iota(jnp.int32, (32, 1), 0)) + CMP_BLOCK - 1)
        s_c = jnp.concatenate([
            jnp.dot(kcb[:n0], top, preferred_element_type=F32) + far_row,
            jnp.dot(kcb[n0:], top, preferred_element_type=F32) + jnp.where(d_edge >= 0, _bias_chain(jnp.maximum(d_edge, 0), tab), NEG)], axis=0)
        p_c = _softmax_cols(s_c)
        oc_scr[...] = own_rows(jnp.dot(vc_ref[...].T.astype(BF16), p_c.astype(BF16), preferred_element_type=F32))
        imp = _dot_exact_rhs(p_c, gsum_ref[...], terms=3)
        mask_scr[...] = _select_blocks(_dot_exact_lhs(pool_ref[...], imp), qpos, N_SEL)

        wk = win_ref[0:LANE, :].T
        d_w = wbuf + tok - lax.broadcasted_iota(jnp.int32, (wbuf, 1), 0)
        near = wbuf - LANE
        s_w = jnp.dot(wk.astype(BF16), top, preferred_element_type=F32)
        s_w = (jnp.concatenate([s_w[:near] + far_row, s_w[near:] + _bias_chain(d_w[near:], tab)], axis=0)
               + jnp.where(d_w < WINDOW, 0.0, NEG))
        reset()
        attend_update(s_w, win_ref[LANE:, :])
        wn = pad_rows(winn_ref[...])
        attend_update(jnp.dot(wn[:, :LANE].astype(BF16), top, preferred_element_type=F32) + new_bias, wn[:, LANE:].T)
        ow_scr[...] = own_rows(acc_scr[...]) / l_scr[...]
        reset()

    k_tile = jnp.concatenate([pg[0:LANE, :].T for pg in pages], axis=0)
    vt_tile = jnp.concatenate([pg[LANE:, :] for pg in pages], axis=1)
    slab = mask_scr[pl.ds(pl.multiple_of(j * MASK_ROWS, MASK_ROWS), MASK_ROWS), :].astype(BF16)
    rhs = jnp.concatenate([top_scr[...], slab, jnp.zeros((LANE - MASK_ROWS, LANE), BF16)], axis=0)
    s = jnp.dot(jnp.concatenate([k_tile.astype(BF16), epat_ref[...]], axis=1), rhs, preferred_element_type=F32)
    near = tile_keys - LANE
    d_near = past + tok - (j * tile_keys + near + lax.broadcasted_iota(jnp.int32, (LANE, 1), 0))
    s = jnp.concatenate([s[:near] + far_row, s[near:] + _bias_chain(d_near, tab)], axis=0)
    attend_update(s, vt_tile)

    @pl.when(j == pl.num_programs(1) - 1)
    def _():
        kn = pad_rows(kvn_ref[...])
        last_blk = past // SEL_BLOCK
        s_n = (jnp.dot(kn[:, 2 * LANE:3 * LANE].astype(BF16), top_scr[...], preferred_element_type=F32)
               + new_bias + mask_scr[last_blk:last_blk + 1, :])
        attend_update(s_n, kn[:, 3 * LANE:].T)
        o_s = own_rows(acc_scr[...]) / l_scr[...]
        g_t = pad_rows(g_ref[...]).T
        gate_rows = []
        for b in range(3):
            row = g_t[b * H_NSA:b * H_NSA + 1]
            for h in range(1, H_NSA):
                row = row + pltpu.roll(g_t[b * H_NSA + h:b * H_NSA + h + 1], TOK_PAD * h, axis=1)
            gate_rows.append(row)
        o_col = oc_scr[...] * gate_rows[0] + o_s * gate_rows[1] + ow_scr[...] * gate_rows[2]
        per_head = [o_col if h == 0 else pltpu.roll(o_col, LANE - TOK_PAD * h, axis=1) for h in range(H_NSA)]
        o_ref[...] = jnp.concatenate(per_head, axis=0).T[:TOK_PAD]


def _nsa_sample(pool, page_table, q, gates, kc, vc, kv_new, win_buf, win_new, rel_table, n_valid):
    B, n_pages_total = page_table.shape
    past = n_pages_total * PAGE_SIZE
    ncp = kc.shape[1]
    ns = past // SEL_BLOCK + 1
    nsp = -(-ns // MASK_ROWS) * MASK_ROWS
    col = np.arange(LANE)
    used = col < SMP_COLS
    gsum = jnp.asarray(((col[:, None] // (G_NSA * TOK_PAD) == col[None, :] // (G_NSA * TOK_PAD))
                        & (col[:, None] % TOK_PAD == col[None, :] % TOK_PAD) & used[:, None] & used[None, :]), BF16)
    tab_cols = jnp.pad(jnp.repeat(rel_table, TOK_PAD, axis=1), ((0, 0), (0, LANE - SMP_COLS)))
    consts = [tab_cols, _pool_matrix(nsp, ncp), gsum, _sel_pattern(SMP_PAGES * PAGE_SIZE, LANE)]
    per_seq = [q, gates, kc, vc, kv_new, win_buf, win_new]
    seq_spec = lambda a: pl.BlockSpec((None,) + a.shape[1:], lambda b, j, pt: (b,) + (0,) * (a.ndim - 1))
    const = lambda a: pl.BlockSpec(a.shape, lambda b, j, pt: (0,) * a.ndim)

    def page_map(p):
        return lambda b, j, pt: (pt[b, j * SMP_PAGES + p], 1, 0)

    grid_spec = pltpu.PrefetchScalarGridSpec(
        num_scalar_prefetch=1,
        grid=(B, n_pages_total // SMP_PAGES),
        in_specs=[pl.BlockSpec((None, 2 * LANE, PAGE_SIZE), page_map(p)) for p in range(SMP_PAGES)]
                 + [seq_spec(a) for a in per_seq] + [const(a) for a in consts],
        out_specs=pl.BlockSpec((None, TOK_PAD, H_NSA * HD), lambda b, j, pt: (b, 0, 0)),
        scratch_shapes=[pltpu.VMEM((LANE, LANE), BF16), pltpu.VMEM((nsp, LANE), F32), pltpu.VMEM((LANE, LANE), F32),
                        pltpu.VMEM((1, LANE), F32), pltpu.VMEM((1, LANE), F32), pltpu.VMEM((HD, LANE), F32), pltpu.VMEM((HD, LANE), F32)],
    )
    return pl.pallas_call(
        functools.partial(_nsa_sample_kernel, n_pages=SMP_PAGES, n_valid=n_valid, past=past),
        grid_spec=grid_spec,
        out_shape=jax.ShapeDtypeStruct((B, TOK_PAD, H_NSA * HD), F32),
        compiler_params=_params(("arbitrary", "arbitrary")),
        name="nsa_sample",
    )(page_table, *([pool] * SMP_PAGES), *per_seq, *consts)


def _outproj_kernel(x_ref, nsa_ref, rw_ref, gt_ref, lng_ref, lnb_ref, w_ref, o_ref):
    half = H_NSA * HD
    out = (jnp.dot(nsa_ref[...].astype(BF16), w_ref[0:half, :], preferred_element_type=F32)
           + jnp.dot(rw_ref[...].astype(BF16), w_ref[half:, :], preferred_element_type=F32))
    y = ALPHA * x_ref[...] + (1.0 + gt_ref[...]) * out
    o_ref[...] = _layer_norm(y, lng_ref[...], lnb_ref[...])


def _outproj(x, o_nsa, o_rwkv, gate, ln_g, ln_b, w_out):
    rows = x.shape[0]
    tm = min(512, rows)
    row = lambda i: (i, 0)
    return pl.pallas_call(
        _outproj_kernel,
        grid=(rows // tm,),
        in_specs=[pl.BlockSpec((tm, D_MODEL), row), pl.BlockSpec((tm, H_NSA * HD), row), pl.BlockSpec((tm, D_RWKV), row),
                  _mod_spec(gate, tm), _resident((1, D_MODEL)), _resident((1, D_MODEL)), _resident(w_out.shape)],
        out_specs=pl.BlockSpec((tm, D_MODEL), row),
        out_shape=jax.ShapeDtypeStruct((rows, D_MODEL), F32),
        compiler_params=_params(("arbitrary",)),
        name="outproj",
    )(x, o_nsa, o_rwkv, gate, ln_g.reshape(1, -1), ln_b.reshape(1, -1), w_out)


def kernel(x_prompt, x_sample, cache_nsa_kv, cache_nsa_win, state_rwkv_shift, state_rwkv_wkv, page_table, c_prompt, c_sample, rel_table, w_ada, b_ada, ln_g, ln_b, ffn1_gate, ffn1_up, ffn1_down, ffn2_gate, ffn2_up, ffn2_down, w_in, w_out, cmp_pe_k, cmp_w1_k, cmp_b1_k, cmp_w2_k, cmp_pe_v, cmp_w1_v, cmp_b1_v, cmp_w2_v, rwkv_mu, rwkv_w0, rwkv_w2, rwkv_a0, rwkv_a2, rwkv_g2, rwkv_k_k, rwkv_k_a, rwkv_r_k, rwkv_gn_w, rwkv_gn_b):
    assert w_ada.shape[0] == DEPTH == 1 and x_prompt.shape[0] == 1
    l = 0
    lw = dict(cmp_pe_k=cmp_pe_k[l], cmp_w1_k=cmp_w1_k[l], cmp_b1_k=cmp_b1_k[l], cmp_w2_k=cmp_w2_k[l],
              cmp_pe_v=cmp_pe_v[l], cmp_w1_v=cmp_w1_v[l], cmp_b1_v=cmp_b1_v[l], cmp_w2_v=cmp_w2_v[l],
              rwkv_mu=rwkv_mu[l], rwkv_w0=rwkv_w0[l], rwkv_w2=rwkv_w2[l], rwkv_a0=rwkv_a0[l], rwkv_a2=rwkv_a2[l], rwkv_g2=rwkv_g2[l],
              rwkv_k_k=rwkv_k_k[l], rwkv_k_a=rwkv_k_a[l], rwkv_r_k=rwkv_r_k[l], rwkv_gn_w=rwkv_gn_w[l], rwkv_gn_b=rwkv_gn_b[l])
    T = x_prompt.shape[1]
    nb, nt = x_sample.shape[0], x_sample.shape[1]
    assert nt <= TOK_PAD
    n_seq = 1 + nb
    c_all = jnp.concatenate([c_prompt, c_sample, jnp.zeros((-n_seq % 8, D_MODEL), F32)], axis=0)
    mod = _ada(c_all, w_ada[l], b_ada[l])
    mod_p = mod[0:1].reshape(9, 1, D_MODEL)
    mod_s = jnp.repeat(mod[1:n_seq].reshape(nb, 9, D_MODEL), nt, axis=0).transpose(1, 0, 2)
    ffn1 = [w[l].astype(BF16) for w in (ffn1_gate, ffn1_up, ffn1_down)]
    ffn2 = [w[l].astype(BF16) for w in (ffn2_gate, ffn2_up, ffn2_down)]
    w_in_p = _prep_w_in(w_in[l])
    w_out_b = w_out[l].astype(BF16)

    def trunk_in(x, m):
        x1 = _ffn(x, m[0], m[1], m[2], ln_g[l, 0], ln_b[l, 0], *ffn1)
        return x1, _proj(x1, m[3], m[4], w_in_p)

    def trunk_out(x1, o_nsa, o_rwkv, m):
        x2 = _outproj(x1, o_nsa, o_rwkv, m[5], ln_g[l, 1], ln_b[l, 1], w_out_b)
        return _ffn(x2, m[6], m[7], m[8], ln_g[l, 2], ln_b[l, 2], *ffn2)

    xp1, (q, kv, win, gates, pr) = trunk_in(x_prompt[0], mod_p)
    o_rw, wkv_p = _rwkv(pr[None], jnp.zeros((1, 1, RW_PAD), F32), jnp.zeros((1, H_RWKV, HD_RWKV, HD_RWKV), F32), lw, min(RW_BLOCK, T))
    n_rows = T // PAGE_SIZE
    kc, vc = _compress(kv.reshape(n_rows, PAGE_SIZE, 4 * LANE), jnp.arange(n_rows, dtype=jnp.int32)[None], lw, transposed=False)
    o_nsa = _nsa_prompt(q, gates, kv, win, kc[0], vc[0], _band(rel_table), rel_table)
    y_prompt = trunk_out(xp1, o_nsa, o_rw[0], mod_p)
    kv_prompt = kv.reshape(1, 1, T, 4, N_KV, HD)
    win_prompt = win[T - min(WINDOW, T):].reshape(1, 1, -1, 2, N_KV, HD)
    shift_prompt = _rwkv_uncols(pr[T - 1]).reshape(1, 1, RWKV_COLS)

    xs1, (q_s, kv_s, win_s, gates_s, pr_s) = trunk_in(x_sample.reshape(nb * nt, D_MODEL), mod_s)
    tokens = lambda a: jnp.pad(a.reshape(nb, nt, -1), ((0, 0), (0, TOK_PAD - nt), (0, 0)))
    o_rw_s, wkv_s = _rwkv(tokens(pr_s), _rwkv_cols(state_rwkv_shift[l])[:, None], state_rwkv_wkv[l], lw, nt)
    pool_t = jnp.transpose(cache_nsa_kv[l], (0, 2, 3, 4, 1)).reshape(-1, 4 * LANE, PAGE_SIZE)
    kc_s, vc_s = _compress(pool_t, page_table, lw, transposed=True)
    win_buf = cache_nsa_win[l]
    win_t = jnp.transpose(win_buf, (0, 2, 3, 4, 1)).reshape(nb, 2 * LANE, -1)
    o_nsa_s = _nsa_sample(pool_t, page_table, tokens(q_s), tokens(gates_s), kc_s, vc_s, tokens(kv_s),
                          win_t, tokens(win_s), rel_table, nt)
    y_sample = trunk_out(xs1, o_nsa_s[:, :nt].reshape(nb * nt, -1), o_rw_s[:, :nt].reshape(nb * nt, -1), mod_s)
    kv_sample = kv_s.reshape(1, nb, nt, 4, N_KV, HD)
    win_sample = jnp.concatenate([win_buf, win_s.reshape(nb, nt, 2, N_KV, HD)], axis=1)[None, :, nt:]
    shift_sample = _rwkv_uncols(pr_s.reshape(nb, nt, -1)[:, -1])[None]
    return (y_prompt[None], y_sample.reshape(nb, nt, D_MODEL), kv_prompt, win_prompt, shift_prompt, wkv_p[None],
            kv_sample, win_sample, shift_sample, wkv_s[None])
```

```python
import functools
import math

import numpy as np
import jax
import jax.numpy as jnp
from jax import lax
from jax.experimental import pallas as pl
from jax.experimental.pallas import tpu as pltpu

D_MODEL = 1024
PAGE_SIZE = 128
H_NSA = 8
N_KV = 2
G_NSA = H_NSA // N_KV
HD = 64
CMP_STRIDE = 16
CMP_BLOCK = 2 * CMP_STRIDE
CMP_HIDDEN = 256
SEL_BLOCK = 64
N_SEL = 16
WINDOW = 512
Q_BLOCK = 128
N_BUCKETS = 32
MAX_DISTANCE = 128
H_RWKV = 8
HD_RWKV = 64
D_RWKV = H_RWKV * HD_RWKV
DECAY_LORA = 32
AAA_LORA = 32
GATE_LORA = 96
GN_EPS = 64e-5
D_FF = 2816
LN_EPS = 1e-5
DEPTH = 1
ALPHA = (2 * DEPTH) ** 0.25

NSA_SIZES = (H_NSA * HD,) + (N_KV * HD,) * 6 + (H_NSA * 3,)
RWKV_SIZES = (D_RWKV, D_RWKV, D_RWKV, DECAY_LORA, AAA_LORA, GATE_LORA)
NSA_COLS = sum(NSA_SIZES)
RWKV_COLS = sum(RWKV_SIZES)

F32 = jnp.float32
BF16 = jnp.bfloat16
LANE = 128
NEG = -(2.0 ** 100)
M_INIT = -(2.0 ** 103)
VMEM_LIMIT = 56 * 1024 * 1024

RW_PAD = 3 * D_RWKV + 3 * LANE
P_Q, P_KV, P_WIN, P_GATE, P_RW = 0, 512, 1024, 1280, 1408
P_COLS = P_RW + RW_PAD
KEY_TILE = 512
MASK_ROWS = 16
V_ROWS = 144
MASK_ROW0 = N_KV * HD
SEL_MASK0 = HD
SEL_FAR0 = SEL_MASK0 + MASK_ROWS
V_ROWS_KV = HD + MASK_ROWS


def _bucket_lows():
    d = np.arange(0, 4 * MAX_DISTANCE, dtype=np.int64)
    max_exact = N_BUCKETS // 2
    df = np.maximum(d, 1).astype(np.float32)
    large = max_exact + (np.log(df / np.float32(max_exact)) / np.float32(math.log(MAX_DISTANCE / max_exact))
                         * np.float32(N_BUCKETS - max_exact)).astype(np.int32)
    b = np.where(d < max_exact, d, np.minimum(large, N_BUCKETS - 1))
    lows = [int(np.argmax(b >= k)) for k in range(N_BUCKETS)]
    return b, lows


_BUCKET_OF, _BUCKET_LOW = _bucket_lows()
FAR_DIST = _BUCKET_LOW[N_BUCKETS - 1]


def _resident(shape):
    nd = len(shape)
    return pl.BlockSpec(shape, lambda *_: (0,) * nd, pipeline_mode=pl.Buffered(1))


def _params(sem):
    return pltpu.CompilerParams(dimension_semantics=sem, vmem_limit_bytes=VMEM_LIMIT)


def _split2(x):
    hi = x.astype(BF16)
    lo = (x - hi.astype(F32)).astype(BF16)
    return hi, lo


def _dot_exact_rhs(x, rhs_bf16, terms=2):
    acc = None
    rem = x
    for _ in range(terms):
        part = rem.astype(BF16)
        d = jnp.dot(part, rhs_bf16, preferred_element_type=F32)
        acc = d if acc is None else acc + d
        rem = rem - part.astype(F32)
    return acc


def _dot_exact_lhs(lhs_bf16, x, terms=3):
    acc = None
    rem = x
    for _ in range(terms):
        part = rem.astype(BF16)
        d = jnp.dot(lhs_bf16, part, preferred_element_type=F32)
        acc = d if acc is None else acc + d
        rem = rem - part.astype(F32)
    return acc


def _layer_norm(y, g, b):
    mu = jnp.mean(y, axis=-1, keepdims=True)
    yc = y - mu
    var = jnp.mean(yc * yc, axis=-1, keepdims=True)
    return yc * lax.rsqrt(var + LN_EPS) * g + b


def _bias_chain(d, tab_rows):
    out = tab_rows[0] + jnp.zeros(d.shape, F32)
    for b in range(1, N_BUCKETS):
        out = jnp.where(d >= _BUCKET_LOW[b], tab_rows[b], out)
    return out


def _ada_kernel(c_ref, w_ref, b_ref, o_ref):
    c = c_ref[...]
    h = (c * jax.nn.sigmoid(c)).astype(BF16)
    o_ref[...] = jnp.dot(h, w_ref[...].astype(BF16), preferred_element_type=F32) + b_ref[...]


def _ada(c_all, w_ada, b_ada):
    rows, n = c_all.shape[0], w_ada.shape[1]
    tn = 1152
    return pl.pallas_call(
        _ada_kernel,
        grid=(n // tn,),
        in_specs=[pl.BlockSpec((rows, D_MODEL), lambda j: (0, 0)),
                  pl.BlockSpec((D_MODEL, tn), lambda j: (0, j)),
                  pl.BlockSpec((1, tn), lambda j: (0, j))],
        out_specs=pl.BlockSpec((rows, tn), lambda j: (0, j)),
        out_shape=jax.ShapeDtypeStruct((rows, n), F32),
        compiler_params=_params(("arbitrary",)),
        name="ada",
    )(c_all, w_ada, b_ada.reshape(1, n))


FF_CHUNKS = 2


def _ffn_kernel(x_ref, sh_ref, sc_ref, gt_ref, lng_ref, lnb_ref, wg_ref, wu_ref, wd_ref, o_ref):
    x = x_ref[...]
    h = (x * (1.0 + sc_ref[...]) + sh_ref[...]).astype(BF16)
    ck = D_FF // FF_CHUNKS
    acc = jnp.zeros(x.shape, F32)
    for c in range(FF_CHUNKS):
        a = jnp.dot(h, wg_ref[:, c * ck:(c + 1) * ck], preferred_element_type=F32)
        b = jnp.dot(h, wu_ref[:, c * ck:(c + 1) * ck], preferred_element_type=F32)
        t = (a * jax.nn.sigmoid(a) * b).astype(BF16)
        acc = acc + jnp.dot(t, wd_ref[c * ck:(c + 1) * ck, :], preferred_element_type=F32)
    y = ALPHA * x + (1.0 + gt_ref[...]) * (0.5 * acc)
    o_ref[...] = _layer_norm(y, lng_ref[...], lnb_ref[...])


def _mod_spec(mod, tm):
    if mod.shape[0] == 1:
        return pl.BlockSpec((1, D_MODEL), lambda i: (0, 0))
    return pl.BlockSpec((tm, D_MODEL), lambda i: (i, 0))


def _ffn(x, shift, scale, gate, ln_g, ln_b, wg, wu, wd):
    rows = x.shape[0]
    tm = min(512, rows)
    row = lambda i: (i, 0)
    return pl.pallas_call(
        _ffn_kernel,
        grid=(rows // tm,),
        in_specs=[pl.BlockSpec((tm, D_MODEL), row), _mod_spec(shift, tm), _mod_spec(scale, tm), _mod_spec(gate, tm),
                  _resident((1, D_MODEL)), _resident((1, D_MODEL)),
                  _resident((D_MODEL, D_FF)), _resident((D_MODEL, D_FF)), _resident((D_FF, D_MODEL))],
        out_specs=pl.BlockSpec((tm, D_MODEL), row),
        out_shape=jax.ShapeDtypeStruct((rows, D_MODEL), F32),
        compiler_params=_params(("arbitrary",)),
        name="ffn",
    )(x, shift, scale, gate, ln_g.reshape(1, -1), ln_b.reshape(1, -1), wg, wu, wd)


def _proj_kernel(x_ref, sh_ref, sc_ref, w_ref, q_ref, kv_ref, win_ref, g_ref, pr_ref):
    h = (x_ref[...] * (1.0 + sc_ref[...]) + sh_ref[...]).astype(BF16)
    p = jnp.dot(h, w_ref[...], preferred_element_type=F32)
    q_ref[...] = p[:, P_Q:P_KV]
    kv_ref[...] = p[:, P_KV:P_WIN]
    win_ref[...] = p[:, P_WIN:P_GATE]
    g_ref[...] = jax.nn.sigmoid(p[:, P_GATE:P_RW])
    pr_ref[...] = p[:, P_RW:P_COLS]


def _proj(x, shift, scale, w_in_p):
    rows = x.shape[0]
    tm = min(512, rows)
    row = lambda i: (i, 0)
    widths = (512, 512, 256, LANE, RW_PAD)
    return pl.pallas_call(
        _proj_kernel,
        grid=(rows // tm,),
        in_specs=[pl.BlockSpec((tm, D_MODEL), row), _mod_spec(shift, tm), _mod_spec(scale, tm),
                  _resident((D_MODEL, P_COLS))],
        out_specs=[pl.BlockSpec((tm, w), row) for w in widths],
        out_shape=[jax.ShapeDtypeStruct((rows, w), F32) for w in widths],
        compiler_params=_params(("arbitrary",)),
        name="proj",
    )(x, shift, scale, w_in_p)


def _prep_w_in(w_in):
    pad = lambda a, n: jnp.pad(a, ((0, 0), (0, n - a.shape[1])))
    nsa, rw = w_in[:, :NSA_COLS], w_in[:, NSA_COLS:]
    gl = nsa[:, 1280:1304].reshape(D_MODEL, H_NSA, 3).transpose(0, 2, 1).reshape(D_MODEL, 3 * H_NSA)
    cols = [nsa[:, :1280], pad(gl, LANE), _rwkv_cols(rw)]
    return jnp.concatenate(cols, axis=1).astype(BF16)


def _rwkv_cols(a):
    pad = lambda t: jnp.pad(t, [(0, 0)] * (t.ndim - 1) + [(0, LANE - t.shape[-1])])
    n = 3 * D_RWKV
    return jnp.concatenate([a[..., :n], pad(a[..., n:n + 32]), pad(a[..., n + 32:n + 64]), pad(a[..., n + 64:n + 160])], axis=-1)


def _rwkv_uncols(a):
    n = 3 * D_RWKV
    return jnp.concatenate([a[..., :n], a[..., n:n + 32], a[..., n + LANE:n + LANE + 32], a[..., n + 2 * LANE:n + 2 * LANE + 96]], axis=-1)


RW_GROUP = 64
RW_TOK_PAD = 16
RW_PAIRS = H_RWKV // 2
RW_BLOCK = 64


def _lora(x, w_ref):
    w = w_ref[...]
    w_hi = w.astype(BF16)
    w_lo = (w - w_hi.astype(F32)).astype(BF16)
    return _dot_exact_rhs(x, w_hi) + jnp.dot(x.astype(BF16), w_lo, preferred_element_type=F32)


def _rwkv_kernel(pr_ref, sh0_ref, s0_ref, mu_ref, w0_ref, a0_ref, kk_ref, ka_ref, rk_ref, gw_ref, gb_ref,
                 w2_ref, a2_ref, g2_ref, bo_ref, lgrp_ref, ggrp_ref, o_ref, sout_ref,
                 prev_scr, s_scr, yacc_scr, *, n_valid):
    tb = pr_ref.shape[0]
    step = pl.program_id(1)

    @pl.when(step == 0)
    def _():
        prev_scr[...] = sh0_ref[...]
        s_scr[...] = s0_ref[...]

    yacc_scr[...] = jnp.zeros(yacc_scr.shape, F32)
    p = pr_ref[...]
    rows = lax.broadcasted_iota(jnp.int32, (tb, 1), 0)
    prev = jnp.where(rows == 0, prev_scr[...], pltpu.roll(p, 1, axis=0))
    prev_scr[...] = p[tb - 1:tb, :]
    xs = p + (prev - p) * mu_ref[...]
    n = D_RWKV
    r, k, v = xs[:, :n], xs[:, n:2 * n], xs[:, 2 * n:3 * n]
    wl, al, gl = xs[:, 3 * n:3 * n + LANE], xs[:, 3 * n + LANE:3 * n + 2 * LANE], xs[:, 3 * n + 2 * LANE:]
    z = -(w0_ref[...] + _lora(jnp.tanh(wl), w2_ref))
    w = -(jnp.maximum(z, 0.0) + jnp.log(1.0 + jnp.exp(-jnp.abs(z)))) - 0.5
    a = jax.nn.sigmoid(a0_ref[...] + _lora(al, a2_ref))
    g = _lora(jax.nn.sigmoid(gl), g2_ref)
    kk = k * kk_ref[...]
    ss = _dot_exact_rhs(kk * kk, bo_ref[...])
    kk = kk / jnp.maximum(jnp.sqrt(ss), 1e-12)
    k2 = k * (1.0 + (a - 1.0) * ka_ref[...])
    G = min(RW_GROUP, tb)
    n_groups = tb // G
    log_dec = -jnp.exp(w)
    bet = kk * a
    if n_valid < tb:
        live = rows < n_valid
        log_dec, kk, bet, k2, v_in = (jnp.where(live, x, 0.0) for x in (log_dec, kk, bet, k2, v))
    else:
        v_in = v
    cum = _dot_exact_lhs(lgrp_ref[...], log_dec, terms=2)
    cum_end = _dot_exact_lhs(ggrp_ref[...], log_dec, terms=2)
    gam_inv = jnp.exp(-cum)
    gam_end = jnp.exp(cum_end - cum)
    k_hat = -kk * jnp.exp(cum - log_dec)
    r_hat = r * jnp.exp(cum)
    b_chk, k_chk = bet * gam_inv, k2 * gam_inv
    b_til, k_til = bet * gam_end, k2 * gam_end
    gam_group = jnp.exp(cum_end)

    lane = lax.broadcasted_iota(jnp.int32, (1, LANE), 1)
    low = lane < HD_RWKV
    lane_t = lane & (RW_BLOCK - 1)
    row = lax.broadcasted_iota(jnp.int32, (LANE, 1), 0)
    row_t = row & (RW_BLOCK - 1)
    same = ((row < HD_RWKV) == low) & ((row_t // G) == (lane_t // G))
    strict, incl = same & (lane_t < row_t), same & (lane_t <= row_t)
    bf = lambda x: x.astype(BF16)
    mm = lambda x, y: jnp.dot(bf(x), bf(y), preferred_element_type=F32)
    mm_nt = lambda x, y: lax.dot_general(bf(x), bf(y), (((1,), (1,)), ((), ())), preferred_element_type=F32)

    def rows_bd(x):
        if tb < RW_BLOCK:
            x = jnp.concatenate([x, jnp.zeros((RW_BLOCK - tb, LANE), F32)], axis=0)
        return jnp.concatenate([jnp.where(low, x, 0.0), jnp.where(low, 0.0, x)], axis=0)

    def mm3(x, y):
        xh, yh = bf(x), bf(y)
        xl, yl = bf(x - xh.astype(F32)), bf(y - yh.astype(F32))
        return jnp.dot(jnp.concatenate([xh, xl, xh], axis=1), jnp.concatenate([yh, yh, yl], axis=0), preferred_element_type=F32)

    pairs = range(RW_PAIRS)
    cols = [slice(pp * LANE, (pp + 1) * LANE) for pp in pairs]
    kh_row = [rows_bd(k_hat[:, c]) for c in cols]
    rh_row = [rows_bd(r_hat[:, c]) for c in cols]
    kh_mat = [x.T for x in kh_row]
    rh_mat = [x.T for x in rh_row]
    state_in = [jnp.concatenate([rows_bd(b_chk[:, c]).T, rows_bd(k_chk[:, c]).T], axis=1) for c in cols]
    upd_rows = [jnp.concatenate([rows_bd(b_til[:, c]), rows_bd(k_til[:, c])], axis=0) for c in cols]
    v_t = [rows_bd(v_in[:, c]).T for c in cols]
    v_t = [x[:HD_RWKV] + x[HD_RWKV:] for x in v_t]
    c_all = [mm(jnp.concatenate([kh_row[pp], rh_row[pp]], axis=0), state_in[pp]) for pp in pairs]
    c_uu = [jnp.where(strict, c[:LANE, :LANE], 0.0) for c in c_all]
    c_uv = [jnp.where(strict, c[:LANE, LANE:], 0.0) for c in c_all]
    c_ru = [jnp.where(incl, c[LANE:, :LANE], 0.0) for c in c_all]
    c_rv = [jnp.where(incl, c[LANE:, LANE:], 0.0) for c in c_all]
    t_neu, power = list(c_uu), list(c_uu)
    span = 2
    while span < G:
        power = [mm(x, x) for x in power]
        t_neu = [t_neu[pp] + power[pp] + mm(t_neu[pp], power[pp]) for pp in pairs]
        span *= 2
    from_v = [mm_nt(v_t[pp], c_uv[pp]) for pp in pairs]
    st = [s_scr[pp] for pp in pairs]
    y_t = [jnp.zeros((HD_RWKV, LANE), F32) for _ in pairs]
    for grp in range(n_groups):
        here = (lane_t // G) == grp
        w_t = [jnp.where(here, mm(st[pp], kh_mat[pp]) + from_v[pp], 0.0) for pp in pairs]
        u_t = [w_t[pp] + mm_nt(w_t[pp], t_neu[pp]) for pp in pairs]
        v_g = [jnp.where(here, x, 0.0) for x in v_t]
        y_t = [y_t[pp] + jnp.where(here, mm(st[pp], rh_mat[pp]), 0.0) + mm_nt(u_t[pp], c_ru[pp]) + mm_nt(v_g[pp], c_rv[pp])
               for pp in pairs]
        st = [st[pp] * gam_group[grp * G:grp * G + 1, cols[pp]] + mm3(jnp.concatenate([u_t[pp], v_g[pp]], axis=1), upd_rows[pp])
              for pp in pairs]
    for pp in pairs:
        s_scr[pp] = st[pp]
        yacc_scr[pp, :HD_RWKV, :] = y_t[pp]

    pieces = []
    lane_t = lax.broadcasted_iota(jnp.int32, (tb, LANE), 1)
    for pp in range(RW_PAIRS):
        yt = yacc_scr[pp].T
        pieces.append(jnp.where(lane_t < HD_RWKV, yt[:tb], pltpu.roll(yt[RW_BLOCK:RW_BLOCK + tb], HD_RWKV, axis=1)))
    y = jnp.concatenate(pieces, axis=1)
    mean = _dot_exact_rhs(y, bo_ref[...]) * (1.0 / HD_RWKV)
    yc = y - mean
    var = _dot_exact_rhs(yc * yc, bo_ref[...]) * (1.0 / HD_RWKV)
    yn = yc * lax.rsqrt(var + GN_EPS) * gw_ref[...] + gb_ref[...]
    bonus = _dot_exact_rhs(r * k2 * rk_ref[...], bo_ref[...]) * v
    o_ref[...] = (yn + bonus) * g
    sout_ref[...] = s_scr[...]


def _pair_state(s):
    B = s.shape[0]
    return s.reshape(B, RW_PAIRS, 2, HD_RWKV, HD_RWKV).transpose(0, 1, 3, 2, 4).reshape(B, RW_PAIRS, HD_RWKV, LANE)


def _unpair_state(s):
    B = s.shape[0]
    return s.reshape(B, RW_PAIRS, HD_RWKV, 2, HD_RWKV).transpose(0, 1, 3, 2, 4).reshape(B, H_RWKV, HD_RWKV, HD_RWKV)


def _rwkv(pr, shift0, s0, lw, n_valid):
    B, T, _ = pr.shape
    tb = min(RW_BLOCK, T)
    n = D_RWKV
    vec = lambda a: a.reshape(1, n)
    padrow = lambda a: jnp.pad(a, ((0, LANE - a.shape[0]), (0, 0)))
    blk = np.arange(n) // HD_RWKV
    block_ones = jnp.asarray(blk[:, None] == blk[None, :], BF16)
    tok = np.arange(tb)
    group = min(RW_GROUP, tb)
    same_group = tok[:, None] // group == tok[None, :] // group
    prefix = jnp.asarray(same_group & (tok[None, :] <= tok[:, None]), BF16)
    consts = [_rwkv_cols(lw['rwkv_mu']).reshape(1, RW_PAD), vec(lw['rwkv_w0']), vec(lw['rwkv_a0']), vec(lw['rwkv_k_k']),
              vec(lw['rwkv_k_a']), vec(lw['rwkv_r_k']), vec(lw['rwkv_gn_w']), vec(lw['rwkv_gn_b']),
              padrow(lw['rwkv_w2']), padrow(lw['rwkv_a2']), padrow(lw['rwkv_g2']), block_ones, prefix, jnp.asarray(same_group, BF16)]
    kern = functools.partial(_rwkv_kernel, n_valid=n_valid)
    state_spec = pl.BlockSpec((None, RW_PAIRS, HD_RWKV, LANE), lambda b, j: (b, 0, 0, 0))
    o, s = pl.pallas_call(
        kern,
        grid=(B, T // tb),
        in_specs=[pl.BlockSpec((None, tb, RW_PAD), lambda b, j: (b, j, 0)),
                  pl.BlockSpec((None, 1, RW_PAD), lambda b, j: (b, 0, 0)), state_spec]
                 + [_resident(c.shape) for c in consts],
        out_specs=[pl.BlockSpec((None, tb, n), lambda b, j: (b, j, 0)), state_spec],
        out_shape=[jax.ShapeDtypeStruct((B, T, n), F32), jax.ShapeDtypeStruct((B, RW_PAIRS, HD_RWKV, LANE), F32)],
        scratch_shapes=[pltpu.VMEM((1, RW_PAD), F32), pltpu.VMEM((RW_PAIRS, HD_RWKV, LANE), F32),
                        pltpu.VMEM((RW_PAIRS, LANE, LANE), F32)],
        compiler_params=_params(("arbitrary", "arbitrary")),
        name="rwkv",
    )(pr, shift0, _pair_state(s0), *consts)
    return o, _unpair_state(s)


CMP_PAGES = 32
CHUNKS_PER_PAGE = PAGE_SIZE // CMP_STRIDE
CMP_K = CMP_STRIDE * N_KV * HD


def _compress_kernel(pt_ref, *refs, n_pages, transposed):
    weights = refs[2 * n_pages + 2:2 * n_pages + 10]
    outs = refs[2 * n_pages + 10:2 * n_pages + 12]
    rows_scr = refs[2 * n_pages + 12]
    rows = CHUNKS_PER_PAGE * n_pages
    for kind in range(2):
        pages, nxt = refs[kind * n_pages:(kind + 1) * n_pages], refs[2 * n_pages + kind]
        pe_ref, w_ref, b_ref, w2_ref = weights[4 * kind:4 * kind + 4]
        for p, pg in enumerate(pages):
            rows_scr[p * PAGE_SIZE:(p + 1) * PAGE_SIZE, :] = pg[...].T if transposed else pg[...]
        nxt_rows = nxt[...].T[:CMP_STRIDE] if transposed else nxt[...]
        x = jnp.concatenate([rows_scr[pl.ds(s, rows, stride=CMP_STRIDE), :] for s in range(CMP_STRIDE)], axis=1)
        xn = jnp.concatenate([nxt_rows[s:s + 1, :] for s in range(CMP_STRIDE)], axis=1)
        x_ext = jnp.concatenate([x, jnp.broadcast_to(xn, (8, CMP_K))], axis=0)
        h_first = jnp.dot((x + pe_ref[0]).astype(BF16), w_ref[0], preferred_element_type=F32)
        h_second = jnp.dot((x_ext + pe_ref[1]).astype(BF16), w_ref[1], preferred_element_type=F32)
        h_next = pltpu.roll(h_second, rows + 8 - 1, axis=0)[:rows]
        hidden = jax.nn.gelu(h_first + h_next + b_ref[...])
        outs[kind][...] = jnp.dot(hidden.astype(BF16), w2_ref[...], preferred_element_type=F32)


def _compress_weights(pe, w1, b1, w2):
    eye = jnp.eye(N_KV, dtype=F32)
    halves = []
    for half in range(2):
        w = w1[half * CMP_STRIDE:(half + 1) * CMP_STRIDE]
        halves.append(jnp.einsum('sdn,hg->shdgn', w, eye).reshape(CMP_K, N_KV * CMP_HIDDEN))
    pe2 = jnp.stack([jnp.broadcast_to(pe[half * CMP_STRIDE:(half + 1) * CMP_STRIDE, None, :], (CMP_STRIDE, N_KV, HD)).reshape(1, CMP_K)
                     for half in range(2)])
    w2bd = jnp.einsum('nd,hg->hngd', w2, eye).reshape(N_KV * CMP_HIDDEN, N_KV * HD)
    return [pe2, jnp.stack(halves).astype(BF16), jnp.tile(b1, N_KV).reshape(1, -1), w2bd.astype(BF16)]


def _compress(pool, page_table, lw, transposed):
    B, n_pages_total = page_table.shape
    n_pages = min(CMP_PAGES, n_pages_total)
    rows = CHUNKS_PER_PAGE * n_pages
    weights = (_compress_weights(lw['cmp_pe_k'], lw['cmp_w1_k'], lw['cmp_b1_k'], lw['cmp_w2_k'])
               + _compress_weights(lw['cmp_pe_v'], lw['cmp_w1_v'], lw['cmp_b1_v'], lw['cmp_w2_v']))
    width = N_KV * HD
    at = (lambda page, kind: (page, kind, 0)) if transposed else (lambda page, kind: (page, 0, kind))

    def page_map(p, kind):
        return lambda b, j, pt: at(pt[b, j * n_pages + p], kind)

    def next_map(kind):
        return lambda b, j, pt: at(pt[b, jnp.minimum((j + 1) * n_pages, n_pages_total - 1)], kind)

    const = lambda a: pl.BlockSpec(a.shape, lambda b, j, pt: (0,) * a.ndim)
    out_spec = pl.BlockSpec((None, rows, N_KV * HD), lambda b, j, pt: (b, j, 0))
    out_shape = jax.ShapeDtypeStruct((B, n_pages_total * CHUNKS_PER_PAGE, N_KV * HD), F32)
    next_rows = PAGE_SIZE if transposed else CMP_STRIDE
    grid_spec = pltpu.PrefetchScalarGridSpec(
        num_scalar_prefetch=1,
        grid=(B, n_pages_total // n_pages),
        in_specs=[pl.BlockSpec((None, PAGE_SIZE, width), page_map(p, kind)) for kind in range(2) for p in range(n_pages)]
                 + [pl.BlockSpec((None, next_rows, width), next_map(kind)) for kind in range(2)] + [const(a) for a in weights],
        out_specs=[out_spec, out_spec],
        scratch_shapes=[pltpu.VMEM((n_pages * PAGE_SIZE, width), F32)],
    )
    return pl.pallas_call(
        functools.partial(_compress_kernel, n_pages=n_pages, transposed=transposed),
        grid_spec=grid_spec,
        out_shape=[out_shape, out_shape],
        compiler_params=_params(("arbitrary", "arbitrary")),
        name="compress",
    )(page_table, *([pool] * (2 * n_pages + 2)), *weights)


BAND_ROWS = 1152


def _band_kernel(tab_ref, bkt_ref, o_ref):
    h = pl.program_id(0)
    bkt = bkt_ref[...]
    out = jnp.full(bkt.shape, NEG, F32)
    for b in range(N_BUCKETS):
        out = jnp.where(bkt == b, tab_ref[b, h], out)
    o_ref[...] = out


def _band(rel_table):
    u = np.arange(BAND_ROWS)[:, None]
    qi = np.arange(Q_BLOCK)[None, :]
    d = qi + WINDOW - u
    bkt = np.where(d >= 0, _BUCKET_OF[np.clip(d, 0, len(_BUCKET_OF) - 1)], -1).astype(np.int32)
    return pl.pallas_call(
        _band_kernel,
        grid=(H_NSA,),
        in_specs=[pl.BlockSpec(memory_space=pltpu.SMEM), pl.BlockSpec((BAND_ROWS, Q_BLOCK), lambda h: (0, 0))],
        out_specs=pl.BlockSpec((None, BAND_ROWS, Q_BLOCK), lambda h: (h, 0, 0)),
        out_shape=jax.ShapeDtypeStruct((H_NSA, BAND_ROWS, Q_BLOCK), F32),
        compiler_params=_params(("arbitrary",)),
        name="band",
    )(rel_table, jnp.asarray(bkt))


def _softmax_cols(s):
    m = jnp.max(s, axis=0, keepdims=True)
    e = jnp.exp(s - m)
    l = jnp.sum(e, axis=0, keepdims=True)
    return e * jnp.where(m > 0.5 * NEG, 1.0 / l, 0.0)


def _select_blocks(impsel, qpos, n_pick):
    ns = impsel.shape[0]
    blk = lax.broadcasted_iota(jnp.int32, impsel.shape, 0)
    cur = jnp.right_shift(qpos, 6)
    future = blk * SEL_BLOCK > qpos
    forced = (blk == 0) | (blk == cur) | (blk == cur - 1)
    score = jnp.where(future, -jnp.inf, jnp.where(forced, jnp.inf, impsel))
    chosen = jnp.zeros(impsel.shape, F32)
    for _ in range(n_pick):
        best = jnp.max(score, axis=0, keepdims=True)
        first = jnp.min(jnp.where(score == best, blk, ns), axis=0, keepdims=True)
        hit = (blk == first) & (best > -jnp.inf)
        chosen = jnp.where(hit, 1.0, chosen)
        score = jnp.where(hit, -jnp.inf, score)
    return jnp.where(chosen > 0.0, 0.0, NEG)


def _pool_matrix(ns, nc):
    j = np.arange(ns)[:, None]
    n = np.arange(nc)[None, :]
    ratio = SEL_BLOCK // CMP_STRIDE
    return jnp.asarray((n >= ratio * j - 1) & (n <= ratio * j + ratio - 1), BF16)


def _nsa_prompt_kernel(tab_ref, q_ref, g_ref, kc_ref, vct_ref, ks_ref, vst_ref, kw_ref, vwt_ref, band_ref, pool_ref, o_ref,
                       rhs_scr, mask_scr, acc_scr, m_scr, sc_scr, sa_scr, sb_scr):
    i = pl.program_id(0)
    ncp = kc_ref.shape[0]
    ns = pool_ref.shape[0]
    s0 = i * Q_BLOCK
    q_t = (q_ref[...] * HD ** -0.5).T
    g_t = g_ref[...].T
    lane_q = lax.broadcasted_iota(jnp.int32, (1, G_NSA * Q_BLOCK), 1) & (Q_BLOCK - 1)
    qpos = s0 + lax.broadcasted_iota(jnp.int32, (1, Q_BLOCK), 1)
    rhs_scr[...] = jnp.zeros(rhs_scr.shape, BF16)
    heads_out = []

    def compressed(k):
        lanes4 = lambda f: jnp.concatenate([f(G_NSA * k + g) for g in range(G_NSA)], axis=1)
        qcols = lanes4(lambda h: q_t[h * HD:(h + 1) * HD, :])
        zero = jnp.zeros_like(qcols)
        top = jnp.concatenate([qcols, zero] if k == 0 else [zero, qcols], axis=0).astype(BF16)
        far_row = lanes4(lambda h: band_ref[h, 0:1, :])
        n0 = pl.multiple_of(jnp.clip(8 * i - 16, 0, ncp - 32), 8)
        nrow = lax.broadcasted_iota(jnp.int32, (ncp, 1), 0)
        sc_scr[...] = jnp.dot(kc_ref[...], top, preferred_element_type=F32) + jnp.where(nrow < n0, far_row, NEG)
        d_edge = qpos - (CMP_STRIDE * (n0 + lax.broadcasted_iota(jnp.int32, (32, 1), 0)) + CMP_BLOCK - 1)
        edge_bias = lanes4(lambda h: jnp.where(d_edge >= 0, _bias_chain(d_edge, [tab_ref[b, h] for b in range(N_BUCKETS)]), NEG))
        sc_scr[pl.ds(n0, 32), :] = jnp.dot(kc_ref[pl.ds(n0, 32), :], top, preferred_element_type=F32) + edge_bias
        p_c = _softmax_cols(sc_scr[...])
        o_c = jnp.dot(vct_ref[...], p_c.astype(BF16), preferred_element_type=F32)[k * HD:(k + 1) * HD]
        imp = p_c[:, 0:Q_BLOCK]
        for g in range(1, G_NSA):
            imp = imp + p_c[:, g * Q_BLOCK:(g + 1) * Q_BLOCK]
        return qcols, top, far_row, o_c, _dot_exact_lhs(pool_ref[...], imp, terms=2)

    branches = [compressed(k) for k in range(N_KV)]
    masks = [_select_blocks(b[4], qpos, min(N_SEL, ns)).astype(BF16) for b in branches]
    for k in range(N_KV):
        heads = [G_NSA * k + g for g in range(G_NSA)]
        lanes4 = lambda f: jnp.concatenate([f(h) for h in heads], axis=1)
        qcols, top, far_row, o_c, _ = branches[k]
        mask_scr[...] = jnp.concatenate([masks[k]] * G_NSA, axis=1)

        m_scr[...] = jnp.full(m_scr.shape, M_INIT, F32)
        acc_scr[...] = jnp.zeros(acc_scr.shape, F32)
        kd = i // 4
        r = i % 4

        far_hi = far_row.astype(BF16).astype(F32)
        rhs_scr[0:HD, :] = qcols.astype(BF16)
        rhs_scr[SEL_FAR0:SEL_FAR0 + MASK_ROWS, :] = jnp.concatenate(
            [far_hi, far_row - far_hi, jnp.zeros((MASK_ROWS - 2, G_NSA * Q_BLOCK), F32)], axis=0).astype(BF16)

        def scores(slab, kts, extra):
            rhs_scr[SEL_MASK0:SEL_MASK0 + MASK_ROWS, :] = mask_scr[pl.ds(pl.multiple_of(slab * MASK_ROWS, MASK_ROWS), MASK_ROWS), :]
            rhs = rhs_scr[...]
            out = []
            for kt, add in zip(kts, extra):
                s = jnp.dot(ks_ref[k, pl.ds(pl.multiple_of(kt * KEY_TILE, KEY_TILE), KEY_TILE), :], rhs, preferred_element_type=F32)
                out.append(s if add is None else s + add)
            return out

        def update(kts, tiles):
            m_old = m_scr[...]
            m_new = m_old
            for s in tiles:
                m_new = jnp.maximum(m_new, jnp.max(s, axis=0, keepdims=True))
            acc = jnp.exp(m_old - m_new) * acc_scr[...]
            for kt, s in zip(kts, tiles):
                acc = acc + jnp.dot(vst_ref[k, kt], jnp.exp(s - m_new).astype(BF16), preferred_element_type=F32)
            acc_scr[...] = acc
            m_scr[...] = m_new

        def attend(slab, kts, extra):
            update(kts, scores(slab, kts, extra))

        near_at = lambda start: lanes4(lambda h: band_ref[h, pl.ds(pl.multiple_of(start, Q_BLOCK), KEY_TILE), :]) - far_row
        prev_near = (r == 0) & (kd >= 1)
        kd_odd = (kd & 1) == 1
        even_prev = jnp.logical_not(kd_odd) & prev_near
        n_pairs = kd // 2 - even_prev.astype(jnp.int32)
        n_quads = n_pairs // 2

        def pair_scores(dst, pair):
            lo, hi = scores(pair, [2 * pair, 2 * pair + 1], [None, None])
            dst[0:KEY_TILE, :] = lo
            dst[KEY_TILE:, :] = hi

        def pair_update(src, pair):
            update([2 * pair, 2 * pair + 1], [src[0:KEY_TILE, :], src[KEY_TILE:, :]])

        @pl.when(n_quads > 0)
        def _():
            pair_scores(sa_scr, 0)

        def quad_body(qd, carry):
            first = 2 * qd
            pair_scores(sb_scr, first + 1)
            pair_update(sa_scr, first)
            pair_scores(sa_scr, jnp.minimum(first + 2, 2 * n_quads - 2))
            pair_update(sb_scr, first + 1)
            return carry

        lax.fori_loop(0, n_quads, quad_body, 0)

        @pl.when((n_pairs & 1) == 1)
        def _():
            attend(n_pairs - 1, [2 * n_pairs - 2, 2 * n_pairs - 1], [None, None])

        @pl.when(kd_odd)
        def _():
            attend(kd // 2, [kd - 1, kd], [jnp.where(prev_near, near_at(0), 0.0), near_at(KEY_TILE - Q_BLOCK * r)])

        @pl.when(even_prev)
        def _():
            attend(kd // 2 - 1, [kd - 2, kd - 1], [None, near_at(0)])

        @pl.when(jnp.logical_not(kd_odd))
        def _():
            attend(kd // 2, [kd], [near_at(KEY_TILE - Q_BLOCK * r)])

        acc = acc_scr[...]
        o_s = acc[0:HD] / acc[HD:HD + 1]

        ws = pl.multiple_of(jnp.maximum(s0 - WINDOW, 0), Q_BLOCK)
        u0 = pl.multiple_of(WINDOW - (s0 - ws), Q_BLOCK)
        n_win = WINDOW + Q_BLOCK
        u = u0 + lax.broadcasted_iota(jnp.int32, (n_win, 1), 0)
        s_w = (jnp.dot(kw_ref[pl.ds(ws, n_win), :], top, preferred_element_type=F32)
               + lanes4(lambda h: band_ref[h, pl.ds(u0, n_win), :]) + jnp.where(u > lane_q, 0.0, NEG))
        m_w = jnp.max(s_w, axis=0, keepdims=True)
        p_w = jnp.exp(s_w - m_w).astype(BF16)
        acc_w = jnp.zeros((V_ROWS, G_NSA * Q_BLOCK), F32)
        for j in range(n_win // Q_BLOCK):
            acc_w = acc_w + jnp.dot(vwt_ref[ws // Q_BLOCK + j], p_w[j * Q_BLOCK:(j + 1) * Q_BLOCK], preferred_element_type=F32)
        o_w = acc_w[k * HD:(k + 1) * HD] / acc_w[N_KV * HD:N_KV * HD + 1]

        for g, h in enumerate(heads):
            cols = slice(g * Q_BLOCK, (g + 1) * Q_BLOCK)
            heads_out.append(o_c[:, cols] * g_t[h:h + 1] + o_s[:, cols] * g_t[H_NSA + h:H_NSA + h + 1]
                             + o_w[:, cols] * g_t[2 * H_NSA + h:2 * H_NSA + h + 1])
    o_ref[...] = jnp.concatenate(heads_out, axis=0).T


def _sel_pattern(rows, width):
    key = np.arange(rows)[:, None]
    b = np.arange(width)[None, :]
    ones = (b >= MASK_ROWS) & (b < MASK_ROWS + 2)
    return jnp.asarray(((key // SEL_BLOCK) % MASK_ROWS == b) | ones, BF16)


def _values_t(v, tile):
    T, n = v.shape
    rows = n + MASK_ROWS
    vt = jnp.concatenate([v.T, jnp.ones((1, T), F32), jnp.zeros((rows - n - 1, T), F32)], axis=0)
    return vt.reshape(rows, T // tile, tile).transpose(1, 0, 2).astype(BF16)


def _nsa_prompt(q, gates, kv, win, kc, vc, band, rel_table):
    T = q.shape[0]
    ncp, ns = kc.shape[0], T // SEL_BLOCK
    k_sel = lambda h: kv[:, 256 + h * HD:256 + (h + 1) * HD]
    v_sel = lambda h: kv[:, 384 + h * HD:384 + (h + 1) * HD]
    pattern = _sel_pattern(T, LANE - HD)
    ks_aug = jnp.stack([jnp.concatenate([k_sel(h).astype(BF16), pattern], axis=1) for h in range(N_KV)])
    operands = [q, gates, kc.astype(BF16), vc.T.astype(BF16), ks_aug, jnp.stack([_values_t(v_sel(h), KEY_TILE) for h in range(N_KV)]),
                win[:, 0:128].astype(BF16), _values_t(win[:, 128:256], Q_BLOCK), band, _pool_matrix(ns, ncp)]
    blk = lambda w: pl.BlockSpec((Q_BLOCK, w), lambda i: (i, 0))
    return pl.pallas_call(
        _nsa_prompt_kernel,
        grid=(T // Q_BLOCK,),
        in_specs=[pl.BlockSpec(memory_space=pltpu.SMEM), blk(H_NSA * HD), blk(LANE)] + [_resident(a.shape) for a in operands[2:]],
        out_specs=blk(H_NSA * HD),
        out_shape=jax.ShapeDtypeStruct((T, H_NSA * HD), F32),
        scratch_shapes=[pltpu.VMEM((LANE, G_NSA * Q_BLOCK), BF16), pltpu.VMEM((ns, G_NSA * Q_BLOCK), BF16),
                        pltpu.VMEM((V_ROWS_KV, G_NSA * Q_BLOCK), F32), pltpu.VMEM((1, G_NSA * Q_BLOCK), F32),
                        pltpu.VMEM((ncp, G_NSA * Q_BLOCK), F32)] + [pltpu.VMEM((2 * KEY_TILE, G_NSA * Q_BLOCK), F32)] * 2,
        compiler_params=_params(("arbitrary",)),
        name="nsa_prompt",
    )(rel_table, *operands)


SMP_PAGES = 8
TOK_PAD = 8
SMP_COLS = H_NSA * TOK_PAD


def _nsa_sample_kernel(pt_ref, *refs, n_pages, n_valid, past):
    pages = refs[:n_pages]
    (q_ref, g_ref, kc_ref, vc_ref, kvn_ref, win_ref, winn_ref, tab_ref, pool_ref, gsum_ref, epat_ref, o_ref,
     top_scr, mask_scr, acc_scr, m_scr, l_scr, oc_scr, ow_scr) = refs[n_pages:]
    j = pl.program_id(1)
    ncp, wbuf = kc_ref.shape[0], win_ref.shape[1]
    tile_keys = n_pages * PAGE_SIZE
    lane = lax.broadcasted_iota(jnp.int32, (1, LANE), 1)
    tok = lane & (TOK_PAD - 1)
    second_kv = lane >= G_NSA * TOK_PAD
    tab = [tab_ref[b:b + 1, :] for b in range(N_BUCKETS)]
    far_row = tab[N_BUCKETS - 1]
    own_rows = lambda x: jnp.where(second_kv, x[HD:2 * HD], x[0:HD])
    pad_rows = lambda x: jnp.concatenate([x, jnp.zeros((LANE - x.shape[0], x.shape[1]), x.dtype)], axis=0)
    trow = lax.broadcasted_iota(jnp.int32, (LANE, 1), 0)
    d_new = tok - trow
    new_bias = jnp.where((d_new >= 0) & (trow < n_valid), _bias_chain(jnp.maximum(d_new, 0), tab), NEG)

    def attend_update(s, values_t):
        m_old = m_scr[...]
        m_new = jnp.maximum(m_old, jnp.max(s, axis=0, keepdims=True))
        alpha = jnp.exp(m_old - m_new)
        p = jnp.exp(s - m_new)
        l_scr[...] = alpha * l_scr[...] + jnp.sum(p, axis=0, keepdims=True)
        acc_scr[...] = alpha * acc_scr[...] + jnp.dot(values_t.astype(BF16), p.astype(BF16), preferred_element_type=F32)
        m_scr[...] = m_new

    def reset():
        m_scr[...] = jnp.full(m_scr.shape, M_INIT, F32)
        l_scr[...] = jnp.zeros(l_scr.shape, F32)
        acc_scr[...] = jnp.zeros(acc_scr.shape, F32)

    @pl.when(j == 0)
    def _():
        q_t = pad_rows(q_ref[...] * HD ** -0.5).T
        halves = []
        for k in range(N_KV):
            part = jnp.zeros((HD, LANE), F32)
            for g in range(G_NSA):
                h = G_NSA * k + g
                piece = q_t[h * HD:(h + 1) * HD, :]
                part = part + (pltpu.roll(piece, TOK_PAD * h, axis=1) if h else piece)
            halves.append(part)
        top = jnp.concatenate(halves, axis=0).astype(BF16)
        top_scr[...] = top
        qpos = past + tok

        n0 = ncp - 32
        kcb = kc_ref[...].astype(BF16)
        d_edge = qpos - (CMP_STRIDE * (n0 + lax.broadcasted_iota(jnp.int32, (32, 1), 0)) + CMP_BLOCK - 1)
        s_c = jnp.concatenate([
            jnp.dot(kcb[:n0], top, preferred_element_type=F32) + far_row,
            jnp.dot(kcb[n0:], top, preferred_element_type=F32) + jnp.where(d_edge >= 0, _bias_chain(jnp.maximum(d_edge, 0), tab), NEG)], axis=0)
        p_c = _softmax_cols(s_c)
        oc_scr[...] = own_rows(jnp.dot(vc_ref[...].T.astype(BF16), p_c.astype(BF16), preferred_element_type=F32))
        imp = _dot_exact_rhs(p_c, gsum_ref[...], terms=3)
        mask_scr[...] = _select_blocks(_dot_exact_lhs(pool_ref[...], imp), qpos, N_SEL)

        wk = win_ref[0:LANE, :].T
        d_w = wbuf + tok - lax.broadcasted_iota(jnp.int32, (wbuf, 1), 0)
        near = wbuf - LANE
        s_w = jnp.dot(wk.astype(BF16), top, preferred_element_type=F32)
        s_w = (jnp.concatenate([s_w[:near] + far_row, s_w[near:] + _bias_chain(d_w[near:], tab)], axis=0)
               + jnp.where(d_w < WINDOW, 0.0, NEG))
        reset()
        attend_update(s_w, win_ref[LANE:, :])
        wn = pad_rows(winn_ref[...])
        attend_update(jnp.dot(wn[:, :LANE].astype(BF16), top, preferred_element_type=F32) + new_bias, wn[:, LANE:].T)
        ow_scr[...] = own_rows(acc_scr[...]) / l_scr[...]
        reset()

    k_tile = jnp.concatenate([pg[0:LANE, :].T for pg in pages], axis=0)
    vt_tile = jnp.concatenate([pg[LANE:, :] for pg in pages], axis=1)
    slab = mask_scr[pl.ds(pl.multiple_of(j * MASK_ROWS, MASK_ROWS), MASK_ROWS), :].astype(BF16)
    rhs = jnp.concatenate([top_scr[...], slab, jnp.zeros((LANE - MASK_ROWS, LANE), BF16)], axis=0)
    s = jnp.dot(jnp.concatenate([k_tile.astype(BF16), epat_ref[...]], axis=1), rhs, preferred_element_type=F32)
    near = tile_keys - LANE
    d_near = past + tok - (j * tile_keys + near + lax.broadcasted_iota(jnp.int32, (LANE, 1), 0))
    s = jnp.concatenate([s[:near] + far_row, s[near:] + _bias_chain(d_near, tab)], axis=0)
    attend_update(s, vt_tile)

    @pl.when(j == pl.num_programs(1) - 1)
    def _():
        kn = pad_rows(kvn_ref[...])
        last_blk = past // SEL_BLOCK
        s_n = (jnp.dot(kn[:, 2 * LANE:3 * LANE].astype(BF16), top_scr[...], preferred_element_type=F32)
               + new_bias + mask_scr[last_blk:last_blk + 1, :])
        attend_update(s_n, kn[:, 3 * LANE:].T)
        o_s = own_rows(acc_scr[...]) / l_scr[...]
        g_t = pad_rows(g_ref[...]).T
        gate_rows = []
        for b in range(3):
            row = g_t[b * H_NSA:b * H_NSA + 1]
            for h in range(1, H_NSA):
                row = row + pltpu.roll(g_t[b * H_NSA + h:b * H_NSA + h + 1], TOK_PAD * h, axis=1)
            gate_rows.append(row)
        o_col = oc_scr[...] * gate_rows[0] + o_s * gate_rows[1] + ow_scr[...] * gate_rows[2]
        per_head = [o_col if h == 0 else pltpu.roll(o_col, LANE - TOK_PAD * h, axis=1) for h in range(H_NSA)]
        o_ref[...] = jnp.concatenate(per_head, axis=0).T[:TOK_PAD]


def _nsa_sample(pool, page_table, q, gates, kc, vc, kv_new, win_buf, win_new, rel_table, n_valid):
    B, n_pages_total = page_table.shape
    past = n_pages_total * PAGE_SIZE
    ncp = kc.shape[1]
    ns = past // SEL_BLOCK + 1
    nsp = -(-ns // MASK_ROWS) * MASK_ROWS
    col = np.arange(LANE)
    used = col < SMP_COLS
    gsum = jnp.asarray(((col[:, None] // (G_NSA * TOK_PAD) == col[None, :] // (G_NSA * TOK_PAD))
                        & (col[:, None] % TOK_PAD == col[None, :] % TOK_PAD) & used[:, None] & used[None, :]), BF16)
    tab_cols = jnp.pad(jnp.repeat(rel_table, TOK_PAD, axis=1), ((0, 0), (0, LANE - SMP_COLS)))
    consts = [tab_cols, _pool_matrix(nsp, ncp), gsum, _sel_pattern(SMP_PAGES * PAGE_SIZE, LANE)]
    per_seq = [q, gates, kc, vc, kv_new, win_buf, win_new]
    seq_spec = lambda a: pl.BlockSpec((None,) + a.shape[1:], lambda b, j, pt: (b,) + (0,) * (a.ndim - 1))
    const = lambda a: pl.BlockSpec(a.shape, lambda b, j, pt: (0,) * a.ndim)

    def page_map(p):
        return lambda b, j, pt: (pt[b, j * SMP_PAGES + p], 1, 0)

    grid_spec = pltpu.PrefetchScalarGridSpec(
        num_scalar_prefetch=1,
        grid=(B, n_pages_total // SMP_PAGES),
        in_specs=[pl.BlockSpec((None, 2 * LANE, PAGE_SIZE), page_map(p)) for p in range(SMP_PAGES)]
                 + [seq_spec(a) for a in per_seq] + [const(a) for a in consts],
        out_specs=pl.BlockSpec((None, TOK_PAD, H_NSA * HD), lambda b, j, pt: (b, 0, 0)),
        scratch_shapes=[pltpu.VMEM((LANE, LANE), BF16), pltpu.VMEM((nsp, LANE), F32), pltpu.VMEM((LANE, LANE), F32),
                        pltpu.VMEM((1, LANE), F32), pltpu.VMEM((1, LANE), F32), pltpu.VMEM((HD, LANE), F32), pltpu.VMEM((HD, LANE), F32)],
    )
    return pl.pallas_call(
        functools.partial(_nsa_sample_kernel, n_pages=SMP_PAGES, n_valid=n_valid, past=past),
        grid_spec=grid_spec,
        out_shape=jax.ShapeDtypeStruct((B, TOK_PAD, H_NSA * HD), F32),
        compiler_params=_params(("arbitrary", "arbitrary")),
        name="nsa_sample",
    )(page_table, *([pool] * SMP_PAGES), *per_seq, *consts)


def _outproj_kernel(x_ref, nsa_ref, rw_ref, gt_ref, lng_ref, lnb_ref, w_ref, o_ref):
    half = H_NSA * HD
    out = (jnp.dot(nsa_ref[...].astype(BF16), w_ref[0:half, :], preferred_element_type=F32)
           + jnp.dot(rw_ref[...].astype(BF16), w_ref[half:, :], preferred_element_type=F32))
    y = ALPHA * x_ref[...] + (1.0 + gt_ref[...]) * out
    o_ref[...] = _layer_norm(y, lng_ref[...], lnb_ref[...])


def _outproj(x, o_nsa, o_rwkv, gate, ln_g, ln_b, w_out):
    rows = x.shape[0]
    tm = min(512, rows)
    row = lambda i: (i, 0)
    return pl.pallas_call(
        _outproj_kernel,
        grid=(rows // tm,),
        in_specs=[pl.BlockSpec((tm, D_MODEL), row), pl.BlockSpec((tm, H_NSA * HD), row), pl.BlockSpec((tm, D_RWKV), row),
                  _mod_spec(gate, tm), _resident((1, D_MODEL)), _resident((1, D_MODEL)), _resident(w_out.shape)],
        out_specs=pl.BlockSpec((tm, D_MODEL), row),
        out_shape=jax.ShapeDtypeStruct((rows, D_MODEL), F32),
        compiler_params=_params(("arbitrary",)),
        name="outproj",
    )(x, o_nsa, o_rwkv, gate, ln_g.reshape(1, -1), ln_b.reshape(1, -1), w_out)


def kernel(x_prompt, x_sample, cache_nsa_kv, cache_nsa_win, state_rwkv_shift, state_rwkv_wkv, page_table, c_prompt, c_sample, rel_table, w_ada, b_ada, ln_g, ln_b, ffn1_gate, ffn1_up, ffn1_down, ffn2_gate, ffn2_up, ffn2_down, w_in, w_out, cmp_pe_k, cmp_w1_k, cmp_b1_k, cmp_w2_k, cmp_pe_v, cmp_w1_v, cmp_b1_v, cmp_w2_v, rwkv_mu, rwkv_w0, rwkv_w2, rwkv_a0, rwkv_a2, rwkv_g2, rwkv_k_k, rwkv_k_a, rwkv_r_k, rwkv_gn_w, rwkv_gn_b):
    assert w_ada.shape[0] == DEPTH == 1 and x_prompt.shape[0] == 1
    l = 0
    lw = dict(cmp_pe_k=cmp_pe_k[l], cmp_w1_k=cmp_w1_k[l], cmp_b1_k=cmp_b1_k[l], cmp_w2_k=cmp_w2_k[l],
              cmp_pe_v=cmp_pe_v[l], cmp_w1_v=cmp_w1_v[l], cmp_b1_v=cmp_b1_v[l], cmp_w2_v=cmp_w2_v[l],
              rwkv_mu=rwkv_mu[l], rwkv_w0=rwkv_w0[l], rwkv_w2=rwkv_w2[l], rwkv_a0=rwkv_a0[l], rwkv_a2=rwkv_a2[l], rwkv_g2=rwkv_g2[l],
              rwkv_k_k=rwkv_k_k[l], rwkv_k_a=rwkv_k_a[l], rwkv_r_k=rwkv_r_k[l], rwkv_gn_w=rwkv_gn_w[l], rwkv_gn_b=rwkv_gn_b[l])
    T = x_prompt.shape[1]
    nb, nt = x_sample.shape[0], x_sample.shape[1]
    assert nt <= TOK_PAD
    n_seq = 1 + nb
    c_all = jnp.concatenate([c_prompt, c_sample, jnp.zeros((-n_seq % 8, D_MODEL), F32)], axis=0)
    mod = _ada(c_all, w_ada[l], b_ada[l])
    mod_p = mod[0:1].reshape(9, 1, D_MODEL)
    mod_s = jnp.repeat(mod[1:n_seq].reshape(nb, 9, D_MODEL), nt, axis=0).transpose(1, 0, 2)
    ffn1 = [w[l].astype(BF16) for w in (ffn1_gate, ffn1_up, ffn1_down)]
    ffn2 = [w[l].astype(BF16) for w in (ffn2_gate, ffn2_up, ffn2_down)]
    w_in_p = _prep_w_in(w_in[l])
    w_out_b = w_out[l].astype(BF16)

    def trunk_in(x, m):
        x1 = _ffn(x, m[0], m[1], m[2], ln_g[l, 0], ln_b[l, 0], *ffn1)
        return x1, _proj(x1, m[3], m[4], w_in_p)

    def trunk_out(x1, o_nsa, o_rwkv, m):
        x2 = _outproj(x1, o_nsa, o_rwkv, m[5], ln_g[l, 1], ln_b[l, 1], w_out_b)
        return _ffn(x2, m[6], m[7], m[8], ln_g[l, 2], ln_b[l, 2], *ffn2)

    xp1, (q, kv, win, gates, pr) = trunk_in(x_prompt[0], mod_p)
    o_rw, wkv_p = _rwkv(pr[None], jnp.zeros((1, 1, RW_PAD), F32), jnp.zeros((1, H_RWKV, HD_RWKV, HD_RWKV), F32), lw, min(RW_BLOCK, T))
    n_rows = T // PAGE_SIZE
    kc, vc = _compress(kv.reshape(n_rows, PAGE_SIZE, 4 * LANE), jnp.arange(n_rows, dtype=jnp.int32)[None], lw, transposed=False)
    o_nsa = _nsa_prompt(q, gates, kv, win, kc[0], vc[0], _band(rel_table), rel_table)
    y_prompt = trunk_out(xp1, o_nsa, o_rw[0], mod_p)
    kv_prompt = kv.reshape(1, 1, T, 4, N_KV, HD)
    win_prompt = win[T - min(WINDOW, T):].reshape(1, 1, -1, 2, N_KV, HD)
    shift_prompt = _rwkv_uncols(pr[T - 1]).reshape(1, 1, RWKV_COLS)

    xs1, (q_s, kv_s, win_s, gates_s, pr_s) = trunk_in(x_sample.reshape(nb * nt, D_MODEL), mod_s)
    tokens = lambda a: jnp.pad(a.reshape(nb, nt, -1), ((0, 0), (0, TOK_PAD - nt), (0, 0)))
    pr_pad = jnp.pad(pr_s.reshape(nb, nt, -1), ((0, 0), (0, RW_TOK_PAD - nt), (0, 0)))
    o_rw_s, wkv_s = _rwkv(pr_pad, _rwkv_cols(state_rwkv_shift[l])[:, None], state_rwkv_wkv[l], lw, nt)
    pool_t = jnp.transpose(cache_nsa_kv[l], (0, 2, 3, 4, 1)).reshape(-1, 4 * LANE, PAGE_SIZE)
    kc_s, vc_s = _compress(pool_t, page_table, lw, transposed=True)
    win_buf = cache_nsa_win[l]
    win_t = jnp.transpose(win_buf, (0, 2, 3, 4, 1)).reshape(nb, 2 * LANE, -1)
    o_nsa_s = _nsa_sample(pool_t, page_table, tokens(q_s), tokens(gates_s), kc_s, vc_s, tokens(kv_s),
                          win_t, tokens(win_s), rel_table, nt)
    y_sample = trunk_out(xs1, o_nsa_s[:, :nt].reshape(nb * nt, -1), o_rw_s[:, :nt].reshape(nb * nt, -1), mod_s)
    kv_sample = kv_s.reshape(1, nb, nt, 4, N_KV, HD)
    win_sample = jnp.concatenate([win_buf, win_s.reshape(nb, nt, 2, N_KV, HD)], axis=1)[None, :, nt:]
    shift_sample = _rwkv_uncols(pr_s.reshape(nb, nt, -1)[:, -1])[None]
    return (y_prompt[None], y_sample.reshape(nb, nt, D_MODEL), kv_prompt, win_prompt, shift_prompt, wkv_p[None],
            kv_sample, win_sample, shift_sample, wkv_s[None])
```

```python
import functools
import math

import numpy as np
import jax
import jax.numpy as jnp
from jax import lax
from jax.experimental import pallas as pl
from jax.experimental.pallas import tpu as pltpu

D_MODEL = 1024
PAGE_SIZE = 128
H_NSA = 8
N_KV = 2
G_NSA = H_NSA // N_KV
HD = 64
CMP_STRIDE = 16
CMP_BLOCK = 2 * CMP_STRIDE
CMP_HIDDEN = 256
SEL_BLOCK = 64
N_SEL = 16
WINDOW = 512
Q_BLOCK = 128
N_BUCKETS = 32
MAX_DISTANCE = 128
H_RWKV = 8
HD_RWKV = 64
D_RWKV = H_RWKV * HD_RWKV
DECAY_LORA = 32
AAA_LORA = 32
GATE_LORA = 96
GN_EPS = 64e-5
D_FF = 2816
LN_EPS = 1e-5
DEPTH = 1
ALPHA = (2 * DEPTH) ** 0.25

NSA_SIZES = (H_NSA * HD,) + (N_KV * HD,) * 6 + (H_NSA * 3,)
RWKV_SIZES = (D_RWKV, D_RWKV, D_RWKV, DECAY_LORA, AAA_LORA, GATE_LORA)
NSA_COLS = sum(NSA_SIZES)
RWKV_COLS = sum(RWKV_SIZES)

F32 = jnp.float32
BF16 = jnp.bfloat16
LANE = 128
NEG = -(2.0 ** 100)
M_INIT = -(2.0 ** 103)
VMEM_LIMIT = 56 * 1024 * 1024

RW_PAD = 3 * D_RWKV + 3 * LANE
P_Q, P_KV, P_WIN, P_GATE, P_RW = 0, 512, 1024, 1280, 1408
P_COLS = P_RW + RW_PAD
KEY_TILE = 512
MASK_ROWS = 16
V_ROWS = 144
MASK_ROW0 = N_KV * HD
SEL_MASK0 = HD
SEL_FAR0 = SEL_MASK0 + MASK_ROWS
V_ROWS_KV = HD + MASK_ROWS


def _bucket_lows():
    d = np.arange(0, 4 * MAX_DISTANCE, dtype=np.int64)
    max_exact = N_BUCKETS // 2
    df = np.maximum(d, 1).astype(np.float32)
    large = max_exact + (np.log(df / np.float32(max_exact)) / np.float32(math.log(MAX_DISTANCE / max_exact))
                         * np.float32(N_BUCKETS - max_exact)).astype(np.int32)
    b = np.where(d < max_exact, d, np.minimum(large, N_BUCKETS - 1))
    lows = [int(np.argmax(b >= k)) for k in range(N_BUCKETS)]
    return b, lows


_BUCKET_OF, _BUCKET_LOW = _bucket_lows()
FAR_DIST = _BUCKET_LOW[N_BUCKETS - 1]


def _resident(shape):
    nd = len(shape)
    return pl.BlockSpec(shape, lambda *_: (0,) * nd, pipeline_mode=pl.Buffered(1))


def _params(sem):
    return pltpu.CompilerParams(dimension_semantics=sem, vmem_limit_bytes=VMEM_LIMIT)


def _split2(x):
    hi = x.astype(BF16)
    lo = (x - hi.astype(F32)).astype(BF16)
    return hi, lo


def _dot_exact_rhs(x, rhs_bf16, terms=2):
    acc = None
    rem = x
    for _ in range(terms):
        part = rem.astype(BF16)
        d = jnp.dot(part, rhs_bf16, preferred_element_type=F32)
        acc = d if acc is None else acc + d
        rem = rem - part.astype(F32)
    return acc


def _dot_exact_lhs(lhs_bf16, x, terms=3):
    acc = None
    rem = x
    for _ in range(terms):
        part = rem.astype(BF16)
        d = jnp.dot(lhs_bf16, part, preferred_element_type=F32)
        acc = d if acc is None else acc + d
        rem = rem - part.astype(F32)
    return acc


def _layer_norm(y, g, b):
    mu = jnp.mean(y, axis=-1, keepdims=True)
    yc = y - mu
    var = jnp.mean(yc * yc, axis=-1, keepdims=True)
    return yc * lax.rsqrt(var + LN_EPS) * g + b


def _bias_chain(d, tab_rows):
    out = tab_rows[0] + jnp.zeros(d.shape, F32)
    for b in range(1, N_BUCKETS):
        out = jnp.where(d >= _BUCKET_LOW[b], tab_rows[b], out)
    return out


def _ada_kernel(c_ref, w_ref, b_ref, o_ref):
    c = c_ref[...]
    h = (c * jax.nn.sigmoid(c)).astype(BF16)
    o_ref[...] = jnp.dot(h, w_ref[...].astype(BF16), preferred_element_type=F32) + b_ref[...]


def _ada(c_all, w_ada, b_ada):
    rows, n = c_all.shape[0], w_ada.shape[1]
    tn = 1152
    return pl.pallas_call(
        _ada_kernel,
        grid=(n // tn,),
        in_specs=[pl.BlockSpec((rows, D_MODEL), lambda j: (0, 0)),
                  pl.BlockSpec((D_MODEL, tn), lambda j: (0, j)),
                  pl.BlockSpec((1, tn), lambda j: (0, j))],
        out_specs=pl.BlockSpec((rows, tn), lambda j: (0, j)),
        out_shape=jax.ShapeDtypeStruct((rows, n), F32),
        compiler_params=_params(("arbitrary",)),
        name="ada",
    )(c_all, w_ada, b_ada.reshape(1, n))


FF_CHUNKS = 2


def _ffn_kernel(x_ref, sh_ref, sc_ref, gt_ref, lng_ref, lnb_ref, wg_ref, wu_ref, wd_ref, o_ref):
    x = x_ref[...]
    h = (x * (1.0 + sc_ref[...]) + sh_ref[...]).astype(BF16)
    ck = D_FF // FF_CHUNKS
    acc = jnp.zeros(x.shape, F32)
    for c in range(FF_CHUNKS):
        a = jnp.dot(h, wg_ref[:, c * ck:(c + 1) * ck], preferred_element_type=F32)
        b = jnp.dot(h, wu_ref[:, c * ck:(c + 1) * ck], preferred_element_type=F32)
        t = (a * jax.nn.sigmoid(a) * b).astype(BF16)
        acc = acc + jnp.dot(t, wd_ref[c * ck:(c + 1) * ck, :], preferred_element_type=F32)
    y = ALPHA * x + (1.0 + gt_ref[...]) * (0.5 * acc)
    o_ref[...] = _layer_norm(y, lng_ref[...], lnb_ref[...])


def _mod_spec(mod, tm):
    if mod.shape[0] == 1:
        return pl.BlockSpec((1, D_MODEL), lambda i: (0, 0))
    return pl.BlockSpec((tm, D_MODEL), lambda i: (i, 0))


def _ffn(x, shift, scale, gate, ln_g, ln_b, wg, wu, wd):
    rows = x.shape[0]
    tm = min(512, rows)
    row = lambda i: (i, 0)
    return pl.pallas_call(
        _ffn_kernel,
        grid=(rows // tm,),
        in_specs=[pl.BlockSpec((tm, D_MODEL), row), _mod_spec(shift, tm), _mod_spec(scale, tm), _mod_spec(gate, tm),
                  _resident((1, D_MODEL)), _resident((1, D_MODEL)),
                  _resident((D_MODEL, D_FF)), _resident((D_MODEL, D_FF)), _resident((D_FF, D_MODEL))],
        out_specs=pl.BlockSpec((tm, D_MODEL), row),
        out_shape=jax.ShapeDtypeStruct((rows, D_MODEL), F32),
        compiler_params=_params(("arbitrary",)),
        name="ffn",
    )(x, shift, scale, gate, ln_g.reshape(1, -1), ln_b.reshape(1, -1), wg, wu, wd)


def _proj_kernel(x_ref, sh_ref, sc_ref, w_ref, q_ref, kv_ref, win_ref, g_ref, pr_ref):
    h = (x_ref[...] * (1.0 + sc_ref[...]) + sh_ref[...]).astype(BF16)
    p = jnp.dot(h, w_ref[...], preferred_element_type=F32)
    q_ref[...] = p[:, P_Q:P_KV]
    kv_ref[...] = p[:, P_KV:P_WIN]
    win_ref[...] = p[:, P_WIN:P_GATE]
    g_ref[...] = jax.nn.sigmoid(p[:, P_GATE:P_RW])
    pr_ref[...] = p[:, P_RW:P_COLS]


def _proj(x, shift, scale, w_in_p):
    rows = x.shape[0]
    tm = min(512, rows)
    row = lambda i: (i, 0)
    widths = (512, 512, 256, LANE, RW_PAD)
    return pl.pallas_call(
        _proj_kernel,
        grid=(rows // tm,),
        in_specs=[pl.BlockSpec((tm, D_MODEL), row), _mod_spec(shift, tm), _mod_spec(scale, tm),
                  _resident((D_MODEL, P_COLS))],
        out_specs=[pl.BlockSpec((tm, w), row) for w in widths],
        out_shape=[jax.ShapeDtypeStruct((rows, w), F32) for w in widths],
        compiler_params=_params(("arbitrary",)),
        name="proj",
    )(x, shift, scale, w_in_p)


def _prep_w_in(w_in):
    pad = lambda a, n: jnp.pad(a, ((0, 0), (0, n - a.shape[1])))
    nsa, rw = w_in[:, :NSA_COLS], w_in[:, NSA_COLS:]
    gl = nsa[:, 1280:1304].reshape(D_MODEL, H_NSA, 3).transpose(0, 2, 1).reshape(D_MODEL, 3 * H_NSA)
    cols = [nsa[:, :1280], pad(gl, LANE), _rwkv_cols(rw)]
    return jnp.concatenate(cols, axis=1).astype(BF16)


def _rwkv_cols(a):
    pad = lambda t: jnp.pad(t, [(0, 0)] * (t.ndim - 1) + [(0, LANE - t.shape[-1])])
    n = 3 * D_RWKV
    return jnp.concatenate([a[..., :n], pad(a[..., n:n + 32]), pad(a[..., n + 32:n + 64]), pad(a[..., n + 64:n + 160])], axis=-1)


def _rwkv_uncols(a):
    n = 3 * D_RWKV
    return jnp.concatenate([a[..., :n], a[..., n:n + 32], a[..., n + LANE:n + LANE + 32], a[..., n + 2 * LANE:n + 2 * LANE + 96]], axis=-1)


RW_GROUP = 64
RW_TOK_PAD = 16
RW_PAIRS = H_RWKV // 2
RW_BLOCK = 64
RW_STEP = 128


def _lora(x, w_ref):
    w = w_ref[...]
    w_hi = w.astype(BF16)
    w_lo = (w - w_hi.astype(F32)).astype(BF16)
    return _dot_exact_rhs(x, w_hi) + jnp.dot(x.astype(BF16), w_lo, preferred_element_type=F32)


def _rwkv_kernel(pr_ref, sh0_ref, s0_ref, mu_ref, w0_ref, a0_ref, kk_ref, ka_ref, rk_ref, gw_ref, gb_ref,
                 w2_ref, a2_ref, g2_ref, bo_ref, lgrp_ref, ggrp_ref, o_ref, sout_ref,
                 prev_scr, s_scr, *, n_valid):
    tb = pr_ref.shape[0]
    step = pl.program_id(1)

    @pl.when(step == 0)
    def _():
        prev_scr[...] = sh0_ref[...]
        s_scr[...] = s0_ref[...]

    p = pr_ref[...]
    rows = lax.broadcasted_iota(jnp.int32, (tb, 1), 0)
    prev = jnp.where(rows == 0, prev_scr[...], pltpu.roll(p, 1, axis=0))
    prev_scr[...] = p[tb - 1:tb, :]
    xs = p + (prev - p) * mu_ref[...]
    n = D_RWKV
    r, k, v = xs[:, :n], xs[:, n:2 * n], xs[:, 2 * n:3 * n]
    wl, al, gl = xs[:, 3 * n:3 * n + LANE], xs[:, 3 * n + LANE:3 * n + 2 * LANE], xs[:, 3 * n + 2 * LANE:]
    z = -(w0_ref[...] + _lora(jnp.tanh(wl), w2_ref))
    w = -(jnp.maximum(z, 0.0) + jnp.log(1.0 + jnp.exp(-jnp.abs(z)))) - 0.5
    a = jax.nn.sigmoid(a0_ref[...] + _lora(al, a2_ref))
    g = _lora(jax.nn.sigmoid(gl), g2_ref)
    kk = k * kk_ref[...]
    ss = _dot_exact_rhs(kk * kk, bo_ref[...])
    kk = kk / jnp.maximum(jnp.sqrt(ss), 1e-12)
    k2 = k * (1.0 + (a - 1.0) * ka_ref[...])
    G = min(RW_GROUP, tb)
    log_dec = -jnp.exp(w)
    bet = kk * a
    if n_valid < tb:
        live = rows < n_valid
        log_dec, kk, bet, k2, v_in = (jnp.where(live, x, 0.0) for x in (log_dec, kk, bet, k2, v))
    else:
        v_in = v
    cum = _dot_exact_lhs(lgrp_ref[...], log_dec, terms=2)
    cum_end = _dot_exact_lhs(ggrp_ref[...], log_dec, terms=2)
    gam_inv = jnp.exp(-cum)
    gam_end = jnp.exp(cum_end - cum)
    k_hat = -kk * jnp.exp(cum - log_dec)
    r_hat = r * jnp.exp(cum)
    b_chk, k_chk = bet * gam_inv, k2 * gam_inv
    b_til, k_til = bet * gam_end, k2 * gam_end
    gam_group = jnp.exp(cum_end)

    lane = lax.broadcasted_iota(jnp.int32, (1, LANE), 1)
    low = lane < HD_RWKV
    lane_t = lane & (RW_BLOCK - 1)
    row = lax.broadcasted_iota(jnp.int32, (LANE, 1), 0)
    row_t = row & (RW_BLOCK - 1)
    same = ((row < HD_RWKV) == low) & ((row_t // G) == (lane_t // G))
    strict, incl = same & (lane_t < row_t), same & (lane_t <= row_t)
    bf = lambda x: x.astype(BF16)
    mm = lambda x, y: jnp.dot(bf(x), bf(y), preferred_element_type=F32)
    mm_nt = lambda x, y: lax.dot_general(bf(x), bf(y), (((1,), (1,)), ((), ())), preferred_element_type=F32)

    def rows_bd(x):
        if tb < RW_BLOCK:
            x = jnp.concatenate([x, jnp.zeros((RW_BLOCK - tb, LANE), F32)], axis=0)
        return jnp.concatenate([jnp.where(low, x, 0.0), jnp.where(low, 0.0, x)], axis=0)

    def mm3(x, y):
        xh, yh = bf(x), bf(y)
        xl, yl = bf(x - xh.astype(F32)), bf(y - yh.astype(F32))
        return jnp.dot(jnp.concatenate([xh, xl, xh], axis=1), jnp.concatenate([yh, yh, yl], axis=0), preferred_element_type=F32)

    units = max(1, tb // RW_BLOCK)
    unit_rows = min(tb, RW_BLOCK)
    pairs = range(RW_PAIRS)
    items = [(un, pp) for un in range(units) for pp in pairs]
    at = lambda x, it: x[it[0] * unit_rows:(it[0] + 1) * unit_rows, it[1] * LANE:(it[1] + 1) * LANE]
    kh_row = [rows_bd(at(k_hat, it)) for it in items]
    rh_row = [rows_bd(at(r_hat, it)) for it in items]
    kh_mat = [x.T for x in kh_row]
    rh_mat = [x.T for x in rh_row]
    state_in = [jnp.concatenate([rows_bd(at(b_chk, it)).T, rows_bd(at(k_chk, it)).T], axis=1) for it in items]
    upd_rows = [jnp.concatenate([rows_bd(at(b_til, it)), rows_bd(at(k_til, it))], axis=0) for it in items]
    v_t = [rows_bd(at(v_in, it)).T for it in items]
    v_t = [x[:HD_RWKV] + x[HD_RWKV:] for x in v_t]
    c_all = [mm(jnp.concatenate([kh_row[n], rh_row[n]], axis=0), state_in[n]) for n in range(len(items))]
    c_uu = [jnp.where(strict, c[:LANE, :LANE], 0.0) for c in c_all]
    c_uv = [jnp.where(strict, c[:LANE, LANE:], 0.0) for c in c_all]
    c_ru = [jnp.where(incl, c[LANE:, :LANE], 0.0) for c in c_all]
    c_rv = [jnp.where(incl, c[LANE:, LANE:], 0.0) for c in c_all]
    t_neu, power = list(c_uu), list(c_uu)
    span = 2
    while span < G:
        power = [mm(x, x) for x in power]
        t_neu = [t_neu[n] + power[n] + mm(t_neu[n], power[n]) for n in range(len(items))]
        span *= 2
    from_v = [mm_nt(v_t[n], c_uv[n]) for n in range(len(items))]
    st = [s_scr[pp] for pp in pairs]
    y_units = []
    for un in range(units):
        ns = [un * RW_PAIRS + pp for pp in pairs]
        y_t = [jnp.zeros((HD_RWKV, LANE), F32) for _ in pairs]
        for grp in range(unit_rows // G):
            here = (lane_t // G) == grp
            first = un * unit_rows + grp * G
            w_t = [jnp.where(here, mm(st[pp], kh_mat[ns[pp]]) + from_v[ns[pp]], 0.0) for pp in pairs]
            u_t = [w_t[pp] + mm_nt(w_t[pp], t_neu[ns[pp]]) for pp in pairs]
            v_g = [jnp.where(here, v_t[ns[pp]], 0.0) for pp in pairs]
            y_t = [y_t[pp] + jnp.where(here, mm(st[pp], rh_mat[ns[pp]]), 0.0) + mm_nt(u_t[pp], c_ru[ns[pp]])
                   + mm_nt(v_g[pp], c_rv[ns[pp]]) for pp in pairs]
            st = [st[pp] * gam_group[first:first + 1, pp * LANE:(pp + 1) * LANE]
                  + mm3(jnp.concatenate([u_t[pp], v_g[pp]], axis=1), upd_rows[ns[pp]]) for pp in pairs]
        lane_u = lax.broadcasted_iota(jnp.int32, (unit_rows, LANE), 1)
        pieces = []
        for pp in pairs:
            yt = jnp.concatenate([y_t[pp], jnp.zeros((LANE - HD_RWKV, LANE), F32)], axis=0).T
            pieces.append(jnp.where(lane_u < HD_RWKV, yt[:unit_rows], pltpu.roll(yt[RW_BLOCK:RW_BLOCK + unit_rows], HD_RWKV, axis=1)))
        y_units.append(jnp.concatenate(pieces, axis=1))
    for pp in pairs:
        s_scr[pp] = st[pp]
    y = y_units[0] if units == 1 else jnp.concatenate(y_units, axis=0)
    mean = _dot_exact_rhs(y, bo_ref[...]) * (1.0 / HD_RWKV)
    yc = y - mean
    var = _dot_exact_rhs(yc * yc, bo_ref[...]) * (1.0 / HD_RWKV)
    yn = yc * lax.rsqrt(var + GN_EPS) * gw_ref[...] + gb_ref[...]
    bonus = _dot_exact_rhs(r * k2 * rk_ref[...], bo_ref[...]) * v
    o_ref[...] = (yn + bonus) * g
    sout_ref[...] = s_scr[...]


def _pair_state(s):
    B = s.shape[0]
    return s.reshape(B, RW_PAIRS, 2, HD_RWKV, HD_RWKV).transpose(0, 1, 3, 2, 4).reshape(B, RW_PAIRS, HD_RWKV, LANE)


def _unpair_state(s):
    B = s.shape[0]
    return s.reshape(B, RW_PAIRS, HD_RWKV, 2, HD_RWKV).transpose(0, 1, 3, 2, 4).reshape(B, H_RWKV, HD_RWKV, HD_RWKV)


def _rwkv(pr, shift0, s0, lw, n_valid):
    B, T, _ = pr.shape
    tb = min(RW_STEP, T)
    n = D_RWKV
    vec = lambda a: a.reshape(1, n)
    padrow = lambda a: jnp.pad(a, ((0, LANE - a.shape[0]), (0, 0)))
    blk = np.arange(n) // HD_RWKV
    block_ones = jnp.asarray(blk[:, None] == blk[None, :], BF16)
    tok = np.arange(tb)
    group = min(RW_GROUP, tb)
    same_group = tok[:, None] // group == tok[None, :] // group
    prefix = jnp.asarray(same_group & (tok[None, :] <= tok[:, None]), BF16)
    consts = [_rwkv_cols(lw['rwkv_mu']).reshape(1, RW_PAD), vec(lw['rwkv_w0']), vec(lw['rwkv_a0']), vec(lw['rwkv_k_k']),
              vec(lw['rwkv_k_a']), vec(lw['rwkv_r_k']), vec(lw['rwkv_gn_w']), vec(lw['rwkv_gn_b']),
              padrow(lw['rwkv_w2']), padrow(lw['rwkv_a2']), padrow(lw['rwkv_g2']), block_ones, prefix, jnp.asarray(same_group, BF16)]
    kern = functools.partial(_rwkv_kernel, n_valid=n_valid)
    state_spec = pl.BlockSpec((None, RW_PAIRS, HD_RWKV, LANE), lambda b, j: (b, 0, 0, 0))
    o, s = pl.pallas_call(
        kern,
        grid=(B, T // tb),
        in_specs=[pl.BlockSpec((None, tb, RW_PAD), lambda b, j: (b, j, 0)),
                  pl.BlockSpec((None, 1, RW_PAD), lambda b, j: (b, 0, 0)), state_spec]
                 + [_resident(c.shape) for c in consts],
        out_specs=[pl.BlockSpec((None, tb, n), lambda b, j: (b, j, 0)), state_spec],
        out_shape=[jax.ShapeDtypeStruct((B, T, n), F32), jax.ShapeDtypeStruct((B, RW_PAIRS, HD_RWKV, LANE), F32)],
        scratch_shapes=[pltpu.VMEM((1, RW_PAD), F32), pltpu.VMEM((RW_PAIRS, HD_RWKV, LANE), F32)],
        compiler_params=_params(("arbitrary", "arbitrary")),
        name="rwkv",
    )(pr, shift0, _pair_state(s0), *consts)
    return o, _unpair_state(s)


CMP_PAGES = 32
CHUNKS_PER_PAGE = PAGE_SIZE // CMP_STRIDE
CMP_K = CMP_STRIDE * N_KV * HD


def _compress_kernel(pt_ref, *refs, n_pages, transposed):
    weights = refs[2 * n_pages + 2:2 * n_pages + 10]
    outs = refs[2 * n_pages + 10:2 * n_pages + 12]
    rows_scr = refs[2 * n_pages + 12]
    rows = CHUNKS_PER_PAGE * n_pages
    for kind in range(2):
        pages, nxt = refs[kind * n_pages:(kind + 1) * n_pages], refs[2 * n_pages + kind]
        pe_ref, w_ref, b_ref, w2_ref = weights[4 * kind:4 * kind + 4]
        for p, pg in enumerate(pages):
            rows_scr[p * PAGE_SIZE:(p + 1) * PAGE_SIZE, :] = pg[...].T if transposed else pg[...]
        nxt_rows = nxt[...].T[:CMP_STRIDE] if transposed else nxt[...]
        x = jnp.concatenate([rows_scr[pl.ds(s, rows, stride=CMP_STRIDE), :] for s in range(CMP_STRIDE)], axis=1)
        xn = jnp.concatenate([nxt_rows[s:s + 1, :] for s in range(CMP_STRIDE)], axis=1)
        x_ext = jnp.concatenate([x, jnp.broadcast_to(xn, (8, CMP_K))], axis=0)
        h_first = jnp.dot((x + pe_ref[0]).astype(BF16), w_ref[0], preferred_element_type=F32)
        h_second = jnp.dot((x_ext + pe_ref[1]).astype(BF16), w_ref[1], preferred_element_type=F32)
        h_next = pltpu.roll(h_second, rows + 8 - 1, axis=0)[:rows]
        hidden = jax.nn.gelu(h_first + h_next + b_ref[...])
        outs[kind][...] = jnp.dot(hidden.astype(BF16), w2_ref[...], preferred_element_type=F32)


def _compress_weights(pe, w1, b1, w2):
    eye = jnp.eye(N_KV, dtype=F32)
    halves = []
    for half in range(2):
        w = w1[half * CMP_STRIDE:(half + 1) * CMP_STRIDE]
        halves.append(jnp.einsum('sdn,hg->shdgn', w, eye).reshape(CMP_K, N_KV * CMP_HIDDEN))
    pe2 = jnp.stack([jnp.broadcast_to(pe[half * CMP_STRIDE:(half + 1) * CMP_STRIDE, None, :], (CMP_STRIDE, N_KV, HD)).reshape(1, CMP_K)
                     for half in range(2)])
    w2bd = jnp.einsum('nd,hg->hngd', w2, eye).reshape(N_KV * CMP_HIDDEN, N_KV * HD)
    return [pe2, jnp.stack(halves).astype(BF16), jnp.tile(b1, N_KV).reshape(1, -1), w2bd.astype(BF16)]


def _compress(pool, page_table, lw, transposed):
    B, n_pages_total = page_table.shape
    n_pages = min(CMP_PAGES, n_pages_total)
    rows = CHUNKS_PER_PAGE * n_pages
    weights = (_compress_weights(lw['cmp_pe_k'], lw['cmp_w1_k'], lw['cmp_b1_k'], lw['cmp_w2_k'])
               + _compress_weights(lw['cmp_pe_v'], lw['cmp_w1_v'], lw['cmp_b1_v'], lw['cmp_w2_v']))
    width = N_KV * HD
    at = (lambda page, kind: (page, kind, 0)) if transposed else (lambda page, kind: (page, 0, kind))

    def page_map(p, kind):
        return lambda b, j, pt: at(pt[b, j * n_pages + p], kind)

    def next_map(kind):
        return lambda b, j, pt: at(pt[b, jnp.minimum((j + 1) * n_pages, n_pages_total - 1)], kind)

    const = lambda a: pl.BlockSpec(a.shape, lambda b, j, pt: (0,) * a.ndim)
    out_spec = pl.BlockSpec((None, rows, N_KV * HD), lambda b, j, pt: (b, j, 0))
    out_shape = jax.ShapeDtypeStruct((B, n_pages_total * CHUNKS_PER_PAGE, N_KV * HD), F32)
    next_rows = PAGE_SIZE if transposed else CMP_STRIDE
    grid_spec = pltpu.PrefetchScalarGridSpec(
        num_scalar_prefetch=1,
        grid=(B, n_pages_total // n_pages),
        in_specs=[pl.BlockSpec((None, PAGE_SIZE, width), page_map(p, kind)) for kind in range(2) for p in range(n_pages)]
                 + [pl.BlockSpec((None, next_rows, width), next_map(kind)) for kind in range(2)] + [const(a) for a in weights],
        out_specs=[out_spec, out_spec],
        scratch_shapes=[pltpu.VMEM((n_pages * PAGE_SIZE, width), F32)],
    )
    return pl.pallas_call(
        functools.partial(_compress_kernel, n_pages=n_pages, transposed=transposed),
        grid_spec=grid_spec,
        out_shape=[out_shape, out_shape],
        compiler_params=_params(("arbitrary", "arbitrary")),
        name="compress",
    )(page_table, *([pool] * (2 * n_pages + 2)), *weights)


BAND_ROWS = 1152


def _band_kernel(tab_ref, bkt_ref, o_ref):
    h = pl.program_id(0)
    bkt = bkt_ref[...]
    out = jnp.full(bkt.shape, NEG, F32)
    for b in range(N_BUCKETS):
        out = jnp.where(bkt == b, tab_ref[b, h], out)
    o_ref[...] = out


def _band(rel_table):
    u = np.arange(BAND_ROWS)[:, None]
    qi = np.arange(Q_BLOCK)[None, :]
    d = qi + WINDOW - u
    bkt = np.where(d >= 0, _BUCKET_OF[np.clip(d, 0, len(_BUCKET_OF) - 1)], -1).astype(np.int32)
    return pl.pallas_call(
        _band_kernel,
        grid=(H_NSA,),
        in_specs=[pl.BlockSpec(memory_space=pltpu.SMEM), pl.BlockSpec((BAND_ROWS, Q_BLOCK), lambda h: (0, 0))],
        out_specs=pl.BlockSpec((None, BAND_ROWS, Q_BLOCK), lambda h: (h, 0, 0)),
        out_shape=jax.ShapeDtypeStruct((H_NSA, BAND_ROWS, Q_BLOCK), F32),
        compiler_params=_params(("arbitrary",)),
        name="band",
    )(rel_table, jnp.asarray(bkt))


def _softmax_cols(s):
    m = jnp.max(s, axis=0, keepdims=True)
    e = jnp.exp(s - m)
    l = jnp.sum(e, axis=0, keepdims=True)
    return e * jnp.where(m > 0.5 * NEG, 1.0 / l, 0.0)


def _select_blocks(impsel, qpos, n_pick):
    ns = impsel.shape[0]
    blk = lax.broadcasted_iota(jnp.int32, impsel.shape, 0)
    cur = jnp.right_shift(qpos, 6)
    future = blk * SEL_BLOCK > qpos
    forced = (blk == 0) | (blk == cur) | (blk == cur - 1)
    score = jnp.where(future, -jnp.inf, jnp.where(forced, jnp.inf, impsel))
    chosen = jnp.zeros(impsel.shape, F32)
    for _ in range(n_pick):
        best = jnp.max(score, axis=0, keepdims=True)
        first = jnp.min(jnp.where(score == best, blk, ns), axis=0, keepdims=True)
        hit = (blk == first) & (best > -jnp.inf)
        chosen = jnp.where(hit, 1.0, chosen)
        score = jnp.where(hit, -jnp.inf, score)
    return jnp.where(chosen > 0.0, 0.0, NEG)


def _pool_matrix(ns, nc):
    j = np.arange(ns)[:, None]
    n = np.arange(nc)[None, :]
    ratio = SEL_BLOCK // CMP_STRIDE
    return jnp.asarray((n >= ratio * j - 1) & (n <= ratio * j + ratio - 1), BF16)


def _nsa_prompt_kernel(tab_ref, q_ref, g_ref, kc_ref, vct_ref, ks_ref, vst_ref, kw_ref, vwt_ref, band_ref, pool_ref, o_ref,
                       rhs_scr, mask_scr, acc_scr, m_scr, sc_scr, sa_scr, sb_scr):
    i = pl.program_id(0)
    ncp = kc_ref.shape[0]
    ns = pool_ref.shape[0]
    s0 = i * Q_BLOCK
    q_t = (q_ref[...] * HD ** -0.5).T
    g_t = g_ref[...].T
    lane_q = lax.broadcasted_iota(jnp.int32, (1, G_NSA * Q_BLOCK), 1) & (Q_BLOCK - 1)
    qpos = s0 + lax.broadcasted_iota(jnp.int32, (1, Q_BLOCK), 1)
    rhs_scr[...] = jnp.zeros(rhs_scr.shape, BF16)
    heads_out = []

    def compressed(k):
        lanes4 = lambda f: jnp.concatenate([f(G_NSA * k + g) for g in range(G_NSA)], axis=1)
        qcols = lanes4(lambda h: q_t[h * HD:(h + 1) * HD, :])
        zero = jnp.zeros_like(qcols)
        top = jnp.concatenate([qcols, zero] if k == 0 else [zero, qcols], axis=0).astype(BF16)
        far_row = lanes4(lambda h: band_ref[h, 0:1, :])
        n0 = pl.multiple_of(jnp.clip(8 * i - 16, 0, ncp - 32), 8)
        nrow = lax.broadcasted_iota(jnp.int32, (ncp, 1), 0)
        sc_scr[...] = jnp.dot(kc_ref[...], top, preferred_element_type=F32) + jnp.where(nrow < n0, far_row, NEG)
        d_edge = qpos - (CMP_STRIDE * (n0 + lax.broadcasted_iota(jnp.int32, (32, 1), 0)) + CMP_BLOCK - 1)
        edge_bias = lanes4(lambda h: jnp.where(d_edge >= 0, _bias_chain(d_edge, [tab_ref[b, h] for b in range(N_BUCKETS)]), NEG))
        sc_scr[pl.ds(n0, 32), :] = jnp.dot(kc_ref[pl.ds(n0, 32), :], top, preferred_element_type=F32) + edge_bias
        p_c = _softmax_cols(sc_scr[...])
        o_c = jnp.dot(vct_ref[...], p_c.astype(BF16), preferred_element_type=F32)[k * HD:(k + 1) * HD]
        imp = p_c[:, 0:Q_BLOCK]
        for g in range(1, G_NSA):
            imp = imp + p_c[:, g * Q_BLOCK:(g + 1) * Q_BLOCK]
        return qcols, top, far_row, o_c, _dot_exact_lhs(pool_ref[...], imp, terms=2)

    branches = [compressed(k) for k in range(N_KV)]
    masks = [_select_blocks(b[4], qpos, min(N_SEL, ns)).astype(BF16) for b in branches]
    for k in range(N_KV):
        heads = [G_NSA * k + g for g in range(G_NSA)]
        lanes4 = lambda f: jnp.concatenate([f(h) for h in heads], axis=1)
        qcols, top, far_row, o_c, _ = branches[k]
        mask_scr[...] = jnp.concatenate([masks[k]] * G_NSA, axis=1)

        m_scr[...] = jnp.full(m_scr.shape, M_INIT, F32)
        acc_scr[...] = jnp.zeros(acc_scr.shape, F32)
        kd = i // 4
        r = i % 4

        far_hi = far_row.astype(BF16).astype(F32)
        rhs_scr[0:HD, :] = qcols.astype(BF16)
        rhs_scr[SEL_FAR0:SEL_FAR0 + MASK_ROWS, :] = jnp.concatenate(
            [far_hi, far_row - far_hi, jnp.zeros((MASK_ROWS - 2, G_NSA * Q_BLOCK), F32)], axis=0).astype(BF16)

        def scores(slab, kts, extra):
            rhs_scr[SEL_MASK0:SEL_MASK0 + MASK_ROWS, :] = mask_scr[pl.ds(pl.multiple_of(slab * MASK_ROWS, MASK_ROWS), MASK_ROWS), :]
            rhs = rhs_scr[...]
            out = []
            for kt, add in zip(kts, extra):
                s = jnp.dot(ks_ref[k, pl.ds(pl.multiple_of(kt * KEY_TILE, KEY_TILE), KEY_TILE), :], rhs, preferred_element_type=F32)
                out.append(s if add is None else s + add)
            return out

        def update(kts, tiles):
            m_old = m_scr[...]
            m_new = m_old
            for s in tiles:
                m_new = jnp.maximum(m_new, jnp.max(s, axis=0, keepdims=True))
            acc = jnp.exp(m_old - m_new) * acc_scr[...]
            for kt, s in zip(kts, tiles):
                acc = acc + jnp.dot(vst_ref[k, kt], jnp.exp(s - m_new).astype(BF16), preferred_element_type=F32)
            acc_scr[...] = acc
            m_scr[...] = m_new

        def attend(slab, kts, extra):
            update(kts, scores(slab, kts, extra))

        near_at = lambda start: lanes4(lambda h: band_ref[h, pl.ds(pl.multiple_of(start, Q_BLOCK), KEY_TILE), :]) - far_row
        prev_near = (r == 0) & (kd >= 1)
        kd_odd = (kd & 1) == 1
        even_prev = jnp.logical_not(kd_odd) & prev_near
        n_pairs = kd // 2 - even_prev.astype(jnp.int32)
        n_quads = n_pairs // 2

        def pair_scores(dst, pair):
            lo, hi = scores(pair, [2 * pair, 2 * pair + 1], [None, None])
            dst[0:KEY_TILE, :] = lo
            dst[KEY_TILE:, :] = hi

        def pair_update(src, pair):
            update([2 * pair, 2 * pair + 1], [src[0:KEY_TILE, :], src[KEY_TILE:, :]])

        @pl.when(n_quads > 0)
        def _():
            pair_scores(sa_scr, 0)

        def quad_body(qd, carry):
            first = 2 * qd
            pair_scores(sb_scr, first + 1)
            pair_update(sa_scr, first)
            pair_scores(sa_scr, jnp.minimum(first + 2, 2 * n_quads - 2))
            pair_update(sb_scr, first + 1)
            return carry

        lax.fori_loop(0, n_quads, quad_body, 0)

        @pl.when((n_pairs & 1) == 1)
        def _():
            attend(n_pairs - 1, [2 * n_pairs - 2, 2 * n_pairs - 1], [None, None])

        @pl.when(kd_odd)
        def _():
            attend(kd // 2, [kd - 1, kd], [jnp.where(prev_near, near_at(0), 0.0), near_at(KEY_TILE - Q_BLOCK * r)])

        @pl.when(even_prev)
        def _():
            attend(kd // 2 - 1, [kd - 2, kd - 1], [None, near_at(0)])

        @pl.when(jnp.logical_not(kd_odd))
        def _():
            attend(kd // 2, [kd], [near_at(KEY_TILE - Q_BLOCK * r)])

        acc = acc_scr[...]
        o_s = acc[0:HD] / acc[HD:HD + 1]

        ws = pl.multiple_of(jnp.maximum(s0 - WINDOW, 0), Q_BLOCK)
        u0 = pl.multiple_of(WINDOW - (s0 - ws), Q_BLOCK)
        n_win = WINDOW + Q_BLOCK
        u = u0 + lax.broadcasted_iota(jnp.int32, (n_win, 1), 0)
        s_w = (jnp.dot(kw_ref[pl.ds(ws, n_win), :], top, preferred_element_type=F32)
               + lanes4(lambda h: band_ref[h, pl.ds(u0, n_win), :]) + jnp.where(u > lane_q, 0.0, NEG))
        m_w = jnp.max(s_w, axis=0, keepdims=True)
        p_w = jnp.exp(s_w - m_w).astype(BF16)
        acc_w = jnp.zeros((V_ROWS, G_NSA * Q_BLOCK), F32)
        for j in range(n_win // Q_BLOCK):
            acc_w = acc_w + jnp.dot(vwt_ref[ws // Q_BLOCK + j], p_w[j * Q_BLOCK:(j + 1) * Q_BLOCK], preferred_element_type=F32)
        o_w = acc_w[k * HD:(k + 1) * HD] / acc_w[N_KV * HD:N_KV * HD + 1]

        for g, h in enumerate(heads):
            cols = slice(g * Q_BLOCK, (g + 1) * Q_BLOCK)
            heads_out.append(o_c[:, cols] * g_t[h:h + 1] + o_s[:, cols] * g_t[H_NSA + h:H_NSA + h + 1]
                             + o_w[:, cols] * g_t[2 * H_NSA + h:2 * H_NSA + h + 1])
    o_ref[...] = jnp.concatenate(heads_out, axis=0).T


def _sel_pattern(rows, width):
    key = np.arange(rows)[:, None]
    b = np.arange(width)[None, :]
    ones = (b >= MASK_ROWS) & (b < MASK_ROWS + 2)
    return jnp.asarray(((key // SEL_BLOCK) % MASK_ROWS == b) | ones, BF16)


def _values_t(v, tile):
    T, n = v.shape
    rows = n + MASK_ROWS
    vt = jnp.concatenate([v.T, jnp.ones((1, T), F32), jnp.zeros((rows - n - 1, T), F32)], axis=0)
    return vt.reshape(rows, T // tile, tile).transpose(1, 0, 2).astype(BF16)


def _nsa_prompt(q, gates, kv, win, kc, vc, band, rel_table):
    T = q.shape[0]
    ncp, ns = kc.shape[0], T // SEL_BLOCK
    k_sel = lambda h: kv[:, 256 + h * HD:256 + (h + 1) * HD]
    v_sel = lambda h: kv[:, 384 + h * HD:384 + (h + 1) * HD]
    pattern = _sel_pattern(T, LANE - HD)
    ks_aug = jnp.stack([jnp.concatenate([k_sel(h).astype(BF16), pattern], axis=1) for h in range(N_KV)])
    operands = [q, gates, kc.astype(BF16), vc.T.astype(BF16), ks_aug, jnp.stack([_values_t(v_sel(h), KEY_TILE) for h in range(N_KV)]),
                win[:, 0:128].astype(BF16), _values_t(win[:, 128:256], Q_BLOCK), band, _pool_matrix(ns, ncp)]
    blk = lambda w: pl.BlockSpec((Q_BLOCK, w), lambda i: (i, 0))
    return pl.pallas_call(
        _nsa_prompt_kernel,
        grid=(T // Q_BLOCK,),
        in_specs=[pl.BlockSpec(memory_space=pltpu.SMEM), blk(H_NSA * HD), blk(LANE)] + [_resident(a.shape) for a in operands[2:]],
        out_specs=blk(H_NSA * HD),
        out_shape=jax.ShapeDtypeStruct((T, H_NSA * HD), F32),
        scratch_shapes=[pltpu.VMEM((LANE, G_NSA * Q_BLOCK), BF16), pltpu.VMEM((ns, G_NSA * Q_BLOCK), BF16),
                        pltpu.VMEM((V_ROWS_KV, G_NSA * Q_BLOCK), F32), pltpu.VMEM((1, G_NSA * Q_BLOCK), F32),
                        pltpu.VMEM((ncp, G_NSA * Q_BLOCK), F32)] + [pltpu.VMEM((2 * KEY_TILE, G_NSA * Q_BLOCK), F32)] * 2,
        compiler_params=_params(("arbitrary",)),
        name="nsa_prompt",
    )(rel_table, *operands)


SMP_PAGES = 8
TOK_PAD = 8
SMP_COLS = H_NSA * TOK_PAD


def _nsa_sample_kernel(pt_ref, *refs, n_pages, n_valid, past):
    pages = refs[:n_pages]
    (q_ref, g_ref, kc_ref, vc_ref, kvn_ref, win_ref, winn_ref, tab_ref, pool_ref, gsum_ref, epat_ref, o_ref,
     top_scr, mask_scr, acc_scr, m_scr, l_scr, oc_scr, ow_scr, nearb_scr) = refs[n_pages:]
    j = pl.program_id(1)
    ncp, wbuf = kc_ref.shape[0], win_ref.shape[1]
    tile_keys = n_pages * PAGE_SIZE
    lane = lax.broadcasted_iota(jnp.int32, (1, LANE), 1)
    tok = lane & (TOK_PAD - 1)
    second_kv = lane >= G_NSA * TOK_PAD
    tab = [tab_ref[b:b + 1, :] for b in range(N_BUCKETS)]
    far_row = tab[N_BUCKETS - 1]
    own_rows = lambda x: jnp.where(second_kv, x[HD:2 * HD], x[0:HD])
    pad_rows = lambda x: jnp.concatenate([x, jnp.zeros((LANE - x.shape[0], x.shape[1]), x.dtype)], axis=0)
    trow = lax.broadcasted_iota(jnp.int32, (LANE, 1), 0)
    d_new = tok - trow
    new_bias = jnp.where((d_new >= 0) & (trow < n_valid), _bias_chain(jnp.maximum(d_new, 0), tab), NEG)

    def attend_update(s, values_t):
        m_old = m_scr[...]
        m_new = jnp.maximum(m_old, jnp.max(s, axis=0, keepdims=True))
        alpha = jnp.exp(m_old - m_new)
        p = jnp.exp(s - m_new)
        l_scr[...] = alpha * l_scr[...] + jnp.sum(p, axis=0, keepdims=True)
        acc_scr[...] = alpha * acc_scr[...] + jnp.dot(values_t.astype(BF16), p.astype(BF16), preferred_element_type=F32)
        m_scr[...] = m_new

    def reset():
        m_scr[...] = jnp.full(m_scr.shape, M_INIT, F32)
        l_scr[...] = jnp.zeros(l_scr.shape, F32)
        acc_scr[...] = jnp.zeros(acc_scr.shape, F32)

    @pl.when(j == 0)
    def _():
        q_t = pad_rows(q_ref[...] * HD ** -0.5).T
        halves = []
        for k in range(N_KV):
            part = jnp.zeros((HD, LANE), F32)
            for g in range(G_NSA):
                h = G_NSA * k + g
                piece = q_t[h * HD:(h + 1) * HD, :]
                part = part + (pltpu.roll(piece, TOK_PAD * h, axis=1) if h else piece)
            halves.append(part)
        top = jnp.concatenate(halves, axis=0).astype(BF16)
        top_scr[...] = top
        qpos = past + tok

        n0 = ncp - 32
        kcb = kc_ref[...].astype(BF16)
        d_edge = qpos - (CMP_STRIDE * (n0 + lax.broadcasted_iota(jnp.int32, (32, 1), 0)) + CMP_BLOCK - 1)
        s_c = jnp.concatenate([
            jnp.dot(kcb[:n0], top, preferred_element_type=F32) + far_row,
            jnp.dot(kcb[n0:], top, preferred_element_type=F32) + jnp.where(d_edge >= 0, _bias_chain(jnp.maximum(d_edge, 0), tab), NEG)], axis=0)
        p_c = _softmax_cols(s_c)
        oc_scr[...] = own_rows(jnp.dot(vc_ref[...].T.astype(BF16), p_c.astype(BF16), preferred_element_type=F32))
        imp = _dot_exact_rhs(p_c, gsum_ref[...], terms=3)
        mask_scr[...] = _select_blocks(_dot_exact_lhs(pool_ref[...], imp), qpos, N_SEL)

        wk = win_ref[0:LANE, :].T
        d_w = wbuf + tok - lax.broadcasted_iota(jnp.int32, (wbuf, 1), 0)
        near = wbuf - LANE
        s_w = jnp.dot(wk.astype(BF16), top, preferred_element_type=F32)
        s_w = (jnp.concatenate([s_w[:near] + far_row, s_w[near:] + _bias_chain(d_w[near:], tab)], axis=0)
               + jnp.where(d_w < WINDOW, 0.0, NEG))
        reset()
        attend_update(s_w, win_ref[LANE:, :])
        wn = pad_rows(winn_ref[...])
        attend_update(jnp.dot(wn[:, :LANE].astype(BF16), top, preferred_element_type=F32) + new_bias, wn[:, LANE:].T)
        ow_scr[...] = own_rows(acc_scr[...]) / l_scr[...]
        reset()
        nearb_scr[...] = _bias_chain(LANE + tok - lax.broadcasted_iota(jnp.int32, (LANE, 1), 0), tab)

    k_tile = jnp.concatenate([pg[0:LANE, :].T for pg in pages], axis=0)
    vt_tile = jnp.concatenate([pg[LANE:, :] for pg in pages], axis=1)
    slab = mask_scr[pl.ds(pl.multiple_of(j * MASK_ROWS, MASK_ROWS), MASK_ROWS), :].astype(BF16)
    rhs = jnp.concatenate([top_scr[...], slab, jnp.zeros((LANE - MASK_ROWS, LANE), BF16)], axis=0)
    s = jnp.dot(jnp.concatenate([k_tile.astype(BF16), epat_ref[...]], axis=1), rhs, preferred_element_type=F32)
    near = tile_keys - LANE
    s = jnp.concatenate([s[:near] + far_row, s[near:] + jnp.where(j == pl.num_programs(1) - 1, nearb_scr[...], far_row)], axis=0)
    attend_update(s, vt_tile)

    @pl.when(j == pl.num_programs(1) - 1)
    def _():
        kn = pad_rows(kvn_ref[...])
        last_blk = past // SEL_BLOCK
        s_n = (jnp.dot(kn[:, 2 * LANE:3 * LANE].astype(BF16), top_scr[...], preferred_element_type=F32)
               + new_bias + mask_scr[last_blk:last_blk + 1, :])
        attend_update(s_n, kn[:, 3 * LANE:].T)
        o_s = own_rows(acc_scr[...]) / l_scr[...]
        g_t = pad_rows(g_ref[...]).T
        gate_rows = []
        for b in range(3):
            row = g_t[b * H_NSA:b * H_NSA + 1]
            for h in range(1, H_NSA):
                row = row + pltpu.roll(g_t[b * H_NSA + h:b * H_NSA + h + 1], TOK_PAD * h, axis=1)
            gate_rows.append(row)
        o_col = oc_scr[...] * gate_rows[0] + o_s * gate_rows[1] + ow_scr[...] * gate_rows[2]
        per_head = [o_col if h == 0 else pltpu.roll(o_col, LANE - TOK_PAD * h, axis=1) for h in range(H_NSA)]
        o_ref[...] = jnp.concatenate(per_head, axis=0).T[:TOK_PAD]


def _nsa_sample(pool, page_table, q, gates, kc, vc, kv_new, win_buf, win_new, rel_table, n_valid):
    B, n_pages_total = page_table.shape
    past = n_pages_total * PAGE_SIZE
    ncp = kc.shape[1]
    ns = past // SEL_BLOCK + 1
    nsp = -(-ns // MASK_ROWS) * MASK_ROWS
    col = np.arange(LANE)
    used = col < SMP_COLS
    gsum = jnp.asarray(((col[:, None] // (G_NSA * TOK_PAD) == col[None, :] // (G_NSA * TOK_PAD))
                        & (col[:, None] % TOK_PAD == col[None, :] % TOK_PAD) & used[:, None] & used[None, :]), BF16)
    tab_cols = jnp.pad(jnp.repeat(rel_table, TOK_PAD, axis=1), ((0, 0), (0, LANE - SMP_COLS)))
    consts = [tab_cols, _pool_matrix(nsp, ncp), gsum, _sel_pattern(SMP_PAGES * PAGE_SIZE, LANE)]
    per_seq = [q, gates, kc, vc, kv_new, win_buf, win_new]
    seq_spec = lambda a: pl.BlockSpec((None,) + a.shape[1:], lambda b, j, pt: (b,) + (0,) * (a.ndim - 1))
    const = lambda a: pl.BlockSpec(a.shape, lambda b, j, pt: (0,) * a.ndim)

    def page_map(p):
        return lambda b, j, pt: (pt[b, j * SMP_PAGES + p], 1, 0)

    grid_spec = pltpu.PrefetchScalarGridSpec(
        num_scalar_prefetch=1,
        grid=(B, n_pages_total // SMP_PAGES),
        in_specs=[pl.BlockSpec((None, 2 * LANE, PAGE_SIZE), page_map(p)) for p in range(SMP_PAGES)]
                 + [seq_spec(a) for a in per_seq] + [const(a) for a in consts],
        out_specs=pl.BlockSpec((None, TOK_PAD, H_NSA * HD), lambda b, j, pt: (b, 0, 0)),
        scratch_shapes=[pltpu.VMEM((LANE, LANE), BF16), pltpu.VMEM((nsp, LANE), F32), pltpu.VMEM((LANE, LANE), F32),
                        pltpu.VMEM((1, LANE), F32), pltpu.VMEM((1, LANE), F32), pltpu.VMEM((HD, LANE), F32), pltpu.VMEM((HD, LANE), F32),
                        pltpu.VMEM((LANE, LANE), F32)],
    )
    return pl.pallas_call(
        functools.partial(_nsa_sample_kernel, n_pages=SMP_PAGES, n_valid=n_valid, past=past),
        grid_spec=grid_spec,
        out_shape=jax.ShapeDtypeStruct((B, TOK_PAD, H_NSA * HD), F32),
        compiler_params=_params(("arbitrary", "arbitrary")),
        name="nsa_sample",
    )(page_table, *([pool] * SMP_PAGES), *per_seq, *consts)


def _outproj_kernel(x_ref, nsa_ref, rw_ref, gt_ref, lng_ref, lnb_ref, w_ref, o_ref):
    half = H_NSA * HD
    out = (jnp.dot(nsa_ref[...].astype(BF16), w_ref[0:half, :], preferred_element_type=F32)
           + jnp.dot(rw_ref[...].astype(BF16), w_ref[half:, :], preferred_element_type=F32))
    y = ALPHA * x_ref[...] + (1.0 + gt_ref[...]) * out
    o_ref[...] = _layer_norm(y, lng_ref[...], lnb_ref[...])


def _outproj(x, o_nsa, o_rwkv, gate, ln_g, ln_b, w_out):
    rows = x.shape[0]
    tm = min(512, rows)
    row = lambda i: (i, 0)
    return pl.pallas_call(
        _outproj_kernel,
        grid=(rows // tm,),
        in_specs=[pl.BlockSpec((tm, D_MODEL), row), pl.BlockSpec((tm, H_NSA * HD), row), pl.BlockSpec((tm, D_RWKV), row),
                  _mod_spec(gate, tm), _resident((1, D_MODEL)), _resident((1, D_MODEL)), _resident(w_out.shape)],
        out_specs=pl.BlockSpec((tm, D_MODEL), row),
        out_shape=jax.ShapeDtypeStruct((rows, D_MODEL), F32),
        compiler_params=_params(("arbitrary",)),
        name="outproj",
    )(x, o_nsa, o_rwkv, gate, ln_g.reshape(1, -1), ln_b.reshape(1, -1), w_out)


def kernel(x_prompt, x_sample, cache_nsa_kv, cache_nsa_win, state_rwkv_shift, state_rwkv_wkv, page_table, c_prompt, c_sample, rel_table, w_ada, b_ada, ln_g, ln_b, ffn1_gate, ffn1_up, ffn1_down, ffn2_gate, ffn2_up, ffn2_down, w_in, w_out, cmp_pe_k, cmp_w1_k, cmp_b1_k, cmp_w2_k, cmp_pe_v, cmp_w1_v, cmp_b1_v, cmp_w2_v, rwkv_mu, rwkv_w0, rwkv_w2, rwkv_a0, rwkv_a2, rwkv_g2, rwkv_k_k, rwkv_k_a, rwkv_r_k, rwkv_gn_w, rwkv_gn_b):
    assert w_ada.shape[0] == DEPTH == 1 and x_prompt.shape[0] == 1
    l = 0
    lw = dict(cmp_pe_k=cmp_pe_k[l], cmp_w1_k=cmp_w1_k[l], cmp_b1_k=cmp_b1_k[l], cmp_w2_k=cmp_w2_k[l],
              cmp_pe_v=cmp_pe_v[l], cmp_w1_v=cmp_w1_v[l], cmp_b1_v=cmp_b1_v[l], cmp_w2_v=cmp_w2_v[l],
              rwkv_mu=rwkv_mu[l], rwkv_w0=rwkv_w0[l], rwkv_w2=rwkv_w2[l], rwkv_a0=rwkv_a0[l], rwkv_a2=rwkv_a2[l], rwkv_g2=rwkv_g2[l],
              rwkv_k_k=rwkv_k_k[l], rwkv_k_a=rwkv_k_a[l], rwkv_r_k=rwkv_r_k[l], rwkv_gn_w=rwkv_gn_w[l], rwkv_gn_b=rwkv_gn_b[l])
    T = x_prompt.shape[1]
    nb, nt = x_sample.shape[0], x_sample.shape[1]
    assert nt <= TOK_PAD
    n_seq = 1 + nb
    c_all = jnp.concatenate([c_prompt, c_sample, jnp.zeros((-n_seq % 8, D_MODEL), F32)], axis=0)
    mod = _ada(c_all, w_ada[l], b_ada[l])
    mod_p = mod[0:1].reshape(9, 1, D_MODEL)
    mod_s = jnp.repeat(mod[1:n_seq].reshape(nb, 9, D_MODEL), nt, axis=0).transpose(1, 0, 2)
    ffn1 = [w[l].astype(BF16) for w in (ffn1_gate, ffn1_up, ffn1_down)]
    ffn2 = [w[l].astype(BF16) for w in (ffn2_gate, ffn2_up, ffn2_down)]
    w_in_p = _prep_w_in(w_in[l])
    w_out_b = w_out[l].astype(BF16)

    def trunk_in(x, m):
        x1 = _ffn(x, m[0], m[1], m[2], ln_g[l, 0], ln_b[l, 0], *ffn1)
        return x1, _proj(x1, m[3], m[4], w_in_p)

    def trunk_out(x1, o_nsa, o_rwkv, m):
        x2 = _outproj(x1, o_nsa, o_rwkv, m[5], ln_g[l, 1], ln_b[l, 1], w_out_b)
        return _ffn(x2, m[6], m[7], m[8], ln_g[l, 2], ln_b[l, 2], *ffn2)

    xp1, (q, kv, win, gates, pr) = trunk_in(x_prompt[0], mod_p)
    o_rw, wkv_p = _rwkv(pr[None], jnp.zeros((1, 1, RW_PAD), F32), jnp.zeros((1, H_RWKV, HD_RWKV, HD_RWKV), F32), lw, min(RW_STEP, T))
    n_rows = T // PAGE_SIZE
    kc, vc = _compress(kv.reshape(n_rows, PAGE_SIZE, 4 * LANE), jnp.arange(n_rows, dtype=jnp.int32)[None], lw, transposed=False)
    o_nsa = _nsa_prompt(q, gates, kv, win, kc[0], vc[0], _band(rel_table), rel_table)
    y_prompt = trunk_out(xp1, o_nsa, o_rw[0], mod_p)
    kv_prompt = kv.reshape(1, 1, T, 4, N_KV, HD)
    win_prompt = win[T - min(WINDOW, T):].reshape(1, 1, -1, 2, N_KV, HD)
    shift_prompt = _rwkv_uncols(pr[T - 1]).reshape(1, 1, RWKV_COLS)

    xs1, (q_s, kv_s, win_s, gates_s, pr_s) = trunk_in(x_sample.reshape(nb * nt, D_MODEL), mod_s)
    tokens = lambda a: jnp.pad(a.reshape(nb, nt, -1), ((0, 0), (0, TOK_PAD - nt), (0, 0)))
    pr_pad = jnp.pad(pr_s.reshape(nb, nt, -1), ((0, 0), (0, RW_TOK_PAD - nt), (0, 0)))
    o_rw_s, wkv_s = _rwkv(pr_pad, _rwkv_cols(state_rwkv_shift[l])[:, None], state_rwkv_wkv[l], lw, nt)
    pool_t = jnp.transpose(cache_nsa_kv[l], (0, 2, 3, 4, 1)).reshape(-1, 4 * LANE, PAGE_SIZE)
    kc_s, vc_s = _compress(pool_t, page_table, lw, transposed=True)
    win_buf = cache_nsa_win[l]
    win_t = jnp.transpose(win_buf, (0, 2, 3, 4, 1)).reshape(nb, 2 * LANE, -1)
    o_nsa_s = _nsa_sample(pool_t, page_table, tokens(q_s), tokens(gates_s), kc_s, vc_s, tokens(kv_s),
                          win_t, tokens(win_s), rel_table, nt)
    y_sample = trunk_out(xs1, o_nsa_s[:, :nt].reshape(nb * nt, -1), o_rw_s[:, :nt].reshape(nb * nt, -1), mod_s)
    kv_sample = kv_s.reshape(1, nb, nt, 4, N_KV, HD)
    win_sample = jnp.concatenate([win_buf, win_s.reshape(nb, nt, 2, N_KV, HD)], axis=1)[None, :, nt:]
    shift_sample = _rwkv_uncols(pr_s.reshape(nb, nt, -1)[:, -1])[None]
    return (y_prompt[None], y_sample.reshape(nb, nt, D_MODEL), kv_prompt, win_prompt, shift_prompt, wkv_p[None],
            kv_sample, win_sample, shift_sample, wkv_s[None])
```

```python
import functools
import math

import numpy as np
import jax
import jax.numpy as jnp
from jax import lax
from jax.experimental import pallas as pl
from jax.experimental.pallas import tpu as pltpu

D_MODEL = 1024
PAGE_SIZE = 128
H_NSA = 8
N_KV = 2
G_NSA = H_NSA // N_KV
HD = 64
CMP_STRIDE = 16
CMP_BLOCK = 2 * CMP_STRIDE
CMP_HIDDEN = 256
SEL_BLOCK = 64
N_SEL = 16
WINDOW = 512
Q_BLOCK = 128
N_BUCKETS = 32
MAX_DISTANCE = 128
H_RWKV = 8
HD_RWKV = 64
D_RWKV = H_RWKV * HD_RWKV
DECAY_LORA = 32
AAA_LORA = 32
GATE_LORA = 96
GN_EPS = 64e-5
D_FF = 2816
LN_EPS = 1e-5
DEPTH = 1
ALPHA = (2 * DEPTH) ** 0.25

NSA_SIZES = (H_NSA * HD,) + (N_KV * HD,) * 6 + (H_NSA * 3,)
RWKV_SIZES = (D_RWKV, D_RWKV, D_RWKV, DECAY_LORA, AAA_LORA, GATE_LORA)
NSA_COLS = sum(NSA_SIZES)
RWKV_COLS = sum(RWKV_SIZES)

F32 = jnp.float32
BF16 = jnp.bfloat16
LANE = 128
NEG = -(2.0 ** 100)
M_INIT = -(2.0 ** 103)
VMEM_LIMIT = 56 * 1024 * 1024

RW_PAD = 3 * D_RWKV + 3 * LANE
P_Q, P_KV, P_WIN, P_GATE, P_RW = 0, 512, 1024, 1280, 1408
P_COLS = P_RW + RW_PAD
KEY_TILE = 512
MASK_ROWS = 16
V_ROWS = 144
MASK_ROW0 = N_KV * HD
SEL_MASK0 = HD
SEL_FAR0 = SEL_MASK0 + MASK_ROWS
V_ROWS_KV = HD + MASK_ROWS


def _bucket_lows():
    d = np.arange(0, 4 * MAX_DISTANCE, dtype=np.int64)
    max_exact = N_BUCKETS // 2
    df = np.maximum(d, 1).astype(np.float32)
    large = max_exact + (np.log(df / np.float32(max_exact)) / np.float32(math.log(MAX_DISTANCE / max_exact))
                         * np.float32(N_BUCKETS - max_exact)).astype(np.int32)
    b = np.where(d < max_exact, d, np.minimum(large, N_BUCKETS - 1))
    lows = [int(np.argmax(b >= k)) for k in range(N_BUCKETS)]
    return b, lows


_BUCKET_OF, _BUCKET_LOW = _bucket_lows()
FAR_DIST = _BUCKET_LOW[N_BUCKETS - 1]


def _resident(shape):
    nd = len(shape)
    return pl.BlockSpec(shape, lambda *_: (0,) * nd, pipeline_mode=pl.Buffered(1))


def _params(sem):
    return pltpu.CompilerParams(dimension_semantics=sem, vmem_limit_bytes=VMEM_LIMIT)


def _split2(x):
    hi = x.astype(BF16)
    lo = (x - hi.astype(F32)).astype(BF16)
    return hi, lo


def _dot_exact_rhs(x, rhs_bf16, terms=2):
    acc = None
    rem = x
    for _ in range(terms):
        part = rem.astype(BF16)
        d = jnp.dot(part, rhs_bf16, preferred_element_type=F32)
        acc = d if acc is None else acc + d
        rem = rem - part.astype(F32)
    return acc


def _dot_exact_lhs(lhs_bf16, x, terms=3):
    acc = None
    rem = x
    for _ in range(terms):
        part = rem.astype(BF16)
        d = jnp.dot(lhs_bf16, part, preferred_element_type=F32)
        acc = d if acc is None else acc + d
        rem = rem - part.astype(F32)
    return acc


def _layer_norm(y, g, b):
    mu = jnp.mean(y, axis=-1, keepdims=True)
    yc = y - mu
    var = jnp.mean(yc * yc, axis=-1, keepdims=True)
    return yc * lax.rsqrt(var + LN_EPS) * g + b


def _bias_chain(d, tab_rows):
    out = tab_rows[0] + jnp.zeros(d.shape, F32)
    for b in range(1, N_BUCKETS):
        out = jnp.where(d >= _BUCKET_LOW[b], tab_rows[b], out)
    return out


def _ada_kernel(c_ref, w_ref, b_ref, o_ref):
    c = c_ref[...]
    h = (c * jax.nn.sigmoid(c)).astype(BF16)
    o_ref[...] = jnp.dot(h, w_ref[...].astype(BF16), preferred_element_type=F32) + b_ref[...]


def _ada(c_all, w_ada, b_ada):
    rows, n = c_all.shape[0], w_ada.shape[1]
    tn = 1152
    return pl.pallas_call(
        _ada_kernel,
        grid=(n // tn,),
        in_specs=[pl.BlockSpec((rows, D_MODEL), lambda j: (0, 0)),
                  pl.BlockSpec((D_MODEL, tn), lambda j: (0, j)),
                  pl.BlockSpec((1, tn), lambda j: (0, j))],
        out_specs=pl.BlockSpec((rows, tn), lambda j: (0, j)),
        out_shape=jax.ShapeDtypeStruct((rows, n), F32),
        compiler_params=_params(("arbitrary",)),
        name="ada",
    )(c_all, w_ada, b_ada.reshape(1, n))


FF_CHUNKS = 2


def _ffn_kernel(x_ref, sh_ref, sc_ref, gt_ref, lng_ref, lnb_ref, wg_ref, wu_ref, wd_ref, o_ref):
    x = x_ref[...]
    h = (x * (1.0 + sc_ref[...]) + sh_ref[...]).astype(BF16)
    ck = D_FF // FF_CHUNKS
    acc = jnp.zeros(x.shape, F32)
    for c in range(FF_CHUNKS):
        a = jnp.dot(h, wg_ref[:, c * ck:(c + 1) * ck], preferred_element_type=F32)
        b = jnp.dot(h, wu_ref[:, c * ck:(c + 1) * ck], preferred_element_type=F32)
        t = (a * jax.nn.sigmoid(a) * b).astype(BF16)
        acc = acc + jnp.dot(t, wd_ref[c * ck:(c + 1) * ck, :], preferred_element_type=F32)
    y = ALPHA * x + (1.0 + gt_ref[...]) * (0.5 * acc)
    o_ref[...] = _layer_norm(y, lng_ref[...], lnb_ref[...])


def _mod_spec(mod, tm):
    if mod.shape[0] == 1:
        return pl.BlockSpec((1, D_MODEL), lambda i: (0, 0))
    return pl.BlockSpec((tm, D_MODEL), lambda i: (i, 0))


def _ffn(x, shift, scale, gate, ln_g, ln_b, wg, wu, wd):
    rows = x.shape[0]
    tm = min(512, rows)
    row = lambda i: (i, 0)
    return pl.pallas_call(
        _ffn_kernel,
        grid=(rows // tm,),
        in_specs=[pl.BlockSpec((tm, D_MODEL), row), _mod_spec(shift, tm), _mod_spec(scale, tm), _mod_spec(gate, tm),
                  _resident((1, D_MODEL)), _resident((1, D_MODEL)),
                  _resident((D_MODEL, D_FF)), _resident((D_MODEL, D_FF)), _resident((D_FF, D_MODEL))],
        out_specs=pl.BlockSpec((tm, D_MODEL), row),
        out_shape=jax.ShapeDtypeStruct((rows, D_MODEL), F32),
        compiler_params=_params(("arbitrary",)),
        name="ffn",
    )(x, shift, scale, gate, ln_g.reshape(1, -1), ln_b.reshape(1, -1), wg, wu, wd)


def _proj_kernel(x_ref, sh_ref, sc_ref, w_ref, q_ref, kv_ref, win_ref, g_ref, pr_ref):
    h = (x_ref[...] * (1.0 + sc_ref[...]) + sh_ref[...]).astype(BF16)
    p = jnp.dot(h, w_ref[...], preferred_element_type=F32)
    q_ref[...] = p[:, P_Q:P_KV]
    kv_ref[...] = p[:, P_KV:P_WIN]
    win_ref[...] = p[:, P_WIN:P_GATE]
    g_ref[...] = jax.nn.sigmoid(p[:, P_GATE:P_RW])
    pr_ref[...] = p[:, P_RW:P_COLS]


def _proj(x, shift, scale, w_in_p):
    rows = x.shape[0]
    tm = min(512, rows)
    row = lambda i: (i, 0)
    widths = (512, 512, 256, LANE, RW_PAD)
    return pl.pallas_call(
        _proj_kernel,
        grid=(rows // tm,),
        in_specs=[pl.BlockSpec((tm, D_MODEL), row), _mod_spec(shift, tm), _mod_spec(scale, tm),
                  _resident((D_MODEL, P_COLS))],
        out_specs=[pl.BlockSpec((tm, w), row) for w in widths],
        out_shape=[jax.ShapeDtypeStruct((rows, w), F32) for w in widths],
        compiler_params=_params(("arbitrary",)),
        name="proj",
    )(x, shift, scale, w_in_p)


def _prep_w_in(w_in):
    pad = lambda a, n: jnp.pad(a, ((0, 0), (0, n - a.shape[1])))
    nsa, rw = w_in[:, :NSA_COLS], w_in[:, NSA_COLS:]
    gl = nsa[:, 1280:1304].reshape(D_MODEL, H_NSA, 3).transpose(0, 2, 1).reshape(D_MODEL, 3 * H_NSA)
    cols = [nsa[:, :1280], pad(gl, LANE), _rwkv_cols(rw)]
    return jnp.concatenate(cols, axis=1).astype(BF16)


def _rwkv_cols(a):
    pad = lambda t: jnp.pad(t, [(0, 0)] * (t.ndim - 1) + [(0, LANE - t.shape[-1])])
    n = 3 * D_RWKV
    return jnp.concatenate([a[..., :n], pad(a[..., n:n + 32]), pad(a[..., n + 32:n + 64]), pad(a[..., n + 64:n + 160])], axis=-1)


def _rwkv_uncols(a):
    n = 3 * D_RWKV
    return jnp.concatenate([a[..., :n], a[..., n:n + 32], a[..., n + LANE:n + LANE + 32], a[..., n + 2 * LANE:n + 2 * LANE + 96]], axis=-1)


RW_GROUP = 64
RW_TOK_PAD = 16
RW_PAIRS = H_RWKV // 2
RW_BLOCK = 64
RW_STEP = 128


def _lora(x, w_ref):
    w = w_ref[...]
    w_hi = w.astype(BF16)
    w_lo = (w - w_hi.astype(F32)).astype(BF16)
    return _dot_exact_rhs(x, w_hi) + jnp.dot(x.astype(BF16), w_lo, preferred_element_type=F32)


def _rwkv_kernel(pr_ref, sh0_ref, s0_ref, mu_ref, w0_ref, a0_ref, kk_ref, ka_ref, rk_ref, gw_ref, gb_ref,
                 w2_ref, a2_ref, g2_ref, bo_ref, lgrp_ref, ggrp_ref, o_ref, sout_ref,
                 prev_scr, s_scr, *, n_valid):
    tb = pr_ref.shape[0]
    step = pl.program_id(1)

    @pl.when(step == 0)
    def _():
        prev_scr[...] = sh0_ref[...]
        s_scr[...] = s0_ref[...]

    p = pr_ref[...]
    rows = lax.broadcasted_iota(jnp.int32, (tb, 1), 0)
    prev = jnp.where(rows == 0, prev_scr[...], pltpu.roll(p, 1, axis=0))
    prev_scr[...] = p[tb - 1:tb, :]
    xs = p + (prev - p) * mu_ref[...]
    n = D_RWKV
    r, k, v = xs[:, :n], xs[:, n:2 * n], xs[:, 2 * n:3 * n]
    wl, al, gl = xs[:, 3 * n:3 * n + LANE], xs[:, 3 * n + LANE:3 * n + 2 * LANE], xs[:, 3 * n + 2 * LANE:]
    z = -(w0_ref[...] + _lora(jnp.tanh(wl), w2_ref))
    w = -(jnp.maximum(z, 0.0) + jnp.log(1.0 + jnp.exp(-jnp.abs(z)))) - 0.5
    a = jax.nn.sigmoid(a0_ref[...] + _lora(al, a2_ref))
    g = _lora(jax.nn.sigmoid(gl), g2_ref)
    kk = k * kk_ref[...]
    ss = _dot_exact_rhs(kk * kk, bo_ref[...])
    kk = kk / jnp.maximum(jnp.sqrt(ss), 1e-12)
    k2 = k * (1.0 + (a - 1.0) * ka_ref[...])
    G = min(RW_GROUP, tb)
    log_dec = -jnp.exp(w)
    bet = kk * a
    if n_valid < tb:
        live = rows < n_valid
        log_dec, kk, bet, k2, v_in = (jnp.where(live, x, 0.0) for x in (log_dec, kk, bet, k2, v))
    else:
        v_in = v
    cum = _dot_exact_lhs(lgrp_ref[...], log_dec, terms=2)
    cum_end = _dot_exact_lhs(ggrp_ref[...], log_dec, terms=2)
    gam_inv = jnp.exp(-cum)
    gam_end = jnp.exp(cum_end - cum)
    k_hat = -kk * jnp.exp(cum - log_dec)
    r_hat = r * jnp.exp(cum)
    b_chk, k_chk = bet * gam_inv, k2 * gam_inv
    b_til, k_til = bet * gam_end, k2 * gam_end
    gam_group = jnp.exp(cum_end)

    lane = lax.broadcasted_iota(jnp.int32, (1, LANE), 1)
    low = lane < HD_RWKV
    lane_t = lane & (RW_BLOCK - 1)
    row = lax.broadcasted_iota(jnp.int32, (LANE, 1), 0)
    row_t = row & (RW_BLOCK - 1)
    same = ((row < HD_RWKV) == low) & ((row_t // G) == (lane_t // G))
    strict, incl = same & (lane_t < row_t), same & (lane_t <= row_t)
    bf = lambda x: x.astype(BF16)
    mm = lambda x, y: jnp.dot(bf(x), bf(y), preferred_element_type=F32)
    mm_nt = lambda x, y: lax.dot_general(bf(x), bf(y), (((1,), (1,)), ((), ())), preferred_element_type=F32)

    def rows_bd(x):
        if tb < RW_BLOCK:
            x = jnp.concatenate([x, jnp.zeros((RW_BLOCK - tb, LANE), F32)], axis=0)
        return jnp.concatenate([jnp.where(low, x, 0.0), jnp.where(low, 0.0, x)], axis=0)

    def mm3(x, y):
        xh, yh = bf(x), bf(y)
        xl, yl = bf(x - xh.astype(F32)), bf(y - yh.astype(F32))
        return jnp.dot(jnp.concatenate([xh, xl, xh], axis=1), jnp.concatenate([yh, yh, yl], axis=0), preferred_element_type=F32)

    units = max(1, tb // RW_BLOCK)
    unit_rows = min(tb, RW_BLOCK)
    pairs = range(RW_PAIRS)
    items = [(un, pp) for un in range(units) for pp in pairs]
    at = lambda x, it: x[it[0] * unit_rows:(it[0] + 1) * unit_rows, it[1] * LANE:(it[1] + 1) * LANE]
    kh_row = [rows_bd(at(k_hat, it)) for it in items]
    rh_row = [rows_bd(at(r_hat, it)) for it in items]
    kh_mat = [x.T for x in kh_row]
    rh_mat = [x.T for x in rh_row]
    state_in = [jnp.concatenate([rows_bd(at(b_chk, it)).T, rows_bd(at(k_chk, it)).T], axis=1) for it in items]
    upd_rows = [jnp.concatenate([rows_bd(at(b_til, it)), rows_bd(at(k_til, it))], axis=0) for it in items]
    v_t = [rows_bd(at(v_in, it)).T for it in items]
    v_t = [x[:HD_RWKV] + x[HD_RWKV:] for x in v_t]
    c_all = [mm(jnp.concatenate([kh_row[n], rh_row[n]], axis=0), state_in[n]) for n in range(len(items))]
    c_uu = [jnp.where(strict, c[:LANE, :LANE], 0.0) for c in c_all]
    c_uv = [jnp.where(strict, c[:LANE, LANE:], 0.0) for c in c_all]
    c_ru = [jnp.where(incl, c[LANE:, :LANE], 0.0) for c in c_all]
    c_rv = [jnp.where(incl, c[LANE:, LANE:], 0.0) for c in c_all]
    t_neu, power = list(c_uu), list(c_uu)
    span = 2
    while span < G:
        power = [mm(x, x) for x in power]
        t_neu = [t_neu[n] + power[n] + mm(t_neu[n], power[n]) for n in range(len(items))]
        span *= 2
    from_v = [mm_nt(v_t[n], c_uv[n]) for n in range(len(items))]
    st = [s_scr[pp] for pp in pairs]
    y_units = []
    for un in range(units):
        ns = [un * RW_PAIRS + pp for pp in pairs]
        y_t = [jnp.zeros((HD_RWKV, LANE), F32) for _ in pairs]
        for grp in range(unit_rows // G):
            here = (lane_t // G) == grp
            first = un * unit_rows + grp * G
            w_t = [jnp.where(here, mm(st[pp], kh_mat[ns[pp]]) + from_v[ns[pp]], 0.0) for pp in pairs]
            u_t = [w_t[pp] + mm_nt(w_t[pp], t_neu[ns[pp]]) for pp in pairs]
            v_g = [jnp.where(here, v_t[ns[pp]], 0.0) for pp in pairs]
            y_t = [y_t[pp] + jnp.where(here, mm(st[pp], rh_mat[ns[pp]]), 0.0) + mm_nt(u_t[pp], c_ru[ns[pp]])
                   + mm_nt(v_g[pp], c_rv[ns[pp]]) for pp in pairs]
            st = [st[pp] * gam_group[first:first + 1, pp * LANE:(pp + 1) * LANE]
                  + mm3(jnp.concatenate([u_t[pp], v_g[pp]], axis=1), upd_rows[ns[pp]]) for pp in pairs]
        lane_u = lax.broadcasted_iota(jnp.int32, (unit_rows, LANE), 1)
        pieces = []
        for pp in pairs:
            yt = jnp.concatenate([y_t[pp], jnp.zeros((LANE - HD_RWKV, LANE), F32)], axis=0).T
            pieces.append(jnp.where(lane_u < HD_RWKV, yt[:unit_rows], pltpu.roll(yt[RW_BLOCK:RW_BLOCK + unit_rows], HD_RWKV, axis=1)))
        y_units.append(jnp.concatenate(pieces, axis=1))
    for pp in pairs:
        s_scr[pp] = st[pp]
    y = y_units[0] if units == 1 else jnp.concatenate(y_units, axis=0)
    mean = _dot_exact_rhs(y, bo_ref[...]) * (1.0 / HD_RWKV)
    yc = y - mean
    var = _dot_exact_rhs(yc * yc, bo_ref[...]) * (1.0 / HD_RWKV)
    yn = yc * lax.rsqrt(var + GN_EPS) * gw_ref[...] + gb_ref[...]
    bonus = _dot_exact_rhs(r * k2 * rk_ref[...], bo_ref[...]) * v
    o_ref[...] = (yn + bonus) * g
    sout_ref[...] = s_scr[...]


def _pair_state(s):
    B = s.shape[0]
    return s.reshape(B, RW_PAIRS, 2, HD_RWKV, HD_RWKV).transpose(0, 1, 3, 2, 4).reshape(B, RW_PAIRS, HD_RWKV, LANE)


def _unpair_state(s):
    B = s.shape[0]
    return s.reshape(B, RW_PAIRS, HD_RWKV, 2, HD_RWKV).transpose(0, 1, 3, 2, 4).reshape(B, H_RWKV, HD_RWKV, HD_RWKV)


def _rwkv(pr, shift0, s0, lw, n_valid):
    B, T, _ = pr.shape
    tb = min(RW_STEP, T)
    n = D_RWKV
    vec = lambda a: a.reshape(1, n)
    padrow = lambda a: jnp.pad(a, ((0, LANE - a.shape[0]), (0, 0)))
    blk = np.arange(n) // HD_RWKV
    block_ones = jnp.asarray(blk[:, None] == blk[None, :], BF16)
    tok = np.arange(tb)
    group = min(RW_GROUP, tb)
    same_group = tok[:, None] // group == tok[None, :] // group
    prefix = jnp.asarray(same_group & (tok[None, :] <= tok[:, None]), BF16)
    consts = [_rwkv_cols(lw['rwkv_mu']).reshape(1, RW_PAD), vec(lw['rwkv_w0']), vec(lw['rwkv_a0']), vec(lw['rwkv_k_k']),
              vec(lw['rwkv_k_a']), vec(lw['rwkv_r_k']), vec(lw['rwkv_gn_w']), vec(lw['rwkv_gn_b']),
              padrow(lw['rwkv_w2']), padrow(lw['rwkv_a2']), padrow(lw['rwkv_g2']), block_ones, prefix, jnp.asarray(same_group, BF16)]
    kern = functools.partial(_rwkv_kernel, n_valid=n_valid)
    state_spec = pl.BlockSpec((None, RW_PAIRS, HD_RWKV, LANE), lambda b, j: (b, 0, 0, 0))
    o, s = pl.pallas_call(
        kern,
        grid=(B, T // tb),
        in_specs=[pl.BlockSpec((None, tb, RW_PAD), lambda b, j: (b, j, 0)),
                  pl.BlockSpec((None, 1, RW_PAD), lambda b, j: (b, 0, 0)), state_spec]
                 + [_resident(c.shape) for c in consts],
        out_specs=[pl.BlockSpec((None, tb, n), lambda b, j: (b, j, 0)), state_spec],
        out_shape=[jax.ShapeDtypeStruct((B, T, n), F32), jax.ShapeDtypeStruct((B, RW_PAIRS, HD_RWKV, LANE), F32)],
        scratch_shapes=[pltpu.VMEM((1, RW_PAD), F32), pltpu.VMEM((RW_PAIRS, HD_RWKV, LANE), F32)],
        compiler_params=_params(("arbitrary", "arbitrary")),
        name="rwkv",
    )(pr, shift0, _pair_state(s0), *consts)
    return o, _unpair_state(s)


CMP_PAGES = 32
CHUNKS_PER_PAGE = PAGE_SIZE // CMP_STRIDE
CMP_K = CMP_STRIDE * N_KV * HD


def _compress_kernel(pt_ref, *refs, n_pages, transposed):
    weights = refs[2 * n_pages + 2:2 * n_pages + 10]
    outs = refs[2 * n_pages + 10:2 * n_pages + 12]
    rows_scr = refs[2 * n_pages + 12]
    rows = CHUNKS_PER_PAGE * n_pages
    for kind in range(2):
        pages, nxt = refs[kind * n_pages:(kind + 1) * n_pages], refs[2 * n_pages + kind]
        pe_ref, w_ref, b_ref, w2_ref = weights[4 * kind:4 * kind + 4]
        for p, pg in enumerate(pages):
            rows_scr[p * PAGE_SIZE:(p + 1) * PAGE_SIZE, :] = pg[...].T if transposed else pg[...]
        nxt_rows = nxt[...].T[:CMP_STRIDE] if transposed else nxt[...]
        x = jnp.concatenate([rows_scr[pl.ds(s, rows, stride=CMP_STRIDE), :] for s in range(CMP_STRIDE)], axis=1)
        xn = jnp.concatenate([nxt_rows[s:s + 1, :] for s in range(CMP_STRIDE)], axis=1)
        x_ext = jnp.concatenate([x, jnp.broadcast_to(xn, (8, CMP_K))], axis=0)
        h_first = jnp.dot((x + pe_ref[0]).astype(BF16), w_ref[0], preferred_element_type=F32)
        h_second = jnp.dot((x_ext + pe_ref[1]).astype(BF16), w_ref[1], preferred_element_type=F32)
        h_next = pltpu.roll(h_second, rows + 8 - 1, axis=0)[:rows]
        hidden = jax.nn.gelu(h_first + h_next + b_ref[...])
        outs[kind][...] = jnp.dot(hidden.astype(BF16), w2_ref[...], preferred_element_type=F32)


def _compress_weights(pe, w1, b1, w2):
    eye = jnp.eye(N_KV, dtype=F32)
    halves = []
    for half in range(2):
        w = w1[half * CMP_STRIDE:(half + 1) * CMP_STRIDE]
        halves.append(jnp.einsum('sdn,hg->shdgn', w, eye).reshape(CMP_K, N_KV * CMP_HIDDEN))
    pe2 = jnp.stack([jnp.broadcast_to(pe[half * CMP_STRIDE:(half + 1) * CMP_STRIDE, None, :], (CMP_STRIDE, N_KV, HD)).reshape(1, CMP_K)
                     for half in range(2)])
    w2bd = jnp.einsum('nd,hg->hngd', w2, eye).reshape(N_KV * CMP_HIDDEN, N_KV * HD)
    return [pe2, jnp.stack(halves).astype(BF16), jnp.tile(b1, N_KV).reshape(1, -1), w2bd.astype(BF16)]


def _compress(pool, page_table, lw, transposed):
    B, n_pages_total = page_table.shape
    n_pages = min(CMP_PAGES, n_pages_total)
    rows = CHUNKS_PER_PAGE * n_pages
    weights = (_compress_weights(lw['cmp_pe_k'], lw['cmp_w1_k'], lw['cmp_b1_k'], lw['cmp_w2_k'])
               + _compress_weights(lw['cmp_pe_v'], lw['cmp_w1_v'], lw['cmp_b1_v'], lw['cmp_w2_v']))
    width = N_KV * HD
    at = (lambda page, kind: (page, kind, 0)) if transposed else (lambda page, kind: (page, 0, kind))

    def page_map(p, kind):
        return lambda b, j, pt: at(pt[b, j * n_pages + p], kind)

    def next_map(kind):
        return lambda b, j, pt: at(pt[b, jnp.minimum((j + 1) * n_pages, n_pages_total - 1)], kind)

    const = lambda a: pl.BlockSpec(a.shape, lambda b, j, pt: (0,) * a.ndim)
    out_spec = pl.BlockSpec((None, rows, N_KV * HD), lambda b, j, pt: (b, j, 0))
    out_shape = jax.ShapeDtypeStruct((B, n_pages_total * CHUNKS_PER_PAGE, N_KV * HD), F32)
    next_rows = PAGE_SIZE if transposed else CMP_STRIDE
    grid_spec = pltpu.PrefetchScalarGridSpec(
        num_scalar_prefetch=1,
        grid=(B, n_pages_total // n_pages),
        in_specs=[pl.BlockSpec((None, PAGE_SIZE, width), page_map(p, kind)) for kind in range(2) for p in range(n_pages)]
                 + [pl.BlockSpec((None, next_rows, width), next_map(kind)) for kind in range(2)] + [const(a) for a in weights],
        out_specs=[out_spec, out_spec],
        scratch_shapes=[pltpu.VMEM((n_pages * PAGE_SIZE, width), F32)],
    )
    return pl.pallas_call(
        functools.partial(_compress_kernel, n_pages=n_pages, transposed=transposed),
        grid_spec=grid_spec,
        out_shape=[out_shape, out_shape],
        compiler_params=_params(("arbitrary", "arbitrary")),
        name="compress",
    )(page_table, *([pool] * (2 * n_pages + 2)), *weights)


BAND_ROWS = 1152


def _band_kernel(tab_ref, bkt_ref, o_ref):
    h = pl.program_id(0)
    bkt = bkt_ref[...]
    out = jnp.full(bkt.shape, NEG, F32)
    for b in range(N_BUCKETS):
        out = jnp.where(bkt == b, tab_ref[b, h], out)
    o_ref[...] = out


def _band(rel_table):
    u = np.arange(BAND_ROWS)[:, None]
    qi = np.arange(Q_BLOCK)[None, :]
    d = qi + WINDOW - u
    bkt = np.where(d >= 0, _BUCKET_OF[np.clip(d, 0, len(_BUCKET_OF) - 1)], -1).astype(np.int32)
    return pl.pallas_call(
        _band_kernel,
        grid=(H_NSA,),
        in_specs=[pl.BlockSpec(memory_space=pltpu.SMEM), pl.BlockSpec((BAND_ROWS, Q_BLOCK), lambda h: (0, 0))],
        out_specs=pl.BlockSpec((None, BAND_ROWS, Q_BLOCK), lambda h: (h, 0, 0)),
        out_shape=jax.ShapeDtypeStruct((H_NSA, BAND_ROWS, Q_BLOCK), F32),
        compiler_params=_params(("arbitrary",)),
        name="band",
    )(rel_table, jnp.asarray(bkt))


def _softmax_cols(s):
    m = jnp.max(s, axis=0, keepdims=True)
    e = jnp.exp(s - m)
    l = jnp.sum(e, axis=0, keepdims=True)
    return e * jnp.where(m > 0.5 * NEG, 1.0 / l, 0.0)


def _select_blocks(impsel, qpos, n_pick):
    ns = impsel.shape[0]
    blk = lax.broadcasted_iota(jnp.int32, impsel.shape, 0)
    cur = jnp.right_shift(qpos, 6)
    future = blk * SEL_BLOCK > qpos
    forced = (blk == 0) | (blk == cur) | (blk == cur - 1)
    score = jnp.where(future, -jnp.inf, jnp.where(forced, jnp.inf, impsel))
    chosen = jnp.zeros(impsel.shape, F32)
    for _ in range(n_pick):
        best = jnp.max(score, axis=0, keepdims=True)
        first = jnp.min(jnp.where(score == best, blk, ns), axis=0, keepdims=True)
        hit = (blk == first) & (best > -jnp.inf)
        chosen = jnp.where(hit, 1.0, chosen)
        score = jnp.where(hit, -jnp.inf, score)
    return jnp.where(chosen > 0.0, 0.0, NEG)


def _pool_matrix(ns, nc):
    j = np.arange(ns)[:, None]
    n = np.arange(nc)[None, :]
    ratio = SEL_BLOCK // CMP_STRIDE
    return jnp.asarray((n >= ratio * j - 1) & (n <= ratio * j + ratio - 1), BF16)


def _nsa_prompt_kernel(tab_ref, q_ref, g_ref, kc_ref, vct_ref, ks_ref, vst_ref, kw_ref, vwt_ref, band_ref, pool_ref, o_ref,
                       rhs_scr, mask_scr, acc_scr, m_scr, sc_scr, sa_scr, sb_scr):
    i = pl.program_id(0)
    ncp = kc_ref.shape[0]
    ns = pool_ref.shape[0]
    s0 = i * Q_BLOCK
    q_t = (q_ref[...] * HD ** -0.5).T
    g_t = g_ref[...].T
    lane_q = lax.broadcasted_iota(jnp.int32, (1, G_NSA * Q_BLOCK), 1) & (Q_BLOCK - 1)
    qpos = s0 + lax.broadcasted_iota(jnp.int32, (1, Q_BLOCK), 1)
    rhs_scr[...] = jnp.zeros(rhs_scr.shape, BF16)
    kvs = range(N_KV)
    kd = i // 4
    r = i % 4
    lanes4 = lambda k, f: jnp.concatenate([f(G_NSA * k + g) for g in range(G_NSA)], axis=1)
    qcols = [lanes4(k, lambda h: q_t[h * HD:(h + 1) * HD, :]) for k in kvs]
    zero = jnp.zeros_like(qcols[0])
    top = [jnp.concatenate([qcols[0], zero], axis=0).astype(BF16), jnp.concatenate([zero, qcols[1]], axis=0).astype(BF16)]
    far_row = [lanes4(k, lambda h: band_ref[h, 0:1, :]) for k in kvs]


    n0 = pl.multiple_of(jnp.clip(8 * i - 16, 0, ncp - 32), 8)
    nrow = lax.broadcasted_iota(jnp.int32, (ncp, 1), 0)
    d_edge = qpos - (CMP_STRIDE * (n0 + lax.broadcasted_iota(jnp.int32, (32, 1), 0)) + CMP_BLOCK - 1)

    def compressed(k):
        sc_scr[k] = jnp.dot(kc_ref[...], top[k], preferred_element_type=F32) + jnp.where(nrow < n0, far_row[k], NEG)
        edge_bias = lanes4(k, lambda h: jnp.where(d_edge >= 0, _bias_chain(d_edge, [tab_ref[b, h] for b in range(N_BUCKETS)]), NEG))
        sc_scr[k, pl.ds(n0, 32), :] = jnp.dot(kc_ref[pl.ds(n0, 32), :], top[k], preferred_element_type=F32) + edge_bias
        p_c = _softmax_cols(sc_scr[k])
        o_c = jnp.dot(vct_ref[...], p_c.astype(BF16), preferred_element_type=F32)[k * HD:(k + 1) * HD]
        imp = p_c[:, 0:Q_BLOCK]
        for g in range(1, G_NSA):
            imp = imp + p_c[:, g * Q_BLOCK:(g + 1) * Q_BLOCK]
        return o_c, _dot_exact_lhs(pool_ref[...], imp, terms=2)

    comp = [compressed(k) for k in kvs]

    ws = pl.multiple_of(jnp.maximum(s0 - WINDOW, 0), Q_BLOCK)
    u0 = pl.multiple_of(WINDOW - (s0 - ws), Q_BLOCK)
    n_win = WINDOW + Q_BLOCK
    u = u0 + lax.broadcasted_iota(jnp.int32, (n_win, 1), 0)
    win_mask = jnp.where(u > lane_q, 0.0, NEG)

    def window(k):
        s_w = (jnp.dot(kw_ref[pl.ds(ws, n_win), :], top[k], preferred_element_type=F32)
               + lanes4(k, lambda h: band_ref[h, pl.ds(u0, n_win), :]) + win_mask)
        m_w = jnp.max(s_w, axis=0, keepdims=True)
        p_w = jnp.exp(s_w - m_w).astype(BF16)
        acc_w = jnp.zeros((V_ROWS, G_NSA * Q_BLOCK), F32)
        for j in range(n_win // Q_BLOCK):
            acc_w = acc_w + jnp.dot(vwt_ref[ws // Q_BLOCK + j], p_w[j * Q_BLOCK:(j + 1) * Q_BLOCK], preferred_element_type=F32)
        return acc_w[k * HD:(k + 1) * HD] / acc_w[N_KV * HD:N_KV * HD + 1]

    o_w = [window(k) for k in kvs]

    masks = [_select_blocks(comp[k][1], qpos, min(N_SEL, ns)).astype(BF16) for k in kvs]

    for k in kvs:
        mask_scr[k] = jnp.concatenate([masks[k]] * G_NSA, axis=1)
        m_scr[k] = jnp.full(m_scr.shape[1:], M_INIT, F32)
        acc_scr[k] = jnp.zeros(acc_scr.shape[1:], F32)
        far_hi = far_row[k].astype(BF16).astype(F32)
        rhs_scr[k, 0:HD, :] = qcols[k].astype(BF16)
        rhs_scr[k, SEL_FAR0:SEL_FAR0 + MASK_ROWS, :] = jnp.concatenate(
            [far_hi, far_row[k] - far_hi, jnp.zeros((MASK_ROWS - 2, G_NSA * Q_BLOCK), F32)], axis=0).astype(BF16)

    def scores(k, slab, kts, extra):
        rhs_scr[k, SEL_MASK0:SEL_MASK0 + MASK_ROWS, :] = mask_scr[k, pl.ds(pl.multiple_of(slab * MASK_ROWS, MASK_ROWS), MASK_ROWS), :]
        rhs = rhs_scr[k]
        out = []
        for kt, add in zip(kts, extra):
            s = jnp.dot(ks_ref[k, pl.ds(pl.multiple_of(kt * KEY_TILE, KEY_TILE), KEY_TILE), :], rhs, preferred_element_type=F32)
            out.append(s if add is None else s + add)
        return out

    def update(k, kts, tiles):
        m_old = m_scr[k]
        m_new = m_old
        for s in tiles:
            m_new = jnp.maximum(m_new, jnp.max(s, axis=0, keepdims=True))
        acc = jnp.exp(m_old - m_new) * acc_scr[k]
        for kt, s in zip(kts, tiles):
            acc = acc + jnp.dot(vst_ref[k, kt], jnp.exp(s - m_new).astype(BF16), preferred_element_type=F32)
        acc_scr[k] = acc
        m_scr[k] = m_new

    def attend(slab, kts, extra):
        tiles = [scores(k, slab, kts, extra(k)) for k in kvs]
        for k in kvs:
            update(k, kts, tiles[k])

    near_at = lambda k, start: lanes4(k, lambda h: band_ref[h, pl.ds(pl.multiple_of(start, Q_BLOCK), KEY_TILE), :]) - far_row[k]
    prev_near = (r == 0) & (kd >= 1)
    kd_odd = (kd & 1) == 1
    even_prev = jnp.logical_not(kd_odd) & prev_near
    n_pairs = kd // 2 - even_prev.astype(jnp.int32)
    n_quads = n_pairs // 2
    no_bias = lambda k: [None, None]

    def pair_scores(k, dst, pair):
        lo, hi = scores(k, pair, [2 * pair, 2 * pair + 1], [None, None])
        dst[k, 0:KEY_TILE, :] = lo
        dst[k, KEY_TILE:, :] = hi

    def pair_update(k, src, pair):
        update(k, [2 * pair, 2 * pair + 1], [src[k, 0:KEY_TILE, :], src[k, KEY_TILE:, :]])

    @pl.when(n_quads > 0)
    def _():
        for k in kvs:
            pair_scores(k, sa_scr, 0)

    for k in kvs:
        def quad_body(qd, carry, k=k):
            first = 2 * qd
            pair_scores(k, sb_scr, first + 1)
            pair_update(k, sa_scr, first)
            pair_scores(k, sa_scr, jnp.minimum(first + 2, 2 * n_quads - 2))
            pair_update(k, sb_scr, first + 1)
            return carry

        lax.fori_loop(0, n_quads, quad_body, 0)

    @pl.when((n_pairs & 1) == 1)
    def _():
        attend(n_pairs - 1, [2 * n_pairs - 2, 2 * n_pairs - 1], no_bias)

    @pl.when(kd_odd)
    def _():
        attend(kd // 2, [kd - 1, kd], lambda k: [jnp.where(prev_near, near_at(k, 0), 0.0), near_at(k, KEY_TILE - Q_BLOCK * r)])

    @pl.when(even_prev)
    def _():
        attend(kd // 2 - 1, [kd - 2, kd - 1], lambda k: [None, near_at(k, 0)])

    @pl.when(jnp.logical_not(kd_odd))
    def _():
        attend(kd // 2, [kd], lambda k: [near_at(k, KEY_TILE - Q_BLOCK * r)])

    heads_out = []
    for k in kvs:
        acc = acc_scr[k]
        o_s = acc[0:HD] / acc[HD:HD + 1]
        for g in range(G_NSA):
            h = G_NSA * k + g
            cols = slice(g * Q_BLOCK, (g + 1) * Q_BLOCK)
            heads_out.append(comp[k][0][:, cols] * g_t[h:h + 1] + o_s[:, cols] * g_t[H_NSA + h:H_NSA + h + 1]
                             + o_w[k][:, cols] * g_t[2 * H_NSA + h:2 * H_NSA + h + 1])
    o_ref[...] = jnp.concatenate(heads_out, axis=0).T


def _sel_pattern(rows, width):
    key = np.arange(rows)[:, None]
    b = np.arange(width)[None, :]
    ones = (b >= MASK_ROWS) & (b < MASK_ROWS + 2)
    return jnp.asarray(((key // SEL_BLOCK) % MASK_ROWS == b) | ones, BF16)


def _values_t(v, tile):
    T, n = v.shape
    rows = n + MASK_ROWS
    vt = jnp.concatenate([v.T, jnp.ones((1, T), F32), jnp.zeros((rows - n - 1, T), F32)], axis=0)
    return vt.reshape(rows, T // tile, tile).transpose(1, 0, 2).astype(BF16)


def _nsa_prompt(q, gates, kv, win, kc, vc, band, rel_table):
    T = q.shape[0]
    ncp, ns = kc.shape[0], T // SEL_BLOCK
    width = G_NSA * Q_BLOCK
    k_sel = lambda h: kv[:, 256 + h * HD:256 + (h + 1) * HD]
    v_sel = lambda h: kv[:, 384 + h * HD:384 + (h + 1) * HD]
    pattern = _sel_pattern(T, LANE - HD)
    ks_aug = jnp.stack([jnp.concatenate([k_sel(h).astype(BF16), pattern], axis=1) for h in range(N_KV)])
    operands = [q, gates, kc.astype(BF16), vc.T.astype(BF16), ks_aug, jnp.stack([_values_t(v_sel(h), KEY_TILE) for h in range(N_KV)]),
                win[:, 0:128].astype(BF16), _values_t(win[:, 128:256], Q_BLOCK), band, _pool_matrix(ns, ncp)]
    blk = lambda w: pl.BlockSpec((Q_BLOCK, w), lambda i: (i, 0))
    return pl.pallas_call(
        _nsa_prompt_kernel,
        grid=(T // Q_BLOCK,),
        in_specs=[pl.BlockSpec(memory_space=pltpu.SMEM), blk(H_NSA * HD), blk(LANE)] + [_resident(a.shape) for a in operands[2:]],
        out_specs=blk(H_NSA * HD),
        out_shape=jax.ShapeDtypeStruct((T, H_NSA * HD), F32),
        scratch_shapes=[pltpu.VMEM((N_KV, LANE, width), BF16), pltpu.VMEM((N_KV, ns, width), BF16),
                        pltpu.VMEM((N_KV, V_ROWS_KV, width), F32), pltpu.VMEM((N_KV, 1, width), F32),
                        pltpu.VMEM((N_KV, ncp, width), F32)] + [pltpu.VMEM((N_KV, 2 * KEY_TILE, width), F32)] * 2,
        compiler_params=_params(("arbitrary",)),
        name="nsa_prompt",
    )(rel_table, *operands)


SMP_PAGES = 8
TOK_PAD = 8
SMP_COLS = H_NSA * TOK_PAD


def _nsa_sample_kernel(pt_ref, *refs, n_pages, n_valid, past):
    pages = refs[:n_pages]
    (q_ref, g_ref, kc_ref, vc_ref, kvn_ref, win_ref, winn_ref, tab_ref, pool_ref, gsum_ref, epat_ref, o_ref,
     top_scr, mask_scr, acc_scr, m_scr, l_scr, oc_scr, ow_scr, nearb_scr) = refs[n_pages:]
    j = pl.program_id(1)
    ncp, wbuf = kc_ref.shape[0], win_ref.shape[1]
    tile_keys = n_pages * PAGE_SIZE
    lane = lax.broadcasted_iota(jnp.int32, (1, LANE), 1)
    tok = lane & (TOK_PAD - 1)
    second_kv = lane >= G_NSA * TOK_PAD
    tab = [tab_ref[b:b + 1, :] for b in range(N_BUCKETS)]
    far_row = tab[N_BUCKETS - 1]
    own_rows = lambda x: jnp.where(second_kv, x[HD:2 * HD], x[0:HD])
    pad_rows = lambda x: jnp.concatenate([x, jnp.zeros((LANE - x.shape[0], x.shape[1]), x.dtype)], axis=0)
    trow = lax.broadcasted_iota(jnp.int32, (LANE, 1), 0)
    d_new = tok - trow
    new_bias = jnp.where((d_new >= 0) & (trow < n_valid), _bias_chain(jnp.maximum(d_new, 0), tab), NEG)

    def attend_update(s, values_t):
        m_old = m_scr[...]
        m_new = jnp.maximum(m_old, jnp.max(s, axis=0, keepdims=True))
        alpha = jnp.exp(m_old - m_new)
        p = jnp.exp(s - m_new)
        l_scr[...] = alpha * l_scr[...] + jnp.sum(p, axis=0, keepdims=True)
        acc_scr[...] = alpha * acc_scr[...] + jnp.dot(values_t.astype(BF16), p.astype(BF16), preferred_element_type=F32)
        m_scr[...] = m_new

    def reset():
        m_scr[...] = jnp.full(m_scr.shape, M_INIT, F32)
        l_scr[...] = jnp.zeros(l_scr.shape, F32)
        acc_scr[...] = jnp.zeros(acc_scr.shape, F32)

    @pl.when(j == 0)
    def _():
        q_t = pad_rows(q_ref[...] * HD ** -0.5).T
        halves = []
        for k in range(N_KV):
            part = jnp.zeros((HD, LANE), F32)
            for g in range(G_NSA):
                h = G_NSA * k + g
                piece = q_t[h * HD:(h + 1) * HD, :]
                part = part + (pltpu.roll(piece, TOK_PAD * h, axis=1) if h else piece)
            halves.append(part)
        top = jnp.concatenate(halves, axis=0).astype(BF16)
        top_scr[...] = top
        qpos = past + tok

        n0 = ncp - 32
        kcb = kc_ref[...].astype(BF16)
        d_edge = qpos - (CMP_STRIDE * (n0 + lax.broadcasted_iota(jnp.int32, (32, 1), 0)) + CMP_BLOCK - 1)
        s_c = jnp.concatenate([
            jnp.dot(kcb[:n0], top, preferred_element_type=F32) + far_row,
            jnp.dot(kcb[n0:], top, preferred_element_type=F32) + jnp.where(d_edge >= 0, _bias_chain(jnp.maximum(d_edge, 0), tab), NEG)], axis=0)
        p_c = _softmax_cols(s_c)
        oc_scr[...] = own_rows(jnp.dot(vc_ref[...].T.astype(BF16), p_c.astype(BF16), preferred_element_type=F32))
        imp = _dot_exact_rhs(p_c, gsum_ref[...], terms=3)
        mask_scr[...] = _select_blocks(_dot_exact_lhs(pool_ref[...], imp), qpos, N_SEL)

        wk = win_ref[0:LANE, :].T
        d_w = wbuf + tok - lax.broadcasted_iota(jnp.int32, (wbuf, 1), 0)
        near = wbuf - LANE
        s_w = jnp.dot(wk.astype(BF16), top, preferred_element_type=F32)
        s_w = (jnp.concatenate([s_w[:near] + far_row, s_w[near:] + _bias_chain(d_w[near:], tab)], axis=0)
               + jnp.where(d_w < WINDOW, 0.0, NEG))
        reset()
        attend_update(s_w, win_ref[LANE:, :])
        wn = pad_rows(winn_ref[...])
        attend_update(jnp.dot(wn[:, :LANE].astype(BF16), top, preferred_element_type=F32) + new_bias, wn[:, LANE:].T)
        ow_scr[...] = own_rows(acc_scr[...]) / l_scr[...]
        reset()
        nearb_scr[...] = _bias_chain(LANE + tok - lax.broadcasted_iota(jnp.int32, (LANE, 1), 0), tab)

    k_tile = jnp.concatenate([pg[0:LANE, :].T for pg in pages], axis=0)
    vt_tile = jnp.concatenate([pg[LANE:, :] for pg in pages], axis=1)
    slab = mask_scr[pl.ds(pl.multiple_of(j * MASK_ROWS, MASK_ROWS), MASK_ROWS), :].astype(BF16)
    rhs = jnp.concatenate([top_scr[...], slab, jnp.zeros((LANE - MASK_ROWS, LANE), BF16)], axis=0)
    s = jnp.dot(jnp.concatenate([k_tile.astype(BF16), epat_ref[...]], axis=1), rhs, preferred_element_type=F32)
    near = tile_keys - LANE
    s = jnp.concatenate([s[:near] + far_row, s[near:] + jnp.where(j == pl.num_programs(1) - 1, nearb_scr[...], far_row)], axis=0)
    attend_update(s, vt_tile)

    @pl.when(j == pl.num_programs(1) - 1)
    def _():
        kn = pad_rows(kvn_ref[...])
        last_blk = past // SEL_BLOCK
        s_n = (jnp.dot(kn[:, 2 * LANE:3 * LANE].astype(BF16), top_scr[...], preferred_element_type=F32)
               + new_bias + mask_scr[last_blk:last_blk + 1, :])
        attend_update(s_n, kn[:, 3 * LANE:].T)
        o_s = own_rows(acc_scr[...]) / l_scr[...]
        g_t = pad_rows(g_ref[...]).T
        gate_rows = []
        for b in range(3):
            row = g_t[b * H_NSA:b * H_NSA + 1]
            for h in range(1, H_NSA):
                row = row + pltpu.roll(g_t[b * H_NSA + h:b * H_NSA + h + 1], TOK_PAD * h, axis=1)
            gate_rows.append(row)
        o_col = oc_scr[...] * gate_rows[0] + o_s * gate_rows[1] + ow_scr[...] * gate_rows[2]
        per_head = [o_col if h == 0 else pltpu.roll(o_col, LANE - TOK_PAD * h, axis=1) for h in range(H_NSA)]
        o_ref[...] = jnp.concatenate(per_head, axis=0).T[:TOK_PAD]


def _nsa_sample(pool, page_table, q, gates, kc, vc, kv_new, win_buf, win_new, rel_table, n_valid):
    B, n_pages_total = page_table.shape
    past = n_pages_total * PAGE_SIZE
    ncp = kc.shape[1]
    ns = past // SEL_BLOCK + 1
    nsp = -(-ns // MASK_ROWS) * MASK_ROWS
    col = np.arange(LANE)
    used = col < SMP_COLS
    gsum = jnp.asarray(((col[:, None] // (G_NSA * TOK_PAD) == col[None, :] // (G_NSA * TOK_PAD))
                        & (col[:, None] % TOK_PAD == col[None, :] % TOK_PAD) & used[:, None] & used[None, :]), BF16)
    tab_cols = jnp.pad(jnp.repeat(rel_table, TOK_PAD, axis=1), ((0, 0), (0, LANE - SMP_COLS)))
    consts = [tab_cols, _pool_matrix(nsp, ncp), gsum, _sel_pattern(SMP_PAGES * PAGE_SIZE, LANE)]
    per_seq = [q, gates, kc, vc, kv_new, win_buf, win_new]
    seq_spec = lambda a: pl.BlockSpec((None,) + a.shape[1:], lambda b, j, pt: (b,) + (0,) * (a.ndim - 1))
    const = lambda a: pl.BlockSpec(a.shape, lambda b, j, pt: (0,) * a.ndim)

    def page_map(p):
        return lambda b, j, pt: (pt[b, j * SMP_PAGES + p], 1, 0)

    grid_spec = pltpu.PrefetchScalarGridSpec(
        num_scalar_prefetch=1,
        grid=(B, n_pages_total // SMP_PAGES),
        in_specs=[pl.BlockSpec((None, 2 * LANE, PAGE_SIZE), page_map(p)) for p in range(SMP_PAGES)]
                 + [seq_spec(a) for a in per_seq] + [const(a) for a in consts],
        out_specs=pl.BlockSpec((None, TOK_PAD, H_NSA * HD), lambda b, j, pt: (b, 0, 0)),
        scratch_shapes=[pltpu.VMEM((LANE, LANE), BF16), pltpu.VMEM((nsp, LANE), F32), pltpu.VMEM((LANE, LANE), F32),
                        pltpu.VMEM((1, LANE), F32), pltpu.VMEM((1, LANE), F32), pltpu.VMEM((HD, LANE), F32), pltpu.VMEM((HD, LANE), F32),
                        pltpu.VMEM((LANE, LANE), F32)],
    )
    return pl.pallas_call(
        functools.partial(_nsa_sample_kernel, n_pages=SMP_PAGES, n_valid=n_valid, past=past),
        grid_spec=grid_spec,
        out_shape=jax.ShapeDtypeStruct((B, TOK_PAD, H_NSA * HD), F32),
        compiler_params=_params(("arbitrary", "arbitrary")),
        name="nsa_sample",
    )(page_table, *([pool] * SMP_PAGES), *per_seq, *consts)


def _outproj_kernel(x_ref, nsa_ref, rw_ref, gt_ref, lng_ref, lnb_ref, w_ref, o_ref):
    half = H_NSA * HD
    out = (jnp.dot(nsa_ref[...].astype(BF16), w_ref[0:half, :], preferred_element_type=F32)
           + jnp.dot(rw_ref[...].astype(BF16), w_ref[half:, :], preferred_element_type=F32))
    y = ALPHA * x_ref[...] + (1.0 + gt_ref[...]) * out
    o_ref[...] = _layer_norm(y, lng_ref[...], lnb_ref[...])


def _outproj(x, o_nsa, o_rwkv, gate, ln_g, ln_b, w_out):
    rows = x.shape[0]
    tm = min(512, rows)
    row = lambda i: (i, 0)
    return pl.pallas_call(
        _outproj_kernel,
        grid=(rows // tm,),
        in_specs=[pl.BlockSpec((tm, D_MODEL), row), pl.BlockSpec((tm, H_NSA * HD), row), pl.BlockSpec((tm, D_RWKV), row),
                  _mod_spec(gate, tm), _resident((1, D_MODEL)), _resident((1, D_MODEL)), _resident(w_out.shape)],
        out_specs=pl.BlockSpec((tm, D_MODEL), row),
        out_shape=jax.ShapeDtypeStruct((rows, D_MODEL), F32),
        compiler_params=_params(("arbitrary",)),
        name="outproj",
    )(x, o_nsa, o_rwkv, gate, ln_g.reshape(1, -1), ln_b.reshape(1, -1), w_out)


def kernel(x_prompt, x_sample, cache_nsa_kv, cache_nsa_win, state_rwkv_shift, state_rwkv_wkv, page_table, c_prompt, c_sample, rel_table, w_ada, b_ada, ln_g, ln_b, ffn1_gate, ffn1_up, ffn1_down, ffn2_gate, ffn2_up, ffn2_down, w_in, w_out, cmp_pe_k, cmp_w1_k, cmp_b1_k, cmp_w2_k, cmp_pe_v, cmp_w1_v, cmp_b1_v, cmp_w2_v, rwkv_mu, rwkv_w0, rwkv_w2, rwkv_a0, rwkv_a2, rwkv_g2, rwkv_k_k, rwkv_k_a, rwkv_r_k, rwkv_gn_w, rwkv_gn_b):
    assert w_ada.shape[0] == DEPTH == 1 and x_prompt.shape[0] == 1
    l = 0
    lw = dict(cmp_pe_k=cmp_pe_k[l], cmp_w1_k=cmp_w1_k[l], cmp_b1_k=cmp_b1_k[l], cmp_w2_k=cmp_w2_k[l],
              cmp_pe_v=cmp_pe_v[l], cmp_w1_v=cmp_w1_v[l], cmp_b1_v=cmp_b1_v[l], cmp_w2_v=cmp_w2_v[l],
              rwkv_mu=rwkv_mu[l], rwkv_w0=rwkv_w0[l], rwkv_w2=rwkv_w2[l], rwkv_a0=rwkv_a0[l], rwkv_a2=rwkv_a2[l], rwkv_g2=rwkv_g2[l],
              rwkv_k_k=rwkv_k_k[l], rwkv_k_a=rwkv_k_a[l], rwkv_r_k=rwkv_r_k[l], rwkv_gn_w=rwkv_gn_w[l], rwkv_gn_b=rwkv_gn_b[l])
    T = x_prompt.shape[1]
    nb, nt = x_sample.shape[0], x_sample.shape[1]
    assert nt <= TOK_PAD
    n_seq = 1 + nb
    c_all = jnp.concatenate([c_prompt, c_sample, jnp.zeros((-n_seq % 8, D_MODEL), F32)], axis=0)
    mod = _ada(c_all, w_ada[l], b_ada[l])
    mod_p = mod[0:1].reshape(9, 1, D_MODEL)
    mod_s = jnp.repeat(mod[1:n_seq].reshape(nb, 9, D_MODEL), nt, axis=0).transpose(1, 0, 2)
    ffn1 = [w[l].astype(BF16) for w in (ffn1_gate, ffn1_up, ffn1_down)]
    ffn2 = [w[l].astype(BF16) for w in (ffn2_gate, ffn2_up, ffn2_down)]
    w_in_p = _prep_w_in(w_in[l])
    w_out_b = w_out[l].astype(BF16)

    def trunk_in(x, m):
        x1 = _ffn(x, m[0], m[1], m[2], ln_g[l, 0], ln_b[l, 0], *ffn1)
        return x1, _proj(x1, m[3], m[4], w_in_p)

    def trunk_out(x1, o_nsa, o_rwkv, m):
        x2 = _outproj(x1, o_nsa, o_rwkv, m[5], ln_g[l, 1], ln_b[l, 1], w_out_b)
        return _ffn(x2, m[6], m[7], m[8], ln_g[l, 2], ln_b[l, 2], *ffn2)

    xp1, (q, kv, win, gates, pr) = trunk_in(x_prompt[0], mod_p)
    o_rw, wkv_p = _rwkv(pr[None], jnp.zeros((1, 1, RW_PAD), F32), jnp.zeros((1, H_RWKV, HD_RWKV, HD_RWKV), F32), lw, min(RW_STEP, T))
    n_rows = T // PAGE_SIZE
    kc, vc = _compress(kv.reshape(n_rows, PAGE_SIZE, 4 * LANE), jnp.arange(n_rows, dtype=jnp.int32)[None], lw, transposed=False)
    o_nsa = _nsa_prompt(q, gates, kv, win, kc[0], vc[0], _band(rel_table), rel_table)
    y_prompt = trunk_out(xp1, o_nsa, o_rw[0], mod_p)
    kv_prompt = kv.reshape(1, 1, T, 4, N_KV, HD)
    win_prompt = win[T - min(WINDOW, T):].reshape(1, 1, -1, 2, N_KV, HD)
    shift_prompt = _rwkv_uncols(pr[T - 1]).reshape(1, 1, RWKV_COLS)

    xs1, (q_s, kv_s, win_s, gates_s, pr_s) = trunk_in(x_sample.reshape(nb * nt, D_MODEL), mod_s)
    tokens = lambda a: jnp.pad(a.reshape(nb, nt, -1), ((0, 0), (0, TOK_PAD - nt), (0, 0)))
    pr_pad = jnp.pad(pr_s.reshape(nb, nt, -1), ((0, 0), (0, RW_TOK_PAD - nt), (0, 0)))
    o_rw_s, wkv_s = _rwkv(pr_pad, _rwkv_cols(state_rwkv_shift[l])[:, None], state_rwkv_wkv[l], lw, nt)
    pool_t = jnp.transpose(cache_nsa_kv[l], (0, 2, 3, 4, 1)).reshape(-1, 4 * LANE, PAGE_SIZE)
    kc_s, vc_s = _compress(pool_t, page_table, lw, transposed=True)
    win_buf = cache_nsa_win[l]
    win_t = jnp.transpose(win_buf, (0, 2, 3, 4, 1)).reshape(nb, 2 * LANE, -1)
    o_nsa_s = _nsa_sample(pool_t, page_table, tokens(q_s), tokens(gates_s), kc_s, vc_s, tokens(kv_s),
                          win_t, tokens(win_s), rel_table, nt)
    y_sample = trunk_out(xs1, o_nsa_s[:, :nt].reshape(nb * nt, -1), o_rw_s[:, :nt].reshape(nb * nt, -1), mod_s)
    kv_sample = kv_s.reshape(1, nb, nt, 4, N_KV, HD)
    win_sample = jnp.concatenate([win_buf, win_s.reshape(nb, nt, 2, N_KV, HD)], axis=1)[None, :, nt:]
    shift_sample = _rwkv_uncols(pr_s.reshape(nb, nt, -1)[:, -1])[None]
    return (y_prompt[None], y_sample.reshape(nb, nt, D_MODEL), kv_prompt, win_prompt, shift_prompt, wkv_p[None],
            kv_sample, win_sample, shift_sample, wkv_s[None])
```

```python
import functools
import math

import numpy as np
import jax
import jax.numpy as jnp
from jax import lax
from jax.experimental import pallas as pl
from jax.experimental.pallas import tpu as pltpu

D_MODEL = 1024
PAGE_SIZE = 128
H_NSA = 8
N_KV = 2
G_NSA = H_NSA // N_KV
HD = 64
CMP_STRIDE = 16
CMP_BLOCK = 2 * CMP_STRIDE
CMP_HIDDEN = 256
SEL_BLOCK = 64
N_SEL = 16
WINDOW = 512
Q_BLOCK = 128
N_BUCKETS = 32
MAX_DISTANCE = 128
H_RWKV = 8
HD_RWKV = 64
D_RWKV = H_RWKV * HD_RWKV
DECAY_LORA = 32
AAA_LORA = 32
GATE_LORA = 96
GN_EPS = 64e-5
D_FF = 2816
LN_EPS = 1e-5
DEPTH = 1
ALPHA = (2 * DEPTH) ** 0.25

NSA_SIZES = (H_NSA * HD,) + (N_KV * HD,) * 6 + (H_NSA * 3,)
RWKV_SIZES = (D_RWKV, D_RWKV, D_RWKV, DECAY_LORA, AAA_LORA, GATE_LORA)
NSA_COLS = sum(NSA_SIZES)
RWKV_COLS = sum(RWKV_SIZES)

F32 = jnp.float32
BF16 = jnp.bfloat16
LANE = 128
NEG = -(2.0 ** 100)
M_INIT = -(2.0 ** 103)
VMEM_LIMIT = 56 * 1024 * 1024

RW_PAD = 3 * D_RWKV + 3 * LANE
P_Q, P_KV, P_WIN, P_GATE, P_RW = 0, 512, 1024, 1280, 1408
P_COLS = P_RW + RW_PAD
KEY_TILE = 512
MASK_ROWS = 16
V_ROWS = 144
MASK_ROW0 = N_KV * HD
SEL_MASK0 = HD
SEL_FAR0 = SEL_MASK0 + MASK_ROWS
V_ROWS_KV = HD + MASK_ROWS


def _bucket_lows():
    d = np.arange(0, 4 * MAX_DISTANCE, dtype=np.int64)
    max_exact = N_BUCKETS // 2
    df = np.maximum(d, 1).astype(np.float32)
    large = max_exact + (np.log(df / np.float32(max_exact)) / np.float32(math.log(MAX_DISTANCE / max_exact))
                         * np.float32(N_BUCKETS - max_exact)).astype(np.int32)
    b = np.where(d < max_exact, d, np.minimum(large, N_BUCKETS - 1))
    lows = [int(np.argmax(b >= k)) for k in range(N_BUCKETS)]
    return b, lows


_BUCKET_OF, _BUCKET_LOW = _bucket_lows()
FAR_DIST = _BUCKET_LOW[N_BUCKETS - 1]


def _resident(shape):
    nd = len(shape)
    return pl.BlockSpec(shape, lambda *_: (0,) * nd, pipeline_mode=pl.Buffered(1))


def _params(sem):
    return pltpu.CompilerParams(dimension_semantics=sem, vmem_limit_bytes=VMEM_LIMIT)


def _split2(x):
    hi = x.astype(BF16)
    lo = (x - hi.astype(F32)).astype(BF16)
    return hi, lo


def _dot_exact_rhs(x, rhs_bf16, terms=2):
    acc = None
    rem = x
    for _ in range(terms):
        part = rem.astype(BF16)
        d = jnp.dot(part, rhs_bf16, preferred_element_type=F32)
        acc = d if acc is None else acc + d
        rem = rem - part.astype(F32)
    return acc


def _dot_exact_lhs(lhs_bf16, x, terms=3):
    acc = None
    rem = x
    for _ in range(terms):
        part = rem.astype(BF16)
        d = jnp.dot(lhs_bf16, part, preferred_element_type=F32)
        acc = d if acc is None else acc + d
        rem = rem - part.astype(F32)
    return acc


def _layer_norm(y, g, b):
    mu = jnp.mean(y, axis=-1, keepdims=True)
    yc = y - mu
    var = jnp.mean(yc * yc, axis=-1, keepdims=True)
    return yc * lax.rsqrt(var + LN_EPS) * g + b


def _bias_chain(d, tab_rows):
    out = tab_rows[0] + jnp.zeros(d.shape, F32)
    for b in range(1, N_BUCKETS):
        out = jnp.where(d >= _BUCKET_LOW[b], tab_rows[b], out)
    return out


def _ada_kernel(c_ref, w_ref, b_ref, o_ref):
    c = c_ref[...]
    h = (c * jax.nn.sigmoid(c)).astype(BF16)
    o_ref[...] = jnp.dot(h, w_ref[...].astype(BF16), preferred_element_type=F32) + b_ref[...]


def _ada(c_all, w_ada, b_ada):
    rows, n = c_all.shape[0], w_ada.shape[1]
    tn = 1152
    return pl.pallas_call(
        _ada_kernel,
        grid=(n // tn,),
        in_specs=[pl.BlockSpec((rows, D_MODEL), lambda j: (0, 0)),
                  pl.BlockSpec((D_MODEL, tn), lambda j: (0, j)),
                  pl.BlockSpec((1, tn), lambda j: (0, j))],
        out_specs=pl.BlockSpec((rows, tn), lambda j: (0, j)),
        out_shape=jax.ShapeDtypeStruct((rows, n), F32),
        compiler_params=_params(("arbitrary",)),
        name="ada",
    )(c_all, w_ada, b_ada.reshape(1, n))


FF_CHUNKS = 2


def _ffn_kernel(x_ref, sh_ref, sc_ref, gt_ref, lng_ref, lnb_ref, wg_ref, wu_ref, wd_ref, o_ref):
    x = x_ref[...]
    h = (x * (1.0 + sc_ref[...]) + sh_ref[...]).astype(BF16)
    ck = D_FF // FF_CHUNKS
    acc = jnp.zeros(x.shape, F32)
    for c in range(FF_CHUNKS):
        a = jnp.dot(h, wg_ref[:, c * ck:(c + 1) * ck], preferred_element_type=F32)
        b = jnp.dot(h, wu_ref[:, c * ck:(c + 1) * ck], preferred_element_type=F32)
        t = (a * jax.nn.sigmoid(a) * b).astype(BF16)
        acc = acc + jnp.dot(t, wd_ref[c * ck:(c + 1) * ck, :], preferred_element_type=F32)
    y = ALPHA * x + (1.0 + gt_ref[...]) * (0.5 * acc)
    o_ref[...] = _layer_norm(y, lng_ref[...], lnb_ref[...])


def _mod_spec(mod, tm):
    if mod.shape[0] == 1:
        return pl.BlockSpec((1, D_MODEL), lambda i: (0, 0))
    return pl.BlockSpec((tm, D_MODEL), lambda i: (i, 0))


def _ffn(x, shift, scale, gate, ln_g, ln_b, wg, wu, wd):
    rows = x.shape[0]
    tm = min(512, rows)
    row = lambda i: (i, 0)
    return pl.pallas_call(
        _ffn_kernel,
        grid=(rows // tm,),
        in_specs=[pl.BlockSpec((tm, D_MODEL), row), _mod_spec(shift, tm), _mod_spec(scale, tm), _mod_spec(gate, tm),
                  _resident((1, D_MODEL)), _resident((1, D_MODEL)),
                  _resident((D_MODEL, D_FF)), _resident((D_MODEL, D_FF)), _resident((D_FF, D_MODEL))],
        out_specs=pl.BlockSpec((tm, D_MODEL), row),
        out_shape=jax.ShapeDtypeStruct((rows, D_MODEL), F32),
        compiler_params=_params(("arbitrary",)),
        name="ffn",
    )(x, shift, scale, gate, ln_g.reshape(1, -1), ln_b.reshape(1, -1), wg, wu, wd)


def _proj_kernel(x_ref, sh_ref, sc_ref, w_ref, q_ref, kv_ref, win_ref, g_ref, pr_ref):
    h = (x_ref[...] * (1.0 + sc_ref[...]) + sh_ref[...]).astype(BF16)
    p = jnp.dot(h, w_ref[...], preferred_element_type=F32)
    q_ref[...] = p[:, P_Q:P_KV]
    kv_ref[...] = p[:, P_KV:P_WIN]
    win_ref[...] = p[:, P_WIN:P_GATE]
    g_ref[...] = jax.nn.sigmoid(p[:, P_GATE:P_RW])
    pr_ref[...] = p[:, P_RW:P_COLS]


def _proj(x, shift, scale, w_in_p):
    rows = x.shape[0]
    tm = min(512, rows)
    row = lambda i: (i, 0)
    widths = (512, 512, 256, LANE, RW_PAD)
    return pl.pallas_call(
        _proj_kernel,
        grid=(rows // tm,),
        in_specs=[pl.BlockSpec((tm, D_MODEL), row), _mod_spec(shift, tm), _mod_spec(scale, tm),
                  _resident((D_MODEL, P_COLS))],
        out_specs=[pl.BlockSpec((tm, w), row) for w in widths],
        out_shape=[jax.ShapeDtypeStruct((rows, w), F32) for w in widths],
        compiler_params=_params(("arbitrary",)),
        name="proj",
    )(x, shift, scale, w_in_p)


def _prep_w_in(w_in):
    pad = lambda a, n: jnp.pad(a, ((0, 0), (0, n - a.shape[1])))
    nsa, rw = w_in[:, :NSA_COLS], w_in[:, NSA_COLS:]
    gl = nsa[:, 1280:1304].reshape(D_MODEL, H_NSA, 3).transpose(0, 2, 1).reshape(D_MODEL, 3 * H_NSA)
    cols = [nsa[:, :1280], pad(gl, LANE), _rwkv_cols(rw)]
    return jnp.concatenate(cols, axis=1).astype(BF16)


def _rwkv_cols(a):
    pad = lambda t: jnp.pad(t, [(0, 0)] * (t.ndim - 1) + [(0, LANE - t.shape[-1])])
    n = 3 * D_RWKV
    return jnp.concatenate([a[..., :n], pad(a[..., n:n + 32]), pad(a[..., n + 32:n + 64]), pad(a[..., n + 64:n + 160])], axis=-1)


def _rwkv_uncols(a):
    n = 3 * D_RWKV
    return jnp.concatenate([a[..., :n], a[..., n:n + 32], a[..., n + LANE:n + LANE + 32], a[..., n + 2 * LANE:n + 2 * LANE + 96]], axis=-1)


RW_GROUP = 64
RW_TOK_PAD = 16
RW_PAIRS = H_RWKV // 2
RW_BLOCK = 64
RW_STEP = 128


def _lora(x, w_ref):
    w = w_ref[...]
    w_hi = w.astype(BF16)
    w_lo = (w - w_hi.astype(F32)).astype(BF16)
    return _dot_exact_rhs(x, w_hi) + jnp.dot(x.astype(BF16), w_lo, preferred_element_type=F32)


def _rwkv_kernel(pr_ref, sh0_ref, s0_ref, mu_ref, w0_ref, a0_ref, kk_ref, ka_ref, rk_ref, gw_ref, gb_ref,
                 w2_ref, a2_ref, g2_ref, bo_ref, lgrp_ref, ggrp_ref, o_ref, sout_ref,
                 prev_scr, s_scr, *, n_valid):
    tb = pr_ref.shape[0]
    step = pl.program_id(1)

    @pl.when(step == 0)
    def _():
        prev_scr[...] = sh0_ref[...]
        s_scr[...] = s0_ref[...]

    p = pr_ref[...]
    rows = lax.broadcasted_iota(jnp.int32, (tb, 1), 0)
    prev = jnp.where(rows == 0, prev_scr[...], pltpu.roll(p, 1, axis=0))
    prev_scr[...] = p[tb - 1:tb, :]
    xs = p + (prev - p) * mu_ref[...]
    n = D_RWKV
    r, k, v = xs[:, :n], xs[:, n:2 * n], xs[:, 2 * n:3 * n]
    wl, al, gl = xs[:, 3 * n:3 * n + LANE], xs[:, 3 * n + LANE:3 * n + 2 * LANE], xs[:, 3 * n + 2 * LANE:]
    z = -(w0_ref[...] + _lora(jnp.tanh(wl), w2_ref))
    w = -(jnp.maximum(z, 0.0) + jnp.log(1.0 + jnp.exp(-jnp.abs(z)))) - 0.5
    a = jax.nn.sigmoid(a0_ref[...] + _lora(al, a2_ref))
    g = _lora(jax.nn.sigmoid(gl), g2_ref)
    kk = k * kk_ref[...]
    ss = _dot_exact_rhs(kk * kk, bo_ref[...])
    kk = kk / jnp.maximum(jnp.sqrt(ss), 1e-12)
    k2 = k * (1.0 + (a - 1.0) * ka_ref[...])
    G = min(RW_GROUP, tb)
    log_dec = -jnp.exp(w)
    bet = kk * a
    if n_valid < tb:
        live = rows < n_valid
        log_dec, kk, bet, k2, v_in = (jnp.where(live, x, 0.0) for x in (log_dec, kk, bet, k2, v))
    else:
        v_in = v
    cum = _dot_exact_lhs(lgrp_ref[...], log_dec, terms=2)
    cum_end = _dot_exact_lhs(ggrp_ref[...], log_dec, terms=2)
    gam_inv = jnp.exp(-cum)
    gam_end = jnp.exp(cum_end - cum)
    k_hat = -kk * jnp.exp(cum - log_dec)
    r_hat = r * jnp.exp(cum)
    b_chk, k_chk = bet * gam_inv, k2 * gam_inv
    b_til, k_til = bet * gam_end, k2 * gam_end
    gam_group = jnp.exp(cum_end)

    lane = lax.broadcasted_iota(jnp.int32, (1, LANE), 1)
    low = lane < HD_RWKV
    lane_t = lane & (RW_BLOCK - 1)
    row = lax.broadcasted_iota(jnp.int32, (LANE, 1), 0)
    row_t = row & (RW_BLOCK - 1)
    same = ((row < HD_RWKV) == low) & ((row_t // G) == (lane_t // G))
    strict, incl = same & (lane_t < row_t), same & (lane_t <= row_t)
    bf = lambda x: x.astype(BF16)
    mm = lambda x, y: jnp.dot(bf(x), bf(y), preferred_element_type=F32)
    mm_nt = lambda x, y: lax.dot_general(bf(x), bf(y), (((1,), (1,)), ((), ())), preferred_element_type=F32)

    def rows_bd(x):
        if tb < RW_BLOCK:
            x = jnp.concatenate([x, jnp.zeros((RW_BLOCK - tb, LANE), F32)], axis=0)
        return jnp.concatenate([jnp.where(low, x, 0.0), jnp.where(low, 0.0, x)], axis=0)

    def mm3(x, y):
        xh, yh = bf(x), bf(y)
        xl, yl = bf(x - xh.astype(F32)), bf(y - yh.astype(F32))
        return jnp.dot(jnp.concatenate([xh, xl, xh], axis=1), jnp.concatenate([yh, yh, yl], axis=0), preferred_element_type=F32)

    units = max(1, tb // RW_BLOCK)
    unit_rows = min(tb, RW_BLOCK)
    pairs = range(RW_PAIRS)
    items = [(un, pp) for un in range(units) for pp in pairs]
    at = lambda x, it: x[it[0] * unit_rows:(it[0] + 1) * unit_rows, it[1] * LANE:(it[1] + 1) * LANE]
    kh_row = [rows_bd(at(k_hat, it)) for it in items]
    rh_row = [rows_bd(at(r_hat, it)) for it in items]
    kh_mat = [x.T for x in kh_row]
    rh_mat = [x.T for x in rh_row]
    state_in = [jnp.concatenate([rows_bd(at(b_chk, it)).T, rows_bd(at(k_chk, it)).T], axis=1) for it in items]
    upd_rows = [jnp.concatenate([rows_bd(at(b_til, it)), rows_bd(at(k_til, it))], axis=0) for it in items]
    v_t = [rows_bd(at(v_in, it)).T for it in items]
    v_t = [x[:HD_RWKV] + x[HD_RWKV:] for x in v_t]
    c_all = [mm(jnp.concatenate([kh_row[n], rh_row[n]], axis=0), state_in[n]) for n in range(len(items))]
    c_uu = [jnp.where(strict, c[:LANE, :LANE], 0.0) for c in c_all]
    c_uv = [jnp.where(strict, c[:LANE, LANE:], 0.0) for c in c_all]
    c_ru = [jnp.where(incl, c[LANE:, :LANE], 0.0) for c in c_all]
    c_rv = [jnp.where(incl, c[LANE:, LANE:], 0.0) for c in c_all]
    t_neu, power = list(c_uu), list(c_uu)
    span = 2
    while span < G:
        power = [mm(x, x) for x in power]
        t_neu = [t_neu[n] + power[n] + mm(t_neu[n], power[n]) for n in range(len(items))]
        span *= 2
    from_v = [mm_nt(v_t[n], c_uv[n]) for n in range(len(items))]
    st = [s_scr[pp] for pp in pairs]
    y_units = []
    for un in range(units):
        ns = [un * RW_PAIRS + pp for pp in pairs]
        y_t = [jnp.zeros((HD_RWKV, LANE), F32) for _ in pairs]
        for grp in range(unit_rows // G):
            here = (lane_t // G) == grp
            first = un * unit_rows + grp * G
            w_t = [jnp.where(here, mm(st[pp], kh_mat[ns[pp]]) + from_v[ns[pp]], 0.0) for pp in pairs]
            u_t = [w_t[pp] + mm_nt(w_t[pp], t_neu[ns[pp]]) for pp in pairs]
            v_g = [jnp.where(here, v_t[ns[pp]], 0.0) for pp in pairs]
            y_t = [y_t[pp] + jnp.where(here, mm(st[pp], rh_mat[ns[pp]]), 0.0) + mm_nt(u_t[pp], c_ru[ns[pp]])
                   + mm_nt(v_g[pp], c_rv[ns[pp]]) for pp in pairs]
            st = [st[pp] * gam_group[first:first + 1, pp * LANE:(pp + 1) * LANE]
                  + mm3(jnp.concatenate([u_t[pp], v_g[pp]], axis=1), upd_rows[ns[pp]]) for pp in pairs]
        lane_u = lax.broadcasted_iota(jnp.int32, (unit_rows, LANE), 1)
        pieces = []
        for pp in pairs:
            yt = jnp.concatenate([y_t[pp], jnp.zeros((LANE - HD_RWKV, LANE), F32)], axis=0).T
            pieces.append(jnp.where(lane_u < HD_RWKV, yt[:unit_rows], pltpu.roll(yt[RW_BLOCK:RW_BLOCK + unit_rows], HD_RWKV, axis=1)))
        y_units.append(jnp.concatenate(pieces, axis=1))
    for pp in pairs:
        s_scr[pp] = st[pp]
    y = y_units[0] if units == 1 else jnp.concatenate(y_units, axis=0)
    mean = _dot_exact_rhs(y, bo_ref[...]) * (1.0 / HD_RWKV)
    yc = y - mean
    var = _dot_exact_rhs(yc * yc, bo_ref[...]) * (1.0 / HD_RWKV)
    yn = yc * lax.rsqrt(var + GN_EPS) * gw_ref[...] + gb_ref[...]
    bonus = _dot_exact_rhs(r * k2 * rk_ref[...], bo_ref[...]) * v
    o_ref[...] = (yn + bonus) * g
    sout_ref[...] = s_scr[...]


def _pair_state(s):
    B = s.shape[0]
    return s.reshape(B, RW_PAIRS, 2, HD_RWKV, HD_RWKV).transpose(0, 1, 3, 2, 4).reshape(B, RW_PAIRS, HD_RWKV, LANE)


def _unpair_state(s):
    B = s.shape[0]
    return s.reshape(B, RW_PAIRS, HD_RWKV, 2, HD_RWKV).transpose(0, 1, 3, 2, 4).reshape(B, H_RWKV, HD_RWKV, HD_RWKV)


def _rwkv(pr, shift0, s0, lw, n_valid):
    B, T, _ = pr.shape
    tb = min(RW_STEP, T)
    n = D_RWKV
    vec = lambda a: a.reshape(1, n)
    padrow = lambda a: jnp.pad(a, ((0, LANE - a.shape[0]), (0, 0)))
    blk = np.arange(n) // HD_RWKV
    block_ones = jnp.asarray(blk[:, None] == blk[None, :], BF16)
    tok = np.arange(tb)
    group = min(RW_GROUP, tb)
    same_group = tok[:, None] // group == tok[None, :] // group
    prefix = jnp.asarray(same_group & (tok[None, :] <= tok[:, None]), BF16)
    consts = [_rwkv_cols(lw['rwkv_mu']).reshape(1, RW_PAD), vec(lw['rwkv_w0']), vec(lw['rwkv_a0']), vec(lw['rwkv_k_k']),
              vec(lw['rwkv_k_a']), vec(lw['rwkv_r_k']), vec(lw['rwkv_gn_w']), vec(lw['rwkv_gn_b']),
              padrow(lw['rwkv_w2']), padrow(lw['rwkv_a2']), padrow(lw['rwkv_g2']), block_ones, prefix, jnp.asarray(same_group, BF16)]
    kern = functools.partial(_rwkv_kernel, n_valid=n_valid)
    state_spec = pl.BlockSpec((None, RW_PAIRS, HD_RWKV, LANE), lambda b, j: (b, 0, 0, 0))
    o, s = pl.pallas_call(
        kern,
        grid=(B, T // tb),
        in_specs=[pl.BlockSpec((None, tb, RW_PAD), lambda b, j: (b, j, 0)),
                  pl.BlockSpec((None, 1, RW_PAD), lambda b, j: (b, 0, 0)), state_spec]
                 + [_resident(c.shape) for c in consts],
        out_specs=[pl.BlockSpec((None, tb, n), lambda b, j: (b, j, 0)), state_spec],
        out_shape=[jax.ShapeDtypeStruct((B, T, n), F32), jax.ShapeDtypeStruct((B, RW_PAIRS, HD_RWKV, LANE), F32)],
        scratch_shapes=[pltpu.VMEM((1, RW_PAD), F32), pltpu.VMEM((RW_PAIRS, HD_RWKV, LANE), F32)],
        compiler_params=_params(("arbitrary", "arbitrary")),
        name="rwkv",
    )(pr, shift0, _pair_state(s0), *consts)
    return o, _unpair_state(s)


CMP_PAGES = 32
CHUNKS_PER_PAGE = PAGE_SIZE // CMP_STRIDE


def _compress_kernel(pt_ref, *refs, n_pages, transposed):
    weights = refs[2 * n_pages + 2:2 * n_pages + 10]
    outs = refs[2 * n_pages + 10:2 * n_pages + 12]
    rows = CHUNKS_PER_PAGE * n_pages
    seg = rows + 8
    kinds = range(2)
    low = lax.broadcasted_iota(jnp.int32, (1, N_KV * HD), 1) < HD
    rows_scr = refs[2 * n_pages + 12:2 * n_pages + 14]

    def by_head(row_s):
        heads = [[], []]
        for s in range(0, CMP_STRIDE, 2):
            a, b = row_s(s), row_s(s + 1)
            heads[0].append(jnp.where(low, a, pltpu.roll(b, HD, axis=1)))
            heads[1].append(jnp.where(low, pltpu.roll(a, HD, axis=1), b))
        return [jnp.concatenate(h, axis=1) for h in heads]

    for kind in kinds:
        pages, nxt = refs[kind * n_pages:(kind + 1) * n_pages], refs[2 * n_pages + kind]
        pe_ref, w_ref, b_ref, w2_ref = weights[4 * kind:4 * kind + 4]
        for p, pg in enumerate(pages):
            rows_scr[kind][p * PAGE_SIZE:(p + 1) * PAGE_SIZE, :] = pg[...].T if transposed else pg[...]
        nxt_rows = nxt[...].T[:CMP_STRIDE] if transposed else nxt[...]
        x = by_head(lambda s: rows_scr[kind][pl.ds(s, rows, stride=CMP_STRIDE), :])
        x_next = by_head(lambda s: jnp.broadcast_to(nxt_rows[s:s + 1, :], (8, N_KV * HD)))
        x_all = jnp.concatenate([x[0], x_next[0], x[1], x_next[1]], axis=0)
        h_first = jnp.dot((x_all + pe_ref[0]).astype(BF16), w_ref[0], preferred_element_type=F32)
        h_second = jnp.dot((x_all + pe_ref[1]).astype(BF16), w_ref[1], preferred_element_type=F32)
        out = None
        for h in range(N_KV):
            h_next = pltpu.roll(h_second[h * seg:(h + 1) * seg], seg - 1, axis=0)[:rows]
            hidden = jax.nn.gelu(h_first[h * seg:h * seg + rows] + h_next + b_ref[...])
            part = jnp.dot(hidden.astype(BF16), w2_ref[h], preferred_element_type=F32)
            out = part if out is None else out + part
        outs[kind][...] = out


def _compress_weights(pe, w1, b1, w2):
    n = CMP_STRIDE * HD
    pe2 = pe.reshape(2, 1, n)
    w_halves = w1.reshape(2, n, CMP_HIDDEN).astype(BF16)
    zero = jnp.zeros_like(w2)
    w2_heads = jnp.stack([jnp.concatenate([w2, zero], axis=1), jnp.concatenate([zero, w2], axis=1)]).astype(BF16)
    return [pe2, w_halves, b1.reshape(1, -1), w2_heads]


def _compress(pool, page_table, lw, transposed):
    B, n_pages_total = page_table.shape
    n_pages = min(CMP_PAGES, n_pages_total)
    rows = CHUNKS_PER_PAGE * n_pages
    weights = (_compress_weights(lw['cmp_pe_k'], lw['cmp_w1_k'], lw['cmp_b1_k'], lw['cmp_w2_k'])
               + _compress_weights(lw['cmp_pe_v'], lw['cmp_w1_v'], lw['cmp_b1_v'], lw['cmp_w2_v']))
    width = N_KV * HD
    at = (lambda page, kind: (page, kind, 0)) if transposed else (lambda page, kind: (page, 0, kind))

    def page_map(p, kind):
        return lambda b, j, pt: at(pt[b, j * n_pages + p], kind)

    def next_map(kind):
        return lambda b, j, pt: at(pt[b, jnp.minimum((j + 1) * n_pages, n_pages_total - 1)], kind)

    const = lambda a: pl.BlockSpec(a.shape, lambda b, j, pt: (0,) * a.ndim)
    out_spec = pl.BlockSpec((None, rows, N_KV * HD), lambda b, j, pt: (b, j, 0))
    out_shape = jax.ShapeDtypeStruct((B, n_pages_total * CHUNKS_PER_PAGE, N_KV * HD), F32)
    next_rows = PAGE_SIZE if transposed else CMP_STRIDE
    grid_spec = pltpu.PrefetchScalarGridSpec(
        num_scalar_prefetch=1,
        grid=(B, n_pages_total // n_pages),
        in_specs=[pl.BlockSpec((None, PAGE_SIZE, width), page_map(p, kind)) for kind in range(2) for p in range(n_pages)]
                 + [pl.BlockSpec((None, next_rows, width), next_map(kind)) for kind in range(2)] + [const(a) for a in weights],
        out_specs=[out_spec, out_spec],
        scratch_shapes=[pltpu.VMEM((n_pages * PAGE_SIZE, width), F32)] * 2,
    )
    return pl.pallas_call(
        functools.partial(_compress_kernel, n_pages=n_pages, transposed=transposed),
        grid_spec=grid_spec,
        out_shape=[out_shape, out_shape],
        compiler_params=_params(("arbitrary", "arbitrary")),
        name="compress",
    )(page_table, *([pool] * (2 * n_pages + 2)), *weights)


BAND_ROWS = 1152


def _band_kernel(tab_ref, bkt_ref, o_ref):
    h = pl.program_id(0)
    bkt = bkt_ref[...]
    out = jnp.full(bkt.shape, NEG, F32)
    for b in range(N_BUCKETS):
        out = jnp.where(bkt == b, tab_ref[b, h], out)
    o_ref[...] = out


def _band(rel_table):
    u = np.arange(BAND_ROWS)[:, None]
    qi = np.arange(Q_BLOCK)[None, :]
    d = qi + WINDOW - u
    bkt = np.where(d >= 0, _BUCKET_OF[np.clip(d, 0, len(_BUCKET_OF) - 1)], -1).astype(np.int32)
    return pl.pallas_call(
        _band_kernel,
        grid=(H_NSA,),
        in_specs=[pl.BlockSpec(memory_space=pltpu.SMEM), pl.BlockSpec((BAND_ROWS, Q_BLOCK), lambda h: (0, 0))],
        out_specs=pl.BlockSpec((None, BAND_ROWS, Q_BLOCK), lambda h: (h, 0, 0)),
        out_shape=jax.ShapeDtypeStruct((H_NSA, BAND_ROWS, Q_BLOCK), F32),
        compiler_params=_params(("arbitrary",)),
        name="band",
    )(rel_table, jnp.asarray(bkt))


def _softmax_cols(s):
    m = jnp.max(s, axis=0, keepdims=True)
    e = jnp.exp(s - m)
    l = jnp.sum(e, axis=0, keepdims=True)
    return e * jnp.where(m > 0.5 * NEG, 1.0 / l, 0.0)


def _select_blocks(impsel, qpos, n_pick):
    ns = impsel.shape[0]
    blk = lax.broadcasted_iota(jnp.int32, impsel.shape, 0)
    cur = jnp.right_shift(qpos, 6)
    future = blk * SEL_BLOCK > qpos
    forced = (blk == 0) | (blk == cur) | (blk == cur - 1)
    score = jnp.where(future, -jnp.inf, jnp.where(forced, jnp.inf, impsel))
    chosen = jnp.zeros(impsel.shape, F32)
    for _ in range(n_pick):
        best = jnp.max(score, axis=0, keepdims=True)
        first = jnp.min(jnp.where(score == best, blk, ns), axis=0, keepdims=True)
        hit = (blk == first) & (best > -jnp.inf)
        chosen = jnp.where(hit, 1.0, chosen)
        score = jnp.where(hit, -jnp.inf, score)
    return jnp.where(chosen > 0.0, 0.0, NEG)


def _pool_matrix(ns, nc):
    j = np.arange(ns)[:, None]
    n = np.arange(nc)[None, :]
    ratio = SEL_BLOCK // CMP_STRIDE
    return jnp.asarray((n >= ratio * j - 1) & (n <= ratio * j + ratio - 1), BF16)


def _nsa_prompt_kernel(tab_ref, q_ref, g_ref, kc_ref, vct_ref, ks_ref, vst_ref, kw_ref, vwt_ref, band_ref, pool_ref, o_ref,
                       rhs_scr, mask_scr, acc_scr, m_scr, sc_scr, sa_scr, sb_scr):
    i = pl.program_id(0)
    ncp = kc_ref.shape[0]
    ns = pool_ref.shape[0]
    s0 = i * Q_BLOCK
    q_t = (q_ref[...] * HD ** -0.5).T
    g_t = g_ref[...].T
    lane_q = lax.broadcasted_iota(jnp.int32, (1, G_NSA * Q_BLOCK), 1) & (Q_BLOCK - 1)
    qpos = s0 + lax.broadcasted_iota(jnp.int32, (1, Q_BLOCK), 1)
    rhs_scr[...] = jnp.zeros(rhs_scr.shape, BF16)
    kvs = range(N_KV)
    kd = i // 4
    r = i % 4
    lanes4 = lambda k, f: jnp.concatenate([f(G_NSA * k + g) for g in range(G_NSA)], axis=1)
    qcols = [lanes4(k, lambda h: q_t[h * HD:(h + 1) * HD, :]) for k in kvs]
    zero = jnp.zeros_like(qcols[0])
    top = [jnp.concatenate([qcols[0], zero], axis=0).astype(BF16), jnp.concatenate([zero, qcols[1]], axis=0).astype(BF16)]
    far_row = [lanes4(k, lambda h: band_ref[h, 0:1, :]) for k in kvs]


    n0 = pl.multiple_of(jnp.clip(8 * i - 16, 0, ncp - 32), 8)
    nrow = lax.broadcasted_iota(jnp.int32, (ncp, 1), 0)
    d_edge = qpos - (CMP_STRIDE * (n0 + lax.broadcasted_iota(jnp.int32, (32, 1), 0)) + CMP_BLOCK - 1)

    def compressed(k):
        sc_scr[k] = jnp.dot(kc_ref[...], top[k], preferred_element_type=F32) + jnp.where(nrow < n0, far_row[k], NEG)
        edge_bias = lanes4(k, lambda h: jnp.where(d_edge >= 0, _bias_chain(d_edge, [tab_ref[b, h] for b in range(N_BUCKETS)]), NEG))
        sc_scr[k, pl.ds(n0, 32), :] = jnp.dot(kc_ref[pl.ds(n0, 32), :], top[k], preferred_element_type=F32) + edge_bias
        p_c = _softmax_cols(sc_scr[k])
        o_c = jnp.dot(vct_ref[...], p_c.astype(BF16), preferred_element_type=F32)[k * HD:(k + 1) * HD]
        imp = p_c[:, 0:Q_BLOCK]
        for g in range(1, G_NSA):
            imp = imp + p_c[:, g * Q_BLOCK:(g + 1) * Q_BLOCK]
        return o_c, _dot_exact_lhs(pool_ref[...], imp, terms=2)

    comp = [compressed(k) for k in kvs]

    ws = pl.multiple_of(jnp.maximum(s0 - WINDOW, 0), Q_BLOCK)
    u0 = pl.multiple_of(WINDOW - (s0 - ws), Q_BLOCK)
    n_win = WINDOW + Q_BLOCK
    u = u0 + lax.broadcasted_iota(jnp.int32, (n_win, 1), 0)
    win_mask = jnp.where(u > lane_q, 0.0, NEG)

    def window(k):
        s_w = (jnp.dot(kw_ref[pl.ds(ws, n_win), :], top[k], preferred_element_type=F32)
               + lanes4(k, lambda h: band_ref[h, pl.ds(u0, n_win), :]) + win_mask)
        m_w = jnp.max(s_w, axis=0, keepdims=True)
        p_w = jnp.exp(s_w - m_w).astype(BF16)
        acc_w = jnp.zeros((V_ROWS, G_NSA * Q_BLOCK), F32)
        for j in range(n_win // Q_BLOCK):
            acc_w = acc_w + jnp.dot(vwt_ref[ws // Q_BLOCK + j], p_w[j * Q_BLOCK:(j + 1) * Q_BLOCK], preferred_element_type=F32)
        return acc_w[k * HD:(k + 1) * HD] / acc_w[N_KV * HD:N_KV * HD + 1]

    o_w = [window(k) for k in kvs]

    masks = [_select_blocks(comp[k][1], qpos, min(N_SEL, ns)).astype(BF16) for k in kvs]

    for k in kvs:
        mask_scr[k] = jnp.concatenate([masks[k]] * G_NSA, axis=1)
        m_scr[k] = jnp.full(m_scr.shape[1:], M_INIT, F32)
        acc_scr[k] = jnp.zeros(acc_scr.shape[1:], F32)
        far_hi = far_row[k].astype(BF16).astype(F32)
        rhs_scr[k, 0:HD, :] = qcols[k].astype(BF16)
        rhs_scr[k, SEL_FAR0:SEL_FAR0 + MASK_ROWS, :] = jnp.concatenate(
            [far_hi, far_row[k] - far_hi, jnp.zeros((MASK_ROWS - 2, G_NSA * Q_BLOCK), F32)], axis=0).astype(BF16)

    def scores(k, slab, kts, extra):
        rhs_scr[k, SEL_MASK0:SEL_MASK0 + MASK_ROWS, :] = mask_scr[k, pl.ds(pl.multiple_of(slab * MASK_ROWS, MASK_ROWS), MASK_ROWS), :]
        rhs = rhs_scr[k]
        out = []
        for kt, add in zip(kts, extra):
            s = jnp.dot(ks_ref[k, pl.ds(pl.multiple_of(kt * KEY_TILE, KEY_TILE), KEY_TILE), :], rhs, preferred_element_type=F32)
            out.append(s if add is None else s + add)
        return out

    def update(k, kts, tiles):
        m_old = m_scr[k]
        m_new = m_old
        for s in tiles:
            m_new = jnp.maximum(m_new, jnp.max(s, axis=0, keepdims=True))
        acc = jnp.exp(m_old - m_new) * acc_scr[k]
        for kt, s in zip(kts, tiles):
            acc = acc + jnp.dot(vst_ref[k, kt], jnp.exp(s - m_new).astype(BF16), preferred_element_type=F32)
        acc_scr[k] = acc
        m_scr[k] = m_new

    def attend(slab, kts, extra):
        tiles = [scores(k, slab, kts, extra(k)) for k in kvs]
        for k in kvs:
            update(k, kts, tiles[k])

    near_at = lambda k, start: lanes4(k, lambda h: band_ref[h, pl.ds(pl.multiple_of(start, Q_BLOCK), KEY_TILE), :]) - far_row[k]
    prev_near = (r == 0) & (kd >= 1)
    kd_odd = (kd & 1) == 1
    even_prev = jnp.logical_not(kd_odd) & prev_near
    n_pairs = kd // 2 - even_prev.astype(jnp.int32)
    n_quads = n_pairs // 2
    no_bias = lambda k: [None, None]

    def pair_scores(k, dst, pair):
        lo, hi = scores(k, pair, [2 * pair, 2 * pair + 1], [None, None])
        dst[k, 0:KEY_TILE, :] = lo
        dst[k, KEY_TILE:, :] = hi

    def pair_update(k, src, pair):
        update(k, [2 * pair, 2 * pair + 1], [src[k, 0:KEY_TILE, :], src[k, KEY_TILE:, :]])

    @pl.when(n_quads > 0)
    def _():
        for k in kvs:
            pair_scores(k, sa_scr, 0)

    for k in kvs:
        def quad_body(qd, carry, k=k):
            first = 2 * qd
            pair_scores(k, sb_scr, first + 1)
            pair_update(k, sa_scr, first)
            pair_scores(k, sa_scr, jnp.minimum(first + 2, 2 * n_quads - 2))
            pair_update(k, sb_scr, first + 1)
            return carry

        lax.fori_loop(0, n_quads, quad_body, 0)

    @pl.when((n_pairs & 1) == 1)
    def _():
        attend(n_pairs - 1, [2 * n_pairs - 2, 2 * n_pairs - 1], no_bias)

    @pl.when(kd_odd)
    def _():
        attend(kd // 2, [kd - 1, kd], lambda k: [jnp.where(prev_near, near_at(k, 0), 0.0), near_at(k, KEY_TILE - Q_BLOCK * r)])

    @pl.when(even_prev)
    def _():
        attend(kd // 2 - 1, [kd - 2, kd - 1], lambda k: [None, near_at(k, 0)])

    @pl.when(jnp.logical_not(kd_odd))
    def _():
        attend(kd // 2, [kd], lambda k: [near_at(k, KEY_TILE - Q_BLOCK * r)])

    heads_out = []
    for k in kvs:
        acc = acc_scr[k]
        o_s = acc[0:HD] / acc[HD:HD + 1]
        for g in range(G_NSA):
            h = G_NSA * k + g
            cols = slice(g * Q_BLOCK, (g + 1) * Q_BLOCK)
            heads_out.append(comp[k][0][:, cols] * g_t[h:h + 1] + o_s[:, cols] * g_t[H_NSA + h:H_NSA + h + 1]
                             + o_w[k][:, cols] * g_t[2 * H_NSA + h:2 * H_NSA + h + 1])
    o_ref[...] = jnp.concatenate(heads_out, axis=0).T


def _sel_pattern(rows, width):
    key = np.arange(rows)[:, None]
    b = np.arange(width)[None, :]
    ones = (b >= MASK_ROWS) & (b < MASK_ROWS + 2)
    return jnp.asarray(((key // SEL_BLOCK) % MASK_ROWS == b) | ones, BF16)


def _values_t(v, tile):
    T, n = v.shape
    rows = n + MASK_ROWS
    vt = jnp.concatenate([v.T, jnp.ones((1, T), F32), jnp.zeros((rows - n - 1, T), F32)], axis=0)
    return vt.reshape(rows, T // tile, tile).transpose(1, 0, 2).astype(BF16)


def _nsa_prompt(q, gates, kv, win, kc, vc, band, rel_table):
    T = q.shape[0]
    ncp, ns = kc.shape[0], T // SEL_BLOCK
    width = G_NSA * Q_BLOCK
    k_sel = lambda h: kv[:, 256 + h * HD:256 + (h + 1) * HD]
    v_sel = lambda h: kv[:, 384 + h * HD:384 + (h + 1) * HD]
    pattern = _sel_pattern(T, LANE - HD)
    ks_aug = jnp.stack([jnp.concatenate([k_sel(h).astype(BF16), pattern], axis=1) for h in range(N_KV)])
    operands = [q, gates, kc.astype(BF16), vc.T.astype(BF16), ks_aug, jnp.stack([_values_t(v_sel(h), KEY_TILE) for h in range(N_KV)]),
                win[:, 0:128].astype(BF16), _values_t(win[:, 128:256], Q_BLOCK), band, _pool_matrix(ns, ncp)]
    blk = lambda w: pl.BlockSpec((Q_BLOCK, w), lambda i: (i, 0))
    return pl.pallas_call(
        _nsa_prompt_kernel,
        grid=(T // Q_BLOCK,),
        in_specs=[pl.BlockSpec(memory_space=pltpu.SMEM), blk(H_NSA * HD), blk(LANE)] + [_resident(a.shape) for a in operands[2:]],
        out_specs=blk(H_NSA * HD),
        out_shape=jax.ShapeDtypeStruct((T, H_NSA * HD), F32),
        scratch_shapes=[pltpu.VMEM((N_KV, LANE, width), BF16), pltpu.VMEM((N_KV, ns, width), BF16),
                        pltpu.VMEM((N_KV, V_ROWS_KV, width), F32), pltpu.VMEM((N_KV, 1, width), F32),
                        pltpu.VMEM((N_KV, ncp, width), F32)] + [pltpu.VMEM((N_KV, 2 * KEY_TILE, width), F32)] * 2,
        compiler_params=_params(("arbitrary",)),
        name="nsa_prompt",
    )(rel_table, *operands)


SLAB_PAGES = 8
SMP_PAGES = 16
TOK_PAD = 8
SMP_COLS = H_NSA * TOK_PAD


def _nsa_sample_kernel(pt_ref, *refs, n_pages, n_valid, past):
    pages = refs[:n_pages]
    (q_ref, g_ref, kc_ref, vc_ref, kvn_ref, win_ref, winn_ref, tab_ref, pool_ref, gsum_ref, epat_ref, o_ref,
     top_scr, mask_scr, acc_scr, m_scr, l_scr, oc_scr, ow_scr, nearb_scr) = refs[n_pages:]
    j = pl.program_id(1)
    ncp, wbuf = kc_ref.shape[0], win_ref.shape[1]
    lane = lax.broadcasted_iota(jnp.int32, (1, LANE), 1)
    tok = lane & (TOK_PAD - 1)
    second_kv = lane >= G_NSA * TOK_PAD
    tab = [tab_ref[b:b + 1, :] for b in range(N_BUCKETS)]
    far_row = tab[N_BUCKETS - 1]
    own_rows = lambda x: jnp.where(second_kv, x[HD:2 * HD], x[0:HD])
    pad_rows = lambda x: jnp.concatenate([x, jnp.zeros((LANE - x.shape[0], x.shape[1]), x.dtype)], axis=0)
    trow = lax.broadcasted_iota(jnp.int32, (LANE, 1), 0)
    d_new = tok - trow
    new_bias = jnp.where((d_new >= 0) & (trow < n_valid), _bias_chain(jnp.maximum(d_new, 0), tab), NEG)

    def attend_update(s, values_t):
        m_old = m_scr[...]
        m_new = jnp.maximum(m_old, jnp.max(s, axis=0, keepdims=True))
        alpha = jnp.exp(m_old - m_new)
        p = jnp.exp(s - m_new)
        l_scr[...] = alpha * l_scr[...] + jnp.sum(p, axis=0, keepdims=True)
        acc_scr[...] = alpha * acc_scr[...] + jnp.dot(values_t.astype(BF16), p.astype(BF16), preferred_element_type=F32)
        m_scr[...] = m_new

    def reset():
        m_scr[...] = jnp.full(m_scr.shape, M_INIT, F32)
        l_scr[...] = jnp.zeros(l_scr.shape, F32)
        acc_scr[...] = jnp.zeros(acc_scr.shape, F32)

    @pl.when(j == 0)
    def _():
        q_t = pad_rows(q_ref[...] * HD ** -0.5).T
        halves = []
        for k in range(N_KV):
            part = jnp.zeros((HD, LANE), F32)
            for g in range(G_NSA):
                h = G_NSA * k + g
                piece = q_t[h * HD:(h + 1) * HD, :]
                part = part + (pltpu.roll(piece, TOK_PAD * h, axis=1) if h else piece)
            halves.append(part)
        top = jnp.concatenate(halves, axis=0).astype(BF16)
        top_scr[...] = top
        qpos = past + tok

        n0 = ncp - 32
        kcb = kc_ref[...].astype(BF16)
        d_edge = qpos - (CMP_STRIDE * (n0 + lax.broadcasted_iota(jnp.int32, (32, 1), 0)) + CMP_BLOCK - 1)
        s_c = jnp.concatenate([
            jnp.dot(kcb[:n0], top, preferred_element_type=F32) + far_row,
            jnp.dot(kcb[n0:], top, preferred_element_type=F32) + jnp.where(d_edge >= 0, _bias_chain(jnp.maximum(d_edge, 0), tab), NEG)], axis=0)
        p_c = _softmax_cols(s_c)
        oc_scr[...] = own_rows(jnp.dot(vc_ref[...].T.astype(BF16), p_c.astype(BF16), preferred_element_type=F32))
        imp = _dot_exact_rhs(p_c, gsum_ref[...], terms=3)
        mask_scr[...] = _select_blocks(_dot_exact_lhs(pool_ref[...], imp), qpos, N_SEL)

        wk = win_ref[0:LANE, :].T
        d_w = wbuf + tok - lax.broadcasted_iota(jnp.int32, (wbuf, 1), 0)
        near = wbuf - LANE
        s_w = jnp.dot(wk.astype(BF16), top, preferred_element_type=F32)
        s_w = (jnp.concatenate([s_w[:near] + far_row, s_w[near:] + _bias_chain(d_w[near:], tab)], axis=0)
               + jnp.where(d_w < WINDOW, 0.0, NEG))
        reset()
        attend_update(s_w, win_ref[LANE:, :])
        wn = pad_rows(winn_ref[...])
        attend_update(jnp.dot(wn[:, :LANE].astype(BF16), top, preferred_element_type=F32) + new_bias, wn[:, LANE:].T)
        ow_scr[...] = own_rows(acc_scr[...]) / l_scr[...]
        reset()
        nearb_scr[...] = _bias_chain(LANE + tok - lax.broadcasted_iota(jnp.int32, (LANE, 1), 0), tab)

    n_slabs = n_pages // SLAB_PAGES
    slab_keys = SLAB_PAGES * PAGE_SIZE
    for sub in range(n_slabs):
        tile_pages = pages[sub * SLAB_PAGES:(sub + 1) * SLAB_PAGES]
        k_tile = jnp.concatenate([pg[0:LANE, :].T for pg in tile_pages], axis=0)
        vt_tile = jnp.concatenate([pg[LANE:, :] for pg in tile_pages], axis=1)
        slab = mask_scr[pl.ds(pl.multiple_of((j * n_slabs + sub) * MASK_ROWS, MASK_ROWS), MASK_ROWS), :].astype(BF16)
        rhs = jnp.concatenate([top_scr[...], slab, jnp.zeros((LANE - MASK_ROWS, LANE), BF16)], axis=0)
        s = jnp.dot(jnp.concatenate([k_tile.astype(BF16), epat_ref[...]], axis=1), rhs, preferred_element_type=F32)
        if sub < n_slabs - 1:
            s = s + far_row
        else:
            near = slab_keys - LANE
            s = jnp.concatenate([s[:near] + far_row, s[near:] + jnp.where(j == pl.num_programs(1) - 1, nearb_scr[...], far_row)], axis=0)
        attend_update(s, vt_tile)

    @pl.when(j == pl.num_programs(1) - 1)
    def _():
        kn = pad_rows(kvn_ref[...])
        last_blk = past // SEL_BLOCK
        s_n = (jnp.dot(kn[:, 2 * LANE:3 * LANE].astype(BF16), top_scr[...], preferred_element_type=F32)
               + new_bias + mask_scr[last_blk:last_blk + 1, :])
        attend_update(s_n, kn[:, 3 * LANE:].T)
        o_s = own_rows(acc_scr[...]) / l_scr[...]
        g_t = pad_rows(g_ref[...]).T
        gate_rows = []
        for b in range(3):
            row = g_t[b * H_NSA:b * H_NSA + 1]
            for h in range(1, H_NSA):
                row = row + pltpu.roll(g_t[b * H_NSA + h:b * H_NSA + h + 1], TOK_PAD * h, axis=1)
            gate_rows.append(row)
        o_col = oc_scr[...] * gate_rows[0] + o_s * gate_rows[1] + ow_scr[...] * gate_rows[2]
        per_head = [o_col if h == 0 else pltpu.roll(o_col, LANE - TOK_PAD * h, axis=1) for h in range(H_NSA)]
        o_ref[...] = jnp.concatenate(per_head, axis=0).T[:TOK_PAD]


def _nsa_sample(pool, page_table, q, gates, kc, vc, kv_new, win_buf, win_new, rel_table, n_valid):
    B, n_pages_total = page_table.shape
    past = n_pages_total * PAGE_SIZE
    ncp = kc.shape[1]
    ns = past // SEL_BLOCK + 1
    nsp = -(-ns // MASK_ROWS) * MASK_ROWS
    col = np.arange(LANE)
    used = col < SMP_COLS
    gsum = jnp.asarray(((col[:, None] // (G_NSA * TOK_PAD) == col[None, :] // (G_NSA * TOK_PAD))
                        & (col[:, None] % TOK_PAD == col[None, :] % TOK_PAD) & used[:, None] & used[None, :]), BF16)
    tab_cols = jnp.pad(jnp.repeat(rel_table, TOK_PAD, axis=1), ((0, 0), (0, LANE - SMP_COLS)))
    n_step = min(SMP_PAGES, n_pages_total)
    consts = [tab_cols, _pool_matrix(nsp, ncp), gsum, _sel_pattern(SLAB_PAGES * PAGE_SIZE, LANE)]
    per_seq = [q, gates, kc, vc, kv_new, win_buf, win_new]
    seq_spec = lambda a: pl.BlockSpec((None,) + a.shape[1:], lambda b, j, pt: (b,) + (0,) * (a.ndim - 1))
    const = lambda a: pl.BlockSpec(a.shape, lambda b, j, pt: (0,) * a.ndim)

    def page_map(p):
        return lambda b, j, pt: (pt[b, j * n_step + p], 1, 0)

    grid_spec = pltpu.PrefetchScalarGridSpec(
        num_scalar_prefetch=1,
        grid=(B, n_pages_total // n_step),
        in_specs=[pl.BlockSpec((None, 2 * LANE, PAGE_SIZE), page_map(p)) for p in range(n_step)]
                 + [seq_spec(a) for a in per_seq] + [const(a) for a in consts],
        out_specs=pl.BlockSpec((None, TOK_PAD, H_NSA * HD), lambda b, j, pt: (b, 0, 0)),
        scratch_shapes=[pltpu.VMEM((LANE, LANE), BF16), pltpu.VMEM((nsp, LANE), F32), pltpu.VMEM((LANE, LANE), F32),
                        pltpu.VMEM((1, LANE), F32), pltpu.VMEM((1, LANE), F32), pltpu.VMEM((HD, LANE), F32), pltpu.VMEM((HD, LANE), F32),
                        pltpu.VMEM((LANE, LANE), F32)],
    )
    return pl.pallas_call(
        functools.partial(_nsa_sample_kernel, n_pages=n_step, n_valid=n_valid, past=past),
        grid_spec=grid_spec,
        out_shape=jax.ShapeDtypeStruct((B, TOK_PAD, H_NSA * HD), F32),
        compiler_params=_params(("arbitrary", "arbitrary")),
        name="nsa_sample",
    )(page_table, *([pool] * n_step), *per_seq, *consts)


def _outproj_kernel(x_ref, nsa_ref, rw_ref, gt_ref, lng_ref, lnb_ref, w_ref, o_ref):
    half = H_NSA * HD
    out = (jnp.dot(nsa_ref[...].astype(BF16), w_ref[0:half, :], preferred_element_type=F32)
           + jnp.dot(rw_ref[...].astype(BF16), w_ref[half:, :], preferred_element_type=F32))
    y = ALPHA * x_ref[...] + (1.0 + gt_ref[...]) * out
    o_ref[...] = _layer_norm(y, lng_ref[...], lnb_ref[...])


def _outproj(x, o_nsa, o_rwkv, gate, ln_g, ln_b, w_out):
    rows = x.shape[0]
    tm = min(512, rows)
    row = lambda i: (i, 0)
    return pl.pallas_call(
        _outproj_kernel,
        grid=(rows // tm,),
        in_specs=[pl.BlockSpec((tm, D_MODEL), row), pl.BlockSpec((tm, H_NSA * HD), row), pl.BlockSpec((tm, D_RWKV), row),
                  _mod_spec(gate, tm), _resident((1, D_MODEL)), _resident((1, D_MODEL)), _resident(w_out.shape)],
        out_specs=pl.BlockSpec((tm, D_MODEL), row),
        out_shape=jax.ShapeDtypeStruct((rows, D_MODEL), F32),
        compiler_params=_params(("arbitrary",)),
        name="outproj",
    )(x, o_nsa, o_rwkv, gate, ln_g.reshape(1, -1), ln_b.reshape(1, -1), w_out)


def kernel(x_prompt, x_sample, cache_nsa_kv, cache_nsa_win, state_rwkv_shift, state_rwkv_wkv, page_table, c_prompt, c_sample, rel_table, w_ada, b_ada, ln_g, ln_b, ffn1_gate, ffn1_up, ffn1_down, ffn2_gate, ffn2_up, ffn2_down, w_in, w_out, cmp_pe_k, cmp_w1_k, cmp_b1_k, cmp_w2_k, cmp_pe_v, cmp_w1_v, cmp_b1_v, cmp_w2_v, rwkv_mu, rwkv_w0, rwkv_w2, rwkv_a0, rwkv_a2, rwkv_g2, rwkv_k_k, rwkv_k_a, rwkv_r_k, rwkv_gn_w, rwkv_gn_b):
    assert w_ada.shape[0] == DEPTH == 1 and x_prompt.shape[0] == 1
    l = 0
    lw = dict(cmp_pe_k=cmp_pe_k[l], cmp_w1_k=cmp_w1_k[l], cmp_b1_k=cmp_b1_k[l], cmp_w2_k=cmp_w2_k[l],
              cmp_pe_v=cmp_pe_v[l], cmp_w1_v=cmp_w1_v[l], cmp_b1_v=cmp_b1_v[l], cmp_w2_v=cmp_w2_v[l],
              rwkv_mu=rwkv_mu[l], rwkv_w0=rwkv_w0[l], rwkv_w2=rwkv_w2[l], rwkv_a0=rwkv_a0[l], rwkv_a2=rwkv_a2[l], rwkv_g2=rwkv_g2[l],
              rwkv_k_k=rwkv_k_k[l], rwkv_k_a=rwkv_k_a[l], rwkv_r_k=rwkv_r_k[l], rwkv_gn_w=rwkv_gn_w[l], rwkv_gn_b=rwkv_gn_b[l])
    T = x_prompt.shape[1]
    nb, nt = x_sample.shape[0], x_sample.shape[1]
    assert nt <= TOK_PAD
    n_seq = 1 + nb
    c_all = jnp.concatenate([c_prompt, c_sample, jnp.zeros((-n_seq % 8, D_MODEL), F32)], axis=0)
    mod = _ada(c_all, w_ada[l], b_ada[l])
    mod_p = mod[0:1].reshape(9, 1, D_MODEL)
    mod_s = jnp.repeat(mod[1:n_seq].reshape(nb, 9, D_MODEL), nt, axis=0).transpose(1, 0, 2)
    ffn1 = [w[l].astype(BF16) for w in (ffn1_gate, ffn1_up, ffn1_down)]
    ffn2 = [w[l].astype(BF16) for w in (ffn2_gate, ffn2_up, ffn2_down)]
    w_in_p = _prep_w_in(w_in[l])
    w_out_b = w_out[l].astype(BF16)

    def trunk_in(x, m):
        x1 = _ffn(x, m[0], m[1], m[2], ln_g[l, 0], ln_b[l, 0], *ffn1)
        return x1, _proj(x1, m[3], m[4], w_in_p)

    def trunk_out(x1, o_nsa, o_rwkv, m):
        x2 = _outproj(x1, o_nsa, o_rwkv, m[5], ln_g[l, 1], ln_b[l, 1], w_out_b)
        return _ffn(x2, m[6], m[7], m[8], ln_g[l, 2], ln_b[l, 2], *ffn2)

    xp1, (q, kv, win, gates, pr) = trunk_in(x_prompt[0], mod_p)
    o_rw, wkv_p = _rwkv(pr[None], jnp.zeros((1, 1, RW_PAD), F32), jnp.zeros((1, H_RWKV, HD_RWKV, HD_RWKV), F32), lw, min(RW_STEP, T))
    n_rows = T // PAGE_SIZE
    kc, vc = _compress(kv.reshape(n_rows, PAGE_SIZE, 4 * LANE), jnp.arange(n_rows, dtype=jnp.int32)[None], lw, transposed=False)
    o_nsa = _nsa_prompt(q, gates, kv, win, kc[0], vc[0], _band(rel_table), rel_table)
    y_prompt = trunk_out(xp1, o_nsa, o_rw[0], mod_p)
    kv_prompt = kv.reshape(1, 1, T, 4, N_KV, HD)
    win_prompt = win[T - min(WINDOW, T):].reshape(1, 1, -1, 2, N_KV, HD)
    shift_prompt = _rwkv_uncols(pr[T - 1]).reshape(1, 1, RWKV_COLS)

    xs1, (q_s, kv_s, win_s, gates_s, pr_s) = trunk_in(x_sample.reshape(nb * nt, D_MODEL), mod_s)
    tokens = lambda a: jnp.pad(a.reshape(nb, nt, -1), ((0, 0), (0, TOK_PAD - nt), (0, 0)))
    pr_pad = jnp.pad(pr_s.reshape(nb, nt, -1), ((0, 0), (0, RW_TOK_PAD - nt), (0, 0)))
    o_rw_s, wkv_s = _rwkv(pr_pad, _rwkv_cols(state_rwkv_shift[l])[:, None], state_rwkv_wkv[l], lw, nt)
    pool_t = jnp.transpose(cache_nsa_kv[l], (0, 2, 3, 4, 1)).reshape(-1, 4 * LANE, PAGE_SIZE)
    kc_s, vc_s = _compress(pool_t, page_table, lw, transposed=True)
    win_buf = cache_nsa_win[l]
    win_t = jnp.transpose(win_buf, (0, 2, 3, 4, 1)).reshape(nb, 2 * LANE, -1)
    o_nsa_s = _nsa_sample(pool_t, page_table, tokens(q_s), tokens(gates_s), kc_s, vc_s, tokens(kv_s),
                          win_t, tokens(win_s), rel_table, nt)
    y_sample = trunk_out(xs1, o_nsa_s[:, :nt].reshape(nb * nt, -1), o_rw_s[:, :nt].reshape(nb * nt, -1), mod_s)
    kv_sample = kv_s.reshape(1, nb, nt, 4, N_KV, HD)
    win_sample = jnp.concatenate([win_buf, win_s.reshape(nb, nt, 2, N_KV, HD)], axis=1)[None, :, nt:]
    shift_sample = _rwkv_uncols(pr_s.reshape(nb, nt, -1)[:, -1])[None]
    return (y_prompt[None], y_sample.reshape(nb, nt, D_MODEL), kv_prompt, win_prompt, shift_prompt, wkv_p[None],
            kv_sample, win_sample, shift_sample, wkv_s[None])
```

```python
import functools
import math

import numpy as np
import jax
import jax.numpy as jnp
from jax import lax
from jax.experimental import pallas as pl
from jax.experimental.pallas import tpu as pltpu

D_MODEL = 1024
PAGE_SIZE = 128
H_NSA = 8
N_KV = 2
G_NSA = H_NSA // N_KV
HD = 64
CMP_STRIDE = 16
CMP_BLOCK = 2 * CMP_STRIDE
CMP_HIDDEN = 256
SEL_BLOCK = 64
N_SEL = 16
WINDOW = 512
Q_BLOCK = 128
N_BUCKETS = 32
MAX_DISTANCE = 128
H_RWKV = 8
HD_RWKV = 64
D_RWKV = H_RWKV * HD_RWKV
DECAY_LORA = 32
AAA_LORA = 32
GATE_LORA = 96
GN_EPS = 64e-5
D_FF = 2816
LN_EPS = 1e-5
DEPTH = 1
ALPHA = (2 * DEPTH) ** 0.25

NSA_SIZES = (H_NSA * HD,) + (N_KV * HD,) * 6 + (H_NSA * 3,)
RWKV_SIZES = (D_RWKV, D_RWKV, D_RWKV, DECAY_LORA, AAA_LORA, GATE_LORA)
NSA_COLS = sum(NSA_SIZES)
RWKV_COLS = sum(RWKV_SIZES)

F32 = jnp.float32
BF16 = jnp.bfloat16
LANE = 128
NEG = -(2.0 ** 100)
M_INIT = -(2.0 ** 103)
VMEM_LIMIT = 56 * 1024 * 1024

RW_PAD = 3 * D_RWKV + 3 * LANE
P_Q, P_KV, P_WIN, P_GATE, P_RW = 0, 512, 1024, 1280, 1408
P_COLS = P_RW + RW_PAD
KEY_TILE = 512
MASK_ROWS = 16
V_ROWS = 144
MASK_ROW0 = N_KV * HD
SEL_MASK0 = HD
SEL_FAR0 = SEL_MASK0 + MASK_ROWS
V_ROWS_KV = HD + MASK_ROWS
CMP_CLASS_ROWS = 256


def _bucket_lows():
    d = np.arange(0, 4 * MAX_DISTANCE, dtype=np.int64)
    max_exact = N_BUCKETS // 2
    df = np.maximum(d, 1).astype(np.float32)
    large = max_exact + (np.log(df / np.float32(max_exact)) / np.float32(math.log(MAX_DISTANCE / max_exact))
                         * np.float32(N_BUCKETS - max_exact)).astype(np.int32)
    b = np.where(d < max_exact, d, np.minimum(large, N_BUCKETS - 1))
    lows = [int(np.argmax(b >= k)) for k in range(N_BUCKETS)]
    return b, lows


_BUCKET_OF, _BUCKET_LOW = _bucket_lows()
FAR_DIST = _BUCKET_LOW[N_BUCKETS - 1]


def _resident(shape):
    nd = len(shape)
    return pl.BlockSpec(shape, lambda *_: (0,) * nd, pipeline_mode=pl.Buffered(1))


def _params(sem):
    return pltpu.CompilerParams(dimension_semantics=sem, vmem_limit_bytes=VMEM_LIMIT)


def _split2(x):
    hi = x.astype(BF16)
    lo = (x - hi.astype(F32)).astype(BF16)
    return hi, lo


def _dot_exact_rhs(x, rhs_bf16, terms=2):
    acc = None
    rem = x
    for _ in range(terms):
        part = rem.astype(BF16)
        d = jnp.dot(part, rhs_bf16, preferred_element_type=F32)
        acc = d if acc is None else acc + d
        rem = rem - part.astype(F32)
    return acc


def _dot_exact_lhs(lhs_bf16, x, terms=3):
    acc = None
    rem = x
    for _ in range(terms):
        part = rem.astype(BF16)
        d = jnp.dot(lhs_bf16, part, preferred_element_type=F32)
        acc = d if acc is None else acc + d
        rem = rem - part.astype(F32)
    return acc


def _layer_norm(y, g, b):
    mu = jnp.mean(y, axis=-1, keepdims=True)
    yc = y - mu
    var = jnp.mean(yc * yc, axis=-1, keepdims=True)
    return yc * lax.rsqrt(var + LN_EPS) * g + b


def _bias_chain(d, tab_rows):
    out = tab_rows[0] + jnp.zeros(d.shape, F32)
    for b in range(1, N_BUCKETS):
        out = jnp.where(d >= _BUCKET_LOW[b], tab_rows[b], out)
    return out


def _ada_kernel(c_ref, w_ref, b_ref, o_ref):
    c = c_ref[...]
    h = (c * jax.nn.sigmoid(c)).astype(BF16)
    o_ref[...] = jnp.dot(h, w_ref[...].astype(BF16), preferred_element_type=F32) + b_ref[...]


def _ada(c_all, w_ada, b_ada):
    rows, n = c_all.shape[0], w_ada.shape[1]
    tn = 1152
    return pl.pallas_call(
        _ada_kernel,
        grid=(n // tn,),
        in_specs=[pl.BlockSpec((rows, D_MODEL), lambda j: (0, 0)),
                  pl.BlockSpec((D_MODEL, tn), lambda j: (0, j)),
                  pl.BlockSpec((1, tn), lambda j: (0, j))],
        out_specs=pl.BlockSpec((rows, tn), lambda j: (0, j)),
        out_shape=jax.ShapeDtypeStruct((rows, n), F32),
        compiler_params=_params(("arbitrary",)),
        name="ada",
    )(c_all, w_ada, b_ada.reshape(1, n))


FF_CHUNKS = 2


def _ffn_kernel(x_ref, sh_ref, sc_ref, gt_ref, lng_ref, lnb_ref, wg_ref, wu_ref, wd_ref, o_ref):
    x = x_ref[...]
    h = (x * (1.0 + sc_ref[...]) + sh_ref[...]).astype(BF16)
    ck = D_FF // FF_CHUNKS
    acc = jnp.zeros(x.shape, F32)
    for c in range(FF_CHUNKS):
        a = jnp.dot(h, wg_ref[:, c * ck:(c + 1) * ck], preferred_element_type=F32)
        b = jnp.dot(h, wu_ref[:, c * ck:(c + 1) * ck], preferred_element_type=F32)
        t = (a * jax.nn.sigmoid(a) * b).astype(BF16)
        acc = acc + jnp.dot(t, wd_ref[c * ck:(c + 1) * ck, :], preferred_element_type=F32)
    y = ALPHA * x + (1.0 + gt_ref[...]) * (0.5 * acc)
    o_ref[...] = _layer_norm(y, lng_ref[...], lnb_ref[...])


def _mod_spec(mod, tm):
    if mod.shape[0] == 1:
        return pl.BlockSpec((1, D_MODEL), lambda i: (0, 0))
    return pl.BlockSpec((tm, D_MODEL), lambda i: (i, 0))


def _ffn(x, shift, scale, gate, ln_g, ln_b, wg, wu, wd):
    rows = x.shape[0]
    tm = min(512, rows)
    row = lambda i: (i, 0)
    return pl.pallas_call(
        _ffn_kernel,
        grid=(rows // tm,),
        in_specs=[pl.BlockSpec((tm, D_MODEL), row), _mod_spec(shift, tm), _mod_spec(scale, tm), _mod_spec(gate, tm),
                  _resident((1, D_MODEL)), _resident((1, D_MODEL)),
                  _resident((D_MODEL, D_FF)), _resident((D_MODEL, D_FF)), _resident((D_FF, D_MODEL))],
        out_specs=pl.BlockSpec((tm, D_MODEL), row),
        out_shape=jax.ShapeDtypeStruct((rows, D_MODEL), F32),
        compiler_params=_params(("arbitrary",)),
        name="ffn",
    )(x, shift, scale, gate, ln_g.reshape(1, -1), ln_b.reshape(1, -1), wg, wu, wd)


def _proj_kernel(x_ref, sh_ref, sc_ref, w_ref, q_ref, kv_ref, win_ref, g_ref, pr_ref):
    h = (x_ref[...] * (1.0 + sc_ref[...]) + sh_ref[...]).astype(BF16)
    p = jnp.dot(h, w_ref[...], preferred_element_type=F32)
    q_ref[...] = p[:, P_Q:P_KV]
    kv_ref[...] = p[:, P_KV:P_WIN]
    win_ref[...] = p[:, P_WIN:P_GATE]
    g_ref[...] = jax.nn.sigmoid(p[:, P_GATE:P_RW])
    pr_ref[...] = p[:, P_RW:P_COLS]


def _proj(x, shift, scale, w_in_p):
    rows = x.shape[0]
    tm = min(512, rows)
    row = lambda i: (i, 0)
    widths = (512, 512, 256, LANE, RW_PAD)
    return pl.pallas_call(
        _proj_kernel,
        grid=(rows // tm,),
        in_specs=[pl.BlockSpec((tm, D_MODEL), row), _mod_spec(shift, tm), _mod_spec(scale, tm),
                  _resident((D_MODEL, P_COLS))],
        out_specs=[pl.BlockSpec((tm, w), row) for w in widths],
        out_shape=[jax.ShapeDtypeStruct((rows, w), F32) for w in widths],
        compiler_params=_params(("arbitrary",)),
        name="proj",
    )(x, shift, scale, w_in_p)


def _prep_w_in(w_in):
    pad = lambda a, n: jnp.pad(a, ((0, 0), (0, n - a.shape[1])))
    nsa, rw = w_in[:, :NSA_COLS], w_in[:, NSA_COLS:]
    gl = nsa[:, 1280:1304].reshape(D_MODEL, H_NSA, 3).transpose(0, 2, 1).reshape(D_MODEL, 3 * H_NSA)
    cols = [nsa[:, :1280], pad(gl, LANE), _rwkv_cols(rw)]
    return jnp.concatenate(cols, axis=1).astype(BF16)


def _rwkv_cols(a):
    pad = lambda t: jnp.pad(t, [(0, 0)] * (t.ndim - 1) + [(0, LANE - t.shape[-1])])
    n = 3 * D_RWKV
    return jnp.concatenate([a[..., :n], pad(a[..., n:n + 32]), pad(a[..., n + 32:n + 64]), pad(a[..., n + 64:n + 160])], axis=-1)


def _rwkv_uncols(a):
    n = 3 * D_RWKV
    return jnp.concatenate([a[..., :n], a[..., n:n + 32], a[..., n + LANE:n + LANE + 32], a[..., n + 2 * LANE:n + 2 * LANE + 96]], axis=-1)


RW_GROUP = 64
RW_TOK_PAD = 16
RW_PAIRS = H_RWKV // 2
RW_BLOCK = 64
RW_STEP = 256


def _lora(x, w_ref):
    w = w_ref[...]
    w_hi = w.astype(BF16)
    w_lo = (w - w_hi.astype(F32)).astype(BF16)
    return _dot_exact_rhs(x, w_hi) + jnp.dot(x.astype(BF16), w_lo, preferred_element_type=F32)


def _rwkv_kernel(pr_ref, sh0_ref, s0_ref, mu_ref, w0_ref, a0_ref, kk_ref, ka_ref, rk_ref, gw_ref, gb_ref,
                 w2_ref, a2_ref, g2_ref, bo_ref, lgrp_ref, ggrp_ref, o_ref, sout_ref,
                 prev_scr, s_scr, *, n_valid):
    tb = pr_ref.shape[0]
    step = pl.program_id(1)

    @pl.when(step == 0)
    def _():
        prev_scr[...] = sh0_ref[...]
        s_scr[...] = s0_ref[...]

    p = pr_ref[...]
    rows = lax.broadcasted_iota(jnp.int32, (tb, 1), 0)
    prev = jnp.where(rows == 0, prev_scr[...], pltpu.roll(p, 1, axis=0))
    prev_scr[...] = p[tb - 1:tb, :]
    xs = p + (prev - p) * mu_ref[...]
    n = D_RWKV
    r, k, v = xs[:, :n], xs[:, n:2 * n], xs[:, 2 * n:3 * n]
    wl, al, gl = xs[:, 3 * n:3 * n + LANE], xs[:, 3 * n + LANE:3 * n + 2 * LANE], xs[:, 3 * n + 2 * LANE:]
    z = -(w0_ref[...] + _lora(jnp.tanh(wl), w2_ref))
    w = -(jnp.maximum(z, 0.0) + jnp.log(1.0 + jnp.exp(-jnp.abs(z)))) - 0.5
    a = jax.nn.sigmoid(a0_ref[...] + _lora(al, a2_ref))
    g = _lora(jax.nn.sigmoid(gl), g2_ref)
    kk = k * kk_ref[...]
    ss = _dot_exact_rhs(kk * kk, bo_ref[...])
    kk = kk / jnp.maximum(jnp.sqrt(ss), 1e-12)
    k2 = k * (1.0 + (a - 1.0) * ka_ref[...])
    G = min(RW_GROUP, tb)
    log_dec = -jnp.exp(w)
    bet = kk * a
    if n_valid < tb:
        live = rows < n_valid
        log_dec, kk, bet, k2, v_in = (jnp.where(live, x, 0.0) for x in (log_dec, kk, bet, k2, v))
    else:
        v_in = v
    cum = _dot_exact_lhs(lgrp_ref[...], log_dec, terms=2)
    cum_end = _dot_exact_lhs(ggrp_ref[...], log_dec, terms=2)
    gam_inv = jnp.exp(-cum)
    gam_end = jnp.exp(cum_end - cum)
    k_hat = -kk * jnp.exp(cum - log_dec)
    r_hat = r * jnp.exp(cum)
    b_chk, k_chk = bet * gam_inv, k2 * gam_inv
    b_til, k_til = bet * gam_end, k2 * gam_end
    gam_group = jnp.exp(cum_end)

    lane = lax.broadcasted_iota(jnp.int32, (1, LANE), 1)
    low = lane < HD_RWKV
    lane_t = lane & (RW_BLOCK - 1)
    row = lax.broadcasted_iota(jnp.int32, (LANE, 1), 0)
    row_t = row & (RW_BLOCK - 1)
    same = ((row < HD_RWKV) == low) & ((row_t // G) == (lane_t // G))
    strict, incl = same & (lane_t < row_t), same & (lane_t <= row_t)
    bf = lambda x: x.astype(BF16)
    mm = lambda x, y: jnp.dot(bf(x), bf(y), preferred_element_type=F32)
    mm_nt = lambda x, y: lax.dot_general(bf(x), bf(y), (((1,), (1,)), ((), ())), preferred_element_type=F32)

    def rows_bd(x):
        if tb < RW_BLOCK:
            x = jnp.concatenate([x, jnp.zeros((RW_BLOCK - tb, LANE), F32)], axis=0)
        return jnp.concatenate([jnp.where(low, x, 0.0), jnp.where(low, 0.0, x)], axis=0)

    def mm3(x, y):
        xh, yh = bf(x), bf(y)
        xl, yl = bf(x - xh.astype(F32)), bf(y - yh.astype(F32))
        return jnp.dot(jnp.concatenate([xh, xl, xh], axis=1), jnp.concatenate([yh, yh, yl], axis=0), preferred_element_type=F32)

    units = max(1, tb // RW_BLOCK)
    unit_rows = min(tb, RW_BLOCK)
    pairs = range(RW_PAIRS)
    items = [(un, pp) for un in range(units) for pp in pairs]
    at = lambda x, it: x[it[0] * unit_rows:(it[0] + 1) * unit_rows, it[1] * LANE:(it[1] + 1) * LANE]
    kh_row = [rows_bd(at(k_hat, it)) for it in items]
    rh_row = [rows_bd(at(r_hat, it)) for it in items]
    kh_mat = [x.T for x in kh_row]
    rh_mat = [x.T for x in rh_row]
    state_in = [jnp.concatenate([rows_bd(at(b_chk, it)).T, rows_bd(at(k_chk, it)).T], axis=1) for it in items]
    upd_rows = [jnp.concatenate([rows_bd(at(b_til, it)), rows_bd(at(k_til, it))], axis=0) for it in items]
    v_t = [rows_bd(at(v_in, it)).T for it in items]
    v_t = [x[:HD_RWKV] + x[HD_RWKV:] for x in v_t]
    c_all = [mm(jnp.concatenate([kh_row[n], rh_row[n]], axis=0), state_in[n]) for n in range(len(items))]
    c_uu = [jnp.where(strict, c[:LANE, :LANE], 0.0) for c in c_all]
    c_uv = [jnp.where(strict, c[:LANE, LANE:], 0.0) for c in c_all]
    c_ru = [jnp.where(incl, c[LANE:, :LANE], 0.0) for c in c_all]
    c_rv = [jnp.where(incl, c[LANE:, LANE:], 0.0) for c in c_all]
    t_neu, power = list(c_uu), list(c_uu)
    span = 2
    while span < G:
        power = [mm(x, x) for x in power]
        t_neu = [t_neu[n] + power[n] + mm(t_neu[n], power[n]) for n in range(len(items))]
        span *= 2
    from_v = [mm_nt(v_t[n], c_uv[n]) for n in range(len(items))]
    st = [s_scr[pp] for pp in pairs]
    y_units = []
    for un in range(units):
        ns = [un * RW_PAIRS + pp for pp in pairs]
        y_t = [jnp.zeros((HD_RWKV, LANE), F32) for _ in pairs]
        for grp in range(unit_rows // G):
            here = (lane_t // G) == grp
            first = un * unit_rows + grp * G
            w_t = [jnp.where(here, mm(st[pp], kh_mat[ns[pp]]) + from_v[ns[pp]], 0.0) for pp in pairs]
            u_t = [w_t[pp] + mm_nt(w_t[pp], t_neu[ns[pp]]) for pp in pairs]
            v_g = [jnp.where(here, v_t[ns[pp]], 0.0) for pp in pairs]
            y_t = [y_t[pp] + jnp.where(here, mm(st[pp], rh_mat[ns[pp]]), 0.0) + mm_nt(u_t[pp], c_ru[ns[pp]])
                   + mm_nt(v_g[pp], c_rv[ns[pp]]) for pp in pairs]
            st = [st[pp] * gam_group[first:first + 1, pp * LANE:(pp + 1) * LANE]
                  + mm3(jnp.concatenate([u_t[pp], v_g[pp]], axis=1), upd_rows[ns[pp]]) for pp in pairs]
        lane_u = lax.broadcasted_iota(jnp.int32, (unit_rows, LANE), 1)
        pieces = []
        for pp in pairs:
            yt = jnp.concatenate([y_t[pp], jnp.zeros((LANE - HD_RWKV, LANE), F32)], axis=0).T
            pieces.append(jnp.where(lane_u < HD_RWKV, yt[:unit_rows], pltpu.roll(yt[RW_BLOCK:RW_BLOCK + unit_rows], HD_RWKV, axis=1)))
        y_units.append(jnp.concatenate(pieces, axis=1))
    for pp in pairs:
        s_scr[pp] = st[pp]
    y = y_units[0] if units == 1 else jnp.concatenate(y_units, axis=0)
    mean = _dot_exact_rhs(y, bo_ref[...]) * (1.0 / HD_RWKV)
    yc = y - mean
    var = _dot_exact_rhs(yc * yc, bo_ref[...]) * (1.0 / HD_RWKV)
    yn = yc * lax.rsqrt(var + GN_EPS) * gw_ref[...] + gb_ref[...]
    bonus = _dot_exact_rhs(r * k2 * rk_ref[...], bo_ref[...]) * v
    o_ref[...] = (yn + bonus) * g
    sout_ref[...] = s_scr[...]


def _pair_state(s):
    B = s.shape[0]
    return s.reshape(B, RW_PAIRS, 2, HD_RWKV, HD_RWKV).transpose(0, 1, 3, 2, 4).reshape(B, RW_PAIRS, HD_RWKV, LANE)


def _unpair_state(s):
    B = s.shape[0]
    return s.reshape(B, RW_PAIRS, HD_RWKV, 2, HD_RWKV).transpose(0, 1, 3, 2, 4).reshape(B, H_RWKV, HD_RWKV, HD_RWKV)


def _rwkv(pr, shift0, s0, lw, n_valid):
    B, T, _ = pr.shape
    tb = min(RW_STEP, T)
    n = D_RWKV
    vec = lambda a: a.reshape(1, n)
    padrow = lambda a: jnp.pad(a, ((0, LANE - a.shape[0]), (0, 0)))
    blk = np.arange(n) // HD_RWKV
    block_ones = jnp.asarray(blk[:, None] == blk[None, :], BF16)
    tok = np.arange(tb)
    group = min(RW_GROUP, tb)
    same_group = tok[:, None] // group == tok[None, :] // group
    prefix = jnp.asarray(same_group & (tok[None, :] <= tok[:, None]), BF16)
    consts = [_rwkv_cols(lw['rwkv_mu']).reshape(1, RW_PAD), vec(lw['rwkv_w0']), vec(lw['rwkv_a0']), vec(lw['rwkv_k_k']),
              vec(lw['rwkv_k_a']), vec(lw['rwkv_r_k']), vec(lw['rwkv_gn_w']), vec(lw['rwkv_gn_b']),
              padrow(lw['rwkv_w2']), padrow(lw['rwkv_a2']), padrow(lw['rwkv_g2']), block_ones, prefix, jnp.asarray(same_group, BF16)]
    kern = functools.partial(_rwkv_kernel, n_valid=n_valid)
    state_spec = pl.BlockSpec((None, RW_PAIRS, HD_RWKV, LANE), lambda b, j: (b, 0, 0, 0))
    o, s = pl.pallas_call(
        kern,
        grid=(B, T // tb),
        in_specs=[pl.BlockSpec((None, tb, RW_PAD), lambda b, j: (b, j, 0)),
                  pl.BlockSpec((None, 1, RW_PAD), lambda b, j: (b, 0, 0)), state_spec]
                 + [_resident(c.shape) for c in consts],
        out_specs=[pl.BlockSpec((None, tb, n), lambda b, j: (b, j, 0)), state_spec],
        out_shape=[jax.ShapeDtypeStruct((B, T, n), F32), jax.ShapeDtypeStruct((B, RW_PAIRS, HD_RWKV, LANE), F32)],
        scratch_shapes=[pltpu.VMEM((1, RW_PAD), F32), pltpu.VMEM((RW_PAIRS, HD_RWKV, LANE), F32)],
        compiler_params=_params(("arbitrary", "arbitrary")),
        name="rwkv",
    )(pr, shift0, _pair_state(s0), *consts)
    return o, _unpair_state(s)


CMP_PAGES = 64
CHUNKS_PER_PAGE = PAGE_SIZE // CMP_STRIDE


def _compress_kernel(pt_ref, *refs, n_pages, transposed):
    weights = refs[2 * n_pages + 2:2 * n_pages + 10]
    outs = refs[2 * n_pages + 10:2 * n_pages + 12]
    rows = CHUNKS_PER_PAGE * n_pages
    seg = rows + 8
    kinds = range(2)
    low = lax.broadcasted_iota(jnp.int32, (1, N_KV * HD), 1) < HD
    rows_scr = refs[2 * n_pages + 12:2 * n_pages + 14]

    def by_head(row_s):
        heads = [[], []]
        for s in range(0, CMP_STRIDE, 2):
            a, b = row_s(s), row_s(s + 1)
            heads[0].append(jnp.where(low, a, pltpu.roll(b, HD, axis=1)))
            heads[1].append(jnp.where(low, pltpu.roll(a, HD, axis=1), b))
        return [jnp.concatenate(h, axis=1) for h in heads]

    for kind in kinds:
        pages, nxt = refs[kind * n_pages:(kind + 1) * n_pages], refs[2 * n_pages + kind]
        pe_ref, w_ref, b_ref, w2_ref = weights[4 * kind:4 * kind + 4]
        for p, pg in enumerate(pages):
            rows_scr[kind][p * PAGE_SIZE:(p + 1) * PAGE_SIZE, :] = pg[...].T if transposed else pg[...]
        nxt_rows = nxt[...].T[:CMP_STRIDE] if transposed else nxt[...]
        x = by_head(lambda s: rows_scr[kind][pl.ds(s, rows, stride=CMP_STRIDE), :])
        x_next = by_head(lambda s: jnp.broadcast_to(nxt_rows[s:s + 1, :], (8, N_KV * HD)))
        x_all = jnp.concatenate([x[0], x_next[0], x[1], x_next[1]], axis=0)
        h_first = jnp.dot((x_all + pe_ref[0]).astype(BF16), w_ref[0], preferred_element_type=F32)
        h_second = jnp.dot((x_all + pe_ref[1]).astype(BF16), w_ref[1], preferred_element_type=F32)
        out = None
        for h in range(N_KV):
            h_next = pltpu.roll(h_second[h * seg:(h + 1) * seg], seg - 1, axis=0)[:rows]
            hidden = jax.nn.gelu(h_first[h * seg:h * seg + rows] + h_next + b_ref[...])
            part = jnp.dot(hidden.astype(BF16), w2_ref[h], preferred_element_type=F32)
            out = part if out is None else out + part
        outs[kind][...] = out


def _compress_weights(pe, w1, b1, w2):
    n = CMP_STRIDE * HD
    pe2 = pe.reshape(2, 1, n)
    w_halves = w1.reshape(2, n, CMP_HIDDEN).astype(BF16)
    zero = jnp.zeros_like(w2)
    w2_heads = jnp.stack([jnp.concatenate([w2, zero], axis=1), jnp.concatenate([zero, w2], axis=1)]).astype(BF16)
    return [pe2, w_halves, b1.reshape(1, -1), w2_heads]


def _compress(pool, page_table, lw, transposed):
    B, n_pages_total = page_table.shape
    n_pages = min(CMP_PAGES, n_pages_total)
    rows = CHUNKS_PER_PAGE * n_pages
    weights = (_compress_weights(lw['cmp_pe_k'], lw['cmp_w1_k'], lw['cmp_b1_k'], lw['cmp_w2_k'])
               + _compress_weights(lw['cmp_pe_v'], lw['cmp_w1_v'], lw['cmp_b1_v'], lw['cmp_w2_v']))
    width = N_KV * HD
    at = (lambda page, kind: (page, kind, 0)) if transposed else (lambda page, kind: (page, 0, kind))

    def page_map(p, kind):
        return lambda b, j, pt: at(pt[b, j * n_pages + p], kind)

    def next_map(kind):
        return lambda b, j, pt: at(pt[b, jnp.minimum((j + 1) * n_pages, n_pages_total - 1)], kind)

    const = lambda a: pl.BlockSpec(a.shape, lambda b, j, pt: (0,) * a.ndim)
    out_spec = pl.BlockSpec((None, rows, N_KV * HD), lambda b, j, pt: (b, j, 0))
    out_shape = jax.ShapeDtypeStruct((B, n_pages_total * CHUNKS_PER_PAGE, N_KV * HD), F32)
    next_rows = PAGE_SIZE if transposed else CMP_STRIDE
    grid_spec = pltpu.PrefetchScalarGridSpec(
        num_scalar_prefetch=1,
        grid=(B, n_pages_total // n_pages),
        in_specs=[pl.BlockSpec((None, PAGE_SIZE, width), page_map(p, kind)) for kind in range(2) for p in range(n_pages)]
                 + [pl.BlockSpec((None, next_rows, width), next_map(kind)) for kind in range(2)] + [const(a) for a in weights],
        out_specs=[out_spec, out_spec],
        scratch_shapes=[pltpu.VMEM((n_pages * PAGE_SIZE, width), F32)] * 2,
    )
    return pl.pallas_call(
        functools.partial(_compress_kernel, n_pages=n_pages, transposed=transposed),
        grid_spec=grid_spec,
        out_shape=[out_shape, out_shape],
        compiler_params=_params(("arbitrary", "arbitrary")),
        name="compress",
    )(page_table, *([pool] * (2 * n_pages + 2)), *weights)


BAND_ROWS = 1152


def _band_kernel(tab_ref, bkt_ref, o_ref):
    h = pl.program_id(0)
    bkt = bkt_ref[...]
    out = jnp.full(bkt.shape, NEG, F32)
    for b in range(N_BUCKETS):
        out = jnp.where(bkt == b, tab_ref[b, h], out)
    o_ref[...] = out


def _band(rel_table):
    u = np.arange(BAND_ROWS)[:, None]
    qi = np.arange(Q_BLOCK)[None, :]
    d = qi + WINDOW - u
    bkt = np.where(d >= 0, _BUCKET_OF[np.clip(d, 0, len(_BUCKET_OF) - 1)], -1).astype(np.int32)
    return pl.pallas_call(
        _band_kernel,
        grid=(H_NSA,),
        in_specs=[pl.BlockSpec(memory_space=pltpu.SMEM), pl.BlockSpec((BAND_ROWS, Q_BLOCK), lambda h: (0, 0))],
        out_specs=pl.BlockSpec((None, BAND_ROWS, Q_BLOCK), lambda h: (h, 0, 0)),
        out_shape=jax.ShapeDtypeStruct((H_NSA, BAND_ROWS, Q_BLOCK), F32),
        compiler_params=_params(("arbitrary",)),
        name="band",
    )(rel_table, jnp.asarray(bkt))


def _softmax_cols(s):
    m = jnp.max(s, axis=0, keepdims=True)
    e = jnp.exp(s - m)
    l = jnp.sum(e, axis=0, keepdims=True)
    return e * jnp.where(m > 0.5 * NEG, 1.0 / l, 0.0)


def _select_blocks(impsel, qpos, n_pick):
    ns = impsel.shape[0]
    blk = lax.broadcasted_iota(jnp.int32, impsel.shape, 0)
    cur = jnp.right_shift(qpos, 6)
    future = blk * SEL_BLOCK > qpos
    forced = (blk == 0) | (blk == cur) | (blk == cur - 1)
    score = jnp.where(future, -jnp.inf, jnp.where(forced, jnp.inf, impsel))
    chosen = jnp.zeros(impsel.shape, F32)
    for _ in range(n_pick):
        best = jnp.max(score, axis=0, keepdims=True)
        first = jnp.min(jnp.where(score == best, blk, ns), axis=0, keepdims=True)
        hit = (blk == first) & (best > -jnp.inf)
        chosen = jnp.where(hit, 1.0, chosen)
        score = jnp.where(hit, -jnp.inf, score)
    return jnp.where(chosen > 0.0, 0.0, NEG)


def _pool_matrix(ns, nc):
    j = np.arange(ns)[:, None]
    n = np.arange(nc)[None, :]
    ratio = SEL_BLOCK // CMP_STRIDE
    return jnp.asarray((n >= ratio * j - 1) & (n <= ratio * j + ratio - 1), BF16)


def _nsa_prompt_kernel(tab_ref, q_ref, g_ref, kc_ref, vct_ref, ks_ref, vst_ref, kw_ref, vwt_ref, band_ref, pool_ref, o_ref,
                       rhs_scr, mask_scr, acc_scr, m_scr, sc_scr, sa_scr, sb_scr, oc_scr):
    i = pl.program_id(0)
    ncp = kc_ref.shape[0]
    ns = pool_ref.shape[0]
    s0 = i * Q_BLOCK
    q_t = (q_ref[...] * HD ** -0.5).T
    g_t = g_ref[...].T
    lane_q = lax.broadcasted_iota(jnp.int32, (1, G_NSA * Q_BLOCK), 1) & (Q_BLOCK - 1)
    qpos = s0 + lax.broadcasted_iota(jnp.int32, (1, Q_BLOCK), 1)
    rhs_scr[...] = jnp.zeros(rhs_scr.shape, BF16)
    kvs = range(N_KV)
    kd = i // 4
    r = i % 4
    lanes4 = lambda k, f: jnp.concatenate([f(G_NSA * k + g) for g in range(G_NSA)], axis=1)
    qcols = [lanes4(k, lambda h: q_t[h * HD:(h + 1) * HD, :]) for k in kvs]
    zero = jnp.zeros_like(qcols[0])
    top = [jnp.concatenate([qcols[0], zero], axis=0).astype(BF16), jnp.concatenate([zero, qcols[1]], axis=0).astype(BF16)]
    far_row = [lanes4(k, lambda h: band_ref[h, 0:1, :]) for k in kvs]


    n0 = pl.multiple_of(jnp.clip(8 * i - 16, 0, ncp - 32), 8)
    nrow = lax.broadcasted_iota(jnp.int32, (ncp, 1), 0)
    d_edge = qpos - (CMP_STRIDE * (n0 + lax.broadcasted_iota(jnp.int32, (32, 1), 0)) + CMP_BLOCK - 1)

    def compressed(k, rows):
        sc_scr[k, 0:rows, :] = (jnp.dot(kc_ref[0:rows, :], top[k], preferred_element_type=F32)
                                + jnp.where(nrow[0:rows] < n0, far_row[k], NEG))
        edge_bias = lanes4(k, lambda h: jnp.where(d_edge >= 0, _bias_chain(d_edge, [tab_ref[b, h] for b in range(N_BUCKETS)]), NEG))
        sc_scr[k, pl.ds(n0, 32), :] = jnp.dot(kc_ref[pl.ds(n0, 32), :], top[k], preferred_element_type=F32) + edge_bias
        p_c = _softmax_cols(sc_scr[k, 0:rows, :])
        o_c = jnp.dot(vct_ref[:, 0:rows], p_c.astype(BF16), preferred_element_type=F32)[k * HD:(k + 1) * HD]
        imp = p_c[:, 0:Q_BLOCK]
        for g in range(1, G_NSA):
            imp = imp + p_c[:, g * Q_BLOCK:(g + 1) * Q_BLOCK]
        n_blk = rows * CMP_STRIDE // SEL_BLOCK
        return o_c, _dot_exact_lhs(pool_ref[0:n_blk, 0:rows], imp, terms=2)

    size_step = min(CMP_CLASS_ROWS, ncp)
    size_class = (n0 + 32 - 1) // size_step
    for cls in range(ncp // size_step):
        @pl.when(size_class == cls)
        def _(rows=(cls + 1) * size_step):
            comp = [compressed(k, rows) for k in kvs]
            n_blk = comp[0][1].shape[0]
            masks = [_select_blocks(comp[k][1], qpos, min(N_SEL, ns)) for k in kvs]
            for k in kvs:
                oc_scr[k] = comp[k][0]
                full = jnp.concatenate([masks[k], jnp.full((ns - n_blk, Q_BLOCK), NEG, F32)], axis=0) if n_blk < ns else masks[k]
                mask_scr[k] = jnp.concatenate([full.astype(BF16)] * G_NSA, axis=1)

    ws = pl.multiple_of(jnp.maximum(s0 - WINDOW, 0), Q_BLOCK)
    u0 = pl.multiple_of(WINDOW - (s0 - ws), Q_BLOCK)
    n_win = WINDOW + Q_BLOCK
    u = u0 + lax.broadcasted_iota(jnp.int32, (n_win, 1), 0)
    win_mask = jnp.where(u > lane_q, 0.0, NEG)

    def window(k):
        s_w = (jnp.dot(kw_ref[pl.ds(ws, n_win), :], top[k], preferred_element_type=F32)
               + lanes4(k, lambda h: band_ref[h, pl.ds(u0, n_win), :]) + win_mask)
        m_w = jnp.max(s_w, axis=0, keepdims=True)
        p_w = jnp.exp(s_w - m_w).astype(BF16)
        acc_w = jnp.zeros((V_ROWS, G_NSA * Q_BLOCK), F32)
        for j in range(n_win // Q_BLOCK):
            acc_w = acc_w + jnp.dot(vwt_ref[ws // Q_BLOCK + j], p_w[j * Q_BLOCK:(j + 1) * Q_BLOCK], preferred_element_type=F32)
        return acc_w[k * HD:(k + 1) * HD] / acc_w[N_KV * HD:N_KV * HD + 1]

    o_w = [window(k) for k in kvs]

    for k in kvs:
        m_scr[k] = jnp.full(m_scr.shape[1:], M_INIT, F32)
        acc_scr[k] = jnp.zeros(acc_scr.shape[1:], F32)
        far_hi = far_row[k].astype(BF16).astype(F32)
        rhs_scr[k, 0:HD, :] = qcols[k].astype(BF16)
        rhs_scr[k, SEL_FAR0:SEL_FAR0 + MASK_ROWS, :] = jnp.concatenate(
            [far_hi, far_row[k] - far_hi, jnp.zeros((MASK_ROWS - 2, G_NSA * Q_BLOCK), F32)], axis=0).astype(BF16)

    def scores(k, slab, kts, extra):
        rhs_scr[k, SEL_MASK0:SEL_MASK0 + MASK_ROWS, :] = mask_scr[k, pl.ds(pl.multiple_of(slab * MASK_ROWS, MASK_ROWS), MASK_ROWS), :]
        rhs = rhs_scr[k]
        out = []
        for kt, add in zip(kts, extra):
            s = jnp.dot(ks_ref[k, pl.ds(pl.multiple_of(kt * KEY_TILE, KEY_TILE), KEY_TILE), :], rhs, preferred_element_type=F32)
            out.append(s if add is None else s + add)
        return out

    def update(k, kts, tiles):
        m_old = m_scr[k]
        m_new = m_old
        for s in tiles:
            m_new = jnp.maximum(m_new, jnp.max(s, axis=0, keepdims=True))
        acc = jnp.exp(m_old - m_new) * acc_scr[k]
        for kt, s in zip(kts, tiles):
            acc = acc + jnp.dot(vst_ref[k, kt], jnp.exp(s - m_new).astype(BF16), preferred_element_type=F32)
        acc_scr[k] = acc
        m_scr[k] = m_new

    def attend(slab, kts, extra):
        tiles = [scores(k, slab, kts, extra(k)) for k in kvs]
        for k in kvs:
            update(k, kts, tiles[k])

    near_at = lambda k, start: lanes4(k, lambda h: band_ref[h, pl.ds(pl.multiple_of(start, Q_BLOCK), KEY_TILE), :]) - far_row[k]
    prev_near = (r == 0) & (kd >= 1)
    kd_odd = (kd & 1) == 1
    even_prev = jnp.logical_not(kd_odd) & prev_near
    n_pairs = kd // 2 - even_prev.astype(jnp.int32)
    n_quads = n_pairs // 2
    no_bias = lambda k: [None, None]

    def pair_scores(k, dst, pair):
        lo, hi = scores(k, pair, [2 * pair, 2 * pair + 1], [None, None])
        dst[k, 0:KEY_TILE, :] = lo
        dst[k, KEY_TILE:, :] = hi

    def pair_update(k, src, pair):
        update(k, [2 * pair, 2 * pair + 1], [src[k, 0:KEY_TILE, :], src[k, KEY_TILE:, :]])

    @pl.when(n_quads > 0)
    def _():
        for k in kvs:
            pair_scores(k, sa_scr, 0)

    for k in kvs:
        def quad_body(qd, carry, k=k):
            first = 2 * qd
            pair_scores(k, sb_scr, first + 1)
            pair_update(k, sa_scr, first)
            pair_scores(k, sa_scr, jnp.minimum(first + 2, 2 * n_quads - 2))
            pair_update(k, sb_scr, first + 1)
            return carry

        lax.fori_loop(0, n_quads, quad_body, 0)

    @pl.when((n_pairs & 1) == 1)
    def _():
        attend(n_pairs - 1, [2 * n_pairs - 2, 2 * n_pairs - 1], no_bias)

    @pl.when(kd_odd)
    def _():
        attend(kd // 2, [kd - 1, kd], lambda k: [jnp.where(prev_near, near_at(k, 0), 0.0), near_at(k, KEY_TILE - Q_BLOCK * r)])

    @pl.when(even_prev)
    def _():
        attend(kd // 2 - 1, [kd - 2, kd - 1], lambda k: [None, near_at(k, 0)])

    @pl.when(jnp.logical_not(kd_odd))
    def _():
        attend(kd // 2, [kd], lambda k: [near_at(k, KEY_TILE - Q_BLOCK * r)])

    heads_out = []
    for k in kvs:
        acc = acc_scr[k]
        o_s = acc[0:HD] / acc[HD:HD + 1]
        o_c = oc_scr[k]
        for g in range(G_NSA):
            h = G_NSA * k + g
            cols = slice(g * Q_BLOCK, (g + 1) * Q_BLOCK)
            heads_out.append(o_c[:, cols] * g_t[h:h + 1] + o_s[:, cols] * g_t[H_NSA + h:H_NSA + h + 1]
                             + o_w[k][:, cols] * g_t[2 * H_NSA + h:2 * H_NSA + h + 1])
    o_ref[...] = jnp.concatenate(heads_out, axis=0).T


def _sel_pattern(rows, width):
    key = np.arange(rows)[:, None]
    b = np.arange(width)[None, :]
    ones = (b >= MASK_ROWS) & (b < MASK_ROWS + 2)
    return jnp.asarray(((key // SEL_BLOCK) % MASK_ROWS == b) | ones, BF16)


def _values_t(v, tile):
    T, n = v.shape
    rows = n + MASK_ROWS
    vt = jnp.concatenate([v.T, jnp.ones((1, T), F32), jnp.zeros((rows - n - 1, T), F32)], axis=0)
    return vt.reshape(rows, T // tile, tile).transpose(1, 0, 2).astype(BF16)


def _nsa_prompt(q, gates, kv, win, kc, vc, band, rel_table):
    T = q.shape[0]
    ncp, ns = kc.shape[0], T // SEL_BLOCK
    width = G_NSA * Q_BLOCK
    k_sel = lambda h: kv[:, 256 + h * HD:256 + (h + 1) * HD]
    v_sel = lambda h: kv[:, 384 + h * HD:384 + (h + 1) * HD]
    pattern = _sel_pattern(T, LANE - HD)
    ks_aug = jnp.stack([jnp.concatenate([k_sel(h).astype(BF16), pattern], axis=1) for h in range(N_KV)])
    operands = [q, gates, kc.astype(BF16), vc.T.astype(BF16), ks_aug, jnp.stack([_values_t(v_sel(h), KEY_TILE) for h in range(N_KV)]),
                win[:, 0:128].astype(BF16), _values_t(win[:, 128:256], Q_BLOCK), band, _pool_matrix(ns, ncp)]
    blk = lambda w: pl.BlockSpec((Q_BLOCK, w), lambda i: (i, 0))
    return pl.pallas_call(
        _nsa_prompt_kernel,
        grid=(T // Q_BLOCK,),
        in_specs=[pl.BlockSpec(memory_space=pltpu.SMEM), blk(H_NSA * HD), blk(LANE)] + [_resident(a.shape) for a in operands[2:]],
        out_specs=blk(H_NSA * HD),
        out_shape=jax.ShapeDtypeStruct((T, H_NSA * HD), F32),
        scratch_shapes=[pltpu.VMEM((N_KV, LANE, width), BF16), pltpu.VMEM((N_KV, ns, width), BF16),
                        pltpu.VMEM((N_KV, V_ROWS_KV, width), F32), pltpu.VMEM((N_KV, 1, width), F32),
                        pltpu.VMEM((N_KV, ncp, width), F32)] + [pltpu.VMEM((N_KV, 2 * KEY_TILE, width), F32)] * 2
                       + [pltpu.VMEM((N_KV, HD, width), F32)],
        compiler_params=_params(("arbitrary",)),
        name="nsa_prompt",
    )(rel_table, *operands)


SLAB_PAGES = 8
SMP_PAGES = 32
TOK_PAD = 8
SMP_COLS = H_NSA * TOK_PAD


def _nsa_sample_kernel(pt_ref, *refs, n_pages, n_valid, past):
    pages = refs[:n_pages]
    (q_ref, g_ref, kc_ref, vc_ref, kvn_ref, win_ref, winn_ref, tab_ref, pool_ref, gsum_ref, epat_ref, o_ref,
     top_scr, mask_scr, acc_scr, m_scr, l_scr, oc_scr, ow_scr, nearb_scr) = refs[n_pages:]
    j = pl.program_id(1)
    ncp, wbuf = kc_ref.shape[0], win_ref.shape[1]
    lane = lax.broadcasted_iota(jnp.int32, (1, LANE), 1)
    tok = lane & (TOK_PAD - 1)
    second_kv = lane >= G_NSA * TOK_PAD
    tab = [tab_ref[b:b + 1, :] for b in range(N_BUCKETS)]
    far_row = tab[N_BUCKETS - 1]
    own_rows = lambda x: jnp.where(second_kv, x[HD:2 * HD], x[0:HD])
    pad_rows = lambda x: jnp.concatenate([x, jnp.zeros((LANE - x.shape[0], x.shape[1]), x.dtype)], axis=0)
    trow = lax.broadcasted_iota(jnp.int32, (LANE, 1), 0)
    d_new = tok - trow
    new_bias = jnp.where((d_new >= 0) & (trow < n_valid), _bias_chain(jnp.maximum(d_new, 0), tab), NEG)

    def attend_update(s, values_t):
        m_old = m_scr[...]
        m_new = jnp.maximum(m_old, jnp.max(s, axis=0, keepdims=True))
        alpha = jnp.exp(m_old - m_new)
        p = jnp.exp(s - m_new)
        l_scr[...] = alpha * l_scr[...] + jnp.sum(p, axis=0, keepdims=True)
        acc_scr[...] = alpha * acc_scr[...] + jnp.dot(values_t.astype(BF16), p.astype(BF16), preferred_element_type=F32)
        m_scr[...] = m_new

    def reset():
        m_scr[...] = jnp.full(m_scr.shape, M_INIT, F32)
        l_scr[...] = jnp.zeros(l_scr.shape, F32)
        acc_scr[...] = jnp.zeros(acc_scr.shape, F32)

    @pl.when(j == 0)
    def _():
        q_t = pad_rows(q_ref[...] * HD ** -0.5).T
        halves = []
        for k in range(N_KV):
            part = jnp.zeros((HD, LANE), F32)
            for g in range(G_NSA):
                h = G_NSA * k + g
                piece = q_t[h * HD:(h + 1) * HD, :]
                part = part + (pltpu.roll(piece, TOK_PAD * h, axis=1) if h else piece)
            halves.append(part)
        top = jnp.concatenate(halves, axis=0).astype(BF16)
        top_scr[...] = top
        qpos = past + tok

        n0 = ncp - 32
        kcb = kc_ref[...].astype(BF16)
        d_edge = qpos - (CMP_STRIDE * (n0 + lax.broadcasted_iota(jnp.int32, (32, 1), 0)) + CMP_BLOCK - 1)
        s_c = jnp.concatenate([
            jnp.dot(kcb[:n0], top, preferred_element_type=F32) + far_row,
            jnp.dot(kcb[n0:], top, preferred_element_type=F32) + jnp.where(d_edge >= 0, _bias_chain(jnp.maximum(d_edge, 0), tab), NEG)], axis=0)
        p_c = _softmax_cols(s_c)
        oc_scr[...] = own_rows(jnp.dot(vc_ref[...].T.astype(BF16), p_c.astype(BF16), preferred_element_type=F32))
        imp = _dot_exact_rhs(p_c, gsum_ref[...], terms=3)
        mask_scr[...] = _select_blocks(_dot_exact_lhs(pool_ref[...], imp), qpos, N_SEL)

        wk = win_ref[0:LANE, :].T
        d_w = wbuf + tok - lax.broadcasted_iota(jnp.int32, (wbuf, 1), 0)
        near = wbuf - LANE
        s_w = jnp.dot(wk.astype(BF16), top, preferred_element_type=F32)
        s_w = (jnp.concatenate([s_w[:near] + far_row, s_w[near:] + _bias_chain(d_w[near:], tab)], axis=0)
               + jnp.where(d_w < WINDOW, 0.0, NEG))
        reset()
        attend_update(s_w, win_ref[LANE:, :])
        wn = pad_rows(winn_ref[...])
        attend_update(jnp.dot(wn[:, :LANE].astype(BF16), top, preferred_element_type=F32) + new_bias, wn[:, LANE:].T)
        ow_scr[...] = own_rows(acc_scr[...]) / l_scr[...]
        reset()
        nearb_scr[...] = _bias_chain(LANE + tok - lax.broadcasted_iota(jnp.int32, (LANE, 1), 0), tab)

    n_slabs = n_pages // SLAB_PAGES
    slab_keys = SLAB_PAGES * PAGE_SIZE
    for sub in range(n_slabs):
        tile_pages = pages[sub * SLAB_PAGES:(sub + 1) * SLAB_PAGES]
        k_tile = jnp.concatenate([pg[0:LANE, :].T for pg in tile_pages], axis=0)
        vt_tile = jnp.concatenate([pg[LANE:, :] for pg in tile_pages], axis=1)
        slab = mask_scr[pl.ds(pl.multiple_of((j * n_slabs + sub) * MASK_ROWS, MASK_ROWS), MASK_ROWS), :].astype(BF16)
        rhs = jnp.concatenate([top_scr[...], slab, jnp.zeros((LANE - MASK_ROWS, LANE), BF16)], axis=0)
        s = jnp.dot(jnp.concatenate([k_tile.astype(BF16), epat_ref[...]], axis=1), rhs, preferred_element_type=F32)
        if sub < n_slabs - 1:
            s = s + far_row
        else:
            near = slab_keys - LANE
            s = jnp.concatenate([s[:near] + far_row, s[near:] + jnp.where(j == pl.num_programs(1) - 1, nearb_scr[...], far_row)], axis=0)
        attend_update(s, vt_tile)

    @pl.when(j == pl.num_programs(1) - 1)
    def _():
        kn = pad_rows(kvn_ref[...])
        last_blk = past // SEL_BLOCK
        s_n = (jnp.dot(kn[:, 2 * LANE:3 * LANE].astype(BF16), top_scr[...], preferred_element_type=F32)
               + new_bias + mask_scr[last_blk:last_blk + 1, :])
        attend_update(s_n, kn[:, 3 * LANE:].T)
        o_s = own_rows(acc_scr[...]) / l_scr[...]
        g_t = pad_rows(g_ref[...]).T
        gate_rows = []
        for b in range(3):
            row = g_t[b * H_NSA:b * H_NSA + 1]
            for h in range(1, H_NSA):
                row = row + pltpu.roll(g_t[b * H_NSA + h:b * H_NSA + h + 1], TOK_PAD * h, axis=1)
            gate_rows.append(row)
        o_col = oc_scr[...] * gate_rows[0] + o_s * gate_rows[1] + ow_scr[...] * gate_rows[2]
        per_head = [o_col if h == 0 else pltpu.roll(o_col, LANE - TOK_PAD * h, axis=1) for h in range(H_NSA)]
        o_ref[...] = jnp.concatenate(per_head, axis=0).T[:TOK_PAD]


def _nsa_sample(pool, page_table, q, gates, kc, vc, kv_new, win_buf, win_new, rel_table, n_valid):
    B, n_pages_total = page_table.shape
    past = n_pages_total * PAGE_SIZE
    ncp = kc.shape[1]
    ns = past // SEL_BLOCK + 1
    nsp = -(-ns // MASK_ROWS) * MASK_ROWS
    col = np.arange(LANE)
    used = col < SMP_COLS
    gsum = jnp.asarray(((col[:, None] // (G_NSA * TOK_PAD) == col[None, :] // (G_NSA * TOK_PAD))
                        & (col[:, None] % TOK_PAD == col[None, :] % TOK_PAD) & used[:, None] & used[None, :]), BF16)
    tab_cols = jnp.pad(jnp.repeat(rel_table, TOK_PAD, axis=1), ((0, 0), (0, LANE - SMP_COLS)))
    n_step = min(SMP_PAGES, n_pages_total)
    consts = [tab_cols, _pool_matrix(nsp, ncp), gsum, _sel_pattern(SLAB_PAGES * PAGE_SIZE, LANE)]
    per_seq = [q, gates, kc, vc, kv_new, win_buf, win_new]
    seq_spec = lambda a: pl.BlockSpec((None,) + a.shape[1:], lambda b, j, pt: (b,) + (0,) * (a.ndim - 1))
    const = lambda a: pl.BlockSpec(a.shape, lambda b, j, pt: (0,) * a.ndim)

    def page_map(p):
        return lambda b, j, pt: (pt[b, j * n_step + p], 1, 0)

    grid_spec = pltpu.PrefetchScalarGridSpec(
        num_scalar_prefetch=1,
        grid=(B, n_pages_total // n_step),
        in_specs=[pl.BlockSpec((None, 2 * LANE, PAGE_SIZE), page_map(p)) for p in range(n_step)]
                 + [seq_spec(a) for a in per_seq] + [const(a) for a in consts],
        out_specs=pl.BlockSpec((None, TOK_PAD, H_NSA * HD), lambda b, j, pt: (b, 0, 0)),
        scratch_shapes=[pltpu.VMEM((LANE, LANE), BF16), pltpu.VMEM((nsp, LANE), F32), pltpu.VMEM((LANE, LANE), F32),
                        pltpu.VMEM((1, LANE), F32), pltpu.VMEM((1, LANE), F32), pltpu.VMEM((HD, LANE), F32), pltpu.VMEM((HD, LANE), F32),
                        pltpu.VMEM((LANE, LANE), F32)],
    )
    return pl.pallas_call(
        functools.partial(_nsa_sample_kernel, n_pages=n_step, n_valid=n_valid, past=past),
        grid_spec=grid_spec,
        out_shape=jax.ShapeDtypeStruct((B, TOK_PAD, H_NSA * HD), F32),
        compiler_params=_params(("arbitrary", "arbitrary")),
        name="nsa_sample",
    )(page_table, *([pool] * n_step), *per_seq, *consts)


def _outproj_kernel(x_ref, nsa_ref, rw_ref, gt_ref, lng_ref, lnb_ref, w_ref, o_ref):
    half = H_NSA * HD
    out = (jnp.dot(nsa_ref[...].astype(BF16), w_ref[0:half, :], preferred_element_type=F32)
           + jnp.dot(rw_ref[...].astype(BF16), w_ref[half:, :], preferred_element_type=F32))
    y = ALPHA * x_ref[...] + (1.0 + gt_ref[...]) * out
    o_ref[...] = _layer_norm(y, lng_ref[...], lnb_ref[...])


def _outproj(x, o_nsa, o_rwkv, gate, ln_g, ln_b, w_out):
    rows = x.shape[0]
    tm = min(512, rows)
    row = lambda i: (i, 0)
    return pl.pallas_call(
        _outproj_kernel,
        grid=(rows // tm,),
        in_specs=[pl.BlockSpec((tm, D_MODEL), row), pl.BlockSpec((tm, H_NSA * HD), row), pl.BlockSpec((tm, D_RWKV), row),
                  _mod_spec(gate, tm), _resident((1, D_MODEL)), _resident((1, D_MODEL)), _resident(w_out.shape)],
        out_specs=pl.BlockSpec((tm, D_MODEL), row),
        out_shape=jax.ShapeDtypeStruct((rows, D_MODEL), F32),
        compiler_params=_params(("arbitrary",)),
        name="outproj",
    )(x, o_nsa, o_rwkv, gate, ln_g.reshape(1, -1), ln_b.reshape(1, -1), w_out)


def kernel(x_prompt, x_sample, cache_nsa_kv, cache_nsa_win, state_rwkv_shift, state_rwkv_wkv, page_table, c_prompt, c_sample, rel_table, w_ada, b_ada, ln_g, ln_b, ffn1_gate, ffn1_up, ffn1_down, ffn2_gate, ffn2_up, ffn2_down, w_in, w_out, cmp_pe_k, cmp_w1_k, cmp_b1_k, cmp_w2_k, cmp_pe_v, cmp_w1_v, cmp_b1_v, cmp_w2_v, rwkv_mu, rwkv_w0, rwkv_w2, rwkv_a0, rwkv_a2, rwkv_g2, rwkv_k_k, rwkv_k_a, rwkv_r_k, rwkv_gn_w, rwkv_gn_b):
    assert w_ada.shape[0] == DEPTH == 1 and x_prompt.shape[0] == 1
    l = 0
    lw = dict(cmp_pe_k=cmp_pe_k[l], cmp_w1_k=cmp_w1_k[l], cmp_b1_k=cmp_b1_k[l], cmp_w2_k=cmp_w2_k[l],
              cmp_pe_v=cmp_pe_v[l], cmp_w1_v=cmp_w1_v[l], cmp_b1_v=cmp_b1_v[l], cmp_w2_v=cmp_w2_v[l],
              rwkv_mu=rwkv_mu[l], rwkv_w0=rwkv_w0[l], rwkv_w2=rwkv_w2[l], rwkv_a0=rwkv_a0[l], rwkv_a2=rwkv_a2[l], rwkv_g2=rwkv_g2[l],
              rwkv_k_k=rwkv_k_k[l], rwkv_k_a=rwkv_k_a[l], rwkv_r_k=rwkv_r_k[l], rwkv_gn_w=rwkv_gn_w[l], rwkv_gn_b=rwkv_gn_b[l])
    T = x_prompt.shape[1]
    nb, nt = x_sample.shape[0], x_sample.shape[1]
    assert nt <= TOK_PAD
    n_seq = 1 + nb
    c_all = jnp.concatenate([c_prompt, c_sample, jnp.zeros((-n_seq % 8, D_MODEL), F32)], axis=0)
    mod = _ada(c_all, w_ada[l], b_ada[l])
    mod_p = mod[0:1].reshape(9, 1, D_MODEL)
    mod_s = jnp.repeat(mod[1:n_seq].reshape(nb, 9, D_MODEL), nt, axis=0).transpose(1, 0, 2)
    ffn1 = [w[l].astype(BF16) for w in (ffn1_gate, ffn1_up, ffn1_down)]
    ffn2 = [w[l].astype(BF16) for w in (ffn2_gate, ffn2_up, ffn2_down)]
    w_in_p = _prep_w_in(w_in[l])
    w_out_b = w_out[l].astype(BF16)

    def trunk_in(x, m):
        x1 = _ffn(x, m[0], m[1], m[2], ln_g[l, 0], ln_b[l, 0], *ffn1)
        return x1, _proj(x1, m[3], m[4], w_in_p)

    def trunk_out(x1, o_nsa, o_rwkv, m):
        x2 = _outproj(x1, o_nsa, o_rwkv, m[5], ln_g[l, 1], ln_b[l, 1], w_out_b)
        return _ffn(x2, m[6], m[7], m[8], ln_g[l, 2], ln_b[l, 2], *ffn2)

    xp1, (q, kv, win, gates, pr) = trunk_in(x_prompt[0], mod_p)
    o_rw, wkv_p = _rwkv(pr[None], jnp.zeros((1, 1, RW_PAD), F32), jnp.zeros((1, H_RWKV, HD_RWKV, HD_RWKV), F32), lw, min(RW_STEP, T))
    n_rows = T // PAGE_SIZE
    kc, vc = _compress(kv.reshape(n_rows, PAGE_SIZE, 4 * LANE), jnp.arange(n_rows, dtype=jnp.int32)[None], lw, transposed=False)
    o_nsa = _nsa_prompt(q, gates, kv, win, kc[0], vc[0], _band(rel_table), rel_table)
    y_prompt = trunk_out(xp1, o_nsa, o_rw[0], mod_p)
    kv_prompt = kv.reshape(1, 1, T, 4, N_KV, HD)
    win_prompt = win[T - min(WINDOW, T):].reshape(1, 1, -1, 2, N_KV, HD)
    shift_prompt = _rwkv_uncols(pr[T - 1]).reshape(1, 1, RWKV_COLS)

    xs1, (q_s, kv_s, win_s, gates_s, pr_s) = trunk_in(x_sample.reshape(nb * nt, D_MODEL), mod_s)
    tokens = lambda a: jnp.pad(a.reshape(nb, nt, -1), ((0, 0), (0, TOK_PAD - nt), (0, 0)))
    pr_pad = jnp.pad(pr_s.reshape(nb, nt, -1), ((0, 0), (0, RW_TOK_PAD - nt), (0, 0)))
    o_rw_s, wkv_s = _rwkv(pr_pad, _rwkv_cols(state_rwkv_shift[l])[:, None], state_rwkv_wkv[l], lw, nt)
    pool_t = jnp.transpose(cache_nsa_kv[l], (0, 2, 3, 4, 1)).reshape(-1, 4 * LANE, PAGE_SIZE)
    kc_s, vc_s = _compress(pool_t, page_table, lw, transposed=True)
    win_buf = cache_nsa_win[l]
    win_t = jnp.transpose(win_buf, (0, 2, 3, 4, 1)).reshape(nb, 2 * LANE, -1)
    o_nsa_s = _nsa_sample(pool_t, page_table, tokens(q_s), tokens(gates_s), kc_s, vc_s, tokens(kv_s),
                          win_t, tokens(win_s), rel_table, nt)
    y_sample = trunk_out(xs1, o_nsa_s[:, :nt].reshape(nb * nt, -1), o_rw_s[:, :nt].reshape(nb * nt, -1), mod_s)
    kv_sample = kv_s.reshape(1, nb, nt, 4, N_KV, HD)
    win_sample = jnp.concatenate([win_buf, win_s.reshape(nb, nt, 2, N_KV, HD)], axis=1)[None, :, nt:]
    shift_sample = _rwkv_uncols(pr_s.reshape(nb, nt, -1)[:, -1])[None]
    return (y_prompt[None], y_sample.reshape(nb, nt, D_MODEL), kv_prompt, win_prompt, shift_prompt, wkv_p[None],
            kv_sample, win_sample, shift_sample, wkv_s[None])
```

```python
import functools
import math

import numpy as np
import jax
import jax.numpy as jnp
from jax import lax
from jax.experimental import pallas as pl
from jax.experimental.pallas import tpu as pltpu

D_MODEL = 1024
PAGE_SIZE = 128
H_NSA = 8
N_KV = 2
G_NSA = H_NSA // N_KV
HD = 64
CMP_STRIDE = 16
CMP_BLOCK = 2 * CMP_STRIDE
CMP_HIDDEN = 256
SEL_BLOCK = 64
N_SEL = 16
WINDOW = 512
Q_BLOCK = 128
N_BUCKETS = 32
MAX_DISTANCE = 128
H_RWKV = 8
HD_RWKV = 64
D_RWKV = H_RWKV * HD_RWKV
DECAY_LORA = 32
AAA_LORA = 32
GATE_LORA = 96
GN_EPS = 64e-5
D_FF = 2816
LN_EPS = 1e-5
DEPTH = 1
ALPHA = (2 * DEPTH) ** 0.25

NSA_SIZES = (H_NSA * HD,) + (N_KV * HD,) * 6 + (H_NSA * 3,)
RWKV_SIZES = (D_RWKV, D_RWKV, D_RWKV, DECAY_LORA, AAA_LORA, GATE_LORA)
NSA_COLS = sum(NSA_SIZES)
RWKV_COLS = sum(RWKV_SIZES)

F32 = jnp.float32
BF16 = jnp.bfloat16
LANE = 128
NEG = -(2.0 ** 100)
M_INIT = -(2.0 ** 103)
VMEM_LIMIT = 56 * 1024 * 1024

RW_PAD = 3 * D_RWKV + 3 * LANE
P_Q, P_KV, P_WIN, P_GATE, P_RW = 0, 512, 1024, 1280, 1408
P_COLS = P_RW + RW_PAD
KEY_TILE = 512
MASK_ROWS = 16
V_ROWS = 144
MASK_ROW0 = N_KV * HD
SEL_MASK0 = HD
SEL_FAR0 = SEL_MASK0 + MASK_ROWS
V_ROWS_KV = HD + MASK_ROWS
CMP_CLASS_ROWS = 256


def _bucket_lows():
    d = np.arange(0, 4 * MAX_DISTANCE, dtype=np.int64)
    max_exact = N_BUCKETS // 2
    df = np.maximum(d, 1).astype(np.float32)
    large = max_exact + (np.log(df / np.float32(max_exact)) / np.float32(math.log(MAX_DISTANCE / max_exact))
                         * np.float32(N_BUCKETS - max_exact)).astype(np.int32)
    b = np.where(d < max_exact, d, np.minimum(large, N_BUCKETS - 1))
    lows = [int(np.argmax(b >= k)) for k in range(N_BUCKETS)]
    return b, lows


_BUCKET_OF, _BUCKET_LOW = _bucket_lows()
FAR_DIST = _BUCKET_LOW[N_BUCKETS - 1]


def _resident(shape):
    nd = len(shape)
    return pl.BlockSpec(shape, lambda *_: (0,) * nd, pipeline_mode=pl.Buffered(1))


def _params(sem):
    return pltpu.CompilerParams(dimension_semantics=sem, vmem_limit_bytes=VMEM_LIMIT)


def _split2(x):
    hi = x.astype(BF16)
    lo = (x - hi.astype(F32)).astype(BF16)
    return hi, lo


def _dot_exact_rhs(x, rhs_bf16, terms=2):
    acc = None
    rem = x
    for _ in range(terms):
        part = rem.astype(BF16)
        d = jnp.dot(part, rhs_bf16, preferred_element_type=F32)
        acc = d if acc is None else acc + d
        rem = rem - part.astype(F32)
    return acc


def _dot_exact_lhs(lhs_bf16, x, terms=3):
    acc = None
    rem = x
    for _ in range(terms):
        part = rem.astype(BF16)
        d = jnp.dot(lhs_bf16, part, preferred_element_type=F32)
        acc = d if acc is None else acc + d
        rem = rem - part.astype(F32)
    return acc


def _layer_norm(y, g, b):
    mu = jnp.mean(y, axis=-1, keepdims=True)
    yc = y - mu
    var = jnp.mean(yc * yc, axis=-1, keepdims=True)
    return yc * lax.rsqrt(var + LN_EPS) * g + b


def _bias_chain(d, tab_rows):
    out = tab_rows[0] + jnp.zeros(d.shape, F32)
    for b in range(1, N_BUCKETS):
        out = jnp.where(d >= _BUCKET_LOW[b], tab_rows[b], out)
    return out


def _ada_kernel(c_ref, w_ref, b_ref, o_ref):
    c = c_ref[...]
    h = (c * jax.nn.sigmoid(c)).astype(BF16)
    o_ref[...] = jnp.dot(h, w_ref[...].astype(BF16), preferred_element_type=F32) + b_ref[...]


def _ada(c_all, w_ada, b_ada):
    rows, n = c_all.shape[0], w_ada.shape[1]
    tn = 1152
    return pl.pallas_call(
        _ada_kernel,
        grid=(n // tn,),
        in_specs=[pl.BlockSpec((rows, D_MODEL), lambda j: (0, 0)),
                  pl.BlockSpec((D_MODEL, tn), lambda j: (0, j)),
                  pl.BlockSpec((1, tn), lambda j: (0, j))],
        out_specs=pl.BlockSpec((rows, tn), lambda j: (0, j)),
        out_shape=jax.ShapeDtypeStruct((rows, n), F32),
        compiler_params=_params(("arbitrary",)),
        name="ada",
    )(c_all, w_ada, b_ada.reshape(1, n))


FF_CHUNKS = 2


def _ffn_kernel(x_ref, sh_ref, sc_ref, gt_ref, lng_ref, lnb_ref, wg_ref, wu_ref, wd_ref, o_ref):
    x = x_ref[...]
    h = (x * (1.0 + sc_ref[...]) + sh_ref[...]).astype(BF16)
    ck = D_FF // FF_CHUNKS
    acc = jnp.zeros(x.shape, F32)
    for c in range(FF_CHUNKS):
        a = jnp.dot(h, wg_ref[:, c * ck:(c + 1) * ck], preferred_element_type=F32)
        b = jnp.dot(h, wu_ref[:, c * ck:(c + 1) * ck], preferred_element_type=F32)
        t = (a * jax.nn.sigmoid(a) * b).astype(BF16)
        acc = acc + jnp.dot(t, wd_ref[c * ck:(c + 1) * ck, :], preferred_element_type=F32)
    y = ALPHA * x + (1.0 + gt_ref[...]) * (0.5 * acc)
    o_ref[...] = _layer_norm(y, lng_ref[...], lnb_ref[...])


def _mod_spec(mod, tm):
    if mod.shape[0] == 1:
        return pl.BlockSpec((1, D_MODEL), lambda i: (0, 0))
    return pl.BlockSpec((tm, D_MODEL), lambda i: (i, 0))


def _ffn(x, shift, scale, gate, ln_g, ln_b, wg, wu, wd):
    rows = x.shape[0]
    tm = min(512, rows)
    row = lambda i: (i, 0)
    return pl.pallas_call(
        _ffn_kernel,
        grid=(rows // tm,),
        in_specs=[pl.BlockSpec((tm, D_MODEL), row), _mod_spec(shift, tm), _mod_spec(scale, tm), _mod_spec(gate, tm),
                  _resident((1, D_MODEL)), _resident((1, D_MODEL)),
                  _resident((D_MODEL, D_FF)), _resident((D_MODEL, D_FF)), _resident((D_FF, D_MODEL))],
        out_specs=pl.BlockSpec((tm, D_MODEL), row),
        out_shape=jax.ShapeDtypeStruct((rows, D_MODEL), F32),
        compiler_params=_params(("arbitrary",)),
        name="ffn",
    )(x, shift, scale, gate, ln_g.reshape(1, -1), ln_b.reshape(1, -1), wg, wu, wd)


def _proj_kernel(x_ref, sh_ref, sc_ref, w_ref, q_ref, kv_ref, win_ref, g_ref, pr_ref):
    h = (x_ref[...] * (1.0 + sc_ref[...]) + sh_ref[...]).astype(BF16)
    p = jnp.dot(h, w_ref[...], preferred_element_type=F32)
    q_ref[...] = p[:, P_Q:P_KV]
    kv_ref[...] = p[:, P_KV:P_WIN]
    win_ref[...] = p[:, P_WIN:P_GATE]
    g_ref[...] = jax.nn.sigmoid(p[:, P_GATE:P_RW])
    pr_ref[...] = p[:, P_RW:P_COLS]


def _proj(x, shift, scale, w_in_p):
    rows = x.shape[0]
    tm = min(512, rows)
    row = lambda i: (i, 0)
    widths = (512, 512, 256, LANE, RW_PAD)
    return pl.pallas_call(
        _proj_kernel,
        grid=(rows // tm,),
        in_specs=[pl.BlockSpec((tm, D_MODEL), row), _mod_spec(shift, tm), _mod_spec(scale, tm),
                  _resident((D_MODEL, P_COLS))],
        out_specs=[pl.BlockSpec((tm, w), row) for w in widths],
        out_shape=[jax.ShapeDtypeStruct((rows, w), F32) for w in widths],
        compiler_params=_params(("arbitrary",)),
        name="proj",
    )(x, shift, scale, w_in_p)


def _prep_w_in(w_in):
    pad = lambda a, n: jnp.pad(a, ((0, 0), (0, n - a.shape[1])))
    nsa, rw = w_in[:, :NSA_COLS], w_in[:, NSA_COLS:]
    gl = nsa[:, 1280:1304].reshape(D_MODEL, H_NSA, 3).transpose(0, 2, 1).reshape(D_MODEL, 3 * H_NSA)
    cols = [nsa[:, :1280], pad(gl, LANE), _rwkv_cols(rw)]
    return jnp.concatenate(cols, axis=1).astype(BF16)


def _rwkv_cols(a):
    pad = lambda t: jnp.pad(t, [(0, 0)] * (t.ndim - 1) + [(0, LANE - t.shape[-1])])
    n = 3 * D_RWKV
    return jnp.concatenate([a[..., :n], pad(a[..., n:n + 32]), pad(a[..., n + 32:n + 64]), pad(a[..., n + 64:n + 160])], axis=-1)


def _rwkv_uncols(a):
    n = 3 * D_RWKV
    return jnp.concatenate([a[..., :n], a[..., n:n + 32], a[..., n + LANE:n + LANE + 32], a[..., n + 2 * LANE:n + 2 * LANE + 96]], axis=-1)


RW_GROUP = 64
RW_TOK_PAD = 16
RW_PAIRS = H_RWKV // 2
RW_BLOCK = 64
RW_STEP = 256


def _lora(x, w_ref):
    w = w_ref[...]
    w_hi = w.astype(BF16)
    w_lo = (w - w_hi.astype(F32)).astype(BF16)
    return _dot_exact_rhs(x, w_hi) + jnp.dot(x.astype(BF16), w_lo, preferred_element_type=F32)


def _rwkv_kernel(pr_ref, sh0_ref, s0_ref, mu_ref, w0_ref, a0_ref, kk_ref, ka_ref, rk_ref, gw_ref, gb_ref,
                 w2_ref, a2_ref, g2_ref, bo_ref, lgrp_ref, ggrp_ref, o_ref, sout_ref,
                 prev_scr, s_scr, *, n_valid):
    tb = pr_ref.shape[0]
    step = pl.program_id(1)

    @pl.when(step == 0)
    def _():
        prev_scr[...] = sh0_ref[...]
        s_scr[...] = s0_ref[...]

    p = pr_ref[...]
    rows = lax.broadcasted_iota(jnp.int32, (tb, 1), 0)
    prev = jnp.where(rows == 0, prev_scr[...], pltpu.roll(p, 1, axis=0))
    prev_scr[...] = p[tb - 1:tb, :]
    xs = p + (prev - p) * mu_ref[...]
    n = D_RWKV
    r, k, v = xs[:, :n], xs[:, n:2 * n], xs[:, 2 * n:3 * n]
    wl, al, gl = xs[:, 3 * n:3 * n + LANE], xs[:, 3 * n + LANE:3 * n + 2 * LANE], xs[:, 3 * n + 2 * LANE:]
    z = -(w0_ref[...] + _lora(jnp.tanh(wl), w2_ref))
    w = -(jnp.maximum(z, 0.0) + jnp.log(1.0 + jnp.exp(-jnp.abs(z)))) - 0.5
    a = jax.nn.sigmoid(a0_ref[...] + _lora(al, a2_ref))
    g = _lora(jax.nn.sigmoid(gl), g2_ref)
    kk = k * kk_ref[...]
    ss = _dot_exact_rhs(kk * kk, bo_ref[...])
    kk = kk / jnp.maximum(jnp.sqrt(ss), 1e-12)
    k2 = k * (1.0 + (a - 1.0) * ka_ref[...])
    G = min(RW_GROUP, tb)
    log_dec = -jnp.exp(w)
    bet = kk * a
    if n_valid < tb:
        live = rows < n_valid
        log_dec, kk, bet, k2, v_in = (jnp.where(live, x, 0.0) for x in (log_dec, kk, bet, k2, v))
    else:
        v_in = v
    cum = _dot_exact_lhs(lgrp_ref[...], log_dec, terms=2)
    cum_end = _dot_exact_lhs(ggrp_ref[...], log_dec, terms=2)
    gam_inv = jnp.exp(-cum)
    gam_end = jnp.exp(cum_end - cum)
    k_hat = -kk * jnp.exp(cum - log_dec)
    r_hat = r * jnp.exp(cum)
    b_chk, k_chk = bet * gam_inv, k2 * gam_inv
    b_til, k_til = bet * gam_end, k2 * gam_end
    gam_group = jnp.exp(cum_end)

    lane = lax.broadcasted_iota(jnp.int32, (1, LANE), 1)
    low = lane < HD_RWKV
    lane_t = lane & (RW_BLOCK - 1)
    row = lax.broadcasted_iota(jnp.int32, (LANE, 1), 0)
    row_t = row & (RW_BLOCK - 1)
    same = ((row < HD_RWKV) == low) & ((row_t // G) == (lane_t // G))
    strict, incl = same & (lane_t < row_t), same & (lane_t <= row_t)
    bf = lambda x: x.astype(BF16)
    mm = lambda x, y: jnp.dot(bf(x), bf(y), preferred_element_type=F32)
    mm_nt = lambda x, y: lax.dot_general(bf(x), bf(y), (((1,), (1,)), ((), ())), preferred_element_type=F32)

    def rows_bd(x):
        if tb < RW_BLOCK:
            x = jnp.concatenate([x, jnp.zeros((RW_BLOCK - tb, LANE), F32)], axis=0)
        return jnp.concatenate([jnp.where(low, x, 0.0), jnp.where(low, 0.0, x)], axis=0)

    def mm3(x, y):
        xh, yh = bf(x), bf(y)
        xl, yl = bf(x - xh.astype(F32)), bf(y - yh.astype(F32))
        return jnp.dot(jnp.concatenate([xh, xl, xh], axis=1), jnp.concatenate([yh, yh, yl], axis=0), preferred_element_type=F32)

    units = max(1, tb // RW_BLOCK)
    unit_rows = min(tb, RW_BLOCK)
    pairs = range(RW_PAIRS)
    items = [(un, pp) for un in range(units) for pp in pairs]
    at = lambda x, it: x[it[0] * unit_rows:(it[0] + 1) * unit_rows, it[1] * LANE:(it[1] + 1) * LANE]
    kh_row = [rows_bd(at(k_hat, it)) for it in items]
    rh_row = [rows_bd(at(r_hat, it)) for it in items]
    kh_mat = [x.T for x in kh_row]
    rh_mat = [x.T for x in rh_row]
    state_in = [jnp.concatenate([rows_bd(at(b_chk, it)).T, rows_bd(at(k_chk, it)).T], axis=1) for it in items]
    upd_rows = [jnp.concatenate([rows_bd(at(b_til, it)), rows_bd(at(k_til, it))], axis=0) for it in items]
    v_t = [rows_bd(at(v_in, it)).T for it in items]
    v_t = [x[:HD_RWKV] + x[HD_RWKV:] for x in v_t]
    c_all = [mm(jnp.concatenate([kh_row[n], rh_row[n]], axis=0), state_in[n]) for n in range(len(items))]
    c_uu = [jnp.where(strict, c[:LANE, :LANE], 0.0) for c in c_all]
    c_uv = [jnp.where(strict, c[:LANE, LANE:], 0.0) for c in c_all]
    c_ru = [jnp.where(incl, c[LANE:, :LANE], 0.0) for c in c_all]
    c_rv = [jnp.where(incl, c[LANE:, LANE:], 0.0) for c in c_all]
    t_neu, power = list(c_uu), list(c_uu)
    span = 2
    while span < G:
        power = [mm(x, x) for x in power]
        t_neu = [t_neu[n] + power[n] + mm(t_neu[n], power[n]) for n in range(len(items))]
        span *= 2
    from_v = [mm_nt(v_t[n], c_uv[n]) for n in range(len(items))]
    st = [s_scr[pp] for pp in pairs]
    y_units = []
    for un in range(units):
        ns = [un * RW_PAIRS + pp for pp in pairs]
        y_t = [jnp.zeros((HD_RWKV, LANE), F32) for _ in pairs]
        for grp in range(unit_rows // G):
            here = (lane_t // G) == grp
            first = un * unit_rows + grp * G
            w_t = [jnp.where(here, mm(st[pp], kh_mat[ns[pp]]) + from_v[ns[pp]], 0.0) for pp in pairs]
            u_t = [w_t[pp] + mm_nt(w_t[pp], t_neu[ns[pp]]) for pp in pairs]
            v_g = [jnp.where(here, v_t[ns[pp]], 0.0) for pp in pairs]
            y_t = [y_t[pp] + jnp.where(here, mm(st[pp], rh_mat[ns[pp]]), 0.0) + mm_nt(u_t[pp], c_ru[ns[pp]])
                   + mm_nt(v_g[pp], c_rv[ns[pp]]) for pp in pairs]
            st = [st[pp] * gam_group[first:first + 1, pp * LANE:(pp + 1) * LANE]
                  + mm3(jnp.concatenate([u_t[pp], v_g[pp]], axis=1), upd_rows[ns[pp]]) for pp in pairs]
        lane_u = lax.broadcasted_iota(jnp.int32, (unit_rows, LANE), 1)
        pieces = []
        for pp in pairs:
            yt = jnp.concatenate([y_t[pp], jnp.zeros((LANE - HD_RWKV, LANE), F32)], axis=0).T
            pieces.append(jnp.where(lane_u < HD_RWKV, yt[:unit_rows], pltpu.roll(yt[RW_BLOCK:RW_BLOCK + unit_rows], HD_RWKV, axis=1)))
        y_units.append(jnp.concatenate(pieces, axis=1))
    for pp in pairs:
        s_scr[pp] = st[pp]
    y = y_units[0] if units == 1 else jnp.concatenate(y_units, axis=0)
    mean = _dot_exact_rhs(y, bo_ref[...]) * (1.0 / HD_RWKV)
    yc = y - mean
    var = _dot_exact_rhs(yc * yc, bo_ref[...]) * (1.0 / HD_RWKV)
    yn = yc * lax.rsqrt(var + GN_EPS) * gw_ref[...] + gb_ref[...]
    bonus = _dot_exact_rhs(r * k2 * rk_ref[...], bo_ref[...]) * v
    o_ref[...] = (yn + bonus) * g
    sout_ref[...] = s_scr[...]


def _pair_state(s):
    B = s.shape[0]
    return s.reshape(B, RW_PAIRS, 2, HD_RWKV, HD_RWKV).transpose(0, 1, 3, 2, 4).reshape(B, RW_PAIRS, HD_RWKV, LANE)


def _unpair_state(s):
    B = s.shape[0]
    return s.reshape(B, RW_PAIRS, HD_RWKV, 2, HD_RWKV).transpose(0, 1, 3, 2, 4).reshape(B, H_RWKV, HD_RWKV, HD_RWKV)


def _rwkv(pr, shift0, s0, lw, n_valid):
    B, T, _ = pr.shape
    tb = min(RW_STEP, T)
    n = D_RWKV
    vec = lambda a: a.reshape(1, n)
    padrow = lambda a: jnp.pad(a, ((0, LANE - a.shape[0]), (0, 0)))
    blk = np.arange(n) // HD_RWKV
    block_ones = jnp.asarray(blk[:, None] == blk[None, :], BF16)
    tok = np.arange(tb)
    group = min(RW_GROUP, tb)
    same_group = tok[:, None] // group == tok[None, :] // group
    prefix = jnp.asarray(same_group & (tok[None, :] <= tok[:, None]), BF16)
    consts = [_rwkv_cols(lw['rwkv_mu']).reshape(1, RW_PAD), vec(lw['rwkv_w0']), vec(lw['rwkv_a0']), vec(lw['rwkv_k_k']),
              vec(lw['rwkv_k_a']), vec(lw['rwkv_r_k']), vec(lw['rwkv_gn_w']), vec(lw['rwkv_gn_b']),
              padrow(lw['rwkv_w2']), padrow(lw['rwkv_a2']), padrow(lw['rwkv_g2']), block_ones, prefix, jnp.asarray(same_group, BF16)]
    kern = functools.partial(_rwkv_kernel, n_valid=n_valid)
    state_spec = pl.BlockSpec((None, RW_PAIRS, HD_RWKV, LANE), lambda b, j: (b, 0, 0, 0))
    o, s = pl.pallas_call(
        kern,
        grid=(B, T // tb),
        in_specs=[pl.BlockSpec((None, tb, RW_PAD), lambda b, j: (b, j, 0)),
                  pl.BlockSpec((None, 1, RW_PAD), lambda b, j: (b, 0, 0)), state_spec]
                 + [_resident(c.shape) for c in consts],
        out_specs=[pl.BlockSpec((None, tb, n), lambda b, j: (b, j, 0)), state_spec],
        out_shape=[jax.ShapeDtypeStruct((B, T, n), F32), jax.ShapeDtypeStruct((B, RW_PAIRS, HD_RWKV, LANE), F32)],
        scratch_shapes=[pltpu.VMEM((1, RW_PAD), F32), pltpu.VMEM((RW_PAIRS, HD_RWKV, LANE), F32)],
        compiler_params=_params(("arbitrary", "arbitrary")),
        name="rwkv",
    )(pr, shift0, _pair_state(s0), *consts)
    return o, _unpair_state(s)


CMP_PAGES = 64
CHUNKS_PER_PAGE = PAGE_SIZE // CMP_STRIDE


def _compress_kernel(pt_ref, *refs, n_pages, transposed):
    pages, nxt = refs[:n_pages], refs[n_pages]
    weights = refs[n_pages + 1:n_pages + 9]
    outs = refs[n_pages + 9:n_pages + 11]
    width = N_KV * HD
    rows = CHUNKS_PER_PAGE * n_pages
    seg = rows + 8
    kinds = range(2)
    low = lax.broadcasted_iota(jnp.int32, (1, N_KV * HD), 1) < HD
    rows_scr = refs[n_pages + 11:n_pages + 13]

    def by_head(row_s):
        heads = [[], []]
        for s in range(0, CMP_STRIDE, 2):
            a, b = row_s(s), row_s(s + 1)
            heads[0].append(jnp.where(low, a, pltpu.roll(b, HD, axis=1)))
            heads[1].append(jnp.where(low, pltpu.roll(a, HD, axis=1), b))
        return [jnp.concatenate(h, axis=1) for h in heads]

    for kind in kinds:
        pe_ref, w_ref, b_ref, w2_ref = weights[4 * kind:4 * kind + 4]
        part = slice(kind * width, (kind + 1) * width)
        for p, pg in enumerate(pages):
            rows_scr[kind][p * PAGE_SIZE:(p + 1) * PAGE_SIZE, :] = pg[part, :].T if transposed else pg[:, part]
        nxt_rows = nxt[part, :].T[:CMP_STRIDE] if transposed else nxt[:, part]
        x = by_head(lambda s: rows_scr[kind][pl.ds(s, rows, stride=CMP_STRIDE), :])
        x_next = by_head(lambda s: jnp.broadcast_to(nxt_rows[s:s + 1, :], (8, N_KV * HD)))
        x_all = jnp.concatenate([x[0], x_next[0], x[1], x_next[1]], axis=0)
        h_first = jnp.dot((x_all + pe_ref[0]).astype(BF16), w_ref[0], preferred_element_type=F32)
        h_second = jnp.dot((x_all + pe_ref[1]).astype(BF16), w_ref[1], preferred_element_type=F32)
        out = None
        for h in range(N_KV):
            h_next = pltpu.roll(h_second[h * seg:(h + 1) * seg], seg - 1, axis=0)[:rows]
            hidden = jax.nn.gelu(h_first[h * seg:h * seg + rows] + h_next + b_ref[...])
            part = jnp.dot(hidden.astype(BF16), w2_ref[h], preferred_element_type=F32)
            out = part if out is None else out + part
        outs[kind][...] = out


def _compress_weights(pe, w1, b1, w2):
    n = CMP_STRIDE * HD
    pe2 = pe.reshape(2, 1, n)
    w_halves = w1.reshape(2, n, CMP_HIDDEN).astype(BF16)
    zero = jnp.zeros_like(w2)
    w2_heads = jnp.stack([jnp.concatenate([w2, zero], axis=1), jnp.concatenate([zero, w2], axis=1)]).astype(BF16)
    return [pe2, w_halves, b1.reshape(1, -1), w2_heads]


def _compress(pool, page_table, lw, transposed):
    B, n_pages_total = page_table.shape
    n_pages = min(CMP_PAGES, n_pages_total)
    rows = CHUNKS_PER_PAGE * n_pages
    weights = (_compress_weights(lw['cmp_pe_k'], lw['cmp_w1_k'], lw['cmp_b1_k'], lw['cmp_w2_k'])
               + _compress_weights(lw['cmp_pe_v'], lw['cmp_w1_v'], lw['cmp_b1_v'], lw['cmp_w2_v']))
    width = N_KV * HD

    def page_map(p):
        return lambda b, j, pt: (pt[b, j * n_pages + p], 0, 0)

    next_map = lambda b, j, pt: (pt[b, jnp.minimum((j + 1) * n_pages, n_pages_total - 1)], 0, 0)
    page_block = (None, 2 * width, PAGE_SIZE) if transposed else (None, PAGE_SIZE, 2 * width)
    next_block = page_block if transposed else (None, CMP_STRIDE, 2 * width)
    const = lambda a: pl.BlockSpec(a.shape, lambda b, j, pt: (0,) * a.ndim)
    out_spec = pl.BlockSpec((None, rows, N_KV * HD), lambda b, j, pt: (b, j, 0))
    out_shape = jax.ShapeDtypeStruct((B, n_pages_total * CHUNKS_PER_PAGE, N_KV * HD), F32)
    grid_spec = pltpu.PrefetchScalarGridSpec(
        num_scalar_prefetch=1,
        grid=(B, n_pages_total // n_pages),
        in_specs=[pl.BlockSpec(page_block, page_map(p)) for p in range(n_pages)]
                 + [pl.BlockSpec(next_block, next_map)] + [const(a) for a in weights],
        out_specs=[out_spec, out_spec],
        scratch_shapes=[pltpu.VMEM((n_pages * PAGE_SIZE, width), F32)] * 2,
    )
    return pl.pallas_call(
        functools.partial(_compress_kernel, n_pages=n_pages, transposed=transposed),
        grid_spec=grid_spec,
        out_shape=[out_shape, out_shape],
        compiler_params=_params(("arbitrary", "arbitrary")),
        name="compress",
    )(page_table, *([pool] * (n_pages + 1)), *weights)


BAND_ROWS = 1152


def _band_kernel(tab_ref, bkt_ref, o_ref):
    h = pl.program_id(0)
    bkt = bkt_ref[...]
    out = jnp.full(bkt.shape, NEG, F32)
    for b in range(N_BUCKETS):
        out = jnp.where(bkt == b, tab_ref[b, h], out)
    o_ref[...] = out


def _band(rel_table):
    u = np.arange(BAND_ROWS)[:, None]
    qi = np.arange(Q_BLOCK)[None, :]
    d = qi + WINDOW - u
    bkt = np.where(d >= 0, _BUCKET_OF[np.clip(d, 0, len(_BUCKET_OF) - 1)], -1).astype(np.int32)
    return pl.pallas_call(
        _band_kernel,
        grid=(H_NSA,),
        in_specs=[pl.BlockSpec(memory_space=pltpu.SMEM), pl.BlockSpec((BAND_ROWS, Q_BLOCK), lambda h: (0, 0))],
        out_specs=pl.BlockSpec((None, BAND_ROWS, Q_BLOCK), lambda h: (h, 0, 0)),
        out_shape=jax.ShapeDtypeStruct((H_NSA, BAND_ROWS, Q_BLOCK), F32),
        compiler_params=_params(("arbitrary",)),
        name="band",
    )(rel_table, jnp.asarray(bkt))


def _softmax_cols(s):
    m = jnp.max(s, axis=0, keepdims=True)
    e = jnp.exp(s - m)
    l = jnp.sum(e, axis=0, keepdims=True)
    return e * jnp.where(m > 0.5 * NEG, 1.0 / l, 0.0)


def _select_blocks(impsel, qpos, n_pick):
    ns = impsel.shape[0]
    blk = lax.broadcasted_iota(jnp.int32, impsel.shape, 0)
    cur = jnp.right_shift(qpos, 6)
    future = blk * SEL_BLOCK > qpos
    forced = (blk == 0) | (blk == cur) | (blk == cur - 1)
    score = jnp.where(future, -jnp.inf, jnp.where(forced, jnp.inf, impsel))
    chosen = jnp.zeros(impsel.shape, F32)
    for _ in range(n_pick):
        best = jnp.max(score, axis=0, keepdims=True)
        first = jnp.min(jnp.where(score == best, blk, ns), axis=0, keepdims=True)
        hit = (blk == first) & (best > -jnp.inf)
        chosen = jnp.where(hit, 1.0, chosen)
        score = jnp.where(hit, -jnp.inf, score)
    return jnp.where(chosen > 0.0, 0.0, NEG)


def _pool_matrix(ns, nc):
    j = np.arange(ns)[:, None]
    n = np.arange(nc)[None, :]
    ratio = SEL_BLOCK // CMP_STRIDE
    return jnp.asarray((n >= ratio * j - 1) & (n <= ratio * j + ratio - 1), BF16)


def _nsa_prompt_kernel(tab_ref, q_ref, g_ref, kc_ref, vct_ref, ks_ref, vst_ref, kw_ref, vwt_ref, band_ref, pool_ref, o_ref,
                       rhs_scr, mask_scr, acc_scr, m_scr, sc_scr, sa_scr, sb_scr, oc_scr):
    i = pl.program_id(0)
    ncp = kc_ref.shape[0]
    ns = pool_ref.shape[0]
    s0 = i * Q_BLOCK
    q_t = (q_ref[...] * HD ** -0.5).T
    g_t = g_ref[...].T
    lane_q = lax.broadcasted_iota(jnp.int32, (1, G_NSA * Q_BLOCK), 1) & (Q_BLOCK - 1)
    qpos = s0 + lax.broadcasted_iota(jnp.int32, (1, Q_BLOCK), 1)
    rhs_scr[...] = jnp.zeros(rhs_scr.shape, BF16)
    kvs = range(N_KV)
    kd = i // 4
    r = i % 4
    lanes4 = lambda k, f: jnp.concatenate([f(G_NSA * k + g) for g in range(G_NSA)], axis=1)
    qcols = [lanes4(k, lambda h: q_t[h * HD:(h + 1) * HD, :]) for k in kvs]
    zero = jnp.zeros_like(qcols[0])
    top = [jnp.concatenate([qcols[0], zero], axis=0).astype(BF16), jnp.concatenate([zero, qcols[1]], axis=0).astype(BF16)]
    far_row = [lanes4(k, lambda h: band_ref[h, 0:1, :]) for k in kvs]


    n0 = pl.multiple_of(jnp.clip(8 * i - 16, 0, ncp - 32), 8)
    nrow = lax.broadcasted_iota(jnp.int32, (ncp, 1), 0)
    d_edge = qpos - (CMP_STRIDE * (n0 + lax.broadcasted_iota(jnp.int32, (32, 1), 0)) + CMP_BLOCK - 1)

    def compressed(k, rows):
        sc_scr[k, 0:rows, :] = (jnp.dot(kc_ref[0:rows, :], top[k], preferred_element_type=F32)
                                + jnp.where(nrow[0:rows] < n0, far_row[k], NEG))
        edge_bias = lanes4(k, lambda h: jnp.where(d_edge >= 0, _bias_chain(d_edge, [tab_ref[b, h] for b in range(N_BUCKETS)]), NEG))
        sc_scr[k, pl.ds(n0, 32), :] = jnp.dot(kc_ref[pl.ds(n0, 32), :], top[k], preferred_element_type=F32) + edge_bias
        p_c = _softmax_cols(sc_scr[k, 0:rows, :])
        o_c = jnp.dot(vct_ref[:, 0:rows], p_c.astype(BF16), preferred_element_type=F32)[k * HD:(k + 1) * HD]
        imp = p_c[:, 0:Q_BLOCK]
        for g in range(1, G_NSA):
            imp = imp + p_c[:, g * Q_BLOCK:(g + 1) * Q_BLOCK]
        n_blk = rows * CMP_STRIDE // SEL_BLOCK
        return o_c, _dot_exact_lhs(pool_ref[0:n_blk, 0:rows], imp, terms=2)

    size_step = min(CMP_CLASS_ROWS, ncp)
    size_class = (n0 + 32 - 1) // size_step
    for cls in range(ncp // size_step):
        @pl.when(size_class == cls)
        def _(rows=(cls + 1) * size_step):
            comp = [compressed(k, rows) for k in kvs]
            n_blk = comp[0][1].shape[0]
            masks = [_select_blocks(comp[k][1], qpos, min(N_SEL, ns)) for k in kvs]
            for k in kvs:
                oc_scr[k] = comp[k][0]
                full = jnp.concatenate([masks[k], jnp.full((ns - n_blk, Q_BLOCK), NEG, F32)], axis=0) if n_blk < ns else masks[k]
                mask_scr[k] = jnp.concatenate([full.astype(BF16)] * G_NSA, axis=1)

    ws = pl.multiple_of(jnp.maximum(s0 - WINDOW, 0), Q_BLOCK)
    u0 = pl.multiple_of(WINDOW - (s0 - ws), Q_BLOCK)
    n_win = WINDOW + Q_BLOCK
    u = u0 + lax.broadcasted_iota(jnp.int32, (n_win, 1), 0)
    win_mask = jnp.where(u > lane_q, 0.0, NEG)

    def window(k):
        s_w = (jnp.dot(kw_ref[pl.ds(ws, n_win), :], top[k], preferred_element_type=F32)
               + lanes4(k, lambda h: band_ref[h, pl.ds(u0, n_win), :]) + win_mask)
        m_w = jnp.max(s_w, axis=0, keepdims=True)
        p_w = jnp.exp(s_w - m_w).astype(BF16)
        acc_w = jnp.zeros((V_ROWS, G_NSA * Q_BLOCK), F32)
        for j in range(n_win // Q_BLOCK):
            acc_w = acc_w + jnp.dot(vwt_ref[ws // Q_BLOCK + j], p_w[j * Q_BLOCK:(j + 1) * Q_BLOCK], preferred_element_type=F32)
        return acc_w[k * HD:(k + 1) * HD] / acc_w[N_KV * HD:N_KV * HD + 1]

    o_w = [window(k) for k in kvs]

    for k in kvs:
        m_scr[k] = jnp.full(m_scr.shape[1:], M_INIT, F32)
        acc_scr[k] = jnp.zeros(acc_scr.shape[1:], F32)
        far_hi = far_row[k].astype(BF16).astype(F32)
        rhs_scr[k, 0:HD, :] = qcols[k].astype(BF16)
        rhs_scr[k, SEL_FAR0:SEL_FAR0 + MASK_ROWS, :] = jnp.concatenate(
            [far_hi, far_row[k] - far_hi, jnp.zeros((MASK_ROWS - 2, G_NSA * Q_BLOCK), F32)], axis=0).astype(BF16)

    def scores(k, slab, kts, extra):
        rhs_scr[k, SEL_MASK0:SEL_MASK0 + MASK_ROWS, :] = mask_scr[k, pl.ds(pl.multiple_of(slab * MASK_ROWS, MASK_ROWS), MASK_ROWS), :]
        rhs = rhs_scr[k]
        out = []
        for kt, add in zip(kts, extra):
            s = jnp.dot(ks_ref[k, pl.ds(pl.multiple_of(kt * KEY_TILE, KEY_TILE), KEY_TILE), :], rhs, preferred_element_type=F32)
            out.append(s if add is None else s + add)
        return out

    def update(k, kts, tiles):
        m_old = m_scr[k]
        m_new = m_old
        for s in tiles:
            m_new = jnp.maximum(m_new, jnp.max(s, axis=0, keepdims=True))
        acc = jnp.exp(m_old - m_new) * acc_scr[k]
        for kt, s in zip(kts, tiles):
            acc = acc + jnp.dot(vst_ref[k, kt], jnp.exp(s - m_new).astype(BF16), preferred_element_type=F32)
        acc_scr[k] = acc
        m_scr[k] = m_new

    def attend(slab, kts, extra):
        tiles = [scores(k, slab, kts, extra(k)) for k in kvs]
        for k in kvs:
            update(k, kts, tiles[k])

    near_at = lambda k, start: lanes4(k, lambda h: band_ref[h, pl.ds(pl.multiple_of(start, Q_BLOCK), KEY_TILE), :]) - far_row[k]
    prev_near = (r == 0) & (kd >= 1)
    kd_odd = (kd & 1) == 1
    even_prev = jnp.logical_not(kd_odd) & prev_near
    n_pairs = kd // 2 - even_prev.astype(jnp.int32)
    n_quads = n_pairs // 2
    no_bias = lambda k: [None, None]

    def pair_scores(k, dst, pair):
        lo, hi = scores(k, pair, [2 * pair, 2 * pair + 1], [None, None])
        dst[k, 0:KEY_TILE, :] = lo
        dst[k, KEY_TILE:, :] = hi

    def pair_update(k, src, pair):
        update(k, [2 * pair, 2 * pair + 1], [src[k, 0:KEY_TILE, :], src[k, KEY_TILE:, :]])

    @pl.when(n_quads > 0)
    def _():
        for k in kvs:
            pair_scores(k, sa_scr, 0)

    for k in kvs:
        def quad_body(qd, carry, k=k):
            first = 2 * qd
            pair_scores(k, sb_scr, first + 1)
            pair_update(k, sa_scr, first)
            pair_scores(k, sa_scr, jnp.minimum(first + 2, 2 * n_quads - 2))
            pair_update(k, sb_scr, first + 1)
            return carry

        lax.fori_loop(0, n_quads, quad_body, 0)

    @pl.when((n_pairs & 1) == 1)
    def _():
        attend(n_pairs - 1, [2 * n_pairs - 2, 2 * n_pairs - 1], no_bias)

    @pl.when(kd_odd)
    def _():
        attend(kd // 2, [kd - 1, kd], lambda k: [jnp.where(prev_near, near_at(k, 0), 0.0), near_at(k, KEY_TILE - Q_BLOCK * r)])

    @pl.when(even_prev)
    def _():
        attend(kd // 2 - 1, [kd - 2, kd - 1], lambda k: [None, near_at(k, 0)])

    @pl.when(jnp.logical_not(kd_odd))
    def _():
        attend(kd // 2, [kd], lambda k: [near_at(k, KEY_TILE - Q_BLOCK * r)])

    heads_out = []
    for k in kvs:
        acc = acc_scr[k]
        o_s = acc[0:HD] / acc[HD:HD + 1]
        o_c = oc_scr[k]
        for g in range(G_NSA):
            h = G_NSA * k + g
            cols = slice(g * Q_BLOCK, (g + 1) * Q_BLOCK)
            heads_out.append(o_c[:, cols] * g_t[h:h + 1] + o_s[:, cols] * g_t[H_NSA + h:H_NSA + h + 1]
                             + o_w[k][:, cols] * g_t[2 * H_NSA + h:2 * H_NSA + h + 1])
    o_ref[...] = jnp.concatenate(heads_out, axis=0).T


def _sel_pattern(rows, width):
    key = np.arange(rows)[:, None]
    b = np.arange(width)[None, :]
    ones = (b >= MASK_ROWS) & (b < MASK_ROWS + 2)
    return jnp.asarray(((key // SEL_BLOCK) % MASK_ROWS == b) | ones, BF16)


def _values_t(v, tile):
    T, n = v.shape
    rows = n + MASK_ROWS
    vt = jnp.concatenate([v.T, jnp.ones((1, T), F32), jnp.zeros((rows - n - 1, T), F32)], axis=0)
    return vt.reshape(rows, T // tile, tile).transpose(1, 0, 2).astype(BF16)


def _nsa_prompt(q, gates, kv, win, kc, vc, band, rel_table):
    T = q.shape[0]
    ncp, ns = kc.shape[0], T // SEL_BLOCK
    width = G_NSA * Q_BLOCK
    k_sel = lambda h: kv[:, 256 + h * HD:256 + (h + 1) * HD]
    v_sel = lambda h: kv[:, 384 + h * HD:384 + (h + 1) * HD]
    pattern = _sel_pattern(T, LANE - HD)
    ks_aug = jnp.stack([jnp.concatenate([k_sel(h).astype(BF16), pattern], axis=1) for h in range(N_KV)])
    operands = [q, gates, kc.astype(BF16), vc.T.astype(BF16), ks_aug, jnp.stack([_values_t(v_sel(h), KEY_TILE) for h in range(N_KV)]),
                win[:, 0:128].astype(BF16), _values_t(win[:, 128:256], Q_BLOCK), band, _pool_matrix(ns, ncp)]
    blk = lambda w: pl.BlockSpec((Q_BLOCK, w), lambda i: (i, 0))
    return pl.pallas_call(
        _nsa_prompt_kernel,
        grid=(T // Q_BLOCK,),
        in_specs=[pl.BlockSpec(memory_space=pltpu.SMEM), blk(H_NSA * HD), blk(LANE)] + [_resident(a.shape) for a in operands[2:]],
        out_specs=blk(H_NSA * HD),
        out_shape=jax.ShapeDtypeStruct((T, H_NSA * HD), F32),
        scratch_shapes=[pltpu.VMEM((N_KV, LANE, width), BF16), pltpu.VMEM((N_KV, ns, width), BF16),
                        pltpu.VMEM((N_KV, V_ROWS_KV, width), F32), pltpu.VMEM((N_KV, 1, width), F32),
                        pltpu.VMEM((N_KV, ncp, width), F32)] + [pltpu.VMEM((N_KV, 2 * KEY_TILE, width), F32)] * 2
                       + [pltpu.VMEM((N_KV, HD, width), F32)],
        compiler_params=_params(("arbitrary",)),
        name="nsa_prompt",
    )(rel_table, *operands)


SLAB_PAGES = 8
SMP_PAGES = 32
TOK_PAD = 8
SMP_COLS = H_NSA * TOK_PAD


def _nsa_sample_kernel(pt_ref, *refs, n_pages, n_valid, past):
    pages = refs[:n_pages]
    (q_ref, g_ref, kc_ref, vc_ref, kvn_ref, win_ref, winn_ref, tab_ref, pool_ref, gsum_ref, epat_ref, o_ref,
     mask_scr, acc_scr, m_scr, l_scr, oc_scr, ow_scr, qrow_scr, maskt_scr, farc_scr, nearbt_scr, m2_scr, l2_scr, acc2_scr) = refs[n_pages:]
    j = pl.program_id(1)
    ncp, wbuf = kc_ref.shape[0], win_ref.shape[1]
    lane = lax.broadcasted_iota(jnp.int32, (1, LANE), 1)
    tok = lane & (TOK_PAD - 1)
    second_kv = lane >= G_NSA * TOK_PAD
    tab = [tab_ref[b:b + 1, :] for b in range(N_BUCKETS)]
    far_row = tab[N_BUCKETS - 1]
    own_rows = lambda x: jnp.where(second_kv, x[HD:2 * HD], x[0:HD])
    pad_rows = lambda x: jnp.concatenate([x, jnp.zeros((LANE - x.shape[0], x.shape[1]), x.dtype)], axis=0)
    trow = lax.broadcasted_iota(jnp.int32, (LANE, 1), 0)
    d_new = tok - trow
    new_bias = jnp.where((d_new >= 0) & (trow < n_valid), _bias_chain(jnp.maximum(d_new, 0), tab), NEG)

    def attend_update(s, values_t):
        m_old = m_scr[...]
        m_new = jnp.maximum(m_old, jnp.max(s, axis=0, keepdims=True))
        alpha = jnp.exp(m_old - m_new)
        p = jnp.exp(s - m_new)
        l_scr[...] = alpha * l_scr[...] + jnp.sum(p, axis=0, keepdims=True)
        acc_scr[...] = alpha * acc_scr[...] + jnp.dot(values_t.astype(BF16), p.astype(BF16), preferred_element_type=F32)
        m_scr[...] = m_new

    def reset():
        m_scr[...] = jnp.full(m_scr.shape, M_INIT, F32)
        l_scr[...] = jnp.zeros(l_scr.shape, F32)
        acc_scr[...] = jnp.zeros(acc_scr.shape, F32)

    @pl.when(j == 0)
    def _():
        q_t = pad_rows(q_ref[...] * HD ** -0.5).T
        halves = []
        for k in range(N_KV):
            part = jnp.zeros((HD, LANE), F32)
            for g in range(G_NSA):
                h = G_NSA * k + g
                piece = q_t[h * HD:(h + 1) * HD, :]
                part = part + (pltpu.roll(piece, TOK_PAD * h, axis=1) if h else piece)
            halves.append(part)
        top_f = jnp.concatenate(halves, axis=0)
        top = top_f.astype(BF16)
        qrow_scr[...] = top_f.T[:SMP_COLS].astype(BF16)
        qpos = past + tok

        n0 = ncp - 32
        kcb = kc_ref[...].astype(BF16)
        d_edge = qpos - (CMP_STRIDE * (n0 + lax.broadcasted_iota(jnp.int32, (32, 1), 0)) + CMP_BLOCK - 1)
        s_c = jnp.concatenate([
            jnp.dot(kcb[:n0], top, preferred_element_type=F32) + far_row,
            jnp.dot(kcb[n0:], top, preferred_element_type=F32) + jnp.where(d_edge >= 0, _bias_chain(jnp.maximum(d_edge, 0), tab), NEG)], axis=0)
        p_c = _softmax_cols(s_c)
        oc_scr[...] = own_rows(jnp.dot(vc_ref[...].T.astype(BF16), p_c.astype(BF16), preferred_element_type=F32))
        imp = _dot_exact_rhs(p_c, gsum_ref[...], terms=3)
        mask_scr[...] = _select_blocks(_dot_exact_lhs(pool_ref[...], imp), qpos, N_SEL)

        wk = win_ref[0:LANE, :].T
        d_w = wbuf + tok - lax.broadcasted_iota(jnp.int32, (wbuf, 1), 0)
        near = wbuf - LANE
        s_w = jnp.dot(wk.astype(BF16), top, preferred_element_type=F32)
        s_w = (jnp.concatenate([s_w[:near] + far_row, s_w[near:] + _bias_chain(d_w[near:], tab)], axis=0)
               + jnp.where(d_w < WINDOW, 0.0, NEG))
        reset()
        attend_update(s_w, win_ref[LANE:, :])
        wn = pad_rows(winn_ref[...])
        attend_update(jnp.dot(wn[:, :LANE].astype(BF16), top, preferred_element_type=F32) + new_bias, wn[:, LANE:].T)
        ow_scr[...] = own_rows(acc_scr[...]) / l_scr[...]
        m2_scr[...] = jnp.full(m2_scr.shape, M_INIT, F32)
        l2_scr[...] = jnp.zeros(l2_scr.shape, F32)
        acc2_scr[...] = jnp.zeros(acc2_scr.shape, F32)
        for sl in range(maskt_scr.shape[0]):
            maskt_scr[sl] = pad_rows(mask_scr[sl * MASK_ROWS:(sl + 1) * MASK_ROWS, :]).T[:SMP_COLS].astype(BF16)
        farc_scr[...] = pad_rows(tab_ref[...]).T[:SMP_COLS, N_BUCKETS - 1:N_BUCKETS]
        nearbt_scr[...] = _bias_chain(LANE + tok - lax.broadcasted_iota(jnp.int32, (LANE, 1), 0), tab).T[:SMP_COLS]

    def rows_update(s, values_t):
        m_old = m2_scr[...]
        m_new = jnp.maximum(m_old, jnp.max(s, axis=1, keepdims=True))
        alpha = jnp.exp(m_old - m_new)
        p = jnp.exp(s - m_new)
        l2_scr[...] = alpha * l2_scr[...] + jnp.sum(p, axis=1, keepdims=True)
        acc2_scr[...] = alpha * acc2_scr[...] + lax.dot_general(p.astype(BF16), values_t.astype(BF16), (((1,), (1,)), ((), ())),
                                                                preferred_element_type=F32)
        m2_scr[...] = m_new

    n_slabs = n_pages // SLAB_PAGES
    slab_keys = SLAB_PAGES * PAGE_SIZE
    far_col = farc_scr[...]
    for sub in range(n_slabs):
        tile_pages = pages[sub * SLAB_PAGES:(sub + 1) * SLAB_PAGES]
        kt_tile = jnp.concatenate([pg[0:LANE, :] for pg in tile_pages], axis=1)
        vt_tile = jnp.concatenate([pg[LANE:, :] for pg in tile_pages], axis=1)
        lhs = jnp.concatenate([qrow_scr[...], maskt_scr[j * n_slabs + sub]], axis=1)
        s = jnp.dot(lhs, jnp.concatenate([kt_tile.astype(BF16), epat_ref[...]], axis=0), preferred_element_type=F32)
        if sub < n_slabs - 1:
            s = s + far_col
        else:
            near = slab_keys - LANE
            s = jnp.concatenate([s[:, :near] + far_col, s[:, near:] + jnp.where(j == pl.num_programs(1) - 1, nearbt_scr[...], far_col)], axis=1)
        rows_update(s, vt_tile)

    @pl.when(j == pl.num_programs(1) - 1)
    def _():
        kn = pad_rows(kvn_ref[...])
        last_blk = past // SEL_BLOCK
        new_mask = maskt_scr[last_blk // MASK_ROWS][:, last_blk % MASK_ROWS:last_blk % MASK_ROWS + 1].astype(F32)
        s_n = (jnp.dot(qrow_scr[...], kn[:, 2 * LANE:3 * LANE].T.astype(BF16), preferred_element_type=F32)
               + new_bias.T[:SMP_COLS] + new_mask)
        rows_update(s_n, kn[:, 3 * LANE:].T)
        acc_t = pad_rows(acc2_scr[...]).T
        l_t = pad_rows(jnp.broadcast_to(l2_scr[...], (SMP_COLS, LANE))).T[0:1]
        o_s = own_rows(acc_t) / jnp.where(lane < SMP_COLS, l_t, 1.0)
        g_t = pad_rows(g_ref[...]).T
        gate_rows = []
        for b in range(3):
            row = g_t[b * H_NSA:b * H_NSA + 1]
            for h in range(1, H_NSA):
                row = row + pltpu.roll(g_t[b * H_NSA + h:b * H_NSA + h + 1], TOK_PAD * h, axis=1)
            gate_rows.append(row)
        o_col = oc_scr[...] * gate_rows[0] + o_s * gate_rows[1] + ow_scr[...] * gate_rows[2]
        per_head = [o_col if h == 0 else pltpu.roll(o_col, LANE - TOK_PAD * h, axis=1) for h in range(H_NSA)]
        o_ref[...] = jnp.concatenate(per_head, axis=0).T[:TOK_PAD]


def _nsa_sample(pool, page_table, q, gates, kc, vc, kv_new, win_buf, win_new, rel_table, n_valid):
    B, n_pages_total = page_table.shape
    past = n_pages_total * PAGE_SIZE
    ncp = kc.shape[1]
    ns = past // SEL_BLOCK + 1
    nsp = -(-ns // MASK_ROWS) * MASK_ROWS
    col = np.arange(LANE)
    used = col < SMP_COLS
    gsum = jnp.asarray(((col[:, None] // (G_NSA * TOK_PAD) == col[None, :] // (G_NSA * TOK_PAD))
                        & (col[:, None] % TOK_PAD == col[None, :] % TOK_PAD) & used[:, None] & used[None, :]), BF16)
    tab_cols = jnp.pad(jnp.repeat(rel_table, TOK_PAD, axis=1), ((0, 0), (0, LANE - SMP_COLS)))
    n_step = min(SMP_PAGES, n_pages_total)
    consts = [tab_cols, _pool_matrix(nsp, ncp), gsum, _sel_pattern(SLAB_PAGES * PAGE_SIZE, LANE).T]
    per_seq = [q, gates, kc, vc, kv_new, win_buf, win_new]
    seq_spec = lambda a: pl.BlockSpec((None,) + a.shape[1:], lambda b, j, pt: (b,) + (0,) * (a.ndim - 1))
    const = lambda a: pl.BlockSpec(a.shape, lambda b, j, pt: (0,) * a.ndim)

    def page_map(p):
        return lambda b, j, pt: (pt[b, j * n_step + p], 1, 0)

    grid_spec = pltpu.PrefetchScalarGridSpec(
        num_scalar_prefetch=1,
        grid=(B, n_pages_total // n_step),
        in_specs=[pl.BlockSpec((None, 2 * LANE, PAGE_SIZE), page_map(p)) for p in range(n_step)]
                 + [seq_spec(a) for a in per_seq] + [const(a) for a in consts],
        out_specs=pl.BlockSpec((None, TOK_PAD, H_NSA * HD), lambda b, j, pt: (b, 0, 0)),
        scratch_shapes=[pltpu.VMEM((nsp, LANE), F32), pltpu.VMEM((LANE, LANE), F32),
                        pltpu.VMEM((1, LANE), F32), pltpu.VMEM((1, LANE), F32), pltpu.VMEM((HD, LANE), F32), pltpu.VMEM((HD, LANE), F32),
                        pltpu.VMEM((SMP_COLS, LANE), BF16), pltpu.VMEM((nsp // MASK_ROWS, SMP_COLS, LANE), BF16),
                        pltpu.VMEM((SMP_COLS, 1), F32), pltpu.VMEM((SMP_COLS, LANE), F32),
                        pltpu.VMEM((SMP_COLS, 1), F32), pltpu.VMEM((SMP_COLS, 1), F32), pltpu.VMEM((SMP_COLS, LANE), F32)],
    )
    return pl.pallas_call(
        functools.partial(_nsa_sample_kernel, n_pages=n_step, n_valid=n_valid, past=past),
        grid_spec=grid_spec,
        out_shape=jax.ShapeDtypeStruct((B, TOK_PAD, H_NSA * HD), F32),
        compiler_params=_params(("arbitrary", "arbitrary")),
        name="nsa_sample",
    )(page_table, *([pool] * n_step), *per_seq, *consts)


def _outproj_kernel(x_ref, nsa_ref, rw_ref, gt_ref, lng_ref, lnb_ref, w_ref, o_ref):
    half = H_NSA * HD
    out = (jnp.dot(nsa_ref[...].astype(BF16), w_ref[0:half, :], preferred_element_type=F32)
           + jnp.dot(rw_ref[...].astype(BF16), w_ref[half:, :], preferred_element_type=F32))
    y = ALPHA * x_ref[...] + (1.0 + gt_ref[...]) * out
    o_ref[...] = _layer_norm(y, lng_ref[...], lnb_ref[...])


def _outproj(x, o_nsa, o_rwkv, gate, ln_g, ln_b, w_out):
    rows = x.shape[0]
    tm = min(512, rows)
    row = lambda i: (i, 0)
    return pl.pallas_call(
        _outproj_kernel,
        grid=(rows // tm,),
        in_specs=[pl.BlockSpec((tm, D_MODEL), row), pl.BlockSpec((tm, H_NSA * HD), row), pl.BlockSpec((tm, D_RWKV), row),
                  _mod_spec(gate, tm), _resident((1, D_MODEL)), _resident((1, D_MODEL)), _resident(w_out.shape)],
        out_specs=pl.BlockSpec((tm, D_MODEL), row),
        out_shape=jax.ShapeDtypeStruct((rows, D_MODEL), F32),
        compiler_params=_params(("arbitrary",)),
        name="outproj",
    )(x, o_nsa, o_rwkv, gate, ln_g.reshape(1, -1), ln_b.reshape(1, -1), w_out)


def kernel(x_prompt, x_sample, cache_nsa_kv, cache_nsa_win, state_rwkv_shift, state_rwkv_wkv, page_table, c_prompt, c_sample, rel_table, w_ada, b_ada, ln_g, ln_b, ffn1_gate, ffn1_up, ffn1_down, ffn2_gate, ffn2_up, ffn2_down, w_in, w_out, cmp_pe_k, cmp_w1_k, cmp_b1_k, cmp_w2_k, cmp_pe_v, cmp_w1_v, cmp_b1_v, cmp_w2_v, rwkv_mu, rwkv_w0, rwkv_w2, rwkv_a0, rwkv_a2, rwkv_g2, rwkv_k_k, rwkv_k_a, rwkv_r_k, rwkv_gn_w, rwkv_gn_b):
    assert w_ada.shape[0] == DEPTH == 1 and x_prompt.shape[0] == 1
    l = 0
    lw = dict(cmp_pe_k=cmp_pe_k[l], cmp_w1_k=cmp_w1_k[l], cmp_b1_k=cmp_b1_k[l], cmp_w2_k=cmp_w2_k[l],
              cmp_pe_v=cmp_pe_v[l], cmp_w1_v=cmp_w1_v[l], cmp_b1_v=cmp_b1_v[l], cmp_w2_v=cmp_w2_v[l],
              rwkv_mu=rwkv_mu[l], rwkv_w0=rwkv_w0[l], rwkv_w2=rwkv_w2[l], rwkv_a0=rwkv_a0[l], rwkv_a2=rwkv_a2[l], rwkv_g2=rwkv_g2[l],
              rwkv_k_k=rwkv_k_k[l], rwkv_k_a=rwkv_k_a[l], rwkv_r_k=rwkv_r_k[l], rwkv_gn_w=rwkv_gn_w[l], rwkv_gn_b=rwkv_gn_b[l])
    T = x_prompt.shape[1]
    nb, nt = x_sample.shape[0], x_sample.shape[1]
    assert nt <= TOK_PAD
    n_seq = 1 + nb
    c_all = jnp.concatenate([c_prompt, c_sample, jnp.zeros((-n_seq % 8, D_MODEL), F32)], axis=0)
    mod = _ada(c_all, w_ada[l], b_ada[l])
    mod_p = mod[0:1].reshape(9, 1, D_MODEL)
    mod_s = jnp.repeat(mod[1:n_seq].reshape(nb, 9, D_MODEL), nt, axis=0).transpose(1, 0, 2)
    ffn1 = [w[l].astype(BF16) for w in (ffn1_gate, ffn1_up, ffn1_down)]
    ffn2 = [w[l].astype(BF16) for w in (ffn2_gate, ffn2_up, ffn2_down)]
    w_in_p = _prep_w_in(w_in[l])
    w_out_b = w_out[l].astype(BF16)

    def trunk_in(x, m):
        x1 = _ffn(x, m[0], m[1], m[2], ln_g[l, 0], ln_b[l, 0], *ffn1)
        return x1, _proj(x1, m[3], m[4], w_in_p)

    def trunk_out(x1, o_nsa, o_rwkv, m):
        x2 = _outproj(x1, o_nsa, o_rwkv, m[5], ln_g[l, 1], ln_b[l, 1], w_out_b)
        return _ffn(x2, m[6], m[7], m[8], ln_g[l, 2], ln_b[l, 2], *ffn2)

    xp1, (q, kv, win, gates, pr) = trunk_in(x_prompt[0], mod_p)
    o_rw, wkv_p = _rwkv(pr[None], jnp.zeros((1, 1, RW_PAD), F32), jnp.zeros((1, H_RWKV, HD_RWKV, HD_RWKV), F32), lw, min(RW_STEP, T))
    n_rows = T // PAGE_SIZE
    kc, vc = _compress(kv.reshape(n_rows, PAGE_SIZE, 4 * LANE), jnp.arange(n_rows, dtype=jnp.int32)[None], lw, transposed=False)
    o_nsa = _nsa_prompt(q, gates, kv, win, kc[0], vc[0], _band(rel_table), rel_table)
    y_prompt = trunk_out(xp1, o_nsa, o_rw[0], mod_p)
    kv_prompt = kv.reshape(1, 1, T, 4, N_KV, HD)
    win_prompt = win[T - min(WINDOW, T):].reshape(1, 1, -1, 2, N_KV, HD)
    shift_prompt = _rwkv_uncols(pr[T - 1]).reshape(1, 1, RWKV_COLS)

    xs1, (q_s, kv_s, win_s, gates_s, pr_s) = trunk_in(x_sample.reshape(nb * nt, D_MODEL), mod_s)
    tokens = lambda a: jnp.pad(a.reshape(nb, nt, -1), ((0, 0), (0, TOK_PAD - nt), (0, 0)))
    pr_pad = jnp.pad(pr_s.reshape(nb, nt, -1), ((0, 0), (0, RW_TOK_PAD - nt), (0, 0)))
    o_rw_s, wkv_s = _rwkv(pr_pad, _rwkv_cols(state_rwkv_shift[l])[:, None], state_rwkv_wkv[l], lw, nt)
    pool_t = jnp.transpose(cache_nsa_kv[l], (0, 2, 3, 4, 1)).reshape(-1, 4 * LANE, PAGE_SIZE)
    kc_s, vc_s = _compress(pool_t, page_table, lw, transposed=True)
    win_buf = cache_nsa_win[l]
    win_t = jnp.transpose(win_buf, (0, 2, 3, 4, 1)).reshape(nb, 2 * LANE, -1)
    o_nsa_s = _nsa_sample(pool_t, page_table, tokens(q_s), tokens(gates_s), kc_s, vc_s, tokens(kv_s),
                          win_t, tokens(win_s), rel_table, nt)
    y_sample = trunk_out(xs1, o_nsa_s[:, :nt].reshape(nb * nt, -1), o_rw_s[:, :nt].reshape(nb * nt, -1), mod_s)
    kv_sample = kv_s.reshape(1, nb, nt, 4, N_KV, HD)
    win_sample = jnp.concatenate([win_buf, win_s.reshape(nb, nt, 2, N_KV, HD)], axis=1)[None, :, nt:]
    shift_sample = _rwkv_uncols(pr_s.reshape(nb, nt, -1)[:, -1])[None]
    return (y_prompt[None], y_sample.reshape(nb, nt, D_MODEL), kv_prompt, win_prompt, shift_prompt, wkv_p[None],
            kv_sample, win_sample, shift_sample, wkv_s[None])
```

```python
import functools
import math

import numpy as np
import jax
import jax.numpy as jnp
from jax import lax
from jax.experimental import pallas as pl
from jax.experimental.pallas import tpu as pltpu

D_MODEL = 1024
PAGE_SIZE = 128
H_NSA = 8
N_KV = 2
G_NSA = H_NSA // N_KV
HD = 64
CMP_STRIDE = 16
CMP_BLOCK = 2 * CMP_STRIDE
CMP_HIDDEN = 256
SEL_BLOCK = 64
N_SEL = 16
WINDOW = 512
Q_BLOCK = 128
N_BUCKETS = 32
MAX_DISTANCE = 128
H_RWKV = 8
HD_RWKV = 64
D_RWKV = H_RWKV * HD_RWKV
DECAY_LORA = 32
AAA_LORA = 32
GATE_LORA = 96
GN_EPS = 64e-5
D_FF = 2816
LN_EPS = 1e-5
DEPTH = 1
ALPHA = (2 * DEPTH) ** 0.25

NSA_SIZES = (H_NSA * HD,) + (N_KV * HD,) * 6 + (H_NSA * 3,)
RWKV_SIZES = (D_RWKV, D_RWKV, D_RWKV, DECAY_LORA, AAA_LORA, GATE_LORA)
NSA_COLS = sum(NSA_SIZES)
RWKV_COLS = sum(RWKV_SIZES)

F32 = jnp.float32
BF16 = jnp.bfloat16
LANE = 128
NEG = -(2.0 ** 100)
M_INIT = -(2.0 ** 103)
VMEM_LIMIT = 56 * 1024 * 1024

RW_PAD = 3 * D_RWKV + 3 * LANE
P_Q, P_KV, P_WIN, P_GATE, P_RW = 0, 512, 1024, 1280, 1408
P_COLS = P_RW + RW_PAD
KEY_TILE = 512
MASK_ROWS = 16
SEL_MASK0 = HD
SEL_FAR0 = SEL_MASK0 + MASK_ROWS
V_ROWS_KV = HD + MASK_ROWS
CMP_CLASS_ROWS = 256
Q_PER_TILE = KEY_TILE // Q_BLOCK
CMP_PER_Q = Q_BLOCK // CMP_STRIDE
EDGE_ROWS = 32


def _bucket_lows():
    d = np.arange(0, 4 * MAX_DISTANCE, dtype=np.int64)
    max_exact = N_BUCKETS // 2
    df = np.maximum(d, 1).astype(np.float32)
    large = max_exact + (np.log(df / np.float32(max_exact)) / np.float32(math.log(MAX_DISTANCE / max_exact))
                         * np.float32(N_BUCKETS - max_exact)).astype(np.int32)
    b = np.where(d < max_exact, d, np.minimum(large, N_BUCKETS - 1))
    lows = [int(np.argmax(b >= k)) for k in range(N_BUCKETS)]
    return b, lows


_BUCKET_OF, _BUCKET_LOW = _bucket_lows()
FAR_DIST = _BUCKET_LOW[N_BUCKETS - 1]


def _resident(shape):
    nd = len(shape)
    return pl.BlockSpec(shape, lambda *_: (0,) * nd, pipeline_mode=pl.Buffered(1))


def _params(sem):
    return pltpu.CompilerParams(dimension_semantics=sem, vmem_limit_bytes=VMEM_LIMIT)


def _dot_exact_rhs(x, rhs_bf16, terms=2):
    acc = None
    rem = x
    for _ in range(terms):
        part = rem.astype(BF16)
        d = jnp.dot(part, rhs_bf16, preferred_element_type=F32)
        acc = d if acc is None else acc + d
        rem = rem - part.astype(F32)
    return acc


def _dot_exact_lhs(lhs_bf16, x, terms=3):
    acc = None
    rem = x
    for _ in range(terms):
        part = rem.astype(BF16)
        d = jnp.dot(lhs_bf16, part, preferred_element_type=F32)
        acc = d if acc is None else acc + d
        rem = rem - part.astype(F32)
    return acc


def _layer_norm(y, g, b):
    mu = jnp.mean(y, axis=-1, keepdims=True)
    yc = y - mu
    var = jnp.mean(yc * yc, axis=-1, keepdims=True)
    return yc * lax.rsqrt(var + LN_EPS) * g + b


def _bias_chain(d, tab_rows):
    out = tab_rows[0] + jnp.zeros(d.shape, F32)
    for b in range(1, N_BUCKETS):
        out = jnp.where(d >= _BUCKET_LOW[b], tab_rows[b], out)
    return out


def _ada_kernel(c_ref, w_ref, b_ref, o_ref):
    c = c_ref[...]
    h = (c * jax.nn.sigmoid(c)).astype(BF16)
    o_ref[...] = jnp.dot(h, w_ref[...].astype(BF16), preferred_element_type=F32) + b_ref[...]


def _ada(c_all, w_ada, b_ada):
    rows, n = c_all.shape[0], w_ada.shape[1]
    tn = 1152
    return pl.pallas_call(
        _ada_kernel,
        grid=(n // tn,),
        in_specs=[pl.BlockSpec((rows, D_MODEL), lambda j: (0, 0)),
                  pl.BlockSpec((D_MODEL, tn), lambda j: (0, j)),
                  pl.BlockSpec((1, tn), lambda j: (0, j))],
        out_specs=pl.BlockSpec((rows, tn), lambda j: (0, j)),
        out_shape=jax.ShapeDtypeStruct((rows, n), F32),
        compiler_params=_params(("arbitrary",)),
        name="ada",
    )(c_all, w_ada, b_ada.reshape(1, n))


FF_CHUNKS = 2


def _ffn_kernel(x_ref, sh_ref, sc_ref, gt_ref, lng_ref, lnb_ref, wg_ref, wu_ref, wd_ref, o_ref):
    x = x_ref[...]
    h = (x * (1.0 + sc_ref[...]) + sh_ref[...]).astype(BF16)
    ck = D_FF // FF_CHUNKS
    acc = jnp.zeros(x.shape, F32)
    for c in range(FF_CHUNKS):
        a = jnp.dot(h, wg_ref[:, c * ck:(c + 1) * ck], preferred_element_type=F32)
        b = jnp.dot(h, wu_ref[:, c * ck:(c + 1) * ck], preferred_element_type=F32)
        t = (a * jax.nn.sigmoid(a) * b).astype(BF16)
        acc = acc + jnp.dot(t, wd_ref[c * ck:(c + 1) * ck, :], preferred_element_type=F32)
    y = ALPHA * x + (1.0 + gt_ref[...]) * (0.5 * acc)
    o_ref[...] = _layer_norm(y, lng_ref[...], lnb_ref[...])


def _mod_spec(mod, tm):
    if mod.shape[0] == 1:
        return pl.BlockSpec((1, D_MODEL), lambda i: (0, 0))
    return pl.BlockSpec((tm, D_MODEL), lambda i: (i, 0))


def _ffn(x, shift, scale, gate, ln_g, ln_b, wg, wu, wd):
    rows = x.shape[0]
    tm = min(512, rows)
    row = lambda i: (i, 0)
    return pl.pallas_call(
        _ffn_kernel,
        grid=(rows // tm,),
        in_specs=[pl.BlockSpec((tm, D_MODEL), row), _mod_spec(shift, tm), _mod_spec(scale, tm), _mod_spec(gate, tm),
                  _resident((1, D_MODEL)), _resident((1, D_MODEL)),
                  _resident((D_MODEL, D_FF)), _resident((D_MODEL, D_FF)), _resident((D_FF, D_MODEL))],
        out_specs=pl.BlockSpec((tm, D_MODEL), row),
        out_shape=jax.ShapeDtypeStruct((rows, D_MODEL), F32),
        compiler_params=_params(("arbitrary",)),
        name="ffn",
    )(x, shift, scale, gate, ln_g.reshape(1, -1), ln_b.reshape(1, -1), wg, wu, wd)


def _proj_kernel(x_ref, sh_ref, sc_ref, w_ref, q_ref, kv_ref, win_ref, g_ref, pr_ref):
    h = (x_ref[...] * (1.0 + sc_ref[...]) + sh_ref[...]).astype(BF16)
    p = jnp.dot(h, w_ref[...], preferred_element_type=F32)
    q_ref[...] = p[:, P_Q:P_KV]
    kv_ref[...] = p[:, P_KV:P_WIN]
    win_ref[...] = p[:, P_WIN:P_GATE]
    g_ref[...] = jax.nn.sigmoid(p[:, P_GATE:P_RW])
    pr_ref[...] = p[:, P_RW:P_COLS]


def _proj(x, shift, scale, w_in_p):
    rows = x.shape[0]
    tm = min(512, rows)
    row = lambda i: (i, 0)
    widths = (512, 512, 256, LANE, RW_PAD)
    return pl.pallas_call(
        _proj_kernel,
        grid=(rows // tm,),
        in_specs=[pl.BlockSpec((tm, D_MODEL), row), _mod_spec(shift, tm), _mod_spec(scale, tm),
                  _resident((D_MODEL, P_COLS))],
        out_specs=[pl.BlockSpec((tm, w), row) for w in widths],
        out_shape=[jax.ShapeDtypeStruct((rows, w), F32) for w in widths],
        compiler_params=_params(("arbitrary",)),
        name="proj",
    )(x, shift, scale, w_in_p)


def _prep_w_in(w_in):
    pad = lambda a, n: jnp.pad(a, ((0, 0), (0, n - a.shape[1])))
    nsa, rw = w_in[:, :NSA_COLS], w_in[:, NSA_COLS:]
    gl = nsa[:, 1280:1304].reshape(D_MODEL, H_NSA, 3).transpose(0, 2, 1).reshape(D_MODEL, 3 * H_NSA)
    cols = [nsa[:, :1280], pad(gl, LANE), _rwkv_cols(rw)]
    return jnp.concatenate(cols, axis=1).astype(BF16)


def _rwkv_cols(a):
    pad = lambda t: jnp.pad(t, [(0, 0)] * (t.ndim - 1) + [(0, LANE - t.shape[-1])])
    n = 3 * D_RWKV
    return jnp.concatenate([a[..., :n], pad(a[..., n:n + 32]), pad(a[..., n + 32:n + 64]), pad(a[..., n + 64:n + 160])], axis=-1)


def _rwkv_uncols(a):
    n = 3 * D_RWKV
    return jnp.concatenate([a[..., :n], a[..., n:n + 32], a[..., n + LANE:n + LANE + 32], a[..., n + 2 * LANE:n + 2 * LANE + 96]], axis=-1)


RW_GROUP = 64
RW_TOK_PAD = 16
RW_PAIRS = H_RWKV // 2
RW_BLOCK = 64
RW_STEP = 256


def _lora(x, w_ref):
    w = w_ref[...]
    w_hi = w.astype(BF16)
    w_lo = (w - w_hi.astype(F32)).astype(BF16)
    return _dot_exact_rhs(x, w_hi) + jnp.dot(x.astype(BF16), w_lo, preferred_element_type=F32)


def _rwkv_kernel(pr_ref, sh0_ref, s0_ref, mu_ref, w0_ref, a0_ref, kk_ref, ka_ref, rk_ref, gw_ref, gb_ref,
                 w2_ref, a2_ref, g2_ref, bo_ref, lgrp_ref, ggrp_ref, o_ref, sout_ref,
                 prev_scr, s_scr, *, n_valid):
    tb = pr_ref.shape[0]
    step = pl.program_id(1)

    @pl.when(step == 0)
    def _():
        prev_scr[...] = sh0_ref[...]
        s_scr[...] = s0_ref[...]

    p = pr_ref[...]
    rows = lax.broadcasted_iota(jnp.int32, (tb, 1), 0)
    prev = jnp.where(rows == 0, prev_scr[...], pltpu.roll(p, 1, axis=0))
    prev_scr[...] = p[tb - 1:tb, :]
    xs = p + (prev - p) * mu_ref[...]
    n = D_RWKV
    r, k, v = xs[:, :n], xs[:, n:2 * n], xs[:, 2 * n:3 * n]
    wl, al, gl = xs[:, 3 * n:3 * n + LANE], xs[:, 3 * n + LANE:3 * n + 2 * LANE], xs[:, 3 * n + 2 * LANE:]
    z = -(w0_ref[...] + _lora(jnp.tanh(wl), w2_ref))
    w = -(jnp.maximum(z, 0.0) + jnp.log(1.0 + jnp.exp(-jnp.abs(z)))) - 0.5
    a = jax.nn.sigmoid(a0_ref[...] + _lora(al, a2_ref))
    g = _lora(jax.nn.sigmoid(gl), g2_ref)
    kk = k * kk_ref[...]
    ss = _dot_exact_rhs(kk * kk, bo_ref[...])
    kk = kk / jnp.maximum(jnp.sqrt(ss), 1e-12)
    k2 = k * (1.0 + (a - 1.0) * ka_ref[...])
    G = min(RW_GROUP, tb)
    log_dec = -jnp.exp(w)
    bet = kk * a
    if n_valid < tb:
        live = rows < n_valid
        log_dec, kk, bet, k2, v_in = (jnp.where(live, x, 0.0) for x in (log_dec, kk, bet, k2, v))
    else:
        v_in = v
    cum = _dot_exact_lhs(lgrp_ref[...], log_dec, terms=2)
    cum_end = _dot_exact_lhs(ggrp_ref[...], log_dec, terms=2)
    gam_inv = jnp.exp(-cum)
    gam_end = jnp.exp(cum_end - cum)
    k_hat = -kk * jnp.exp(cum - log_dec)
    r_hat = r * jnp.exp(cum)
    b_chk, k_chk = bet * gam_inv, k2 * gam_inv
    b_til, k_til = bet * gam_end, k2 * gam_end
    gam_group = jnp.exp(cum_end)

    lane = lax.broadcasted_iota(jnp.int32, (1, LANE), 1)
    low = lane < HD_RWKV
    lane_t = lane & (RW_BLOCK - 1)
    row = lax.broadcasted_iota(jnp.int32, (LANE, 1), 0)
    row_t = row & (RW_BLOCK - 1)
    same = ((row < HD_RWKV) == low) & ((row_t // G) == (lane_t // G))
    strict, incl = same & (lane_t < row_t), same & (lane_t <= row_t)
    bf = lambda x: x.astype(BF16)
    mm = lambda x, y: jnp.dot(bf(x), bf(y), preferred_element_type=F32)
    mm_nt = lambda x, y: lax.dot_general(bf(x), bf(y), (((1,), (1,)), ((), ())), preferred_element_type=F32)

    def rows_bd(x):
        if tb < RW_BLOCK:
            x = jnp.concatenate([x, jnp.zeros((RW_BLOCK - tb, LANE), F32)], axis=0)
        return jnp.concatenate([jnp.where(low, x, 0.0), jnp.where(low, 0.0, x)], axis=0)

    def mm3(x, y):
        xh, yh = bf(x), bf(y)
        xl, yl = bf(x - xh.astype(F32)), bf(y - yh.astype(F32))
        return jnp.dot(jnp.concatenate([xh, xl, xh], axis=1), jnp.concatenate([yh, yh, yl], axis=0), preferred_element_type=F32)

    units = max(1, tb // RW_BLOCK)
    unit_rows = min(tb, RW_BLOCK)
    pairs = range(RW_PAIRS)
    items = [(un, pp) for un in range(units) for pp in pairs]
    at = lambda x, it: x[it[0] * unit_rows:(it[0] + 1) * unit_rows, it[1] * LANE:(it[1] + 1) * LANE]
    kh_row = [rows_bd(at(k_hat, it)) for it in items]
    rh_row = [rows_bd(at(r_hat, it)) for it in items]
    kh_mat = [x.T for x in kh_row]
    rh_mat = [x.T for x in rh_row]
    state_in = [jnp.concatenate([rows_bd(at(b_chk, it)).T, rows_bd(at(k_chk, it)).T], axis=1) for it in items]
    upd_rows = [jnp.concatenate([rows_bd(at(b_til, it)), rows_bd(at(k_til, it))], axis=0) for it in items]
    v_t = [rows_bd(at(v_in, it)).T for it in items]
    v_t = [x[:HD_RWKV] + x[HD_RWKV:] for x in v_t]
    c_all = [mm(jnp.concatenate([kh_row[n], rh_row[n]], axis=0), state_in[n]) for n in range(len(items))]
    c_uu = [jnp.where(strict, c[:LANE, :LANE], 0.0) for c in c_all]
    c_uv = [jnp.where(strict, c[:LANE, LANE:], 0.0) for c in c_all]
    c_ru = [jnp.where(incl, c[LANE:, :LANE], 0.0) for c in c_all]
    c_rv = [jnp.where(incl, c[LANE:, LANE:], 0.0) for c in c_all]
    t_neu, power = list(c_uu), list(c_uu)
    span = 2
    while span < G:
        power = [mm(x, x) for x in power]
        t_neu = [t_neu[n] + power[n] + mm(t_neu[n], power[n]) for n in range(len(items))]
        span *= 2
    from_v = [mm_nt(v_t[n], c_uv[n]) for n in range(len(items))]
    st = [s_scr[pp] for pp in pairs]
    y_units = []
    for un in range(units):
        ns = [un * RW_PAIRS + pp for pp in pairs]
        y_t = [jnp.zeros((HD_RWKV, LANE), F32) for _ in pairs]
        for grp in range(unit_rows // G):
            here = (lane_t // G) == grp
            first = un * unit_rows + grp * G
            w_t = [jnp.where(here, mm(st[pp], kh_mat[ns[pp]]) + from_v[ns[pp]], 0.0) for pp in pairs]
            u_t = [w_t[pp] + mm_nt(w_t[pp], t_neu[ns[pp]]) for pp in pairs]
            v_g = [jnp.where(here, v_t[ns[pp]], 0.0) for pp in pairs]
            y_t = [y_t[pp] + jnp.where(here, mm(st[pp], rh_mat[ns[pp]]), 0.0) + mm_nt(u_t[pp], c_ru[ns[pp]])
                   + mm_nt(v_g[pp], c_rv[ns[pp]]) for pp in pairs]
            st = [st[pp] * gam_group[first:first + 1, pp * LANE:(pp + 1) * LANE]
                  + mm3(jnp.concatenate([u_t[pp], v_g[pp]], axis=1), upd_rows[ns[pp]]) for pp in pairs]
        lane_u = lax.broadcasted_iota(jnp.int32, (unit_rows, LANE), 1)
        pieces = []
        for pp in pairs:
            yt = jnp.concatenate([y_t[pp], jnp.zeros((LANE - HD_RWKV, LANE), F32)], axis=0).T
            pieces.append(jnp.where(lane_u < HD_RWKV, yt[:unit_rows], pltpu.roll(yt[RW_BLOCK:RW_BLOCK + unit_rows], HD_RWKV, axis=1)))
        y_units.append(jnp.concatenate(pieces, axis=1))
    for pp in pairs:
        s_scr[pp] = st[pp]
    y = y_units[0] if units == 1 else jnp.concatenate(y_units, axis=0)
    mean = _dot_exact_rhs(y, bo_ref[...]) * (1.0 / HD_RWKV)
    yc = y - mean
    var = _dot_exact_rhs(yc * yc, bo_ref[...]) * (1.0 / HD_RWKV)
    yn = yc * lax.rsqrt(var + GN_EPS) * gw_ref[...] + gb_ref[...]
    bonus = _dot_exact_rhs(r * k2 * rk_ref[...], bo_ref[...]) * v
    o_ref[...] = (yn + bonus) * g
    sout_ref[...] = s_scr[...]


def _pair_state(s):
    B = s.shape[0]
    return s.reshape(B, RW_PAIRS, 2, HD_RWKV, HD_RWKV).transpose(0, 1, 3, 2, 4).reshape(B, RW_PAIRS, HD_RWKV, LANE)


def _unpair_state(s):
    B = s.shape[0]
    return s.reshape(B, RW_PAIRS, HD_RWKV, 2, HD_RWKV).transpose(0, 1, 3, 2, 4).reshape(B, H_RWKV, HD_RWKV, HD_RWKV)


def _rwkv(pr, shift0, s0, lw, n_valid):
    B, T, _ = pr.shape
    tb = min(RW_STEP, T)
    n = D_RWKV
    vec = lambda a: a.reshape(1, n)
    padrow = lambda a: jnp.pad(a, ((0, LANE - a.shape[0]), (0, 0)))
    blk = np.arange(n) // HD_RWKV
    block_ones = jnp.asarray(blk[:, None] == blk[None, :], BF16)
    tok = np.arange(tb)
    group = min(RW_GROUP, tb)
    same_group = tok[:, None] // group == tok[None, :] // group
    prefix = jnp.asarray(same_group & (tok[None, :] <= tok[:, None]), BF16)
    consts = [_rwkv_cols(lw['rwkv_mu']).reshape(1, RW_PAD), vec(lw['rwkv_w0']), vec(lw['rwkv_a0']), vec(lw['rwkv_k_k']),
              vec(lw['rwkv_k_a']), vec(lw['rwkv_r_k']), vec(lw['rwkv_gn_w']), vec(lw['rwkv_gn_b']),
              padrow(lw['rwkv_w2']), padrow(lw['rwkv_a2']), padrow(lw['rwkv_g2']), block_ones, prefix, jnp.asarray(same_group, BF16)]
    kern = functools.partial(_rwkv_kernel, n_valid=n_valid)
    state_spec = pl.BlockSpec((None, RW_PAIRS, HD_RWKV, LANE), lambda b, j: (b, 0, 0, 0))
    o, s = pl.pallas_call(
        kern,
        grid=(B, T // tb),
        in_specs=[pl.BlockSpec((None, tb, RW_PAD), lambda b, j: (b, j, 0)),
                  pl.BlockSpec((None, 1, RW_PAD), lambda b, j: (b, 0, 0)), state_spec]
                 + [_resident(c.shape) for c in consts],
        out_specs=[pl.BlockSpec((None, tb, n), lambda b, j: (b, j, 0)), state_spec],
        out_shape=[jax.ShapeDtypeStruct((B, T, n), F32), jax.ShapeDtypeStruct((B, RW_PAIRS, HD_RWKV, LANE), F32)],
        scratch_shapes=[pltpu.VMEM((1, RW_PAD), F32), pltpu.VMEM((RW_PAIRS, HD_RWKV, LANE), F32)],
        compiler_params=_params(("arbitrary", "arbitrary")),
        name="rwkv",
    )(pr, shift0, _pair_state(s0), *consts)
    return o, _unpair_state(s)


CMP_PAGES = 64
CHUNKS_PER_PAGE = PAGE_SIZE // CMP_STRIDE


def _compress_kernel(pt_ref, *refs, n_pages, transposed):
    pages, nxt = refs[:n_pages], refs[n_pages]
    weights = refs[n_pages + 1:n_pages + 9]
    outs = refs[n_pages + 9:n_pages + 11]
    width = N_KV * HD
    rows = CHUNKS_PER_PAGE * n_pages
    seg = rows + 8
    kinds = range(2)
    low = lax.broadcasted_iota(jnp.int32, (1, N_KV * HD), 1) < HD
    rows_scr = refs[n_pages + 11:n_pages + 13]

    def by_head(row_s):
        heads = [[], []]
        for s in range(0, CMP_STRIDE, 2):
            a, b = row_s(s), row_s(s + 1)
            heads[0].append(jnp.where(low, a, pltpu.roll(b, HD, axis=1)))
            heads[1].append(jnp.where(low, pltpu.roll(a, HD, axis=1), b))
        return [jnp.concatenate(h, axis=1) for h in heads]

    for kind in kinds:
        pe_ref, w_ref, b_ref, w2_ref = weights[4 * kind:4 * kind + 4]
        part = slice(kind * width, (kind + 1) * width)
        for p, pg in enumerate(pages):
            rows_scr[kind][p * PAGE_SIZE:(p + 1) * PAGE_SIZE, :] = pg[part, :].T if transposed else pg[:, part]
        nxt_rows = nxt[part, :].T[:CMP_STRIDE] if transposed else nxt[:, part]
        x = by_head(lambda s: rows_scr[kind][pl.ds(s, rows, stride=CMP_STRIDE), :])
        x_next = by_head(lambda s: jnp.broadcast_to(nxt_rows[s:s + 1, :], (8, N_KV * HD)))
        x_all = jnp.concatenate([x[0], x_next[0], x[1], x_next[1]], axis=0)
        h_first = jnp.dot((x_all + pe_ref[0]).astype(BF16), w_ref[0], preferred_element_type=F32)
        h_second = jnp.dot((x_all + pe_ref[1]).astype(BF16), w_ref[1], preferred_element_type=F32)
        out = None
        for h in range(N_KV):
            h_next = pltpu.roll(h_second[h * seg:(h + 1) * seg], seg - 1, axis=0)[:rows]
            hidden = jax.nn.gelu(h_first[h * seg:h * seg + rows] + h_next + b_ref[...])
            part = jnp.dot(hidden.astype(BF16), w2_ref[h], preferred_element_type=F32)
            out = part if out is None else out + part
        outs[kind][...] = out


def _compress_weights(pe, w1, b1, w2):
    n = CMP_STRIDE * HD
    pe2 = pe.reshape(2, 1, n)
    w_halves = w1.reshape(2, n, CMP_HIDDEN).astype(BF16)
    zero = jnp.zeros_like(w2)
    w2_heads = jnp.stack([jnp.concatenate([w2, zero], axis=1), jnp.concatenate([zero, w2], axis=1)]).astype(BF16)
    return [pe2, w_halves, b1.reshape(1, -1), w2_heads]


def _compress(pool, page_table, lw, transposed):
    B, n_pages_total = page_table.shape
    n_pages = min(CMP_PAGES, n_pages_total)
    rows = CHUNKS_PER_PAGE * n_pages
    weights = (_compress_weights(lw['cmp_pe_k'], lw['cmp_w1_k'], lw['cmp_b1_k'], lw['cmp_w2_k'])
               + _compress_weights(lw['cmp_pe_v'], lw['cmp_w1_v'], lw['cmp_b1_v'], lw['cmp_w2_v']))
    width = N_KV * HD

    def page_map(p):
        return lambda b, j, pt: (pt[b, j * n_pages + p], 0, 0)

    next_map = lambda b, j, pt: (pt[b, jnp.minimum((j + 1) * n_pages, n_pages_total - 1)], 0, 0)
    page_block = (None, 2 * width, PAGE_SIZE) if transposed else (None, PAGE_SIZE, 2 * width)
    next_block = page_block if transposed else (None, CMP_STRIDE, 2 * width)
    const = lambda a: pl.BlockSpec(a.shape, lambda b, j, pt: (0,) * a.ndim)
    out_spec = pl.BlockSpec((None, rows, N_KV * HD), lambda b, j, pt: (b, j, 0))
    out_shape = jax.ShapeDtypeStruct((B, n_pages_total * CHUNKS_PER_PAGE, N_KV * HD), F32)
    grid_spec = pltpu.PrefetchScalarGridSpec(
        num_scalar_prefetch=1,
        grid=(B, n_pages_total // n_pages),
        in_specs=[pl.BlockSpec(page_block, page_map(p)) for p in range(n_pages)]
                 + [pl.BlockSpec(next_block, next_map)] + [const(a) for a in weights],
        out_specs=[out_spec, out_spec],
        scratch_shapes=[pltpu.VMEM((n_pages * PAGE_SIZE, width), F32)] * 2,
    )
    return pl.pallas_call(
        functools.partial(_compress_kernel, n_pages=n_pages, transposed=transposed),
        grid_spec=grid_spec,
        out_shape=[out_shape, out_shape],
        compiler_params=_params(("arbitrary", "arbitrary")),
        name="compress",
    )(page_table, *([pool] * (n_pages + 1)), *weights)


BAND_ROWS = 1152


def _band_kernel(tab_ref, bkt_ref, o_ref):
    h = pl.program_id(0)
    bkt = bkt_ref[...]
    out = jnp.full(bkt.shape, NEG, F32)
    for b in range(N_BUCKETS):
        out = jnp.where(bkt == b, tab_ref[b, h], out)
    o_ref[...] = out


def _band(rel_table):
    u = np.arange(BAND_ROWS)[:, None]
    qi = np.arange(Q_BLOCK)[None, :]
    d = qi + WINDOW - u
    bkt = np.where(d >= 0, _BUCKET_OF[np.clip(d, 0, len(_BUCKET_OF) - 1)], -1).astype(np.int32)
    return pl.pallas_call(
        _band_kernel,
        grid=(H_NSA,),
        in_specs=[pl.BlockSpec(memory_space=pltpu.SMEM), pl.BlockSpec((BAND_ROWS, Q_BLOCK), lambda h: (0, 0))],
        out_specs=pl.BlockSpec((None, BAND_ROWS, Q_BLOCK), lambda h: (h, 0, 0)),
        out_shape=jax.ShapeDtypeStruct((H_NSA, BAND_ROWS, Q_BLOCK), F32),
        compiler_params=_params(("arbitrary",)),
        name="band",
    )(rel_table, jnp.asarray(bkt))


def _softmax_cols(s):
    m = jnp.max(s, axis=0, keepdims=True)
    e = jnp.exp(s - m)
    l = jnp.sum(e, axis=0, keepdims=True)
    return e * jnp.where(m > 0.5 * NEG, 1.0 / l, 0.0)


def _select_blocks(impsel, qpos, n_pick):
    ns = impsel.shape[0]
    blk = lax.broadcasted_iota(jnp.int32, impsel.shape, 0)
    cur = jnp.right_shift(qpos, SEL_BLOCK.bit_length() - 1)
    future = blk * SEL_BLOCK > qpos
    forced = (blk == 0) | (blk == cur) | (blk == cur - 1)
    score = jnp.where(future, -jnp.inf, jnp.where(forced, jnp.inf, impsel))
    chosen = jnp.zeros(impsel.shape, F32)
    for _ in range(n_pick):
        best = jnp.max(score, axis=0, keepdims=True)
        first = jnp.min(jnp.where(score == best, blk, ns), axis=0, keepdims=True)
        hit = (blk == first) & (best > -jnp.inf)
        chosen = jnp.where(hit, 1.0, chosen)
        score = jnp.where(hit, -jnp.inf, score)
    return jnp.where(chosen > 0.0, 0.0, NEG)


def _pool_matrix(ns, nc):
    j = np.arange(ns)[:, None]
    n = np.arange(nc)[None, :]
    ratio = SEL_BLOCK // CMP_STRIDE
    return jnp.asarray((n >= ratio * j - 1) & (n <= ratio * j + ratio - 1), BF16)


def _nsa_prompt_kernel(tab_ref, q_ref, g_ref, kc_ref, vct_ref, ks_ref, vst_ref, kw_ref, vwt_ref, band_ref, pool_ref, o_ref,
                       rhs_scr, mask_scr, acc_scr, m_scr, sc_scr, sa_scr, sb_scr, oc_scr):
    i = pl.program_id(0)
    ncp = kc_ref.shape[0]
    ns = pool_ref.shape[0]
    s0 = i * Q_BLOCK
    q_t = (q_ref[...] * HD ** -0.5).T
    g_t = g_ref[...].T
    lane_q = lax.broadcasted_iota(jnp.int32, (1, G_NSA * Q_BLOCK), 1) & (Q_BLOCK - 1)
    qpos = s0 + lax.broadcasted_iota(jnp.int32, (1, Q_BLOCK), 1)
    rhs_scr[...] = jnp.zeros(rhs_scr.shape, BF16)
    kvs = range(N_KV)
    kd = i // Q_PER_TILE
    r = i % Q_PER_TILE
    lanes4 = lambda k, f: jnp.concatenate([f(G_NSA * k + g) for g in range(G_NSA)], axis=1)
    qcols = [lanes4(k, lambda h: q_t[h * HD:(h + 1) * HD, :]) for k in kvs]
    zero = jnp.zeros_like(qcols[0])
    top = [jnp.concatenate([qcols[0], zero], axis=0).astype(BF16), jnp.concatenate([zero, qcols[1]], axis=0).astype(BF16)]
    far_row = [lanes4(k, lambda h: band_ref[h, 0:1, :]) for k in kvs]


    n0 = pl.multiple_of(jnp.clip(CMP_PER_Q * i - EDGE_ROWS // 2, 0, ncp - EDGE_ROWS), 8)
    nrow = lax.broadcasted_iota(jnp.int32, (ncp, 1), 0)
    d_edge = qpos - (CMP_STRIDE * (n0 + lax.broadcasted_iota(jnp.int32, (EDGE_ROWS, 1), 0)) + CMP_BLOCK - 1)

    def compressed(k, rows):
        sc_scr[k, 0:rows, :] = (jnp.dot(kc_ref[0:rows, :], top[k], preferred_element_type=F32)
                                + jnp.where(nrow[0:rows] < n0, far_row[k], NEG))
        edge_bias = lanes4(k, lambda h: jnp.where(d_edge >= 0, _bias_chain(d_edge, [tab_ref[b, h] for b in range(N_BUCKETS)]), NEG))
        sc_scr[k, pl.ds(n0, EDGE_ROWS), :] = jnp.dot(kc_ref[pl.ds(n0, EDGE_ROWS), :], top[k], preferred_element_type=F32) + edge_bias
        p_c = _softmax_cols(sc_scr[k, 0:rows, :])
        o_c = jnp.dot(vct_ref[:, 0:rows], p_c.astype(BF16), preferred_element_type=F32)[k * HD:(k + 1) * HD]
        imp = p_c[:, 0:Q_BLOCK]
        for g in range(1, G_NSA):
            imp = imp + p_c[:, g * Q_BLOCK:(g + 1) * Q_BLOCK]
        n_blk = rows * CMP_STRIDE // SEL_BLOCK
        return o_c, _dot_exact_lhs(pool_ref[0:n_blk, 0:rows], imp, terms=2)

    size_step = min(CMP_CLASS_ROWS, ncp)
    size_class = (n0 + EDGE_ROWS - 1) // size_step
    for cls in range(ncp // size_step):
        @pl.when(size_class == cls)
        def _(rows=(cls + 1) * size_step):
            comp = [compressed(k, rows) for k in kvs]
            n_blk = comp[0][1].shape[0]
            masks = [_select_blocks(comp[k][1], qpos, min(N_SEL, ns)) for k in kvs]
            for k in kvs:
                oc_scr[k] = comp[k][0]
                full = jnp.concatenate([masks[k], jnp.full((ns - n_blk, Q_BLOCK), NEG, F32)], axis=0) if n_blk < ns else masks[k]
                mask_scr[k] = jnp.concatenate([full.astype(BF16)] * G_NSA, axis=1)

    ws = pl.multiple_of(jnp.maximum(s0 - WINDOW, 0), Q_BLOCK)
    u0 = pl.multiple_of(WINDOW - (s0 - ws), Q_BLOCK)
    n_win = WINDOW + Q_BLOCK
    u = u0 + lax.broadcasted_iota(jnp.int32, (n_win, 1), 0)
    win_mask = jnp.where(u > lane_q, 0.0, NEG)

    def window(k):
        s_w = (jnp.dot(kw_ref[pl.ds(ws, n_win), :], top[k], preferred_element_type=F32)
               + lanes4(k, lambda h: band_ref[h, pl.ds(u0, n_win), :]) + win_mask)
        m_w = jnp.max(s_w, axis=0, keepdims=True)
        p_w = jnp.exp(s_w - m_w).astype(BF16)
        acc_w = jnp.zeros((V_ROWS_KV, G_NSA * Q_BLOCK), F32)
        for j in range(n_win // Q_BLOCK):
            acc_w = acc_w + jnp.dot(vwt_ref[k, ws // Q_BLOCK + j], p_w[j * Q_BLOCK:(j + 1) * Q_BLOCK], preferred_element_type=F32)
        return acc_w[0:HD] / acc_w[HD:HD + 1]

    o_w = [window(k) for k in kvs]

    for k in kvs:
        m_scr[k] = jnp.full(m_scr.shape[1:], M_INIT, F32)
        acc_scr[k] = jnp.zeros(acc_scr.shape[1:], F32)
        far_hi = far_row[k].astype(BF16).astype(F32)
        rhs_scr[k, 0:HD, :] = qcols[k].astype(BF16)
        rhs_scr[k, SEL_FAR0:SEL_FAR0 + MASK_ROWS, :] = jnp.concatenate(
            [far_hi, far_row[k] - far_hi, jnp.zeros((MASK_ROWS - 2, G_NSA * Q_BLOCK), F32)], axis=0).astype(BF16)

    def scores(k, slab, kts, extra):
        rhs_scr[k, SEL_MASK0:SEL_MASK0 + MASK_ROWS, :] = mask_scr[k, pl.ds(pl.multiple_of(slab * MASK_ROWS, MASK_ROWS), MASK_ROWS), :]
        rhs = rhs_scr[k]
        out = []
        for kt, add in zip(kts, extra):
            s = jnp.dot(ks_ref[k, pl.ds(pl.multiple_of(kt * KEY_TILE, KEY_TILE), KEY_TILE), :], rhs, preferred_element_type=F32)
            out.append(s if add is None else s + add)
        return out

    def update(k, kts, tiles):
        m_old = m_scr[k]
        m_new = m_old
        for s in tiles:
            m_new = jnp.maximum(m_new, jnp.max(s, axis=0, keepdims=True))
        acc = jnp.exp(m_old - m_new) * acc_scr[k]
        for kt, s in zip(kts, tiles):
            acc = acc + jnp.dot(vst_ref[k, kt], jnp.exp(s - m_new).astype(BF16), preferred_element_type=F32)
        acc_scr[k] = acc
        m_scr[k] = m_new

    def attend(slab, kts, extra):
        tiles = [scores(k, slab, kts, extra(k)) for k in kvs]
        for k in kvs:
            update(k, kts, tiles[k])

    near_at = lambda k, start: lanes4(k, lambda h: band_ref[h, pl.ds(pl.multiple_of(start, Q_BLOCK), KEY_TILE), :]) - far_row[k]
    prev_near = (r == 0) & (kd >= 1)
    kd_odd = (kd & 1) == 1
    even_prev = jnp.logical_not(kd_odd) & prev_near
    n_pairs = kd // 2 - even_prev.astype(jnp.int32)
    n_quads = n_pairs // 2
    no_bias = lambda k: [None, None]

    def pair_scores(k, dst, pair):
        lo, hi = scores(k, pair, [2 * pair, 2 * pair + 1], [None, None])
        dst[k, 0:KEY_TILE, :] = lo
        dst[k, KEY_TILE:, :] = hi

    def pair_update(k, src, pair):
        update(k, [2 * pair, 2 * pair + 1], [src[k, 0:KEY_TILE, :], src[k, KEY_TILE:, :]])

    @pl.when(n_quads > 0)
    def _():
        for k in kvs:
            pair_scores(k, sa_scr, 0)

    for k in kvs:
        def quad_body(qd, carry, k=k):
            first = 2 * qd
            pair_scores(k, sb_scr, first + 1)
            pair_update(k, sa_scr, first)
            pair_scores(k, sa_scr, jnp.minimum(first + 2, 2 * n_quads - 2))
            pair_update(k, sb_scr, first + 1)
            return carry

        lax.fori_loop(0, n_quads, quad_body, 0)

    @pl.when((n_pairs & 1) == 1)
    def _():
        attend(n_pairs - 1, [2 * n_pairs - 2, 2 * n_pairs - 1], no_bias)

    @pl.when(kd_odd)
    def _():
        attend(kd // 2, [kd - 1, kd], lambda k: [jnp.where(prev_near, near_at(k, 0), 0.0), near_at(k, KEY_TILE - Q_BLOCK * r)])

    @pl.when(even_prev)
    def _():
        attend(kd // 2 - 1, [kd - 2, kd - 1], lambda k: [None, near_at(k, 0)])

    @pl.when(jnp.logical_not(kd_odd))
    def _():
        attend(kd // 2, [kd], lambda k: [near_at(k, KEY_TILE - Q_BLOCK * r)])

    heads_out = []
    for k in kvs:
        acc = acc_scr[k]
        o_s = acc[0:HD] / acc[HD:HD + 1]
        o_c = oc_scr[k]
        for g in range(G_NSA):
            h = G_NSA * k + g
            cols = slice(g * Q_BLOCK, (g + 1) * Q_BLOCK)
            heads_out.append(o_c[:, cols] * g_t[h:h + 1] + o_s[:, cols] * g_t[H_NSA + h:H_NSA + h + 1]
                             + o_w[k][:, cols] * g_t[2 * H_NSA + h:2 * H_NSA + h + 1])
    o_ref[...] = jnp.concatenate(heads_out, axis=0).T


def _sel_pattern(rows, width):
    key = np.arange(rows)[:, None]
    b = np.arange(width)[None, :]
    ones = (b >= MASK_ROWS) & (b < MASK_ROWS + 2)
    return jnp.asarray(((key // SEL_BLOCK) % MASK_ROWS == b) | ones, BF16)


def _values_t(v, tile):
    T, n = v.shape
    rows = n + MASK_ROWS
    vt = jnp.concatenate([v.T, jnp.ones((1, T), F32), jnp.zeros((rows - n - 1, T), F32)], axis=0)
    return vt.reshape(rows, T // tile, tile).transpose(1, 0, 2).astype(BF16)


def _nsa_prompt(q, gates, kv, win, kc, vc, band, rel_table):
    T = q.shape[0]
    ncp, ns = kc.shape[0], T // SEL_BLOCK
    width = G_NSA * Q_BLOCK
    k_sel = lambda h: kv[:, 256 + h * HD:256 + (h + 1) * HD]
    v_sel = lambda h: kv[:, 384 + h * HD:384 + (h + 1) * HD]
    pattern = _sel_pattern(T, LANE - HD)
    ks_aug = jnp.stack([jnp.concatenate([k_sel(h).astype(BF16), pattern], axis=1) for h in range(N_KV)])
    operands = [q, gates, kc.astype(BF16), vc.T.astype(BF16), ks_aug, jnp.stack([_values_t(v_sel(h), KEY_TILE) for h in range(N_KV)]),
                win[:, 0:128].astype(BF16), jnp.stack([_values_t(win[:, 128 + h * HD:128 + (h + 1) * HD], Q_BLOCK) for h in range(N_KV)]),
                band, _pool_matrix(ns, ncp)]
    blk = lambda w: pl.BlockSpec((Q_BLOCK, w), lambda i: (i, 0))
    return pl.pallas_call(
        _nsa_prompt_kernel,
        grid=(T // Q_BLOCK,),
        in_specs=[pl.BlockSpec(memory_space=pltpu.SMEM), blk(H_NSA * HD), blk(LANE)] + [_resident(a.shape) for a in operands[2:]],
        out_specs=blk(H_NSA * HD),
        out_shape=jax.ShapeDtypeStruct((T, H_NSA * HD), F32),
        scratch_shapes=[pltpu.VMEM((N_KV, LANE, width), BF16), pltpu.VMEM((N_KV, ns, width), BF16),
                        pltpu.VMEM((N_KV, V_ROWS_KV, width), F32), pltpu.VMEM((N_KV, 1, width), F32),
                        pltpu.VMEM((N_KV, ncp, width), F32)] + [pltpu.VMEM((N_KV, 2 * KEY_TILE, width), F32)] * 2
                       + [pltpu.VMEM((N_KV, HD, width), F32)],
        compiler_params=_params(("arbitrary",)),
        name="nsa_prompt",
    )(rel_table, *operands)


SLAB_PAGES = 8
SMP_PAGES = 32
TOK_PAD = 8
SMP_COLS = H_NSA * TOK_PAD


def _nsa_sample_kernel(pt_ref, *refs, n_pages, n_valid, past):
    pages = refs[:n_pages]
    (q_ref, g_ref, kc_ref, vc_ref, kvn_ref, win_ref, winn_ref, tab_ref, pool_ref, gsum_ref, epat_ref, o_ref,
     mask_scr, acc_scr, m_scr, l_scr, oc_scr, ow_scr, qrow_scr, maskt_scr, farc_scr, nearbt_scr, m2_scr, l2_scr, acc2_scr) = refs[n_pages:]
    j = pl.program_id(1)
    ncp, wbuf = kc_ref.shape[0], win_ref.shape[1]
    lane = lax.broadcasted_iota(jnp.int32, (1, LANE), 1)
    tok = lane & (TOK_PAD - 1)
    second_kv = lane >= G_NSA * TOK_PAD
    tab = [tab_ref[b:b + 1, :] for b in range(N_BUCKETS)]
    far_row = tab[N_BUCKETS - 1]
    own_rows = lambda x: jnp.where(second_kv, x[HD:2 * HD], x[0:HD])
    pad_rows = lambda x: jnp.concatenate([x, jnp.zeros((LANE - x.shape[0], x.shape[1]), x.dtype)], axis=0)
    trow = lax.broadcasted_iota(jnp.int32, (LANE, 1), 0)
    d_new = tok - trow
    new_bias = jnp.where((d_new >= 0) & (trow < n_valid), _bias_chain(jnp.maximum(d_new, 0), tab), NEG)

    def attend_update(s, values_t):
        m_old = m_scr[...]
        m_new = jnp.maximum(m_old, jnp.max(s, axis=0, keepdims=True))
        alpha = jnp.exp(m_old - m_new)
        p = jnp.exp(s - m_new)
        l_scr[...] = alpha * l_scr[...] + jnp.sum(p, axis=0, keepdims=True)
        acc_scr[...] = alpha * acc_scr[...] + jnp.dot(values_t.astype(BF16), p.astype(BF16), preferred_element_type=F32)
        m_scr[...] = m_new

    def reset():
        m_scr[...] = jnp.full(m_scr.shape, M_INIT, F32)
        l_scr[...] = jnp.zeros(l_scr.shape, F32)
        acc_scr[...] = jnp.zeros(acc_scr.shape, F32)

    @pl.when(j == 0)
    def _():
        q_t = pad_rows(q_ref[...] * HD ** -0.5).T
        halves = []
        for k in range(N_KV):
            part = jnp.zeros((HD, LANE), F32)
            for g in range(G_NSA):
                h = G_NSA * k + g
                piece = q_t[h * HD:(h + 1) * HD, :]
                part = part + (pltpu.roll(piece, TOK_PAD * h, axis=1) if h else piece)
            halves.append(part)
        top_f = jnp.concatenate(halves, axis=0)
        top = top_f.astype(BF16)
        qrow_scr[...] = top_f.T[:SMP_COLS].astype(BF16)
        qpos = past + tok

        n0 = ncp - EDGE_ROWS
        kcb = kc_ref[...].astype(BF16)
        d_edge = qpos - (CMP_STRIDE * (n0 + lax.broadcasted_iota(jnp.int32, (EDGE_ROWS, 1), 0)) + CMP_BLOCK - 1)
        s_c = jnp.concatenate([
            jnp.dot(kcb[:n0], top, preferred_element_type=F32) + far_row,
            jnp.dot(kcb[n0:], top, preferred_element_type=F32) + jnp.where(d_edge >= 0, _bias_chain(jnp.maximum(d_edge, 0), tab), NEG)], axis=0)
        p_c = _softmax_cols(s_c)
        oc_scr[...] = own_rows(jnp.dot(vc_ref[...].T.astype(BF16), p_c.astype(BF16), preferred_element_type=F32))
        imp = _dot_exact_rhs(p_c, gsum_ref[...], terms=3)
        mask_scr[...] = _select_blocks(_dot_exact_lhs(pool_ref[...], imp), qpos, N_SEL)

        wk = win_ref[0:LANE, :].T
        d_w = wbuf + tok - lax.broadcasted_iota(jnp.int32, (wbuf, 1), 0)
        near = wbuf - LANE
        s_w = jnp.dot(wk.astype(BF16), top, preferred_element_type=F32)
        s_w = (jnp.concatenate([s_w[:near] + far_row, s_w[near:] + _bias_chain(d_w[near:], tab)], axis=0)
               + jnp.where(d_w < WINDOW, 0.0, NEG))
        reset()
        attend_update(s_w, win_ref[LANE:, :])
        wn = pad_rows(winn_ref[...])
        attend_update(jnp.dot(wn[:, :LANE].astype(BF16), top, preferred_element_type=F32) + new_bias, wn[:, LANE:].T)
        ow_scr[...] = own_rows(acc_scr[...]) / l_scr[...]
        m2_scr[...] = jnp.full(m2_scr.shape, M_INIT, F32)
        l2_scr[...] = jnp.zeros(l2_scr.shape, F32)
        acc2_scr[...] = jnp.zeros(acc2_scr.shape, F32)
        for sl in range(maskt_scr.shape[0]):
            maskt_scr[sl] = pad_rows(mask_scr[sl * MASK_ROWS:(sl + 1) * MASK_ROWS, :]).T[:SMP_COLS].astype(BF16)
        farc_scr[...] = pad_rows(tab_ref[...]).T[:SMP_COLS, N_BUCKETS - 1:N_BUCKETS]
        nearbt_scr[...] = _bias_chain(LANE + tok - lax.broadcasted_iota(jnp.int32, (LANE, 1), 0), tab).T[:SMP_COLS]

    def rows_update(s, values_t):
        m_old = m2_scr[...]
        m_new = jnp.maximum(m_old, jnp.max(s, axis=1, keepdims=True))
        alpha = jnp.exp(m_old - m_new)
        p = jnp.exp(s - m_new)
        l2_scr[...] = alpha * l2_scr[...] + jnp.sum(p, axis=1, keepdims=True)
        acc2_scr[...] = alpha * acc2_scr[...] + lax.dot_general(p.astype(BF16), values_t.astype(BF16), (((1,), (1,)), ((), ())),
                                                                preferred_element_type=F32)
        m2_scr[...] = m_new

    n_slabs = n_pages // SLAB_PAGES
    slab_keys = SLAB_PAGES * PAGE_SIZE
    far_col = farc_scr[...]
    for sub in range(n_slabs):
        tile_pages = pages[sub * SLAB_PAGES:(sub + 1) * SLAB_PAGES]
        kt_tile = jnp.concatenate([pg[0:LANE, :] for pg in tile_pages], axis=1)
        vt_tile = jnp.concatenate([pg[LANE:, :] for pg in tile_pages], axis=1)
        lhs = jnp.concatenate([qrow_scr[...], maskt_scr[j * n_slabs + sub]], axis=1)
        s = jnp.dot(lhs, jnp.concatenate([kt_tile.astype(BF16), epat_ref[...]], axis=0), preferred_element_type=F32)
        if sub < n_slabs - 1:
            s = s + far_col
        else:
            near = slab_keys - LANE
            s = jnp.concatenate([s[:, :near] + far_col, s[:, near:] + jnp.where(j == pl.num_programs(1) - 1, nearbt_scr[...], far_col)], axis=1)
        rows_update(s, vt_tile)

    @pl.when(j == pl.num_programs(1) - 1)
    def _():
        kn = pad_rows(kvn_ref[...])
        last_blk = past // SEL_BLOCK
        new_mask = maskt_scr[last_blk // MASK_ROWS][:, last_blk % MASK_ROWS:last_blk % MASK_ROWS + 1].astype(F32)
        s_n = (jnp.dot(qrow_scr[...], kn[:, 2 * LANE:3 * LANE].T.astype(BF16), preferred_element_type=F32)
               + new_bias.T[:SMP_COLS] + new_mask)
        rows_update(s_n, kn[:, 3 * LANE:].T)
        acc_t = pad_rows(acc2_scr[...]).T
        l_t = pad_rows(jnp.broadcast_to(l2_scr[...], (SMP_COLS, LANE))).T[0:1]
        o_s = own_rows(acc_t) / jnp.where(lane < SMP_COLS, l_t, 1.0)
        g_t = pad_rows(g_ref[...]).T
        gate_rows = []
        for b in range(3):
            row = g_t[b * H_NSA:b * H_NSA + 1]
            for h in range(1, H_NSA):
                row = row + pltpu.roll(g_t[b * H_NSA + h:b * H_NSA + h + 1], TOK_PAD * h, axis=1)
            gate_rows.append(row)
        o_col = oc_scr[...] * gate_rows[0] + o_s * gate_rows[1] + ow_scr[...] * gate_rows[2]
        per_head = [o_col if h == 0 else pltpu.roll(o_col, LANE - TOK_PAD * h, axis=1) for h in range(H_NSA)]
        o_ref[...] = jnp.concatenate(per_head, axis=0).T[:TOK_PAD]


def _nsa_sample(pool, page_table, q, gates, kc, vc, kv_new, win_buf, win_new, rel_table, n_valid):
    B, n_pages_total = page_table.shape
    past = n_pages_total * PAGE_SIZE
    ncp = kc.shape[1]
    ns = past // SEL_BLOCK + 1
    nsp = -(-ns // MASK_ROWS) * MASK_ROWS
    col = np.arange(LANE)
    used = col < SMP_COLS
    gsum = jnp.asarray(((col[:, None] // (G_NSA * TOK_PAD) == col[None, :] // (G_NSA * TOK_PAD))
                        & (col[:, None] % TOK_PAD == col[None, :] % TOK_PAD) & used[:, None] & used[None, :]), BF16)
    tab_cols = jnp.pad(jnp.repeat(rel_table, TOK_PAD, axis=1), ((0, 0), (0, LANE - SMP_COLS)))
    n_step = min(SMP_PAGES, n_pages_total)
    consts = [tab_cols, _pool_matrix(nsp, ncp), gsum, _sel_pattern(SLAB_PAGES * PAGE_SIZE, LANE).T]
    per_seq = [q, gates, kc, vc, kv_new, win_buf, win_new]
    seq_spec = lambda a: pl.BlockSpec((None,) + a.shape[1:], lambda b, j, pt: (b,) + (0,) * (a.ndim - 1))
    const = lambda a: pl.BlockSpec(a.shape, lambda b, j, pt: (0,) * a.ndim)

    def page_map(p):
        return lambda b, j, pt: (pt[b, j * n_step + p], 1, 0)

    grid_spec = pltpu.PrefetchScalarGridSpec(
        num_scalar_prefetch=1,
        grid=(B, n_pages_total // n_step),
        in_specs=[pl.BlockSpec((None, 2 * LANE, PAGE_SIZE), page_map(p)) for p in range(n_step)]
                 + [seq_spec(a) for a in per_seq] + [const(a) for a in consts],
        out_specs=pl.BlockSpec((None, TOK_PAD, H_NSA * HD), lambda b, j, pt: (b, 0, 0)),
        scratch_shapes=[pltpu.VMEM((nsp, LANE), F32), pltpu.VMEM((LANE, LANE), F32),
                        pltpu.VMEM((1, LANE), F32), pltpu.VMEM((1, LANE), F32), pltpu.VMEM((HD, LANE), F32), pltpu.VMEM((HD, LANE), F32),
                        pltpu.VMEM((SMP_COLS, LANE), BF16), pltpu.VMEM((nsp // MASK_ROWS, SMP_COLS, LANE), BF16),
                        pltpu.VMEM((SMP_COLS, 1), F32), pltpu.VMEM((SMP_COLS, LANE), F32),
                        pltpu.VMEM((SMP_COLS, 1), F32), pltpu.VMEM((SMP_COLS, 1), F32), pltpu.VMEM((SMP_COLS, LANE), F32)],
    )
    return pl.pallas_call(
        functools.partial(_nsa_sample_kernel, n_pages=n_step, n_valid=n_valid, past=past),
        grid_spec=grid_spec,
        out_shape=jax.ShapeDtypeStruct((B, TOK_PAD, H_NSA * HD), F32),
        compiler_params=_params(("arbitrary", "arbitrary")),
        name="nsa_sample",
    )(page_table, *([pool] * n_step), *per_seq, *consts)


def _outproj_kernel(x_ref, nsa_ref, rw_ref, gt_ref, lng_ref, lnb_ref, w_ref, o_ref):
    half = H_NSA * HD
    out = (jnp.dot(nsa_ref[...].astype(BF16), w_ref[0:half, :], preferred_element_type=F32)
           + jnp.dot(rw_ref[...].astype(BF16), w_ref[half:, :], preferred_element_type=F32))
    y = ALPHA * x_ref[...] + (1.0 + gt_ref[...]) * out
    o_ref[...] = _layer_norm(y, lng_ref[...], lnb_ref[...])


def _outproj(x, o_nsa, o_rwkv, gate, ln_g, ln_b, w_out):
    rows = x.shape[0]
    tm = min(512, rows)
    row = lambda i: (i, 0)
    return pl.pallas_call(
        _outproj_kernel,
        grid=(rows // tm,),
        in_specs=[pl.BlockSpec((tm, D_MODEL), row), pl.BlockSpec((tm, H_NSA * HD), row), pl.BlockSpec((tm, D_RWKV), row),
                  _mod_spec(gate, tm), _resident((1, D_MODEL)), _resident((1, D_MODEL)), _resident(w_out.shape)],
        out_specs=pl.BlockSpec((tm, D_MODEL), row),
        out_shape=jax.ShapeDtypeStruct((rows, D_MODEL), F32),
        compiler_params=_params(("arbitrary",)),
        name="outproj",
    )(x, o_nsa, o_rwkv, gate, ln_g.reshape(1, -1), ln_b.reshape(1, -1), w_out)


def kernel(x_prompt, x_sample, cache_nsa_kv, cache_nsa_win, state_rwkv_shift, state_rwkv_wkv, page_table, c_prompt, c_sample, rel_table, w_ada, b_ada, ln_g, ln_b, ffn1_gate, ffn1_up, ffn1_down, ffn2_gate, ffn2_up, ffn2_down, w_in, w_out, cmp_pe_k, cmp_w1_k, cmp_b1_k, cmp_w2_k, cmp_pe_v, cmp_w1_v, cmp_b1_v, cmp_w2_v, rwkv_mu, rwkv_w0, rwkv_w2, rwkv_a0, rwkv_a2, rwkv_g2, rwkv_k_k, rwkv_k_a, rwkv_r_k, rwkv_gn_w, rwkv_gn_b):
    assert w_ada.shape[0] == DEPTH == 1 and x_prompt.shape[0] == 1
    l = 0
    lw = dict(cmp_pe_k=cmp_pe_k[l], cmp_w1_k=cmp_w1_k[l], cmp_b1_k=cmp_b1_k[l], cmp_w2_k=cmp_w2_k[l],
              cmp_pe_v=cmp_pe_v[l], cmp_w1_v=cmp_w1_v[l], cmp_b1_v=cmp_b1_v[l], cmp_w2_v=cmp_w2_v[l],
              rwkv_mu=rwkv_mu[l], rwkv_w0=rwkv_w0[l], rwkv_w2=rwkv_w2[l], rwkv_a0=rwkv_a0[l], rwkv_a2=rwkv_a2[l], rwkv_g2=rwkv_g2[l],
              rwkv_k_k=rwkv_k_k[l], rwkv_k_a=rwkv_k_a[l], rwkv_r_k=rwkv_r_k[l], rwkv_gn_w=rwkv_gn_w[l], rwkv_gn_b=rwkv_gn_b[l])
    T = x_prompt.shape[1]
    nb, nt = x_sample.shape[0], x_sample.shape[1]
    assert nt <= TOK_PAD
    n_seq = 1 + nb
    c_all = jnp.concatenate([c_prompt, c_sample, jnp.zeros((-n_seq % 8, D_MODEL), F32)], axis=0)
    mod = _ada(c_all, w_ada[l], b_ada[l])
    mod_p = mod[0:1].reshape(9, 1, D_MODEL)
    mod_s = jnp.repeat(mod[1:n_seq].reshape(nb, 9, D_MODEL), nt, axis=0).transpose(1, 0, 2)
    ffn1 = [w[l].astype(BF16) for w in (ffn1_gate, ffn1_up, ffn1_down)]
    ffn2 = [w[l].astype(BF16) for w in (ffn2_gate, ffn2_up, ffn2_down)]
    w_in_p = _prep_w_in(w_in[l])
    w_out_b = w_out[l].astype(BF16)

    def trunk_in(x, m):
        x1 = _ffn(x, m[0], m[1], m[2], ln_g[l, 0], ln_b[l, 0], *ffn1)
        return x1, _proj(x1, m[3], m[4], w_in_p)

    def trunk_out(x1, o_nsa, o_rwkv, m):
        x2 = _outproj(x1, o_nsa, o_rwkv, m[5], ln_g[l, 1], ln_b[l, 1], w_out_b)
        return _ffn(x2, m[6], m[7], m[8], ln_g[l, 2], ln_b[l, 2], *ffn2)

    xp1, (q, kv, win, gates, pr) = trunk_in(x_prompt[0], mod_p)
    o_rw, wkv_p = _rwkv(pr[None], jnp.zeros((1, 1, RW_PAD), F32), jnp.zeros((1, H_RWKV, HD_RWKV, HD_RWKV), F32), lw, min(RW_STEP, T))
    n_rows = T // PAGE_SIZE
    kc, vc = _compress(kv.reshape(n_rows, PAGE_SIZE, 4 * LANE), jnp.arange(n_rows, dtype=jnp.int32)[None], lw, transposed=False)
    o_nsa = _nsa_prompt(q, gates, kv, win, kc[0], vc[0], _band(rel_table), rel_table)
    y_prompt = trunk_out(xp1, o_nsa, o_rw[0], mod_p)
    kv_prompt = kv.reshape(1, 1, T, 4, N_KV, HD)
    win_prompt = win[T - min(WINDOW, T):].reshape(1, 1, -1, 2, N_KV, HD)
    shift_prompt = _rwkv_uncols(pr[T - 1]).reshape(1, 1, RWKV_COLS)

    xs1, (q_s, kv_s, win_s, gates_s, pr_s) = trunk_in(x_sample.reshape(nb * nt, D_MODEL), mod_s)
    tokens = lambda a: jnp.pad(a.reshape(nb, nt, -1), ((0, 0), (0, TOK_PAD - nt), (0, 0)))
    pr_pad = jnp.pad(pr_s.reshape(nb, nt, -1), ((0, 0), (0, RW_TOK_PAD - nt), (0, 0)))
    o_rw_s, wkv_s = _rwkv(pr_pad, _rwkv_cols(state_rwkv_shift[l])[:, None], state_rwkv_wkv[l], lw, nt)
    pool_t = jnp.transpose(cache_nsa_kv[l], (0, 2, 3, 4, 1)).reshape(-1, 4 * LANE, PAGE_SIZE)
    kc_s, vc_s = _compress(pool_t, page_table, lw, transposed=True)
    win_buf = cache_nsa_win[l]
    win_t = jnp.transpose(win_buf, (0, 2, 3, 4, 1)).reshape(nb, 2 * LANE, -1)
    o_nsa_s = _nsa_sample(pool_t, page_table, tokens(q_s), tokens(gates_s), kc_s, vc_s, tokens(kv_s),
                          win_t, tokens(win_s), rel_table, nt)
    y_sample = trunk_out(xs1, o_nsa_s[:, :nt].reshape(nb * nt, -1), o_rw_s[:, :nt].reshape(nb * nt, -1), mod_s)
    kv_sample = kv_s.reshape(1, nb, nt, 4, N_KV, HD)
    win_sample = jnp.concatenate([win_buf, win_s.reshape(nb, nt, 2, N_KV, HD)], axis=1)[None, :, nt:]
    shift_sample = _rwkv_uncols(pr_s.reshape(nb, nt, -1)[:, -1])[None]
    return (y_prompt[None], y_sample.reshape(nb, nt, D_MODEL), kv_prompt, win_prompt, shift_prompt, wkv_p[None],
            kv_sample, win_sample, shift_sample, wkv_s[None])
```

```python
import functools
import math

import numpy as np
import jax
import jax.numpy as jnp
from jax import lax
from jax.experimental import pallas as pl
from jax.experimental.pallas import tpu as pltpu

D_MODEL = 1024
PAGE_SIZE = 128
H_NSA = 8
N_KV = 2
G_NSA = H_NSA // N_KV
HD = 64
CMP_STRIDE = 16
CMP_BLOCK = 2 * CMP_STRIDE
CMP_HIDDEN = 256
SEL_BLOCK = 64
N_SEL = 16
WINDOW = 512
Q_BLOCK = 128
N_BUCKETS = 32
MAX_DISTANCE = 128
H_RWKV = 8
HD_RWKV = 64
D_RWKV = H_RWKV * HD_RWKV
DECAY_LORA = 32
AAA_LORA = 32
GATE_LORA = 96
GN_EPS = 64e-5
D_FF = 2816
LN_EPS = 1e-5
DEPTH = 1
ALPHA = (2 * DEPTH) ** 0.25

NSA_SIZES = (H_NSA * HD,) + (N_KV * HD,) * 6 + (H_NSA * 3,)
RWKV_SIZES = (D_RWKV, D_RWKV, D_RWKV, DECAY_LORA, AAA_LORA, GATE_LORA)
NSA_COLS = sum(NSA_SIZES)
RWKV_COLS = sum(RWKV_SIZES)

F32 = jnp.float32
BF16 = jnp.bfloat16
LANE = 128
NEG = -(2.0 ** 100)
M_INIT = -(2.0 ** 103)
VMEM_LIMIT = 56 * 1024 * 1024

RW_PAD = 3 * D_RWKV + 3 * LANE
P_Q, P_KV, P_WIN, P_GATE, P_RW = 0, 512, 1024, 1280, 1408
P_COLS = P_RW + RW_PAD
KEY_TILE = 512
MASK_ROWS = 16
SEL_MASK0 = HD
SEL_FAR0 = SEL_MASK0 + MASK_ROWS
V_ROWS_KV = HD + MASK_ROWS
CMP_CLASS_ROWS = 256
Q_PER_TILE = KEY_TILE // Q_BLOCK
CMP_PER_Q = Q_BLOCK // CMP_STRIDE
EDGE_ROWS = 32


def _bucket_lows():
    d = np.arange(0, 4 * MAX_DISTANCE, dtype=np.int64)
    max_exact = N_BUCKETS // 2
    df = np.maximum(d, 1).astype(np.float32)
    large = max_exact + (np.log(df / np.float32(max_exact)) / np.float32(math.log(MAX_DISTANCE / max_exact))
                         * np.float32(N_BUCKETS - max_exact)).astype(np.int32)
    b = np.where(d < max_exact, d, np.minimum(large, N_BUCKETS - 1))
    lows = [int(np.argmax(b >= k)) for k in range(N_BUCKETS)]
    return b, lows


_BUCKET_OF, _BUCKET_LOW = _bucket_lows()
FAR_DIST = _BUCKET_LOW[N_BUCKETS - 1]


def _resident(shape):
    nd = len(shape)
    return pl.BlockSpec(shape, lambda *_: (0,) * nd, pipeline_mode=pl.Buffered(1))


def _params(sem):
    return pltpu.CompilerParams(dimension_semantics=sem, vmem_limit_bytes=VMEM_LIMIT)


def _dot_exact_rhs(x, rhs_bf16, terms=2):
    acc = None
    rem = x
    for _ in range(terms):
        part = rem.astype(BF16)
        d = jnp.dot(part, rhs_bf16, preferred_element_type=F32)
        acc = d if acc is None else acc + d
        rem = rem - part.astype(F32)
    return acc


def _dot_exact_lhs(lhs_bf16, x, terms=3):
    acc = None
    rem = x
    for _ in range(terms):
        part = rem.astype(BF16)
        d = jnp.dot(lhs_bf16, part, preferred_element_type=F32)
        acc = d if acc is None else acc + d
        rem = rem - part.astype(F32)
    return acc


def _layer_norm(y, g, b):
    mu = jnp.mean(y, axis=-1, keepdims=True)
    yc = y - mu
    var = jnp.mean(yc * yc, axis=-1, keepdims=True)
    return yc * lax.rsqrt(var + LN_EPS) * g + b


def _bias_chain(d, tab_rows):
    out = tab_rows[0] + jnp.zeros(d.shape, F32)
    for b in range(1, N_BUCKETS):
        out = jnp.where(d >= _BUCKET_LOW[b], tab_rows[b], out)
    return out


def _ada_kernel(c_ref, w_ref, b_ref, o_ref):
    c = c_ref[...]
    h = (c * jax.nn.sigmoid(c)).astype(BF16)
    o_ref[...] = jnp.dot(h, w_ref[...].astype(BF16), preferred_element_type=F32) + b_ref[...]


def _ada(c_all, w_ada, b_ada):
    rows, n = c_all.shape[0], w_ada.shape[1]
    tn = 1152
    return pl.pallas_call(
        _ada_kernel,
        grid=(n // tn,),
        in_specs=[pl.BlockSpec((rows, D_MODEL), lambda j: (0, 0)),
                  pl.BlockSpec((D_MODEL, tn), lambda j: (0, j)),
                  pl.BlockSpec((1, tn), lambda j: (0, j))],
        out_specs=pl.BlockSpec((rows, tn), lambda j: (0, j)),
        out_shape=jax.ShapeDtypeStruct((rows, n), F32),
        compiler_params=_params(("arbitrary",)),
        name="ada",
    )(c_all, w_ada, b_ada.reshape(1, n))


FF_CHUNKS = 2


def _ffn_kernel(x_ref, sh_ref, sc_ref, gt_ref, lng_ref, lnb_ref, wg_ref, wu_ref, wd_ref, o_ref):
    x = x_ref[...]
    h = (x * (1.0 + sc_ref[...]) + sh_ref[...]).astype(BF16)
    ck = D_FF // FF_CHUNKS
    acc = jnp.zeros(x.shape, F32)
    for c in range(FF_CHUNKS):
        a = jnp.dot(h, wg_ref[:, c * ck:(c + 1) * ck], preferred_element_type=F32)
        b = jnp.dot(h, wu_ref[:, c * ck:(c + 1) * ck], preferred_element_type=F32)
        t = (a * jax.nn.sigmoid(a) * b).astype(BF16)
        acc = acc + jnp.dot(t, wd_ref[c * ck:(c + 1) * ck, :], preferred_element_type=F32)
    y = ALPHA * x + (1.0 + gt_ref[...]) * (0.5 * acc)
    o_ref[...] = _layer_norm(y, lng_ref[...], lnb_ref[...])


def _mod_spec(mod, tm):
    if mod.shape[0] == 1:
        return pl.BlockSpec((1, D_MODEL), lambda i: (0, 0))
    return pl.BlockSpec((tm, D_MODEL), lambda i: (i, 0))


def _ffn(x, shift, scale, gate, ln_g, ln_b, wg, wu, wd):
    rows = x.shape[0]
    tm = min(512, rows)
    row = lambda i: (i, 0)
    return pl.pallas_call(
        _ffn_kernel,
        grid=(rows // tm,),
        in_specs=[pl.BlockSpec((tm, D_MODEL), row), _mod_spec(shift, tm), _mod_spec(scale, tm), _mod_spec(gate, tm),
                  _resident((1, D_MODEL)), _resident((1, D_MODEL)),
                  _resident((D_MODEL, D_FF)), _resident((D_MODEL, D_FF)), _resident((D_FF, D_MODEL))],
        out_specs=pl.BlockSpec((tm, D_MODEL), row),
        out_shape=jax.ShapeDtypeStruct((rows, D_MODEL), F32),
        compiler_params=_params(("arbitrary",)),
        name="ffn",
    )(x, shift, scale, gate, ln_g.reshape(1, -1), ln_b.reshape(1, -1), wg, wu, wd)


def _proj_kernel(x_ref, sh_ref, sc_ref, w_ref, q_ref, kv_ref, win_ref, g_ref, pr_ref):
    h = (x_ref[...] * (1.0 + sc_ref[...]) + sh_ref[...]).astype(BF16)
    p = jnp.dot(h, w_ref[...], preferred_element_type=F32)
    q_ref[...] = p[:, P_Q:P_KV]
    kv_ref[...] = p[:, P_KV:P_WIN]
    win_ref[...] = p[:, P_WIN:P_GATE]
    g_ref[...] = jax.nn.sigmoid(p[:, P_GATE:P_RW])
    pr_ref[...] = p[:, P_RW:P_COLS]


def _proj(x, shift, scale, w_in_p):
    rows = x.shape[0]
    tm = min(512, rows)
    row = lambda i: (i, 0)
    widths = (512, 512, 256, LANE, RW_PAD)
    return pl.pallas_call(
        _proj_kernel,
        grid=(rows // tm,),
        in_specs=[pl.BlockSpec((tm, D_MODEL), row), _mod_spec(shift, tm), _mod_spec(scale, tm),
                  _resident((D_MODEL, P_COLS))],
        out_specs=[pl.BlockSpec((tm, w), row) for w in widths],
        out_shape=[jax.ShapeDtypeStruct((rows, w), F32) for w in widths],
        compiler_params=_params(("arbitrary",)),
        name="proj",
    )(x, shift, scale, w_in_p)


def _prep_w_in(w_in):
    pad = lambda a, n: jnp.pad(a, ((0, 0), (0, n - a.shape[1])))
    nsa, rw = w_in[:, :NSA_COLS], w_in[:, NSA_COLS:]
    gl = nsa[:, 1280:1304].reshape(D_MODEL, H_NSA, 3).transpose(0, 2, 1).reshape(D_MODEL, 3 * H_NSA)
    cols = [nsa[:, :1280], pad(gl, LANE), _rwkv_cols(rw)]
    return jnp.concatenate(cols, axis=1).astype(BF16)


def _rwkv_cols(a):
    pad = lambda t: jnp.pad(t, [(0, 0)] * (t.ndim - 1) + [(0, LANE - t.shape[-1])])
    n = 3 * D_RWKV
    return jnp.concatenate([a[..., :n], pad(a[..., n:n + 32]), pad(a[..., n + 32:n + 64]), pad(a[..., n + 64:n + 160])], axis=-1)


def _rwkv_uncols(a):
    n = 3 * D_RWKV
    return jnp.concatenate([a[..., :n], a[..., n:n + 32], a[..., n + LANE:n + LANE + 32], a[..., n + 2 * LANE:n + 2 * LANE + 96]], axis=-1)


RW_GROUP = 64
RW_TOK_PAD = 16
RW_PAIRS = H_RWKV // 2
RW_BLOCK = 64
RW_STEP = 256


def _lora(x, w_ref):
    w = w_ref[...]
    w_hi = w.astype(BF16)
    w_lo = (w - w_hi.astype(F32)).astype(BF16)
    return _dot_exact_rhs(x, w_hi) + jnp.dot(x.astype(BF16), w_lo, preferred_element_type=F32)


def _rwkv_kernel(pr_ref, sh0_ref, s0_ref, mu_ref, w0_ref, a0_ref, kk_ref, ka_ref, rk_ref, gw_ref, gb_ref,
                 w2_ref, a2_ref, g2_ref, bo_ref, lgrp_ref, ggrp_ref, o_ref, sout_ref,
                 prev_scr, s_scr, *, n_valid):
    tb = pr_ref.shape[0]
    step = pl.program_id(1)

    @pl.when(step == 0)
    def _():
        prev_scr[...] = sh0_ref[...]
        s_scr[...] = s0_ref[...]

    p = pr_ref[...]
    rows = lax.broadcasted_iota(jnp.int32, (tb, 1), 0)
    prev = jnp.where(rows == 0, prev_scr[...], pltpu.roll(p, 1, axis=0))
    prev_scr[...] = p[tb - 1:tb, :]
    xs = p + (prev - p) * mu_ref[...]
    n = D_RWKV
    r, k, v = xs[:, :n], xs[:, n:2 * n], xs[:, 2 * n:3 * n]

    def head_sums(x):
        return jnp.concatenate([_dot_exact_rhs(x[:, :n // 2], bo_ref[...]), _dot_exact_rhs(x[:, n // 2:], bo_ref[...])], axis=1)

    wl, al, gl = xs[:, 3 * n:3 * n + LANE], xs[:, 3 * n + LANE:3 * n + 2 * LANE], xs[:, 3 * n + 2 * LANE:]
    z = -(w0_ref[...] + _lora(jnp.tanh(wl), w2_ref))
    w = -(jnp.maximum(z, 0.0) + jnp.log(1.0 + jnp.exp(-jnp.abs(z)))) - 0.5
    a = jax.nn.sigmoid(a0_ref[...] + _lora(al, a2_ref))
    g = _lora(jax.nn.sigmoid(gl), g2_ref)
    kk = k * kk_ref[...]
    ss = head_sums(kk * kk)
    kk = kk / jnp.maximum(jnp.sqrt(ss), 1e-12)
    k2 = k * (1.0 + (a - 1.0) * ka_ref[...])
    G = min(RW_GROUP, tb)
    log_dec = -jnp.exp(w)
    bet = kk * a
    if n_valid < tb:
        live = rows < n_valid
        log_dec, kk, bet, k2, v_in = (jnp.where(live, x, 0.0) for x in (log_dec, kk, bet, k2, v))
    else:
        v_in = v
    cum = _dot_exact_lhs(lgrp_ref[...], log_dec, terms=2)
    cum_end = _dot_exact_lhs(ggrp_ref[...], log_dec, terms=2)
    gam_inv = jnp.exp(-cum)
    gam_end = jnp.exp(cum_end - cum)
    k_hat = -kk * jnp.exp(cum - log_dec)
    r_hat = r * jnp.exp(cum)
    b_chk, k_chk = bet * gam_inv, k2 * gam_inv
    b_til, k_til = bet * gam_end, k2 * gam_end
    gam_group = jnp.exp(cum_end)

    lane = lax.broadcasted_iota(jnp.int32, (1, LANE), 1)
    low = lane < HD_RWKV
    lane_t = lane & (RW_BLOCK - 1)
    row = lax.broadcasted_iota(jnp.int32, (LANE, 1), 0)
    row_t = row & (RW_BLOCK - 1)
    same = ((row < HD_RWKV) == low) & ((row_t // G) == (lane_t // G))
    strict, incl = same & (lane_t < row_t), same & (lane_t <= row_t)
    bf = lambda x: x.astype(BF16)
    mm = lambda x, y: jnp.dot(bf(x), bf(y), preferred_element_type=F32)
    mm_nt = lambda x, y: lax.dot_general(bf(x), bf(y), (((1,), (1,)), ((), ())), preferred_element_type=F32)

    def rows_bd(x):
        if tb < RW_BLOCK:
            x = jnp.concatenate([x, jnp.zeros((RW_BLOCK - tb, LANE), F32)], axis=0)
        return jnp.concatenate([jnp.where(low, x, 0.0), jnp.where(low, 0.0, x)], axis=0)

    def mm3(x, y):
        xh, yh = bf(x), bf(y)
        xl, yl = bf(x - xh.astype(F32)), bf(y - yh.astype(F32))
        return jnp.dot(jnp.concatenate([xh, xl, xh], axis=1), jnp.concatenate([yh, yh, yl], axis=0), preferred_element_type=F32)

    units = max(1, tb // RW_BLOCK)
    unit_rows = min(tb, RW_BLOCK)
    pairs = range(RW_PAIRS)
    items = [(un, pp) for un in range(units) for pp in pairs]
    at = lambda x, it: x[it[0] * unit_rows:(it[0] + 1) * unit_rows, it[1] * LANE:(it[1] + 1) * LANE]
    kh_row = [rows_bd(at(k_hat, it)) for it in items]
    rh_row = [rows_bd(at(r_hat, it)) for it in items]
    kh_mat = [x.T for x in kh_row]
    rh_mat = [x.T for x in rh_row]
    state_in = [jnp.concatenate([rows_bd(at(b_chk, it)).T, rows_bd(at(k_chk, it)).T], axis=1) for it in items]
    upd_rows = [jnp.concatenate([rows_bd(at(b_til, it)), rows_bd(at(k_til, it))], axis=0) for it in items]
    v_t = [rows_bd(at(v_in, it)).T for it in items]
    v_t = [x[:HD_RWKV] + x[HD_RWKV:] for x in v_t]
    c_all = [mm(jnp.concatenate([kh_row[n], rh_row[n]], axis=0), state_in[n]) for n in range(len(items))]
    c_uu = [jnp.where(strict, c[:LANE, :LANE], 0.0) for c in c_all]
    c_uv = [jnp.where(strict, c[:LANE, LANE:], 0.0) for c in c_all]
    c_ru = [jnp.where(incl, c[LANE:, :LANE], 0.0) for c in c_all]
    c_rv = [jnp.where(incl, c[LANE:, LANE:], 0.0) for c in c_all]
    t_neu, power = list(c_uu), list(c_uu)
    span = 2
    while span < G:
        power = [mm(x, x) for x in power]
        t_neu = [t_neu[n] + power[n] + mm(t_neu[n], power[n]) for n in range(len(items))]
        span *= 2
    from_v = [mm_nt(v_t[n], c_uv[n]) for n in range(len(items))]
    st = [s_scr[pp] for pp in pairs]
    y_units = []
    for un in range(units):
        ns = [un * RW_PAIRS + pp for pp in pairs]
        y_t = [jnp.zeros((HD_RWKV, LANE), F32) for _ in pairs]
        for grp in range(unit_rows // G):
            here = (lane_t // G) == grp
            first = un * unit_rows + grp * G
            w_t = [jnp.where(here, mm(st[pp], kh_mat[ns[pp]]) + from_v[ns[pp]], 0.0) for pp in pairs]
            u_t = [w_t[pp] + mm_nt(w_t[pp], t_neu[ns[pp]]) for pp in pairs]
            v_g = [jnp.where(here, v_t[ns[pp]], 0.0) for pp in pairs]
            y_t = [y_t[pp] + jnp.where(here, mm(st[pp], rh_mat[ns[pp]]), 0.0) + mm_nt(u_t[pp], c_ru[ns[pp]])
                   + mm_nt(v_g[pp], c_rv[ns[pp]]) for pp in pairs]
            st = [st[pp] * gam_group[first:first + 1, pp * LANE:(pp + 1) * LANE]
                  + mm3(jnp.concatenate([u_t[pp], v_g[pp]], axis=1), upd_rows[ns[pp]]) for pp in pairs]
        lane_u = lax.broadcasted_iota(jnp.int32, (unit_rows, LANE), 1)
        pieces = []
        for pp in pairs:
            yt = jnp.concatenate([y_t[pp], jnp.zeros((LANE - HD_RWKV, LANE), F32)], axis=0).T
            pieces.append(jnp.where(lane_u < HD_RWKV, yt[:unit_rows], pltpu.roll(yt[RW_BLOCK:RW_BLOCK + unit_rows], HD_RWKV, axis=1)))
        y_units.append(jnp.concatenate(pieces, axis=1))
    for pp in pairs:
        s_scr[pp] = st[pp]
    y = y_units[0] if units == 1 else jnp.concatenate(y_units, axis=0)
    mean = head_sums(y) * (1.0 / HD_RWKV)
    yc = y - mean
    var = head_sums(yc * yc) * (1.0 / HD_RWKV)
    yn = yc * lax.rsqrt(var + GN_EPS) * gw_ref[...] + gb_ref[...]
    bonus = head_sums(r * k2 * rk_ref[...]) * v
    o_ref[...] = (yn + bonus) * g
    sout_ref[...] = s_scr[...]


def _pair_state(s):
    B = s.shape[0]
    return s.reshape(B, RW_PAIRS, 2, HD_RWKV, HD_RWKV).transpose(0, 1, 3, 2, 4).reshape(B, RW_PAIRS, HD_RWKV, LANE)


def _unpair_state(s):
    B = s.shape[0]
    return s.reshape(B, RW_PAIRS, HD_RWKV, 2, HD_RWKV).transpose(0, 1, 3, 2, 4).reshape(B, H_RWKV, HD_RWKV, HD_RWKV)


def _rwkv(pr, shift0, s0, lw, n_valid):
    B, T, _ = pr.shape
    tb = min(RW_STEP, T)
    n = D_RWKV
    vec = lambda a: a.reshape(1, n)
    padrow = lambda a: jnp.pad(a, ((0, LANE - a.shape[0]), (0, 0)))
    blk = np.arange(n // 2) // HD_RWKV
    block_ones = jnp.asarray(blk[:, None] == blk[None, :], BF16)
    tok = np.arange(tb)
    group = min(RW_GROUP, tb)
    same_group = tok[:, None] // group == tok[None, :] // group
    prefix = jnp.asarray(same_group & (tok[None, :] <= tok[:, None]), BF16)
    consts = [_rwkv_cols(lw['rwkv_mu']).reshape(1, RW_PAD), vec(lw['rwkv_w0']), vec(lw['rwkv_a0']), vec(lw['rwkv_k_k']),
              vec(lw['rwkv_k_a']), vec(lw['rwkv_r_k']), vec(lw['rwkv_gn_w']), vec(lw['rwkv_gn_b']),
              padrow(lw['rwkv_w2']), padrow(lw['rwkv_a2']), padrow(lw['rwkv_g2']), block_ones, prefix, jnp.asarray(same_group, BF16)]
    kern = functools.partial(_rwkv_kernel, n_valid=n_valid)
    state_spec = pl.BlockSpec((None, RW_PAIRS, HD_RWKV, LANE), lambda b, j: (b, 0, 0, 0))
    o, s = pl.pallas_call(
        kern,
        grid=(B, T // tb),
        in_specs=[pl.BlockSpec((None, tb, RW_PAD), lambda b, j: (b, j, 0)),
                  pl.BlockSpec((None, 1, RW_PAD), lambda b, j: (b, 0, 0)), state_spec]
                 + [_resident(c.shape) for c in consts],
        out_specs=[pl.BlockSpec((None, tb, n), lambda b, j: (b, j, 0)), state_spec],
        out_shape=[jax.ShapeDtypeStruct((B, T, n), F32), jax.ShapeDtypeStruct((B, RW_PAIRS, HD_RWKV, LANE), F32)],
        scratch_shapes=[pltpu.VMEM((1, RW_PAD), F32), pltpu.VMEM((RW_PAIRS, HD_RWKV, LANE), F32)],
        compiler_params=_params(("arbitrary", "arbitrary")),
        name="rwkv",
    )(pr, shift0, _pair_state(s0), *consts)
    return o, _unpair_state(s)


CMP_PAGES = 64
CHUNKS_PER_PAGE = PAGE_SIZE // CMP_STRIDE


def _compress_kernel(pt_ref, *refs, n_pages, transposed):
    pages, nxt = refs[:n_pages], refs[n_pages]
    weights = refs[n_pages + 1:n_pages + 9]
    outs = refs[n_pages + 9:n_pages + 11]
    width = N_KV * HD
    rows = CHUNKS_PER_PAGE * n_pages
    seg = rows + 8
    kinds = range(2)
    low = lax.broadcasted_iota(jnp.int32, (1, N_KV * HD), 1) < HD
    rows_scr = refs[n_pages + 11:n_pages + 13]

    def by_head(row_s):
        heads = [[], []]
        for s in range(0, CMP_STRIDE, 2):
            a, b = row_s(s), row_s(s + 1)
            heads[0].append(jnp.where(low, a, pltpu.roll(b, HD, axis=1)))
            heads[1].append(jnp.where(low, pltpu.roll(a, HD, axis=1), b))
        return [jnp.concatenate(h, axis=1) for h in heads]

    for kind in kinds:
        pe_ref, w_ref, b_ref, w2_ref = weights[4 * kind:4 * kind + 4]
        part = slice(kind * width, (kind + 1) * width)
        for p, pg in enumerate(pages):
            rows_scr[kind][p * PAGE_SIZE:(p + 1) * PAGE_SIZE, :] = pg[part, :].T if transposed else pg[:, part]
        nxt_rows = nxt[part, :].T[:CMP_STRIDE] if transposed else nxt[:, part]
        x = by_head(lambda s: rows_scr[kind][pl.ds(s, rows, stride=CMP_STRIDE), :])
        x_next = by_head(lambda s: jnp.broadcast_to(nxt_rows[s:s + 1, :], (8, N_KV * HD)))
        x_all = jnp.concatenate([x[0], x_next[0], x[1], x_next[1]], axis=0)
        h_first = jnp.dot((x_all + pe_ref[0]).astype(BF16), w_ref[0], preferred_element_type=F32)
        h_second = jnp.dot((x_all + pe_ref[1]).astype(BF16), w_ref[1], preferred_element_type=F32)
        out = None
        for h in range(N_KV):
            h_next = pltpu.roll(h_second[h * seg:(h + 1) * seg], seg - 1, axis=0)[:rows]
            hidden = jax.nn.gelu(h_first[h * seg:h * seg + rows] + h_next + b_ref[...])
            part = jnp.dot(hidden.astype(BF16), w2_ref[h], preferred_element_type=F32)
            out = part if out is None else out + part
        outs[kind][...] = out


def _compress_weights(pe, w1, b1, w2):
    n = CMP_STRIDE * HD
    pe2 = pe.reshape(2, 1, n)
    w_halves = w1.reshape(2, n, CMP_HIDDEN).astype(BF16)
    zero = jnp.zeros_like(w2)
    w2_heads = jnp.stack([jnp.concatenate([w2, zero], axis=1), jnp.concatenate([zero, w2], axis=1)]).astype(BF16)
    return [pe2, w_halves, b1.reshape(1, -1), w2_heads]


def _compress(pool, page_table, lw, transposed):
    B, n_pages_total = page_table.shape
    n_pages = min(CMP_PAGES, n_pages_total)
    rows = CHUNKS_PER_PAGE * n_pages
    weights = (_compress_weights(lw['cmp_pe_k'], lw['cmp_w1_k'], lw['cmp_b1_k'], lw['cmp_w2_k'])
               + _compress_weights(lw['cmp_pe_v'], lw['cmp_w1_v'], lw['cmp_b1_v'], lw['cmp_w2_v']))
    width = N_KV * HD

    def page_map(p):
        return lambda b, j, pt: (pt[b, j * n_pages + p], 0, 0)

    next_map = lambda b, j, pt: (pt[b, jnp.minimum((j + 1) * n_pages, n_pages_total - 1)], 0, 0)
    page_block = (None, 2 * width, PAGE_SIZE) if transposed else (None, PAGE_SIZE, 2 * width)
    next_block = page_block if transposed else (None, CMP_STRIDE, 2 * width)
    const = lambda a: pl.BlockSpec(a.shape, lambda b, j, pt: (0,) * a.ndim)
    out_spec = pl.BlockSpec((None, rows, N_KV * HD), lambda b, j, pt: (b, j, 0))
    out_shape = jax.ShapeDtypeStruct((B, n_pages_total * CHUNKS_PER_PAGE, N_KV * HD), F32)
    grid_spec = pltpu.PrefetchScalarGridSpec(
        num_scalar_prefetch=1,
        grid=(B, n_pages_total // n_pages),
        in_specs=[pl.BlockSpec(page_block, page_map(p)) for p in range(n_pages)]
                 + [pl.BlockSpec(next_block, next_map)] + [const(a) for a in weights],
        out_specs=[out_spec, out_spec],
        scratch_shapes=[pltpu.VMEM((n_pages * PAGE_SIZE, width), F32)] * 2,
    )
    return pl.pallas_call(
        functools.partial(_compress_kernel, n_pages=n_pages, transposed=transposed),
        grid_spec=grid_spec,
        out_shape=[out_shape, out_shape],
        compiler_params=_params(("arbitrary", "arbitrary")),
        name="compress",
    )(page_table, *([pool] * (n_pages + 1)), *weights)


BAND_ROWS = 1152


def _band_kernel(tab_ref, bkt_ref, o_ref):
    h = pl.program_id(0)
    bkt = bkt_ref[...]
    out = jnp.full(bkt.shape, NEG, F32)
    for b in range(N_BUCKETS):
        out = jnp.where(bkt == b, tab_ref[b, h], out)
    o_ref[...] = out


def _band(rel_table):
    u = np.arange(BAND_ROWS)[:, None]
    qi = np.arange(Q_BLOCK)[None, :]
    d = qi + WINDOW - u
    bkt = np.where(d >= 0, _BUCKET_OF[np.clip(d, 0, len(_BUCKET_OF) - 1)], -1).astype(np.int32)
    return pl.pallas_call(
        _band_kernel,
        grid=(H_NSA,),
        in_specs=[pl.BlockSpec(memory_space=pltpu.SMEM), pl.BlockSpec((BAND_ROWS, Q_BLOCK), lambda h: (0, 0))],
        out_specs=pl.BlockSpec((None, BAND_ROWS, Q_BLOCK), lambda h: (h, 0, 0)),
        out_shape=jax.ShapeDtypeStruct((H_NSA, BAND_ROWS, Q_BLOCK), F32),
        compiler_params=_params(("arbitrary",)),
        name="band",
    )(rel_table, jnp.asarray(bkt))


def _softmax_cols(s):
    m = jnp.max(s, axis=0, keepdims=True)
    e = jnp.exp(s - m)
    l = jnp.sum(e, axis=0, keepdims=True)
    return e * jnp.where(m > 0.5 * NEG, 1.0 / l, 0.0)


def _select_blocks(impsel, qpos, n_pick):
    ns = impsel.shape[0]
    blk = lax.broadcasted_iota(jnp.int32, impsel.shape, 0)
    cur = jnp.right_shift(qpos, SEL_BLOCK.bit_length() - 1)
    future = blk * SEL_BLOCK > qpos
    forced = (blk == 0) | (blk == cur) | (blk == cur - 1)
    score = jnp.where(future, -jnp.inf, jnp.where(forced, jnp.inf, impsel))
    chosen = jnp.zeros(impsel.shape, F32)
    for _ in range(n_pick):
        best = jnp.max(score, axis=0, keepdims=True)
        first = jnp.min(jnp.where(score == best, blk, ns), axis=0, keepdims=True)
        hit = (blk == first) & (best > -jnp.inf)
        chosen = jnp.where(hit, 1.0, chosen)
        score = jnp.where(hit, -jnp.inf, score)
    return jnp.where(chosen > 0.0, 0.0, NEG)


def _pool_matrix(ns, nc):
    j = np.arange(ns)[:, None]
    n = np.arange(nc)[None, :]
    ratio = SEL_BLOCK // CMP_STRIDE
    return jnp.asarray((n >= ratio * j - 1) & (n <= ratio * j + ratio - 1), BF16)


def _nsa_prompt_kernel(tab_ref, q_ref, g_ref, kc_ref, vct_ref, ks_ref, vst_ref, kw_ref, vwt_ref, band_ref, pool_ref, o_ref,
                       rhs_scr, mask_scr, acc_scr, m_scr, sc_scr, sa_scr, sb_scr, oc_scr):
    i = pl.program_id(0)
    ncp = kc_ref.shape[0]
    ns = pool_ref.shape[0]
    s0 = i * Q_BLOCK
    q_t = (q_ref[...] * HD ** -0.5).T
    g_t = g_ref[...].T
    lane_q = lax.broadcasted_iota(jnp.int32, (1, G_NSA * Q_BLOCK), 1) & (Q_BLOCK - 1)
    qpos = s0 + lax.broadcasted_iota(jnp.int32, (1, Q_BLOCK), 1)
    rhs_scr[...] = jnp.zeros(rhs_scr.shape, BF16)
    kvs = range(N_KV)
    kd = i // Q_PER_TILE
    r = i % Q_PER_TILE
    lanes4 = lambda k, f: jnp.concatenate([f(G_NSA * k + g) for g in range(G_NSA)], axis=1)
    qcols = [lanes4(k, lambda h: q_t[h * HD:(h + 1) * HD, :]) for k in kvs]
    zero = jnp.zeros_like(qcols[0])
    top = [jnp.concatenate([qcols[0], zero], axis=0).astype(BF16), jnp.concatenate([zero, qcols[1]], axis=0).astype(BF16)]
    far_row = [lanes4(k, lambda h: band_ref[h, 0:1, :]) for k in kvs]


    n0 = pl.multiple_of(jnp.clip(CMP_PER_Q * i - EDGE_ROWS // 2, 0, ncp - EDGE_ROWS), 8)
    nrow = lax.broadcasted_iota(jnp.int32, (ncp, 1), 0)
    d_edge = qpos - (CMP_STRIDE * (n0 + lax.broadcasted_iota(jnp.int32, (EDGE_ROWS, 1), 0)) + CMP_BLOCK - 1)

    def compressed(k, rows):
        sc_scr[k, 0:rows, :] = (jnp.dot(kc_ref[0:rows, :], top[k], preferred_element_type=F32)
                                + jnp.where(nrow[0:rows] < n0, far_row[k], NEG))
        edge_bias = lanes4(k, lambda h: jnp.where(d_edge >= 0, _bias_chain(d_edge, [tab_ref[b, h] for b in range(N_BUCKETS)]), NEG))
        sc_scr[k, pl.ds(n0, EDGE_ROWS), :] = jnp.dot(kc_ref[pl.ds(n0, EDGE_ROWS), :], top[k], preferred_element_type=F32) + edge_bias
        p_c = _softmax_cols(sc_scr[k, 0:rows, :])
        o_c = jnp.dot(vct_ref[:, 0:rows], p_c.astype(BF16), preferred_element_type=F32)[k * HD:(k + 1) * HD]
        imp = p_c[:, 0:Q_BLOCK]
        for g in range(1, G_NSA):
            imp = imp + p_c[:, g * Q_BLOCK:(g + 1) * Q_BLOCK]
        n_blk = rows * CMP_STRIDE // SEL_BLOCK
        return o_c, _dot_exact_lhs(pool_ref[0:n_blk, 0:rows], imp, terms=2)

    size_step = min(CMP_CLASS_ROWS, ncp)
    size_class = (n0 + EDGE_ROWS - 1) // size_step
    for cls in range(ncp // size_step):
        @pl.when(size_class == cls)
        def _(rows=(cls + 1) * size_step):
            comp = [compressed(k, rows) for k in kvs]
            n_blk = comp[0][1].shape[0]
            masks = [_select_blocks(comp[k][1], qpos, min(N_SEL, ns)) for k in kvs]
            for k in kvs:
                oc_scr[k] = comp[k][0]
                full = jnp.concatenate([masks[k], jnp.full((ns - n_blk, Q_BLOCK), NEG, F32)], axis=0) if n_blk < ns else masks[k]
                mask_scr[k] = jnp.concatenate([full.astype(BF16)] * G_NSA, axis=1)

    ws = pl.multiple_of(jnp.maximum(s0 - WINDOW, 0), Q_BLOCK)
    u0 = pl.multiple_of(WINDOW - (s0 - ws), Q_BLOCK)
    n_win = WINDOW + Q_BLOCK
    u = u0 + lax.broadcasted_iota(jnp.int32, (n_win, 1), 0)
    win_mask = jnp.where(u > lane_q, 0.0, NEG)

    def window(k):
        s_w = (jnp.dot(kw_ref[pl.ds(ws, n_win), :], top[k], preferred_element_type=F32)
               + lanes4(k, lambda h: band_ref[h, pl.ds(u0, n_win), :]) + win_mask)
        m_w = jnp.max(s_w, axis=0, keepdims=True)
        p_w = jnp.exp(s_w - m_w).astype(BF16)
        acc_w = jnp.zeros((V_ROWS_KV, G_NSA * Q_BLOCK), F32)
        for j in range(n_win // Q_BLOCK):
            acc_w = acc_w + jnp.dot(vwt_ref[k, ws // Q_BLOCK + j], p_w[j * Q_BLOCK:(j + 1) * Q_BLOCK], preferred_element_type=F32)
        return acc_w[0:HD] / acc_w[HD:HD + 1]

    o_w = [window(k) for k in kvs]

    for k in kvs:
        m_scr[k] = jnp.full(m_scr.shape[1:], M_INIT, F32)
        acc_scr[k] = jnp.zeros(acc_scr.shape[1:], F32)
        far_hi = far_row[k].astype(BF16).astype(F32)
        rhs_scr[k, 0:HD, :] = qcols[k].astype(BF16)
        rhs_scr[k, SEL_FAR0:SEL_FAR0 + MASK_ROWS, :] = jnp.concatenate(
            [far_hi, far_row[k] - far_hi, jnp.zeros((MASK_ROWS - 2, G_NSA * Q_BLOCK), F32)], axis=0).astype(BF16)

    def scores(k, slab, kts, extra):
        rhs_scr[k, SEL_MASK0:SEL_MASK0 + MASK_ROWS, :] = mask_scr[k, pl.ds(pl.multiple_of(slab * MASK_ROWS, MASK_ROWS), MASK_ROWS), :]
        rhs = rhs_scr[k]
        out = []
        for kt, add in zip(kts, extra):
            s = jnp.dot(ks_ref[k, pl.ds(pl.multiple_of(kt * KEY_TILE, KEY_TILE), KEY_TILE), :], rhs, preferred_element_type=F32)
            out.append(s if add is None else s + add)
        return out

    def update(k, kts, tiles):
        m_old = m_scr[k]
        m_new = m_old
        for s in tiles:
            m_new = jnp.maximum(m_new, jnp.max(s, axis=0, keepdims=True))
        acc = jnp.exp(m_old - m_new) * acc_scr[k]
        for kt, s in zip(kts, tiles):
            acc = acc + jnp.dot(vst_ref[k, kt], jnp.exp(s - m_new).astype(BF16), preferred_element_type=F32)
        acc_scr[k] = acc
        m_scr[k] = m_new

    def attend(slab, kts, extra):
        tiles = [scores(k, slab, kts, extra(k)) for k in kvs]
        for k in kvs:
            update(k, kts, tiles[k])

    near_at = lambda k, start: lanes4(k, lambda h: band_ref[h, pl.ds(pl.multiple_of(start, Q_BLOCK), KEY_TILE), :]) - far_row[k]
    prev_near = (r == 0) & (kd >= 1)
    kd_odd = (kd & 1) == 1
    even_prev = jnp.logical_not(kd_odd) & prev_near
    n_pairs = kd // 2 - even_prev.astype(jnp.int32)
    n_quads = n_pairs // 2
    no_bias = lambda k: [None, None]

    def pair_scores(k, dst, pair):
        lo, hi = scores(k, pair, [2 * pair, 2 * pair + 1], [None, None])
        dst[k, 0:KEY_TILE, :] = lo
        dst[k, KEY_TILE:, :] = hi

    def pair_update(k, src, pair):
        update(k, [2 * pair, 2 * pair + 1], [src[k, 0:KEY_TILE, :], src[k, KEY_TILE:, :]])

    @pl.when(n_quads > 0)
    def _():
        for k in kvs:
            pair_scores(k, sa_scr, 0)

    for k in kvs:
        def quad_body(qd, carry, k=k):
            first = 2 * qd
            pair_scores(k, sb_scr, first + 1)
            pair_update(k, sa_scr, first)
            pair_scores(k, sa_scr, jnp.minimum(first + 2, 2 * n_quads - 2))
            pair_update(k, sb_scr, first + 1)
            return carry

        lax.fori_loop(0, n_quads, quad_body, 0)

    @pl.when((n_pairs & 1) == 1)
    def _():
        attend(n_pairs - 1, [2 * n_pairs - 2, 2 * n_pairs - 1], no_bias)

    @pl.when(kd_odd)
    def _():
        attend(kd // 2, [kd - 1, kd], lambda k: [jnp.where(prev_near, near_at(k, 0), 0.0), near_at(k, KEY_TILE - Q_BLOCK * r)])

    @pl.when(even_prev)
    def _():
        attend(kd // 2 - 1, [kd - 2, kd - 1], lambda k: [None, near_at(k, 0)])

    @pl.when(jnp.logical_not(kd_odd))
    def _():
        attend(kd // 2, [kd], lambda k: [near_at(k, KEY_TILE - Q_BLOCK * r)])

    heads_out = []
    for k in kvs:
        acc = acc_scr[k]
        o_s = acc[0:HD] / acc[HD:HD + 1]
        o_c = oc_scr[k]
        for g in range(G_NSA):
            h = G_NSA * k + g
            cols = slice(g * Q_BLOCK, (g + 1) * Q_BLOCK)
            heads_out.append(o_c[:, cols] * g_t[h:h + 1] + o_s[:, cols] * g_t[H_NSA + h:H_NSA + h + 1]
                             + o_w[k][:, cols] * g_t[2 * H_NSA + h:2 * H_NSA + h + 1])
    o_ref[...] = jnp.concatenate(heads_out, axis=0).T


def _sel_pattern(rows, width):
    key = np.arange(rows)[:, None]
    b = np.arange(width)[None, :]
    ones = (b >= MASK_ROWS) & (b < MASK_ROWS + 2)
    return jnp.asarray(((key // SEL_BLOCK) % MASK_ROWS == b) | ones, BF16)


def _values_t(v, tile):
    T, n = v.shape
    rows = n + MASK_ROWS
    vt = jnp.concatenate([v.T, jnp.ones((1, T), F32), jnp.zeros((rows - n - 1, T), F32)], axis=0)
    return vt.reshape(rows, T // tile, tile).transpose(1, 0, 2).astype(BF16)


def _nsa_prompt(q, gates, kv, win, kc, vc, band, rel_table):
    T = q.shape[0]
    ncp, ns = kc.shape[0], T // SEL_BLOCK
    width = G_NSA * Q_BLOCK
    k_sel = lambda h: kv[:, 256 + h * HD:256 + (h + 1) * HD]
    v_sel = lambda h: kv[:, 384 + h * HD:384 + (h + 1) * HD]
    pattern = _sel_pattern(T, LANE - HD)
    ks_aug = jnp.stack([jnp.concatenate([k_sel(h).astype(BF16), pattern], axis=1) for h in range(N_KV)])
    operands = [q, gates, kc.astype(BF16), vc.T.astype(BF16), ks_aug, jnp.stack([_values_t(v_sel(h), KEY_TILE) for h in range(N_KV)]),
                win[:, 0:128].astype(BF16), jnp.stack([_values_t(win[:, 128 + h * HD:128 + (h + 1) * HD], Q_BLOCK) for h in range(N_KV)]),
                band, _pool_matrix(ns, ncp)]
    blk = lambda w: pl.BlockSpec((Q_BLOCK, w), lambda i: (i, 0))
    return pl.pallas_call(
        _nsa_prompt_kernel,
        grid=(T // Q_BLOCK,),
        in_specs=[pl.BlockSpec(memory_space=pltpu.SMEM), blk(H_NSA * HD), blk(LANE)] + [_resident(a.shape) for a in operands[2:]],
        out_specs=blk(H_NSA * HD),
        out_shape=jax.ShapeDtypeStruct((T, H_NSA * HD), F32),
        scratch_shapes=[pltpu.VMEM((N_KV, LANE, width), BF16), pltpu.VMEM((N_KV, ns, width), BF16),
                        pltpu.VMEM((N_KV, V_ROWS_KV, width), F32), pltpu.VMEM((N_KV, 1, width), F32),
                        pltpu.VMEM((N_KV, ncp, width), F32)] + [pltpu.VMEM((N_KV, 2 * KEY_TILE, width), F32)] * 2
                       + [pltpu.VMEM((N_KV, HD, width), F32)],
        compiler_params=_params(("arbitrary",)),
        name="nsa_prompt",
    )(rel_table, *operands)


SLAB_PAGES = 8
SMP_PAGES = 32
TOK_PAD = 8
SMP_COLS = H_NSA * TOK_PAD


def _nsa_sample_kernel(pt_ref, *refs, n_pages, n_valid, past):
    pages = refs[:n_pages]
    (q_ref, g_ref, kc_ref, vc_ref, kvn_ref, win_ref, winn_ref, tab_ref, pool_ref, gsum_ref, epat_ref, o_ref,
     mask_scr, acc_scr, m_scr, l_scr, oc_scr, ow_scr, qrow_scr, maskt_scr, farc_scr, nearbt_scr, m2_scr, l2_scr, acc2_scr) = refs[n_pages:]
    j = pl.program_id(1)
    ncp, wbuf = kc_ref.shape[0], win_ref.shape[1]
    lane = lax.broadcasted_iota(jnp.int32, (1, LANE), 1)
    tok = lane & (TOK_PAD - 1)
    second_kv = lane >= G_NSA * TOK_PAD
    tab = [tab_ref[b:b + 1, :] for b in range(N_BUCKETS)]
    far_row = tab[N_BUCKETS - 1]
    own_rows = lambda x: jnp.where(second_kv, x[HD:2 * HD], x[0:HD])
    pad_rows = lambda x: jnp.concatenate([x, jnp.zeros((LANE - x.shape[0], x.shape[1]), x.dtype)], axis=0)
    trow = lax.broadcasted_iota(jnp.int32, (LANE, 1), 0)
    d_new = tok - trow
    new_bias = jnp.where((d_new >= 0) & (trow < n_valid), _bias_chain(jnp.maximum(d_new, 0), tab), NEG)

    def attend_update(s, values_t):
        m_old = m_scr[...]
        m_new = jnp.maximum(m_old, jnp.max(s, axis=0, keepdims=True))
        alpha = jnp.exp(m_old - m_new)
        p = jnp.exp(s - m_new)
        l_scr[...] = alpha * l_scr[...] + jnp.sum(p, axis=0, keepdims=True)
        acc_scr[...] = alpha * acc_scr[...] + jnp.dot(values_t.astype(BF16), p.astype(BF16), preferred_element_type=F32)
        m_scr[...] = m_new

    def reset():
        m_scr[...] = jnp.full(m_scr.shape, M_INIT, F32)
        l_scr[...] = jnp.zeros(l_scr.shape, F32)
        acc_scr[...] = jnp.zeros(acc_scr.shape, F32)

    @pl.when(j == 0)
    def _():
        q_t = pad_rows(q_ref[...] * HD ** -0.5).T
        halves = []
        for k in range(N_KV):
            part = jnp.zeros((HD, LANE), F32)
            for g in range(G_NSA):
                h = G_NSA * k + g
                piece = q_t[h * HD:(h + 1) * HD, :]
                part = part + (pltpu.roll(piece, TOK_PAD * h, axis=1) if h else piece)
            halves.append(part)
        top_f = jnp.concatenate(halves, axis=0)
        top = top_f.astype(BF16)
        qrow_scr[...] = top_f.T[:SMP_COLS].astype(BF16)
        qpos = past + tok

        n0 = ncp - EDGE_ROWS
        kcb = kc_ref[...].astype(BF16)
        d_edge = qpos - (CMP_STRIDE * (n0 + lax.broadcasted_iota(jnp.int32, (EDGE_ROWS, 1), 0)) + CMP_BLOCK - 1)
        s_c = jnp.concatenate([
            jnp.dot(kcb[:n0], top, preferred_element_type=F32) + far_row,
            jnp.dot(kcb[n0:], top, preferred_element_type=F32) + jnp.where(d_edge >= 0, _bias_chain(jnp.maximum(d_edge, 0), tab), NEG)], axis=0)
        p_c = _softmax_cols(s_c)
        oc_scr[...] = own_rows(jnp.dot(vc_ref[...].T.astype(BF16), p_c.astype(BF16), preferred_element_type=F32))
        imp = _dot_exact_rhs(p_c, gsum_ref[...], terms=3)
        mask_scr[...] = _select_blocks(_dot_exact_lhs(pool_ref[...], imp), qpos, N_SEL)

        wk = win_ref[0:LANE, :].T
        d_w = wbuf + tok - lax.broadcasted_iota(jnp.int32, (wbuf, 1), 0)
        near = wbuf - LANE
        s_w = jnp.dot(wk.astype(BF16), top, preferred_element_type=F32)
        s_w = (jnp.concatenate([s_w[:near] + far_row, s_w[near:] + _bias_chain(d_w[near:], tab)], axis=0)
               + jnp.where(d_w < WINDOW, 0.0, NEG))
        reset()
        attend_update(s_w, win_ref[LANE:, :])
        wn = pad_rows(winn_ref[...])
        attend_update(jnp.dot(wn[:, :LANE].astype(BF16), top, preferred_element_type=F32) + new_bias, wn[:, LANE:].T)
        ow_scr[...] = own_rows(acc_scr[...]) / l_scr[...]
        m2_scr[...] = jnp.full(m2_scr.shape, M_INIT, F32)
        l2_scr[...] = jnp.zeros(l2_scr.shape, F32)
        acc2_scr[...] = jnp.zeros(acc2_scr.shape, F32)
        for sl in range(maskt_scr.shape[0]):
            maskt_scr[sl] = pad_rows(mask_scr[sl * MASK_ROWS:(sl + 1) * MASK_ROWS, :]).T[:SMP_COLS].astype(BF16)
        farc_scr[...] = pad_rows(tab_ref[...]).T[:SMP_COLS, N_BUCKETS - 1:N_BUCKETS]
        nearbt_scr[...] = _bias_chain(LANE + tok - lax.broadcasted_iota(jnp.int32, (LANE, 1), 0), tab).T[:SMP_COLS]

    def rows_update(s, values_t):
        m_old = m2_scr[...]
        m_new = jnp.maximum(m_old, jnp.max(s, axis=1, keepdims=True))
        alpha = jnp.exp(m_old - m_new)
        p = jnp.exp(s - m_new)
        l2_scr[...] = alpha * l2_scr[...] + jnp.sum(p, axis=1, keepdims=True)
        acc2_scr[...] = alpha * acc2_scr[...] + lax.dot_general(p.astype(BF16), values_t.astype(BF16), (((1,), (1,)), ((), ())),
                                                                preferred_element_type=F32)
        m2_scr[...] = m_new

    n_slabs = n_pages // SLAB_PAGES
    slab_keys = SLAB_PAGES * PAGE_SIZE
    far_col = farc_scr[...]
    for sub in range(n_slabs):
        tile_pages = pages[sub * SLAB_PAGES:(sub + 1) * SLAB_PAGES]
        kt_tile = jnp.concatenate([pg[0:LANE, :] for pg in tile_pages], axis=1)
        vt_tile = jnp.concatenate([pg[LANE:, :] for pg in tile_pages], axis=1)
        lhs = jnp.concatenate([qrow_scr[...], maskt_scr[j * n_slabs + sub]], axis=1)
        s = jnp.dot(lhs, jnp.concatenate([kt_tile.astype(BF16), epat_ref[...]], axis=0), preferred_element_type=F32)
        if sub < n_slabs - 1:
            s = s + far_col
        else:
            near = slab_keys - LANE
            s = jnp.concatenate([s[:, :near] + far_col, s[:, near:] + jnp.where(j == pl.num_programs(1) - 1, nearbt_scr[...], far_col)], axis=1)
        rows_update(s, vt_tile)

    @pl.when(j == pl.num_programs(1) - 1)
    def _():
        kn = pad_rows(kvn_ref[...])
        last_blk = past // SEL_BLOCK
        new_mask = maskt_scr[last_blk // MASK_ROWS][:, last_blk % MASK_ROWS:last_blk % MASK_ROWS + 1].astype(F32)
        s_n = (jnp.dot(qrow_scr[...], kn[:, 2 * LANE:3 * LANE].T.astype(BF16), preferred_element_type=F32)
               + new_bias.T[:SMP_COLS] + new_mask)
        rows_update(s_n, kn[:, 3 * LANE:].T)
        acc_t = pad_rows(acc2_scr[...]).T
        l_t = pad_rows(jnp.broadcast_to(l2_scr[...], (SMP_COLS, LANE))).T[0:1]
        o_s = own_rows(acc_t) / jnp.where(lane < SMP_COLS, l_t, 1.0)
        g_t = pad_rows(g_ref[...]).T
        gate_rows = []
        for b in range(3):
            row = g_t[b * H_NSA:b * H_NSA + 1]
            for h in range(1, H_NSA):
                row = row + pltpu.roll(g_t[b * H_NSA + h:b * H_NSA + h + 1], TOK_PAD * h, axis=1)
            gate_rows.append(row)
        o_col = oc_scr[...] * gate_rows[0] + o_s * gate_rows[1] + ow_scr[...] * gate_rows[2]
        per_head = [o_col if h == 0 else pltpu.roll(o_col, LANE - TOK_PAD * h, axis=1) for h in range(H_NSA)]
        o_ref[...] = jnp.concatenate(per_head, axis=0).T[:TOK_PAD]


def _nsa_sample(pool, page_table, q, gates, kc, vc, kv_new, win_buf, win_new, rel_table, n_valid):
    B, n_pages_total = page_table.shape
    past = n_pages_total * PAGE_SIZE
    ncp = kc.shape[1]
    ns = past // SEL_BLOCK + 1
    nsp = -(-ns // MASK_ROWS) * MASK_ROWS
    col = np.arange(LANE)
    used = col < SMP_COLS
    gsum = jnp.asarray(((col[:, None] // (G_NSA * TOK_PAD) == col[None, :] // (G_NSA * TOK_PAD))
                        & (col[:, None] % TOK_PAD == col[None, :] % TOK_PAD) & used[:, None] & used[None, :]), BF16)
    tab_cols = jnp.pad(jnp.repeat(rel_table, TOK_PAD, axis=1), ((0, 0), (0, LANE - SMP_COLS)))
    n_step = min(SMP_PAGES, n_pages_total)
    consts = [tab_cols, _pool_matrix(nsp, ncp), gsum, _sel_pattern(SLAB_PAGES * PAGE_SIZE, LANE).T]
    per_seq = [q, gates, kc, vc, kv_new, win_buf, win_new]
    seq_spec = lambda a: pl.BlockSpec((None,) + a.shape[1:], lambda b, j, pt: (b,) + (0,) * (a.ndim - 1))
    const = lambda a: pl.BlockSpec(a.shape, lambda b, j, pt: (0,) * a.ndim)

    def page_map(p):
        return lambda b, j, pt: (pt[b, j * n_step + p], 1, 0)

    grid_spec = pltpu.PrefetchScalarGridSpec(
        num_scalar_prefetch=1,
        grid=(B, n_pages_total // n_step),
        in_specs=[pl.BlockSpec((None, 2 * LANE, PAGE_SIZE), page_map(p)) for p in range(n_step)]
                 + [seq_spec(a) for a in per_seq] + [const(a) for a in consts],
        out_specs=pl.BlockSpec((None, TOK_PAD, H_NSA * HD), lambda b, j, pt: (b, 0, 0)),
        scratch_shapes=[pltpu.VMEM((nsp, LANE), F32), pltpu.VMEM((LANE, LANE), F32),
                        pltpu.VMEM((1, LANE), F32), pltpu.VMEM((1, LANE), F32), pltpu.VMEM((HD, LANE), F32), pltpu.VMEM((HD, LANE), F32),
                        pltpu.VMEM((SMP_COLS, LANE), BF16), pltpu.VMEM((nsp // MASK_ROWS, SMP_COLS, LANE), BF16),
                        pltpu.VMEM((SMP_COLS, 1), F32), pltpu.VMEM((SMP_COLS, LANE), F32),
                        pltpu.VMEM((SMP_COLS, 1), F32), pltpu.VMEM((SMP_COLS, 1), F32), pltpu.VMEM((SMP_COLS, LANE), F32)],
    )
    return pl.pallas_call(
        functools.partial(_nsa_sample_kernel, n_pages=n_step, n_valid=n_valid, past=past),
        grid_spec=grid_spec,
        out_shape=jax.ShapeDtypeStruct((B, TOK_PAD, H_NSA * HD), F32),
        compiler_params=_params(("arbitrary", "arbitrary")),
        name="nsa_sample",
    )(page_table, *([pool] * n_step), *per_seq, *consts)


def _outproj_kernel(x_ref, nsa_ref, rw_ref, gt_ref, lng_ref, lnb_ref, w_ref, o_ref):
    half = H_NSA * HD
    out = (jnp.dot(nsa_ref[...].astype(BF16), w_ref[0:half, :], preferred_element_type=F32)
           + jnp.dot(rw_ref[...].astype(BF16), w_ref[half:, :], preferred_element_type=F32))
    y = ALPHA * x_ref[...] + (1.0 + gt_ref[...]) * out
    o_ref[...] = _layer_norm(y, lng_ref[...], lnb_ref[...])


def _outproj(x, o_nsa, o_rwkv, gate, ln_g, ln_b, w_out):
    rows = x.shape[0]
    tm = min(512, rows)
    row = lambda i: (i, 0)
    return pl.pallas_call(
        _outproj_kernel,
        grid=(rows // tm,),
        in_specs=[pl.BlockSpec((tm, D_MODEL), row), pl.BlockSpec((tm, H_NSA * HD), row), pl.BlockSpec((tm, D_RWKV), row),
                  _mod_spec(gate, tm), _resident((1, D_MODEL)), _resident((1, D_MODEL)), _resident(w_out.shape)],
        out_specs=pl.BlockSpec((tm, D_MODEL), row),
        out_shape=jax.ShapeDtypeStruct((rows, D_MODEL), F32),
        compiler_params=_params(("arbitrary",)),
        name="outproj",
    )(x, o_nsa, o_rwkv, gate, ln_g.reshape(1, -1), ln_b.reshape(1, -1), w_out)


def kernel(x_prompt, x_sample, cache_nsa_kv, cache_nsa_win, state_rwkv_shift, state_rwkv_wkv, page_table, c_prompt, c_sample, rel_table, w_ada, b_ada, ln_g, ln_b, ffn1_gate, ffn1_up, ffn1_down, ffn2_gate, ffn2_up, ffn2_down, w_in, w_out, cmp_pe_k, cmp_w1_k, cmp_b1_k, cmp_w2_k, cmp_pe_v, cmp_w1_v, cmp_b1_v, cmp_w2_v, rwkv_mu, rwkv_w0, rwkv_w2, rwkv_a0, rwkv_a2, rwkv_g2, rwkv_k_k, rwkv_k_a, rwkv_r_k, rwkv_gn_w, rwkv_gn_b):
    assert w_ada.shape[0] == DEPTH == 1 and x_prompt.shape[0] == 1
    l = 0
    lw = dict(cmp_pe_k=cmp_pe_k[l], cmp_w1_k=cmp_w1_k[l], cmp_b1_k=cmp_b1_k[l], cmp_w2_k=cmp_w2_k[l],
              cmp_pe_v=cmp_pe_v[l], cmp_w1_v=cmp_w1_v[l], cmp_b1_v=cmp_b1_v[l], cmp_w2_v=cmp_w2_v[l],
              rwkv_mu=rwkv_mu[l], rwkv_w0=rwkv_w0[l], rwkv_w2=rwkv_w2[l], rwkv_a0=rwkv_a0[l], rwkv_a2=rwkv_a2[l], rwkv_g2=rwkv_g2[l],
              rwkv_k_k=rwkv_k_k[l], rwkv_k_a=rwkv_k_a[l], rwkv_r_k=rwkv_r_k[l], rwkv_gn_w=rwkv_gn_w[l], rwkv_gn_b=rwkv_gn_b[l])
    T = x_prompt.shape[1]
    nb, nt = x_sample.shape[0], x_sample.shape[1]
    assert nt <= TOK_PAD
    n_seq = 1 + nb
    c_all = jnp.concatenate([c_prompt, c_sample, jnp.zeros((-n_seq % 8, D_MODEL), F32)], axis=0)
    mod = _ada(c_all, w_ada[l], b_ada[l])
    mod_p = mod[0:1].reshape(9, 1, D_MODEL)
    mod_s = jnp.repeat(mod[1:n_seq].reshape(nb, 9, D_MODEL), nt, axis=0).transpose(1, 0, 2)
    ffn1 = [w[l].astype(BF16) for w in (ffn1_gate, ffn1_up, ffn1_down)]
    ffn2 = [w[l].astype(BF16) for w in (ffn2_gate, ffn2_up, ffn2_down)]
    w_in_p = _prep_w_in(w_in[l])
    w_out_b = w_out[l].astype(BF16)

    def trunk_in(x, m):
        x1 = _ffn(x, m[0], m[1], m[2], ln_g[l, 0], ln_b[l, 0], *ffn1)
        return x1, _proj(x1, m[3], m[4], w_in_p)

    def trunk_out(x1, o_nsa, o_rwkv, m):
        x2 = _outproj(x1, o_nsa, o_rwkv, m[5], ln_g[l, 1], ln_b[l, 1], w_out_b)
        return _ffn(x2, m[6], m[7], m[8], ln_g[l, 2], ln_b[l, 2], *ffn2)

    xp1, (q, kv, win, gates, pr) = trunk_in(x_prompt[0], mod_p)
    o_rw, wkv_p = _rwkv(pr[None], jnp.zeros((1, 1, RW_PAD), F32), jnp.zeros((1, H_RWKV, HD_RWKV, HD_RWKV), F32), lw, min(RW_STEP, T))
    n_rows = T // PAGE_SIZE
    kc, vc = _compress(kv.reshape(n_rows, PAGE_SIZE, 4 * LANE), jnp.arange(n_rows, dtype=jnp.int32)[None], lw, transposed=False)
    o_nsa = _nsa_prompt(q, gates, kv, win, kc[0], vc[0], _band(rel_table), rel_table)
    y_prompt = trunk_out(xp1, o_nsa, o_rw[0], mod_p)
    kv_prompt = kv.reshape(1, 1, T, 4, N_KV, HD)
    win_prompt = win[T - min(WINDOW, T):].reshape(1, 1, -1, 2, N_KV, HD)
    shift_prompt = _rwkv_uncols(pr[T - 1]).reshape(1, 1, RWKV_COLS)

    xs1, (q_s, kv_s, win_s, gates_s, pr_s) = trunk_in(x_sample.reshape(nb * nt, D_MODEL), mod_s)
    tokens = lambda a: jnp.pad(a.reshape(nb, nt, -1), ((0, 0), (0, TOK_PAD - nt), (0, 0)))
    pr_pad = jnp.pad(pr_s.reshape(nb, nt, -1), ((0, 0), (0, RW_TOK_PAD - nt), (0, 0)))
    o_rw_s, wkv_s = _rwkv(pr_pad, _rwkv_cols(state_rwkv_shift[l])[:, None], state_rwkv_wkv[l], lw, nt)
    pool_t = jnp.transpose(cache_nsa_kv[l], (0, 2, 3, 4, 1)).reshape(-1, 4 * LANE, PAGE_SIZE)
    kc_s, vc_s = _compress(pool_t, page_table, lw, transposed=True)
    win_buf = cache_nsa_win[l]
    win_t = jnp.transpose(win_buf, (0, 2, 3, 4, 1)).reshape(nb, 2 * LANE, -1)
    o_nsa_s = _nsa_sample(pool_t, page_table, tokens(q_s), tokens(gates_s), kc_s, vc_s, tokens(kv_s),
                          win_t, tokens(win_s), rel_table, nt)
    y_sample = trunk_out(xs1, o_nsa_s[:, :nt].reshape(nb * nt, -1), o_rw_s[:, :nt].reshape(nb * nt, -1), mod_s)
    kv_sample = kv_s.reshape(1, nb, nt, 4, N_KV, HD)
    win_sample = jnp.concatenate([win_buf, win_s.reshape(nb, nt, 2, N_KV, HD)], axis=1)[None, :, nt:]
    shift_sample = _rwkv_uncols(pr_s.reshape(nb, nt, -1)[:, -1])[None]
    return (y_prompt[None], y_sample.reshape(nb, nt, D_MODEL), kv_prompt, win_prompt, shift_prompt, wkv_p[None],
            kv_sample, win_sample, shift_sample, wkv_s[None])
```

```python
import functools
import math

import numpy as np
import jax
import jax.numpy as jnp
from jax import lax
from jax.experimental import pallas as pl
from jax.experimental.pallas import tpu as pltpu

D_MODEL = 1024
PAGE_SIZE = 128
H_NSA = 8
N_KV = 2
G_NSA = H_NSA // N_KV
HD = 64
CMP_STRIDE = 16
CMP_BLOCK = 2 * CMP_STRIDE
CMP_HIDDEN = 256
SEL_BLOCK = 64
N_SEL = 16
WINDOW = 512
Q_BLOCK = 128
N_BUCKETS = 32
MAX_DISTANCE = 128
H_RWKV = 8
HD_RWKV = 64
D_RWKV = H_RWKV * HD_RWKV
DECAY_LORA = 32
AAA_LORA = 32
GATE_LORA = 96
GN_EPS = 64e-5
D_FF = 2816
LN_EPS = 1e-5
DEPTH = 1
ALPHA = (2 * DEPTH) ** 0.25

NSA_SIZES = (H_NSA * HD,) + (N_KV * HD,) * 6 + (H_NSA * 3,)
RWKV_SIZES = (D_RWKV, D_RWKV, D_RWKV, DECAY_LORA, AAA_LORA, GATE_LORA)
NSA_COLS = sum(NSA_SIZES)
RWKV_COLS = sum(RWKV_SIZES)

F32 = jnp.float32
BF16 = jnp.bfloat16
LANE = 128
NEG = -(2.0 ** 100)
M_INIT = -(2.0 ** 103)
VMEM_LIMIT = 56 * 1024 * 1024

RW_PAD = 3 * D_RWKV + 3 * LANE
P_Q, P_KV, P_WIN, P_GATE, P_RW = 0, 512, 1024, 1280, 1408
P_COLS = P_RW + RW_PAD
KEY_TILE = 512
MASK_ROWS = 16
SEL_MASK0 = HD
SEL_FAR0 = SEL_MASK0 + MASK_ROWS
V_ROWS_KV = HD + MASK_ROWS
CMP_CLASS_ROWS = 256
Q_PER_TILE = KEY_TILE // Q_BLOCK
CMP_PER_Q = Q_BLOCK // CMP_STRIDE
EDGE_ROWS = 32


def _bucket_lows():
    d = np.arange(0, 4 * MAX_DISTANCE, dtype=np.int64)
    max_exact = N_BUCKETS // 2
    df = np.maximum(d, 1).astype(np.float32)
    large = max_exact + (np.log(df / np.float32(max_exact)) / np.float32(math.log(MAX_DISTANCE / max_exact))
                         * np.float32(N_BUCKETS - max_exact)).astype(np.int32)
    b = np.where(d < max_exact, d, np.minimum(large, N_BUCKETS - 1))
    lows = [int(np.argmax(b >= k)) for k in range(N_BUCKETS)]
    return b, lows


_BUCKET_OF, _BUCKET_LOW = _bucket_lows()
FAR_DIST = _BUCKET_LOW[N_BUCKETS - 1]


def _resident(shape):
    nd = len(shape)
    return pl.BlockSpec(shape, lambda *_: (0,) * nd, pipeline_mode=pl.Buffered(1))


def _params(sem):
    return pltpu.CompilerParams(dimension_semantics=sem, vmem_limit_bytes=VMEM_LIMIT)


def _dot_exact_rhs(x, rhs_bf16, terms=2):
    acc = None
    rem = x
    for _ in range(terms):
        part = rem.astype(BF16)
        d = jnp.dot(part, rhs_bf16, preferred_element_type=F32)
        acc = d if acc is None else acc + d
        rem = rem - part.astype(F32)
    return acc


def _dot_exact_lhs(lhs_bf16, x, terms=3):
    acc = None
    rem = x
    for _ in range(terms):
        part = rem.astype(BF16)
        d = jnp.dot(lhs_bf16, part, preferred_element_type=F32)
        acc = d if acc is None else acc + d
        rem = rem - part.astype(F32)
    return acc


def _layer_norm(y, g, b):
    mu = jnp.mean(y, axis=-1, keepdims=True)
    yc = y - mu
    var = jnp.mean(yc * yc, axis=-1, keepdims=True)
    return yc * lax.rsqrt(var + LN_EPS) * g + b


def _bias_chain(d, tab_rows):
    out = tab_rows[0] + jnp.zeros(d.shape, F32)
    for b in range(1, N_BUCKETS):
        out = jnp.where(d >= _BUCKET_LOW[b], tab_rows[b], out)
    return out


def _ada_kernel(c_ref, w_ref, b_ref, o_ref):
    c = c_ref[...]
    h = (c * jax.nn.sigmoid(c)).astype(BF16)
    o_ref[...] = jnp.dot(h, w_ref[...].astype(BF16), preferred_element_type=F32) + b_ref[...]


def _ada(c_all, w_ada, b_ada):
    rows, n = c_all.shape[0], w_ada.shape[1]
    tn = 1152
    return pl.pallas_call(
        _ada_kernel,
        grid=(n // tn,),
        in_specs=[pl.BlockSpec((rows, D_MODEL), lambda j: (0, 0)),
                  pl.BlockSpec((D_MODEL, tn), lambda j: (0, j)),
                  pl.BlockSpec((1, tn), lambda j: (0, j))],
        out_specs=pl.BlockSpec((rows, tn), lambda j: (0, j)),
        out_shape=jax.ShapeDtypeStruct((rows, n), F32),
        compiler_params=_params(("arbitrary",)),
        name="ada",
    )(c_all, w_ada, b_ada.reshape(1, n))


FF_CHUNKS = 2


def _ffn_kernel(x_ref, sh_ref, sc_ref, gt_ref, lng_ref, lnb_ref, wg_ref, wu_ref, wd_ref, o_ref):
    x = x_ref[...]
    h = (x * (1.0 + sc_ref[...]) + sh_ref[...]).astype(BF16)
    ck = D_FF // FF_CHUNKS
    acc = jnp.zeros(x.shape, F32)
    for c in range(FF_CHUNKS):
        a = jnp.dot(h, wg_ref[:, c * ck:(c + 1) * ck], preferred_element_type=F32)
        b = jnp.dot(h, wu_ref[:, c * ck:(c + 1) * ck], preferred_element_type=F32)
        t = (a * jax.nn.sigmoid(a) * b).astype(BF16)
        acc = acc + jnp.dot(t, wd_ref[c * ck:(c + 1) * ck, :], preferred_element_type=F32)
    y = ALPHA * x + (1.0 + gt_ref[...]) * (0.5 * acc)
    o_ref[...] = _layer_norm(y, lng_ref[...], lnb_ref[...])


def _mod_spec(mod, tm):
    if mod.shape[0] == 1:
        return pl.BlockSpec((1, D_MODEL), lambda i: (0, 0))
    return pl.BlockSpec((tm, D_MODEL), lambda i: (i, 0))


def _ffn(x, shift, scale, gate, ln_g, ln_b, wg, wu, wd):
    rows = x.shape[0]
    tm = min(512, rows)
    row = lambda i: (i, 0)
    return pl.pallas_call(
        _ffn_kernel,
        grid=(rows // tm,),
        in_specs=[pl.BlockSpec((tm, D_MODEL), row), _mod_spec(shift, tm), _mod_spec(scale, tm), _mod_spec(gate, tm),
                  _resident((1, D_MODEL)), _resident((1, D_MODEL)),
                  _resident((D_MODEL, D_FF)), _resident((D_MODEL, D_FF)), _resident((D_FF, D_MODEL))],
        out_specs=pl.BlockSpec((tm, D_MODEL), row),
        out_shape=jax.ShapeDtypeStruct((rows, D_MODEL), F32),
        compiler_params=_params(("arbitrary",)),
        name="ffn",
    )(x, shift, scale, gate, ln_g.reshape(1, -1), ln_b.reshape(1, -1), wg, wu, wd)


def _proj_kernel(x_ref, sh_ref, sc_ref, w_ref, q_ref, kv_ref, win_ref, g_ref, pr_ref, *attn_refs):
    h = (x_ref[...] * (1.0 + sc_ref[...]) + sh_ref[...]).astype(BF16)
    p = jnp.dot(h, w_ref[...], preferred_element_type=F32)
    q_ref[...] = p[:, P_Q:P_KV]
    kv_ref[...] = p[:, P_KV:P_WIN]
    win_ref[...] = p[:, P_WIN:P_GATE]
    g_ref[...] = jax.nn.sigmoid(p[:, P_GATE:P_RW])
    pr_ref[...] = p[:, P_RW:P_COLS]
    if attn_refs:
        ks_ref, vst_ref, kw_ref, vwt_ref = attn_refs
        tm = x_ref.shape[0]
        lane = lax.broadcasted_iota(jnp.int32, (1, LANE), 1)
        key = pl.program_id(0) * tm + lax.broadcasted_iota(jnp.int32, (tm, 1), 0)
        slot = lane - HD
        pattern = (((key // SEL_BLOCK) % MASK_ROWS == slot) | ((slot >= MASK_ROWS) & (slot < MASK_ROWS + 2))).astype(F32)
        k_sel = p[:, P_KV + 2 * LANE:P_KV + 3 * LANE]
        ks_ref[0] = jnp.where(lane < HD, k_sel, pattern).astype(BF16)
        ks_ref[1] = jnp.where(lane < HD, pltpu.roll(k_sel, HD, axis=1), pattern).astype(BF16)
        tail = jnp.concatenate([jnp.ones((1, tm), F32), jnp.zeros((MASK_ROWS - 1, tm), F32)], axis=0)
        v_sel_t = p[:, P_KV + 3 * LANE:P_WIN].T
        v_win_t = p[:, P_WIN + LANE:P_GATE].T
        kw_ref[...] = p[:, P_WIN:P_WIN + LANE].astype(BF16)
        for hd in range(N_KV):
            vst_ref[hd, 0] = jnp.concatenate([v_sel_t[hd * HD:(hd + 1) * HD], tail], axis=0).astype(BF16)
            win_rows = jnp.concatenate([v_win_t[hd * HD:(hd + 1) * HD], tail], axis=0).astype(BF16)
            for j in range(tm // Q_BLOCK):
                vwt_ref[hd, j] = win_rows[:, j * Q_BLOCK:(j + 1) * Q_BLOCK]


def _proj(x, shift, scale, w_in_p, attn_operands=False):
    rows = x.shape[0]
    tm = min(KEY_TILE, rows)
    row = lambda i: (i, 0)
    widths = (512, 512, 256, LANE, RW_PAD)
    out_specs = [pl.BlockSpec((tm, w), row) for w in widths]
    out_shape = [jax.ShapeDtypeStruct((rows, w), F32) for w in widths]
    if attn_operands:
        assert tm == KEY_TILE
        per_q = tm // Q_BLOCK
        out_specs += [pl.BlockSpec((N_KV, tm, LANE), lambda i: (0, i, 0)), pl.BlockSpec((N_KV, 1, V_ROWS_KV, tm), lambda i: (0, i, 0, 0)),
                      pl.BlockSpec((tm, LANE), row), pl.BlockSpec((N_KV, per_q, V_ROWS_KV, Q_BLOCK), lambda i: (0, i, 0, 0))]
        out_shape += [jax.ShapeDtypeStruct((N_KV, rows, LANE), BF16), jax.ShapeDtypeStruct((N_KV, rows // tm, V_ROWS_KV, tm), BF16),
                      jax.ShapeDtypeStruct((rows, LANE), BF16), jax.ShapeDtypeStruct((N_KV, rows // Q_BLOCK, V_ROWS_KV, Q_BLOCK), BF16)]
    return pl.pallas_call(
        _proj_kernel,
        grid=(rows // tm,),
        in_specs=[pl.BlockSpec((tm, D_MODEL), row), _mod_spec(shift, tm), _mod_spec(scale, tm),
                  _resident((D_MODEL, P_COLS))],
        out_specs=out_specs,
        out_shape=out_shape,
        compiler_params=_params(("arbitrary",)),
        name="proj",
    )(x, shift, scale, w_in_p)


def _prep_w_in(w_in):
    pad = lambda a, n: jnp.pad(a, ((0, 0), (0, n - a.shape[1])))
    nsa, rw = w_in[:, :NSA_COLS], w_in[:, NSA_COLS:]
    gl = nsa[:, 1280:1304].reshape(D_MODEL, H_NSA, 3).transpose(0, 2, 1).reshape(D_MODEL, 3 * H_NSA)
    cols = [nsa[:, :1280], pad(gl, LANE), _rwkv_cols(rw)]
    return jnp.concatenate(cols, axis=1).astype(BF16)


def _rwkv_cols(a):
    pad = lambda t: jnp.pad(t, [(0, 0)] * (t.ndim - 1) + [(0, LANE - t.shape[-1])])
    n = 3 * D_RWKV
    return jnp.concatenate([a[..., :n], pad(a[..., n:n + 32]), pad(a[..., n + 32:n + 64]), pad(a[..., n + 64:n + 160])], axis=-1)


def _rwkv_uncols(a):
    n = 3 * D_RWKV
    return jnp.concatenate([a[..., :n], a[..., n:n + 32], a[..., n + LANE:n + LANE + 32], a[..., n + 2 * LANE:n + 2 * LANE + 96]], axis=-1)


RW_GROUP = 64
RW_TOK_PAD = 16
RW_PAIRS = H_RWKV // 2
RW_BLOCK = 64
RW_STEP = 256


def _lora(x, w_ref):
    w = w_ref[...]
    w_hi = w.astype(BF16)
    w_lo = (w - w_hi.astype(F32)).astype(BF16)
    return _dot_exact_rhs(x, w_hi) + jnp.dot(x.astype(BF16), w_lo, preferred_element_type=F32)


def _rwkv_kernel(pr_ref, sh0_ref, s0_ref, mu_ref, w0_ref, a0_ref, kk_ref, ka_ref, rk_ref, gw_ref, gb_ref,
                 w2_ref, a2_ref, g2_ref, bo_ref, lgrp_ref, ggrp_ref, o_ref, sout_ref,
                 prev_scr, s_scr, *, n_valid):
    tb = pr_ref.shape[0]
    step = pl.program_id(1)

    @pl.when(step == 0)
    def _():
        prev_scr[...] = sh0_ref[...]
        s_scr[...] = s0_ref[...]

    p = pr_ref[...]
    rows = lax.broadcasted_iota(jnp.int32, (tb, 1), 0)
    prev = jnp.where(rows == 0, prev_scr[...], pltpu.roll(p, 1, axis=0))
    prev_scr[...] = p[tb - 1:tb, :]
    xs = p + (prev - p) * mu_ref[...]
    n = D_RWKV
    r, k, v = xs[:, :n], xs[:, n:2 * n], xs[:, 2 * n:3 * n]

    def head_sums(x):
        return jnp.concatenate([_dot_exact_rhs(x[:, :n // 2], bo_ref[...]), _dot_exact_rhs(x[:, n // 2:], bo_ref[...])], axis=1)

    wl, al, gl = xs[:, 3 * n:3 * n + LANE], xs[:, 3 * n + LANE:3 * n + 2 * LANE], xs[:, 3 * n + 2 * LANE:]
    z = -(w0_ref[...] + _lora(jnp.tanh(wl), w2_ref))
    w = -(jnp.maximum(z, 0.0) + jnp.log(1.0 + jnp.exp(-jnp.abs(z)))) - 0.5
    a = jax.nn.sigmoid(a0_ref[...] + _lora(al, a2_ref))
    g = _lora(jax.nn.sigmoid(gl), g2_ref)
    kk = k * kk_ref[...]
    ss = head_sums(kk * kk)
    kk = kk / jnp.maximum(jnp.sqrt(ss), 1e-12)
    k2 = k * (1.0 + (a - 1.0) * ka_ref[...])
    G = min(RW_GROUP, tb)
    log_dec = -jnp.exp(w)
    bet = kk * a
    if n_valid < tb:
        live = rows < n_valid
        log_dec, kk, bet, k2, v_in = (jnp.where(live, x, 0.0) for x in (log_dec, kk, bet, k2, v))
    else:
        v_in = v
    cum = _dot_exact_lhs(lgrp_ref[...], log_dec, terms=2)
    cum_end = _dot_exact_lhs(ggrp_ref[...], log_dec, terms=2)
    gam_inv = jnp.exp(-cum)
    gam_end = jnp.exp(cum_end - cum)
    k_hat = -kk * jnp.exp(cum - log_dec)
    r_hat = r * jnp.exp(cum)
    b_chk, k_chk = bet * gam_inv, k2 * gam_inv
    b_til, k_til = bet * gam_end, k2 * gam_end
    gam_group = jnp.exp(cum_end)

    lane = lax.broadcasted_iota(jnp.int32, (1, LANE), 1)
    low = lane < HD_RWKV
    lane_t = lane & (RW_BLOCK - 1)
    row = lax.broadcasted_iota(jnp.int32, (LANE, 1), 0)
    row_t = row & (RW_BLOCK - 1)
    same = ((row < HD_RWKV) == low) & ((row_t // G) == (lane_t // G))
    strict, incl = same & (lane_t < row_t), same & (lane_t <= row_t)
    bf = lambda x: x.astype(BF16)
    mm = lambda x, y: jnp.dot(bf(x), bf(y), preferred_element_type=F32)
    mm_nt = lambda x, y: lax.dot_general(bf(x), bf(y), (((1,), (1,)), ((), ())), preferred_element_type=F32)

    def rows_bd(x):
        if tb < RW_BLOCK:
            x = jnp.concatenate([x, jnp.zeros((RW_BLOCK - tb, LANE), F32)], axis=0)
        return jnp.concatenate([jnp.where(low, x, 0.0), jnp.where(low, 0.0, x)], axis=0)

    def mm3(x, y):
        xh, yh = bf(x), bf(y)
        xl, yl = bf(x - xh.astype(F32)), bf(y - yh.astype(F32))
        return jnp.dot(jnp.concatenate([xh, xl, xh], axis=1), jnp.concatenate([yh, yh, yl], axis=0), preferred_element_type=F32)

    units = max(1, tb // RW_BLOCK)
    unit_rows = min(tb, RW_BLOCK)
    pairs = range(RW_PAIRS)
    items = [(un, pp) for un in range(units) for pp in pairs]
    at = lambda x, it: x[it[0] * unit_rows:(it[0] + 1) * unit_rows, it[1] * LANE:(it[1] + 1) * LANE]
    kh_row = [rows_bd(at(k_hat, it)) for it in items]
    rh_row = [rows_bd(at(r_hat, it)) for it in items]
    kh_mat = [x.T for x in kh_row]
    rh_mat = [x.T for x in rh_row]
    state_in = [jnp.concatenate([rows_bd(at(b_chk, it)).T, rows_bd(at(k_chk, it)).T], axis=1) for it in items]
    upd_rows = [jnp.concatenate([rows_bd(at(b_til, it)), rows_bd(at(k_til, it))], axis=0) for it in items]
    v_t = [rows_bd(at(v_in, it)).T for it in items]
    v_t = [x[:HD_RWKV] + x[HD_RWKV:] for x in v_t]
    c_all = [mm(jnp.concatenate([kh_row[n], rh_row[n]], axis=0), state_in[n]) for n in range(len(items))]
    c_uu = [jnp.where(strict, c[:LANE, :LANE], 0.0) for c in c_all]
    c_uv = [jnp.where(strict, c[:LANE, LANE:], 0.0) for c in c_all]
    c_ru = [jnp.where(incl, c[LANE:, :LANE], 0.0) for c in c_all]
    c_rv = [jnp.where(incl, c[LANE:, LANE:], 0.0) for c in c_all]
    t_neu, power = list(c_uu), list(c_uu)
    span = 2
    while span < G:
        power = [mm(x, x) for x in power]
        t_neu = [t_neu[n] + power[n] + mm(t_neu[n], power[n]) for n in range(len(items))]
        span *= 2
    from_v = [mm_nt(v_t[n], c_uv[n]) for n in range(len(items))]
    st = [s_scr[pp] for pp in pairs]
    y_units = []
    for un in range(units):
        ns = [un * RW_PAIRS + pp for pp in pairs]
        y_t = [jnp.zeros((HD_RWKV, LANE), F32) for _ in pairs]
        for grp in range(unit_rows // G):
            here = (lane_t // G) == grp
            first = un * unit_rows + grp * G
            w_t = [jnp.where(here, mm(st[pp], kh_mat[ns[pp]]) + from_v[ns[pp]], 0.0) for pp in pairs]
            u_t = [w_t[pp] + mm_nt(w_t[pp], t_neu[ns[pp]]) for pp in pairs]
            v_g = [jnp.where(here, v_t[ns[pp]], 0.0) for pp in pairs]
            y_t = [y_t[pp] + jnp.where(here, mm(st[pp], rh_mat[ns[pp]]), 0.0) + mm_nt(u_t[pp], c_ru[ns[pp]])
                   + mm_nt(v_g[pp], c_rv[ns[pp]]) for pp in pairs]
            st = [st[pp] * gam_group[first:first + 1, pp * LANE:(pp + 1) * LANE]
                  + mm3(jnp.concatenate([u_t[pp], v_g[pp]], axis=1), upd_rows[ns[pp]]) for pp in pairs]
        lane_u = lax.broadcasted_iota(jnp.int32, (unit_rows, LANE), 1)
        pieces = []
        for pp in pairs:
            yt = jnp.concatenate([y_t[pp], jnp.zeros((LANE - HD_RWKV, LANE), F32)], axis=0).T
            pieces.append(jnp.where(lane_u < HD_RWKV, yt[:unit_rows], pltpu.roll(yt[RW_BLOCK:RW_BLOCK + unit_rows], HD_RWKV, axis=1)))
        y_units.append(jnp.concatenate(pieces, axis=1))
    for pp in pairs:
        s_scr[pp] = st[pp]
    y = y_units[0] if units == 1 else jnp.concatenate(y_units, axis=0)
    mean = head_sums(y) * (1.0 / HD_RWKV)
    yc = y - mean
    var = head_sums(yc * yc) * (1.0 / HD_RWKV)
    yn = yc * lax.rsqrt(var + GN_EPS) * gw_ref[...] + gb_ref[...]
    bonus = head_sums(r * k2 * rk_ref[...]) * v
    o_ref[...] = (yn + bonus) * g
    sout_ref[...] = s_scr[...]


def _pair_state(s):
    B = s.shape[0]
    return s.reshape(B, RW_PAIRS, 2, HD_RWKV, HD_RWKV).transpose(0, 1, 3, 2, 4).reshape(B, RW_PAIRS, HD_RWKV, LANE)


def _unpair_state(s):
    B = s.shape[0]
    return s.reshape(B, RW_PAIRS, HD_RWKV, 2, HD_RWKV).transpose(0, 1, 3, 2, 4).reshape(B, H_RWKV, HD_RWKV, HD_RWKV)


def _rwkv(pr, shift0, s0, lw, n_valid):
    B, T, _ = pr.shape
    tb = min(RW_STEP, T)
    n = D_RWKV
    vec = lambda a: a.reshape(1, n)
    padrow = lambda a: jnp.pad(a, ((0, LANE - a.shape[0]), (0, 0)))
    blk = np.arange(n // 2) // HD_RWKV
    block_ones = jnp.asarray(blk[:, None] == blk[None, :], BF16)
    tok = np.arange(tb)
    group = min(RW_GROUP, tb)
    same_group = tok[:, None] // group == tok[None, :] // group
    prefix = jnp.asarray(same_group & (tok[None, :] <= tok[:, None]), BF16)
    consts = [_rwkv_cols(lw['rwkv_mu']).reshape(1, RW_PAD), vec(lw['rwkv_w0']), vec(lw['rwkv_a0']), vec(lw['rwkv_k_k']),
              vec(lw['rwkv_k_a']), vec(lw['rwkv_r_k']), vec(lw['rwkv_gn_w']), vec(lw['rwkv_gn_b']),
              padrow(lw['rwkv_w2']), padrow(lw['rwkv_a2']), padrow(lw['rwkv_g2']), block_ones, prefix, jnp.asarray(same_group, BF16)]
    kern = functools.partial(_rwkv_kernel, n_valid=n_valid)
    state_spec = pl.BlockSpec((None, RW_PAIRS, HD_RWKV, LANE), lambda b, j: (b, 0, 0, 0))
    o, s = pl.pallas_call(
        kern,
        grid=(B, T // tb),
        in_specs=[pl.BlockSpec((None, tb, RW_PAD), lambda b, j: (b, j, 0)),
                  pl.BlockSpec((None, 1, RW_PAD), lambda b, j: (b, 0, 0)), state_spec]
                 + [_resident(c.shape) for c in consts],
        out_specs=[pl.BlockSpec((None, tb, n), lambda b, j: (b, j, 0)), state_spec],
        out_shape=[jax.ShapeDtypeStruct((B, T, n), F32), jax.ShapeDtypeStruct((B, RW_PAIRS, HD_RWKV, LANE), F32)],
        scratch_shapes=[pltpu.VMEM((1, RW_PAD), F32), pltpu.VMEM((RW_PAIRS, HD_RWKV, LANE), F32)],
        compiler_params=_params(("arbitrary", "arbitrary")),
        name="rwkv",
    )(pr, shift0, _pair_state(s0), *consts)
    return o, _unpair_state(s)


CMP_PAGES = 64
CHUNKS_PER_PAGE = PAGE_SIZE // CMP_STRIDE


def _compress_kernel(pt_ref, *refs, n_pages, transposed):
    pages, nxt = refs[:n_pages], refs[n_pages]
    weights = refs[n_pages + 1:n_pages + 9]
    outs = refs[n_pages + 9:n_pages + 11]
    width = N_KV * HD
    rows = CHUNKS_PER_PAGE * n_pages
    seg = rows + 8
    kinds = range(2)
    low = lax.broadcasted_iota(jnp.int32, (1, N_KV * HD), 1) < HD
    rows_scr = refs[n_pages + 11:n_pages + 13]

    def by_head(row_s):
        heads = [[], []]
        for s in range(0, CMP_STRIDE, 2):
            a, b = row_s(s), row_s(s + 1)
            heads[0].append(jnp.where(low, a, pltpu.roll(b, HD, axis=1)))
            heads[1].append(jnp.where(low, pltpu.roll(a, HD, axis=1), b))
        return [jnp.concatenate(h, axis=1) for h in heads]

    for kind in kinds:
        pe_ref, w_ref, b_ref, w2_ref = weights[4 * kind:4 * kind + 4]
        part = slice(kind * width, (kind + 1) * width)
        for p, pg in enumerate(pages):
            rows_scr[kind][p * PAGE_SIZE:(p + 1) * PAGE_SIZE, :] = pg[part, :].T if transposed else pg[:, part]
        nxt_rows = nxt[part, :].T[:CMP_STRIDE] if transposed else nxt[:, part]
        x = by_head(lambda s: rows_scr[kind][pl.ds(s, rows, stride=CMP_STRIDE), :])
        x_next = by_head(lambda s: jnp.broadcast_to(nxt_rows[s:s + 1, :], (8, N_KV * HD)))
        x_all = jnp.concatenate([x[0], x_next[0], x[1], x_next[1]], axis=0)
        h_first = jnp.dot((x_all + pe_ref[0]).astype(BF16), w_ref[0], preferred_element_type=F32)
        h_second = jnp.dot((x_all + pe_ref[1]).astype(BF16), w_ref[1], preferred_element_type=F32)
        out = None
        for h in range(N_KV):
            h_next = pltpu.roll(h_second[h * seg:(h + 1) * seg], seg - 1, axis=0)[:rows]
            hidden = jax.nn.gelu(h_first[h * seg:h * seg + rows] + h_next + b_ref[...])
            part = jnp.dot(hidden.astype(BF16), w2_ref[h], preferred_element_type=F32)
            out = part if out is None else out + part
        outs[kind][...] = out


def _compress_weights(pe, w1, b1, w2):
    n = CMP_STRIDE * HD
    pe2 = pe.reshape(2, 1, n)
    w_halves = w1.reshape(2, n, CMP_HIDDEN).astype(BF16)
    zero = jnp.zeros_like(w2)
    w2_heads = jnp.stack([jnp.concatenate([w2, zero], axis=1), jnp.concatenate([zero, w2], axis=1)]).astype(BF16)
    return [pe2, w_halves, b1.reshape(1, -1), w2_heads]


def _compress(pool, page_table, lw, transposed):
    B, n_pages_total = page_table.shape
    n_pages = min(CMP_PAGES, n_pages_total)
    rows = CHUNKS_PER_PAGE * n_pages
    weights = (_compress_weights(lw['cmp_pe_k'], lw['cmp_w1_k'], lw['cmp_b1_k'], lw['cmp_w2_k'])
               + _compress_weights(lw['cmp_pe_v'], lw['cmp_w1_v'], lw['cmp_b1_v'], lw['cmp_w2_v']))
    width = N_KV * HD

    def page_map(p):
        return lambda b, j, pt: (pt[b, j * n_pages + p], 0, 0)

    next_map = lambda b, j, pt: (pt[b, jnp.minimum((j + 1) * n_pages, n_pages_total - 1)], 0, 0)
    page_block = (None, 2 * width, PAGE_SIZE) if transposed else (None, PAGE_SIZE, 2 * width)
    next_block = page_block if transposed else (None, CMP_STRIDE, 2 * width)
    const = lambda a: pl.BlockSpec(a.shape, lambda b, j, pt: (0,) * a.ndim)
    out_spec = pl.BlockSpec((None, rows, N_KV * HD), lambda b, j, pt: (b, j, 0))
    out_shape = jax.ShapeDtypeStruct((B, n_pages_total * CHUNKS_PER_PAGE, N_KV * HD), F32)
    grid_spec = pltpu.PrefetchScalarGridSpec(
        num_scalar_prefetch=1,
        grid=(B, n_pages_total // n_pages),
        in_specs=[pl.BlockSpec(page_block, page_map(p)) for p in range(n_pages)]
                 + [pl.BlockSpec(next_block, next_map)] + [const(a) for a in weights],
        out_specs=[out_spec, out_spec],
        scratch_shapes=[pltpu.VMEM((n_pages * PAGE_SIZE, width), F32)] * 2,
    )
    return pl.pallas_call(
        functools.partial(_compress_kernel, n_pages=n_pages, transposed=transposed),
        grid_spec=grid_spec,
        out_shape=[out_shape, out_shape],
        compiler_params=_params(("arbitrary", "arbitrary")),
        name="compress",
    )(page_table, *([pool] * (n_pages + 1)), *weights)


BAND_ROWS = 1152


def _band_kernel(tab_ref, bkt_ref, o_ref):
    h = pl.program_id(0)
    bkt = bkt_ref[...]
    out = jnp.full(bkt.shape, NEG, F32)
    for b in range(N_BUCKETS):
        out = jnp.where(bkt == b, tab_ref[b, h], out)
    o_ref[...] = out


def _band(rel_table):
    u = np.arange(BAND_ROWS)[:, None]
    qi = np.arange(Q_BLOCK)[None, :]
    d = qi + WINDOW - u
    bkt = np.where(d >= 0, _BUCKET_OF[np.clip(d, 0, len(_BUCKET_OF) - 1)], -1).astype(np.int32)
    return pl.pallas_call(
        _band_kernel,
        grid=(H_NSA,),
        in_specs=[pl.BlockSpec(memory_space=pltpu.SMEM), pl.BlockSpec((BAND_ROWS, Q_BLOCK), lambda h: (0, 0))],
        out_specs=pl.BlockSpec((None, BAND_ROWS, Q_BLOCK), lambda h: (h, 0, 0)),
        out_shape=jax.ShapeDtypeStruct((H_NSA, BAND_ROWS, Q_BLOCK), F32),
        compiler_params=_params(("arbitrary",)),
        name="band",
    )(rel_table, jnp.asarray(bkt))


def _softmax_cols(s):
    m = jnp.max(s, axis=0, keepdims=True)
    e = jnp.exp(s - m)
    l = jnp.sum(e, axis=0, keepdims=True)
    return e * jnp.where(m > 0.5 * NEG, 1.0 / l, 0.0)


def _select_blocks(impsel, qpos, n_pick):
    ns = impsel.shape[0]
    blk = lax.broadcasted_iota(jnp.int32, impsel.shape, 0)
    cur = jnp.right_shift(qpos, SEL_BLOCK.bit_length() - 1)
    future = blk * SEL_BLOCK > qpos
    forced = (blk == 0) | (blk == cur) | (blk == cur - 1)
    score = jnp.where(future, -jnp.inf, jnp.where(forced, jnp.inf, impsel))
    chosen = jnp.zeros(impsel.shape, F32)
    for _ in range(n_pick):
        best = jnp.max(score, axis=0, keepdims=True)
        first = jnp.min(jnp.where(score == best, blk, ns), axis=0, keepdims=True)
        hit = (blk == first) & (best > -jnp.inf)
        chosen = jnp.where(hit, 1.0, chosen)
        score = jnp.where(hit, -jnp.inf, score)
    return jnp.where(chosen > 0.0, 0.0, NEG)


def _pool_matrix(ns, nc):
    j = np.arange(ns)[:, None]
    n = np.arange(nc)[None, :]
    ratio = SEL_BLOCK // CMP_STRIDE
    return jnp.asarray((n >= ratio * j - 1) & (n <= ratio * j + ratio - 1), BF16)


def _nsa_prompt_kernel(tab_ref, q_ref, g_ref, kc_ref, vct_ref, ks_ref, vst_ref, kw_ref, vwt_ref, band_ref, pool_ref, o_ref,
                       rhs_scr, mask_scr, acc_scr, m_scr, sc_scr, sa_scr, sb_scr, oc_scr):
    i = pl.program_id(0)
    ncp = kc_ref.shape[0]
    ns = pool_ref.shape[0]
    s0 = i * Q_BLOCK
    q_t = (q_ref[...] * HD ** -0.5).T
    g_t = g_ref[...].T
    lane_q = lax.broadcasted_iota(jnp.int32, (1, G_NSA * Q_BLOCK), 1) & (Q_BLOCK - 1)
    qpos = s0 + lax.broadcasted_iota(jnp.int32, (1, Q_BLOCK), 1)
    rhs_scr[...] = jnp.zeros(rhs_scr.shape, BF16)
    kvs = range(N_KV)
    kd = i // Q_PER_TILE
    r = i % Q_PER_TILE
    lanes4 = lambda k, f: jnp.concatenate([f(G_NSA * k + g) for g in range(G_NSA)], axis=1)
    qcols = [lanes4(k, lambda h: q_t[h * HD:(h + 1) * HD, :]) for k in kvs]
    zero = jnp.zeros_like(qcols[0])
    top = [jnp.concatenate([qcols[0], zero], axis=0).astype(BF16), jnp.concatenate([zero, qcols[1]], axis=0).astype(BF16)]
    far_row = [lanes4(k, lambda h: band_ref[h, 0:1, :]) for k in kvs]


    n0 = pl.multiple_of(jnp.clip(CMP_PER_Q * i - EDGE_ROWS // 2, 0, ncp - EDGE_ROWS), 8)
    nrow = lax.broadcasted_iota(jnp.int32, (ncp, 1), 0)
    d_edge = qpos - (CMP_STRIDE * (n0 + lax.broadcasted_iota(jnp.int32, (EDGE_ROWS, 1), 0)) + CMP_BLOCK - 1)

    def compressed(k, rows):
        sc_scr[k, 0:rows, :] = (jnp.dot(kc_ref[0:rows, :], top[k], preferred_element_type=F32)
                                + jnp.where(nrow[0:rows] < n0, far_row[k], NEG))
        edge_bias = lanes4(k, lambda h: jnp.where(d_edge >= 0, _bias_chain(d_edge, [tab_ref[b, h] for b in range(N_BUCKETS)]), NEG))
        sc_scr[k, pl.ds(n0, EDGE_ROWS), :] = jnp.dot(kc_ref[pl.ds(n0, EDGE_ROWS), :], top[k], preferred_element_type=F32) + edge_bias
        p_c = _softmax_cols(sc_scr[k, 0:rows, :])
        o_c = jnp.dot(vct_ref[:, 0:rows], p_c.astype(BF16), preferred_element_type=F32)[k * HD:(k + 1) * HD]
        imp = p_c[:, 0:Q_BLOCK]
        for g in range(1, G_NSA):
            imp = imp + p_c[:, g * Q_BLOCK:(g + 1) * Q_BLOCK]
        n_blk = rows * CMP_STRIDE // SEL_BLOCK
        return o_c, _dot_exact_lhs(pool_ref[0:n_blk, 0:rows], imp, terms=2)

    size_step = min(CMP_CLASS_ROWS, ncp)
    size_class = (n0 + EDGE_ROWS - 1) // size_step
    for cls in range(ncp // size_step):
        @pl.when(size_class == cls)
        def _(rows=(cls + 1) * size_step):
            comp = [compressed(k, rows) for k in kvs]
            n_blk = comp[0][1].shape[0]
            masks = [_select_blocks(comp[k][1], qpos, min(N_SEL, ns)) for k in kvs]
            for k in kvs:
                oc_scr[k] = comp[k][0]
                full = jnp.concatenate([masks[k], jnp.full((ns - n_blk, Q_BLOCK), NEG, F32)], axis=0) if n_blk < ns else masks[k]
                mask_scr[k] = jnp.concatenate([full.astype(BF16)] * G_NSA, axis=1)

    ws = pl.multiple_of(jnp.maximum(s0 - WINDOW, 0), Q_BLOCK)
    u0 = pl.multiple_of(WINDOW - (s0 - ws), Q_BLOCK)
    n_win = WINDOW + Q_BLOCK
    u = u0 + lax.broadcasted_iota(jnp.int32, (n_win, 1), 0)
    win_mask = jnp.where(u > lane_q, 0.0, NEG)

    def window(k):
        s_w = (jnp.dot(kw_ref[pl.ds(ws, n_win), :], top[k], preferred_element_type=F32)
               + lanes4(k, lambda h: band_ref[h, pl.ds(u0, n_win), :]) + win_mask)
        m_w = jnp.max(s_w, axis=0, keepdims=True)
        p_w = jnp.exp(s_w - m_w).astype(BF16)
        acc_w = jnp.zeros((V_ROWS_KV, G_NSA * Q_BLOCK), F32)
        for j in range(n_win // Q_BLOCK):
            acc_w = acc_w + jnp.dot(vwt_ref[k, ws // Q_BLOCK + j], p_w[j * Q_BLOCK:(j + 1) * Q_BLOCK], preferred_element_type=F32)
        return acc_w[0:HD] / acc_w[HD:HD + 1]

    o_w = [window(k) for k in kvs]

    for k in kvs:
        m_scr[k] = jnp.full(m_scr.shape[1:], M_INIT, F32)
        acc_scr[k] = jnp.zeros(acc_scr.shape[1:], F32)
        far_hi = far_row[k].astype(BF16).astype(F32)
        rhs_scr[k, 0:HD, :] = qcols[k].astype(BF16)
        rhs_scr[k, SEL_FAR0:SEL_FAR0 + MASK_ROWS, :] = jnp.concatenate(
            [far_hi, far_row[k] - far_hi, jnp.zeros((MASK_ROWS - 2, G_NSA * Q_BLOCK), F32)], axis=0).astype(BF16)

    def scores(k, slab, kts, extra):
        rhs_scr[k, SEL_MASK0:SEL_MASK0 + MASK_ROWS, :] = mask_scr[k, pl.ds(pl.multiple_of(slab * MASK_ROWS, MASK_ROWS), MASK_ROWS), :]
        rhs = rhs_scr[k]
        out = []
        for kt, add in zip(kts, extra):
            s = jnp.dot(ks_ref[k, pl.ds(pl.multiple_of(kt * KEY_TILE, KEY_TILE), KEY_TILE), :], rhs, preferred_element_type=F32)
            out.append(s if add is None else s + add)
        return out

    def update(k, kts, tiles):
        m_old = m_scr[k]
        m_new = m_old
        for s in tiles:
            m_new = jnp.maximum(m_new, jnp.max(s, axis=0, keepdims=True))
        acc = jnp.exp(m_old - m_new) * acc_scr[k]
        for kt, s in zip(kts, tiles):
            acc = acc + jnp.dot(vst_ref[k, kt], jnp.exp(s - m_new).astype(BF16), preferred_element_type=F32)
        acc_scr[k] = acc
        m_scr[k] = m_new

    def attend(slab, kts, extra):
        tiles = [scores(k, slab, kts, extra(k)) for k in kvs]
        for k in kvs:
            update(k, kts, tiles[k])

    near_at = lambda k, start: lanes4(k, lambda h: band_ref[h, pl.ds(pl.multiple_of(start, Q_BLOCK), KEY_TILE), :]) - far_row[k]
    prev_near = (r == 0) & (kd >= 1)
    kd_odd = (kd & 1) == 1
    even_prev = jnp.logical_not(kd_odd) & prev_near
    n_pairs = kd // 2 - even_prev.astype(jnp.int32)
    n_quads = n_pairs // 2
    no_bias = lambda k: [None, None]

    def pair_scores(k, dst, pair):
        lo, hi = scores(k, pair, [2 * pair, 2 * pair + 1], [None, None])
        dst[k, 0:KEY_TILE, :] = lo
        dst[k, KEY_TILE:, :] = hi

    def pair_update(k, src, pair):
        update(k, [2 * pair, 2 * pair + 1], [src[k, 0:KEY_TILE, :], src[k, KEY_TILE:, :]])

    @pl.when(n_quads > 0)
    def _():
        for k in kvs:
            pair_scores(k, sa_scr, 0)

    for k in kvs:
        def quad_body(qd, carry, k=k):
            first = 2 * qd
            pair_scores(k, sb_scr, first + 1)
            pair_update(k, sa_scr, first)
            pair_scores(k, sa_scr, jnp.minimum(first + 2, 2 * n_quads - 2))
            pair_update(k, sb_scr, first + 1)
            return carry

        lax.fori_loop(0, n_quads, quad_body, 0)

    @pl.when((n_pairs & 1) == 1)
    def _():
        attend(n_pairs - 1, [2 * n_pairs - 2, 2 * n_pairs - 1], no_bias)

    @pl.when(kd_odd)
    def _():
        attend(kd // 2, [kd - 1, kd], lambda k: [jnp.where(prev_near, near_at(k, 0), 0.0), near_at(k, KEY_TILE - Q_BLOCK * r)])

    @pl.when(even_prev)
    def _():
        attend(kd // 2 - 1, [kd - 2, kd - 1], lambda k: [None, near_at(k, 0)])

    @pl.when(jnp.logical_not(kd_odd))
    def _():
        attend(kd // 2, [kd], lambda k: [near_at(k, KEY_TILE - Q_BLOCK * r)])

    heads_out = []
    for k in kvs:
        acc = acc_scr[k]
        o_s = acc[0:HD] / acc[HD:HD + 1]
        o_c = oc_scr[k]
        for g in range(G_NSA):
            h = G_NSA * k + g
            cols = slice(g * Q_BLOCK, (g + 1) * Q_BLOCK)
            heads_out.append(o_c[:, cols] * g_t[h:h + 1] + o_s[:, cols] * g_t[H_NSA + h:H_NSA + h + 1]
                             + o_w[k][:, cols] * g_t[2 * H_NSA + h:2 * H_NSA + h + 1])
    o_ref[...] = jnp.concatenate(heads_out, axis=0).T


def _sel_pattern(rows, width):
    key = np.arange(rows)[:, None]
    b = np.arange(width)[None, :]
    ones = (b >= MASK_ROWS) & (b < MASK_ROWS + 2)
    return jnp.asarray(((key // SEL_BLOCK) % MASK_ROWS == b) | ones, BF16)


def _nsa_prompt(q, gates, attn, kc, vc, band, rel_table):
    T = q.shape[0]
    ncp, ns = kc.shape[0], T // SEL_BLOCK
    width = G_NSA * Q_BLOCK
    ks_aug, vs_t, kw, vw_t = attn
    operands = [q, gates, kc.astype(BF16), vc.T.astype(BF16), ks_aug, vs_t, kw, vw_t, band, _pool_matrix(ns, ncp)]
    blk = lambda w: pl.BlockSpec((Q_BLOCK, w), lambda i: (i, 0))
    return pl.pallas_call(
        _nsa_prompt_kernel,
        grid=(T // Q_BLOCK,),
        in_specs=[pl.BlockSpec(memory_space=pltpu.SMEM), blk(H_NSA * HD), blk(LANE)] + [_resident(a.shape) for a in operands[2:]],
        out_specs=blk(H_NSA * HD),
        out_shape=jax.ShapeDtypeStruct((T, H_NSA * HD), F32),
        scratch_shapes=[pltpu.VMEM((N_KV, LANE, width), BF16), pltpu.VMEM((N_KV, ns, width), BF16),
                        pltpu.VMEM((N_KV, V_ROWS_KV, width), F32), pltpu.VMEM((N_KV, 1, width), F32),
                        pltpu.VMEM((N_KV, ncp, width), F32)] + [pltpu.VMEM((N_KV, 2 * KEY_TILE, width), F32)] * 2
                       + [pltpu.VMEM((N_KV, HD, width), F32)],
        compiler_params=_params(("arbitrary",)),
        name="nsa_prompt",
    )(rel_table, *operands)


SLAB_PAGES = 8
SMP_PAGES = 32
TOK_PAD = 8
SMP_COLS = H_NSA * TOK_PAD


def _nsa_sample_kernel(pt_ref, *refs, n_pages, n_valid, past):
    pages = refs[:n_pages]
    (q_ref, g_ref, kc_ref, vc_ref, kvn_ref, win_ref, winn_ref, tab_ref, pool_ref, gsum_ref, epat_ref, o_ref,
     mask_scr, acc_scr, m_scr, l_scr, oc_scr, ow_scr, qrow_scr, maskt_scr, farc_scr, nearbt_scr, m2_scr, l2_scr, acc2_scr) = refs[n_pages:]
    j = pl.program_id(1)
    ncp, wbuf = kc_ref.shape[0], win_ref.shape[1]
    lane = lax.broadcasted_iota(jnp.int32, (1, LANE), 1)
    tok = lane & (TOK_PAD - 1)
    second_kv = lane >= G_NSA * TOK_PAD
    tab = [tab_ref[b:b + 1, :] for b in range(N_BUCKETS)]
    far_row = tab[N_BUCKETS - 1]
    own_rows = lambda x: jnp.where(second_kv, x[HD:2 * HD], x[0:HD])
    pad_rows = lambda x: jnp.concatenate([x, jnp.zeros((LANE - x.shape[0], x.shape[1]), x.dtype)], axis=0)
    trow = lax.broadcasted_iota(jnp.int32, (LANE, 1), 0)
    d_new = tok - trow
    new_bias = jnp.where((d_new >= 0) & (trow < n_valid), _bias_chain(jnp.maximum(d_new, 0), tab), NEG)

    def attend_update(s, values_t):
        m_old = m_scr[...]
        m_new = jnp.maximum(m_old, jnp.max(s, axis=0, keepdims=True))
        alpha = jnp.exp(m_old - m_new)
        p = jnp.exp(s - m_new)
        l_scr[...] = alpha * l_scr[...] + jnp.sum(p, axis=0, keepdims=True)
        acc_scr[...] = alpha * acc_scr[...] + jnp.dot(values_t.astype(BF16), p.astype(BF16), preferred_element_type=F32)
        m_scr[...] = m_new

    def reset():
        m_scr[...] = jnp.full(m_scr.shape, M_INIT, F32)
        l_scr[...] = jnp.zeros(l_scr.shape, F32)
        acc_scr[...] = jnp.zeros(acc_scr.shape, F32)

    @pl.when(j == 0)
    def _():
        q_t = pad_rows(q_ref[...] * HD ** -0.5).T
        halves = []
        for k in range(N_KV):
            part = jnp.zeros((HD, LANE), F32)
            for g in range(G_NSA):
                h = G_NSA * k + g
                piece = q_t[h * HD:(h + 1) * HD, :]
                part = part + (pltpu.roll(piece, TOK_PAD * h, axis=1) if h else piece)
            halves.append(part)
        top_f = jnp.concatenate(halves, axis=0)
        top = top_f.astype(BF16)
        qrow_scr[...] = top_f.T[:SMP_COLS].astype(BF16)
        qpos = past + tok

        n0 = ncp - EDGE_ROWS
        kcb = kc_ref[...].astype(BF16)
        d_edge = qpos - (CMP_STRIDE * (n0 + lax.broadcasted_iota(jnp.int32, (EDGE_ROWS, 1), 0)) + CMP_BLOCK - 1)
        s_c = jnp.concatenate([
            jnp.dot(kcb[:n0], top, preferred_element_type=F32) + far_row,
            jnp.dot(kcb[n0:], top, preferred_element_type=F32) + jnp.where(d_edge >= 0, _bias_chain(jnp.maximum(d_edge, 0), tab), NEG)], axis=0)
        p_c = _softmax_cols(s_c)
        oc_scr[...] = own_rows(jnp.dot(vc_ref[...].T.astype(BF16), p_c.astype(BF16), preferred_element_type=F32))
        imp = _dot_exact_rhs(p_c, gsum_ref[...], terms=3)
        mask_scr[...] = _select_blocks(_dot_exact_lhs(pool_ref[...], imp), qpos, N_SEL)

        wk = win_ref[0:LANE, :].T
        d_w = wbuf + tok - lax.broadcasted_iota(jnp.int32, (wbuf, 1), 0)
        near = wbuf - LANE
        s_w = jnp.dot(wk.astype(BF16), top, preferred_element_type=F32)
        s_w = (jnp.concatenate([s_w[:near] + far_row, s_w[near:] + _bias_chain(d_w[near:], tab)], axis=0)
               + jnp.where(d_w < WINDOW, 0.0, NEG))
        reset()
        attend_update(s_w, win_ref[LANE:, :])
        wn = pad_rows(winn_ref[...])
        attend_update(jnp.dot(wn[:, :LANE].astype(BF16), top, preferred_element_type=F32) + new_bias, wn[:, LANE:].T)
        ow_scr[...] = own_rows(acc_scr[...]) / l_scr[...]
        m2_scr[...] = jnp.full(m2_scr.shape, M_INIT, F32)
        l2_scr[...] = jnp.zeros(l2_scr.shape, F32)
        acc2_scr[...] = jnp.zeros(acc2_scr.shape, F32)
        for sl in range(maskt_scr.shape[0]):
            maskt_scr[sl] = pad_rows(mask_scr[sl * MASK_ROWS:(sl + 1) * MASK_ROWS, :]).T[:SMP_COLS].astype(BF16)
        farc_scr[...] = pad_rows(tab_ref[...]).T[:SMP_COLS, N_BUCKETS - 1:N_BUCKETS]
        nearbt_scr[...] = _bias_chain(LANE + tok - lax.broadcasted_iota(jnp.int32, (LANE, 1), 0), tab).T[:SMP_COLS]

    def rows_update(s, values_t):
        m_old = m2_scr[...]
        m_new = jnp.maximum(m_old, jnp.max(s, axis=1, keepdims=True))
        alpha = jnp.exp(m_old - m_new)
        p = jnp.exp(s - m_new)
        l2_scr[...] = alpha * l2_scr[...] + jnp.sum(p, axis=1, keepdims=True)
        acc2_scr[...] = alpha * acc2_scr[...] + lax.dot_general(p.astype(BF16), values_t.astype(BF16), (((1,), (1,)), ((), ())),
                                                                preferred_element_type=F32)
        m2_scr[...] = m_new

    n_slabs = n_pages // SLAB_PAGES
    slab_keys = SLAB_PAGES * PAGE_SIZE
    far_col = farc_scr[...]
    for sub in range(n_slabs):
        tile_pages = pages[sub * SLAB_PAGES:(sub + 1) * SLAB_PAGES]
        kt_tile = jnp.concatenate([pg[0:LANE, :] for pg in tile_pages], axis=1)
        vt_tile = jnp.concatenate([pg[LANE:, :] for pg in tile_pages], axis=1)
        lhs = jnp.concatenate([qrow_scr[...], maskt_scr[j * n_slabs + sub]], axis=1)
        s = jnp.dot(lhs, jnp.concatenate([kt_tile.astype(BF16), epat_ref[...]], axis=0), preferred_element_type=F32)
        if sub < n_slabs - 1:
            s = s + far_col
        else:
            near = slab_keys - LANE
            s = jnp.concatenate([s[:, :near] + far_col, s[:, near:] + jnp.where(j == pl.num_programs(1) - 1, nearbt_scr[...], far_col)], axis=1)
        rows_update(s, vt_tile)

    @pl.when(j == pl.num_programs(1) - 1)
    def _():
        kn = pad_rows(kvn_ref[...])
        last_blk = past // SEL_BLOCK
        new_mask = maskt_scr[last_blk // MASK_ROWS][:, last_blk % MASK_ROWS:last_blk % MASK_ROWS + 1].astype(F32)
        s_n = (jnp.dot(qrow_scr[...], kn[:, 2 * LANE:3 * LANE].T.astype(BF16), preferred_element_type=F32)
               + new_bias.T[:SMP_COLS] + new_mask)
        rows_update(s_n, kn[:, 3 * LANE:].T)
        acc_t = pad_rows(acc2_scr[...]).T
        l_t = pad_rows(jnp.broadcast_to(l2_scr[...], (SMP_COLS, LANE))).T[0:1]
        o_s = own_rows(acc_t) / jnp.where(lane < SMP_COLS, l_t, 1.0)
        g_t = pad_rows(g_ref[...]).T
        gate_rows = []
        for b in range(3):
            row = g_t[b * H_NSA:b * H_NSA + 1]
            for h in range(1, H_NSA):
                row = row + pltpu.roll(g_t[b * H_NSA + h:b * H_NSA + h + 1], TOK_PAD * h, axis=1)
            gate_rows.append(row)
        o_col = oc_scr[...] * gate_rows[0] + o_s * gate_rows[1] + ow_scr[...] * gate_rows[2]
        per_head = [o_col if h == 0 else pltpu.roll(o_col, LANE - TOK_PAD * h, axis=1) for h in range(H_NSA)]
        o_ref[...] = jnp.concatenate(per_head, axis=0).T[:TOK_PAD]


def _nsa_sample(pool, page_table, q, gates, kc, vc, kv_new, win_buf, win_new, rel_table, n_valid):
    B, n_pages_total = page_table.shape
    past = n_pages_total * PAGE_SIZE
    ncp = kc.shape[1]
    ns = past // SEL_BLOCK + 1
    nsp = -(-ns // MASK_ROWS) * MASK_ROWS
    col = np.arange(LANE)
    used = col < SMP_COLS
    gsum = jnp.asarray(((col[:, None] // (G_NSA * TOK_PAD) == col[None, :] // (G_NSA * TOK_PAD))
                        & (col[:, None] % TOK_PAD == col[None, :] % TOK_PAD) & used[:, None] & used[None, :]), BF16)
    tab_cols = jnp.pad(jnp.repeat(rel_table, TOK_PAD, axis=1), ((0, 0), (0, LANE - SMP_COLS)))
    n_step = min(SMP_PAGES, n_pages_total)
    consts = [tab_cols, _pool_matrix(nsp, ncp), gsum, _sel_pattern(SLAB_PAGES * PAGE_SIZE, LANE).T]
    per_seq = [q, gates, kc, vc, kv_new, win_buf, win_new]
    seq_spec = lambda a: pl.BlockSpec((None,) + a.shape[1:], lambda b, j, pt: (b,) + (0,) * (a.ndim - 1))
    const = lambda a: pl.BlockSpec(a.shape, lambda b, j, pt: (0,) * a.ndim)

    def page_map(p):
        return lambda b, j, pt: (pt[b, j * n_step + p], 1, 0)

    grid_spec = pltpu.PrefetchScalarGridSpec(
        num_scalar_prefetch=1,
        grid=(B, n_pages_total // n_step),
        in_specs=[pl.BlockSpec((None, 2 * LANE, PAGE_SIZE), page_map(p)) for p in range(n_step)]
                 + [seq_spec(a) for a in per_seq] + [const(a) for a in consts],
        out_specs=pl.BlockSpec((None, TOK_PAD, H_NSA * HD), lambda b, j, pt: (b, 0, 0)),
        scratch_shapes=[pltpu.VMEM((nsp, LANE), F32), pltpu.VMEM((LANE, LANE), F32),
                        pltpu.VMEM((1, LANE), F32), pltpu.VMEM((1, LANE), F32), pltpu.VMEM((HD, LANE), F32), pltpu.VMEM((HD, LANE), F32),
                        pltpu.VMEM((SMP_COLS, LANE), BF16), pltpu.VMEM((nsp // MASK_ROWS, SMP_COLS, LANE), BF16),
                        pltpu.VMEM((SMP_COLS, 1), F32), pltpu.VMEM((SMP_COLS, LANE), F32),
                        pltpu.VMEM((SMP_COLS, 1), F32), pltpu.VMEM((SMP_COLS, 1), F32), pltpu.VMEM((SMP_COLS, LANE), F32)],
    )
    return pl.pallas_call(
        functools.partial(_nsa_sample_kernel, n_pages=n_step, n_valid=n_valid, past=past),
        grid_spec=grid_spec,
        out_shape=jax.ShapeDtypeStruct((B, TOK_PAD, H_NSA * HD), F32),
        compiler_params=_params(("arbitrary", "arbitrary")),
        name="nsa_sample",
    )(page_table, *([pool] * n_step), *per_seq, *consts)


def _outproj_kernel(x_ref, nsa_ref, rw_ref, gt_ref, lng_ref, lnb_ref, w_ref, o_ref):
    half = H_NSA * HD
    out = (jnp.dot(nsa_ref[...].astype(BF16), w_ref[0:half, :], preferred_element_type=F32)
           + jnp.dot(rw_ref[...].astype(BF16), w_ref[half:, :], preferred_element_type=F32))
    y = ALPHA * x_ref[...] + (1.0 + gt_ref[...]) * out
    o_ref[...] = _layer_norm(y, lng_ref[...], lnb_ref[...])


def _outproj(x, o_nsa, o_rwkv, gate, ln_g, ln_b, w_out):
    rows = x.shape[0]
    tm = min(512, rows)
    row = lambda i: (i, 0)
    return pl.pallas_call(
        _outproj_kernel,
        grid=(rows // tm,),
        in_specs=[pl.BlockSpec((tm, D_MODEL), row), pl.BlockSpec((tm, H_NSA * HD), row), pl.BlockSpec((tm, D_RWKV), row),
                  _mod_spec(gate, tm), _resident((1, D_MODEL)), _resident((1, D_MODEL)), _resident(w_out.shape)],
        out_specs=pl.BlockSpec((tm, D_MODEL), row),
        out_shape=jax.ShapeDtypeStruct((rows, D_MODEL), F32),
        compiler_params=_params(("arbitrary",)),
        name="outproj",
    )(x, o_nsa, o_rwkv, gate, ln_g.reshape(1, -1), ln_b.reshape(1, -1), w_out)


def kernel(x_prompt, x_sample, cache_nsa_kv, cache_nsa_win, state_rwkv_shift, state_rwkv_wkv, page_table, c_prompt, c_sample, rel_table, w_ada, b_ada, ln_g, ln_b, ffn1_gate, ffn1_up, ffn1_down, ffn2_gate, ffn2_up, ffn2_down, w_in, w_out, cmp_pe_k, cmp_w1_k, cmp_b1_k, cmp_w2_k, cmp_pe_v, cmp_w1_v, cmp_b1_v, cmp_w2_v, rwkv_mu, rwkv_w0, rwkv_w2, rwkv_a0, rwkv_a2, rwkv_g2, rwkv_k_k, rwkv_k_a, rwkv_r_k, rwkv_gn_w, rwkv_gn_b):
    assert w_ada.shape[0] == DEPTH == 1 and x_prompt.shape[0] == 1
    l = 0
    lw = dict(cmp_pe_k=cmp_pe_k[l], cmp_w1_k=cmp_w1_k[l], cmp_b1_k=cmp_b1_k[l], cmp_w2_k=cmp_w2_k[l],
              cmp_pe_v=cmp_pe_v[l], cmp_w1_v=cmp_w1_v[l], cmp_b1_v=cmp_b1_v[l], cmp_w2_v=cmp_w2_v[l],
              rwkv_mu=rwkv_mu[l], rwkv_w0=rwkv_w0[l], rwkv_w2=rwkv_w2[l], rwkv_a0=rwkv_a0[l], rwkv_a2=rwkv_a2[l], rwkv_g2=rwkv_g2[l],
              rwkv_k_k=rwkv_k_k[l], rwkv_k_a=rwkv_k_a[l], rwkv_r_k=rwkv_r_k[l], rwkv_gn_w=rwkv_gn_w[l], rwkv_gn_b=rwkv_gn_b[l])
    T = x_prompt.shape[1]
    nb, nt = x_sample.shape[0], x_sample.shape[1]
    assert nt <= TOK_PAD
    n_seq = 1 + nb
    c_all = jnp.concatenate([c_prompt, c_sample, jnp.zeros((-n_seq % 8, D_MODEL), F32)], axis=0)
    mod = _ada(c_all, w_ada[l], b_ada[l])
    mod_p = mod[0:1].reshape(9, 1, D_MODEL)
    mod_s = jnp.repeat(mod[1:n_seq].reshape(nb, 9, D_MODEL), nt, axis=0).transpose(1, 0, 2)
    ffn1 = [w[l].astype(BF16) for w in (ffn1_gate, ffn1_up, ffn1_down)]
    ffn2 = [w[l].astype(BF16) for w in (ffn2_gate, ffn2_up, ffn2_down)]
    w_in_p = _prep_w_in(w_in[l])
    w_out_b = w_out[l].astype(BF16)

    def trunk_in(x, m, attn_operands=False):
        x1 = _ffn(x, m[0], m[1], m[2], ln_g[l, 0], ln_b[l, 0], *ffn1)
        return x1, _proj(x1, m[3], m[4], w_in_p, attn_operands)

    def trunk_out(x1, o_nsa, o_rwkv, m):
        x2 = _outproj(x1, o_nsa, o_rwkv, m[5], ln_g[l, 1], ln_b[l, 1], w_out_b)
        return _ffn(x2, m[6], m[7], m[8], ln_g[l, 2], ln_b[l, 2], *ffn2)

    xp1, (q, kv, win, gates, pr, *attn) = trunk_in(x_prompt[0], mod_p, attn_operands=True)
    o_rw, wkv_p = _rwkv(pr[None], jnp.zeros((1, 1, RW_PAD), F32), jnp.zeros((1, H_RWKV, HD_RWKV, HD_RWKV), F32), lw, min(RW_STEP, T))
    n_rows = T // PAGE_SIZE
    kc, vc = _compress(kv.reshape(n_rows, PAGE_SIZE, 4 * LANE), jnp.arange(n_rows, dtype=jnp.int32)[None], lw, transposed=False)
    o_nsa = _nsa_prompt(q, gates, attn, kc[0], vc[0], _band(rel_table), rel_table)
    y_prompt = trunk_out(xp1, o_nsa, o_rw[0], mod_p)
    kv_prompt = kv.reshape(1, 1, T, 4, N_KV, HD)
    win_prompt = win[T - min(WINDOW, T):].reshape(1, 1, -1, 2, N_KV, HD)
    shift_prompt = _rwkv_uncols(pr[T - 1]).reshape(1, 1, RWKV_COLS)

    xs1, (q_s, kv_s, win_s, gates_s, pr_s) = trunk_in(x_sample.reshape(nb * nt, D_MODEL), mod_s)
    tokens = lambda a: jnp.pad(a.reshape(nb, nt, -1), ((0, 0), (0, TOK_PAD - nt), (0, 0)))
    pr_pad = jnp.pad(pr_s.reshape(nb, nt, -1), ((0, 0), (0, RW_TOK_PAD - nt), (0, 0)))
    o_rw_s, wkv_s = _rwkv(pr_pad, _rwkv_cols(state_rwkv_shift[l])[:, None], state_rwkv_wkv[l], lw, nt)
    pool_t = jnp.transpose(cache_nsa_kv[l], (0, 2, 3, 4, 1)).reshape(-1, 4 * LANE, PAGE_SIZE)
    kc_s, vc_s = _compress(pool_t, page_table, lw, transposed=True)
    win_buf = cache_nsa_win[l]
    win_t = jnp.transpose(win_buf, (0, 2, 3, 4, 1)).reshape(nb, 2 * LANE, -1)
    o_nsa_s = _nsa_sample(pool_t, page_table, tokens(q_s), tokens(gates_s), kc_s, vc_s, tokens(kv_s),
                          win_t, tokens(win_s), rel_table, nt)
    y_sample = trunk_out(xs1, o_nsa_s[:, :nt].reshape(nb * nt, -1), o_rw_s[:, :nt].reshape(nb * nt, -1), mod_s)
    kv_sample = kv_s.reshape(1, nb, nt, 4, N_KV, HD)
    win_sample = jnp.concatenate([win_buf, win_s.reshape(nb, nt, 2, N_KV, HD)], axis=1)[None, :, nt:]
    shift_sample = _rwkv_uncols(pr_s.reshape(nb, nt, -1)[:, -1])[None]
    return (y_prompt[None], y_sample.reshape(nb, nt, D_MODEL), kv_prompt, win_prompt, shift_prompt, wkv_p[None],
            kv_sample, win_sample, shift_sample, wkv_s[None])
```

```python
import functools
import math

import numpy as np
import jax
import jax.numpy as jnp
from jax import lax
from jax.experimental import pallas as pl
from jax.experimental.pallas import tpu as pltpu

D_MODEL = 1024
PAGE_SIZE = 128
H_NSA = 8
N_KV = 2
G_NSA = H_NSA // N_KV
HD = 64
CMP_STRIDE = 16
CMP_BLOCK = 2 * CMP_STRIDE
CMP_HIDDEN = 256
SEL_BLOCK = 64
N_SEL = 16
WINDOW = 512
Q_BLOCK = 128
N_BUCKETS = 32
MAX_DISTANCE = 128
H_RWKV = 8
HD_RWKV = 64
D_RWKV = H_RWKV * HD_RWKV
DECAY_LORA = 32
AAA_LORA = 32
GATE_LORA = 96
GN_EPS = 64e-5
D_FF = 2816
LN_EPS = 1e-5
DEPTH = 1
ALPHA = (2 * DEPTH) ** 0.25

NSA_SIZES = (H_NSA * HD,) + (N_KV * HD,) * 6 + (H_NSA * 3,)
RWKV_SIZES = (D_RWKV, D_RWKV, D_RWKV, DECAY_LORA, AAA_LORA, GATE_LORA)
NSA_COLS = sum(NSA_SIZES)
RWKV_COLS = sum(RWKV_SIZES)

F32 = jnp.float32
BF16 = jnp.bfloat16
LANE = 128
NEG = -(2.0 ** 100)
M_INIT = -(2.0 ** 103)
VMEM_LIMIT = 56 * 1024 * 1024

RW_PAD = 3 * D_RWKV + 3 * LANE
P_Q, P_KV, P_WIN, P_GATE, P_RW = 0, 512, 1024, 1280, 1408
P_COLS = P_RW + RW_PAD
KEY_TILE = 512
MASK_ROWS = 16
SEL_MASK0 = HD
SEL_FAR0 = SEL_MASK0 + MASK_ROWS
V_ROWS_KV = HD + MASK_ROWS
CMP_CLASS_ROWS = 256
Q_PER_TILE = KEY_TILE // Q_BLOCK
CMP_PER_Q = Q_BLOCK // CMP_STRIDE
EDGE_ROWS = 32


def _bucket_lows():
    d = np.arange(0, 4 * MAX_DISTANCE, dtype=np.int64)
    max_exact = N_BUCKETS // 2
    df = np.maximum(d, 1).astype(np.float32)
    large = max_exact + (np.log(df / np.float32(max_exact)) / np.float32(math.log(MAX_DISTANCE / max_exact))
                         * np.float32(N_BUCKETS - max_exact)).astype(np.int32)
    b = np.where(d < max_exact, d, np.minimum(large, N_BUCKETS - 1))
    lows = [int(np.argmax(b >= k)) for k in range(N_BUCKETS)]
    return b, lows


_BUCKET_OF, _BUCKET_LOW = _bucket_lows()
FAR_DIST = _BUCKET_LOW[N_BUCKETS - 1]


def _resident(shape):
    nd = len(shape)
    return pl.BlockSpec(shape, lambda *_: (0,) * nd, pipeline_mode=pl.Buffered(1))


def _params(sem):
    return pltpu.CompilerParams(dimension_semantics=sem, vmem_limit_bytes=VMEM_LIMIT)


def _dot_exact_rhs(x, rhs_bf16, terms=2):
    acc = None
    rem = x
    for _ in range(terms):
        part = rem.astype(BF16)
        d = jnp.dot(part, rhs_bf16, preferred_element_type=F32)
        acc = d if acc is None else acc + d
        rem = rem - part.astype(F32)
    return acc


def _dot_exact_lhs(lhs_bf16, x, terms=3):
    acc = None
    rem = x
    for _ in range(terms):
        part = rem.astype(BF16)
        d = jnp.dot(lhs_bf16, part, preferred_element_type=F32)
        acc = d if acc is None else acc + d
        rem = rem - part.astype(F32)
    return acc


def _layer_norm(y, g, b):
    mu = jnp.mean(y, axis=-1, keepdims=True)
    yc = y - mu
    var = jnp.mean(yc * yc, axis=-1, keepdims=True)
    return yc * lax.rsqrt(var + LN_EPS) * g + b


def _bias_chain(d, tab_rows):
    out = tab_rows[0] + jnp.zeros(d.shape, F32)
    for b in range(1, N_BUCKETS):
        out = jnp.where(d >= _BUCKET_LOW[b], tab_rows[b], out)
    return out


def _ada_kernel(c_ref, w_ref, b_ref, o_ref):
    c = c_ref[...]
    h = (c * jax.nn.sigmoid(c)).astype(BF16)
    o_ref[...] = jnp.dot(h, w_ref[...].astype(BF16), preferred_element_type=F32) + b_ref[...]


def _ada(c_all, w_ada, b_ada):
    rows, n = c_all.shape[0], w_ada.shape[1]
    tn = 1152
    return pl.pallas_call(
        _ada_kernel,
        grid=(n // tn,),
        in_specs=[pl.BlockSpec((rows, D_MODEL), lambda j: (0, 0)),
                  pl.BlockSpec((D_MODEL, tn), lambda j: (0, j)),
                  pl.BlockSpec((1, tn), lambda j: (0, j))],
        out_specs=pl.BlockSpec((rows, tn), lambda j: (0, j)),
        out_shape=jax.ShapeDtypeStruct((rows, n), F32),
        compiler_params=_params(("arbitrary",)),
        name="ada",
    )(c_all, w_ada, b_ada.reshape(1, n))


FF_CHUNKS = 2


def _ffn_block(x, shift, scale, gate, ln_g, ln_b, wg_ref, wu_ref, wd_ref):
    h = (x * (1.0 + scale) + shift).astype(BF16)
    ck = D_FF // FF_CHUNKS
    acc = jnp.zeros(x.shape, F32)
    for c in range(FF_CHUNKS):
        a = jnp.dot(h, wg_ref[:, c * ck:(c + 1) * ck], preferred_element_type=F32)
        b = jnp.dot(h, wu_ref[:, c * ck:(c + 1) * ck], preferred_element_type=F32)
        t = (a * jax.nn.sigmoid(a) * b).astype(BF16)
        acc = acc + jnp.dot(t, wd_ref[c * ck:(c + 1) * ck, :], preferred_element_type=F32)
    y = ALPHA * x + (1.0 + gate) * (0.5 * acc)
    return _layer_norm(y, ln_g, ln_b)


def _ffn_kernel(x_ref, sh_ref, sc_ref, gt_ref, lng_ref, lnb_ref, wg_ref, wu_ref, wd_ref, o_ref):
    o_ref[...] = _ffn_block(x_ref[...], sh_ref[...], sc_ref[...], gt_ref[...], lng_ref[...], lnb_ref[...], wg_ref, wu_ref, wd_ref)


def _mod_spec(mod, tm):
    if mod.shape[0] == 1:
        return pl.BlockSpec((1, D_MODEL), lambda i: (0, 0))
    return pl.BlockSpec((tm, D_MODEL), lambda i: (i, 0))


def _ffn(x, shift, scale, gate, ln_g, ln_b, wg, wu, wd):
    rows = x.shape[0]
    tm = min(512, rows)
    row = lambda i: (i, 0)
    return pl.pallas_call(
        _ffn_kernel,
        grid=(rows // tm,),
        in_specs=[pl.BlockSpec((tm, D_MODEL), row), _mod_spec(shift, tm), _mod_spec(scale, tm), _mod_spec(gate, tm),
                  _resident((1, D_MODEL)), _resident((1, D_MODEL)),
                  _resident((D_MODEL, D_FF)), _resident((D_MODEL, D_FF)), _resident((D_FF, D_MODEL))],
        out_specs=pl.BlockSpec((tm, D_MODEL), row),
        out_shape=jax.ShapeDtypeStruct((rows, D_MODEL), F32),
        compiler_params=_params(("arbitrary",)),
        name="ffn",
    )(x, shift, scale, gate, ln_g.reshape(1, -1), ln_b.reshape(1, -1), wg, wu, wd)


def _proj_kernel(x_ref, sh_ref, sc_ref, w_ref, q_ref, kv_ref, win_ref, g_ref, pr_ref, *attn_refs):
    h = (x_ref[...] * (1.0 + sc_ref[...]) + sh_ref[...]).astype(BF16)
    p = jnp.dot(h, w_ref[...], preferred_element_type=F32)
    q_ref[...] = p[:, P_Q:P_KV]
    kv_ref[...] = p[:, P_KV:P_WIN]
    win_ref[...] = p[:, P_WIN:P_GATE]
    g_ref[...] = jax.nn.sigmoid(p[:, P_GATE:P_RW])
    pr_ref[...] = p[:, P_RW:P_COLS]
    if attn_refs:
        ks_ref, vst_ref, kw_ref, vwt_ref = attn_refs
        tm = x_ref.shape[0]
        lane = lax.broadcasted_iota(jnp.int32, (1, LANE), 1)
        key = pl.program_id(0) * tm + lax.broadcasted_iota(jnp.int32, (tm, 1), 0)
        slot = lane - HD
        pattern = (((key // SEL_BLOCK) % MASK_ROWS == slot) | ((slot >= MASK_ROWS) & (slot < MASK_ROWS + 2))).astype(F32)
        k_sel = p[:, P_KV + 2 * LANE:P_KV + 3 * LANE]
        ks_ref[0] = jnp.where(lane < HD, k_sel, pattern).astype(BF16)
        ks_ref[1] = jnp.where(lane < HD, pltpu.roll(k_sel, HD, axis=1), pattern).astype(BF16)
        tail = jnp.concatenate([jnp.ones((1, tm), F32), jnp.zeros((MASK_ROWS - 1, tm), F32)], axis=0)
        v_sel_t = p[:, P_KV + 3 * LANE:P_WIN].T
        v_win_t = p[:, P_WIN + LANE:P_GATE].T
        kw_ref[...] = p[:, P_WIN:P_WIN + LANE].astype(BF16)
        for hd in range(N_KV):
            vst_ref[hd, 0] = jnp.concatenate([v_sel_t[hd * HD:(hd + 1) * HD], tail], axis=0).astype(BF16)
            win_rows = jnp.concatenate([v_win_t[hd * HD:(hd + 1) * HD], tail], axis=0).astype(BF16)
            for j in range(tm // Q_BLOCK):
                vwt_ref[hd, j] = win_rows[:, j * Q_BLOCK:(j + 1) * Q_BLOCK]


def _proj(x, shift, scale, w_in_p, attn_operands=False):
    rows = x.shape[0]
    tm = min(KEY_TILE, rows)
    row = lambda i: (i, 0)
    widths = (512, 512, 256, LANE, RW_PAD)
    out_specs = [pl.BlockSpec((tm, w), row) for w in widths]
    out_shape = [jax.ShapeDtypeStruct((rows, w), F32) for w in widths]
    if attn_operands:
        assert tm == KEY_TILE
        per_q = tm // Q_BLOCK
        out_specs += [pl.BlockSpec((N_KV, tm, LANE), lambda i: (0, i, 0)), pl.BlockSpec((N_KV, 1, V_ROWS_KV, tm), lambda i: (0, i, 0, 0)),
                      pl.BlockSpec((tm, LANE), row), pl.BlockSpec((N_KV, per_q, V_ROWS_KV, Q_BLOCK), lambda i: (0, i, 0, 0))]
        out_shape += [jax.ShapeDtypeStruct((N_KV, rows, LANE), BF16), jax.ShapeDtypeStruct((N_KV, rows // tm, V_ROWS_KV, tm), BF16),
                      jax.ShapeDtypeStruct((rows, LANE), BF16), jax.ShapeDtypeStruct((N_KV, rows // Q_BLOCK, V_ROWS_KV, Q_BLOCK), BF16)]
    return pl.pallas_call(
        _proj_kernel,
        grid=(rows // tm,),
        in_specs=[pl.BlockSpec((tm, D_MODEL), row), _mod_spec(shift, tm), _mod_spec(scale, tm),
                  _resident((D_MODEL, P_COLS))],
        out_specs=out_specs,
        out_shape=out_shape,
        compiler_params=_params(("arbitrary",)),
        name="proj",
    )(x, shift, scale, w_in_p)


def _prep_w_in(w_in):
    pad = lambda a, n: jnp.pad(a, ((0, 0), (0, n - a.shape[1])))
    nsa, rw = w_in[:, :NSA_COLS], w_in[:, NSA_COLS:]
    gl = nsa[:, 1280:1304].reshape(D_MODEL, H_NSA, 3).transpose(0, 2, 1).reshape(D_MODEL, 3 * H_NSA)
    cols = [nsa[:, :1280], pad(gl, LANE), _rwkv_cols(rw)]
    return jnp.concatenate(cols, axis=1).astype(BF16)


def _rwkv_cols(a):
    pad = lambda t: jnp.pad(t, [(0, 0)] * (t.ndim - 1) + [(0, LANE - t.shape[-1])])
    n = 3 * D_RWKV
    return jnp.concatenate([a[..., :n], pad(a[..., n:n + 32]), pad(a[..., n + 32:n + 64]), pad(a[..., n + 64:n + 160])], axis=-1)


def _rwkv_uncols(a):
    n = 3 * D_RWKV
    return jnp.concatenate([a[..., :n], a[..., n:n + 32], a[..., n + LANE:n + LANE + 32], a[..., n + 2 * LANE:n + 2 * LANE + 96]], axis=-1)


RW_GROUP = 64
RW_TOK_PAD = 16
RW_PAIRS = H_RWKV // 2
RW_BLOCK = 64
RW_STEP = 256


def _lora(x, w_ref):
    w = w_ref[...]
    w_hi = w.astype(BF16)
    w_lo = (w - w_hi.astype(F32)).astype(BF16)
    return _dot_exact_rhs(x, w_hi) + jnp.dot(x.astype(BF16), w_lo, preferred_element_type=F32)


def _rwkv_kernel(pr_ref, sh0_ref, s0_ref, mu_ref, w0_ref, a0_ref, kk_ref, ka_ref, rk_ref, gw_ref, gb_ref,
                 w2_ref, a2_ref, g2_ref, bo_ref, lgrp_ref, ggrp_ref, o_ref, sout_ref,
                 prev_scr, s_scr, *, n_valid):
    tb = pr_ref.shape[0]
    step = pl.program_id(1)

    @pl.when(step == 0)
    def _():
        prev_scr[...] = sh0_ref[...]
        s_scr[...] = s0_ref[...]

    p = pr_ref[...]
    rows = lax.broadcasted_iota(jnp.int32, (tb, 1), 0)
    prev = jnp.where(rows == 0, prev_scr[...], pltpu.roll(p, 1, axis=0))
    prev_scr[...] = p[tb - 1:tb, :]
    xs = p + (prev - p) * mu_ref[...]
    n = D_RWKV
    r, k, v = xs[:, :n], xs[:, n:2 * n], xs[:, 2 * n:3 * n]

    def head_sums(x):
        return jnp.concatenate([_dot_exact_rhs(x[:, :n // 2], bo_ref[...]), _dot_exact_rhs(x[:, n // 2:], bo_ref[...])], axis=1)

    wl, al, gl = xs[:, 3 * n:3 * n + LANE], xs[:, 3 * n + LANE:3 * n + 2 * LANE], xs[:, 3 * n + 2 * LANE:]
    z = -(w0_ref[...] + _lora(jnp.tanh(wl), w2_ref))
    w = -(jnp.maximum(z, 0.0) + jnp.log(1.0 + jnp.exp(-jnp.abs(z)))) - 0.5
    a = jax.nn.sigmoid(a0_ref[...] + _lora(al, a2_ref))
    g = _lora(jax.nn.sigmoid(gl), g2_ref)
    kk = k * kk_ref[...]
    ss = head_sums(kk * kk)
    kk = kk / jnp.maximum(jnp.sqrt(ss), 1e-12)
    k2 = k * (1.0 + (a - 1.0) * ka_ref[...])
    G = min(RW_GROUP, tb)
    log_dec = -jnp.exp(w)
    bet = kk * a
    if n_valid < tb:
        live = rows < n_valid
        log_dec, kk, bet, k2, v_in = (jnp.where(live, x, 0.0) for x in (log_dec, kk, bet, k2, v))
    else:
        v_in = v
    cum = _dot_exact_lhs(lgrp_ref[...], log_dec, terms=2)
    cum_end = _dot_exact_lhs(ggrp_ref[...], log_dec, terms=2)
    gam_inv = jnp.exp(-cum)
    gam_end = jnp.exp(cum_end - cum)
    k_hat = -kk * jnp.exp(cum - log_dec)
    r_hat = r * jnp.exp(cum)
    b_chk, k_chk = bet * gam_inv, k2 * gam_inv
    b_til, k_til = bet * gam_end, k2 * gam_end
    gam_group = jnp.exp(cum_end)

    lane = lax.broadcasted_iota(jnp.int32, (1, LANE), 1)
    low = lane < HD_RWKV
    lane_t = lane & (RW_BLOCK - 1)
    row = lax.broadcasted_iota(jnp.int32, (LANE, 1), 0)
    row_t = row & (RW_BLOCK - 1)
    same = ((row < HD_RWKV) == low) & ((row_t // G) == (lane_t // G))
    strict, incl = same & (lane_t < row_t), same & (lane_t <= row_t)
    bf = lambda x: x.astype(BF16)
    mm = lambda x, y: jnp.dot(bf(x), bf(y), preferred_element_type=F32)
    mm_nt = lambda x, y: lax.dot_general(bf(x), bf(y), (((1,), (1,)), ((), ())), preferred_element_type=F32)

    def rows_bd(x):
        if tb < RW_BLOCK:
            x = jnp.concatenate([x, jnp.zeros((RW_BLOCK - tb, LANE), F32)], axis=0)
        return jnp.concatenate([jnp.where(low, x, 0.0), jnp.where(low, 0.0, x)], axis=0)

    def mm3(x, y):
        xh, yh = bf(x), bf(y)
        xl, yl = bf(x - xh.astype(F32)), bf(y - yh.astype(F32))
        return jnp.dot(jnp.concatenate([xh, xl, xh], axis=1), jnp.concatenate([yh, yh, yl], axis=0), preferred_element_type=F32)

    units = max(1, tb // RW_BLOCK)
    unit_rows = min(tb, RW_BLOCK)
    pairs = range(RW_PAIRS)
    items = [(un, pp) for un in range(units) for pp in pairs]
    at = lambda x, it: x[it[0] * unit_rows:(it[0] + 1) * unit_rows, it[1] * LANE:(it[1] + 1) * LANE]
    kh_row = [rows_bd(at(k_hat, it)) for it in items]
    rh_row = [rows_bd(at(r_hat, it)) for it in items]
    kh_mat = [x.T for x in kh_row]
    rh_mat = [x.T for x in rh_row]
    state_in = [jnp.concatenate([rows_bd(at(b_chk, it)).T, rows_bd(at(k_chk, it)).T], axis=1) for it in items]
    upd_rows = [jnp.concatenate([rows_bd(at(b_til, it)), rows_bd(at(k_til, it))], axis=0) for it in items]
    v_t = [rows_bd(at(v_in, it)).T for it in items]
    v_t = [x[:HD_RWKV] + x[HD_RWKV:] for x in v_t]
    c_all = [mm(jnp.concatenate([kh_row[n], rh_row[n]], axis=0), state_in[n]) for n in range(len(items))]
    c_uu = [jnp.where(strict, c[:LANE, :LANE], 0.0) for c in c_all]
    c_uv = [jnp.where(strict, c[:LANE, LANE:], 0.0) for c in c_all]
    c_ru = [jnp.where(incl, c[LANE:, :LANE], 0.0) for c in c_all]
    c_rv = [jnp.where(incl, c[LANE:, LANE:], 0.0) for c in c_all]
    t_neu, power = list(c_uu), list(c_uu)
    span = 2
    while span < G:
        power = [mm(x, x) for x in power]
        t_neu = [t_neu[n] + power[n] + mm(t_neu[n], power[n]) for n in range(len(items))]
        span *= 2
    from_v = [mm_nt(v_t[n], c_uv[n]) for n in range(len(items))]
    st = [s_scr[pp] for pp in pairs]
    y_units = []
    for un in range(units):
        ns = [un * RW_PAIRS + pp for pp in pairs]
        y_t = [jnp.zeros((HD_RWKV, LANE), F32) for _ in pairs]
        for grp in range(unit_rows // G):
            here = (lane_t // G) == grp
            first = un * unit_rows + grp * G
            w_t = [jnp.where(here, mm(st[pp], kh_mat[ns[pp]]) + from_v[ns[pp]], 0.0) for pp in pairs]
            u_t = [w_t[pp] + mm_nt(w_t[pp], t_neu[ns[pp]]) for pp in pairs]
            v_g = [jnp.where(here, v_t[ns[pp]], 0.0) for pp in pairs]
            y_t = [y_t[pp] + jnp.where(here, mm(st[pp], rh_mat[ns[pp]]), 0.0) + mm_nt(u_t[pp], c_ru[ns[pp]])
                   + mm_nt(v_g[pp], c_rv[ns[pp]]) for pp in pairs]
            st = [st[pp] * gam_group[first:first + 1, pp * LANE:(pp + 1) * LANE]
                  + mm3(jnp.concatenate([u_t[pp], v_g[pp]], axis=1), upd_rows[ns[pp]]) for pp in pairs]
        lane_u = lax.broadcasted_iota(jnp.int32, (unit_rows, LANE), 1)
        pieces = []
        for pp in pairs:
            yt = jnp.concatenate([y_t[pp], jnp.zeros((LANE - HD_RWKV, LANE), F32)], axis=0).T
            pieces.append(jnp.where(lane_u < HD_RWKV, yt[:unit_rows], pltpu.roll(yt[RW_BLOCK:RW_BLOCK + unit_rows], HD_RWKV, axis=1)))
        y_units.append(jnp.concatenate(pieces, axis=1))
    for pp in pairs:
        s_scr[pp] = st[pp]
    y = y_units[0] if units == 1 else jnp.concatenate(y_units, axis=0)
    mean = head_sums(y) * (1.0 / HD_RWKV)
    yc = y - mean
    var = head_sums(yc * yc) * (1.0 / HD_RWKV)
    yn = yc * lax.rsqrt(var + GN_EPS) * gw_ref[...] + gb_ref[...]
    bonus = head_sums(r * k2 * rk_ref[...]) * v
    o_ref[...] = (yn + bonus) * g
    sout_ref[...] = s_scr[...]


def _pair_state(s):
    B = s.shape[0]
    return s.reshape(B, RW_PAIRS, 2, HD_RWKV, HD_RWKV).transpose(0, 1, 3, 2, 4).reshape(B, RW_PAIRS, HD_RWKV, LANE)


def _unpair_state(s):
    B = s.shape[0]
    return s.reshape(B, RW_PAIRS, HD_RWKV, 2, HD_RWKV).transpose(0, 1, 3, 2, 4).reshape(B, H_RWKV, HD_RWKV, HD_RWKV)


def _rwkv(pr, shift0, s0, lw, n_valid):
    B, T, _ = pr.shape
    tb = min(RW_STEP, T)
    n = D_RWKV
    vec = lambda a: a.reshape(1, n)
    padrow = lambda a: jnp.pad(a, ((0, LANE - a.shape[0]), (0, 0)))
    blk = np.arange(n // 2) // HD_RWKV
    block_ones = jnp.asarray(blk[:, None] == blk[None, :], BF16)
    tok = np.arange(tb)
    group = min(RW_GROUP, tb)
    same_group = tok[:, None] // group == tok[None, :] // group
    prefix = jnp.asarray(same_group & (tok[None, :] <= tok[:, None]), BF16)
    consts = [_rwkv_cols(lw['rwkv_mu']).reshape(1, RW_PAD), vec(lw['rwkv_w0']), vec(lw['rwkv_a0']), vec(lw['rwkv_k_k']),
              vec(lw['rwkv_k_a']), vec(lw['rwkv_r_k']), vec(lw['rwkv_gn_w']), vec(lw['rwkv_gn_b']),
              padrow(lw['rwkv_w2']), padrow(lw['rwkv_a2']), padrow(lw['rwkv_g2']), block_ones, prefix, jnp.asarray(same_group, BF16)]
    kern = functools.partial(_rwkv_kernel, n_valid=n_valid)
    state_spec = pl.BlockSpec((None, RW_PAIRS, HD_RWKV, LANE), lambda b, j: (b, 0, 0, 0))
    o, s = pl.pallas_call(
        kern,
        grid=(B, T // tb),
        in_specs=[pl.BlockSpec((None, tb, RW_PAD), lambda b, j: (b, j, 0)),
                  pl.BlockSpec((None, 1, RW_PAD), lambda b, j: (b, 0, 0)), state_spec]
                 + [_resident(c.shape) for c in consts],
        out_specs=[pl.BlockSpec((None, tb, n), lambda b, j: (b, j, 0)), state_spec],
        out_shape=[jax.ShapeDtypeStruct((B, T, n), F32), jax.ShapeDtypeStruct((B, RW_PAIRS, HD_RWKV, LANE), F32)],
        scratch_shapes=[pltpu.VMEM((1, RW_PAD), F32), pltpu.VMEM((RW_PAIRS, HD_RWKV, LANE), F32)],
        compiler_params=_params(("arbitrary", "arbitrary")),
        name="rwkv",
    )(pr, shift0, _pair_state(s0), *consts)
    return o, _unpair_state(s)


CMP_PAGES = 64
CHUNKS_PER_PAGE = PAGE_SIZE // CMP_STRIDE


def _compress_kernel(pt_ref, *refs, n_pages, transposed):
    pages, nxt = refs[:n_pages], refs[n_pages]
    weights = refs[n_pages + 1:n_pages + 9]
    outs = refs[n_pages + 9:n_pages + 11]
    width = N_KV * HD
    rows = CHUNKS_PER_PAGE * n_pages
    seg = rows + 8
    kinds = range(2)
    low = lax.broadcasted_iota(jnp.int32, (1, N_KV * HD), 1) < HD
    rows_scr = refs[n_pages + 11:n_pages + 13]

    def by_head(row_s):
        heads = [[], []]
        for s in range(0, CMP_STRIDE, 2):
            a, b = row_s(s), row_s(s + 1)
            heads[0].append(jnp.where(low, a, pltpu.roll(b, HD, axis=1)))
            heads[1].append(jnp.where(low, pltpu.roll(a, HD, axis=1), b))
        return [jnp.concatenate(h, axis=1) for h in heads]

    for kind in kinds:
        pe_ref, w_ref, b_ref, w2_ref = weights[4 * kind:4 * kind + 4]
        part = slice(kind * width, (kind + 1) * width)
        for p, pg in enumerate(pages):
            rows_scr[kind][p * PAGE_SIZE:(p + 1) * PAGE_SIZE, :] = pg[part, :].T if transposed else pg[:, part]
        nxt_rows = nxt[part, :].T[:CMP_STRIDE] if transposed else nxt[:, part]
        x = by_head(lambda s: rows_scr[kind][pl.ds(s, rows, stride=CMP_STRIDE), :])
        x_next = by_head(lambda s: jnp.broadcast_to(nxt_rows[s:s + 1, :], (8, N_KV * HD)))
        x_all = jnp.concatenate([x[0], x_next[0], x[1], x_next[1]], axis=0)
        h_first = jnp.dot((x_all + pe_ref[0]).astype(BF16), w_ref[0], preferred_element_type=F32)
        h_second = jnp.dot((x_all + pe_ref[1]).astype(BF16), w_ref[1], preferred_element_type=F32)
        out = None
        for h in range(N_KV):
            h_next = pltpu.roll(h_second[h * seg:(h + 1) * seg], seg - 1, axis=0)[:rows]
            hidden = jax.nn.gelu(h_first[h * seg:h * seg + rows] + h_next + b_ref[...])
            part = jnp.dot(hidden.astype(BF16), w2_ref[h], preferred_element_type=F32)
            out = part if out is None else out + part
        outs[kind][...] = out


def _compress_weights(pe, w1, b1, w2):
    n = CMP_STRIDE * HD
    pe2 = pe.reshape(2, 1, n)
    w_halves = w1.reshape(2, n, CMP_HIDDEN).astype(BF16)
    zero = jnp.zeros_like(w2)
    w2_heads = jnp.stack([jnp.concatenate([w2, zero], axis=1), jnp.concatenate([zero, w2], axis=1)]).astype(BF16)
    return [pe2, w_halves, b1.reshape(1, -1), w2_heads]


def _compress(pool, page_table, lw, transposed):
    B, n_pages_total = page_table.shape
    n_pages = min(CMP_PAGES, n_pages_total)
    rows = CHUNKS_PER_PAGE * n_pages
    weights = (_compress_weights(lw['cmp_pe_k'], lw['cmp_w1_k'], lw['cmp_b1_k'], lw['cmp_w2_k'])
               + _compress_weights(lw['cmp_pe_v'], lw['cmp_w1_v'], lw['cmp_b1_v'], lw['cmp_w2_v']))
    width = N_KV * HD

    def page_map(p):
        return lambda b, j, pt: (pt[b, j * n_pages + p], 0, 0)

    next_map = lambda b, j, pt: (pt[b, jnp.minimum((j + 1) * n_pages, n_pages_total - 1)], 0, 0)
    page_block = (None, 2 * width, PAGE_SIZE) if transposed else (None, PAGE_SIZE, 2 * width)
    next_block = page_block if transposed else (None, CMP_STRIDE, 2 * width)
    const = lambda a: pl.BlockSpec(a.shape, lambda b, j, pt: (0,) * a.ndim)
    out_spec = pl.BlockSpec((None, rows, N_KV * HD), lambda b, j, pt: (b, j, 0))
    out_shape = jax.ShapeDtypeStruct((B, n_pages_total * CHUNKS_PER_PAGE, N_KV * HD), F32)
    grid_spec = pltpu.PrefetchScalarGridSpec(
        num_scalar_prefetch=1,
        grid=(B, n_pages_total // n_pages),
        in_specs=[pl.BlockSpec(page_block, page_map(p)) for p in range(n_pages)]
                 + [pl.BlockSpec(next_block, next_map)] + [const(a) for a in weights],
        out_specs=[out_spec, out_spec],
        scratch_shapes=[pltpu.VMEM((n_pages * PAGE_SIZE, width), F32)] * 2,
    )
    return pl.pallas_call(
        functools.partial(_compress_kernel, n_pages=n_pages, transposed=transposed),
        grid_spec=grid_spec,
        out_shape=[out_shape, out_shape],
        compiler_params=_params(("arbitrary", "arbitrary")),
        name="compress",
    )(page_table, *([pool] * (n_pages + 1)), *weights)


BAND_ROWS = 1152


def _band_kernel(tab_ref, bkt_ref, o_ref):
    h = pl.program_id(0)
    bkt = bkt_ref[...]
    out = jnp.full(bkt.shape, NEG, F32)
    for b in range(N_BUCKETS):
        out = jnp.where(bkt == b, tab_ref[b, h], out)
    o_ref[...] = out


def _band(rel_table):
    u = np.arange(BAND_ROWS)[:, None]
    qi = np.arange(Q_BLOCK)[None, :]
    d = qi + WINDOW - u
    bkt = np.where(d >= 0, _BUCKET_OF[np.clip(d, 0, len(_BUCKET_OF) - 1)], -1).astype(np.int32)
    return pl.pallas_call(
        _band_kernel,
        grid=(H_NSA,),
        in_specs=[pl.BlockSpec(memory_space=pltpu.SMEM), pl.BlockSpec((BAND_ROWS, Q_BLOCK), lambda h: (0, 0))],
        out_specs=pl.BlockSpec((None, BAND_ROWS, Q_BLOCK), lambda h: (h, 0, 0)),
        out_shape=jax.ShapeDtypeStruct((H_NSA, BAND_ROWS, Q_BLOCK), F32),
        compiler_params=_params(("arbitrary",)),
        name="band",
    )(rel_table, jnp.asarray(bkt))


def _softmax_cols(s):
    m = jnp.max(s, axis=0, keepdims=True)
    e = jnp.exp(s - m)
    l = jnp.sum(e, axis=0, keepdims=True)
    return e * jnp.where(m > 0.5 * NEG, 1.0 / l, 0.0)


def _select_blocks(impsel, qpos, n_pick):
    ns = impsel.shape[0]
    blk = lax.broadcasted_iota(jnp.int32, impsel.shape, 0)
    cur = jnp.right_shift(qpos, SEL_BLOCK.bit_length() - 1)
    future = blk * SEL_BLOCK > qpos
    forced = (blk == 0) | (blk == cur) | (blk == cur - 1)
    score = jnp.where(future, -jnp.inf, jnp.where(forced, jnp.inf, impsel))
    chosen = jnp.zeros(impsel.shape, F32)
    for _ in range(n_pick):
        best = jnp.max(score, axis=0, keepdims=True)
        first = jnp.min(jnp.where(score == best, blk, ns), axis=0, keepdims=True)
        hit = (blk == first) & (best > -jnp.inf)
        chosen = jnp.where(hit, 1.0, chosen)
        score = jnp.where(hit, -jnp.inf, score)
    return jnp.where(chosen > 0.0, 0.0, NEG)


def _pool_matrix(ns, nc):
    j = np.arange(ns)[:, None]
    n = np.arange(nc)[None, :]
    ratio = SEL_BLOCK // CMP_STRIDE
    return jnp.asarray((n >= ratio * j - 1) & (n <= ratio * j + ratio - 1), BF16)


def _nsa_prompt_kernel(tab_ref, q_ref, g_ref, kc_ref, vct_ref, ks_ref, vst_ref, kw_ref, vwt_ref, band_ref, pool_ref, o_ref,
                       rhs_scr, mask_scr, acc_scr, m_scr, sc_scr, sa_scr, sb_scr, oc_scr):
    i = pl.program_id(0)
    ncp = kc_ref.shape[0]
    ns = pool_ref.shape[0]
    s0 = i * Q_BLOCK
    q_t = (q_ref[...] * HD ** -0.5).T
    g_t = g_ref[...].T
    lane_q = lax.broadcasted_iota(jnp.int32, (1, G_NSA * Q_BLOCK), 1) & (Q_BLOCK - 1)
    qpos = s0 + lax.broadcasted_iota(jnp.int32, (1, Q_BLOCK), 1)
    rhs_scr[...] = jnp.zeros(rhs_scr.shape, BF16)
    kvs = range(N_KV)
    kd = i // Q_PER_TILE
    r = i % Q_PER_TILE
    lanes4 = lambda k, f: jnp.concatenate([f(G_NSA * k + g) for g in range(G_NSA)], axis=1)
    qcols = [lanes4(k, lambda h: q_t[h * HD:(h + 1) * HD, :]) for k in kvs]
    zero = jnp.zeros_like(qcols[0])
    top = [jnp.concatenate([qcols[0], zero], axis=0).astype(BF16), jnp.concatenate([zero, qcols[1]], axis=0).astype(BF16)]
    far_row = [lanes4(k, lambda h: band_ref[h, 0:1, :]) for k in kvs]


    n0 = pl.multiple_of(jnp.clip(CMP_PER_Q * i - EDGE_ROWS // 2, 0, ncp - EDGE_ROWS), 8)
    nrow = lax.broadcasted_iota(jnp.int32, (ncp, 1), 0)
    d_edge = qpos - (CMP_STRIDE * (n0 + lax.broadcasted_iota(jnp.int32, (EDGE_ROWS, 1), 0)) + CMP_BLOCK - 1)

    def compressed(k, rows):
        sc_scr[k, 0:rows, :] = (jnp.dot(kc_ref[0:rows, :], top[k], preferred_element_type=F32)
                                + jnp.where(nrow[0:rows] < n0, far_row[k], NEG))
        edge_bias = lanes4(k, lambda h: jnp.where(d_edge >= 0, _bias_chain(d_edge, [tab_ref[b, h] for b in range(N_BUCKETS)]), NEG))
        sc_scr[k, pl.ds(n0, EDGE_ROWS), :] = jnp.dot(kc_ref[pl.ds(n0, EDGE_ROWS), :], top[k], preferred_element_type=F32) + edge_bias
        p_c = _softmax_cols(sc_scr[k, 0:rows, :])
        o_c = jnp.dot(vct_ref[:, 0:rows], p_c.astype(BF16), preferred_element_type=F32)[k * HD:(k + 1) * HD]
        imp = p_c[:, 0:Q_BLOCK]
        for g in range(1, G_NSA):
            imp = imp + p_c[:, g * Q_BLOCK:(g + 1) * Q_BLOCK]
        n_blk = rows * CMP_STRIDE // SEL_BLOCK
        return o_c, _dot_exact_lhs(pool_ref[0:n_blk, 0:rows], imp, terms=2)

    size_step = min(CMP_CLASS_ROWS, ncp)
    size_class = (n0 + EDGE_ROWS - 1) // size_step
    for cls in range(ncp // size_step):
        @pl.when(size_class == cls)
        def _(rows=(cls + 1) * size_step):
            comp = [compressed(k, rows) for k in kvs]
            n_blk = comp[0][1].shape[0]
            masks = [_select_blocks(comp[k][1], qpos, min(N_SEL, ns)) for k in kvs]
            for k in kvs:
                oc_scr[k] = comp[k][0]
                full = jnp.concatenate([masks[k], jnp.full((ns - n_blk, Q_BLOCK), NEG, F32)], axis=0) if n_blk < ns else masks[k]
                mask_scr[k] = jnp.concatenate([full.astype(BF16)] * G_NSA, axis=1)

    ws = pl.multiple_of(jnp.maximum(s0 - WINDOW, 0), Q_BLOCK)
    u0 = pl.multiple_of(WINDOW - (s0 - ws), Q_BLOCK)
    n_win = WINDOW + Q_BLOCK
    u = u0 + lax.broadcasted_iota(jnp.int32, (n_win, 1), 0)
    win_mask = jnp.where(u > lane_q, 0.0, NEG)

    def window(k):
        s_w = (jnp.dot(kw_ref[pl.ds(ws, n_win), :], top[k], preferred_element_type=F32)
               + lanes4(k, lambda h: band_ref[h, pl.ds(u0, n_win), :]) + win_mask)
        m_w = jnp.max(s_w, axis=0, keepdims=True)
        p_w = jnp.exp(s_w - m_w).astype(BF16)
        acc_w = jnp.zeros((V_ROWS_KV, G_NSA * Q_BLOCK), F32)
        for j in range(n_win // Q_BLOCK):
            acc_w = acc_w + jnp.dot(vwt_ref[k, ws // Q_BLOCK + j], p_w[j * Q_BLOCK:(j + 1) * Q_BLOCK], preferred_element_type=F32)
        return acc_w[0:HD] / acc_w[HD:HD + 1]

    o_w = [window(k) for k in kvs]

    for k in kvs:
        m_scr[k] = jnp.full(m_scr.shape[1:], M_INIT, F32)
        acc_scr[k] = jnp.zeros(acc_scr.shape[1:], F32)
        far_hi = far_row[k].astype(BF16).astype(F32)
        rhs_scr[k, 0:HD, :] = qcols[k].astype(BF16)
        rhs_scr[k, SEL_FAR0:SEL_FAR0 + MASK_ROWS, :] = jnp.concatenate(
            [far_hi, far_row[k] - far_hi, jnp.zeros((MASK_ROWS - 2, G_NSA * Q_BLOCK), F32)], axis=0).astype(BF16)

    def scores(k, slab, kts, extra):
        rhs_scr[k, SEL_MASK0:SEL_MASK0 + MASK_ROWS, :] = mask_scr[k, pl.ds(pl.multiple_of(slab * MASK_ROWS, MASK_ROWS), MASK_ROWS), :]
        rhs = rhs_scr[k]
        out = []
        for kt, add in zip(kts, extra):
            s = jnp.dot(ks_ref[k, pl.ds(pl.multiple_of(kt * KEY_TILE, KEY_TILE), KEY_TILE), :], rhs, preferred_element_type=F32)
            out.append(s if add is None else s + add)
        return out

    def update(k, kts, tiles):
        m_old = m_scr[k]
        m_new = m_old
        for s in tiles:
            m_new = jnp.maximum(m_new, jnp.max(s, axis=0, keepdims=True))
        acc = jnp.exp(m_old - m_new) * acc_scr[k]
        for kt, s in zip(kts, tiles):
            acc = acc + jnp.dot(vst_ref[k, kt], jnp.exp(s - m_new).astype(BF16), preferred_element_type=F32)
        acc_scr[k] = acc
        m_scr[k] = m_new

    def attend(slab, kts, extra):
        tiles = [scores(k, slab, kts, extra(k)) for k in kvs]
        for k in kvs:
            update(k, kts, tiles[k])

    near_at = lambda k, start: lanes4(k, lambda h: band_ref[h, pl.ds(pl.multiple_of(start, Q_BLOCK), KEY_TILE), :]) - far_row[k]
    prev_near = (r == 0) & (kd >= 1)
    kd_odd = (kd & 1) == 1
    even_prev = jnp.logical_not(kd_odd) & prev_near
    n_pairs = kd // 2 - even_prev.astype(jnp.int32)
    n_quads = n_pairs // 2
    no_bias = lambda k: [None, None]

    def pair_scores(k, dst, pair):
        lo, hi = scores(k, pair, [2 * pair, 2 * pair + 1], [None, None])
        dst[k, 0:KEY_TILE, :] = lo
        dst[k, KEY_TILE:, :] = hi

    def pair_update(k, src, pair):
        update(k, [2 * pair, 2 * pair + 1], [src[k, 0:KEY_TILE, :], src[k, KEY_TILE:, :]])

    @pl.when(n_quads > 0)
    def _():
        for k in kvs:
            pair_scores(k, sa_scr, 0)

    for k in kvs:
        def quad_body(qd, carry, k=k):
            first = 2 * qd
            pair_scores(k, sb_scr, first + 1)
            pair_update(k, sa_scr, first)
            pair_scores(k, sa_scr, jnp.minimum(first + 2, 2 * n_quads - 2))
            pair_update(k, sb_scr, first + 1)
            return carry

        lax.fori_loop(0, n_quads, quad_body, 0)

    @pl.when((n_pairs & 1) == 1)
    def _():
        attend(n_pairs - 1, [2 * n_pairs - 2, 2 * n_pairs - 1], no_bias)

    @pl.when(kd_odd)
    def _():
        attend(kd // 2, [kd - 1, kd], lambda k: [jnp.where(prev_near, near_at(k, 0), 0.0), near_at(k, KEY_TILE - Q_BLOCK * r)])

    @pl.when(even_prev)
    def _():
        attend(kd // 2 - 1, [kd - 2, kd - 1], lambda k: [None, near_at(k, 0)])

    @pl.when(jnp.logical_not(kd_odd))
    def _():
        attend(kd // 2, [kd], lambda k: [near_at(k, KEY_TILE - Q_BLOCK * r)])

    heads_out = []
    for k in kvs:
        acc = acc_scr[k]
        o_s = acc[0:HD] / acc[HD:HD + 1]
        o_c = oc_scr[k]
        for g in range(G_NSA):
            h = G_NSA * k + g
            cols = slice(g * Q_BLOCK, (g + 1) * Q_BLOCK)
            heads_out.append(o_c[:, cols] * g_t[h:h + 1] + o_s[:, cols] * g_t[H_NSA + h:H_NSA + h + 1]
                             + o_w[k][:, cols] * g_t[2 * H_NSA + h:2 * H_NSA + h + 1])
    o_ref[...] = jnp.concatenate(heads_out, axis=0).T


def _sel_pattern(rows, width):
    key = np.arange(rows)[:, None]
    b = np.arange(width)[None, :]
    ones = (b >= MASK_ROWS) & (b < MASK_ROWS + 2)
    return jnp.asarray(((key // SEL_BLOCK) % MASK_ROWS == b) | ones, BF16)


def _nsa_prompt(q, gates, attn, kc, vc, band, rel_table):
    T = q.shape[0]
    ncp, ns = kc.shape[0], T // SEL_BLOCK
    width = G_NSA * Q_BLOCK
    ks_aug, vs_t, kw, vw_t = attn
    operands = [q, gates, kc.astype(BF16), vc.T.astype(BF16), ks_aug, vs_t, kw, vw_t, band, _pool_matrix(ns, ncp)]
    blk = lambda w: pl.BlockSpec((Q_BLOCK, w), lambda i: (i, 0))
    return pl.pallas_call(
        _nsa_prompt_kernel,
        grid=(T // Q_BLOCK,),
        in_specs=[pl.BlockSpec(memory_space=pltpu.SMEM), blk(H_NSA * HD), blk(LANE)] + [_resident(a.shape) for a in operands[2:]],
        out_specs=blk(H_NSA * HD),
        out_shape=jax.ShapeDtypeStruct((T, H_NSA * HD), F32),
        scratch_shapes=[pltpu.VMEM((N_KV, LANE, width), BF16), pltpu.VMEM((N_KV, ns, width), BF16),
                        pltpu.VMEM((N_KV, V_ROWS_KV, width), F32), pltpu.VMEM((N_KV, 1, width), F32),
                        pltpu.VMEM((N_KV, ncp, width), F32)] + [pltpu.VMEM((N_KV, 2 * KEY_TILE, width), F32)] * 2
                       + [pltpu.VMEM((N_KV, HD, width), F32)],
        compiler_params=_params(("arbitrary",)),
        name="nsa_prompt",
    )(rel_table, *operands)


SLAB_PAGES = 8
SMP_PAGES = 32
TOK_PAD = 8
SMP_COLS = H_NSA * TOK_PAD


def _nsa_sample_kernel(pt_ref, *refs, n_pages, n_valid, past):
    pages = refs[:n_pages]
    (q_ref, g_ref, kc_ref, vc_ref, kvn_ref, win_ref, winn_ref, tab_ref, pool_ref, gsum_ref, epat_ref, o_ref,
     mask_scr, acc_scr, m_scr, l_scr, oc_scr, ow_scr, qrow_scr, maskt_scr, farc_scr, nearbt_scr, m2_scr, l2_scr, acc2_scr) = refs[n_pages:]
    j = pl.program_id(1)
    ncp, wbuf = kc_ref.shape[0], win_ref.shape[1]
    lane = lax.broadcasted_iota(jnp.int32, (1, LANE), 1)
    tok = lane & (TOK_PAD - 1)
    second_kv = lane >= G_NSA * TOK_PAD
    tab = [tab_ref[b:b + 1, :] for b in range(N_BUCKETS)]
    far_row = tab[N_BUCKETS - 1]
    own_rows = lambda x: jnp.where(second_kv, x[HD:2 * HD], x[0:HD])
    pad_rows = lambda x: jnp.concatenate([x, jnp.zeros((LANE - x.shape[0], x.shape[1]), x.dtype)], axis=0)
    trow = lax.broadcasted_iota(jnp.int32, (LANE, 1), 0)
    d_new = tok - trow
    new_bias = jnp.where((d_new >= 0) & (trow < n_valid), _bias_chain(jnp.maximum(d_new, 0), tab), NEG)

    def attend_update(s, values_t):
        m_old = m_scr[...]
        m_new = jnp.maximum(m_old, jnp.max(s, axis=0, keepdims=True))
        alpha = jnp.exp(m_old - m_new)
        p = jnp.exp(s - m_new)
        l_scr[...] = alpha * l_scr[...] + jnp.sum(p, axis=0, keepdims=True)
        acc_scr[...] = alpha * acc_scr[...] + jnp.dot(values_t.astype(BF16), p.astype(BF16), preferred_element_type=F32)
        m_scr[...] = m_new

    def reset():
        m_scr[...] = jnp.full(m_scr.shape, M_INIT, F32)
        l_scr[...] = jnp.zeros(l_scr.shape, F32)
        acc_scr[...] = jnp.zeros(acc_scr.shape, F32)

    @pl.when(j == 0)
    def _():
        q_t = pad_rows(q_ref[...] * HD ** -0.5).T
        halves = []
        for k in range(N_KV):
            part = jnp.zeros((HD, LANE), F32)
            for g in range(G_NSA):
                h = G_NSA * k + g
                piece = q_t[h * HD:(h + 1) * HD, :]
                part = part + (pltpu.roll(piece, TOK_PAD * h, axis=1) if h else piece)
            halves.append(part)
        top_f = jnp.concatenate(halves, axis=0)
        top = top_f.astype(BF16)
        qrow_scr[...] = top_f.T[:SMP_COLS].astype(BF16)
        qpos = past + tok

        n0 = ncp - EDGE_ROWS
        kcb = kc_ref[...].astype(BF16)
        d_edge = qpos - (CMP_STRIDE * (n0 + lax.broadcasted_iota(jnp.int32, (EDGE_ROWS, 1), 0)) + CMP_BLOCK - 1)
        s_c = jnp.concatenate([
            jnp.dot(kcb[:n0], top, preferred_element_type=F32) + far_row,
            jnp.dot(kcb[n0:], top, preferred_element_type=F32) + jnp.where(d_edge >= 0, _bias_chain(jnp.maximum(d_edge, 0), tab), NEG)], axis=0)
        p_c = _softmax_cols(s_c)
        oc_scr[...] = own_rows(jnp.dot(vc_ref[...].T.astype(BF16), p_c.astype(BF16), preferred_element_type=F32))
        imp = _dot_exact_rhs(p_c, gsum_ref[...], terms=3)
        mask_scr[...] = _select_blocks(_dot_exact_lhs(pool_ref[...], imp), qpos, N_SEL)

        wk = win_ref[0:LANE, :].T
        d_w = wbuf + tok - lax.broadcasted_iota(jnp.int32, (wbuf, 1), 0)
        near = wbuf - LANE
        s_w = jnp.dot(wk.astype(BF16), top, preferred_element_type=F32)
        s_w = (jnp.concatenate([s_w[:near] + far_row, s_w[near:] + _bias_chain(d_w[near:], tab)], axis=0)
               + jnp.where(d_w < WINDOW, 0.0, NEG))
        reset()
        attend_update(s_w, win_ref[LANE:, :])
        wn = pad_rows(winn_ref[...])
        attend_update(jnp.dot(wn[:, :LANE].astype(BF16), top, preferred_element_type=F32) + new_bias, wn[:, LANE:].T)
        ow_scr[...] = own_rows(acc_scr[...]) / l_scr[...]
        m2_scr[...] = jnp.full(m2_scr.shape, M_INIT, F32)
        l2_scr[...] = jnp.zeros(l2_scr.shape, F32)
        acc2_scr[...] = jnp.zeros(acc2_scr.shape, F32)
        for sl in range(maskt_scr.shape[0]):
            maskt_scr[sl] = pad_rows(mask_scr[sl * MASK_ROWS:(sl + 1) * MASK_ROWS, :]).T[:SMP_COLS].astype(BF16)
        farc_scr[...] = pad_rows(tab_ref[...]).T[:SMP_COLS, N_BUCKETS - 1:N_BUCKETS]
        nearbt_scr[...] = _bias_chain(LANE + tok - lax.broadcasted_iota(jnp.int32, (LANE, 1), 0), tab).T[:SMP_COLS]

    def rows_update(s, values_t):
        m_old = m2_scr[...]
        m_new = jnp.maximum(m_old, jnp.max(s, axis=1, keepdims=True))
        alpha = jnp.exp(m_old - m_new)
        p = jnp.exp(s - m_new)
        l2_scr[...] = alpha * l2_scr[...] + jnp.sum(p, axis=1, keepdims=True)
        acc2_scr[...] = alpha * acc2_scr[...] + lax.dot_general(p.astype(BF16), values_t.astype(BF16), (((1,), (1,)), ((), ())),
                                                                preferred_element_type=F32)
        m2_scr[...] = m_new

    n_slabs = n_pages // SLAB_PAGES
    slab_keys = SLAB_PAGES * PAGE_SIZE
    far_col = farc_scr[...]
    for sub in range(n_slabs):
        tile_pages = pages[sub * SLAB_PAGES:(sub + 1) * SLAB_PAGES]
        kt_tile = jnp.concatenate([pg[0:LANE, :] for pg in tile_pages], axis=1)
        vt_tile = jnp.concatenate([pg[LANE:, :] for pg in tile_pages], axis=1)
        lhs = jnp.concatenate([qrow_scr[...], maskt_scr[j * n_slabs + sub]], axis=1)
        s = jnp.dot(lhs, jnp.concatenate([kt_tile.astype(BF16), epat_ref[...]], axis=0), preferred_element_type=F32)
        if sub < n_slabs - 1:
            s = s + far_col
        else:
            near = slab_keys - LANE
            s = jnp.concatenate([s[:, :near] + far_col, s[:, near:] + jnp.where(j == pl.num_programs(1) - 1, nearbt_scr[...], far_col)], axis=1)
        rows_update(s, vt_tile)

    @pl.when(j == pl.num_programs(1) - 1)
    def _():
        kn = pad_rows(kvn_ref[...])
        last_blk = past // SEL_BLOCK
        new_mask = maskt_scr[last_blk // MASK_ROWS][:, last_blk % MASK_ROWS:last_blk % MASK_ROWS + 1].astype(F32)
        s_n = (jnp.dot(qrow_scr[...], kn[:, 2 * LANE:3 * LANE].T.astype(BF16), preferred_element_type=F32)
               + new_bias.T[:SMP_COLS] + new_mask)
        rows_update(s_n, kn[:, 3 * LANE:].T)
        acc_t = pad_rows(acc2_scr[...]).T
        l_t = pad_rows(jnp.broadcast_to(l2_scr[...], (SMP_COLS, LANE))).T[0:1]
        o_s = own_rows(acc_t) / jnp.where(lane < SMP_COLS, l_t, 1.0)
        g_t = pad_rows(g_ref[...]).T
        gate_rows = []
        for b in range(3):
            row = g_t[b * H_NSA:b * H_NSA + 1]
            for h in range(1, H_NSA):
                row = row + pltpu.roll(g_t[b * H_NSA + h:b * H_NSA + h + 1], TOK_PAD * h, axis=1)
            gate_rows.append(row)
        o_col = oc_scr[...] * gate_rows[0] + o_s * gate_rows[1] + ow_scr[...] * gate_rows[2]
        per_head = [o_col if h == 0 else pltpu.roll(o_col, LANE - TOK_PAD * h, axis=1) for h in range(H_NSA)]
        o_ref[...] = jnp.concatenate(per_head, axis=0).T[:TOK_PAD]


def _nsa_sample(pool, page_table, q, gates, kc, vc, kv_new, win_buf, win_new, rel_table, n_valid):
    B, n_pages_total = page_table.shape
    past = n_pages_total * PAGE_SIZE
    ncp = kc.shape[1]
    ns = past // SEL_BLOCK + 1
    nsp = -(-ns // MASK_ROWS) * MASK_ROWS
    col = np.arange(LANE)
    used = col < SMP_COLS
    gsum = jnp.asarray(((col[:, None] // (G_NSA * TOK_PAD) == col[None, :] // (G_NSA * TOK_PAD))
                        & (col[:, None] % TOK_PAD == col[None, :] % TOK_PAD) & used[:, None] & used[None, :]), BF16)
    tab_cols = jnp.pad(jnp.repeat(rel_table, TOK_PAD, axis=1), ((0, 0), (0, LANE - SMP_COLS)))
    n_step = min(SMP_PAGES, n_pages_total)
    consts = [tab_cols, _pool_matrix(nsp, ncp), gsum, _sel_pattern(SLAB_PAGES * PAGE_SIZE, LANE).T]
    per_seq = [q, gates, kc, vc, kv_new, win_buf, win_new]
    seq_spec = lambda a: pl.BlockSpec((None,) + a.shape[1:], lambda b, j, pt: (b,) + (0,) * (a.ndim - 1))
    const = lambda a: pl.BlockSpec(a.shape, lambda b, j, pt: (0,) * a.ndim)

    def page_map(p):
        return lambda b, j, pt: (pt[b, j * n_step + p], 1, 0)

    grid_spec = pltpu.PrefetchScalarGridSpec(
        num_scalar_prefetch=1,
        grid=(B, n_pages_total // n_step),
        in_specs=[pl.BlockSpec((None, 2 * LANE, PAGE_SIZE), page_map(p)) for p in range(n_step)]
                 + [seq_spec(a) for a in per_seq] + [const(a) for a in consts],
        out_specs=pl.BlockSpec((None, TOK_PAD, H_NSA * HD), lambda b, j, pt: (b, 0, 0)),
        scratch_shapes=[pltpu.VMEM((nsp, LANE), F32), pltpu.VMEM((LANE, LANE), F32),
                        pltpu.VMEM((1, LANE), F32), pltpu.VMEM((1, LANE), F32), pltpu.VMEM((HD, LANE), F32), pltpu.VMEM((HD, LANE), F32),
                        pltpu.VMEM((SMP_COLS, LANE), BF16), pltpu.VMEM((nsp // MASK_ROWS, SMP_COLS, LANE), BF16),
                        pltpu.VMEM((SMP_COLS, 1), F32), pltpu.VMEM((SMP_COLS, LANE), F32),
                        pltpu.VMEM((SMP_COLS, 1), F32), pltpu.VMEM((SMP_COLS, 1), F32), pltpu.VMEM((SMP_COLS, LANE), F32)],
    )
    return pl.pallas_call(
        functools.partial(_nsa_sample_kernel, n_pages=n_step, n_valid=n_valid, past=past),
        grid_spec=grid_spec,
        out_shape=jax.ShapeDtypeStruct((B, TOK_PAD, H_NSA * HD), F32),
        compiler_params=_params(("arbitrary", "arbitrary")),
        name="nsa_sample",
    )(page_table, *([pool] * n_step), *per_seq, *consts)


def _out_ffn_kernel(x_ref, nsa_ref, rw_ref, g1_ref, lng1_ref, lnb1_ref, w_ref,
                    sh_ref, sc_ref, g2_ref, lng2_ref, lnb2_ref, wg_ref, wu_ref, wd_ref, o_ref):
    half = H_NSA * HD
    out = (jnp.dot(nsa_ref[...].astype(BF16), w_ref[0:half, :], preferred_element_type=F32)
           + jnp.dot(rw_ref[...].astype(BF16), w_ref[half:, :], preferred_element_type=F32))
    x2 = _layer_norm(ALPHA * x_ref[...] + (1.0 + g1_ref[...]) * out, lng1_ref[...], lnb1_ref[...])
    o_ref[...] = _ffn_block(x2, sh_ref[...], sc_ref[...], g2_ref[...], lng2_ref[...], lnb2_ref[...], wg_ref, wu_ref, wd_ref)


def _out_ffn(x, o_nsa, o_rwkv, gate1, ln_g1, ln_b1, w_out, shift, scale, gate2, ln_g2, ln_b2, wg, wu, wd):
    rows = x.shape[0]
    tm = min(512, rows)
    row = lambda i: (i, 0)
    vec = lambda: _resident((1, D_MODEL))
    return pl.pallas_call(
        _out_ffn_kernel,
        grid=(rows // tm,),
        in_specs=[pl.BlockSpec((tm, D_MODEL), row), pl.BlockSpec((tm, H_NSA * HD), row), pl.BlockSpec((tm, D_RWKV), row),
                  _mod_spec(gate1, tm), vec(), vec(), _resident(w_out.shape),
                  _mod_spec(shift, tm), _mod_spec(scale, tm), _mod_spec(gate2, tm), vec(), vec(),
                  _resident((D_MODEL, D_FF)), _resident((D_MODEL, D_FF)), _resident((D_FF, D_MODEL))],
        out_specs=pl.BlockSpec((tm, D_MODEL), row),
        out_shape=jax.ShapeDtypeStruct((rows, D_MODEL), F32),
        compiler_params=_params(("arbitrary",)),
        name="out_ffn",
    )(x, o_nsa, o_rwkv, gate1, ln_g1.reshape(1, -1), ln_b1.reshape(1, -1), w_out,
      shift, scale, gate2, ln_g2.reshape(1, -1), ln_b2.reshape(1, -1), wg, wu, wd)


def kernel(x_prompt, x_sample, cache_nsa_kv, cache_nsa_win, state_rwkv_shift, state_rwkv_wkv, page_table, c_prompt, c_sample, rel_table, w_ada, b_ada, ln_g, ln_b, ffn1_gate, ffn1_up, ffn1_down, ffn2_gate, ffn2_up, ffn2_down, w_in, w_out, cmp_pe_k, cmp_w1_k, cmp_b1_k, cmp_w2_k, cmp_pe_v, cmp_w1_v, cmp_b1_v, cmp_w2_v, rwkv_mu, rwkv_w0, rwkv_w2, rwkv_a0, rwkv_a2, rwkv_g2, rwkv_k_k, rwkv_k_a, rwkv_r_k, rwkv_gn_w, rwkv_gn_b):
    assert w_ada.shape[0] == DEPTH == 1 and x_prompt.shape[0] == 1
    l = 0
    lw = dict(cmp_pe_k=cmp_pe_k[l], cmp_w1_k=cmp_w1_k[l], cmp_b1_k=cmp_b1_k[l], cmp_w2_k=cmp_w2_k[l],
              cmp_pe_v=cmp_pe_v[l], cmp_w1_v=cmp_w1_v[l], cmp_b1_v=cmp_b1_v[l], cmp_w2_v=cmp_w2_v[l],
              rwkv_mu=rwkv_mu[l], rwkv_w0=rwkv_w0[l], rwkv_w2=rwkv_w2[l], rwkv_a0=rwkv_a0[l], rwkv_a2=rwkv_a2[l], rwkv_g2=rwkv_g2[l],
              rwkv_k_k=rwkv_k_k[l], rwkv_k_a=rwkv_k_a[l], rwkv_r_k=rwkv_r_k[l], rwkv_gn_w=rwkv_gn_w[l], rwkv_gn_b=rwkv_gn_b[l])
    T = x_prompt.shape[1]
    nb, nt = x_sample.shape[0], x_sample.shape[1]
    assert nt <= TOK_PAD
    n_seq = 1 + nb
    c_all = jnp.concatenate([c_prompt, c_sample, jnp.zeros((-n_seq % 8, D_MODEL), F32)], axis=0)
    mod = _ada(c_all, w_ada[l], b_ada[l])
    mod_p = mod[0:1].reshape(9, 1, D_MODEL)
    mod_s = jnp.repeat(mod[1:n_seq].reshape(nb, 9, D_MODEL), nt, axis=0).transpose(1, 0, 2)
    ffn1 = [w[l].astype(BF16) for w in (ffn1_gate, ffn1_up, ffn1_down)]
    ffn2 = [w[l].astype(BF16) for w in (ffn2_gate, ffn2_up, ffn2_down)]
    w_in_p = _prep_w_in(w_in[l])
    w_out_b = w_out[l].astype(BF16)

    def trunk_in(x, m, attn_operands=False):
        x1 = _ffn(x, m[0], m[1], m[2], ln_g[l, 0], ln_b[l, 0], *ffn1)
        return x1, _proj(x1, m[3], m[4], w_in_p, attn_operands)

    def trunk_out(x1, o_nsa, o_rwkv, m):
        return _out_ffn(x1, o_nsa, o_rwkv, m[5], ln_g[l, 1], ln_b[l, 1], w_out_b, m[6], m[7], m[8], ln_g[l, 2], ln_b[l, 2], *ffn2)

    xp1, (q, kv, win, gates, pr, *attn) = trunk_in(x_prompt[0], mod_p, attn_operands=True)
    o_rw, wkv_p = _rwkv(pr[None], jnp.zeros((1, 1, RW_PAD), F32), jnp.zeros((1, H_RWKV, HD_RWKV, HD_RWKV), F32), lw, min(RW_STEP, T))
    n_rows = T // PAGE_SIZE
    kc, vc = _compress(kv.reshape(n_rows, PAGE_SIZE, 4 * LANE), jnp.arange(n_rows, dtype=jnp.int32)[None], lw, transposed=False)
    o_nsa = _nsa_prompt(q, gates, attn, kc[0], vc[0], _band(rel_table), rel_table)
    y_prompt = trunk_out(xp1, o_nsa, o_rw[0], mod_p)
    kv_prompt = kv.reshape(1, 1, T, 4, N_KV, HD)
    win_prompt = win[T - min(WINDOW, T):].reshape(1, 1, -1, 2, N_KV, HD)
    shift_prompt = _rwkv_uncols(pr[T - 1]).reshape(1, 1, RWKV_COLS)

    xs1, (q_s, kv_s, win_s, gates_s, pr_s) = trunk_in(x_sample.reshape(nb * nt, D_MODEL), mod_s)
    tokens = lambda a: jnp.pad(a.reshape(nb, nt, -1), ((0, 0), (0, TOK_PAD - nt), (0, 0)))
    pr_pad = jnp.pad(pr_s.reshape(nb, nt, -1), ((0, 0), (0, RW_TOK_PAD - nt), (0, 0)))
    o_rw_s, wkv_s = _rwkv(pr_pad, _rwkv_cols(state_rwkv_shift[l])[:, None], state_rwkv_wkv[l], lw, nt)
    pool_t = jnp.transpose(cache_nsa_kv[l], (0, 2, 3, 4, 1)).reshape(-1, 4 * LANE, PAGE_SIZE)
    kc_s, vc_s = _compress(pool_t, page_table, lw, transposed=True)
    win_buf = cache_nsa_win[l]
    win_t = jnp.transpose(win_buf, (0, 2, 3, 4, 1)).reshape(nb, 2 * LANE, -1)
    o_nsa_s = _nsa_sample(pool_t, page_table, tokens(q_s), tokens(gates_s), kc_s, vc_s, tokens(kv_s),
                          win_t, tokens(win_s), rel_table, nt)
    y_sample = trunk_out(xs1, o_nsa_s[:, :nt].reshape(nb * nt, -1), o_rw_s[:, :nt].reshape(nb * nt, -1), mod_s)
    kv_sample = kv_s.reshape(1, nb, nt, 4, N_KV, HD)
    win_sample = jnp.concatenate([win_buf, win_s.reshape(nb, nt, 2, N_KV, HD)], axis=1)[None, :, nt:]
    shift_sample = _rwkv_uncols(pr_s.reshape(nb, nt, -1)[:, -1])[None]
    return (y_prompt[None], y_sample.reshape(nb, nt, D_MODEL), kv_prompt, win_prompt, shift_prompt, wkv_p[None],
            kv_sample, win_sample, shift_sample, wkv_s[None])
```

```python
import functools
import math

import numpy as np
import jax
import jax.numpy as jnp
from jax import lax
from jax.experimental import pallas as pl
from jax.experimental.pallas import tpu as pltpu

D_MODEL = 1024
PAGE_SIZE = 128
H_NSA = 8
N_KV = 2
G_NSA = H_NSA // N_KV
HD = 64
CMP_STRIDE = 16
CMP_BLOCK = 2 * CMP_STRIDE
CMP_HIDDEN = 256
SEL_BLOCK = 64
N_SEL = 16
WINDOW = 512
Q_BLOCK = 128
N_BUCKETS = 32
MAX_DISTANCE = 128
H_RWKV = 8
HD_RWKV = 64
D_RWKV = H_RWKV * HD_RWKV
DECAY_LORA = 32
AAA_LORA = 32
GATE_LORA = 96
GN_EPS = 64e-5
D_FF = 2816
LN_EPS = 1e-5
DEPTH = 1
ALPHA = (2 * DEPTH) ** 0.25

NSA_SIZES = (H_NSA * HD,) + (N_KV * HD,) * 6 + (H_NSA * 3,)
RWKV_SIZES = (D_RWKV, D_RWKV, D_RWKV, DECAY_LORA, AAA_LORA, GATE_LORA)
NSA_COLS = sum(NSA_SIZES)
RWKV_COLS = sum(RWKV_SIZES)

F32 = jnp.float32
BF16 = jnp.bfloat16
LANE = 128
NEG = -(2.0 ** 100)
M_INIT = -(2.0 ** 103)
VMEM_LIMIT = 56 * 1024 * 1024

RW_PAD = 3 * D_RWKV + 3 * LANE
P_Q, P_KV, P_WIN, P_GATE, P_RW = 0, 512, 1024, 1280, 1408
P_COLS = P_RW + RW_PAD
KEY_TILE = 512
MASK_ROWS = 16
SEL_MASK0 = HD
SEL_FAR0 = SEL_MASK0 + MASK_ROWS
V_ROWS_KV = HD + MASK_ROWS
CMP_CLASS_ROWS = 256
Q_PER_TILE = KEY_TILE // Q_BLOCK
CMP_PER_Q = Q_BLOCK // CMP_STRIDE
EDGE_ROWS = 32


def _bucket_lows():
    d = np.arange(0, 4 * MAX_DISTANCE, dtype=np.int64)
    max_exact = N_BUCKETS // 2
    df = np.maximum(d, 1).astype(np.float32)
    large = max_exact + (np.log(df / np.float32(max_exact)) / np.float32(math.log(MAX_DISTANCE / max_exact))
                         * np.float32(N_BUCKETS - max_exact)).astype(np.int32)
    b = np.where(d < max_exact, d, np.minimum(large, N_BUCKETS - 1))
    lows = [int(np.argmax(b >= k)) for k in range(N_BUCKETS)]
    return b, lows


_BUCKET_OF, _BUCKET_LOW = _bucket_lows()
FAR_DIST = _BUCKET_LOW[N_BUCKETS - 1]


def _resident(shape):
    nd = len(shape)
    return pl.BlockSpec(shape, lambda *_: (0,) * nd, pipeline_mode=pl.Buffered(1))


def _params(sem):
    return pltpu.CompilerParams(dimension_semantics=sem, vmem_limit_bytes=VMEM_LIMIT)


def _dot_exact_rhs(x, rhs_bf16, terms=2):
    acc = None
    rem = x
    for _ in range(terms):
        part = rem.astype(BF16)
        d = jnp.dot(part, rhs_bf16, preferred_element_type=F32)
        acc = d if acc is None else acc + d
        rem = rem - part.astype(F32)
    return acc


def _dot_exact_lhs(lhs_bf16, x, terms=3):
    acc = None
    rem = x
    for _ in range(terms):
        part = rem.astype(BF16)
        d = jnp.dot(lhs_bf16, part, preferred_element_type=F32)
        acc = d if acc is None else acc + d
        rem = rem - part.astype(F32)
    return acc


def _layer_norm(y, g, b):
    mu = jnp.mean(y, axis=-1, keepdims=True)
    yc = y - mu
    var = jnp.mean(yc * yc, axis=-1, keepdims=True)
    return yc * lax.rsqrt(var + LN_EPS) * g + b


def _bias_chain(d, tab_rows):
    out = tab_rows[0] + jnp.zeros(d.shape, F32)
    for b in range(1, N_BUCKETS):
        out = jnp.where(d >= _BUCKET_LOW[b], tab_rows[b], out)
    return out


def _ada_kernel(c_ref, w_ref, b_ref, o_ref):
    c = c_ref[...]
    h = (c * jax.nn.sigmoid(c)).astype(BF16)
    o_ref[...] = jnp.dot(h, w_ref[...].astype(BF16), preferred_element_type=F32) + b_ref[...]


def _ada(c_all, w_ada, b_ada):
    rows, n = c_all.shape[0], w_ada.shape[1]
    tn = 1152
    return pl.pallas_call(
        _ada_kernel,
        grid=(n // tn,),
        in_specs=[pl.BlockSpec((rows, D_MODEL), lambda j: (0, 0)),
                  pl.BlockSpec((D_MODEL, tn), lambda j: (0, j)),
                  pl.BlockSpec((1, tn), lambda j: (0, j))],
        out_specs=pl.BlockSpec((rows, tn), lambda j: (0, j)),
        out_shape=jax.ShapeDtypeStruct((rows, n), F32),
        compiler_params=_params(("arbitrary",)),
        name="ada",
    )(c_all, w_ada, b_ada.reshape(1, n))


FF_CHUNKS = 2


def _ffn_block(x, shift, scale, gate, ln_g, ln_b, wg_ref, wu_ref, wd_ref):
    h = (x * (1.0 + scale) + shift).astype(BF16)
    ck = D_FF // FF_CHUNKS
    acc = jnp.zeros(x.shape, F32)
    for c in range(FF_CHUNKS):
        a = jnp.dot(h, wg_ref[:, c * ck:(c + 1) * ck], preferred_element_type=F32)
        b = jnp.dot(h, wu_ref[:, c * ck:(c + 1) * ck], preferred_element_type=F32)
        t = (a * jax.nn.sigmoid(a) * b).astype(BF16)
        acc = acc + jnp.dot(t, wd_ref[c * ck:(c + 1) * ck, :], preferred_element_type=F32)
    y = ALPHA * x + (1.0 + gate) * (0.5 * acc)
    return _layer_norm(y, ln_g, ln_b)


def _ffn_kernel(x_ref, sh_ref, sc_ref, gt_ref, lng_ref, lnb_ref, wg_ref, wu_ref, wd_ref, o_ref):
    o_ref[...] = _ffn_block(x_ref[...], sh_ref[...], sc_ref[...], gt_ref[...], lng_ref[...], lnb_ref[...], wg_ref, wu_ref, wd_ref)


def _mod_spec(mod, tm):
    if mod.shape[0] == 1:
        return pl.BlockSpec((1, D_MODEL), lambda i: (0, 0))
    return pl.BlockSpec((tm, D_MODEL), lambda i: (i, 0))


def _ffn(x, shift, scale, gate, ln_g, ln_b, wg, wu, wd):
    rows = x.shape[0]
    tm = min(512, rows)
    row = lambda i: (i, 0)
    return pl.pallas_call(
        _ffn_kernel,
        grid=(rows // tm,),
        in_specs=[pl.BlockSpec((tm, D_MODEL), row), _mod_spec(shift, tm), _mod_spec(scale, tm), _mod_spec(gate, tm),
                  _resident((1, D_MODEL)), _resident((1, D_MODEL)),
                  _resident((D_MODEL, D_FF)), _resident((D_MODEL, D_FF)), _resident((D_FF, D_MODEL))],
        out_specs=pl.BlockSpec((tm, D_MODEL), row),
        out_shape=jax.ShapeDtypeStruct((rows, D_MODEL), F32),
        compiler_params=_params(("arbitrary",)),
        name="ffn",
    )(x, shift, scale, gate, ln_g.reshape(1, -1), ln_b.reshape(1, -1), wg, wu, wd)


def _proj_kernel(x_ref, sh_ref, sc_ref, w_ref, q_ref, kv_ref, win_ref, g_ref, pr_ref, *attn_refs):
    h = (x_ref[...] * (1.0 + sc_ref[...]) + sh_ref[...]).astype(BF16)
    p = jnp.dot(h, w_ref[...], preferred_element_type=F32)
    q_ref[...] = p[:, P_Q:P_KV]
    kv_ref[...] = p[:, P_KV:P_WIN]
    win_ref[...] = p[:, P_WIN:P_GATE]
    g_ref[...] = jax.nn.sigmoid(p[:, P_GATE:P_RW])
    pr_ref[...] = p[:, P_RW:P_COLS]
    if attn_refs:
        ks_ref, vst_ref, kw_ref, vwt_ref = attn_refs
        tm = x_ref.shape[0]
        lane = lax.broadcasted_iota(jnp.int32, (1, LANE), 1)
        key = pl.program_id(0) * tm + lax.broadcasted_iota(jnp.int32, (tm, 1), 0)
        slot = lane - HD
        pattern = (((key // SEL_BLOCK) % MASK_ROWS == slot) | ((slot >= MASK_ROWS) & (slot < MASK_ROWS + 2))).astype(F32)
        k_sel = p[:, P_KV + 2 * LANE:P_KV + 3 * LANE]
        ks_ref[0] = jnp.where(lane < HD, k_sel, pattern).astype(BF16)
        ks_ref[1] = jnp.where(lane < HD, pltpu.roll(k_sel, HD, axis=1), pattern).astype(BF16)
        tail = jnp.concatenate([jnp.ones((1, tm), F32), jnp.zeros((MASK_ROWS - 1, tm), F32)], axis=0)
        v_sel_t = p[:, P_KV + 3 * LANE:P_WIN].T
        v_win_t = p[:, P_WIN + LANE:P_GATE].T
        kw_ref[...] = p[:, P_WIN:P_WIN + LANE].astype(BF16)
        for hd in range(N_KV):
            vst_ref[hd, 0] = jnp.concatenate([v_sel_t[hd * HD:(hd + 1) * HD], tail], axis=0).astype(BF16)
            win_rows = jnp.concatenate([v_win_t[hd * HD:(hd + 1) * HD], tail], axis=0).astype(BF16)
            for j in range(tm // Q_BLOCK):
                vwt_ref[hd, j] = win_rows[:, j * Q_BLOCK:(j + 1) * Q_BLOCK]


def _proj(x, shift, scale, w_in_p, attn_operands=False):
    rows = x.shape[0]
    tm = min(KEY_TILE, rows)
    row = lambda i: (i, 0)
    widths = (512, 512, 256, LANE, RW_PAD)
    out_specs = [pl.BlockSpec((tm, w), row) for w in widths]
    out_shape = [jax.ShapeDtypeStruct((rows, w), F32) for w in widths]
    if attn_operands:
        assert tm == KEY_TILE
        per_q = tm // Q_BLOCK
        out_specs += [pl.BlockSpec((N_KV, tm, LANE), lambda i: (0, i, 0)), pl.BlockSpec((N_KV, 1, V_ROWS_KV, tm), lambda i: (0, i, 0, 0)),
                      pl.BlockSpec((tm, LANE), row), pl.BlockSpec((N_KV, per_q, V_ROWS_KV, Q_BLOCK), lambda i: (0, i, 0, 0))]
        out_shape += [jax.ShapeDtypeStruct((N_KV, rows, LANE), BF16), jax.ShapeDtypeStruct((N_KV, rows // tm, V_ROWS_KV, tm), BF16),
                      jax.ShapeDtypeStruct((rows, LANE), BF16), jax.ShapeDtypeStruct((N_KV, rows // Q_BLOCK, V_ROWS_KV, Q_BLOCK), BF16)]
    return pl.pallas_call(
        _proj_kernel,
        grid=(rows // tm,),
        in_specs=[pl.BlockSpec((tm, D_MODEL), row), _mod_spec(shift, tm), _mod_spec(scale, tm),
                  _resident((D_MODEL, P_COLS))],
        out_specs=out_specs,
        out_shape=out_shape,
        compiler_params=_params(("arbitrary",)),
        name="proj",
    )(x, shift, scale, w_in_p)


def _prep_w_in(w_in):
    pad = lambda a, n: jnp.pad(a, ((0, 0), (0, n - a.shape[1])))
    nsa, rw = w_in[:, :NSA_COLS], w_in[:, NSA_COLS:]
    gl = nsa[:, 1280:1304].reshape(D_MODEL, H_NSA, 3).transpose(0, 2, 1).reshape(D_MODEL, 3 * H_NSA)
    cols = [nsa[:, :1280], pad(gl, LANE), _rwkv_cols(rw)]
    return jnp.concatenate(cols, axis=1).astype(BF16)


def _rwkv_cols(a):
    pad = lambda t: jnp.pad(t, [(0, 0)] * (t.ndim - 1) + [(0, LANE - t.shape[-1])])
    n = 3 * D_RWKV
    return jnp.concatenate([a[..., :n], pad(a[..., n:n + 32]), pad(a[..., n + 32:n + 64]), pad(a[..., n + 64:n + 160])], axis=-1)


def _rwkv_uncols(a):
    n = 3 * D_RWKV
    return jnp.concatenate([a[..., :n], a[..., n:n + 32], a[..., n + LANE:n + LANE + 32], a[..., n + 2 * LANE:n + 2 * LANE + 96]], axis=-1)


RW_GROUP = 64
RW_TOK_PAD = 16
RW_PAIRS = H_RWKV // 2
RW_BLOCK = 64
RW_STEP = 256


def _lora(x, w_ref):
    w = w_ref[...]
    w_hi = w.astype(BF16)
    w_lo = (w - w_hi.astype(F32)).astype(BF16)
    return _dot_exact_rhs(x, w_hi) + jnp.dot(x.astype(BF16), w_lo, preferred_element_type=F32)


def _rwkv_kernel(pr_ref, sh0_ref, s0_ref, mu_ref, w0_ref, a0_ref, kk_ref, ka_ref, rk_ref, gw_ref, gb_ref,
                 w2_ref, a2_ref, g2_ref, bo_ref, lgrp_ref, ggrp_ref, o_ref, sout_ref,
                 prev_scr, s_scr, *, n_valid):
    tb = pr_ref.shape[0]
    step = pl.program_id(1)

    @pl.when(step == 0)
    def _():
        prev_scr[...] = sh0_ref[...]
        s_scr[...] = s0_ref[...]

    p = pr_ref[...]
    rows = lax.broadcasted_iota(jnp.int32, (tb, 1), 0)
    prev = jnp.where(rows == 0, prev_scr[...], pltpu.roll(p, 1, axis=0))
    prev_scr[...] = p[tb - 1:tb, :]
    xs = p + (prev - p) * mu_ref[...]
    n = D_RWKV
    r, k, v = xs[:, :n], xs[:, n:2 * n], xs[:, 2 * n:3 * n]

    def head_sums(x):
        return jnp.concatenate([_dot_exact_rhs(x[:, :n // 2], bo_ref[...]), _dot_exact_rhs(x[:, n // 2:], bo_ref[...])], axis=1)

    wl, al, gl = xs[:, 3 * n:3 * n + LANE], xs[:, 3 * n + LANE:3 * n + 2 * LANE], xs[:, 3 * n + 2 * LANE:]
    z = -(w0_ref[...] + _lora(jnp.tanh(wl), w2_ref))
    w = -(jnp.maximum(z, 0.0) + jnp.log(1.0 + jnp.exp(-jnp.abs(z)))) - 0.5
    a = jax.nn.sigmoid(a0_ref[...] + _lora(al, a2_ref))
    g = _lora(jax.nn.sigmoid(gl), g2_ref)
    kk = k * kk_ref[...]
    ss = head_sums(kk * kk)
    kk = kk / jnp.maximum(jnp.sqrt(ss), 1e-12)
    k2 = k * (1.0 + (a - 1.0) * ka_ref[...])
    G = min(RW_GROUP, tb)
    log_dec = -jnp.exp(w)
    bet = kk * a
    if n_valid < tb:
        live = rows < n_valid
        log_dec, kk, bet, k2, v_in = (jnp.where(live, x, 0.0) for x in (log_dec, kk, bet, k2, v))
    else:
        v_in = v
    cum = _dot_exact_lhs(lgrp_ref[...], log_dec, terms=2)
    cum_end = _dot_exact_lhs(ggrp_ref[...], log_dec, terms=2)
    gam_inv = jnp.exp(-cum)
    gam_end = jnp.exp(cum_end - cum)
    k_hat = -kk * jnp.exp(cum - log_dec)
    r_hat = r * jnp.exp(cum)
    b_chk, k_chk = bet * gam_inv, k2 * gam_inv
    b_til, k_til = bet * gam_end, k2 * gam_end
    gam_group = jnp.exp(cum_end)

    lane = lax.broadcasted_iota(jnp.int32, (1, LANE), 1)
    low = lane < HD_RWKV
    lane_t = lane & (RW_BLOCK - 1)
    row = lax.broadcasted_iota(jnp.int32, (LANE, 1), 0)
    row_t = row & (RW_BLOCK - 1)
    same = ((row < HD_RWKV) == low) & ((row_t // G) == (lane_t // G))
    strict, incl = same & (lane_t < row_t), same & (lane_t <= row_t)
    bf = lambda x: x.astype(BF16)
    mm = lambda x, y: jnp.dot(bf(x), bf(y), preferred_element_type=F32)
    mm_nt = lambda x, y: lax.dot_general(bf(x), bf(y), (((1,), (1,)), ((), ())), preferred_element_type=F32)

    def rows_bd(x):
        if tb < RW_BLOCK:
            x = jnp.concatenate([x, jnp.zeros((RW_BLOCK - tb, LANE), F32)], axis=0)
        return jnp.concatenate([jnp.where(low, x, 0.0), jnp.where(low, 0.0, x)], axis=0)

    def mm3(x, y):
        xh, yh = bf(x), bf(y)
        xl, yl = bf(x - xh.astype(F32)), bf(y - yh.astype(F32))
        return jnp.dot(jnp.concatenate([xh, xl, xh], axis=1), jnp.concatenate([yh, yh, yl], axis=0), preferred_element_type=F32)

    units = max(1, tb // RW_BLOCK)
    unit_rows = min(tb, RW_BLOCK)
    pairs = range(RW_PAIRS)
    items = [(un, pp) for un in range(units) for pp in pairs]
    at = lambda x, it: x[it[0] * unit_rows:(it[0] + 1) * unit_rows, it[1] * LANE:(it[1] + 1) * LANE]
    kh_row = [rows_bd(at(k_hat, it)) for it in items]
    rh_row = [rows_bd(at(r_hat, it)) for it in items]
    kh_mat = [x.T for x in kh_row]
    rh_mat = [x.T for x in rh_row]
    state_in = [jnp.concatenate([rows_bd(at(b_chk, it)).T, rows_bd(at(k_chk, it)).T], axis=1) for it in items]
    upd_rows = [jnp.concatenate([rows_bd(at(b_til, it)), rows_bd(at(k_til, it))], axis=0) for it in items]
    v_t = [rows_bd(at(v_in, it)).T for it in items]
    v_t = [x[:HD_RWKV] + x[HD_RWKV:] for x in v_t]
    c_all = [mm(jnp.concatenate([kh_row[n], rh_row[n]], axis=0), state_in[n]) for n in range(len(items))]
    c_uu = [jnp.where(strict, c[:LANE, :LANE], 0.0) for c in c_all]
    c_uv = [jnp.where(strict, c[:LANE, LANE:], 0.0) for c in c_all]
    c_ru = [jnp.where(incl, c[LANE:, :LANE], 0.0) for c in c_all]
    c_rv = [jnp.where(incl, c[LANE:, LANE:], 0.0) for c in c_all]
    t_neu, power = list(c_uu), list(c_uu)
    span = 2
    while span < G:
        power = [mm(x, x) for x in power]
        t_neu = [t_neu[n] + power[n] + mm(t_neu[n], power[n]) for n in range(len(items))]
        span *= 2
    from_v = [mm_nt(v_t[n], c_uv[n]) for n in range(len(items))]
    st = [s_scr[pp] for pp in pairs]
    y_units = []
    for un in range(units):
        ns = [un * RW_PAIRS + pp for pp in pairs]
        y_t = [jnp.zeros((HD_RWKV, LANE), F32) for _ in pairs]
        for grp in range(unit_rows // G):
            here = (lane_t // G) == grp
            first = un * unit_rows + grp * G
            w_t = [jnp.where(here, mm(st[pp], kh_mat[ns[pp]]) + from_v[ns[pp]], 0.0) for pp in pairs]
            u_t = [w_t[pp] + mm_nt(w_t[pp], t_neu[ns[pp]]) for pp in pairs]
            v_g = [jnp.where(here, v_t[ns[pp]], 0.0) for pp in pairs]
            y_t = [y_t[pp] + jnp.where(here, mm(st[pp], rh_mat[ns[pp]]), 0.0) + mm_nt(u_t[pp], c_ru[ns[pp]])
                   + mm_nt(v_g[pp], c_rv[ns[pp]]) for pp in pairs]
            st = [st[pp] * gam_group[first:first + 1, pp * LANE:(pp + 1) * LANE]
                  + mm3(jnp.concatenate([u_t[pp], v_g[pp]], axis=1), upd_rows[ns[pp]]) for pp in pairs]
        lane_u = lax.broadcasted_iota(jnp.int32, (unit_rows, LANE), 1)
        pieces = []
        for pp in pairs:
            yt = jnp.concatenate([y_t[pp], jnp.zeros((LANE - HD_RWKV, LANE), F32)], axis=0).T
            pieces.append(jnp.where(lane_u < HD_RWKV, yt[:unit_rows], pltpu.roll(yt[RW_BLOCK:RW_BLOCK + unit_rows], HD_RWKV, axis=1)))
        y_units.append(jnp.concatenate(pieces, axis=1))
    for pp in pairs:
        s_scr[pp] = st[pp]
    y = y_units[0] if units == 1 else jnp.concatenate(y_units, axis=0)
    mean = head_sums(y) * (1.0 / HD_RWKV)
    yc = y - mean
    var = head_sums(yc * yc) * (1.0 / HD_RWKV)
    yn = yc * lax.rsqrt(var + GN_EPS) * gw_ref[...] + gb_ref[...]
    bonus = head_sums(r * k2 * rk_ref[...]) * v
    o_ref[...] = (yn + bonus) * g
    sout_ref[...] = s_scr[...]


def _pair_state(s):
    B = s.shape[0]
    return s.reshape(B, RW_PAIRS, 2, HD_RWKV, HD_RWKV).transpose(0, 1, 3, 2, 4).reshape(B, RW_PAIRS, HD_RWKV, LANE)


def _unpair_state(s):
    B = s.shape[0]
    return s.reshape(B, RW_PAIRS, HD_RWKV, 2, HD_RWKV).transpose(0, 1, 3, 2, 4).reshape(B, H_RWKV, HD_RWKV, HD_RWKV)


def _rwkv(pr, shift0, s0, lw, n_valid):
    B, T, _ = pr.shape
    tb = min(RW_STEP, T)
    n = D_RWKV
    vec = lambda a: a.reshape(1, n)
    padrow = lambda a: jnp.pad(a, ((0, LANE - a.shape[0]), (0, 0)))
    blk = np.arange(n // 2) // HD_RWKV
    block_ones = jnp.asarray(blk[:, None] == blk[None, :], BF16)
    tok = np.arange(tb)
    group = min(RW_GROUP, tb)
    same_group = tok[:, None] // group == tok[None, :] // group
    prefix = jnp.asarray(same_group & (tok[None, :] <= tok[:, None]), BF16)
    consts = [_rwkv_cols(lw['rwkv_mu']).reshape(1, RW_PAD), vec(lw['rwkv_w0']), vec(lw['rwkv_a0']), vec(lw['rwkv_k_k']),
              vec(lw['rwkv_k_a']), vec(lw['rwkv_r_k']), vec(lw['rwkv_gn_w']), vec(lw['rwkv_gn_b']),
              padrow(lw['rwkv_w2']), padrow(lw['rwkv_a2']), padrow(lw['rwkv_g2']), block_ones, prefix, jnp.asarray(same_group, BF16)]
    kern = functools.partial(_rwkv_kernel, n_valid=n_valid)
    state_spec = pl.BlockSpec((None, RW_PAIRS, HD_RWKV, LANE), lambda b, j: (b, 0, 0, 0))
    o, s = pl.pallas_call(
        kern,
        grid=(B, T // tb),
        in_specs=[pl.BlockSpec((None, tb, RW_PAD), lambda b, j: (b, j, 0)),
                  pl.BlockSpec((None, 1, RW_PAD), lambda b, j: (b, 0, 0)), state_spec]
                 + [_resident(c.shape) for c in consts],
        out_specs=[pl.BlockSpec((None, tb, n), lambda b, j: (b, j, 0)), state_spec],
        out_shape=[jax.ShapeDtypeStruct((B, T, n), F32), jax.ShapeDtypeStruct((B, RW_PAIRS, HD_RWKV, LANE), F32)],
        scratch_shapes=[pltpu.VMEM((1, RW_PAD), F32), pltpu.VMEM((RW_PAIRS, HD_RWKV, LANE), F32)],
        compiler_params=_params(("arbitrary", "arbitrary")),
        name="rwkv",
    )(pr, shift0, _pair_state(s0), *consts)
    return o, _unpair_state(s)


CMP_PAGES = 64
CHUNKS_PER_PAGE = PAGE_SIZE // CMP_STRIDE


def _compress_kernel(pt_ref, *refs, n_pages, transposed):
    pages, nxt = refs[:n_pages], refs[n_pages]
    weights = refs[n_pages + 1:n_pages + 9]
    outs = refs[n_pages + 9:n_pages + 11]
    width = N_KV * HD
    rows = CHUNKS_PER_PAGE * n_pages
    seg = rows + 8
    kinds = range(2)
    low = lax.broadcasted_iota(jnp.int32, (1, N_KV * HD), 1) < HD
    rows_scr = refs[n_pages + 11:n_pages + 13]

    def by_head(row_s):
        heads = [[], []]
        for s in range(0, CMP_STRIDE, 2):
            a, b = row_s(s), row_s(s + 1)
            heads[0].append(jnp.where(low, a, pltpu.roll(b, HD, axis=1)))
            heads[1].append(jnp.where(low, pltpu.roll(a, HD, axis=1), b))
        return [jnp.concatenate(h, axis=1) for h in heads]

    for kind in kinds:
        pe_ref, w_ref, b_ref, w2_ref = weights[4 * kind:4 * kind + 4]
        part = slice(kind * width, (kind + 1) * width)
        for p, pg in enumerate(pages):
            rows_scr[kind][p * PAGE_SIZE:(p + 1) * PAGE_SIZE, :] = pg[part, :].T if transposed else pg[:, part]
        nxt_rows = nxt[part, :].T[:CMP_STRIDE] if transposed else nxt[:, part]
        x = by_head(lambda s: rows_scr[kind][pl.ds(s, rows, stride=CMP_STRIDE), :])
        x_next = by_head(lambda s: jnp.broadcast_to(nxt_rows[s:s + 1, :], (8, N_KV * HD)))
        x_all = jnp.concatenate([x[0], x_next[0], x[1], x_next[1]], axis=0)
        h_first = jnp.dot((x_all + pe_ref[0]).astype(BF16), w_ref[0], preferred_element_type=F32)
        h_second = jnp.dot((x_all + pe_ref[1]).astype(BF16), w_ref[1], preferred_element_type=F32)
        out = None
        for h in range(N_KV):
            h_next = pltpu.roll(h_second[h * seg:(h + 1) * seg], seg - 1, axis=0)[:rows]
            hidden = jax.nn.gelu(h_first[h * seg:h * seg + rows] + h_next + b_ref[...])
            part = jnp.dot(hidden.astype(BF16), w2_ref[h], preferred_element_type=F32)
            out = part if out is None else out + part
        outs[kind][...] = out


def _compress_weights(pe, w1, b1, w2):
    n = CMP_STRIDE * HD
    pe2 = pe.reshape(2, 1, n)
    w_halves = w1.reshape(2, n, CMP_HIDDEN).astype(BF16)
    zero = jnp.zeros_like(w2)
    w2_heads = jnp.stack([jnp.concatenate([w2, zero], axis=1), jnp.concatenate([zero, w2], axis=1)]).astype(BF16)
    return [pe2, w_halves, b1.reshape(1, -1), w2_heads]


def _compress(pool, page_table, lw, transposed):
    B, n_pages_total = page_table.shape
    n_pages = min(CMP_PAGES, n_pages_total)
    rows = CHUNKS_PER_PAGE * n_pages
    weights = (_compress_weights(lw['cmp_pe_k'], lw['cmp_w1_k'], lw['cmp_b1_k'], lw['cmp_w2_k'])
               + _compress_weights(lw['cmp_pe_v'], lw['cmp_w1_v'], lw['cmp_b1_v'], lw['cmp_w2_v']))
    width = N_KV * HD

    def page_map(p):
        return lambda b, j, pt: (pt[b, j * n_pages + p], 0, 0)

    next_map = lambda b, j, pt: (pt[b, jnp.minimum((j + 1) * n_pages, n_pages_total - 1)], 0, 0)
    page_block = (None, 2 * width, PAGE_SIZE) if transposed else (None, PAGE_SIZE, 2 * width)
    next_block = page_block if transposed else (None, CMP_STRIDE, 2 * width)
    const = lambda a: pl.BlockSpec(a.shape, lambda b, j, pt: (0,) * a.ndim)
    out_spec = pl.BlockSpec((None, rows, N_KV * HD), lambda b, j, pt: (b, j, 0))
    out_shape = jax.ShapeDtypeStruct((B, n_pages_total * CHUNKS_PER_PAGE, N_KV * HD), F32)
    grid_spec = pltpu.PrefetchScalarGridSpec(
        num_scalar_prefetch=1,
        grid=(B, n_pages_total // n_pages),
        in_specs=[pl.BlockSpec(page_block, page_map(p)) for p in range(n_pages)]
                 + [pl.BlockSpec(next_block, next_map)] + [const(a) for a in weights],
        out_specs=[out_spec, out_spec],
        scratch_shapes=[pltpu.VMEM((n_pages * PAGE_SIZE, width), F32)] * 2,
    )
    return pl.pallas_call(
        functools.partial(_compress_kernel, n_pages=n_pages, transposed=transposed),
        grid_spec=grid_spec,
        out_shape=[out_shape, out_shape],
        compiler_params=_params(("arbitrary", "arbitrary")),
        name="compress",
    )(page_table, *([pool] * (n_pages + 1)), *weights)


BAND_ROWS = 1152


def _band_kernel(tab_ref, bkt_ref, o_ref):
    h = pl.program_id(0)
    bkt = bkt_ref[...]
    out = jnp.full(bkt.shape, NEG, F32)
    for b in range(N_BUCKETS):
        out = jnp.where(bkt == b, tab_ref[b, h], out)
    o_ref[...] = out


def _band(rel_table):
    u = np.arange(BAND_ROWS)[:, None]
    qi = np.arange(Q_BLOCK)[None, :]
    d = qi + WINDOW - u
    bkt = np.where(d >= 0, _BUCKET_OF[np.clip(d, 0, len(_BUCKET_OF) - 1)], -1).astype(np.int32)
    return pl.pallas_call(
        _band_kernel,
        grid=(H_NSA,),
        in_specs=[pl.BlockSpec(memory_space=pltpu.SMEM), pl.BlockSpec((BAND_ROWS, Q_BLOCK), lambda h: (0, 0))],
        out_specs=pl.BlockSpec((None, BAND_ROWS, Q_BLOCK), lambda h: (h, 0, 0)),
        out_shape=jax.ShapeDtypeStruct((H_NSA, BAND_ROWS, Q_BLOCK), F32),
        compiler_params=_params(("arbitrary",)),
        name="band",
    )(rel_table, jnp.asarray(bkt))


def _softmax_cols(s):
    m = jnp.max(s, axis=0, keepdims=True)
    e = jnp.exp(s - m)
    l = jnp.sum(e, axis=0, keepdims=True)
    return e * jnp.where(m > 0.5 * NEG, 1.0 / l, 0.0)


def _select_blocks(impsel, qpos, n_pick):
    ns = impsel.shape[0]
    blk = lax.broadcasted_iota(jnp.int32, impsel.shape, 0)
    cur = jnp.right_shift(qpos, SEL_BLOCK.bit_length() - 1)
    future = blk * SEL_BLOCK > qpos
    forced = (blk == 0) | (blk == cur) | (blk == cur - 1)
    score = jnp.where(future, -jnp.inf, jnp.where(forced, jnp.inf, impsel))
    chosen = jnp.zeros(impsel.shape, F32)
    for _ in range(n_pick):
        best = jnp.max(score, axis=0, keepdims=True)
        first = jnp.min(jnp.where(score == best, blk, ns), axis=0, keepdims=True)
        hit = (blk == first) & (best > -jnp.inf)
        chosen = jnp.where(hit, 1.0, chosen)
        score = jnp.where(hit, -jnp.inf, score)
    return jnp.where(chosen > 0.0, 0.0, NEG)


def _pool_matrix(ns, nc):
    j = np.arange(ns)[:, None]
    n = np.arange(nc)[None, :]
    ratio = SEL_BLOCK // CMP_STRIDE
    return jnp.asarray((n >= ratio * j - 1) & (n <= ratio * j + ratio - 1), BF16)


def _nsa_prompt_kernel(tab_ref, q_ref, g_ref, kc_ref, vct_ref, ks_ref, vst_ref, kw_ref, vwt_ref, band_ref, pool_ref, o_ref,
                       rhs_scr, mask_scr, acc_scr, m_scr, sc_scr, sa_scr, sb_scr, oc_scr):
    i = pl.program_id(0)
    ncp = kc_ref.shape[0]
    ns = pool_ref.shape[0]
    s0 = i * Q_BLOCK
    q_t = (q_ref[...] * HD ** -0.5).T
    g_t = g_ref[...].T
    lane_q = lax.broadcasted_iota(jnp.int32, (1, G_NSA * Q_BLOCK), 1) & (Q_BLOCK - 1)
    qpos = s0 + lax.broadcasted_iota(jnp.int32, (1, Q_BLOCK), 1)
    rhs_scr[...] = jnp.zeros(rhs_scr.shape, BF16)
    kvs = range(N_KV)
    kd = i // Q_PER_TILE
    r = i % Q_PER_TILE
    lanes4 = lambda k, f: jnp.concatenate([f(G_NSA * k + g) for g in range(G_NSA)], axis=1)
    qcols = [lanes4(k, lambda h: q_t[h * HD:(h + 1) * HD, :]) for k in kvs]
    zero = jnp.zeros_like(qcols[0])
    top = [jnp.concatenate([qcols[0], zero], axis=0).astype(BF16), jnp.concatenate([zero, qcols[1]], axis=0).astype(BF16)]
    far_row = [lanes4(k, lambda h: band_ref[h, 0:1, :]) for k in kvs]


    n0 = pl.multiple_of(jnp.clip(CMP_PER_Q * i - EDGE_ROWS // 2, 0, ncp - EDGE_ROWS), 8)
    nrow = lax.broadcasted_iota(jnp.int32, (ncp, 1), 0)
    d_edge = qpos - (CMP_STRIDE * (n0 + lax.broadcasted_iota(jnp.int32, (EDGE_ROWS, 1), 0)) + CMP_BLOCK - 1)

    def compressed(k, rows):
        sc_scr[k, 0:rows, :] = (jnp.dot(kc_ref[0:rows, :], top[k], preferred_element_type=F32)
                                + jnp.where(nrow[0:rows] < n0, far_row[k], NEG))
        edge_bias = lanes4(k, lambda h: jnp.where(d_edge >= 0, _bias_chain(d_edge, [tab_ref[b, h] for b in range(N_BUCKETS)]), NEG))
        sc_scr[k, pl.ds(n0, EDGE_ROWS), :] = jnp.dot(kc_ref[pl.ds(n0, EDGE_ROWS), :], top[k], preferred_element_type=F32) + edge_bias
        p_c = _softmax_cols(sc_scr[k, 0:rows, :])
        o_c = jnp.dot(vct_ref[:, 0:rows], p_c.astype(BF16), preferred_element_type=F32)[k * HD:(k + 1) * HD]
        imp = p_c[:, 0:Q_BLOCK]
        for g in range(1, G_NSA):
            imp = imp + p_c[:, g * Q_BLOCK:(g + 1) * Q_BLOCK]
        n_blk = rows * CMP_STRIDE // SEL_BLOCK
        return o_c, _dot_exact_lhs(pool_ref[0:n_blk, 0:rows], imp, terms=2)

    size_step = min(CMP_CLASS_ROWS, ncp)
    size_class = (n0 + EDGE_ROWS - 1) // size_step
    for cls in range(ncp // size_step):
        @pl.when(size_class == cls)
        def _(rows=(cls + 1) * size_step):
            comp = [compressed(k, rows) for k in kvs]
            n_blk = comp[0][1].shape[0]
            masks = [_select_blocks(comp[k][1], qpos, min(N_SEL, ns)) for k in kvs]
            for k in kvs:
                oc_scr[k] = comp[k][0]
                full = jnp.concatenate([masks[k], jnp.full((ns - n_blk, Q_BLOCK), NEG, F32)], axis=0) if n_blk < ns else masks[k]
                mask_scr[k] = jnp.concatenate([full.astype(BF16)] * G_NSA, axis=1)

    ws = pl.multiple_of(jnp.maximum(s0 - WINDOW, 0), Q_BLOCK)
    u0 = pl.multiple_of(WINDOW - (s0 - ws), Q_BLOCK)
    n_win = WINDOW + Q_BLOCK
    u = u0 + lax.broadcasted_iota(jnp.int32, (n_win, 1), 0)
    win_mask = jnp.where(u > lane_q, 0.0, NEG)

    def window(k):
        s_w = (jnp.dot(kw_ref[pl.ds(ws, n_win), :], top[k], preferred_element_type=F32)
               + lanes4(k, lambda h: band_ref[h, pl.ds(u0, n_win), :]) + win_mask)
        m_w = jnp.max(s_w, axis=0, keepdims=True)
        p_w = jnp.exp(s_w - m_w).astype(BF16)
        acc_w = jnp.zeros((V_ROWS_KV, G_NSA * Q_BLOCK), F32)
        for j in range(n_win // Q_BLOCK):
            acc_w = acc_w + jnp.dot(vwt_ref[k, ws // Q_BLOCK + j], p_w[j * Q_BLOCK:(j + 1) * Q_BLOCK], preferred_element_type=F32)
        return acc_w[0:HD] / acc_w[HD:HD + 1]

    o_w = [window(k) for k in kvs]

    for k in kvs:
        m_scr[k] = jnp.full(m_scr.shape[1:], M_INIT, F32)
        acc_scr[k] = jnp.zeros(acc_scr.shape[1:], F32)
        far_hi = far_row[k].astype(BF16).astype(F32)
        rhs_scr[k, 0:HD, :] = qcols[k].astype(BF16)
        rhs_scr[k, SEL_FAR0:SEL_FAR0 + MASK_ROWS, :] = jnp.concatenate(
            [far_hi, far_row[k] - far_hi, jnp.zeros((MASK_ROWS - 2, G_NSA * Q_BLOCK), F32)], axis=0).astype(BF16)

    def scores(k, slab, kts, extra):
        rhs_scr[k, SEL_MASK0:SEL_MASK0 + MASK_ROWS, :] = mask_scr[k, pl.ds(pl.multiple_of(slab * MASK_ROWS, MASK_ROWS), MASK_ROWS), :]
        rhs = rhs_scr[k]
        out = []
        for kt, add in zip(kts, extra):
            s = jnp.dot(ks_ref[k, pl.ds(pl.multiple_of(kt * KEY_TILE, KEY_TILE), KEY_TILE), :], rhs, preferred_element_type=F32)
            out.append(s if add is None else s + add)
        return out

    def update(k, kts, tiles):
        m_old = m_scr[k]
        m_new = m_old
        for s in tiles:
            m_new = jnp.maximum(m_new, jnp.max(s, axis=0, keepdims=True))
        acc = jnp.exp(m_old - m_new) * acc_scr[k]
        for kt, s in zip(kts, tiles):
            acc = acc + jnp.dot(vst_ref[k, kt], jnp.exp(s - m_new).astype(BF16), preferred_element_type=F32)
        acc_scr[k] = acc
        m_scr[k] = m_new

    def attend(slab, kts, extra):
        tiles = [scores(k, slab, kts, extra(k)) for k in kvs]
        for k in kvs:
            update(k, kts, tiles[k])

    near_at = lambda k, start: lanes4(k, lambda h: band_ref[h, pl.ds(pl.multiple_of(start, Q_BLOCK), KEY_TILE), :]) - far_row[k]
    prev_near = (r == 0) & (kd >= 1)
    kd_odd = (kd & 1) == 1
    even_prev = jnp.logical_not(kd_odd) & prev_near
    n_pairs = kd // 2 - even_prev.astype(jnp.int32)
    n_quads = n_pairs // 2
    no_bias = lambda k: [None, None]

    def pair_scores(k, dst, pair):
        lo, hi = scores(k, pair, [2 * pair, 2 * pair + 1], [None, None])
        dst[k, 0:KEY_TILE, :] = lo
        dst[k, KEY_TILE:, :] = hi

    def pair_update(k, src, pair):
        update(k, [2 * pair, 2 * pair + 1], [src[k, 0:KEY_TILE, :], src[k, KEY_TILE:, :]])

    @pl.when(n_quads > 0)
    def _():
        for k in kvs:
            pair_scores(k, sa_scr, 0)

    for k in kvs:
        def quad_body(qd, carry, k=k):
            first = 2 * qd
            pair_scores(k, sb_scr, first + 1)
            pair_update(k, sa_scr, first)
            pair_scores(k, sa_scr, jnp.minimum(first + 2, 2 * n_quads - 2))
            pair_update(k, sb_scr, first + 1)
            return carry

        lax.fori_loop(0, n_quads, quad_body, 0)

    @pl.when((n_pairs & 1) == 1)
    def _():
        attend(n_pairs - 1, [2 * n_pairs - 2, 2 * n_pairs - 1], no_bias)

    @pl.when(kd_odd)
    def _():
        attend(kd // 2, [kd - 1, kd], lambda k: [jnp.where(prev_near, near_at(k, 0), 0.0), near_at(k, KEY_TILE - Q_BLOCK * r)])

    @pl.when(even_prev)
    def _():
        attend(kd // 2 - 1, [kd - 2, kd - 1], lambda k: [None, near_at(k, 0)])

    @pl.when(jnp.logical_not(kd_odd))
    def _():
        attend(kd // 2, [kd], lambda k: [near_at(k, KEY_TILE - Q_BLOCK * r)])

    heads_out = []
    for k in kvs:
        acc = acc_scr[k]
        o_s = acc[0:HD] / acc[HD:HD + 1]
        o_c = oc_scr[k]
        for g in range(G_NSA):
            h = G_NSA * k + g
            cols = slice(g * Q_BLOCK, (g + 1) * Q_BLOCK)
            heads_out.append(o_c[:, cols] * g_t[h:h + 1] + o_s[:, cols] * g_t[H_NSA + h:H_NSA + h + 1]
                             + o_w[k][:, cols] * g_t[2 * H_NSA + h:2 * H_NSA + h + 1])
    o_ref[...] = jnp.concatenate(heads_out, axis=0).T


def _sel_pattern(rows, width):
    key = np.arange(rows)[:, None]
    b = np.arange(width)[None, :]
    ones = (b >= MASK_ROWS) & (b < MASK_ROWS + 2)
    return jnp.asarray(((key // SEL_BLOCK) % MASK_ROWS == b) | ones, BF16)


def _nsa_prompt(q, gates, attn, kc, vc, band, rel_table):
    T = q.shape[0]
    ncp, ns = kc.shape[0], T // SEL_BLOCK
    width = G_NSA * Q_BLOCK
    ks_aug, vs_t, kw, vw_t = attn
    operands = [q, gates, kc.astype(BF16), vc.T.astype(BF16), ks_aug, vs_t, kw, vw_t, band, _pool_matrix(ns, ncp)]
    blk = lambda w: pl.BlockSpec((Q_BLOCK, w), lambda i: (i, 0))
    return pl.pallas_call(
        _nsa_prompt_kernel,
        grid=(T // Q_BLOCK,),
        in_specs=[pl.BlockSpec(memory_space=pltpu.SMEM), blk(H_NSA * HD), blk(LANE)] + [_resident(a.shape) for a in operands[2:]],
        out_specs=blk(H_NSA * HD),
        out_shape=jax.ShapeDtypeStruct((T, H_NSA * HD), F32),
        scratch_shapes=[pltpu.VMEM((N_KV, LANE, width), BF16), pltpu.VMEM((N_KV, ns, width), BF16),
                        pltpu.VMEM((N_KV, V_ROWS_KV, width), F32), pltpu.VMEM((N_KV, 1, width), F32),
                        pltpu.VMEM((N_KV, ncp, width), F32)] + [pltpu.VMEM((N_KV, 2 * KEY_TILE, width), F32)] * 2
                       + [pltpu.VMEM((N_KV, HD, width), F32)],
        compiler_params=_params(("arbitrary",)),
        name="nsa_prompt",
    )(rel_table, *operands)


SLAB_PAGES = 8
SMP_PAGES = 64
TOK_PAD = 8
SMP_COLS = H_NSA * TOK_PAD


def _nsa_sample_kernel(pt_ref, *refs, n_pages, n_valid, past):
    pages = refs[:n_pages]
    (q_ref, g_ref, kc_ref, vc_ref, kvn_ref, win_ref, winn_ref, tab_ref, pool_ref, gsum_ref, epat_ref, o_ref,
     mask_scr, acc_scr, m_scr, l_scr, oc_scr, ow_scr, qrow_scr, maskt_scr, farc_scr, nearbt_scr, m2_scr, l2_scr, acc2_scr) = refs[n_pages:]
    j = pl.program_id(1)
    ncp, wbuf = kc_ref.shape[0], win_ref.shape[1]
    lane = lax.broadcasted_iota(jnp.int32, (1, LANE), 1)
    tok = lane & (TOK_PAD - 1)
    second_kv = lane >= G_NSA * TOK_PAD
    tab = [tab_ref[b:b + 1, :] for b in range(N_BUCKETS)]
    far_row = tab[N_BUCKETS - 1]
    own_rows = lambda x: jnp.where(second_kv, x[HD:2 * HD], x[0:HD])
    pad_rows = lambda x: jnp.concatenate([x, jnp.zeros((LANE - x.shape[0], x.shape[1]), x.dtype)], axis=0)
    trow = lax.broadcasted_iota(jnp.int32, (LANE, 1), 0)
    d_new = tok - trow
    new_bias = jnp.where((d_new >= 0) & (trow < n_valid), _bias_chain(jnp.maximum(d_new, 0), tab), NEG)

    def attend_update(s, values_t):
        m_old = m_scr[...]
        m_new = jnp.maximum(m_old, jnp.max(s, axis=0, keepdims=True))
        alpha = jnp.exp(m_old - m_new)
        p = jnp.exp(s - m_new)
        l_scr[...] = alpha * l_scr[...] + jnp.sum(p, axis=0, keepdims=True)
        acc_scr[...] = alpha * acc_scr[...] + jnp.dot(values_t.astype(BF16), p.astype(BF16), preferred_element_type=F32)
        m_scr[...] = m_new

    def reset():
        m_scr[...] = jnp.full(m_scr.shape, M_INIT, F32)
        l_scr[...] = jnp.zeros(l_scr.shape, F32)
        acc_scr[...] = jnp.zeros(acc_scr.shape, F32)

    @pl.when(j == 0)
    def _():
        q_t = pad_rows(q_ref[...] * HD ** -0.5).T
        halves = []
        for k in range(N_KV):
            part = jnp.zeros((HD, LANE), F32)
            for g in range(G_NSA):
                h = G_NSA * k + g
                piece = q_t[h * HD:(h + 1) * HD, :]
                part = part + (pltpu.roll(piece, TOK_PAD * h, axis=1) if h else piece)
            halves.append(part)
        top_f = jnp.concatenate(halves, axis=0)
        top = top_f.astype(BF16)
        qrow_scr[...] = top_f.T[:SMP_COLS].astype(BF16)
        qpos = past + tok

        n0 = ncp - EDGE_ROWS
        kcb = kc_ref[...].astype(BF16)
        d_edge = qpos - (CMP_STRIDE * (n0 + lax.broadcasted_iota(jnp.int32, (EDGE_ROWS, 1), 0)) + CMP_BLOCK - 1)
        s_c = jnp.concatenate([
            jnp.dot(kcb[:n0], top, preferred_element_type=F32) + far_row,
            jnp.dot(kcb[n0:], top, preferred_element_type=F32) + jnp.where(d_edge >= 0, _bias_chain(jnp.maximum(d_edge, 0), tab), NEG)], axis=0)
        p_c = _softmax_cols(s_c)
        oc_scr[...] = own_rows(jnp.dot(vc_ref[...].T.astype(BF16), p_c.astype(BF16), preferred_element_type=F32))
        imp = _dot_exact_rhs(p_c, gsum_ref[...], terms=3)
        mask_scr[...] = _select_blocks(_dot_exact_lhs(pool_ref[...], imp), qpos, N_SEL)

        wk = win_ref[0:LANE, :].T
        d_w = wbuf + tok - lax.broadcasted_iota(jnp.int32, (wbuf, 1), 0)
        near = wbuf - LANE
        s_w = jnp.dot(wk.astype(BF16), top, preferred_element_type=F32)
        s_w = (jnp.concatenate([s_w[:near] + far_row, s_w[near:] + _bias_chain(d_w[near:], tab)], axis=0)
               + jnp.where(d_w < WINDOW, 0.0, NEG))
        reset()
        attend_update(s_w, win_ref[LANE:, :])
        wn = pad_rows(winn_ref[...])
        attend_update(jnp.dot(wn[:, :LANE].astype(BF16), top, preferred_element_type=F32) + new_bias, wn[:, LANE:].T)
        ow_scr[...] = own_rows(acc_scr[...]) / l_scr[...]
        m2_scr[...] = jnp.full(m2_scr.shape, M_INIT, F32)
        l2_scr[...] = jnp.zeros(l2_scr.shape, F32)
        acc2_scr[...] = jnp.zeros(acc2_scr.shape, F32)
        for sl in range(maskt_scr.shape[0]):
            maskt_scr[sl] = pad_rows(mask_scr[sl * MASK_ROWS:(sl + 1) * MASK_ROWS, :]).T[:SMP_COLS].astype(BF16)
        farc_scr[...] = pad_rows(tab_ref[...]).T[:SMP_COLS, N_BUCKETS - 1:N_BUCKETS]
        nearbt_scr[...] = _bias_chain(LANE + tok - lax.broadcasted_iota(jnp.int32, (LANE, 1), 0), tab).T[:SMP_COLS]

    def rows_update(s, values_t):
        m_old = m2_scr[...]
        m_new = jnp.maximum(m_old, jnp.max(s, axis=1, keepdims=True))
        alpha = jnp.exp(m_old - m_new)
        p = jnp.exp(s - m_new)
        l2_scr[...] = alpha * l2_scr[...] + jnp.sum(p, axis=1, keepdims=True)
        acc2_scr[...] = alpha * acc2_scr[...] + lax.dot_general(p.astype(BF16), values_t.astype(BF16), (((1,), (1,)), ((), ())),
                                                                preferred_element_type=F32)
        m2_scr[...] = m_new

    n_slabs = n_pages // SLAB_PAGES
    slab_keys = SLAB_PAGES * PAGE_SIZE
    far_col = farc_scr[...]
    for sub in range(n_slabs):
        tile_pages = pages[sub * SLAB_PAGES:(sub + 1) * SLAB_PAGES]
        kt_tile = jnp.concatenate([pg[0:LANE, :] for pg in tile_pages], axis=1)
        vt_tile = jnp.concatenate([pg[LANE:, :] for pg in tile_pages], axis=1)
        lhs = jnp.concatenate([qrow_scr[...], maskt_scr[j * n_slabs + sub]], axis=1)
        s = jnp.dot(lhs, jnp.concatenate([kt_tile.astype(BF16), epat_ref[...]], axis=0), preferred_element_type=F32)
        if sub < n_slabs - 1:
            s = s + far_col
        else:
            near = slab_keys - LANE
            s = jnp.concatenate([s[:, :near] + far_col, s[:, near:] + jnp.where(j == pl.num_programs(1) - 1, nearbt_scr[...], far_col)], axis=1)
        rows_update(s, vt_tile)

    @pl.when(j == pl.num_programs(1) - 1)
    def _():
        kn = pad_rows(kvn_ref[...])
        last_blk = past // SEL_BLOCK
        new_mask = maskt_scr[last_blk // MASK_ROWS][:, last_blk % MASK_ROWS:last_blk % MASK_ROWS + 1].astype(F32)
        s_n = (jnp.dot(qrow_scr[...], kn[:, 2 * LANE:3 * LANE].T.astype(BF16), preferred_element_type=F32)
               + new_bias.T[:SMP_COLS] + new_mask)
        rows_update(s_n, kn[:, 3 * LANE:].T)
        acc_t = pad_rows(acc2_scr[...]).T
        l_t = pad_rows(jnp.broadcast_to(l2_scr[...], (SMP_COLS, LANE))).T[0:1]
        o_s = own_rows(acc_t) / jnp.where(lane < SMP_COLS, l_t, 1.0)
        g_t = pad_rows(g_ref[...]).T
        gate_rows = []
        for b in range(3):
            row = g_t[b * H_NSA:b * H_NSA + 1]
            for h in range(1, H_NSA):
                row = row + pltpu.roll(g_t[b * H_NSA + h:b * H_NSA + h + 1], TOK_PAD * h, axis=1)
            gate_rows.append(row)
        o_col = oc_scr[...] * gate_rows[0] + o_s * gate_rows[1] + ow_scr[...] * gate_rows[2]
        per_head = [o_col if h == 0 else pltpu.roll(o_col, LANE - TOK_PAD * h, axis=1) for h in range(H_NSA)]
        o_ref[...] = jnp.concatenate(per_head, axis=0).T[:TOK_PAD]


def _nsa_sample(pool, page_table, q, gates, kc, vc, kv_new, win_buf, win_new, rel_table, n_valid):
    B, n_pages_total = page_table.shape
    past = n_pages_total * PAGE_SIZE
    ncp = kc.shape[1]
    ns = past // SEL_BLOCK + 1
    nsp = -(-ns // MASK_ROWS) * MASK_ROWS
    col = np.arange(LANE)
    used = col < SMP_COLS
    gsum = jnp.asarray(((col[:, None] // (G_NSA * TOK_PAD) == col[None, :] // (G_NSA * TOK_PAD))
                        & (col[:, None] % TOK_PAD == col[None, :] % TOK_PAD) & used[:, None] & used[None, :]), BF16)
    tab_cols = jnp.pad(jnp.repeat(rel_table, TOK_PAD, axis=1), ((0, 0), (0, LANE - SMP_COLS)))
    n_step = min(SMP_PAGES, n_pages_total)
    consts = [tab_cols, _pool_matrix(nsp, ncp), gsum, _sel_pattern(SLAB_PAGES * PAGE_SIZE, LANE).T]
    per_seq = [q, gates, kc, vc, kv_new, win_buf, win_new]
    seq_spec = lambda a: pl.BlockSpec((None,) + a.shape[1:], lambda b, j, pt: (b,) + (0,) * (a.ndim - 1))
    const = lambda a: pl.BlockSpec(a.shape, lambda b, j, pt: (0,) * a.ndim)

    def page_map(p):
        return lambda b, j, pt: (pt[b, j * n_step + p], 1, 0)

    grid_spec = pltpu.PrefetchScalarGridSpec(
        num_scalar_prefetch=1,
        grid=(B, n_pages_total // n_step),
        in_specs=[pl.BlockSpec((None, 2 * LANE, PAGE_SIZE), page_map(p)) for p in range(n_step)]
                 + [seq_spec(a) for a in per_seq] + [const(a) for a in consts],
        out_specs=pl.BlockSpec((None, TOK_PAD, H_NSA * HD), lambda b, j, pt: (b, 0, 0)),
        scratch_shapes=[pltpu.VMEM((nsp, LANE), F32), pltpu.VMEM((LANE, LANE), F32),
                        pltpu.VMEM((1, LANE), F32), pltpu.VMEM((1, LANE), F32), pltpu.VMEM((HD, LANE), F32), pltpu.VMEM((HD, LANE), F32),
                        pltpu.VMEM((SMP_COLS, LANE), BF16), pltpu.VMEM((nsp // MASK_ROWS, SMP_COLS, LANE), BF16),
                        pltpu.VMEM((SMP_COLS, 1), F32), pltpu.VMEM((SMP_COLS, LANE), F32),
                        pltpu.VMEM((SMP_COLS, 1), F32), pltpu.VMEM((SMP_COLS, 1), F32), pltpu.VMEM((SMP_COLS, LANE), F32)],
    )
    return pl.pallas_call(
        functools.partial(_nsa_sample_kernel, n_pages=n_step, n_valid=n_valid, past=past),
        grid_spec=grid_spec,
        out_shape=jax.ShapeDtypeStruct((B, TOK_PAD, H_NSA * HD), F32),
        compiler_params=_params(("arbitrary", "arbitrary")),
        name="nsa_sample",
    )(page_table, *([pool] * n_step), *per_seq, *consts)


def _out_ffn_kernel(x_ref, nsa_ref, rw_ref, g1_ref, lng1_ref, lnb1_ref, w_ref,
                    sh_ref, sc_ref, g2_ref, lng2_ref, lnb2_ref, wg_ref, wu_ref, wd_ref, o_ref):
    half = H_NSA * HD
    out = (jnp.dot(nsa_ref[...].astype(BF16), w_ref[0:half, :], preferred_element_type=F32)
           + jnp.dot(rw_ref[...].astype(BF16), w_ref[half:, :], preferred_element_type=F32))
    x2 = _layer_norm(ALPHA * x_ref[...] + (1.0 + g1_ref[...]) * out, lng1_ref[...], lnb1_ref[...])
    o_ref[...] = _ffn_block(x2, sh_ref[...], sc_ref[...], g2_ref[...], lng2_ref[...], lnb2_ref[...], wg_ref, wu_ref, wd_ref)


def _out_ffn(x, o_nsa, o_rwkv, gate1, ln_g1, ln_b1, w_out, shift, scale, gate2, ln_g2, ln_b2, wg, wu, wd):
    rows = x.shape[0]
    tm = min(512, rows)
    row = lambda i: (i, 0)
    vec = lambda: _resident((1, D_MODEL))
    return pl.pallas_call(
        _out_ffn_kernel,
        grid=(rows // tm,),
        in_specs=[pl.BlockSpec((tm, D_MODEL), row), pl.BlockSpec((tm, H_NSA * HD), row), pl.BlockSpec((tm, D_RWKV), row),
                  _mod_spec(gate1, tm), vec(), vec(), _resident(w_out.shape),
                  _mod_spec(shift, tm), _mod_spec(scale, tm), _mod_spec(gate2, tm), vec(), vec(),
                  _resident((D_MODEL, D_FF)), _resident((D_MODEL, D_FF)), _resident((D_FF, D_MODEL))],
        out_specs=pl.BlockSpec((tm, D_MODEL), row),
        out_shape=jax.ShapeDtypeStruct((rows, D_MODEL), F32),
        compiler_params=_params(("arbitrary",)),
        name="out_ffn",
    )(x, o_nsa, o_rwkv, gate1, ln_g1.reshape(1, -1), ln_b1.reshape(1, -1), w_out,
      shift, scale, gate2, ln_g2.reshape(1, -1), ln_b2.reshape(1, -1), wg, wu, wd)


def kernel(x_prompt, x_sample, cache_nsa_kv, cache_nsa_win, state_rwkv_shift, state_rwkv_wkv, page_table, c_prompt, c_sample, rel_table, w_ada, b_ada, ln_g, ln_b, ffn1_gate, ffn1_up, ffn1_down, ffn2_gate, ffn2_up, ffn2_down, w_in, w_out, cmp_pe_k, cmp_w1_k, cmp_b1_k, cmp_w2_k, cmp_pe_v, cmp_w1_v, cmp_b1_v, cmp_w2_v, rwkv_mu, rwkv_w0, rwkv_w2, rwkv_a0, rwkv_a2, rwkv_g2, rwkv_k_k, rwkv_k_a, rwkv_r_k, rwkv_gn_w, rwkv_gn_b):
    assert w_ada.shape[0] == DEPTH == 1 and x_prompt.shape[0] == 1
    l = 0
    lw = dict(cmp_pe_k=cmp_pe_k[l], cmp_w1_k=cmp_w1_k[l], cmp_b1_k=cmp_b1_k[l], cmp_w2_k=cmp_w2_k[l],
              cmp_pe_v=cmp_pe_v[l], cmp_w1_v=cmp_w1_v[l], cmp_b1_v=cmp_b1_v[l], cmp_w2_v=cmp_w2_v[l],
              rwkv_mu=rwkv_mu[l], rwkv_w0=rwkv_w0[l], rwkv_w2=rwkv_w2[l], rwkv_a0=rwkv_a0[l], rwkv_a2=rwkv_a2[l], rwkv_g2=rwkv_g2[l],
              rwkv_k_k=rwkv_k_k[l], rwkv_k_a=rwkv_k_a[l], rwkv_r_k=rwkv_r_k[l], rwkv_gn_w=rwkv_gn_w[l], rwkv_gn_b=rwkv_gn_b[l])
    T = x_prompt.shape[1]
    nb, nt = x_sample.shape[0], x_sample.shape[1]
    assert nt <= TOK_PAD
    n_seq = 1 + nb
    c_all = jnp.concatenate([c_prompt, c_sample, jnp.zeros((-n_seq % 8, D_MODEL), F32)], axis=0)
    mod = _ada(c_all, w_ada[l], b_ada[l])
    mod_p = mod[0:1].reshape(9, 1, D_MODEL)
    mod_s = jnp.repeat(mod[1:n_seq].reshape(nb, 9, D_MODEL), nt, axis=0).transpose(1, 0, 2)
    ffn1 = [w[l].astype(BF16) for w in (ffn1_gate, ffn1_up, ffn1_down)]
    ffn2 = [w[l].astype(BF16) for w in (ffn2_gate, ffn2_up, ffn2_down)]
    w_in_p = _prep_w_in(w_in[l])
    w_out_b = w_out[l].astype(BF16)

    def trunk_in(x, m, attn_operands=False):
        x1 = _ffn(x, m[0], m[1], m[2], ln_g[l, 0], ln_b[l, 0], *ffn1)
        return x1, _proj(x1, m[3], m[4], w_in_p, attn_operands)

    def trunk_out(x1, o_nsa, o_rwkv, m):
        return _out_ffn(x1, o_nsa, o_rwkv, m[5], ln_g[l, 1], ln_b[l, 1], w_out_b, m[6], m[7], m[8], ln_g[l, 2], ln_b[l, 2], *ffn2)

    xp1, (q, kv, win, gates, pr, *attn) = trunk_in(x_prompt[0], mod_p, attn_operands=True)
    o_rw, wkv_p = _rwkv(pr[None], jnp.zeros((1, 1, RW_PAD), F32), jnp.zeros((1, H_RWKV, HD_RWKV, HD_RWKV), F32), lw, min(RW_STEP, T))
    n_rows = T // PAGE_SIZE
    kc, vc = _compress(kv.reshape(n_rows, PAGE_SIZE, 4 * LANE), jnp.arange(n_rows, dtype=jnp.int32)[None], lw, transposed=False)
    o_nsa = _nsa_prompt(q, gates, attn, kc[0], vc[0], _band(rel_table), rel_table)
    y_prompt = trunk_out(xp1, o_nsa, o_rw[0], mod_p)
    kv_prompt = kv.reshape(1, 1, T, 4, N_KV, HD)
    win_prompt = win[T - min(WINDOW, T):].reshape(1, 1, -1, 2, N_KV, HD)
    shift_prompt = _rwkv_uncols(pr[T - 1]).reshape(1, 1, RWKV_COLS)

    xs1, (q_s, kv_s, win_s, gates_s, pr_s) = trunk_in(x_sample.reshape(nb * nt, D_MODEL), mod_s)
    tokens = lambda a: jnp.pad(a.reshape(nb, nt, -1), ((0, 0), (0, TOK_PAD - nt), (0, 0)))
    pr_pad = jnp.pad(pr_s.reshape(nb, nt, -1), ((0, 0), (0, RW_TOK_PAD - nt), (0, 0)))
    o_rw_s, wkv_s = _rwkv(pr_pad, _rwkv_cols(state_rwkv_shift[l])[:, None], state_rwkv_wkv[l], lw, nt)
    pool_t = jnp.transpose(cache_nsa_kv[l], (0, 2, 3, 4, 1)).reshape(-1, 4 * LANE, PAGE_SIZE)
    kc_s, vc_s = _compress(pool_t, page_table, lw, transposed=True)
    win_buf = cache_nsa_win[l]
    win_t = jnp.transpose(win_buf, (0, 2, 3, 4, 1)).reshape(nb, 2 * LANE, -1)
    o_nsa_s = _nsa_sample(pool_t, page_table, tokens(q_s), tokens(gates_s), kc_s, vc_s, tokens(kv_s),
                          win_t, tokens(win_s), rel_table, nt)
    y_sample = trunk_out(xs1, o_nsa_s[:, :nt].reshape(nb * nt, -1), o_rw_s[:, :nt].reshape(nb * nt, -1), mod_s)
    kv_sample = kv_s.reshape(1, nb, nt, 4, N_KV, HD)
    win_sample = jnp.concatenate([win_buf, win_s.reshape(nb, nt, 2, N_KV, HD)], axis=1)[None, :, nt:]
    shift_sample = _rwkv_uncols(pr_s.reshape(nb, nt, -1)[:, -1])[None]
    return (y_prompt[None], y_sample.reshape(nb, nt, D_MODEL), kv_prompt, win_prompt, shift_prompt, wkv_p[None],
            kv_sample, win_sample, shift_sample, wkv_s[None])
```

```python
import functools
import math

import numpy as np
import jax
import jax.numpy as jnp
from jax import lax
from jax.experimental import pallas as pl
from jax.experimental.pallas import tpu as pltpu

D_MODEL = 1024
PAGE_SIZE = 128
H_NSA = 8
N_KV = 2
G_NSA = H_NSA // N_KV
HD = 64
CMP_STRIDE = 16
CMP_BLOCK = 2 * CMP_STRIDE
CMP_HIDDEN = 256
SEL_BLOCK = 64
N_SEL = 16
WINDOW = 512
Q_BLOCK = 128
N_BUCKETS = 32
MAX_DISTANCE = 128
H_RWKV = 8
HD_RWKV = 64
D_RWKV = H_RWKV * HD_RWKV
DECAY_LORA = 32
AAA_LORA = 32
GATE_LORA = 96
GN_EPS = 64e-5
D_FF = 2816
LN_EPS = 1e-5
DEPTH = 1
ALPHA = (2 * DEPTH) ** 0.25

NSA_SIZES = (H_NSA * HD,) + (N_KV * HD,) * 6 + (H_NSA * 3,)
RWKV_SIZES = (D_RWKV, D_RWKV, D_RWKV, DECAY_LORA, AAA_LORA, GATE_LORA)
NSA_COLS = sum(NSA_SIZES)
RWKV_COLS = sum(RWKV_SIZES)

F32 = jnp.float32
BF16 = jnp.bfloat16
LANE = 128
NEG = -(2.0 ** 100)
M_INIT = -(2.0 ** 103)
VMEM_LIMIT = 56 * 1024 * 1024

RW_PAD = 3 * D_RWKV + 3 * LANE
P_Q, P_KV, P_WIN, P_GATE, P_RW = 0, 512, 1024, 1280, 1408
P_COLS = P_RW + RW_PAD
KEY_TILE = 512
MASK_ROWS = 16
SEL_MASK0 = HD
SEL_FAR0 = SEL_MASK0 + MASK_ROWS
V_ROWS_KV = HD + MASK_ROWS
CMP_CLASS_ROWS = 256
Q_PER_TILE = KEY_TILE // Q_BLOCK
CMP_PER_Q = Q_BLOCK // CMP_STRIDE
EDGE_ROWS = 32


def _bucket_lows():
    d = np.arange(0, 4 * MAX_DISTANCE, dtype=np.int64)
    max_exact = N_BUCKETS // 2
    df = np.maximum(d, 1).astype(np.float32)
    large = max_exact + (np.log(df / np.float32(max_exact)) / np.float32(math.log(MAX_DISTANCE / max_exact))
                         * np.float32(N_BUCKETS - max_exact)).astype(np.int32)
    b = np.where(d < max_exact, d, np.minimum(large, N_BUCKETS - 1))
    lows = [int(np.argmax(b >= k)) for k in range(N_BUCKETS)]
    return b, lows


_BUCKET_OF, _BUCKET_LOW = _bucket_lows()
FAR_DIST = _BUCKET_LOW[N_BUCKETS - 1]


def _resident(shape):
    nd = len(shape)
    return pl.BlockSpec(shape, lambda *_: (0,) * nd, pipeline_mode=pl.Buffered(1))


def _params(sem):
    return pltpu.CompilerParams(dimension_semantics=sem, vmem_limit_bytes=VMEM_LIMIT)


def _dot_exact_rhs(x, rhs_bf16, terms=2):
    acc = None
    rem = x
    for _ in range(terms):
        part = rem.astype(BF16)
        d = jnp.dot(part, rhs_bf16, preferred_element_type=F32)
        acc = d if acc is None else acc + d
        rem = rem - part.astype(F32)
    return acc


def _dot_exact_lhs(lhs_bf16, x, terms=3):
    acc = None
    rem = x
    for _ in range(terms):
        part = rem.astype(BF16)
        d = jnp.dot(lhs_bf16, part, preferred_element_type=F32)
        acc = d if acc is None else acc + d
        rem = rem - part.astype(F32)
    return acc


def _layer_norm(y, g, b):
    mu = jnp.mean(y, axis=-1, keepdims=True)
    yc = y - mu
    var = jnp.mean(yc * yc, axis=-1, keepdims=True)
    return yc * lax.rsqrt(var + LN_EPS) * g + b


def _bias_chain(d, tab_rows):
    out = tab_rows[0] + jnp.zeros(d.shape, F32)
    for b in range(1, N_BUCKETS):
        out = jnp.where(d >= _BUCKET_LOW[b], tab_rows[b], out)
    return out


def _ada_kernel(c_ref, w_ref, b_ref, o_ref):
    c = c_ref[...]
    h = (c * jax.nn.sigmoid(c)).astype(BF16)
    o_ref[...] = jnp.dot(h, w_ref[...].astype(BF16), preferred_element_type=F32) + b_ref[...]


def _ada(c_all, w_ada, b_ada):
    rows, n = c_all.shape[0], w_ada.shape[1]
    tn = 1152
    return pl.pallas_call(
        _ada_kernel,
        grid=(n // tn,),
        in_specs=[pl.BlockSpec((rows, D_MODEL), lambda j: (0, 0)),
                  pl.BlockSpec((D_MODEL, tn), lambda j: (0, j)),
                  pl.BlockSpec((1, tn), lambda j: (0, j))],
        out_specs=pl.BlockSpec((rows, tn), lambda j: (0, j)),
        out_shape=jax.ShapeDtypeStruct((rows, n), F32),
        compiler_params=_params(("arbitrary",)),
        name="ada",
    )(c_all, w_ada, b_ada.reshape(1, n))


FF_CHUNKS = 2


def _ffn_block(x, shift, scale, gate, ln_g, ln_b, wg_ref, wu_ref, wd_ref):
    h = (x * (1.0 + scale) + shift).astype(BF16)
    ck = D_FF // FF_CHUNKS
    acc = jnp.zeros(x.shape, F32)
    for c in range(FF_CHUNKS):
        a = jnp.dot(h, wg_ref[:, c * ck:(c + 1) * ck], preferred_element_type=F32)
        b = jnp.dot(h, wu_ref[:, c * ck:(c + 1) * ck], preferred_element_type=F32)
        t = (a * jax.nn.sigmoid(a) * b).astype(BF16)
        acc = acc + jnp.dot(t, wd_ref[c * ck:(c + 1) * ck, :], preferred_element_type=F32)
    y = ALPHA * x + (1.0 + gate) * (0.5 * acc)
    return _layer_norm(y, ln_g, ln_b)


def _ffn_kernel(x_ref, sh_ref, sc_ref, gt_ref, lng_ref, lnb_ref, wg_ref, wu_ref, wd_ref, o_ref):
    o_ref[...] = _ffn_block(x_ref[...], sh_ref[...], sc_ref[...], gt_ref[...], lng_ref[...], lnb_ref[...], wg_ref, wu_ref, wd_ref)


def _mod_spec(mod, tm):
    if mod.shape[0] == 1:
        return pl.BlockSpec((1, D_MODEL), lambda i: (0, 0))
    return pl.BlockSpec((tm, D_MODEL), lambda i: (i, 0))


def _ffn(x, shift, scale, gate, ln_g, ln_b, wg, wu, wd):
    rows = x.shape[0]
    tm = min(512, rows)
    row = lambda i: (i, 0)
    return pl.pallas_call(
        _ffn_kernel,
        grid=(rows // tm,),
        in_specs=[pl.BlockSpec((tm, D_MODEL), row), _mod_spec(shift, tm), _mod_spec(scale, tm), _mod_spec(gate, tm),
                  _resident((1, D_MODEL)), _resident((1, D_MODEL)),
                  _resident((D_MODEL, D_FF)), _resident((D_MODEL, D_FF)), _resident((D_FF, D_MODEL))],
        out_specs=pl.BlockSpec((tm, D_MODEL), row),
        out_shape=jax.ShapeDtypeStruct((rows, D_MODEL), F32),
        compiler_params=_params(("arbitrary",)),
        name="ffn",
    )(x, shift, scale, gate, ln_g.reshape(1, -1), ln_b.reshape(1, -1), wg, wu, wd)


def _proj_kernel(x_ref, sh_ref, sc_ref, w_ref, q_ref, kv_ref, win_ref, g_ref, pr_ref, *attn_refs):
    h = (x_ref[...] * (1.0 + sc_ref[...]) + sh_ref[...]).astype(BF16)
    p = jnp.dot(h, w_ref[...], preferred_element_type=F32)
    q_ref[...] = p[:, P_Q:P_KV]
    kv_ref[...] = p[:, P_KV:P_WIN]
    win_ref[...] = p[:, P_WIN:P_GATE]
    g_ref[...] = jax.nn.sigmoid(p[:, P_GATE:P_RW])
    pr_ref[...] = p[:, P_RW:P_COLS]
    if attn_refs:
        ks_ref, vst_ref, kw_ref, vwt_ref = attn_refs
        tm = x_ref.shape[0]
        lane = lax.broadcasted_iota(jnp.int32, (1, LANE), 1)
        key = pl.program_id(0) * tm + lax.broadcasted_iota(jnp.int32, (tm, 1), 0)
        slot = lane - HD
        pattern = (((key // SEL_BLOCK) % MASK_ROWS == slot) | ((slot >= MASK_ROWS) & (slot < MASK_ROWS + 2))).astype(F32)
        k_sel = p[:, P_KV + 2 * LANE:P_KV + 3 * LANE]
        ks_ref[0] = jnp.where(lane < HD, k_sel, pattern).astype(BF16)
        ks_ref[1] = jnp.where(lane < HD, pltpu.roll(k_sel, HD, axis=1), pattern).astype(BF16)
        tail = jnp.concatenate([jnp.ones((1, tm), F32), jnp.zeros((MASK_ROWS - 1, tm), F32)], axis=0)
        v_sel_t = p[:, P_KV + 3 * LANE:P_WIN].T
        v_win_t = p[:, P_WIN + LANE:P_GATE].T
        kw_ref[...] = p[:, P_WIN:P_WIN + LANE].astype(BF16)
        for hd in range(N_KV):
            vst_ref[hd, 0] = jnp.concatenate([v_sel_t[hd * HD:(hd + 1) * HD], tail], axis=0).astype(BF16)
            win_rows = jnp.concatenate([v_win_t[hd * HD:(hd + 1) * HD], tail], axis=0).astype(BF16)
            for j in range(tm // Q_BLOCK):
                vwt_ref[hd, j] = win_rows[:, j * Q_BLOCK:(j + 1) * Q_BLOCK]


def _proj(x, shift, scale, w_in_p, attn_operands=False):
    rows = x.shape[0]
    tm = min(KEY_TILE, rows)
    row = lambda i: (i, 0)
    widths = (512, 512, 256, LANE, RW_PAD)
    out_specs = [pl.BlockSpec((tm, w), row) for w in widths]
    out_shape = [jax.ShapeDtypeStruct((rows, w), F32) for w in widths]
    if attn_operands:
        assert tm == KEY_TILE
        per_q = tm // Q_BLOCK
        out_specs += [pl.BlockSpec((N_KV, tm, LANE), lambda i: (0, i, 0)), pl.BlockSpec((N_KV, 1, V_ROWS_KV, tm), lambda i: (0, i, 0, 0)),
                      pl.BlockSpec((tm, LANE), row), pl.BlockSpec((N_KV, per_q, V_ROWS_KV, Q_BLOCK), lambda i: (0, i, 0, 0))]
        out_shape += [jax.ShapeDtypeStruct((N_KV, rows, LANE), BF16), jax.ShapeDtypeStruct((N_KV, rows // tm, V_ROWS_KV, tm), BF16),
                      jax.ShapeDtypeStruct((rows, LANE), BF16), jax.ShapeDtypeStruct((N_KV, rows // Q_BLOCK, V_ROWS_KV, Q_BLOCK), BF16)]
    return pl.pallas_call(
        _proj_kernel,
        grid=(rows // tm,),
        in_specs=[pl.BlockSpec((tm, D_MODEL), row), _mod_spec(shift, tm), _mod_spec(scale, tm),
                  _resident((D_MODEL, P_COLS))],
        out_specs=out_specs,
        out_shape=out_shape,
        compiler_params=_params(("arbitrary",)),
        name="proj",
    )(x, shift, scale, w_in_p)


def _prep_w_in(w_in):
    pad = lambda a, n: jnp.pad(a, ((0, 0), (0, n - a.shape[1])))
    nsa, rw = w_in[:, :NSA_COLS], w_in[:, NSA_COLS:]
    gl = nsa[:, 1280:1304].reshape(D_MODEL, H_NSA, 3).transpose(0, 2, 1).reshape(D_MODEL, 3 * H_NSA)
    cols = [nsa[:, :1280], pad(gl, LANE), _rwkv_cols(rw)]
    return jnp.concatenate(cols, axis=1).astype(BF16)


def _rwkv_cols(a):
    pad = lambda t: jnp.pad(t, [(0, 0)] * (t.ndim - 1) + [(0, LANE - t.shape[-1])])
    n = 3 * D_RWKV
    return jnp.concatenate([a[..., :n], pad(a[..., n:n + 32]), pad(a[..., n + 32:n + 64]), pad(a[..., n + 64:n + 160])], axis=-1)


def _rwkv_uncols(a):
    n = 3 * D_RWKV
    return jnp.concatenate([a[..., :n], a[..., n:n + 32], a[..., n + LANE:n + LANE + 32], a[..., n + 2 * LANE:n + 2 * LANE + 96]], axis=-1)


RW_GROUP = 64
RW_TOK_PAD = 16
RW_PAIRS = H_RWKV // 2
RW_BLOCK = 64
RW_STEP = 256


def _lora(x, w_ref):
    w = w_ref[...]
    w_hi = w.astype(BF16)
    w_lo = (w - w_hi.astype(F32)).astype(BF16)
    return _dot_exact_rhs(x, w_hi) + jnp.dot(x.astype(BF16), w_lo, preferred_element_type=F32)


def _rwkv_kernel(pr_ref, sh0_ref, s0_ref, mu_ref, w0_ref, a0_ref, kk_ref, ka_ref, rk_ref, gw_ref, gb_ref,
                 w2_ref, a2_ref, g2_ref, bo_ref, lgrp_ref, ggrp_ref, o_ref, sout_ref,
                 prev_scr, s_scr, *, n_valid):
    tb = pr_ref.shape[0]
    step = pl.program_id(1)

    @pl.when(step == 0)
    def _():
        prev_scr[...] = sh0_ref[...]
        s_scr[...] = s0_ref[...]

    p = pr_ref[...]
    rows = lax.broadcasted_iota(jnp.int32, (tb, 1), 0)
    prev = jnp.where(rows == 0, prev_scr[...], pltpu.roll(p, 1, axis=0))
    prev_scr[...] = p[tb - 1:tb, :]
    xs = p + (prev - p) * mu_ref[...]
    n = D_RWKV
    r, k, v = xs[:, :n], xs[:, n:2 * n], xs[:, 2 * n:3 * n]

    def head_sums(x):
        return jnp.concatenate([_dot_exact_rhs(x[:, :n // 2], bo_ref[...]), _dot_exact_rhs(x[:, n // 2:], bo_ref[...])], axis=1)

    wl, al, gl = xs[:, 3 * n:3 * n + LANE], xs[:, 3 * n + LANE:3 * n + 2 * LANE], xs[:, 3 * n + 2 * LANE:]
    z = -(w0_ref[...] + _lora(jnp.tanh(wl), w2_ref))
    w = -(jnp.maximum(z, 0.0) + jnp.log(1.0 + jnp.exp(-jnp.abs(z)))) - 0.5
    one_pass = lambda x, w_ref: jnp.dot(x.astype(BF16), w_ref[...].astype(BF16), preferred_element_type=F32)
    a = jax.nn.sigmoid(a0_ref[...] + one_pass(al, a2_ref))
    g = one_pass(jax.nn.sigmoid(gl), g2_ref)
    kk = k * kk_ref[...]
    ss = head_sums(kk * kk)
    kk = kk / jnp.maximum(jnp.sqrt(ss), 1e-12)
    k2 = k * (1.0 + (a - 1.0) * ka_ref[...])
    G = min(RW_GROUP, tb)
    log_dec = -jnp.exp(w)
    bet = kk * a
    if n_valid < tb:
        live = rows < n_valid
        log_dec, kk, bet, k2, v_in = (jnp.where(live, x, 0.0) for x in (log_dec, kk, bet, k2, v))
    else:
        v_in = v
    cum = _dot_exact_lhs(lgrp_ref[...], log_dec, terms=2)
    cum_end = _dot_exact_lhs(ggrp_ref[...], log_dec, terms=2)
    gam_inv = jnp.exp(-cum)
    gam_end = jnp.exp(cum_end - cum)
    k_hat = -kk * jnp.exp(cum - log_dec)
    r_hat = r * jnp.exp(cum)
    b_chk, k_chk = bet * gam_inv, k2 * gam_inv
    b_til, k_til = bet * gam_end, k2 * gam_end
    gam_group = jnp.exp(cum_end)

    lane = lax.broadcasted_iota(jnp.int32, (1, LANE), 1)
    low = lane < HD_RWKV
    lane_t = lane & (RW_BLOCK - 1)
    row = lax.broadcasted_iota(jnp.int32, (LANE, 1), 0)
    row_t = row & (RW_BLOCK - 1)
    same = ((row < HD_RWKV) == low) & ((row_t // G) == (lane_t // G))
    strict, incl = same & (lane_t < row_t), same & (lane_t <= row_t)
    bf = lambda x: x.astype(BF16)
    mm = lambda x, y: jnp.dot(bf(x), bf(y), preferred_element_type=F32)
    mm_nt = lambda x, y: lax.dot_general(bf(x), bf(y), (((1,), (1,)), ((), ())), preferred_element_type=F32)

    def rows_bd(x):
        if tb < RW_BLOCK:
            x = jnp.concatenate([x, jnp.zeros((RW_BLOCK - tb, LANE), F32)], axis=0)
        return jnp.concatenate([jnp.where(low, x, 0.0), jnp.where(low, 0.0, x)], axis=0)

    def mm3(x, y):
        xh, yh = bf(x), bf(y)
        xl, yl = bf(x - xh.astype(F32)), bf(y - yh.astype(F32))
        return jnp.dot(jnp.concatenate([xh, xl, xh], axis=1), jnp.concatenate([yh, yh, yl], axis=0), preferred_element_type=F32)

    units = max(1, tb // RW_BLOCK)
    unit_rows = min(tb, RW_BLOCK)
    pairs = range(RW_PAIRS)
    items = [(un, pp) for un in range(units) for pp in pairs]
    at = lambda x, it: x[it[0] * unit_rows:(it[0] + 1) * unit_rows, it[1] * LANE:(it[1] + 1) * LANE]
    kh_row = [rows_bd(at(k_hat, it)) for it in items]
    rh_row = [rows_bd(at(r_hat, it)) for it in items]
    kh_mat = [x.T for x in kh_row]
    rh_mat = [x.T for x in rh_row]
    state_in = [jnp.concatenate([rows_bd(at(b_chk, it)).T, rows_bd(at(k_chk, it)).T], axis=1) for it in items]
    upd_rows = [jnp.concatenate([rows_bd(at(b_til, it)), rows_bd(at(k_til, it))], axis=0) for it in items]
    v_t = [rows_bd(at(v_in, it)).T for it in items]
    v_t = [x[:HD_RWKV] + x[HD_RWKV:] for x in v_t]
    c_all = [mm(jnp.concatenate([kh_row[n], rh_row[n]], axis=0), state_in[n]) for n in range(len(items))]
    c_uu = [jnp.where(strict, c[:LANE, :LANE], 0.0) for c in c_all]
    c_uv = [jnp.where(strict, c[:LANE, LANE:], 0.0) for c in c_all]
    c_ru = [jnp.where(incl, c[LANE:, :LANE], 0.0) for c in c_all]
    c_rv = [jnp.where(incl, c[LANE:, LANE:], 0.0) for c in c_all]
    t_neu, power = list(c_uu), list(c_uu)
    span = 2
    while span < G:
        power = [mm(x, x) for x in power]
        t_neu = [t_neu[n] + power[n] + mm(t_neu[n], power[n]) for n in range(len(items))]
        span *= 2
    from_v = [mm_nt(v_t[n], c_uv[n]) for n in range(len(items))]
    st = [s_scr[pp] for pp in pairs]
    y_units = []
    for un in range(units):
        ns = [un * RW_PAIRS + pp for pp in pairs]
        y_t = [jnp.zeros((HD_RWKV, LANE), F32) for _ in pairs]
        for grp in range(unit_rows // G):
            here = (lane_t // G) == grp
            first = un * unit_rows + grp * G
            w_t = [jnp.where(here, mm(st[pp], kh_mat[ns[pp]]) + from_v[ns[pp]], 0.0) for pp in pairs]
            u_t = [w_t[pp] + mm_nt(w_t[pp], t_neu[ns[pp]]) for pp in pairs]
            v_g = [jnp.where(here, v_t[ns[pp]], 0.0) for pp in pairs]
            y_t = [y_t[pp] + jnp.where(here, mm(st[pp], rh_mat[ns[pp]]), 0.0) + mm_nt(u_t[pp], c_ru[ns[pp]])
                   + mm_nt(v_g[pp], c_rv[ns[pp]]) for pp in pairs]
            st = [st[pp] * gam_group[first:first + 1, pp * LANE:(pp + 1) * LANE]
                  + mm3(jnp.concatenate([u_t[pp], v_g[pp]], axis=1), upd_rows[ns[pp]]) for pp in pairs]
        lane_u = lax.broadcasted_iota(jnp.int32, (unit_rows, LANE), 1)
        pieces = []
        for pp in pairs:
            yt = jnp.concatenate([y_t[pp], jnp.zeros((LANE - HD_RWKV, LANE), F32)], axis=0).T
            pieces.append(jnp.where(lane_u < HD_RWKV, yt[:unit_rows], pltpu.roll(yt[RW_BLOCK:RW_BLOCK + unit_rows], HD_RWKV, axis=1)))
        y_units.append(jnp.concatenate(pieces, axis=1))
    for pp in pairs:
        s_scr[pp] = st[pp]
    y = y_units[0] if units == 1 else jnp.concatenate(y_units, axis=0)
    mean = head_sums(y) * (1.0 / HD_RWKV)
    yc = y - mean
    var = head_sums(yc * yc) * (1.0 / HD_RWKV)
    yn = yc * lax.rsqrt(var + GN_EPS) * gw_ref[...] + gb_ref[...]
    bonus = head_sums(r * k2 * rk_ref[...]) * v
    o_ref[...] = (yn + bonus) * g
    sout_ref[...] = s_scr[...]


def _pair_state(s):
    B = s.shape[0]
    return s.reshape(B, RW_PAIRS, 2, HD_RWKV, HD_RWKV).transpose(0, 1, 3, 2, 4).reshape(B, RW_PAIRS, HD_RWKV, LANE)


def _unpair_state(s):
    B = s.shape[0]
    return s.reshape(B, RW_PAIRS, HD_RWKV, 2, HD_RWKV).transpose(0, 1, 3, 2, 4).reshape(B, H_RWKV, HD_RWKV, HD_RWKV)


def _rwkv(pr, shift0, s0, lw, n_valid):
    B, T, _ = pr.shape
    tb = min(RW_STEP, T)
    n = D_RWKV
    vec = lambda a: a.reshape(1, n)
    padrow = lambda a: jnp.pad(a, ((0, LANE - a.shape[0]), (0, 0)))
    blk = np.arange(n // 2) // HD_RWKV
    block_ones = jnp.asarray(blk[:, None] == blk[None, :], BF16)
    tok = np.arange(tb)
    group = min(RW_GROUP, tb)
    same_group = tok[:, None] // group == tok[None, :] // group
    prefix = jnp.asarray(same_group & (tok[None, :] <= tok[:, None]), BF16)
    consts = [_rwkv_cols(lw['rwkv_mu']).reshape(1, RW_PAD), vec(lw['rwkv_w0']), vec(lw['rwkv_a0']), vec(lw['rwkv_k_k']),
              vec(lw['rwkv_k_a']), vec(lw['rwkv_r_k']), vec(lw['rwkv_gn_w']), vec(lw['rwkv_gn_b']),
              padrow(lw['rwkv_w2']), padrow(lw['rwkv_a2']), padrow(lw['rwkv_g2']), block_ones, prefix, jnp.asarray(same_group, BF16)]
    kern = functools.partial(_rwkv_kernel, n_valid=n_valid)
    state_spec = pl.BlockSpec((None, RW_PAIRS, HD_RWKV, LANE), lambda b, j: (b, 0, 0, 0))
    o, s = pl.pallas_call(
        kern,
        grid=(B, T // tb),
        in_specs=[pl.BlockSpec((None, tb, RW_PAD), lambda b, j: (b, j, 0)),
                  pl.BlockSpec((None, 1, RW_PAD), lambda b, j: (b, 0, 0)), state_spec]
                 + [_resident(c.shape) for c in consts],
        out_specs=[pl.BlockSpec((None, tb, n), lambda b, j: (b, j, 0)), state_spec],
        out_shape=[jax.ShapeDtypeStruct((B, T, n), F32), jax.ShapeDtypeStruct((B, RW_PAIRS, HD_RWKV, LANE), F32)],
        scratch_shapes=[pltpu.VMEM((1, RW_PAD), F32), pltpu.VMEM((RW_PAIRS, HD_RWKV, LANE), F32)],
        compiler_params=_params(("arbitrary", "arbitrary")),
        name="rwkv",
    )(pr, shift0, _pair_state(s0), *consts)
    return o, _unpair_state(s)


CMP_PAGES = 64
CHUNKS_PER_PAGE = PAGE_SIZE // CMP_STRIDE


def _compress_kernel(pt_ref, *refs, n_pages, transposed):
    pages, nxt = refs[:n_pages], refs[n_pages]
    weights = refs[n_pages + 1:n_pages + 9]
    outs = refs[n_pages + 9:n_pages + 11]
    width = N_KV * HD
    rows = CHUNKS_PER_PAGE * n_pages
    seg = rows + 8
    kinds = range(2)
    low = lax.broadcasted_iota(jnp.int32, (1, N_KV * HD), 1) < HD
    rows_scr = refs[n_pages + 11:n_pages + 13]

    def by_head(row_s):
        heads = [[], []]
        for s in range(0, CMP_STRIDE, 2):
            a, b = row_s(s), row_s(s + 1)
            heads[0].append(jnp.where(low, a, pltpu.roll(b, HD, axis=1)))
            heads[1].append(jnp.where(low, pltpu.roll(a, HD, axis=1), b))
        return [jnp.concatenate(h, axis=1) for h in heads]

    for kind in kinds:
        pe_ref, w_ref, b_ref, w2_ref = weights[4 * kind:4 * kind + 4]
        part = slice(kind * width, (kind + 1) * width)
        for p, pg in enumerate(pages):
            rows_scr[kind][p * PAGE_SIZE:(p + 1) * PAGE_SIZE, :] = pg[part, :].T if transposed else pg[:, part]
        nxt_rows = nxt[part, :].T[:CMP_STRIDE] if transposed else nxt[:, part]
        x = by_head(lambda s: rows_scr[kind][pl.ds(s, rows, stride=CMP_STRIDE), :])
        x_next = by_head(lambda s: jnp.broadcast_to(nxt_rows[s:s + 1, :], (8, N_KV * HD)))
        x_all = jnp.concatenate([x[0], x_next[0], x[1], x_next[1]], axis=0)
        h_first = jnp.dot((x_all + pe_ref[0]).astype(BF16), w_ref[0], preferred_element_type=F32)
        h_second = jnp.dot((x_all + pe_ref[1]).astype(BF16), w_ref[1], preferred_element_type=F32)
        out = None
        for h in range(N_KV):
            h_next = pltpu.roll(h_second[h * seg:(h + 1) * seg], seg - 1, axis=0)[:rows]
            hidden = jax.nn.gelu(h_first[h * seg:h * seg + rows] + h_next + b_ref[...])
            part = jnp.dot(hidden.astype(BF16), w2_ref[h], preferred_element_type=F32)
            out = part if out is None else out + part
        outs[kind][...] = out


def _compress_weights(pe, w1, b1, w2):
    n = CMP_STRIDE * HD
    pe2 = pe.reshape(2, 1, n)
    w_halves = w1.reshape(2, n, CMP_HIDDEN).astype(BF16)
    zero = jnp.zeros_like(w2)
    w2_heads = jnp.stack([jnp.concatenate([w2, zero], axis=1), jnp.concatenate([zero, w2], axis=1)]).astype(BF16)
    return [pe2, w_halves, b1.reshape(1, -1), w2_heads]


def _compress(pool, page_table, lw, transposed):
    B, n_pages_total = page_table.shape
    n_pages = min(CMP_PAGES, n_pages_total)
    rows = CHUNKS_PER_PAGE * n_pages
    weights = (_compress_weights(lw['cmp_pe_k'], lw['cmp_w1_k'], lw['cmp_b1_k'], lw['cmp_w2_k'])
               + _compress_weights(lw['cmp_pe_v'], lw['cmp_w1_v'], lw['cmp_b1_v'], lw['cmp_w2_v']))
    width = N_KV * HD

    def page_map(p):
        return lambda b, j, pt: (pt[b, j * n_pages + p], 0, 0)

    next_map = lambda b, j, pt: (pt[b, jnp.minimum((j + 1) * n_pages, n_pages_total - 1)], 0, 0)
    page_block = (None, 2 * width, PAGE_SIZE) if transposed else (None, PAGE_SIZE, 2 * width)
    next_block = page_block if transposed else (None, CMP_STRIDE, 2 * width)
    const = lambda a: pl.BlockSpec(a.shape, lambda b, j, pt: (0,) * a.ndim)
    out_spec = pl.BlockSpec((None, rows, N_KV * HD), lambda b, j, pt: (b, j, 0))
    out_shape = jax.ShapeDtypeStruct((B, n_pages_total * CHUNKS_PER_PAGE, N_KV * HD), F32)
    grid_spec = pltpu.PrefetchScalarGridSpec(
        num_scalar_prefetch=1,
        grid=(B, n_pages_total // n_pages),
        in_specs=[pl.BlockSpec(page_block, page_map(p)) for p in range(n_pages)]
                 + [pl.BlockSpec(next_block, next_map)] + [const(a) for a in weights],
        out_specs=[out_spec, out_spec],
        scratch_shapes=[pltpu.VMEM((n_pages * PAGE_SIZE, width), F32)] * 2,
    )
    return pl.pallas_call(
        functools.partial(_compress_kernel, n_pages=n_pages, transposed=transposed),
        grid_spec=grid_spec,
        out_shape=[out_shape, out_shape],
        compiler_params=_params(("arbitrary", "arbitrary")),
        name="compress",
    )(page_table, *([pool] * (n_pages + 1)), *weights)


BAND_ROWS = 1152


def _band_kernel(tab_ref, bkt_ref, o_ref):
    h = pl.program_id(0)
    bkt = bkt_ref[...]
    out = jnp.full(bkt.shape, NEG, F32)
    for b in range(N_BUCKETS):
        out = jnp.where(bkt == b, tab_ref[b, h], out)
    o_ref[...] = out


def _band(rel_table):
    u = np.arange(BAND_ROWS)[:, None]
    qi = np.arange(Q_BLOCK)[None, :]
    d = qi + WINDOW - u
    bkt = np.where(d >= 0, _BUCKET_OF[np.clip(d, 0, len(_BUCKET_OF) - 1)], -1).astype(np.int32)
    return pl.pallas_call(
        _band_kernel,
        grid=(H_NSA,),
        in_specs=[pl.BlockSpec(memory_space=pltpu.SMEM), pl.BlockSpec((BAND_ROWS, Q_BLOCK), lambda h: (0, 0))],
        out_specs=pl.BlockSpec((None, BAND_ROWS, Q_BLOCK), lambda h: (h, 0, 0)),
        out_shape=jax.ShapeDtypeStruct((H_NSA, BAND_ROWS, Q_BLOCK), F32),
        compiler_params=_params(("arbitrary",)),
        name="band",
    )(rel_table, jnp.asarray(bkt))


def _softmax_cols(s):
    m = jnp.max(s, axis=0, keepdims=True)
    e = jnp.exp(s - m)
    l = jnp.sum(e, axis=0, keepdims=True)
    return e * jnp.where(m > 0.5 * NEG, 1.0 / l, 0.0)


def _select_blocks(impsel, qpos, n_pick):
    ns = impsel.shape[0]
    blk = lax.broadcasted_iota(jnp.int32, impsel.shape, 0)
    cur = jnp.right_shift(qpos, SEL_BLOCK.bit_length() - 1)
    future = blk * SEL_BLOCK > qpos
    forced = (blk == 0) | (blk == cur) | (blk == cur - 1)
    score = jnp.where(future, -jnp.inf, jnp.where(forced, jnp.inf, impsel))
    chosen = jnp.zeros(impsel.shape, F32)
    for _ in range(n_pick):
        best = jnp.max(score, axis=0, keepdims=True)
        first = jnp.min(jnp.where(score == best, blk, ns), axis=0, keepdims=True)
        hit = (blk == first) & (best > -jnp.inf)
        chosen = jnp.where(hit, 1.0, chosen)
        score = jnp.where(hit, -jnp.inf, score)
    return jnp.where(chosen > 0.0, 0.0, NEG)


def _pool_matrix(ns, nc):
    j = np.arange(ns)[:, None]
    n = np.arange(nc)[None, :]
    ratio = SEL_BLOCK // CMP_STRIDE
    return jnp.asarray((n >= ratio * j - 1) & (n <= ratio * j + ratio - 1), BF16)


def _nsa_prompt_kernel(tab_ref, q_ref, g_ref, kc_ref, vct_ref, ks_ref, vst_ref, kw_ref, vwt_ref, band_ref, pool_ref, o_ref,
                       rhs_scr, mask_scr, acc_scr, m_scr, sc_scr, sa_scr, sb_scr, oc_scr):
    i = pl.program_id(0)
    ncp = kc_ref.shape[0]
    ns = pool_ref.shape[0]
    s0 = i * Q_BLOCK
    q_t = (q_ref[...] * HD ** -0.5).T
    g_t = g_ref[...].T
    lane_q = lax.broadcasted_iota(jnp.int32, (1, G_NSA * Q_BLOCK), 1) & (Q_BLOCK - 1)
    qpos = s0 + lax.broadcasted_iota(jnp.int32, (1, Q_BLOCK), 1)
    rhs_scr[...] = jnp.zeros(rhs_scr.shape, BF16)
    kvs = range(N_KV)
    kd = i // Q_PER_TILE
    r = i % Q_PER_TILE
    lanes4 = lambda k, f: jnp.concatenate([f(G_NSA * k + g) for g in range(G_NSA)], axis=1)
    qcols = [lanes4(k, lambda h: q_t[h * HD:(h + 1) * HD, :]) for k in kvs]
    zero = jnp.zeros_like(qcols[0])
    top = [jnp.concatenate([qcols[0], zero], axis=0).astype(BF16), jnp.concatenate([zero, qcols[1]], axis=0).astype(BF16)]
    far_row = [lanes4(k, lambda h: band_ref[h, 0:1, :]) for k in kvs]


    n0 = pl.multiple_of(jnp.clip(CMP_PER_Q * i - EDGE_ROWS // 2, 0, ncp - EDGE_ROWS), 8)
    nrow = lax.broadcasted_iota(jnp.int32, (ncp, 1), 0)
    d_edge = qpos - (CMP_STRIDE * (n0 + lax.broadcasted_iota(jnp.int32, (EDGE_ROWS, 1), 0)) + CMP_BLOCK - 1)

    def compressed(k, rows):
        sc_scr[k, 0:rows, :] = (jnp.dot(kc_ref[0:rows, :], top[k], preferred_element_type=F32)
                                + jnp.where(nrow[0:rows] < n0, far_row[k], NEG))
        edge_bias = lanes4(k, lambda h: jnp.where(d_edge >= 0, _bias_chain(d_edge, [tab_ref[b, h] for b in range(N_BUCKETS)]), NEG))
        sc_scr[k, pl.ds(n0, EDGE_ROWS), :] = jnp.dot(kc_ref[pl.ds(n0, EDGE_ROWS), :], top[k], preferred_element_type=F32) + edge_bias
        p_c = _softmax_cols(sc_scr[k, 0:rows, :])
        o_c = jnp.dot(vct_ref[:, 0:rows], p_c.astype(BF16), preferred_element_type=F32)[k * HD:(k + 1) * HD]
        imp = p_c[:, 0:Q_BLOCK]
        for g in range(1, G_NSA):
            imp = imp + p_c[:, g * Q_BLOCK:(g + 1) * Q_BLOCK]
        n_blk = rows * CMP_STRIDE // SEL_BLOCK
        return o_c, _dot_exact_lhs(pool_ref[0:n_blk, 0:rows], imp, terms=2)

    size_step = min(CMP_CLASS_ROWS, ncp)
    size_class = (n0 + EDGE_ROWS - 1) // size_step
    for cls in range(ncp // size_step):
        @pl.when(size_class == cls)
        def _(rows=(cls + 1) * size_step):
            comp = [compressed(k, rows) for k in kvs]
            n_blk = comp[0][1].shape[0]
            masks = [_select_blocks(comp[k][1], qpos, min(N_SEL, ns)) for k in kvs]
            for k in kvs:
                oc_scr[k] = comp[k][0]
                full = jnp.concatenate([masks[k], jnp.full((ns - n_blk, Q_BLOCK), NEG, F32)], axis=0) if n_blk < ns else masks[k]
                mask_scr[k] = jnp.concatenate([full.astype(BF16)] * G_NSA, axis=1)

    ws = pl.multiple_of(jnp.maximum(s0 - WINDOW, 0), Q_BLOCK)
    u0 = pl.multiple_of(WINDOW - (s0 - ws), Q_BLOCK)
    n_win = WINDOW + Q_BLOCK
    u = u0 + lax.broadcasted_iota(jnp.int32, (n_win, 1), 0)
    win_mask = jnp.where(u > lane_q, 0.0, NEG)

    def window(k):
        s_w = (jnp.dot(kw_ref[pl.ds(ws, n_win), :], top[k], preferred_element_type=F32)
               + lanes4(k, lambda h: band_ref[h, pl.ds(u0, n_win), :]) + win_mask)
        m_w = jnp.max(s_w, axis=0, keepdims=True)
        p_w = jnp.exp(s_w - m_w).astype(BF16)
        acc_w = jnp.zeros((V_ROWS_KV, G_NSA * Q_BLOCK), F32)
        for j in range(n_win // Q_BLOCK):
            acc_w = acc_w + jnp.dot(vwt_ref[k, ws // Q_BLOCK + j], p_w[j * Q_BLOCK:(j + 1) * Q_BLOCK], preferred_element_type=F32)
        return acc_w[0:HD] / acc_w[HD:HD + 1]

    o_w = [window(k) for k in kvs]

    for k in kvs:
        m_scr[k] = jnp.full(m_scr.shape[1:], M_INIT, F32)
        acc_scr[k] = jnp.zeros(acc_scr.shape[1:], F32)
        far_hi = far_row[k].astype(BF16).astype(F32)
        rhs_scr[k, 0:HD, :] = qcols[k].astype(BF16)
        rhs_scr[k, SEL_FAR0:SEL_FAR0 + MASK_ROWS, :] = jnp.concatenate(
            [far_hi, far_row[k] - far_hi, jnp.zeros((MASK_ROWS - 2, G_NSA * Q_BLOCK), F32)], axis=0).astype(BF16)

    def scores(k, slab, kts, extra):
        rhs_scr[k, SEL_MASK0:SEL_MASK0 + MASK_ROWS, :] = mask_scr[k, pl.ds(pl.multiple_of(slab * MASK_ROWS, MASK_ROWS), MASK_ROWS), :]
        rhs = rhs_scr[k]
        out = []
        for kt, add in zip(kts, extra):
            s = jnp.dot(ks_ref[k, pl.ds(pl.multiple_of(kt * KEY_TILE, KEY_TILE), KEY_TILE), :], rhs, preferred_element_type=F32)
            out.append(s if add is None else s + add)
        return out

    def update(k, kts, tiles):
        m_old = m_scr[k]
        m_new = m_old
        for s in tiles:
            m_new = jnp.maximum(m_new, jnp.max(s, axis=0, keepdims=True))
        acc = jnp.exp(m_old - m_new) * acc_scr[k]
        for kt, s in zip(kts, tiles):
            acc = acc + jnp.dot(vst_ref[k, kt], jnp.exp(s - m_new).astype(BF16), preferred_element_type=F32)
        acc_scr[k] = acc
        m_scr[k] = m_new

    def attend(slab, kts, extra):
        tiles = [scores(k, slab, kts, extra(k)) for k in kvs]
        for k in kvs:
            update(k, kts, tiles[k])

    near_at = lambda k, start: lanes4(k, lambda h: band_ref[h, pl.ds(pl.multiple_of(start, Q_BLOCK), KEY_TILE), :]) - far_row[k]
    prev_near = (r == 0) & (kd >= 1)
    kd_odd = (kd & 1) == 1
    even_prev = jnp.logical_not(kd_odd) & prev_near
    n_pairs = kd // 2 - even_prev.astype(jnp.int32)
    n_quads = n_pairs // 2
    no_bias = lambda k: [None, None]

    def pair_scores(k, dst, pair):
        lo, hi = scores(k, pair, [2 * pair, 2 * pair + 1], [None, None])
        dst[k, 0:KEY_TILE, :] = lo
        dst[k, KEY_TILE:, :] = hi

    def pair_update(k, src, pair):
        update(k, [2 * pair, 2 * pair + 1], [src[k, 0:KEY_TILE, :], src[k, KEY_TILE:, :]])

    @pl.when(n_quads > 0)
    def _():
        for k in kvs:
            pair_scores(k, sa_scr, 0)

    for k in kvs:
        def quad_body(qd, carry, k=k):
            first = 2 * qd
            pair_scores(k, sb_scr, first + 1)
            pair_update(k, sa_scr, first)
            pair_scores(k, sa_scr, jnp.minimum(first + 2, 2 * n_quads - 2))
            pair_update(k, sb_scr, first + 1)
            return carry

        lax.fori_loop(0, n_quads, quad_body, 0)

    @pl.when((n_pairs & 1) == 1)
    def _():
        attend(n_pairs - 1, [2 * n_pairs - 2, 2 * n_pairs - 1], no_bias)

    @pl.when(kd_odd)
    def _():
        attend(kd // 2, [kd - 1, kd], lambda k: [jnp.where(prev_near, near_at(k, 0), 0.0), near_at(k, KEY_TILE - Q_BLOCK * r)])

    @pl.when(even_prev)
    def _():
        attend(kd // 2 - 1, [kd - 2, kd - 1], lambda k: [None, near_at(k, 0)])

    @pl.when(jnp.logical_not(kd_odd))
    def _():
        attend(kd // 2, [kd], lambda k: [near_at(k, KEY_TILE - Q_BLOCK * r)])

    heads_out = []
    for k in kvs:
        acc = acc_scr[k]
        o_s = acc[0:HD] / acc[HD:HD + 1]
        o_c = oc_scr[k]
        for g in range(G_NSA):
            h = G_NSA * k + g
            cols = slice(g * Q_BLOCK, (g + 1) * Q_BLOCK)
            heads_out.append(o_c[:, cols] * g_t[h:h + 1] + o_s[:, cols] * g_t[H_NSA + h:H_NSA + h + 1]
                             + o_w[k][:, cols] * g_t[2 * H_NSA + h:2 * H_NSA + h + 1])
    o_ref[...] = jnp.concatenate(heads_out, axis=0).T


def _sel_pattern(rows, width):
    key = np.arange(rows)[:, None]
    b = np.arange(width)[None, :]
    ones = (b >= MASK_ROWS) & (b < MASK_ROWS + 2)
    return jnp.asarray(((key // SEL_BLOCK) % MASK_ROWS == b) | ones, BF16)


def _nsa_prompt(q, gates, attn, kc, vc, band, rel_table):
    T = q.shape[0]
    ncp, ns = kc.shape[0], T // SEL_BLOCK
    width = G_NSA * Q_BLOCK
    ks_aug, vs_t, kw, vw_t = attn
    operands = [q, gates, kc.astype(BF16), vc.T.astype(BF16), ks_aug, vs_t, kw, vw_t, band, _pool_matrix(ns, ncp)]
    blk = lambda w: pl.BlockSpec((Q_BLOCK, w), lambda i: (i, 0))
    return pl.pallas_call(
        _nsa_prompt_kernel,
        grid=(T // Q_BLOCK,),
        in_specs=[pl.BlockSpec(memory_space=pltpu.SMEM), blk(H_NSA * HD), blk(LANE)] + [_resident(a.shape) for a in operands[2:]],
        out_specs=blk(H_NSA * HD),
        out_shape=jax.ShapeDtypeStruct((T, H_NSA * HD), F32),
        scratch_shapes=[pltpu.VMEM((N_KV, LANE, width), BF16), pltpu.VMEM((N_KV, ns, width), BF16),
                        pltpu.VMEM((N_KV, V_ROWS_KV, width), F32), pltpu.VMEM((N_KV, 1, width), F32),
                        pltpu.VMEM((N_KV, ncp, width), F32)] + [pltpu.VMEM((N_KV, 2 * KEY_TILE, width), F32)] * 2
                       + [pltpu.VMEM((N_KV, HD, width), F32)],
        compiler_params=_params(("arbitrary",)),
        name="nsa_prompt",
    )(rel_table, *operands)


SLAB_PAGES = 8
SMP_PAGES = 64
TOK_PAD = 8
SMP_COLS = H_NSA * TOK_PAD


def _nsa_sample_kernel(pt_ref, *refs, n_pages, n_valid, past):
    pages = refs[:n_pages]
    (q_ref, g_ref, kc_ref, vc_ref, kvn_ref, win_ref, winn_ref, tab_ref, pool_ref, gsum_ref, epat_ref, o_ref,
     mask_scr, acc_scr, m_scr, l_scr, oc_scr, ow_scr, qrow_scr, maskt_scr, farc_scr, nearbt_scr, m2_scr, l2_scr, acc2_scr) = refs[n_pages:]
    j = pl.program_id(1)
    ncp, wbuf = kc_ref.shape[0], win_ref.shape[1]
    lane = lax.broadcasted_iota(jnp.int32, (1, LANE), 1)
    tok = lane & (TOK_PAD - 1)
    second_kv = lane >= G_NSA * TOK_PAD
    tab = [tab_ref[b:b + 1, :] for b in range(N_BUCKETS)]
    far_row = tab[N_BUCKETS - 1]
    own_rows = lambda x: jnp.where(second_kv, x[HD:2 * HD], x[0:HD])
    pad_rows = lambda x: jnp.concatenate([x, jnp.zeros((LANE - x.shape[0], x.shape[1]), x.dtype)], axis=0)
    trow = lax.broadcasted_iota(jnp.int32, (LANE, 1), 0)
    d_new = tok - trow
    new_bias = jnp.where((d_new >= 0) & (trow < n_valid), _bias_chain(jnp.maximum(d_new, 0), tab), NEG)

    def attend_update(s, values_t):
        m_old = m_scr[...]
        m_new = jnp.maximum(m_old, jnp.max(s, axis=0, keepdims=True))
        alpha = jnp.exp(m_old - m_new)
        p = jnp.exp(s - m_new)
        l_scr[...] = alpha * l_scr[...] + jnp.sum(p, axis=0, keepdims=True)
        acc_scr[...] = alpha * acc_scr[...] + jnp.dot(values_t.astype(BF16), p.astype(BF16), preferred_element_type=F32)
        m_scr[...] = m_new

    def reset():
        m_scr[...] = jnp.full(m_scr.shape, M_INIT, F32)
        l_scr[...] = jnp.zeros(l_scr.shape, F32)
        acc_scr[...] = jnp.zeros(acc_scr.shape, F32)

    @pl.when(j == 0)
    def _():
        q_t = pad_rows(q_ref[...] * HD ** -0.5).T
        halves = []
        for k in range(N_KV):
            part = jnp.zeros((HD, LANE), F32)
            for g in range(G_NSA):
                h = G_NSA * k + g
                piece = q_t[h * HD:(h + 1) * HD, :]
                part = part + (pltpu.roll(piece, TOK_PAD * h, axis=1) if h else piece)
            halves.append(part)
        top_f = jnp.concatenate(halves, axis=0)
        top = top_f.astype(BF16)
        qrow_scr[...] = top_f.T[:SMP_COLS].astype(BF16)
        qpos = past + tok

        n0 = ncp - EDGE_ROWS
        kcb = kc_ref[...].astype(BF16)
        d_edge = qpos - (CMP_STRIDE * (n0 + lax.broadcasted_iota(jnp.int32, (EDGE_ROWS, 1), 0)) + CMP_BLOCK - 1)
        s_c = jnp.concatenate([
            jnp.dot(kcb[:n0], top, preferred_element_type=F32) + far_row,
            jnp.dot(kcb[n0:], top, preferred_element_type=F32) + jnp.where(d_edge >= 0, _bias_chain(jnp.maximum(d_edge, 0), tab), NEG)], axis=0)
        p_c = _softmax_cols(s_c)
        oc_scr[...] = own_rows(jnp.dot(vc_ref[...].T.astype(BF16), p_c.astype(BF16), preferred_element_type=F32))
        imp = _dot_exact_rhs(p_c, gsum_ref[...], terms=3)
        mask_scr[...] = _select_blocks(_dot_exact_lhs(pool_ref[...], imp), qpos, N_SEL)

        wk = win_ref[0:LANE, :].T
        d_w = wbuf + tok - lax.broadcasted_iota(jnp.int32, (wbuf, 1), 0)
        near = wbuf - LANE
        s_w = jnp.dot(wk.astype(BF16), top, preferred_element_type=F32)
        s_w = (jnp.concatenate([s_w[:near] + far_row, s_w[near:] + _bias_chain(d_w[near:], tab)], axis=0)
               + jnp.where(d_w < WINDOW, 0.0, NEG))
        reset()
        attend_update(s_w, win_ref[LANE:, :])
        wn = pad_rows(winn_ref[...])
        attend_update(jnp.dot(wn[:, :LANE].astype(BF16), top, preferred_element_type=F32) + new_bias, wn[:, LANE:].T)
        ow_scr[...] = own_rows(acc_scr[...]) / l_scr[...]
        m2_scr[...] = jnp.full(m2_scr.shape, M_INIT, F32)
        l2_scr[...] = jnp.zeros(l2_scr.shape, F32)
        acc2_scr[...] = jnp.zeros(acc2_scr.shape, F32)
        for sl in range(maskt_scr.shape[0]):
            maskt_scr[sl] = pad_rows(mask_scr[sl * MASK_ROWS:(sl + 1) * MASK_ROWS, :]).T[:SMP_COLS].astype(BF16)
        farc_scr[...] = pad_rows(tab_ref[...]).T[:SMP_COLS, N_BUCKETS - 1:N_BUCKETS]
        nearbt_scr[...] = _bias_chain(LANE + tok - lax.broadcasted_iota(jnp.int32, (LANE, 1), 0), tab).T[:SMP_COLS]

    def rows_update(s, values_t):
        m_old = m2_scr[...]
        m_new = jnp.maximum(m_old, jnp.max(s, axis=1, keepdims=True))
        alpha = jnp.exp(m_old - m_new)
        p = jnp.exp(s - m_new)
        l2_scr[...] = alpha * l2_scr[...] + jnp.sum(p, axis=1, keepdims=True)
        acc2_scr[...] = alpha * acc2_scr[...] + lax.dot_general(p.astype(BF16), values_t.astype(BF16), (((1,), (1,)), ((), ())),
                                                                preferred_element_type=F32)
        m2_scr[...] = m_new

    n_slabs = n_pages // SLAB_PAGES
    slab_keys = SLAB_PAGES * PAGE_SIZE
    far_col = farc_scr[...]
    for sub in range(n_slabs):
        tile_pages = pages[sub * SLAB_PAGES:(sub + 1) * SLAB_PAGES]
        kt_tile = jnp.concatenate([pg[0:LANE, :] for pg in tile_pages], axis=1)
        vt_tile = jnp.concatenate([pg[LANE:, :] for pg in tile_pages], axis=1)
        lhs = jnp.concatenate([qrow_scr[...], maskt_scr[j * n_slabs + sub]], axis=1)
        s = jnp.dot(lhs, jnp.concatenate([kt_tile.astype(BF16), epat_ref[...]], axis=0), preferred_element_type=F32)
        if sub < n_slabs - 1:
            s = s + far_col
        else:
            near = slab_keys - LANE
            s = jnp.concatenate([s[:, :near] + far_col, s[:, near:] + jnp.where(j == pl.num_programs(1) - 1, nearbt_scr[...], far_col)], axis=1)
        rows_update(s, vt_tile)

    @pl.when(j == pl.num_programs(1) - 1)
    def _():
        kn = pad_rows(kvn_ref[...])
        last_blk = past // SEL_BLOCK
        new_mask = maskt_scr[last_blk // MASK_ROWS][:, last_blk % MASK_ROWS:last_blk % MASK_ROWS + 1].astype(F32)
        s_n = (jnp.dot(qrow_scr[...], kn[:, 2 * LANE:3 * LANE].T.astype(BF16), preferred_element_type=F32)
               + new_bias.T[:SMP_COLS] + new_mask)
        rows_update(s_n, kn[:, 3 * LANE:].T)
        acc_t = pad_rows(acc2_scr[...]).T
        l_t = pad_rows(jnp.broadcast_to(l2_scr[...], (SMP_COLS, LANE))).T[0:1]
        o_s = own_rows(acc_t) / jnp.where(lane < SMP_COLS, l_t, 1.0)
        g_t = pad_rows(g_ref[...]).T
        gate_rows = []
        for b in range(3):
            row = g_t[b * H_NSA:b * H_NSA + 1]
            for h in range(1, H_NSA):
                row = row + pltpu.roll(g_t[b * H_NSA + h:b * H_NSA + h + 1], TOK_PAD * h, axis=1)
            gate_rows.append(row)
        o_col = oc_scr[...] * gate_rows[0] + o_s * gate_rows[1] + ow_scr[...] * gate_rows[2]
        per_head = [o_col if h == 0 else pltpu.roll(o_col, LANE - TOK_PAD * h, axis=1) for h in range(H_NSA)]
        o_ref[...] = jnp.concatenate(per_head, axis=0).T[:TOK_PAD]


def _nsa_sample(pool, page_table, q, gates, kc, vc, kv_new, win_buf, win_new, rel_table, n_valid):
    B, n_pages_total = page_table.shape
    past = n_pages_total * PAGE_SIZE
    ncp = kc.shape[1]
    ns = past // SEL_BLOCK + 1
    nsp = -(-ns // MASK_ROWS) * MASK_ROWS
    col = np.arange(LANE)
    used = col < SMP_COLS
    gsum = jnp.asarray(((col[:, None] // (G_NSA * TOK_PAD) == col[None, :] // (G_NSA * TOK_PAD))
                        & (col[:, None] % TOK_PAD == col[None, :] % TOK_PAD) & used[:, None] & used[None, :]), BF16)
    tab_cols = jnp.pad(jnp.repeat(rel_table, TOK_PAD, axis=1), ((0, 0), (0, LANE - SMP_COLS)))
    n_step = min(SMP_PAGES, n_pages_total)
    consts = [tab_cols, _pool_matrix(nsp, ncp), gsum, _sel_pattern(SLAB_PAGES * PAGE_SIZE, LANE).T]
    per_seq = [q, gates, kc, vc, kv_new, win_buf, win_new]
    seq_spec = lambda a: pl.BlockSpec((None,) + a.shape[1:], lambda b, j, pt: (b,) + (0,) * (a.ndim - 1))
    const = lambda a: pl.BlockSpec(a.shape, lambda b, j, pt: (0,) * a.ndim)

    def page_map(p):
        return lambda b, j, pt: (pt[b, j * n_step + p], 1, 0)

    grid_spec = pltpu.PrefetchScalarGridSpec(
        num_scalar_prefetch=1,
        grid=(B, n_pages_total // n_step),
        in_specs=[pl.BlockSpec((None, 2 * LANE, PAGE_SIZE), page_map(p)) for p in range(n_step)]
                 + [seq_spec(a) for a in per_seq] + [const(a) for a in consts],
        out_specs=pl.BlockSpec((None, TOK_PAD, H_NSA * HD), lambda b, j, pt: (b, 0, 0)),
        scratch_shapes=[pltpu.VMEM((nsp, LANE), F32), pltpu.VMEM((LANE, LANE), F32),
                        pltpu.VMEM((1, LANE), F32), pltpu.VMEM((1, LANE), F32), pltpu.VMEM((HD, LANE), F32), pltpu.VMEM((HD, LANE), F32),
                        pltpu.VMEM((SMP_COLS, LANE), BF16), pltpu.VMEM((nsp // MASK_ROWS, SMP_COLS, LANE), BF16),
                        pltpu.VMEM((SMP_COLS, 1), F32), pltpu.VMEM((SMP_COLS, LANE), F32),
                        pltpu.VMEM((SMP_COLS, 1), F32), pltpu.VMEM((SMP_COLS, 1), F32), pltpu.VMEM((SMP_COLS, LANE), F32)],
    )
    return pl.pallas_call(
        functools.partial(_nsa_sample_kernel, n_pages=n_step, n_valid=n_valid, past=past),
        grid_spec=grid_spec,
        out_shape=jax.ShapeDtypeStruct((B, TOK_PAD, H_NSA * HD), F32),
        compiler_params=_params(("arbitrary", "arbitrary")),
        name="nsa_sample",
    )(page_table, *([pool] * n_step), *per_seq, *consts)


def _out_ffn_kernel(x_ref, nsa_ref, rw_ref, g1_ref, lng1_ref, lnb1_ref, w_ref,
                    sh_ref, sc_ref, g2_ref, lng2_ref, lnb2_ref, wg_ref, wu_ref, wd_ref, o_ref):
    half = H_NSA * HD
    out = (jnp.dot(nsa_ref[...].astype(BF16), w_ref[0:half, :], preferred_element_type=F32)
           + jnp.dot(rw_ref[...].astype(BF16), w_ref[half:, :], preferred_element_type=F32))
    x2 = _layer_norm(ALPHA * x_ref[...] + (1.0 + g1_ref[...]) * out, lng1_ref[...], lnb1_ref[...])
    o_ref[...] = _ffn_block(x2, sh_ref[...], sc_ref[...], g2_ref[...], lng2_ref[...], lnb2_ref[...], wg_ref, wu_ref, wd_ref)


def _out_ffn(x, o_nsa, o_rwkv, gate1, ln_g1, ln_b1, w_out, shift, scale, gate2, ln_g2, ln_b2, wg, wu, wd):
    rows = x.shape[0]
    tm = min(512, rows)
    row = lambda i: (i, 0)
    vec = lambda: _resident((1, D_MODEL))
    return pl.pallas_call(
        _out_ffn_kernel,
        grid=(rows // tm,),
        in_specs=[pl.BlockSpec((tm, D_MODEL), row), pl.BlockSpec((tm, H_NSA * HD), row), pl.BlockSpec((tm, D_RWKV), row),
                  _mod_spec(gate1, tm), vec(), vec(), _resident(w_out.shape),
                  _mod_spec(shift, tm), _mod_spec(scale, tm), _mod_spec(gate2, tm), vec(), vec(),
                  _resident((D_MODEL, D_FF)), _resident((D_MODEL, D_FF)), _resident((D_FF, D_MODEL))],
        out_specs=pl.BlockSpec((tm, D_MODEL), row),
        out_shape=jax.ShapeDtypeStruct((rows, D_MODEL), F32),
        compiler_params=_params(("arbitrary",)),
        name="out_ffn",
    )(x, o_nsa, o_rwkv, gate1, ln_g1.reshape(1, -1), ln_b1.reshape(1, -1), w_out,
      shift, scale, gate2, ln_g2.reshape(1, -1), ln_b2.reshape(1, -1), wg, wu, wd)


def kernel(x_prompt, x_sample, cache_nsa_kv, cache_nsa_win, state_rwkv_shift, state_rwkv_wkv, page_table, c_prompt, c_sample, rel_table, w_ada, b_ada, ln_g, ln_b, ffn1_gate, ffn1_up, ffn1_down, ffn2_gate, ffn2_up, ffn2_down, w_in, w_out, cmp_pe_k, cmp_w1_k, cmp_b1_k, cmp_w2_k, cmp_pe_v, cmp_w1_v, cmp_b1_v, cmp_w2_v, rwkv_mu, rwkv_w0, rwkv_w2, rwkv_a0, rwkv_a2, rwkv_g2, rwkv_k_k, rwkv_k_a, rwkv_r_k, rwkv_gn_w, rwkv_gn_b):
    assert w_ada.shape[0] == DEPTH == 1 and x_prompt.shape[0] == 1
    l = 0
    lw = dict(cmp_pe_k=cmp_pe_k[l], cmp_w1_k=cmp_w1_k[l], cmp_b1_k=cmp_b1_k[l], cmp_w2_k=cmp_w2_k[l],
              cmp_pe_v=cmp_pe_v[l], cmp_w1_v=cmp_w1_v[l], cmp_b1_v=cmp_b1_v[l], cmp_w2_v=cmp_w2_v[l],
              rwkv_mu=rwkv_mu[l], rwkv_w0=rwkv_w0[l], rwkv_w2=rwkv_w2[l], rwkv_a0=rwkv_a0[l], rwkv_a2=rwkv_a2[l], rwkv_g2=rwkv_g2[l],
              rwkv_k_k=rwkv_k_k[l], rwkv_k_a=rwkv_k_a[l], rwkv_r_k=rwkv_r_k[l], rwkv_gn_w=rwkv_gn_w[l], rwkv_gn_b=rwkv_gn_b[l])
    T = x_prompt.shape[1]
    nb, nt = x_sample.shape[0], x_sample.shape[1]
    assert nt <= TOK_PAD
    n_seq = 1 + nb
    c_all = jnp.concatenate([c_prompt, c_sample, jnp.zeros((-n_seq % 8, D_MODEL), F32)], axis=0)
    mod = _ada(c_all, w_ada[l], b_ada[l])
    mod_p = mod[0:1].reshape(9, 1, D_MODEL)
    mod_s = jnp.repeat(mod[1:n_seq].reshape(nb, 9, D_MODEL), nt, axis=0).transpose(1, 0, 2)
    ffn1 = [w[l].astype(BF16) for w in (ffn1_gate, ffn1_up, ffn1_down)]
    ffn2 = [w[l].astype(BF16) for w in (ffn2_gate, ffn2_up, ffn2_down)]
    w_in_p = _prep_w_in(w_in[l])
    w_out_b = w_out[l].astype(BF16)

    def trunk_in(x, m, attn_operands=False):
        x1 = _ffn(x, m[0], m[1], m[2], ln_g[l, 0], ln_b[l, 0], *ffn1)
        return x1, _proj(x1, m[3], m[4], w_in_p, attn_operands)

    def trunk_out(x1, o_nsa, o_rwkv, m):
        return _out_ffn(x1, o_nsa, o_rwkv, m[5], ln_g[l, 1], ln_b[l, 1], w_out_b, m[6], m[7], m[8], ln_g[l, 2], ln_b[l, 2], *ffn2)

    xp1, (q, kv, win, gates, pr, *attn) = trunk_in(x_prompt[0], mod_p, attn_operands=True)
    o_rw, wkv_p = _rwkv(pr[None], jnp.zeros((1, 1, RW_PAD), F32), jnp.zeros((1, H_RWKV, HD_RWKV, HD_RWKV), F32), lw, min(RW_STEP, T))
    n_rows = T // PAGE_SIZE
    kc, vc = _compress(kv.reshape(n_rows, PAGE_SIZE, 4 * LANE), jnp.arange(n_rows, dtype=jnp.int32)[None], lw, transposed=False)
    o_nsa = _nsa_prompt(q, gates, attn, kc[0], vc[0], _band(rel_table), rel_table)
    y_prompt = trunk_out(xp1, o_nsa, o_rw[0], mod_p)
    kv_prompt = kv.reshape(1, 1, T, 4, N_KV, HD)
    win_prompt = win[T - min(WINDOW, T):].reshape(1, 1, -1, 2, N_KV, HD)
    shift_prompt = _rwkv_uncols(pr[T - 1]).reshape(1, 1, RWKV_COLS)

    xs1, (q_s, kv_s, win_s, gates_s, pr_s) = trunk_in(x_sample.reshape(nb * nt, D_MODEL), mod_s)
    tokens = lambda a: jnp.pad(a.reshape(nb, nt, -1), ((0, 0), (0, TOK_PAD - nt), (0, 0)))
    pr_pad = jnp.pad(pr_s.reshape(nb, nt, -1), ((0, 0), (0, RW_TOK_PAD - nt), (0, 0)))
    o_rw_s, wkv_s = _rwkv(pr_pad, _rwkv_cols(state_rwkv_shift[l])[:, None], state_rwkv_wkv[l], lw, nt)
    pool_t = jnp.transpose(cache_nsa_kv[l], (0, 2, 3, 4, 1)).reshape(-1, 4 * LANE, PAGE_SIZE)
    kc_s, vc_s = _compress(pool_t, page_table, lw, transposed=True)
    win_buf = cache_nsa_win[l]
    win_t = jnp.transpose(win_buf, (0, 2, 3, 4, 1)).reshape(nb, 2 * LANE, -1)
    o_nsa_s = _nsa_sample(pool_t, page_table, tokens(q_s), tokens(gates_s), kc_s, vc_s, tokens(kv_s),
                          win_t, tokens(win_s), rel_table, nt)
    y_sample = trunk_out(xs1, o_nsa_s[:, :nt].reshape(nb * nt, -1), o_rw_s[:, :nt].reshape(nb * nt, -1), mod_s)
    kv_sample = kv_s.reshape(1, nb, nt, 4, N_KV, HD)
    win_sample = jnp.concatenate([win_buf, win_s.reshape(nb, nt, 2, N_KV, HD)], axis=1)[None, :, nt:]
    shift_sample = _rwkv_uncols(pr_s.reshape(nb, nt, -1)[:, -1])[None]
    return (y_prompt[None], y_sample.reshape(nb, nt, D_MODEL), kv_prompt, win_prompt, shift_prompt, wkv_p[None],
            kv_sample, win_sample, shift_sample, wkv_s[None])
```
